```python
import math
import jax
import jax.numpy as jnp
from jax import lax
import numpy as np

D_MODEL = 2048
BATCH = 4
SEQ = 4096
DEPTH = 2

HEAD_DIM = 128
D_FF = 5632
PLE_DIM = 256
ROPE_THETA = 10000.0
NORM_EPS = 1e-6
Q_BLOCK = 128
GATHER_Q_BLOCK = 64

A_HEADS = 8
A_QK_DIM = 64
A_V_DIM = 2 * A_QK_DIM
B_GROUPS = ((128, 1), (512, 4), (2048, 16))
B_HEADS_PER_GROUP = 4
B_HEADS = B_HEADS_PER_GROUP * len(B_GROUPS)
C_HEADS = 8
C_Q_LORA = 512
C_KV_LORA = 256
C_NOPE = 128
C_ROPE = 64
C_V = 128
D_HEADS = 8
MOBA_BLOCK = 256
MOBA_TOPK = 3

A_Q_WIDTH = A_HEADS * 2 * A_QK_DIM
A_V_WIDTH = A_HEADS * A_V_DIM
B_WIDTH = B_HEADS * HEAD_DIM
AB_SPLITS = (A_Q_WIDTH, A_Q_WIDTH, A_V_WIDTH, B_WIDTH, B_WIDTH, B_WIDTH)
AB_IN = sum(AB_SPLITS)
AB_OUT = A_V_WIDTH + B_HEADS_PER_GROUP * HEAD_DIM
D_WIDTH = D_HEADS * HEAD_DIM
CD_SPLITS = (C_Q_LORA, C_KV_LORA, C_ROPE, D_WIDTH, D_WIDTH, D_WIDTH)
CD_IN = sum(CD_SPLITS)
CD_OUT = C_HEADS * C_V + D_WIDTH

kernel_name = 'hybrid_diff_dilated_mla_moba_block'


def rms_norm(x, gain):
    xf = x.astype(jnp.float32)
    xf = xf * lax.rsqrt(jnp.mean(xf * xf, axis=-1, keepdims=True) + NORM_EPS)
    return (xf * gain.astype(jnp.float32)).astype(x.dtype)


def rope_tables(seq, dim):
    inv = 1.0 / (ROPE_THETA ** (jnp.arange(0, dim, 2, dtype=jnp.float32) / dim))
    ang = jnp.arange(seq, dtype=jnp.float32)[:, None] * inv[None, :]
    return jnp.cos(ang), jnp.sin(ang)


def apply_rope(x, cos, sin):
    x1, x2 = jnp.split(x, 2, axis=-1)
    c = cos.astype(x.dtype)
    s = sin.astype(x.dtype)
    return jnp.concatenate([x1 * c - x2 * s, x2 * c + x1 * s], axis=-1)


def split_cols(z, sizes):
    return jnp.split(z, np.cumsum(sizes)[:-1].tolist(), axis=-1)


def swiglu(h, w_gu, w_down):
    g, u = jnp.split(h @ w_gu, 2, axis=-1)
    return (jax.nn.silu(g) * u) @ w_down


def sweep_query_blocks(block_fn, seq, block):
    out = lax.map(block_fn, jnp.arange(seq // block))
    n, b, h, t, d = out.shape
    return jnp.moveaxis(out, 0, 2).reshape(b, h, n * t, d)


def causal_attention(q, k, v, scale):
    s = q.shape[2]
    kpos = jnp.arange(s)

    def block(i):
        q0 = i * Q_BLOCK
        qb = lax.dynamic_slice_in_dim(q, q0, Q_BLOCK, axis=2)
        qpos = q0 + jnp.arange(Q_BLOCK)
        sc = jnp.einsum('bhqd,bhkd->bhqk', qb, k).astype(jnp.float32) * scale
        sc = jnp.where(kpos[None, :] <= qpos[:, None], sc, -jnp.inf)
        pr = jax.nn.softmax(sc, axis=-1).astype(v.dtype)
        return jnp.einsum('bhqk,bhkd->bhqd', pr, v)

    return sweep_query_blocks(block, s, Q_BLOCK)


def diff_attention(q, k, v, lam, scale):
    s = v.shape[2]
    kpos = jnp.arange(s)

    def block(i):
        q0 = i * Q_BLOCK
        qb = lax.dynamic_slice_in_dim(q, q0, Q_BLOCK, axis=3)
        qpos = q0 + jnp.arange(Q_BLOCK)
        sc = jnp.einsum('bhmqd,bhmkd->bhmqk', qb, k).astype(jnp.float32) * scale
        sc = jnp.where(kpos[None, :] <= qpos[:, None], sc, -jnp.inf)
        pr = jax.nn.softmax(sc, axis=-1)
        attn = (pr[:, :, 0] - lam * pr[:, :, 1]).astype(v.dtype)
        return jnp.einsum('bhqk,bhkd->bhqd', attn, v)

    return sweep_query_blocks(block, s, Q_BLOCK)


def dilated_window_attention(q, k, v):
    b, _, s, d = q.shape
    n_g = len(B_GROUPS)
    qg_all = q.reshape(b, n_g, B_HEADS_PER_GROUP, s, d)
    kg_all = k.reshape(b, n_g, B_HEADS_PER_GROUP, s, d)
    vg_all = v.reshape(b, n_g, B_HEADS_PER_GROUP, s, d)
    scale = d ** -0.5

    def block(i):
        q0 = i * GATHER_Q_BLOCK
        qpos = q0 + jnp.arange(GATHER_Q_BLOCK)
        outs, lses = [], []
        for g, (window, dil) in enumerate(B_GROUPS):
            n_keys = window // dil + 1
            idx = qpos[:, None] - dil * jnp.arange(n_keys)[None, :]
            valid = idx >= 0
            idx = jnp.maximum(idx, 0)
            qb = lax.dynamic_slice_in_dim(qg_all[:, g], q0, GATHER_Q_BLOCK, axis=2)
            kb = jnp.take(kg_all[:, g], idx, axis=2)
            vb = jnp.take(vg_all[:, g], idx, axis=2)
            sc = jnp.einsum('bhqd,bhqnd->bhqn', qb, kb).astype(jnp.float32) * scale
            sc = jnp.where(valid, sc, -jnp.inf)
            lse = jax.nn.logsumexp(sc, axis=-1)
            pr = jnp.exp(sc - lse[..., None]).astype(v.dtype)
            outs.append(jnp.einsum('bhqn,bhqnd->bhqd', pr, vb))
            lses.append(lse)
        wts = jax.nn.softmax(jnp.stack(lses), axis=0).astype(v.dtype)
        return jnp.einsum('gbhq,gbhqd->bhqd', wts, jnp.stack(outs))

    return sweep_query_blocks(block, s, GATHER_Q_BLOCK)


def moba_attention(q, k, v):
    b, h, s, d = q.shape
    scale = d ** -0.5
    n_blk = -(-s // MOBA_BLOCK)
    pad = n_blk * MOBA_BLOCK - s
    widths = ((0, 0), (0, 0), (0, pad), (0, 0))
    k_pad = jnp.pad(k, widths)
    v_pad = jnp.pad(v, widths)
    k_blocks = k_pad.reshape(b, h, n_blk, MOBA_BLOCK, d)
    v_blocks = v_pad.reshape(b, h, n_blk, MOBA_BLOCK, d)
    k_mean = jnp.mean(k_blocks.astype(jnp.float32), axis=3).astype(k.dtype)
    top_k = min(MOBA_TOPK, n_blk - 1)
    n_sel = top_k * MOBA_BLOCK
    b_idx = jnp.arange(b)[:, None, None, None]
    h_idx = jnp.arange(h)[None, :, None, None]

    def block(i):
        q0 = i * GATHER_Q_BLOCK
        qpos = q0 + jnp.arange(GATHER_Q_BLOCK)
        own = q0 // MOBA_BLOCK
        qb = lax.dynamic_slice_in_dim(q, q0, GATHER_Q_BLOCK, axis=2)
        k_own = lax.dynamic_slice_in_dim(k_pad, own * MOBA_BLOCK, MOBA_BLOCK, axis=2)
        v_own = lax.dynamic_slice_in_dim(v_pad, own * MOBA_BLOCK, MOBA_BLOCK, axis=2)
        kpos = own * MOBA_BLOCK + jnp.arange(MOBA_BLOCK)
        s_own = jnp.einsum('bhqd,bhkd->bhqk', qb, k_own).astype(jnp.float32) * scale
        s_own = jnp.where(kpos[None, :] <= qpos[:, None], s_own, -jnp.inf)
        if top_k == 0:
            pr = jax.nn.softmax(s_own, axis=-1).astype(v.dtype)
            return jnp.einsum('bhqk,bhkd->bhqd', pr, v_own)
        gate = jnp.einsum('bhqd,bhnd->bhqn', qb, k_mean).astype(jnp.float32)
        gate = jnp.where(jnp.arange(n_blk) < own, gate, -jnp.inf)
        _, sel = lax.top_k(gate, top_k)
        sel_ok = sel < own
        k_sel = k_blocks[b_idx, h_idx, sel]
        v_sel = v_blocks[b_idx, h_idx, sel].reshape(b, h, GATHER_Q_BLOCK, n_sel, d)
        s_sel = jnp.einsum('bhqd,bhqnld->bhqnl', qb, k_sel).astype(jnp.float32) * scale
        s_sel = jnp.where(sel_ok[..., None], s_sel, -jnp.inf).reshape(b, h, GATHER_Q_BLOCK, n_sel)
        pr = jax.nn.softmax(jnp.concatenate([s_sel, s_own], axis=-1), axis=-1).astype(v.dtype)
        return (jnp.einsum('bhqn,bhqnd->bhqd', pr[..., :n_sel], v_sel)
                + jnp.einsum('bhqk,bhkd->bhqd', pr[..., n_sel:], v_own))

    return sweep_query_blocks(block, s, GATHER_Q_BLOCK)


def ab_mixer(h, w_in, lam_params, subln, w_out, layer_idx, rope64, rope128):
    b, s, _ = h.shape
    qa, ka, va, qb, kb, vb = split_cols(h @ w_in, AB_SPLITS)
    qa = apply_rope(qa.reshape(b, s, A_HEADS, 2, A_QK_DIM).transpose(0, 2, 3, 1, 4), *rope64)
    ka = apply_rope(ka.reshape(b, s, A_HEADS, 2, A_QK_DIM).transpose(0, 2, 3, 1, 4), *rope64)
    va = va.reshape(b, s, A_HEADS, A_V_DIM).transpose(0, 2, 1, 3)
    lam_init = 0.8 - 0.6 * math.exp(-0.3 * layer_idx)
    lp = lam_params.astype(jnp.float32)
    lam = jnp.exp(jnp.sum(lp[0] * lp[1])) - jnp.exp(jnp.sum(lp[2] * lp[3])) + lam_init
    oa = diff_attention(qa, ka, va, lam, A_QK_DIM ** -0.5)
    oa = rms_norm(oa, subln) * (1.0 - lam_init)
    qb = apply_rope(qb.reshape(b, s, B_HEADS, HEAD_DIM).transpose(0, 2, 1, 3), *rope128)
    kb = apply_rope(kb.reshape(b, s, B_HEADS, HEAD_DIM).transpose(0, 2, 1, 3), *rope128)
    vb = vb.reshape(b, s, B_HEADS, HEAD_DIM).transpose(0, 2, 1, 3)
    ob = dilated_window_attention(qb, kb, vb)
    o = jnp.concatenate([oa, ob], axis=1).transpose(0, 2, 1, 3).reshape(b, s, AB_OUT)
    return o @ w_out


def cd_mixer(h, w_in, q_norm, w_uq, kv_norm, w_ukv, w_out, rope64, rope128):
    b, s, _ = h.shape
    c_q, c_kv, k_rope, qd, kd, vd = split_cols(h @ w_in, CD_SPLITS)
    qc = (rms_norm(c_q, q_norm) @ w_uq).reshape(b, s, C_HEADS, C_NOPE + C_ROPE).transpose(0, 2, 1, 3)
    kv = (rms_norm(c_kv, kv_norm) @ w_ukv).reshape(b, s, C_HEADS, C_NOPE + C_V).transpose(0, 2, 1, 3)
    k_rope = apply_rope(k_rope, *rope64)[:, None]
    qc = jnp.concatenate([qc[..., :C_NOPE], apply_rope(qc[..., C_NOPE:], *rope64)], axis=-1)
    kc = jnp.concatenate([kv[..., :C_NOPE], jnp.broadcast_to(k_rope, (b, C_HEADS, s, C_ROPE))], axis=-1)
    oc = causal_attention(qc, kc, kv[..., C_NOPE:], (C_NOPE + C_ROPE) ** -0.5)
    qd = apply_rope(qd.reshape(b, s, D_HEADS, HEAD_DIM).transpose(0, 2, 1, 3), *rope128)
    kd = apply_rope(kd.reshape(b, s, D_HEADS, HEAD_DIM).transpose(0, 2, 1, 3), *rope128)
    vd = vd.reshape(b, s, D_HEADS, HEAD_DIM).transpose(0, 2, 1, 3)
    od = moba_attention(qd, kd, vd)
    o = jnp.concatenate([oc, od], axis=1).transpose(0, 2, 1, 3).reshape(b, s, CD_OUT)
    return o @ w_out


def setup_inputs(seed: int = 0) -> dict:
    key = jax.random.key(seed)
    ks = jax.random.split(key, 20)
    n_even = (DEPTH + 1) // 2
    n_odd = DEPTH // 2
    f32 = jnp.float32

    def dense(k, shape):
        return jax.random.normal(k, shape, f32) * (shape[-2] ** -0.5)

    def gain(k, shape):
        return 1.0 + 0.02 * jax.random.normal(k, shape, f32)

    return {
        'x': jax.random.normal(ks[0], (BATCH, SEQ, D_MODEL), f32),
        'p': jax.random.normal(ks[1], (DEPTH, BATCH, SEQ, PLE_DIM), f32),
        'ffn_norm': gain(ks[2], (DEPTH, 2, D_MODEL)),
        'ffn_w_gu': dense(ks[3], (DEPTH, 2, D_MODEL, 2 * D_FF)),
        'ffn_w_down': dense(ks[4], (DEPTH, 2, D_FF, D_MODEL)),
        'mix_norm': gain(ks[5], (DEPTH, D_MODEL)),
        'ab_w_in': dense(ks[6], (n_even, D_MODEL, AB_IN)),
        'ab_lambda': 0.1 * jax.random.normal(ks[7], (n_even, 4, A_QK_DIM), f32),
        'ab_subln': gain(ks[8], (n_even, A_V_DIM)),
        'ab_w_out': dense(ks[9], (n_even, AB_OUT, D_MODEL)),
        'cd_w_in': dense(ks[10], (n_odd, D_MODEL, CD_IN)),
        'cd_q_norm': gain(ks[11], (n_odd, C_Q_LORA)),
        'cd_w_uq': dense(ks[12], (n_odd, C_Q_LORA, C_HEADS * (C_NOPE + C_ROPE))),
        'cd_kv_norm': gain(ks[13], (n_odd, C_KV_LORA)),
        'cd_w_ukv': dense(ks[14], (n_odd, C_KV_LORA, C_HEADS * (C_NOPE + C_V))),
        'cd_w_out': dense(ks[15], (n_odd, CD_OUT, D_MODEL)),
        'ple_norm': gain(ks[16], (DEPTH, D_MODEL)),
        'ple_w_gate': dense(ks[17], (DEPTH, D_MODEL, D_MODEL)),
        'ple_w_proj': dense(ks[18], (DEPTH, PLE_DIM, D_MODEL)),
        'final_norm': gain(ks[19], (D_MODEL,)),
    }


def reference(x, p, ffn_norm, ffn_w_gu, ffn_w_down, mix_norm, ab_w_in, ab_lambda, ab_subln, ab_w_out,
              cd_w_in, cd_q_norm, cd_w_uq, cd_kv_norm, cd_w_ukv, cd_w_out, ple_norm, ple_w_gate,
              ple_w_proj, final_norm):
    s = x.shape[1]
    rope64 = rope_tables(s, 64)
    rope128 = rope_tables(s, HEAD_DIM)
    for i in range(DEPTH):
        j = i // 2
        x = x + 0.5 * swiglu(rms_norm(x, ffn_norm[i, 0]), ffn_w_gu[i, 0], ffn_w_down[i, 0])
        h = rms_norm(x, mix_norm[i])
        if i % 2 == 0:
            x = x + ab_mixer(h, ab_w_in[j], ab_lambda[j], ab_subln[j], ab_w_out[j], i, rope64, rope128)
        else:
            x = x + cd_mixer(h, cd_w_in[j], cd_q_norm[j], cd_w_uq[j], cd_kv_norm[j], cd_w_ukv[j],
                             cd_w_out[j], rope64, rope128)
        x = x + 0.5 * swiglu(rms_norm(x, ffn_norm[i, 1]), ffn_w_gu[i, 1], ffn_w_down[i, 1])
        gate = jax.nn.sigmoid(rms_norm(x, ple_norm[i]) @ ple_w_gate[i])
        x = x + gate * (p[i] @ ple_w_proj[i])
    return rms_norm(x, final_norm)
```

```python
import functools
import math

import jax
import jax.numpy as jnp
from jax import lax
from jax.experimental import pallas as pl
from jax.experimental.pallas import tpu as pltpu

F32 = jnp.float32
BF16 = jnp.bfloat16

D_MODEL = 2048
D_FF = 5632
PLE_DIM = 256
HEAD_DIM = 128
ROPE_THETA = 10000.0
NORM_EPS = 1e-6

A_HEADS = 8
A_QK_DIM = 64
B_GROUPS = ((128, 1), (512, 4), (2048, 16))
B_HEADS_PER_GROUP = 4
B_HEADS = B_HEADS_PER_GROUP * len(B_GROUPS)
C_HEADS = 8
C_Q_LORA = 512
C_KV_LORA = 256
C_NOPE = 128
C_ROPE = 64
D_HEADS = 8
MOBA_BLOCK = 256
MOBA_TOPK = 3

LANES = 128
VMEM_LIMIT = 56 * 1024 * 1024

PLAIN, ROPE128, ROPE64 = 0, 1, 2


def _rms(x, gain):
    return x * lax.rsqrt(jnp.mean(x * x, axis=-1, keepdims=True) + NORM_EPS) * gain


def _params(*semantics):
    return pltpu.CompilerParams(dimension_semantics=semantics, vmem_limit_bytes=VMEM_LIMIT)


def _norm_proj_body(x_ref, g_ref, w_ref, c128_ref, s128_ref, c64_ref, s64_ref, o_ref, xn_ref, *,
                    tile_patterns):
    j = pl.program_id(1)

    @pl.when(j == 0)
    def _():
        xn_ref[...] = _rms(x_ref[...], g_ref[...]).astype(BF16)

    acc = jnp.dot(xn_ref[...], w_ref[...], preferred_element_type=F32)

    def epilogue(pattern):
        for c, (kind, scale) in enumerate(pattern):
            a = acc[:, c * LANES:(c + 1) * LANES]
            if scale != 1.0:
                a = a * scale
            if kind == ROPE128:
                a = a * c128_ref[...] + pltpu.roll(a, 64, 1) * s128_ref[...]
            elif kind == ROPE64:
                a = a * c64_ref[...] + pltpu.roll(a, 64, 1) * s64_ref[...]
            o_ref[:, c * LANES:(c + 1) * LANES] = a.astype(o_ref.dtype)

    distinct = sorted(set(tile_patterns), key=tile_patterns.index)
    if len(distinct) == 1:
        epilogue(distinct[0])
    else:
        for pattern in distinct:
            tiles = [t for t, p in enumerate(tile_patterns) if p == pattern]
            cond = functools.reduce(jnp.logical_or, [j == t for t in tiles])
            pl.when(cond)(functools.partial(epilogue, pattern))


def _norm_proj(x, gain, w, rope, chunk_modes, *, seq, tm, tn, out_dtype):
    t, k = x.shape
    n = w.shape[1]
    per_tile = tn // LANES
    assert t % tm == 0 and n % tn == 0 and seq % tm == 0 and len(chunk_modes) * LANES == n
    tile_patterns = tuple(tuple(chunk_modes[a * per_tile:(a + 1) * per_tile]) for a in range(n // tn))
    seq_tiles = seq // tm
    tab = pl.BlockSpec((tm, LANES), lambda i, j: (i % seq_tiles, 0))
    return pl.pallas_call(
        functools.partial(_norm_proj_body, tile_patterns=tile_patterns),
        grid=(t // tm, n // tn),
        in_specs=[pl.BlockSpec((tm, k), lambda i, j: (i, 0)),
                  pl.BlockSpec((1, k), lambda i, j: (0, 0)),
                  pl.BlockSpec((k, tn), lambda i, j: (0, j)),
                  tab, tab, tab, tab],
        out_specs=pl.BlockSpec((tm, tn), lambda i, j: (i, j)),
        out_shape=jax.ShapeDtypeStruct((t, n), out_dtype),
        scratch_shapes=[pltpu.VMEM((tm, k), BF16)],
        compiler_params=_params("parallel", "arbitrary"),
        name="norm_proj",
    )(x, gain.reshape(1, k), w, *rope)


def _out_proj_body(a_ref, w_ref, res_ref, o_ref):
    o_ref[...] = res_ref[...] + jnp.dot(a_ref[...], w_ref[...], preferred_element_type=F32)


def _out_proj(a, w, res, *, tm, tn):
    t, k = a.shape
    n = w.shape[1]
    assert t % tm == 0 and n % tn == 0
    return pl.pallas_call(
        _out_proj_body,
        grid=(t // tm, n // tn),
        in_specs=[pl.BlockSpec((tm, k), lambda i, j: (i, 0)),
                  pl.BlockSpec((k, tn), lambda i, j: (0, j)),
                  pl.BlockSpec((tm, tn), lambda i, j: (i, j))],
        out_specs=pl.BlockSpec((tm, tn), lambda i, j: (i, j)),
        out_shape=jax.ShapeDtypeStruct((t, n), F32),
        compiler_params=_params("parallel", "parallel"),
        name="out_proj",
    )(a, w, res)


def _ffn_body(x_ref, g_ref, wg_ref, wu_ref, wd_ref, o_ref, xn_ref, acc_ref):
    j = pl.program_id(1)

    @pl.when(j == 0)
    def _():
        xn_ref[...] = _rms(x_ref[...], g_ref[...]).astype(BF16)
        acc_ref[...] = jnp.zeros_like(acc_ref)

    xn = xn_ref[...]
    g = jnp.dot(xn, wg_ref[...], preferred_element_type=F32)
    u = jnp.dot(xn, wu_ref[...], preferred_element_type=F32)
    h = (g * jax.nn.sigmoid(g) * u).astype(BF16)
    acc_ref[...] += jnp.dot(h, wd_ref[...], preferred_element_type=F32)

    @pl.when(j == pl.num_programs(1) - 1)
    def _():
        o_ref[...] = x_ref[...] + 0.5 * acc_ref[...]


def _ffn(x, gain, w_gu, w_down, *, tm, tf):
    t, d = x.shape
    d_ff = w_down.shape[0]
    assert t % tm == 0 and d_ff % tf == 0
    nf = d_ff // tf
    return pl.pallas_call(
        _ffn_body,
        grid=(t // tm, nf),
        in_specs=[pl.BlockSpec((tm, d), lambda i, j: (i, 0)),
                  pl.BlockSpec((1, d), lambda i, j: (0, 0)),
                  pl.BlockSpec((d, tf), lambda i, j: (0, j)),
                  pl.BlockSpec((d, tf), lambda i, j: (0, nf + j)),
                  pl.BlockSpec((tf, d), lambda i, j: (j, 0))],
        out_specs=pl.BlockSpec((tm, d), lambda i, j: (i, 0)),
        out_shape=jax.ShapeDtypeStruct((t, d), F32),
        scratch_shapes=[pltpu.VMEM((tm, d), BF16), pltpu.VMEM((tm, d), F32)],
        compiler_params=_params("parallel", "arbitrary"),
        name="ffn",
    )(x, gain.reshape(1, d), w_gu, w_gu, w_down)


def _ple_body(x_ref, p_ref, g_ref, wg_ref, wp_ref, fg_ref, o_ref, *, final_norm):
    x = x_ref[...]
    xn = _rms(x, g_ref[...]).astype(BF16)
    gate = jax.nn.sigmoid(jnp.dot(xn, wg_ref[...], preferred_element_type=F32))
    proj = jnp.dot(p_ref[...].astype(BF16), wp_ref[...], preferred_element_type=F32)
    y = x + gate * proj
    if final_norm:
        y = _rms(y, fg_ref[...])
    o_ref[...] = y


def _ple(x, p, gain, w_gate, w_proj, final_gain, *, tm, final_norm):
    t, d = x.shape
    pd = p.shape[1]
    assert t % tm == 0
    const = lambda i: (0, 0)
    return pl.pallas_call(
        functools.partial(_ple_body, final_norm=final_norm),
        grid=(t // tm,),
        in_specs=[pl.BlockSpec((tm, d), lambda i: (i, 0)),
                  pl.BlockSpec((tm, pd), lambda i: (i, 0)),
                  pl.BlockSpec((1, d), const),
                  pl.BlockSpec((d, d), const),
                  pl.BlockSpec((pd, d), const),
                  pl.BlockSpec((1, d), const)],
        out_specs=pl.BlockSpec((tm, d), lambda i: (i, 0)),
        out_shape=jax.ShapeDtypeStruct((t, d), F32),
        compiler_params=_params("parallel"),
        name="ple",
    )(x, p, gain.reshape(1, d), w_gate, w_proj, final_gain.reshape(1, d))


def _dot_nt(a, b):
    return lax.dot_general(a, b, (((1,), (1,)), ((), ())), preferred_element_type=F32)


def _softmax_update(carry, s, v):
    m, l, acc = carry
    m_new = jnp.maximum(m, jnp.max(s, axis=-1, keepdims=True))
    alpha = jnp.exp(m - m_new)
    p = jnp.exp(s - m_new)
    l = alpha * l + jnp.sum(p, axis=-1, keepdims=True)
    acc = alpha * acc + jnp.dot(p.astype(BF16), v, preferred_element_type=F32)
    return m_new, l, acc


def _softmax_init(rows, dv):
    return (jnp.full((rows, 1), -jnp.inf, F32), jnp.zeros((rows, 1), F32), jnp.zeros((rows, dv), F32))


def _causal_flash(score_fn, v_ref, rows, row_pos, qi, tk):
    def full_block(j, carry):
        start = pl.multiple_of(j * tk, tk)
        return _softmax_update(carry, score_fn(start), v_ref[pl.ds(start, tk), :])

    carry = lax.fori_loop(0, qi, full_block, _softmax_init(rows, v_ref.shape[-1]))
    start = pl.multiple_of(qi * tk, tk)
    kpos = start + lax.broadcasted_iota(jnp.int32, (1, tk), 1)
    s = jnp.where(kpos <= row_pos, score_fn(start), -jnp.inf)
    return _softmax_update(carry, s, v_ref[pl.ds(start, tk), :])


def _diff_attn_body(lam_ref, subln_ref, q_ref, k_ref, v_ref, o_ref, *, tq, lam_init):
    qi = pl.program_id(2)
    q = q_ref[...]
    lane = lax.broadcasted_iota(jnp.int32, (1, LANES), 1)
    is_map0 = (lane % 64) < 32
    zero = jnp.zeros_like(q)
    qq = jnp.concatenate([jnp.where(is_map0, q, zero), jnp.where(is_map0, zero, q)], axis=0)
    pos = qi * tq + lax.broadcasted_iota(jnp.int32, (tq, 1), 0)
    row_pos = jnp.concatenate([pos, pos], axis=0)

    score_fn = lambda start: _dot_nt(qq, k_ref[pl.ds(start, tq), :])
    _, l, acc = _causal_flash(score_fn, v_ref, 2 * tq, row_pos, qi, tq)

    lp = lam_ref[...]
    lam = (jnp.exp(jnp.sum(lp[0:1] * lp[1:2], axis=-1, keepdims=True))
           - jnp.exp(jnp.sum(lp[2:3] * lp[3:4], axis=-1, keepdims=True)) + lam_init)
    o = acc / l
    o = o[:tq] - lam * o[tq:]
    o_ref[...] = (_rms(o, subln_ref[...]) * (1.0 - lam_init)).astype(o_ref.dtype)


def _diff_attention(qkv, lam_params, subln, *, batch, seq, tq, q_col, k_col, v_col, lam_init):
    nq = seq // tq
    return pl.pallas_call(
        functools.partial(_diff_attn_body, tq=tq, lam_init=lam_init),
        grid=(batch, A_HEADS, nq),
        in_specs=[pl.BlockSpec((4, A_QK_DIM), lambda b, h, i: (0, 0)),
                  pl.BlockSpec((1, LANES), lambda b, h, i: (0, 0)),
                  pl.BlockSpec((tq, LANES), lambda b, h, i: (b * nq + i, q_col + h)),
                  pl.BlockSpec((seq, LANES), lambda b, h, i: (b, k_col + h)),
                  pl.BlockSpec((seq, LANES), lambda b, h, i: (b, v_col + h))],
        out_specs=pl.BlockSpec((tq, LANES), lambda b, h, i: (b * nq + i, h)),
        out_shape=jax.ShapeDtypeStruct((batch * seq, A_HEADS * LANES), BF16),
        compiler_params=_params("parallel", "parallel", "parallel"),
        name="diff_attn",
    )(lam_params, subln.reshape(1, LANES), qkv, qkv, qkv)


def _dilated_body(q_ref, k_ref, v_ref, o_ref, lse_ref, *, tile):
    length = q_ref.shape[0]
    row = lax.broadcasted_iota(jnp.int32, (tile, 1), 0)

    def attend(q, k, v, valid, out_start):
        s = jnp.where(valid, _dot_nt(q, k), -jnp.inf)
        m = jnp.max(s, axis=-1, keepdims=True)
        p = jnp.exp(s - m)
        l = jnp.sum(p, axis=-1, keepdims=True)
        o = jnp.dot(p.astype(BF16), v, preferred_element_type=F32) / l
        o_ref[pl.ds(out_start, tile), :] = o
        lse_ref[pl.ds(out_start, tile), :] = jnp.broadcast_to(m + jnp.log(l), (tile, LANES))

    col = lax.broadcasted_iota(jnp.int32, (1, tile), 1)
    attend(q_ref[0:tile, :], k_ref[0:tile, :], v_ref[0:tile, :], col <= row, 0)

    col2 = lax.broadcasted_iota(jnp.int32, (1, 2 * tile), 1)
    band = jnp.logical_and(col2 >= row, col2 <= row + tile)

    def body(i, _):
        q_start = pl.multiple_of(i * tile, tile)
        kv_start = pl.multiple_of((i - 1) * tile, tile)
        attend(q_ref[pl.ds(q_start, tile), :], k_ref[pl.ds(kv_start, 2 * tile), :],
               v_ref[pl.ds(kv_start, 2 * tile), :], band, q_start)
        return 0

    lax.fori_loop(1, length // tile, body, 0)


def _dilated_group(qkv, *, batch, seq, group, dil, q_col, k_col, v_col):
    n_cols = qkv.shape[1] // LANES
    length = seq // dil
    view = qkv.reshape(batch, length, dil * qkv.shape[1])
    hpg = B_HEADS_PER_GROUP
    tile = B_GROUPS[group][0] // dil
    assert tile == LANES and length % tile == 0

    def in_spec(col):
        return pl.BlockSpec((None, length, LANES),
                            lambda b, h, r: (b, 0, r * n_cols + col + group * hpg + h))

    out_spec = pl.BlockSpec((None, length, LANES), lambda b, h, r: (b, 0, r * hpg + h))
    out_sds = jax.ShapeDtypeStruct((batch, length, dil * hpg * LANES), F32)
    o, lse = pl.pallas_call(
        functools.partial(_dilated_body, tile=tile),
        grid=(batch, hpg, dil),
        in_specs=[in_spec(q_col), in_spec(k_col), in_spec(v_col)],
        out_specs=[out_spec, out_spec],
        out_shape=[out_sds, out_sds],
        compiler_params=_params("parallel", "parallel", "parallel"),
        name=f"dilated_g{group}",
    )(view, view, view)
    width = hpg * LANES
    return o.reshape(batch * seq, width), lse.reshape(batch * seq, width)


def _merge_body(o0_ref, o1_ref, o2_ref, l0_ref, l1_ref, l2_ref, out_ref):
    l0, l1, l2 = l0_ref[...], l1_ref[...], l2_ref[...]
    m = jnp.maximum(jnp.maximum(l0, l1), l2)
    w0, w1, w2 = jnp.exp(l0 - m), jnp.exp(l1 - m), jnp.exp(l2 - m)
    out = (w0 * o0_ref[...] + w1 * o1_ref[...] + w2 * o2_ref[...]) / (w0 + w1 + w2)
    out_ref[...] = out.astype(out_ref.dtype)


def _merge_groups(outs, lses, *, tm):
    t, width = outs[0].shape
    spec = pl.BlockSpec((tm, width), lambda i: (i, 0))
    return pl.pallas_call(
        _merge_body,
        grid=(t // tm,),
        in_specs=[spec] * 6,
        out_specs=spec,
        out_shape=jax.ShapeDtypeStruct((t, width), BF16),
        compiler_params=_params("parallel"),
        name="dilated_merge",
    )(*outs, *lses)


def _mla_body(q_ref, kn_ref, kr_ref, v_ref, o_ref, *, tq):
    qi = pl.program_id(2)
    q = q_ref[...]
    qn, qr = q[:, :LANES], q[:, LANES:]
    row_pos = qi * tq + lax.broadcasted_iota(jnp.int32, (tq, 1), 0)
    score_fn = lambda start: (_dot_nt(qn, kn_ref[pl.ds(start, tq), :])
                              + _dot_nt(qr, kr_ref[pl.ds(start, tq), :]))
    _, l, acc = _causal_flash(score_fn, v_ref, tq, row_pos, qi, tq)
    o_ref[...] = (acc / l).astype(o_ref.dtype)


def _mla_attention(q, kv, k_rope_src, *, batch, seq, tq, k_rope_col):
    nq = seq // tq
    return pl.pallas_call(
        functools.partial(_mla_body, tq=tq),
        grid=(batch, C_HEADS, nq),
        in_specs=[pl.BlockSpec((tq, 2 * LANES), lambda b, h, i: (b * nq + i, h)),
                  pl.BlockSpec((seq, LANES), lambda b, h, i: (b, 2 * h)),
                  pl.BlockSpec((seq, LANES), lambda b, h, i: (b, k_rope_col)),
                  pl.BlockSpec((seq, LANES), lambda b, h, i: (b, 2 * h + 1))],
        out_specs=pl.BlockSpec((tq, LANES), lambda b, h, i: (b * nq + i, h)),
        out_shape=jax.ShapeDtypeStruct((batch * seq, C_HEADS * LANES), BF16),
        compiler_params=_params("parallel", "parallel", "parallel"),
        name="mla_attn",
    )(q, kv, k_rope_src, kv)


def _moba_body(q_ref, k_ref, v_ref, o_ref, kmean_ref, *, n_blk):
    qi = pl.program_id(2)
    blk = MOBA_BLOCK

    @pl.when(qi == 0)
    def _():
        k_all = k_ref[...].astype(F32).reshape(n_blk, blk, LANES)
        kmean_ref[...] = jnp.mean(k_all, axis=1)

    q = q_ref[...]
    km = kmean_ref[...]
    km_hi = km.astype(BF16)
    rem = km - km_hi.astype(F32)
    km_mid = rem.astype(BF16)
    km_lo = (rem - km_mid.astype(F32)).astype(BF16)
    gate = _dot_nt(q, km_hi) + _dot_nt(q, km_mid) + _dot_nt(q, km_lo)

    blk_id = lax.broadcasted_iota(jnp.int32, (1, n_blk), 1)
    past = blk_id < qi
    gate = jnp.where(past, gate, -jnp.inf)
    selected = jnp.zeros(gate.shape, F32)
    for _ in range(min(MOBA_TOPK, n_blk - 1)):
        best = jnp.max(gate, axis=-1, keepdims=True)
        first = jnp.min(jnp.where(gate == best, blk_id, n_blk), axis=-1, keepdims=True)
        pick = blk_id == first
        selected = jnp.where(jnp.logical_and(pick, past), 1.0, selected)
        gate = jnp.where(pick, -jnp.inf, gate)

    row = lax.broadcasted_iota(jnp.int32, (blk, 1), 0)
    col = lax.broadcasted_iota(jnp.int32, (1, blk), 1)
    own = pl.multiple_of(qi * blk, blk)
    s_own = jnp.where(col <= row, _dot_nt(q, k_ref[pl.ds(own, blk), :]), -jnp.inf)
    carry = _softmax_update(_softmax_init(blk, LANES), s_own, v_ref[pl.ds(own, blk), :])

    def past_block(n, carry):
        start = pl.multiple_of(n * blk, blk)
        chosen = jnp.sum(jnp.where(blk_id == n, selected, 0.0), axis=-1, keepdims=True) > 0.0
        s = jnp.where(chosen, _dot_nt(q, k_ref[pl.ds(start, blk), :]), -jnp.inf)
        return _softmax_update(carry, s, v_ref[pl.ds(start, blk), :])

    _, l, acc = lax.fori_loop(0, qi, past_block, carry)
    o_ref[...] = (acc / l).astype(o_ref.dtype)


def _moba_attention(qkv, *, batch, seq, q_col, k_col, v_col):
    blk = MOBA_BLOCK
    assert seq % blk == 0
    n_blk = seq // blk
    return pl.pallas_call(
        functools.partial(_moba_body, n_blk=n_blk),
        grid=(batch, D_HEADS, n_blk),
        in_specs=[pl.BlockSpec((blk, LANES), lambda b, h, i: (b * n_blk + i, q_col + h)),
                  pl.BlockSpec((seq, LANES), lambda b, h, i: (b, k_col + h)),
                  pl.BlockSpec((seq, LANES), lambda b, h, i: (b, v_col + h))],
        out_specs=pl.BlockSpec((blk, LANES), lambda b, h, i: (b * n_blk + i, h)),
        out_shape=jax.ShapeDtypeStruct((batch * seq, D_HEADS * LANES), BF16),
        scratch_shapes=[pltpu.VMEM((n_blk, LANES), F32)],
        compiler_params=_params("parallel", "parallel", "arbitrary"),
        name="moba_attn",
    )(qkv, qkv, qkv)


def _rope_tables(seq):
    pos = jnp.arange(seq, dtype=F32)[:, None]

    def angles(dim):
        inv = 1.0 / (ROPE_THETA ** (jnp.arange(0, dim, 2, dtype=F32) / dim))
        return pos * inv[None, :]

    a128 = angles(HEAD_DIM)
    a64 = angles(A_QK_DIM)
    c128 = jnp.concatenate([jnp.cos(a128)] * 2, axis=1)
    s128 = jnp.concatenate([-jnp.sin(a128), jnp.sin(a128)], axis=1)
    c64 = jnp.concatenate([jnp.cos(a64)] * 4, axis=1)
    s64 = jnp.concatenate([-jnp.sin(a64)] * 2 + [jnp.sin(a64)] * 2, axis=1)
    return c128, s128, c64, s64


def _interleave_diff_heads(w):
    d = w.shape[0]
    return w.reshape(d, A_HEADS, 2, 2, 32).transpose(0, 1, 3, 2, 4).reshape(d, A_HEADS * LANES)


def _spread_rope64(w):
    z = jnp.zeros(w.shape[:-1] + (32,), w.dtype)
    return jnp.concatenate([w[..., :32], z, w[..., 32:], z], axis=-1)


def _ab_mixer(x, gain, w_in, lam_params, subln, w_out, layer_idx, rope, *, batch, seq):
    aw = A_HEADS * LANES
    w = jnp.concatenate([_interleave_diff_heads(w_in[:, :aw]), _interleave_diff_heads(w_in[:, aw:2 * aw]),
                         w_in[:, 2 * aw:]], axis=1).astype(BF16)
    a_scale = A_QK_DIM ** -0.5
    b_scale = HEAD_DIM ** -0.5
    modes = ([(ROPE64, a_scale)] * A_HEADS + [(ROPE64, 1.0)] * A_HEADS + [(PLAIN, 1.0)] * A_HEADS
             + [(ROPE128, b_scale)] * B_HEADS + [(ROPE128, 1.0)] * B_HEADS + [(PLAIN, 1.0)] * B_HEADS)
    qkv = _norm_proj(x, gain, w, rope, modes, seq=seq, tm=512, tn=512, out_dtype=BF16)
    lam_init = 0.8 - 0.6 * math.exp(-0.3 * layer_idx)
    oa = _diff_attention(qkv, lam_params, subln, batch=batch, seq=seq, tq=256,
                         q_col=0, k_col=A_HEADS, v_col=2 * A_HEADS, lam_init=lam_init)
    b0 = 3 * A_HEADS
    outs, lses = [], []
    for g, (_, dil) in enumerate(B_GROUPS):
        o, lse = _dilated_group(qkv, batch=batch, seq=seq, group=g, dil=dil,
                                q_col=b0, k_col=b0 + B_HEADS, v_col=b0 + 2 * B_HEADS)
        outs.append(o)
        lses.append(lse)
    ob = _merge_groups(outs, lses, tm=512)
    o = jnp.concatenate([oa, ob], axis=1)
    return _out_proj(o, w_out.astype(BF16), x, tm=512, tn=1024)


def _cd_mixer(x, gain, w_in, q_norm, w_uq, kv_norm, w_ukv, w_out, rope, *, batch, seq):
    lat = C_Q_LORA + C_KV_LORA
    d_scale = HEAD_DIM ** -0.5
    latent = _norm_proj(x, gain, w_in[:, :lat].astype(BF16), rope, [(PLAIN, 1.0)] * (lat // LANES),
                        seq=seq, tm=512, tn=lat // 2, out_dtype=F32)
    w_main = jnp.concatenate([_spread_rope64(w_in[:, lat:lat + C_ROPE]), w_in[:, lat + C_ROPE:]],
                             axis=1).astype(BF16)
    modes = ([(ROPE64, 1.0)] + [(ROPE128, d_scale)] * D_HEADS + [(ROPE128, 1.0)] * D_HEADS
             + [(PLAIN, 1.0)] * D_HEADS)
    main = _norm_proj(x, gain, w_main, rope, modes, seq=seq, tm=512, tn=5 * LANES, out_dtype=BF16)

    c_scale = (C_NOPE + C_ROPE) ** -0.5
    w_q = w_uq.reshape(C_Q_LORA, C_HEADS, C_NOPE + C_ROPE)
    w_q = jnp.concatenate([w_q[..., :C_NOPE], _spread_rope64(w_q[..., C_NOPE:])], axis=-1)
    w_q = w_q.reshape(C_Q_LORA, C_HEADS * 2 * LANES).astype(BF16)
    q_modes = [(PLAIN, c_scale), (ROPE64, c_scale)] * C_HEADS
    qc = _norm_proj(latent[:, :C_Q_LORA], q_norm, w_q, rope, q_modes, seq=seq, tm=512, tn=512, out_dtype=BF16)
    kv = _norm_proj(latent[:, C_Q_LORA:], kv_norm, w_ukv.astype(BF16), rope,
                    [(PLAIN, 1.0)] * (2 * C_HEADS), seq=seq, tm=512, tn=512, out_dtype=BF16)
    oc = _mla_attention(qc, kv, main, batch=batch, seq=seq, tq=256, k_rope_col=0)
    od = _moba_attention(main, batch=batch, seq=seq, q_col=1, k_col=1 + D_HEADS, v_col=1 + 2 * D_HEADS)
    o = jnp.concatenate([oc, od], axis=1)
    return _out_proj(o, w_out.astype(BF16), x, tm=512, tn=1024)


def kernel(x, p, ffn_norm, ffn_w_gu, ffn_w_down, mix_norm, ab_w_in, ab_lambda, ab_subln, ab_w_out,
           cd_w_in, cd_q_norm, cd_w_uq, cd_kv_norm, cd_w_ukv, cd_w_out, ple_norm, ple_w_gate,
           ple_w_proj, final_norm):
    batch, seq, d = x.shape
    depth = p.shape[0]
    rope = _rope_tables(seq)
    x = x.reshape(batch * seq, d)
    for i in range(depth):
        j = i // 2
        x = _ffn(x, ffn_norm[i, 0], ffn_w_gu[i, 0].astype(BF16), ffn_w_down[i, 0].astype(BF16), tm=512, tf=512)
        if i % 2 == 0:
            x = _ab_mixer(x, mix_norm[i], ab_w_in[j], ab_lambda[j], ab_subln[j], ab_w_out[j], i, rope,
                          batch=batch, seq=seq)
        else:
            x = _cd_mixer(x, mix_norm[i], cd_w_in[j], cd_q_norm[j], cd_w_uq[j], cd_kv_norm[j], cd_w_ukv[j],
                          cd_w_out[j], rope, batch=batch, seq=seq)
        x = _ffn(x, ffn_norm[i, 1], ffn_w_gu[i, 1].astype(BF16), ffn_w_down[i, 1].astype(BF16), tm=512, tf=512)
        x = _ple(x, p[i].reshape(batch * seq, -1), ple_norm[i], ple_w_gate[i].astype(BF16),
                 ple_w_proj[i].astype(BF16), final_norm, tm=256, final_norm=(i == depth - 1))
    return x.reshape(batch, seq, d)
```

```python
import functools
import math

import jax
import jax.numpy as jnp
from jax import lax
from jax.experimental import pallas as pl
from jax.experimental.pallas import tpu as pltpu

F32 = jnp.float32
BF16 = jnp.bfloat16

D_MODEL = 2048
D_FF = 5632
PLE_DIM = 256
HEAD_DIM = 128
ROPE_THETA = 10000.0
NORM_EPS = 1e-6

A_HEADS = 8
A_QK_DIM = 64
B_GROUPS = ((128, 1), (512, 4), (2048, 16))
B_HEADS_PER_GROUP = 4
B_HEADS = B_HEADS_PER_GROUP * len(B_GROUPS)
C_HEADS = 8
C_Q_LORA = 512
C_KV_LORA = 256
C_NOPE = 128
C_ROPE = 64
D_HEADS = 8
MOBA_BLOCK = 256
MOBA_TOPK = 3

LANES = 128
VMEM_LIMIT = 56 * 1024 * 1024

PLAIN, ROPE128, ROPE64 = 0, 1, 2


def _rms(x, gain):
    return x * lax.rsqrt(jnp.mean(x * x, axis=-1, keepdims=True) + NORM_EPS) * gain


def _params(*semantics):
    return pltpu.CompilerParams(dimension_semantics=semantics, vmem_limit_bytes=VMEM_LIMIT)


def _norm_proj_body(x_ref, g_ref, w_ref, c128_ref, s128_ref, c64_ref, s64_ref, o_ref, xn_ref, *,
                    tile_patterns):
    j = pl.program_id(1)

    @pl.when(j == 0)
    def _():
        xn_ref[...] = _rms(x_ref[...], g_ref[...]).astype(BF16)

    acc = jnp.dot(xn_ref[...], w_ref[...], preferred_element_type=F32)

    def epilogue(pattern):
        for c, (kind, scale) in enumerate(pattern):
            a = acc[:, c * LANES:(c + 1) * LANES]
            if scale != 1.0:
                a = a * scale
            if kind == ROPE128:
                a = a * c128_ref[...] + pltpu.roll(a, 64, 1) * s128_ref[...]
            elif kind == ROPE64:
                a = a * c64_ref[...] + pltpu.roll(a, 64, 1) * s64_ref[...]
            o_ref[:, c * LANES:(c + 1) * LANES] = a.astype(o_ref.dtype)

    distinct = sorted(set(tile_patterns), key=tile_patterns.index)
    if len(distinct) == 1:
        epilogue(distinct[0])
    else:
        for pattern in distinct:
            tiles = [t for t, p in enumerate(tile_patterns) if p == pattern]
            cond = functools.reduce(jnp.logical_or, [j == t for t in tiles])
            pl.when(cond)(functools.partial(epilogue, pattern))


def _norm_proj(x, gain, w, rope, chunk_modes, *, seq, tm, tn, out_dtype):
    t, k = x.shape
    n = w.shape[1]
    per_tile = tn // LANES
    assert t % tm == 0 and n % tn == 0 and seq % tm == 0 and len(chunk_modes) * LANES == n
    tile_patterns = tuple(tuple(chunk_modes[a * per_tile:(a + 1) * per_tile]) for a in range(n // tn))
    seq_tiles = seq // tm
    tab = pl.BlockSpec((tm, LANES), lambda i, j: (i % seq_tiles, 0))
    return pl.pallas_call(
        functools.partial(_norm_proj_body, tile_patterns=tile_patterns),
        grid=(t // tm, n // tn),
        in_specs=[pl.BlockSpec((tm, k), lambda i, j: (i, 0)),
                  pl.BlockSpec((1, k), lambda i, j: (0, 0)),
                  pl.BlockSpec((k, tn), lambda i, j: (0, j)),
                  tab, tab, tab, tab],
        out_specs=pl.BlockSpec((tm, tn), lambda i, j: (i, j)),
        out_shape=jax.ShapeDtypeStruct((t, n), out_dtype),
        scratch_shapes=[pltpu.VMEM((tm, k), BF16)],
        compiler_params=_params("parallel", "arbitrary"),
        name="norm_proj",
    )(x, gain.reshape(1, k), w, *rope)


def _out_proj_body(a_ref, w_ref, res_ref, o_ref):
    o_ref[...] = res_ref[...] + jnp.dot(a_ref[...], w_ref[...], preferred_element_type=F32)


def _out_proj(a, w, res, *, tm, tn):
    t, k = a.shape
    n = w.shape[1]
    assert t % tm == 0 and n % tn == 0
    return pl.pallas_call(
        _out_proj_body,
        grid=(t // tm, n // tn),
        in_specs=[pl.BlockSpec((tm, k), lambda i, j: (i, 0)),
                  pl.BlockSpec((k, tn), lambda i, j: (0, j)),
                  pl.BlockSpec((tm, tn), lambda i, j: (i, j))],
        out_specs=pl.BlockSpec((tm, tn), lambda i, j: (i, j)),
        out_shape=jax.ShapeDtypeStruct((t, n), F32),
        compiler_params=_params("parallel", "parallel"),
        name="out_proj",
    )(a, w, res)


def _ffn_body(x_ref, g_ref, wg_ref, wu_ref, wd_ref, o_ref, xn_ref, acc_ref):
    j = pl.program_id(1)

    @pl.when(j == 0)
    def _():
        xn_ref[...] = _rms(x_ref[...], g_ref[...]).astype(BF16)
        acc_ref[...] = jnp.zeros_like(acc_ref)

    xn = xn_ref[...]
    g = jnp.dot(xn, wg_ref[...], preferred_element_type=F32)
    u = jnp.dot(xn, wu_ref[...], preferred_element_type=F32)
    h = (g * jax.nn.sigmoid(g) * u).astype(BF16)
    acc_ref[...] += jnp.dot(h, wd_ref[...], preferred_element_type=F32)

    @pl.when(j == pl.num_programs(1) - 1)
    def _():
        o_ref[...] = x_ref[...] + 0.5 * acc_ref[...]


def _ffn(x, gain, w_gu, w_down, *, tm, tf):
    t, d = x.shape
    d_ff = w_down.shape[0]
    assert t % tm == 0 and d_ff % tf == 0
    nf = d_ff // tf
    return pl.pallas_call(
        _ffn_body,
        grid=(t // tm, nf),
        in_specs=[pl.BlockSpec((tm, d), lambda i, j: (i, 0)),
                  pl.BlockSpec((1, d), lambda i, j: (0, 0)),
                  pl.BlockSpec((d, tf), lambda i, j: (0, j)),
                  pl.BlockSpec((d, tf), lambda i, j: (0, nf + j)),
                  pl.BlockSpec((tf, d), lambda i, j: (j, 0))],
        out_specs=pl.BlockSpec((tm, d), lambda i, j: (i, 0)),
        out_shape=jax.ShapeDtypeStruct((t, d), F32),
        scratch_shapes=[pltpu.VMEM((tm, d), BF16), pltpu.VMEM((tm, d), F32)],
        compiler_params=_params("parallel", "arbitrary"),
        name="ffn",
    )(x, gain.reshape(1, d), w_gu, w_gu, w_down)


def _ple_body(x_ref, p_ref, g_ref, wg_ref, wp_ref, fg_ref, o_ref, *, final_norm):
    x = x_ref[...]
    xn = _rms(x, g_ref[...]).astype(BF16)
    gate = jax.nn.sigmoid(jnp.dot(xn, wg_ref[...], preferred_element_type=F32))
    proj = jnp.dot(p_ref[...].astype(BF16), wp_ref[...], preferred_element_type=F32)
    y = x + gate * proj
    if final_norm:
        y = _rms(y, fg_ref[...])
    o_ref[...] = y


def _ple(x, p, gain, w_gate, w_proj, final_gain, *, tm, final_norm):
    t, d = x.shape
    pd = p.shape[1]
    assert t % tm == 0
    const = lambda i: (0, 0)
    return pl.pallas_call(
        functools.partial(_ple_body, final_norm=final_norm),
        grid=(t // tm,),
        in_specs=[pl.BlockSpec((tm, d), lambda i: (i, 0)),
                  pl.BlockSpec((tm, pd), lambda i: (i, 0)),
                  pl.BlockSpec((1, d), const),
                  pl.BlockSpec((d, d), const),
                  pl.BlockSpec((pd, d), const),
                  pl.BlockSpec((1, d), const)],
        out_specs=pl.BlockSpec((tm, d), lambda i: (i, 0)),
        out_shape=jax.ShapeDtypeStruct((t, d), F32),
        compiler_params=_params("parallel"),
        name="ple",
    )(x, p, gain.reshape(1, d), w_gate, w_proj, final_gain.reshape(1, d))


def _dot_nt(a, b):
    return lax.dot_general(a, b, (((1,), (1,)), ((), ())), preferred_element_type=F32)


def _lanes(c, width=LANES):
    return slice(c * width, (c + 1) * width)


def _softmax_update_t(carry, s, v_t):
    m, l, acc = carry
    m_new = jnp.maximum(m, jnp.max(s, axis=0, keepdims=True))
    alpha = jnp.exp(m - m_new)
    p = jnp.exp(s - m_new)
    l = alpha * l + jnp.sum(p, axis=0, keepdims=True)
    acc = alpha * acc + jnp.dot(v_t, p.astype(BF16), preferred_element_type=F32)
    return m_new, l, acc


def _softmax_init_t(cols, dv):
    return (jnp.full((1, cols), jnp.finfo(F32).min, F32), jnp.zeros((1, cols), F32),
            jnp.zeros((dv, cols), F32))


def _fill_transposed(vt_ref, v_ref, lane_offsets, chunk):
    dv = vt_ref.shape[1]

    def body(j, _):
        start = pl.multiple_of(j * chunk, chunk)
        for c, off in enumerate(lane_offsets):
            vt_ref[c, :, pl.ds(start, chunk)] = v_ref[pl.ds(start, chunk), off:off + dv].T
        return 0

    lax.fori_loop(0, v_ref.shape[0] // chunk, body, 0)


def _causal_flash_t(score_fns, vt_ref, cols, qi, tk, past_mask_fns=None):
    heads = range(len(score_fns))

    def scores(block):
        start = pl.multiple_of(block * tk, tk)
        return tuple(score_fns[c](start) for c in heads)

    def update(states, s, block):
        start = pl.multiple_of(block * tk, tk)
        return tuple(_softmax_update_t(states[c], s[c], vt_ref[c, :, pl.ds(start, tk)]) for c in heads)

    def past_block(j, carry):
        states, s = carry
        s_next = scores(j + 1)
        if past_mask_fns is not None:
            s = tuple(jnp.where(past_mask_fns[c](j), s[c], -jnp.inf) for c in heads)
        return update(states, s, j), s_next

    init = tuple(_softmax_init_t(cols, vt_ref.shape[1]) for _ in heads)
    states, s = lax.fori_loop(0, qi, past_block, (init, scores(0)))
    key_row = lax.broadcasted_iota(jnp.int32, (tk, 1), 0)
    query_col = lax.broadcasted_iota(jnp.int32, (1, cols), 1) % tk
    visible = key_row <= query_col
    return update(states, tuple(jnp.where(visible, s[c], -jnp.inf) for c in heads), qi)


def _diff_attn_body(lam_ref, subln_ref, q_ref, k_ref, v_ref, o_ref, vt_ref, *, tq, heads, lam_init):
    qi = pl.program_id(2)

    @pl.when(qi == 0)
    def _():
        _fill_transposed(vt_ref, v_ref, [c * LANES for c in range(heads)], tq)

    is_map0 = (lax.broadcasted_iota(jnp.int32, (LANES, 1), 0) % 64) < 32

    def make_score_fn(c):
        q_t = q_ref[:, _lanes(c)].T
        zero = jnp.zeros_like(q_t)
        qq_t = jnp.concatenate([jnp.where(is_map0, q_t, zero), jnp.where(is_map0, zero, q_t)], axis=1)
        return lambda start: jnp.dot(k_ref[pl.ds(start, tq), _lanes(c)], qq_t, preferred_element_type=F32)

    results = _causal_flash_t([make_score_fn(c) for c in range(heads)], vt_ref, 2 * tq, qi, tq)

    lp = lam_ref[...]
    lam = (jnp.exp(jnp.sum(lp[0:1] * lp[1:2], axis=-1, keepdims=True))
           - jnp.exp(jnp.sum(lp[2:3] * lp[3:4], axis=-1, keepdims=True)) + lam_init)
    for c, (_, l, acc) in enumerate(results):
        o_t = acc / l
        o = (o_t[:, :tq] - lam * o_t[:, tq:]).T
        o_ref[:, _lanes(c)] = (_rms(o, subln_ref[...]) * (1.0 - lam_init)).astype(o_ref.dtype)


def _diff_attention(qkv, lam_params, subln, *, batch, seq, tq, heads, q_col, k_col, v_col, lam_init):
    nq = seq // tq
    width = heads * LANES
    assert A_HEADS % heads == 0 and q_col % heads == 0 and k_col % heads == 0 and v_col % heads == 0
    return pl.pallas_call(
        functools.partial(_diff_attn_body, tq=tq, heads=heads, lam_init=lam_init),
        grid=(batch, A_HEADS // heads, nq),
        in_specs=[pl.BlockSpec((4, A_QK_DIM), lambda b, h, i: (0, 0)),
                  pl.BlockSpec((1, LANES), lambda b, h, i: (0, 0)),
                  pl.BlockSpec((tq, width), lambda b, h, i: (b * nq + i, q_col // heads + h)),
                  pl.BlockSpec((seq, width), lambda b, h, i: (b, k_col // heads + h)),
                  pl.BlockSpec((seq, width), lambda b, h, i: (b, v_col // heads + h))],
        out_specs=pl.BlockSpec((tq, width), lambda b, h, i: (b * nq + i, h)),
        out_shape=jax.ShapeDtypeStruct((batch * seq, A_HEADS * LANES), BF16),
        scratch_shapes=[pltpu.VMEM((heads, LANES, seq), BF16)],
        compiler_params=_params("parallel", "parallel", "arbitrary"),
        name="diff_attn",
    )(lam_params, subln.reshape(1, LANES), qkv, qkv, qkv)


def _dilated_body(q_ref, k_ref, v_ref, o_ref, lse_ref, *, tile):
    length = q_ref.shape[0]
    row = lax.broadcasted_iota(jnp.int32, (tile, 1), 0)

    def attend(q, k, v, valid, out_start):
        s = jnp.where(valid, _dot_nt(q, k), -jnp.inf)
        m = jnp.max(s, axis=-1, keepdims=True)
        p = jnp.exp(s - m)
        l = jnp.sum(p, axis=-1, keepdims=True)
        o = jnp.dot(p.astype(BF16), v, preferred_element_type=F32) / l
        o_ref[pl.ds(out_start, tile), :] = o
        lse_ref[pl.ds(out_start, tile), :] = jnp.broadcast_to(m + jnp.log(l), (tile, LANES))

    col = lax.broadcasted_iota(jnp.int32, (1, tile), 1)
    attend(q_ref[0:tile, :], k_ref[0:tile, :], v_ref[0:tile, :], col <= row, 0)

    col2 = lax.broadcasted_iota(jnp.int32, (1, 2 * tile), 1)
    band = jnp.logical_and(col2 >= row, col2 <= row + tile)

    def body(i, _):
        q_start = pl.multiple_of(i * tile, tile)
        kv_start = pl.multiple_of((i - 1) * tile, tile)
        attend(q_ref[pl.ds(q_start, tile), :], k_ref[pl.ds(kv_start, 2 * tile), :],
               v_ref[pl.ds(kv_start, 2 * tile), :], band, q_start)
        return 0

    lax.fori_loop(1, length // tile, body, 0)


def _dilated_group(qkv, *, batch, seq, group, dil, q_col, k_col, v_col):
    n_cols = qkv.shape[1] // LANES
    length = seq // dil
    view = qkv.reshape(batch, length, dil * qkv.shape[1])
    hpg = B_HEADS_PER_GROUP
    tile = B_GROUPS[group][0] // dil
    assert tile == LANES and length % tile == 0

    def in_spec(col):
        return pl.BlockSpec((None, length, LANES),
                            lambda b, h, r: (b, 0, r * n_cols + col + group * hpg + h))

    out_spec = pl.BlockSpec((None, length, LANES), lambda b, h, r: (b, 0, r * hpg + h))
    out_sds = jax.ShapeDtypeStruct((batch, length, dil * hpg * LANES), F32)
    o, lse = pl.pallas_call(
        functools.partial(_dilated_body, tile=tile),
        grid=(batch, hpg, dil),
        in_specs=[in_spec(q_col), in_spec(k_col), in_spec(v_col)],
        out_specs=[out_spec, out_spec],
        out_shape=[out_sds, out_sds],
        compiler_params=_params("parallel", "parallel", "parallel"),
        name=f"dilated_g{group}",
    )(view, view, view)
    width = hpg * LANES
    return o.reshape(batch * seq, width), lse.reshape(batch * seq, width)


def _merge_body(o0_ref, o1_ref, o2_ref, l0_ref, l1_ref, l2_ref, out_ref):
    l0, l1, l2 = l0_ref[...], l1_ref[...], l2_ref[...]
    m = jnp.maximum(jnp.maximum(l0, l1), l2)
    w0, w1, w2 = jnp.exp(l0 - m), jnp.exp(l1 - m), jnp.exp(l2 - m)
    out = (w0 * o0_ref[...] + w1 * o1_ref[...] + w2 * o2_ref[...]) / (w0 + w1 + w2)
    out_ref[...] = out.astype(out_ref.dtype)


def _merge_groups(outs, lses, *, tm):
    t, width = outs[0].shape
    spec = pl.BlockSpec((tm, width), lambda i: (i, 0))
    return pl.pallas_call(
        _merge_body,
        grid=(t // tm,),
        in_specs=[spec] * 6,
        out_specs=spec,
        out_shape=jax.ShapeDtypeStruct((t, width), BF16),
        compiler_params=_params("parallel"),
        name="dilated_merge",
    )(*outs, *lses)


def _mla_body(q_ref, kv_ref, kr_ref, o_ref, vt_ref, *, tq, heads):
    qi = pl.program_id(2)

    @pl.when(qi == 0)
    def _():
        _fill_transposed(vt_ref, kv_ref, [(2 * c + 1) * LANES for c in range(heads)], tq)

    def make_score_fn(c):
        q_t = q_ref[:, _lanes(c, 2 * LANES)].T

        def score(start):
            k = jnp.concatenate([kv_ref[pl.ds(start, tq), _lanes(2 * c)], kr_ref[pl.ds(start, tq), :]], axis=1)
            return jnp.dot(k, q_t, preferred_element_type=F32)

        return score

    results = _causal_flash_t([make_score_fn(c) for c in range(heads)], vt_ref, tq, qi, tq)
    for c, (_, l, acc) in enumerate(results):
        o_ref[:, _lanes(c)] = (acc / l).T.astype(o_ref.dtype)


def _mla_attention(q, kv, k_rope_src, *, batch, seq, tq, heads, k_rope_col):
    nq = seq // tq
    assert C_HEADS % heads == 0
    return pl.pallas_call(
        functools.partial(_mla_body, tq=tq, heads=heads),
        grid=(batch, C_HEADS // heads, nq),
        in_specs=[pl.BlockSpec((tq, heads * 2 * LANES), lambda b, h, i: (b * nq + i, h)),
                  pl.BlockSpec((seq, heads * 2 * LANES), lambda b, h, i: (b, h)),
                  pl.BlockSpec((seq, LANES), lambda b, h, i: (b, k_rope_col))],
        out_specs=pl.BlockSpec((tq, heads * LANES), lambda b, h, i: (b * nq + i, h)),
        out_shape=jax.ShapeDtypeStruct((batch * seq, C_HEADS * LANES), BF16),
        scratch_shapes=[pltpu.VMEM((heads, LANES, seq), BF16)],
        compiler_params=_params("parallel", "parallel", "arbitrary"),
        name="mla_attn",
    )(q, kv, k_rope_src)


def _moba_body(q_ref, k_ref, v_ref, o_ref, kmean_ref, vt_ref, sel_ref, *, n_blk, heads):
    qi = pl.program_id(2)
    blk = MOBA_BLOCK

    @pl.when(qi == 0)
    def _():
        for c in range(heads):
            k_all = k_ref[:, _lanes(c)].astype(F32).reshape(n_blk, blk, LANES)
            kmean_ref[c] = jnp.mean(k_all, axis=1)
        _fill_transposed(vt_ref, v_ref, [c * LANES for c in range(heads)], blk)

    blk_id = lax.broadcasted_iota(jnp.int32, (n_blk, 1), 0)
    past = blk_id < qi

    def make_score_fn(c):
        q_t = q_ref[:, _lanes(c)].T
        km = kmean_ref[c]
        km_hi = km.astype(BF16)
        rem = km - km_hi.astype(F32)
        km_mid = rem.astype(BF16)
        km_lo = (rem - km_mid.astype(F32)).astype(BF16)
        gate = (jnp.dot(km_hi, q_t, preferred_element_type=F32) + jnp.dot(km_mid, q_t, preferred_element_type=F32)
                + jnp.dot(km_lo, q_t, preferred_element_type=F32))
        gate = jnp.where(past, gate, -jnp.inf)
        selected = jnp.zeros(gate.shape, F32)
        for _ in range(min(MOBA_TOPK, n_blk - 1)):
            best = jnp.max(gate, axis=0, keepdims=True)
            first = jnp.min(jnp.where(gate == best, blk_id, n_blk), axis=0, keepdims=True)
            pick = blk_id == first
            selected = jnp.where(jnp.logical_and(pick, past), 1.0, selected)
            gate = jnp.where(pick, -jnp.inf, gate)
        sel_ref[c] = selected
        return lambda start: jnp.dot(k_ref[pl.ds(start, blk), _lanes(c)], q_t, preferred_element_type=F32)

    score_fns = [make_score_fn(c) for c in range(heads)]
    chosen_fns = [lambda n, c=c: sel_ref[c, pl.ds(n, 1), :] > 0.0 for c in range(heads)]
    results = _causal_flash_t(score_fns, vt_ref, blk, qi, blk, past_mask_fns=chosen_fns)
    for c, (_, l, acc) in enumerate(results):
        o_ref[:, _lanes(c)] = (acc / l).T.astype(o_ref.dtype)


def _moba_attention(qkv, *, batch, seq, heads, q_col, k_col, v_col):
    blk = MOBA_BLOCK
    n_blk = seq // blk
    width = heads * LANES
    assert seq % blk == 0 and D_HEADS % heads == 0
    assert q_col % heads == 0 and k_col % heads == 0 and v_col % heads == 0
    return pl.pallas_call(
        functools.partial(_moba_body, n_blk=n_blk, heads=heads),
        grid=(batch, D_HEADS // heads, n_blk),
        in_specs=[pl.BlockSpec((blk, width), lambda b, h, i: (b * n_blk + i, q_col // heads + h)),
                  pl.BlockSpec((seq, width), lambda b, h, i: (b, k_col // heads + h)),
                  pl.BlockSpec((seq, width), lambda b, h, i: (b, v_col // heads + h))],
        out_specs=pl.BlockSpec((blk, width), lambda b, h, i: (b * n_blk + i, h)),
        out_shape=jax.ShapeDtypeStruct((batch * seq, D_HEADS * LANES), BF16),
        scratch_shapes=[pltpu.VMEM((heads, n_blk, LANES), F32),
                        pltpu.VMEM((heads, LANES, seq), BF16),
                        pltpu.VMEM((heads, n_blk, blk), F32)],
        compiler_params=_params("parallel", "parallel", "arbitrary"),
        name="moba_attn",
    )(qkv, qkv, qkv)


def _rope_tables(seq):
    pos = jnp.arange(seq, dtype=F32)[:, None]

    def angles(dim):
        inv = 1.0 / (ROPE_THETA ** (jnp.arange(0, dim, 2, dtype=F32) / dim))
        return pos * inv[None, :]

    a128 = angles(HEAD_DIM)
    a64 = angles(A_QK_DIM)
    c128 = jnp.concatenate([jnp.cos(a128)] * 2, axis=1)
    s128 = jnp.concatenate([-jnp.sin(a128), jnp.sin(a128)], axis=1)
    c64 = jnp.concatenate([jnp.cos(a64)] * 4, axis=1)
    s64 = jnp.concatenate([-jnp.sin(a64)] * 2 + [jnp.sin(a64)] * 2, axis=1)
    return c128, s128, c64, s64


def _interleave_diff_heads(w):
    d = w.shape[0]
    return w.reshape(d, A_HEADS, 2, 2, 32).transpose(0, 1, 3, 2, 4).reshape(d, A_HEADS * LANES)


def _spread_rope64(w):
    z = jnp.zeros(w.shape[:-1] + (32,), w.dtype)
    return jnp.concatenate([w[..., :32], z, w[..., 32:], z], axis=-1)


def _ab_mixer(x, gain, w_in, lam_params, subln, w_out, layer_idx, rope, *, batch, seq):
    aw = A_HEADS * LANES
    w = jnp.concatenate([_interleave_diff_heads(w_in[:, :aw]), _interleave_diff_heads(w_in[:, aw:2 * aw]),
                         w_in[:, 2 * aw:]], axis=1).astype(BF16)
    a_scale = A_QK_DIM ** -0.5
    b_scale = HEAD_DIM ** -0.5
    modes = ([(ROPE64, a_scale)] * A_HEADS + [(ROPE64, 1.0)] * A_HEADS + [(PLAIN, 1.0)] * A_HEADS
             + [(ROPE128, b_scale)] * B_HEADS + [(ROPE128, 1.0)] * B_HEADS + [(PLAIN, 1.0)] * B_HEADS)
    qkv = _norm_proj(x, gain, w, rope, modes, seq=seq, tm=512, tn=512, out_dtype=BF16)
    lam_init = 0.8 - 0.6 * math.exp(-0.3 * layer_idx)
    oa = _diff_attention(qkv, lam_params, subln, batch=batch, seq=seq, tq=256, heads=2,
                         q_col=0, k_col=A_HEADS, v_col=2 * A_HEADS, lam_init=lam_init)
    b0 = 3 * A_HEADS
    outs, lses = [], []
    for g, (_, dil) in enumerate(B_GROUPS):
        o, lse = _dilated_group(qkv, batch=batch, seq=seq, group=g, dil=dil,
                                q_col=b0, k_col=b0 + B_HEADS, v_col=b0 + 2 * B_HEADS)
        outs.append(o)
        lses.append(lse)
    ob = _merge_groups(outs, lses, tm=512)
    o = jnp.concatenate([oa, ob], axis=1)
    return _out_proj(o, w_out.astype(BF16), x, tm=512, tn=1024)


def _cd_mixer(x, gain, w_in, q_norm, w_uq, kv_norm, w_ukv, w_out, rope, *, batch, seq):
    lat = C_Q_LORA + C_KV_LORA
    d_scale = HEAD_DIM ** -0.5
    latent = _norm_proj(x, gain, w_in[:, :lat].astype(BF16), rope, [(PLAIN, 1.0)] * (lat // LANES),
                        seq=seq, tm=512, tn=lat // 2, out_dtype=F32)
    w_main = jnp.concatenate([w_in[:, lat + C_ROPE:], _spread_rope64(w_in[:, lat:lat + C_ROPE])],
                             axis=1).astype(BF16)
    modes = ([(ROPE128, d_scale)] * D_HEADS + [(ROPE128, 1.0)] * D_HEADS + [(PLAIN, 1.0)] * D_HEADS
             + [(ROPE64, 1.0)])
    main = _norm_proj(x, gain, w_main, rope, modes, seq=seq, tm=512, tn=5 * LANES, out_dtype=BF16)

    c_scale = (C_NOPE + C_ROPE) ** -0.5
    w_q = w_uq.reshape(C_Q_LORA, C_HEADS, C_NOPE + C_ROPE)
    w_q = jnp.concatenate([w_q[..., :C_NOPE], _spread_rope64(w_q[..., C_NOPE:])], axis=-1)
    w_q = w_q.reshape(C_Q_LORA, C_HEADS * 2 * LANES).astype(BF16)
    q_modes = [(PLAIN, c_scale), (ROPE64, c_scale)] * C_HEADS
    qc = _norm_proj(latent[:, :C_Q_LORA], q_norm, w_q, rope, q_modes, seq=seq, tm=512, tn=512, out_dtype=BF16)
    kv = _norm_proj(latent[:, C_Q_LORA:], kv_norm, w_ukv.astype(BF16), rope,
                    [(PLAIN, 1.0)] * (2 * C_HEADS), seq=seq, tm=512, tn=512, out_dtype=BF16)
    oc = _mla_attention(qc, kv, main, batch=batch, seq=seq, tq=256, heads=4, k_rope_col=3 * D_HEADS)
    od = _moba_attention(main, batch=batch, seq=seq, heads=4, q_col=0, k_col=D_HEADS, v_col=2 * D_HEADS)
    o = jnp.concatenate([oc, od], axis=1)
    return _out_proj(o, w_out.astype(BF16), x, tm=512, tn=1024)


def kernel(x, p, ffn_norm, ffn_w_gu, ffn_w_down, mix_norm, ab_w_in, ab_lambda, ab_subln, ab_w_out,
           cd_w_in, cd_q_norm, cd_w_uq, cd_kv_norm, cd_w_ukv, cd_w_out, ple_norm, ple_w_gate,
           ple_w_proj, final_norm):
    batch, seq, d = x.shape
    depth = p.shape[0]
    rope = _rope_tables(seq)
    x = x.reshape(batch * seq, d)
    for i in range(depth):
        j = i // 2
        x = _ffn(x, ffn_norm[i, 0], ffn_w_gu[i, 0].astype(BF16), ffn_w_down[i, 0].astype(BF16), tm=512, tf=512)
        if i % 2 == 0:
            x = _ab_mixer(x, mix_norm[i], ab_w_in[j], ab_lambda[j], ab_subln[j], ab_w_out[j], i, rope,
                          batch=batch, seq=seq)
        else:
            x = _cd_mixer(x, mix_norm[i], cd_w_in[j], cd_q_norm[j], cd_w_uq[j], cd_kv_norm[j], cd_w_ukv[j],
                          cd_w_out[j], rope, batch=batch, seq=seq)
        x = _ffn(x, ffn_norm[i, 1], ffn_w_gu[i, 1].astype(BF16), ffn_w_down[i, 1].astype(BF16), tm=512, tf=512)
        x = _ple(x, p[i].reshape(batch * seq, -1), ple_norm[i], ple_w_gate[i].astype(BF16),
                 ple_w_proj[i].astype(BF16), final_norm, tm=256, final_norm=(i == depth - 1))
    return x.reshape(batch, seq, d)
```

```python
import functools
import math

import jax
import jax.numpy as jnp
from jax import lax
from jax.experimental import pallas as pl
from jax.experimental.pallas import tpu as pltpu

F32 = jnp.float32
BF16 = jnp.bfloat16

D_MODEL = 2048
D_FF = 5632
PLE_DIM = 256
HEAD_DIM = 128
ROPE_THETA = 10000.0
NORM_EPS = 1e-6

A_HEADS = 8
A_QK_DIM = 64
B_GROUPS = ((128, 1), (512, 4), (2048, 16))
B_HEADS_PER_GROUP = 4
B_HEADS = B_HEADS_PER_GROUP * len(B_GROUPS)
C_HEADS = 8
C_Q_LORA = 512
C_KV_LORA = 256
C_NOPE = 128
C_ROPE = 64
D_HEADS = 8
MOBA_BLOCK = 256
MOBA_TOPK = 3

LANES = 128
VMEM_LIMIT = 56 * 1024 * 1024

PLAIN, ROPE128, ROPE64 = 0, 1, 2


def _rms(x, gain):
    return x * lax.rsqrt(jnp.mean(x * x, axis=-1, keepdims=True) + NORM_EPS) * gain


def _params(*semantics):
    return pltpu.CompilerParams(dimension_semantics=semantics, vmem_limit_bytes=VMEM_LIMIT)


def _norm_proj_body(x_ref, g_ref, w_ref, c128_ref, s128_ref, c64_ref, s64_ref, o_ref, xn_ref, stage_ref, *,
                    tile_patterns):
    j = pl.program_id(1)
    tm = o_ref.shape[0]

    @pl.when(j == 0)
    def _():
        xn_ref[...] = _rms(x_ref[...], g_ref[...]).astype(BF16)

    def tile(pattern):
        acc = jnp.dot(xn_ref[...], w_ref[...], preferred_element_type=F32)
        for c, (kind, scale, dil) in enumerate(pattern):
            a = acc[:, _lanes(c)]
            if scale != 1.0:
                a = a * scale
            if kind == ROPE128:
                a = a * c128_ref[...] + pltpu.roll(a, 64, 1) * s128_ref[...]
            elif kind == ROPE64:
                a = a * c64_ref[...] + pltpu.roll(a, 64, 1) * s64_ref[...]
            if dil == 1:
                o_ref[:, _lanes(c)] = a.astype(o_ref.dtype)
            else:
                rows = tm // dil
                stage_ref[c] = a
                for r in range(dil):
                    o_ref[r * rows:(r + 1) * rows, _lanes(c)] = (
                        stage_ref[c, pl.ds(r, rows, stride=dil), :].astype(o_ref.dtype))

    distinct = sorted(set(tile_patterns), key=tile_patterns.index)
    if len(distinct) == 1:
        tile(distinct[0])
    else:
        for pattern in distinct:
            tiles = [t for t, p in enumerate(tile_patterns) if p == pattern]
            cond = functools.reduce(jnp.logical_or, [j == t for t in tiles])
            pl.when(cond)(functools.partial(tile, pattern))


def _norm_proj(x, gain, w, rope, chunk_modes, *, seq, tm, tn, out_dtype):
    t, k = x.shape
    n = w.shape[1]
    per_tile = tn // LANES
    assert t % tm == 0 and n % tn == 0 and seq % tm == 0 and len(chunk_modes) * LANES == n
    tile_patterns = tuple(tuple(chunk_modes[a * per_tile:(a + 1) * per_tile]) for a in range(n // tn))
    seq_tiles = seq // tm
    tab = pl.BlockSpec((tm, LANES), lambda i, j: (i % seq_tiles, 0))
    return pl.pallas_call(
        functools.partial(_norm_proj_body, tile_patterns=tile_patterns),
        grid=(t // tm, n // tn),
        in_specs=[pl.BlockSpec((tm, k), lambda i, j: (i, 0)),
                  pl.BlockSpec((1, k), lambda i, j: (0, 0)),
                  pl.BlockSpec((k, tn), lambda i, j: (0, j)),
                  tab, tab, tab, tab],
        out_specs=pl.BlockSpec((tm, tn), lambda i, j: (i, j)),
        out_shape=jax.ShapeDtypeStruct((t, n), out_dtype),
        scratch_shapes=[pltpu.VMEM((tm, k), BF16), pltpu.VMEM((per_tile, tm, LANES), F32)],
        compiler_params=_params("parallel", "arbitrary"),
        name="norm_proj",
    )(x, gain.reshape(1, k), w, *rope)


def _out_proj_body(a_ref, w_ref, res_ref, o_ref):
    o_ref[...] = res_ref[...] + jnp.dot(a_ref[...], w_ref[...], preferred_element_type=F32)


def _out_proj(a, w, res, *, tm, tn):
    t, k = a.shape
    n = w.shape[1]
    assert t % tm == 0 and n % tn == 0
    return pl.pallas_call(
        _out_proj_body,
        grid=(t // tm, n // tn),
        in_specs=[pl.BlockSpec((tm, k), lambda i, j: (i, 0)),
                  pl.BlockSpec((k, tn), lambda i, j: (0, j)),
                  pl.BlockSpec((tm, tn), lambda i, j: (i, j))],
        out_specs=pl.BlockSpec((tm, tn), lambda i, j: (i, j)),
        out_shape=jax.ShapeDtypeStruct((t, n), F32),
        compiler_params=_params("parallel", "parallel"),
        name="out_proj",
    )(a, w, res)


def _ffn_body(x_ref, g_ref, wg_ref, wu_ref, wd_ref, o_ref, xn_ref, acc_ref):
    j = pl.program_id(1)

    @pl.when(j == 0)
    def _():
        xn_ref[...] = _rms(x_ref[...], g_ref[...]).astype(BF16)
        acc_ref[...] = jnp.zeros_like(acc_ref)

    xn = xn_ref[...]
    g = jnp.dot(xn, wg_ref[...], preferred_element_type=F32)
    u = jnp.dot(xn, wu_ref[...], preferred_element_type=F32)
    h = (g * jax.nn.sigmoid(g) * u).astype(BF16)
    acc_ref[...] += jnp.dot(h, wd_ref[...], preferred_element_type=F32)

    @pl.when(j == pl.num_programs(1) - 1)
    def _():
        o_ref[...] = x_ref[...] + 0.5 * acc_ref[...]


def _ffn(x, gain, w_gu, w_down, *, tm, tf):
    t, d = x.shape
    d_ff = w_down.shape[0]
    assert t % tm == 0 and d_ff % tf == 0
    nf = d_ff // tf
    return pl.pallas_call(
        _ffn_body,
        grid=(t // tm, nf),
        in_specs=[pl.BlockSpec((tm, d), lambda i, j: (i, 0)),
                  pl.BlockSpec((1, d), lambda i, j: (0, 0)),
                  pl.BlockSpec((d, tf), lambda i, j: (0, j)),
                  pl.BlockSpec((d, tf), lambda i, j: (0, nf + j)),
                  pl.BlockSpec((tf, d), lambda i, j: (j, 0))],
        out_specs=pl.BlockSpec((tm, d), lambda i, j: (i, 0)),
        out_shape=jax.ShapeDtypeStruct((t, d), F32),
        scratch_shapes=[pltpu.VMEM((tm, d), BF16), pltpu.VMEM((tm, d), F32)],
        compiler_params=_params("parallel", "arbitrary"),
        name="ffn",
    )(x, gain.reshape(1, d), w_gu, w_gu, w_down)


def _ple_body(x_ref, p_ref, g_ref, wg_ref, wp_ref, fg_ref, o_ref, *, final_norm):
    x = x_ref[...]
    xn = _rms(x, g_ref[...]).astype(BF16)
    gate = jax.nn.sigmoid(jnp.dot(xn, wg_ref[...], preferred_element_type=F32))
    proj = jnp.dot(p_ref[...].astype(BF16), wp_ref[...], preferred_element_type=F32)
    y = x + gate * proj
    if final_norm:
        y = _rms(y, fg_ref[...])
    o_ref[...] = y


def _ple(x, p, gain, w_gate, w_proj, final_gain, *, tm, final_norm):
    t, d = x.shape
    pd = p.shape[1]
    assert t % tm == 0
    const = lambda i: (0, 0)
    return pl.pallas_call(
        functools.partial(_ple_body, final_norm=final_norm),
        grid=(t // tm,),
        in_specs=[pl.BlockSpec((tm, d), lambda i: (i, 0)),
                  pl.BlockSpec((tm, pd), lambda i: (i, 0)),
                  pl.BlockSpec((1, d), const),
                  pl.BlockSpec((d, d), const),
                  pl.BlockSpec((pd, d), const),
                  pl.BlockSpec((1, d), const)],
        out_specs=pl.BlockSpec((tm, d), lambda i: (i, 0)),
        out_shape=jax.ShapeDtypeStruct((t, d), F32),
        compiler_params=_params("parallel"),
        name="ple",
    )(x, p, gain.reshape(1, d), w_gate, w_proj, final_gain.reshape(1, d))


def _dot_nt(a, b):
    return lax.dot_general(a, b, (((1,), (1,)), ((), ())), preferred_element_type=F32)


def _lanes(c, width=LANES):
    return slice(c * width, (c + 1) * width)


def _softmax_update_t(carry, s, v_t):
    m, l, acc = carry
    m_new = jnp.maximum(m, jnp.max(s, axis=0, keepdims=True))
    alpha = jnp.exp(m - m_new)
    p = jnp.exp(s - m_new)
    l = alpha * l + jnp.sum(p, axis=0, keepdims=True)
    acc = alpha * acc + jnp.dot(v_t, p.astype(BF16), preferred_element_type=F32)
    return m_new, l, acc


def _softmax_init_t(cols, dv):
    return (jnp.full((1, cols), jnp.finfo(F32).min, F32), jnp.zeros((1, cols), F32),
            jnp.zeros((dv, cols), F32))


def _fill_transposed(vt_ref, v_ref, lane_offsets, chunk):
    dv = vt_ref.shape[1]

    def body(j, _):
        start = pl.multiple_of(j * chunk, chunk)
        for c, off in enumerate(lane_offsets):
            vt_ref[c, :, pl.ds(start, chunk)] = v_ref[pl.ds(start, chunk), off:off + dv].T
        return 0

    lax.fori_loop(0, v_ref.shape[0] // chunk, body, 0)


def _causal_flash_t(score_fns, vt_ref, cols, qi, tk, past_mask_fns=None):
    heads = range(len(score_fns))

    def scores(block):
        start = pl.multiple_of(block * tk, tk)
        return tuple(score_fns[c](start) for c in heads)

    def update(states, s, block):
        start = pl.multiple_of(block * tk, tk)
        return tuple(_softmax_update_t(states[c], s[c], vt_ref[c, :, pl.ds(start, tk)]) for c in heads)

    def past_block(j, carry):
        states, s = carry
        s_next = scores(j + 1)
        if past_mask_fns is not None:
            s = tuple(jnp.where(past_mask_fns[c](j), s[c], -jnp.inf) for c in heads)
        return update(states, s, j), s_next

    init = tuple(_softmax_init_t(cols, vt_ref.shape[1]) for _ in heads)
    states, s = lax.fori_loop(0, qi, past_block, (init, scores(0)))
    key_row = lax.broadcasted_iota(jnp.int32, (tk, 1), 0)
    query_col = lax.broadcasted_iota(jnp.int32, (1, cols), 1) % tk
    visible = key_row <= query_col
    return update(states, tuple(jnp.where(visible, s[c], -jnp.inf) for c in heads), qi)


def _diff_attn_body(lam_ref, subln_ref, q_ref, k_ref, v_ref, o_ref, vt_ref, *, tq, heads, lam_init):
    qi = pl.program_id(2)

    @pl.when(qi == 0)
    def _():
        _fill_transposed(vt_ref, v_ref, [c * LANES for c in range(heads)], tq)

    is_map0 = (lax.broadcasted_iota(jnp.int32, (LANES, 1), 0) % 64) < 32

    def make_score_fn(c):
        q_t = q_ref[:, _lanes(c)].T
        zero = jnp.zeros_like(q_t)
        qq_t = jnp.concatenate([jnp.where(is_map0, q_t, zero), jnp.where(is_map0, zero, q_t)], axis=1)
        return lambda start: jnp.dot(k_ref[pl.ds(start, tq), _lanes(c)], qq_t, preferred_element_type=F32)

    results = _causal_flash_t([make_score_fn(c) for c in range(heads)], vt_ref, 2 * tq, qi, tq)

    lp = lam_ref[...]
    lam = (jnp.exp(jnp.sum(lp[0:1] * lp[1:2], axis=-1, keepdims=True))
           - jnp.exp(jnp.sum(lp[2:3] * lp[3:4], axis=-1, keepdims=True)) + lam_init)
    for c, (_, l, acc) in enumerate(results):
        o_t = acc / l
        o = (o_t[:, :tq] - lam * o_t[:, tq:]).T
        o_ref[:, _lanes(c)] = (_rms(o, subln_ref[...]) * (1.0 - lam_init)).astype(o_ref.dtype)


def _diff_attention(qkv, lam_params, subln, *, batch, seq, tq, heads, q_col, k_col, v_col, lam_init):
    nq = seq // tq
    width = heads * LANES
    assert A_HEADS % heads == 0 and q_col % heads == 0 and k_col % heads == 0 and v_col % heads == 0
    return pl.pallas_call(
        functools.partial(_diff_attn_body, tq=tq, heads=heads, lam_init=lam_init),
        grid=(batch, A_HEADS // heads, nq),
        in_specs=[pl.BlockSpec((4, A_QK_DIM), lambda b, h, i: (0, 0)),
                  pl.BlockSpec((1, LANES), lambda b, h, i: (0, 0)),
                  pl.BlockSpec((tq, width), lambda b, h, i: (b * nq + i, q_col // heads + h)),
                  pl.BlockSpec((seq, width), lambda b, h, i: (b, k_col // heads + h)),
                  pl.BlockSpec((seq, width), lambda b, h, i: (b, v_col // heads + h))],
        out_specs=pl.BlockSpec((tq, width), lambda b, h, i: (b * nq + i, h)),
        out_shape=jax.ShapeDtypeStruct((batch * seq, A_HEADS * LANES), BF16),
        scratch_shapes=[pltpu.VMEM((heads, LANES, seq), BF16)],
        compiler_params=_params("parallel", "parallel", "arbitrary"),
        name="diff_attn",
    )(lam_params, subln.reshape(1, LANES), qkv, qkv, qkv)


def _tile_index(ref, i, tile):
    per_slab = ref.shape[1]
    if per_slab >= tile:
        runs = per_slab // tile
        return i // runs, pl.ds(pl.multiple_of((i % runs) * tile, tile), tile)
    slabs = tile // per_slab
    return pl.ds(i * slabs, slabs), slice(None)


def _load_tile(ref, i, c, tile):
    slab, rows = _tile_index(ref, i, tile)
    return ref[slab, rows, _lanes(c)].reshape(tile, LANES)


def _store_tile(ref, i, c, tile, value):
    slab, rows = _tile_index(ref, i, tile)
    per_slab = ref.shape[1]
    ref[slab, rows, _lanes(c)] = value if per_slab >= tile else value.reshape(tile // per_slab, per_slab, LANES)


def _dilated_body(q_ref, k_ref, v_ref, o_ref, lse_ref, *, tile, heads):
    length = q_ref.shape[0] * q_ref.shape[1]
    row = lax.broadcasted_iota(jnp.int32, (tile, 1), 0)

    def attend(i, c, k, v, valid):
        s = jnp.where(valid, _dot_nt(_load_tile(q_ref, i, c, tile), k), -jnp.inf)
        m = jnp.max(s, axis=-1, keepdims=True)
        p = jnp.exp(s - m)
        l = jnp.sum(p, axis=-1, keepdims=True)
        o = jnp.dot(p.astype(BF16), v, preferred_element_type=F32) / l
        _store_tile(o_ref, i, c, tile, o)
        _store_tile(lse_ref, i, c, tile, jnp.broadcast_to(m + jnp.log(l), (tile, LANES)))

    col = lax.broadcasted_iota(jnp.int32, (1, tile), 1)
    for c in range(heads):
        attend(0, c, _load_tile(k_ref, 0, c, tile), _load_tile(v_ref, 0, c, tile), col <= row)

    col2 = lax.broadcasted_iota(jnp.int32, (1, 2 * tile), 1)
    band = jnp.logical_and(col2 >= row, col2 <= row + tile)

    def body(i, _):
        for c in range(heads):
            k = jnp.concatenate([_load_tile(k_ref, i - 1, c, tile), _load_tile(k_ref, i, c, tile)], axis=0)
            v = jnp.concatenate([_load_tile(v_ref, i - 1, c, tile), _load_tile(v_ref, i, c, tile)], axis=0)
            attend(i, c, k, v, band)
        return 0

    lax.fori_loop(1, length // tile, body, 0)


def _dilated_group(qkv, *, batch, seq, row_tile, group, dil, heads, q_col, k_col, v_col):
    hpg = B_HEADS_PER_GROUP
    tile = B_GROUPS[group][0] // dil
    slabs, per_slab = seq // row_tile, row_tile // dil
    assert tile == LANES and hpg % heads == 0 and (seq // dil) % tile == 0
    assert per_slab % tile == 0 or tile % per_slab == 0
    width = heads * LANES
    view = qkv.reshape(batch, slabs, dil, per_slab, qkv.shape[1])

    def in_spec(col):
        first = col + group * hpg
        assert first % heads == 0
        return pl.BlockSpec((None, slabs, None, per_slab, width),
                            lambda b, h, r: (b, 0, r, 0, first // heads + h))

    out_spec = pl.BlockSpec((None, slabs, None, per_slab, width), lambda b, h, r: (b, 0, r, 0, h))
    out_sds = jax.ShapeDtypeStruct((batch, slabs, dil, per_slab, hpg * LANES), F32)
    o, lse = pl.pallas_call(
        functools.partial(_dilated_body, tile=tile, heads=heads),
        grid=(batch, hpg // heads, dil),
        in_specs=[in_spec(q_col), in_spec(k_col), in_spec(v_col)],
        out_specs=[out_spec, out_spec],
        out_shape=[out_sds, out_sds],
        compiler_params=_params("parallel", "parallel", "parallel"),
        name=f"dilated_g{group}",
    )(view, view, view)
    shape = (batch * slabs, dil, per_slab, hpg * LANES)
    return o.reshape(shape), lse.reshape(shape)


def _merge_body(o0_ref, o1_ref, o2_ref, l0_ref, l1_ref, l2_ref, out_ref, nat_ref):
    def natural(ref, slot, c):
        dil, rows = ref.shape[0], ref.shape[1]
        if dil == 1:
            return ref[0, :, _lanes(c)]
        for r in range(dil):
            nat_ref[slot, pl.ds(r, rows, stride=dil), :] = ref[r, :, _lanes(c)]
        return nat_ref[slot]

    for c in range(out_ref.shape[1] // LANES):
        o0, o1, o2 = natural(o0_ref, None, c), natural(o1_ref, 0, c), natural(o2_ref, 1, c)
        l0, l1, l2 = natural(l0_ref, None, c), natural(l1_ref, 2, c), natural(l2_ref, 3, c)
        m = jnp.maximum(jnp.maximum(l0, l1), l2)
        w0, w1, w2 = jnp.exp(l0 - m), jnp.exp(l1 - m), jnp.exp(l2 - m)
        out_ref[:, _lanes(c)] = ((w0 * o0 + w1 * o1 + w2 * o2) / (w0 + w1 + w2)).astype(out_ref.dtype)


def _merge_groups(outs, lses):
    tiles, _, _, width = outs[0].shape
    row_tile = outs[0].shape[1] * outs[0].shape[2]
    specs = [pl.BlockSpec((None,) + a.shape[1:], lambda i: (i, 0, 0, 0)) for a in (*outs, *lses)]
    return pl.pallas_call(
        _merge_body,
        grid=(tiles,),
        in_specs=specs,
        out_specs=pl.BlockSpec((row_tile, width), lambda i: (i, 0)),
        out_shape=jax.ShapeDtypeStruct((tiles * row_tile, width), BF16),
        scratch_shapes=[pltpu.VMEM((4, row_tile, LANES), F32)],
        compiler_params=_params("parallel"),
        name="dilated_merge",
    )(*outs, *lses)


def _mla_body(q_ref, kv_ref, kr_ref, o_ref, vt_ref, *, tq, heads):
    qi = pl.program_id(2)

    @pl.when(qi == 0)
    def _():
        _fill_transposed(vt_ref, kv_ref, [(2 * c + 1) * LANES for c in range(heads)], tq)

    def make_score_fn(c):
        q_t = q_ref[:, _lanes(c, 2 * LANES)].T

        def score(start):
            k = jnp.concatenate([kv_ref[pl.ds(start, tq), _lanes(2 * c)], kr_ref[pl.ds(start, tq), :]], axis=1)
            return jnp.dot(k, q_t, preferred_element_type=F32)

        return score

    results = _causal_flash_t([make_score_fn(c) for c in range(heads)], vt_ref, tq, qi, tq)
    for c, (_, l, acc) in enumerate(results):
        o_ref[:, _lanes(c)] = (acc / l).T.astype(o_ref.dtype)


def _mla_attention(q, kv, k_rope_src, *, batch, seq, tq, heads, k_rope_col):
    nq = seq // tq
    assert C_HEADS % heads == 0
    return pl.pallas_call(
        functools.partial(_mla_body, tq=tq, heads=heads),
        grid=(batch, C_HEADS // heads, nq),
        in_specs=[pl.BlockSpec((tq, heads * 2 * LANES), lambda b, h, i: (b * nq + i, h)),
                  pl.BlockSpec((seq, heads * 2 * LANES), lambda b, h, i: (b, h)),
                  pl.BlockSpec((seq, LANES), lambda b, h, i: (b, k_rope_col))],
        out_specs=pl.BlockSpec((tq, heads * LANES), lambda b, h, i: (b * nq + i, h)),
        out_shape=jax.ShapeDtypeStruct((batch * seq, C_HEADS * LANES), BF16),
        scratch_shapes=[pltpu.VMEM((heads, LANES, seq), BF16)],
        compiler_params=_params("parallel", "parallel", "arbitrary"),
        name="mla_attn",
    )(q, kv, k_rope_src)


def _moba_body(q_ref, k_ref, v_ref, o_ref, kmean_ref, vt_ref, sel_ref, *, n_blk, heads):
    qi = pl.program_id(2)
    blk = MOBA_BLOCK

    @pl.when(qi == 0)
    def _():
        for c in range(heads):
            k_all = k_ref[:, _lanes(c)].astype(F32).reshape(n_blk, blk, LANES)
            kmean_ref[c] = jnp.mean(k_all, axis=1)
        _fill_transposed(vt_ref, v_ref, [c * LANES for c in range(heads)], blk)

    blk_id = lax.broadcasted_iota(jnp.int32, (n_blk, 1), 0)
    past = blk_id < qi

    def make_score_fn(c):
        q_t = q_ref[:, _lanes(c)].T
        km = kmean_ref[c]
        km_hi = km.astype(BF16)
        rem = km - km_hi.astype(F32)
        km_mid = rem.astype(BF16)
        km_lo = (rem - km_mid.astype(F32)).astype(BF16)
        gate = (jnp.dot(km_hi, q_t, preferred_element_type=F32) + jnp.dot(km_mid, q_t, preferred_element_type=F32)
                + jnp.dot(km_lo, q_t, preferred_element_type=F32))
        gate = jnp.where(past, gate, -jnp.inf)
        selected = jnp.zeros(gate.shape, F32)
        for _ in range(min(MOBA_TOPK, n_blk - 1)):
            best = jnp.max(gate, axis=0, keepdims=True)
            first = jnp.min(jnp.where(gate == best, blk_id, n_blk), axis=0, keepdims=True)
            pick = blk_id == first
            selected = jnp.where(jnp.logical_and(pick, past), 1.0, selected)
            gate = jnp.where(pick, -jnp.inf, gate)
        sel_ref[c] = selected
        return lambda start: jnp.dot(k_ref[pl.ds(start, blk), _lanes(c)], q_t, preferred_element_type=F32)

    score_fns = [make_score_fn(c) for c in range(heads)]
    chosen_fns = [lambda n, c=c: sel_ref[c, pl.ds(n, 1), :] > 0.0 for c in range(heads)]
    results = _causal_flash_t(score_fns, vt_ref, blk, qi, blk, past_mask_fns=chosen_fns)
    for c, (_, l, acc) in enumerate(results):
        o_ref[:, _lanes(c)] = (acc / l).T.astype(o_ref.dtype)


def _moba_attention(qkv, *, batch, seq, heads, q_col, k_col, v_col):
    blk = MOBA_BLOCK
    n_blk = seq // blk
    width = heads * LANES
    assert seq % blk == 0 and D_HEADS % heads == 0
    assert q_col % heads == 0 and k_col % heads == 0 and v_col % heads == 0
    return pl.pallas_call(
        functools.partial(_moba_body, n_blk=n_blk, heads=heads),
        grid=(batch, D_HEADS // heads, n_blk),
        in_specs=[pl.BlockSpec((blk, width), lambda b, h, i: (b * n_blk + i, q_col // heads + h)),
                  pl.BlockSpec((seq, width), lambda b, h, i: (b, k_col // heads + h)),
                  pl.BlockSpec((seq, width), lambda b, h, i: (b, v_col // heads + h))],
        out_specs=pl.BlockSpec((blk, width), lambda b, h, i: (b * n_blk + i, h)),
        out_shape=jax.ShapeDtypeStruct((batch * seq, D_HEADS * LANES), BF16),
        scratch_shapes=[pltpu.VMEM((heads, n_blk, LANES), F32),
                        pltpu.VMEM((heads, LANES, seq), BF16),
                        pltpu.VMEM((heads, n_blk, blk), F32)],
        compiler_params=_params("parallel", "parallel", "arbitrary"),
        name="moba_attn",
    )(qkv, qkv, qkv)


def _rope_tables(seq):
    pos = jnp.arange(seq, dtype=F32)[:, None]

    def angles(dim):
        inv = 1.0 / (ROPE_THETA ** (jnp.arange(0, dim, 2, dtype=F32) / dim))
        return pos * inv[None, :]

    a128 = angles(HEAD_DIM)
    a64 = angles(A_QK_DIM)
    c128 = jnp.concatenate([jnp.cos(a128)] * 2, axis=1)
    s128 = jnp.concatenate([-jnp.sin(a128), jnp.sin(a128)], axis=1)
    c64 = jnp.concatenate([jnp.cos(a64)] * 4, axis=1)
    s64 = jnp.concatenate([-jnp.sin(a64)] * 2 + [jnp.sin(a64)] * 2, axis=1)
    return c128, s128, c64, s64


def _interleave_diff_heads(w):
    d = w.shape[0]
    return w.reshape(d, A_HEADS, 2, 2, 32).transpose(0, 1, 3, 2, 4).reshape(d, A_HEADS * LANES)


def _spread_rope64(w):
    z = jnp.zeros(w.shape[:-1] + (32,), w.dtype)
    return jnp.concatenate([w[..., :32], z, w[..., 32:], z], axis=-1)


def _ab_mixer(x, gain, w_in, lam_params, subln, w_out, layer_idx, rope, *, batch, seq):
    aw = A_HEADS * LANES
    w = jnp.concatenate([_interleave_diff_heads(w_in[:, :aw]), _interleave_diff_heads(w_in[:, aw:2 * aw]),
                         w_in[:, 2 * aw:]], axis=1).astype(BF16)
    a_scale = A_QK_DIM ** -0.5
    b_scale = HEAD_DIM ** -0.5
    dils = [dil for _, dil in B_GROUPS for _ in range(B_HEADS_PER_GROUP)]
    modes = ([(ROPE64, a_scale, 1)] * A_HEADS + [(ROPE64, 1.0, 1)] * A_HEADS + [(PLAIN, 1.0, 1)] * A_HEADS
             + [(ROPE128, b_scale, d) for d in dils] + [(ROPE128, 1.0, d) for d in dils]
             + [(PLAIN, 1.0, d) for d in dils])
    row_tile = 512
    qkv = _norm_proj(x, gain, w, rope, modes, seq=seq, tm=row_tile, tn=512, out_dtype=BF16)
    lam_init = 0.8 - 0.6 * math.exp(-0.3 * layer_idx)
    oa = _diff_attention(qkv, lam_params, subln, batch=batch, seq=seq, tq=256, heads=2,
                         q_col=0, k_col=A_HEADS, v_col=2 * A_HEADS, lam_init=lam_init)
    b0 = 3 * A_HEADS
    outs, lses = [], []
    for g, (_, dil) in enumerate(B_GROUPS):
        o, lse = _dilated_group(qkv, batch=batch, seq=seq, row_tile=row_tile, group=g, dil=dil,
                                heads=2 if dil == 1 else 4,
                                q_col=b0, k_col=b0 + B_HEADS, v_col=b0 + 2 * B_HEADS)
        outs.append(o)
        lses.append(lse)
    ob = _merge_groups(outs, lses)
    o = jnp.concatenate([oa, ob], axis=1)
    return _out_proj(o, w_out.astype(BF16), x, tm=512, tn=1024)


def _cd_mixer(x, gain, w_in, q_norm, w_uq, kv_norm, w_ukv, w_out, rope, *, batch, seq):
    lat = C_Q_LORA + C_KV_LORA
    d_scale = HEAD_DIM ** -0.5
    latent = _norm_proj(x, gain, w_in[:, :lat].astype(BF16), rope, [(PLAIN, 1.0, 1)] * (lat // LANES),
                        seq=seq, tm=512, tn=lat // 2, out_dtype=F32)
    w_main = jnp.concatenate([w_in[:, lat + C_ROPE:], _spread_rope64(w_in[:, lat:lat + C_ROPE])],
                             axis=1).astype(BF16)
    modes = ([(ROPE128, d_scale, 1)] * D_HEADS + [(ROPE128, 1.0, 1)] * D_HEADS + [(PLAIN, 1.0, 1)] * D_HEADS
             + [(ROPE64, 1.0, 1)])
    main = _norm_proj(x, gain, w_main, rope, modes, seq=seq, tm=512, tn=5 * LANES, out_dtype=BF16)

    c_scale = (C_NOPE + C_ROPE) ** -0.5
    w_q = w_uq.reshape(C_Q_LORA, C_HEADS, C_NOPE + C_ROPE)
    w_q = jnp.concatenate([w_q[..., :C_NOPE], _spread_rope64(w_q[..., C_NOPE:])], axis=-1)
    w_q = w_q.reshape(C_Q_LORA, C_HEADS * 2 * LANES).astype(BF16)
    q_modes = [(PLAIN, c_scale, 1), (ROPE64, c_scale, 1)] * C_HEADS
    qc = _norm_proj(latent[:, :C_Q_LORA], q_norm, w_q, rope, q_modes, seq=seq, tm=512, tn=512, out_dtype=BF16)
    kv = _norm_proj(latent[:, C_Q_LORA:], kv_norm, w_ukv.astype(BF16), rope,
                    [(PLAIN, 1.0, 1)] * (2 * C_HEADS), seq=seq, tm=512, tn=512, out_dtype=BF16)
    oc = _mla_attention(qc, kv, main, batch=batch, seq=seq, tq=256, heads=4, k_rope_col=3 * D_HEADS)
    od = _moba_attention(main, batch=batch, seq=seq, heads=4, q_col=0, k_col=D_HEADS, v_col=2 * D_HEADS)
    o = jnp.concatenate([oc, od], axis=1)
    return _out_proj(o, w_out.astype(BF16), x, tm=512, tn=1024)


def kernel(x, p, ffn_norm, ffn_w_gu, ffn_w_down, mix_norm, ab_w_in, ab_lambda, ab_subln, ab_w_out,
           cd_w_in, cd_q_norm, cd_w_uq, cd_kv_norm, cd_w_ukv, cd_w_out, ple_norm, ple_w_gate,
           ple_w_proj, final_norm):
    batch, seq, d = x.shape
    depth = p.shape[0]
    rope = _rope_tables(seq)
    x = x.reshape(batch * seq, d)
    for i in range(depth):
        j = i // 2
        x = _ffn(x, ffn_norm[i, 0], ffn_w_gu[i, 0].astype(BF16), ffn_w_down[i, 0].astype(BF16), tm=512, tf=512)
        if i % 2 == 0:
            x = _ab_mixer(x, mix_norm[i], ab_w_in[j], ab_lambda[j], ab_subln[j], ab_w_out[j], i, rope,
                          batch=batch, seq=seq)
        else:
            x = _cd_mixer(x, mix_norm[i], cd_w_in[j], cd_q_norm[j], cd_w_uq[j], cd_kv_norm[j], cd_w_ukv[j],
                          cd_w_out[j], rope, batch=batch, seq=seq)
        x = _ffn(x, ffn_norm[i, 1], ffn_w_gu[i, 1].astype(BF16), ffn_w_down[i, 1].astype(BF16), tm=512, tf=512)
        x = _ple(x, p[i].reshape(batch * seq, -1), ple_norm[i], ple_w_gate[i].astype(BF16),
                 ple_w_proj[i].astype(BF16), final_norm, tm=256, final_norm=(i == depth - 1))
    return x.reshape(batch, seq, d)
```

```python
import functools
import math

import jax
import jax.numpy as jnp
from jax import lax
from jax.experimental import pallas as pl
from jax.experimental.pallas import tpu as pltpu

F32 = jnp.float32
BF16 = jnp.bfloat16

D_MODEL = 2048
D_FF = 5632
PLE_DIM = 256
HEAD_DIM = 128
ROPE_THETA = 10000.0
NORM_EPS = 1e-6

A_HEADS = 8
A_QK_DIM = 64
B_GROUPS = ((128, 1), (512, 4), (2048, 16))
B_HEADS_PER_GROUP = 4
B_HEADS = B_HEADS_PER_GROUP * len(B_GROUPS)
C_HEADS = 8
C_Q_LORA = 512
C_KV_LORA = 256
C_NOPE = 128
C_ROPE = 64
D_HEADS = 8
MOBA_BLOCK = 256
MOBA_TOPK = 3

LANES = 128
VMEM_LIMIT = 56 * 1024 * 1024

PLAIN, ROPE128, ROPE64 = 0, 1, 2


def _rms(x, gain):
    return x * lax.rsqrt(jnp.mean(x * x, axis=-1, keepdims=True) + NORM_EPS) * gain


def _params(*semantics):
    return pltpu.CompilerParams(dimension_semantics=semantics, vmem_limit_bytes=VMEM_LIMIT)


def _norm_proj_body(x_ref, g_ref, w_ref, c128_ref, s128_ref, c64_ref, s64_ref, o_ref, xn_ref, stage_ref, *,
                    tile_patterns):
    j = pl.program_id(1)
    tm = o_ref.shape[0]

    @pl.when(j == 0)
    def _():
        xn_ref[...] = _rms(x_ref[...], g_ref[...]).astype(BF16)

    def tile(pattern):
        acc = jnp.dot(xn_ref[...], w_ref[...], preferred_element_type=F32)
        for c, (kind, scale, dil) in enumerate(pattern):
            a = acc[:, _lanes(c)]
            if scale != 1.0:
                a = a * scale
            if kind == ROPE128:
                a = a * c128_ref[...] + pltpu.roll(a, 64, 1) * s128_ref[...]
            elif kind == ROPE64:
                a = a * c64_ref[...] + pltpu.roll(a, 64, 1) * s64_ref[...]
            if dil == 1:
                o_ref[:, _lanes(c)] = a.astype(o_ref.dtype)
            else:
                rows = tm // dil
                stage_ref[c] = a
                for r in range(dil):
                    o_ref[r * rows:(r + 1) * rows, _lanes(c)] = (
                        stage_ref[c, pl.ds(r, rows, stride=dil), :].astype(o_ref.dtype))

    distinct = sorted(set(tile_patterns), key=tile_patterns.index)
    if len(distinct) == 1:
        tile(distinct[0])
    else:
        for pattern in distinct:
            tiles = [t for t, p in enumerate(tile_patterns) if p == pattern]
            cond = functools.reduce(jnp.logical_or, [j == t for t in tiles])
            pl.when(cond)(functools.partial(tile, pattern))


def _norm_proj(x, gain, w, rope, chunk_modes, *, seq, tm, tn, out_dtype):
    t, k = x.shape
    n = w.shape[1]
    per_tile = tn // LANES
    assert t % tm == 0 and n % tn == 0 and seq % tm == 0 and len(chunk_modes) * LANES == n
    tile_patterns = tuple(tuple(chunk_modes[a * per_tile:(a + 1) * per_tile]) for a in range(n // tn))
    seq_tiles = seq // tm
    tab = pl.BlockSpec((tm, LANES), lambda i, j: (i % seq_tiles, 0))
    return pl.pallas_call(
        functools.partial(_norm_proj_body, tile_patterns=tile_patterns),
        grid=(t // tm, n // tn),
        in_specs=[pl.BlockSpec((tm, k), lambda i, j: (i, 0)),
                  pl.BlockSpec((1, k), lambda i, j: (0, 0)),
                  pl.BlockSpec((k, tn), lambda i, j: (0, j)),
                  tab, tab, tab, tab],
        out_specs=pl.BlockSpec((tm, tn), lambda i, j: (i, j)),
        out_shape=jax.ShapeDtypeStruct((t, n), out_dtype),
        scratch_shapes=[pltpu.VMEM((tm, k), BF16), pltpu.VMEM((per_tile, tm, LANES), F32)],
        compiler_params=_params("parallel", "arbitrary"),
        name="norm_proj",
    )(x, gain.reshape(1, k), w, *rope)


def _out_proj_body(a_ref, w_ref, res_ref, o_ref):
    o_ref[...] = res_ref[...] + jnp.dot(a_ref[...], w_ref[...], preferred_element_type=F32)


def _out_proj(a, w, res, *, tm, tn):
    t, k = a.shape
    n = w.shape[1]
    assert t % tm == 0 and n % tn == 0
    return pl.pallas_call(
        _out_proj_body,
        grid=(t // tm, n // tn),
        in_specs=[pl.BlockSpec((tm, k), lambda i, j: (i, 0)),
                  pl.BlockSpec((k, tn), lambda i, j: (0, j)),
                  pl.BlockSpec((tm, tn), lambda i, j: (i, j))],
        out_specs=pl.BlockSpec((tm, tn), lambda i, j: (i, j)),
        out_shape=jax.ShapeDtypeStruct((t, n), F32),
        compiler_params=_params("parallel", "parallel"),
        name="out_proj",
    )(a, w, res)


def _ffn_body(x_ref, g_ref, wg_ref, wu_ref, wd_ref, o_ref, xn_ref, acc_ref):
    j = pl.program_id(1)

    @pl.when(j == 0)
    def _():
        xn_ref[...] = _rms(x_ref[...], g_ref[...]).astype(BF16)
        acc_ref[...] = jnp.zeros_like(acc_ref)

    xn = xn_ref[...]
    g = jnp.dot(xn, wg_ref[...], preferred_element_type=F32)
    u = jnp.dot(xn, wu_ref[...], preferred_element_type=F32)
    h = (g * jax.nn.sigmoid(g) * u).astype(BF16)
    acc_ref[...] += jnp.dot(h, wd_ref[...], preferred_element_type=F32)

    @pl.when(j == pl.num_programs(1) - 1)
    def _():
        o_ref[...] = x_ref[...] + 0.5 * acc_ref[...]


def _ffn(x, gain, w_gu, w_down, *, tm, tf):
    t, d = x.shape
    d_ff = w_down.shape[0]
    assert t % tm == 0 and d_ff % tf == 0
    nf = d_ff // tf
    return pl.pallas_call(
        _ffn_body,
        grid=(t // tm, nf),
        in_specs=[pl.BlockSpec((tm, d), lambda i, j: (i, 0)),
                  pl.BlockSpec((1, d), lambda i, j: (0, 0)),
                  pl.BlockSpec((d, tf), lambda i, j: (0, j)),
                  pl.BlockSpec((d, tf), lambda i, j: (0, nf + j)),
                  pl.BlockSpec((tf, d), lambda i, j: (j, 0))],
        out_specs=pl.BlockSpec((tm, d), lambda i, j: (i, 0)),
        out_shape=jax.ShapeDtypeStruct((t, d), F32),
        scratch_shapes=[pltpu.VMEM((tm, d), BF16), pltpu.VMEM((tm, d), F32)],
        compiler_params=_params("parallel", "arbitrary"),
        name="ffn",
    )(x, gain.reshape(1, d), w_gu, w_gu, w_down)


def _ple_body(x_ref, p_ref, g_ref, wg_ref, wp_ref, fg_ref, o_ref, *, final_norm):
    x = x_ref[...]
    xn = _rms(x, g_ref[...]).astype(BF16)
    gate = jax.nn.sigmoid(jnp.dot(xn, wg_ref[...], preferred_element_type=F32))
    proj = jnp.dot(p_ref[...].astype(BF16), wp_ref[...], preferred_element_type=F32)
    y = x + gate * proj
    if final_norm:
        y = _rms(y, fg_ref[...])
    o_ref[...] = y


def _ple(x, p, gain, w_gate, w_proj, final_gain, *, tm, final_norm):
    t, d = x.shape
    pd = p.shape[1]
    assert t % tm == 0
    const = lambda i: (0, 0)
    return pl.pallas_call(
        functools.partial(_ple_body, final_norm=final_norm),
        grid=(t // tm,),
        in_specs=[pl.BlockSpec((tm, d), lambda i: (i, 0)),
                  pl.BlockSpec((tm, pd), lambda i: (i, 0)),
                  pl.BlockSpec((1, d), const),
                  pl.BlockSpec((d, d), const),
                  pl.BlockSpec((pd, d), const),
                  pl.BlockSpec((1, d), const)],
        out_specs=pl.BlockSpec((tm, d), lambda i: (i, 0)),
        out_shape=jax.ShapeDtypeStruct((t, d), F32),
        compiler_params=_params("parallel"),
        name="ple",
    )(x, p, gain.reshape(1, d), w_gate, w_proj, final_gain.reshape(1, d))


def _dot_nt(a, b):
    return lax.dot_general(a, b, (((1,), (1,)), ((), ())), preferred_element_type=F32)


def _lanes(c, width=LANES):
    return slice(c * width, (c + 1) * width)


def _fill_transposed(vt_ref, v_ref, lane_offsets, chunk):
    dv = vt_ref.shape[1]

    def body(j, _):
        start = pl.multiple_of(j * chunk, chunk)
        for c, off in enumerate(lane_offsets):
            vt_ref[c, :, pl.ds(start, chunk)] = v_ref[pl.ds(start, chunk), off:off + dv].T
        return 0

    lax.fori_loop(0, v_ref.shape[0] // chunk, body, 0)


def _flash_scratch(heads, tk, cols, dv):
    return [pltpu.VMEM((heads, tk, cols), F32), pltpu.VMEM((heads, tk, cols), BF16),
            pltpu.VMEM((heads, dv, cols), F32)]


def _causal_flash_t(score_fns, vt_ref, s_ref, p_ref, acc_ref, qi, past_mask_fns=None):
    heads = range(len(score_fns))
    _, tk, cols = s_ref.shape

    def stage_scores(block):
        start = pl.multiple_of(block * tk, tk)
        for c in heads:
            s = score_fns[c](start)
            if past_mask_fns is not None:
                s = jnp.where(past_mask_fns[c](block), s, -jnp.inf)
            s_ref[c] = s

    def accumulate(block, alphas):
        start = pl.multiple_of(block * tk, tk)
        for c in heads:
            pv = jnp.dot(vt_ref[c, :, pl.ds(start, tk)], p_ref[c], preferred_element_type=F32)
            acc_ref[c] = alphas[c] * acc_ref[c] + pv

    def softmax(stats, visible=None):
        new_stats, alphas = [], []
        for c in heads:
            m, l = stats[c]
            read = (lambda: s_ref[c]) if visible is None else (lambda: jnp.where(visible, s_ref[c], -jnp.inf))
            m_new = jnp.maximum(m, jnp.max(read(), axis=0, keepdims=True))
            alpha = jnp.exp(m - m_new)
            p = jnp.exp(read() - m_new)
            p_ref[c] = p.astype(BF16)
            new_stats.append((m_new, alpha * l + jnp.sum(p, axis=0, keepdims=True)))
            alphas.append(alpha)
        return tuple(new_stats), tuple(alphas)

    def past_block(j, carry):
        stats, alphas = carry
        accumulate(jnp.maximum(j - 1, 0), alphas)
        stats, alphas = softmax(stats)
        stage_scores(j + 1)
        return stats, alphas

    for c in heads:
        p_ref[c] = jnp.zeros(p_ref.shape[1:], BF16)
        acc_ref[c] = jnp.zeros(acc_ref.shape[1:], F32)
    stage_scores(0)
    init = (tuple((jnp.full((1, cols), jnp.finfo(F32).min, F32), jnp.zeros((1, cols), F32)) for _ in heads),
            tuple(jnp.ones((1, cols), F32) for _ in heads))
    stats, alphas = lax.fori_loop(0, qi, past_block, init)

    accumulate(jnp.maximum(qi - 1, 0), alphas)
    key_row = lax.broadcasted_iota(jnp.int32, (tk, 1), 0)
    query_col = lax.broadcasted_iota(jnp.int32, (1, cols), 1) % tk
    stats, alphas = softmax(stats, visible=key_row <= query_col)
    accumulate(qi, alphas)
    return tuple((stats[c][1], acc_ref[c]) for c in heads)


def _diff_attn_body(lam_ref, subln_ref, q_ref, k_ref, v_ref, o_ref, vt_ref, s_ref, p_ref, acc_ref, *,
                    tq, heads, lam_init):
    qi = pl.program_id(2)

    @pl.when(qi == 0)
    def _():
        _fill_transposed(vt_ref, v_ref, [c * LANES for c in range(heads)], tq)

    is_map0 = (lax.broadcasted_iota(jnp.int32, (LANES, 1), 0) % 64) < 32

    def make_score_fn(c):
        q_t = q_ref[:, _lanes(c)].T
        zero = jnp.zeros_like(q_t)
        qq_t = jnp.concatenate([jnp.where(is_map0, q_t, zero), jnp.where(is_map0, zero, q_t)], axis=1)
        return lambda start: jnp.dot(k_ref[pl.ds(start, tq), _lanes(c)], qq_t, preferred_element_type=F32)

    results = _causal_flash_t([make_score_fn(c) for c in range(heads)], vt_ref, s_ref, p_ref, acc_ref, qi)

    lp = lam_ref[...]
    lam = (jnp.exp(jnp.sum(lp[0:1] * lp[1:2], axis=-1, keepdims=True))
           - jnp.exp(jnp.sum(lp[2:3] * lp[3:4], axis=-1, keepdims=True)) + lam_init)
    for c, (l, acc) in enumerate(results):
        o_t = acc / l
        o = (o_t[:, :tq] - lam * o_t[:, tq:]).T
        o_ref[:, _lanes(c)] = (_rms(o, subln_ref[...]) * (1.0 - lam_init)).astype(o_ref.dtype)


def _diff_attention(qkv, lam_params, subln, *, batch, seq, tq, heads, q_col, k_col, v_col, lam_init):
    nq = seq // tq
    width = heads * LANES
    assert A_HEADS % heads == 0 and q_col % heads == 0 and k_col % heads == 0 and v_col % heads == 0
    return pl.pallas_call(
        functools.partial(_diff_attn_body, tq=tq, heads=heads, lam_init=lam_init),
        grid=(batch, A_HEADS // heads, nq),
        in_specs=[pl.BlockSpec((4, A_QK_DIM), lambda b, h, i: (0, 0)),
                  pl.BlockSpec((1, LANES), lambda b, h, i: (0, 0)),
                  pl.BlockSpec((tq, width), lambda b, h, i: (b * nq + i, q_col // heads + h)),
                  pl.BlockSpec((seq, width), lambda b, h, i: (b, k_col // heads + h)),
                  pl.BlockSpec((seq, width), lambda b, h, i: (b, v_col // heads + h))],
        out_specs=pl.BlockSpec((tq, width), lambda b, h, i: (b * nq + i, h)),
        out_shape=jax.ShapeDtypeStruct((batch * seq, A_HEADS * LANES), BF16),
        scratch_shapes=[pltpu.VMEM((heads, LANES, seq), BF16)] + _flash_scratch(heads, tq, 2 * tq, LANES),
        compiler_params=_params("parallel", "parallel", "arbitrary"),
        name="diff_attn",
    )(lam_params, subln.reshape(1, LANES), qkv, qkv, qkv)


def _tile_index(ref, i, tile):
    per_slab = ref.shape[1]
    if per_slab >= tile:
        runs = per_slab // tile
        return i // runs, pl.ds(pl.multiple_of((i % runs) * tile, tile), tile)
    slabs = tile // per_slab
    return pl.ds(i * slabs, slabs), slice(None)


def _load_tile(ref, i, c, tile):
    slab, rows = _tile_index(ref, i, tile)
    return ref[slab, rows, _lanes(c)].reshape(tile, LANES)


def _store_tile(ref, i, c, tile, value):
    slab, rows = _tile_index(ref, i, tile)
    per_slab = ref.shape[1]
    ref[slab, rows, _lanes(c)] = value if per_slab >= tile else value.reshape(tile // per_slab, per_slab, LANES)


def _dilated_body(q_ref, k_ref, v_ref, o_ref, lse_ref, *, tile, heads):
    length = q_ref.shape[0] * q_ref.shape[1]
    row = lax.broadcasted_iota(jnp.int32, (tile, 1), 0)

    def attend(i, c, k, v, valid):
        s = jnp.where(valid, _dot_nt(_load_tile(q_ref, i, c, tile), k), -jnp.inf)
        m = jnp.max(s, axis=-1, keepdims=True)
        p = jnp.exp(s - m)
        l = jnp.sum(p, axis=-1, keepdims=True)
        o = jnp.dot(p.astype(BF16), v, preferred_element_type=F32) / l
        _store_tile(o_ref, i, c, tile, o)
        _store_tile(lse_ref, i, c, tile, jnp.broadcast_to(m + jnp.log(l), (tile, LANES)))

    col = lax.broadcasted_iota(jnp.int32, (1, tile), 1)
    for c in range(heads):
        attend(0, c, _load_tile(k_ref, 0, c, tile), _load_tile(v_ref, 0, c, tile), col <= row)

    col2 = lax.broadcasted_iota(jnp.int32, (1, 2 * tile), 1)
    band = jnp.logical_and(col2 >= row, col2 <= row + tile)

    def body(i, _):
        for c in range(heads):
            k = jnp.concatenate([_load_tile(k_ref, i - 1, c, tile), _load_tile(k_ref, i, c, tile)], axis=0)
            v = jnp.concatenate([_load_tile(v_ref, i - 1, c, tile), _load_tile(v_ref, i, c, tile)], axis=0)
            attend(i, c, k, v, band)
        return 0

    lax.fori_loop(1, length // tile, body, 0)


def _dilated_group(qkv, *, batch, seq, row_tile, group, dil, heads, q_col, k_col, v_col):
    hpg = B_HEADS_PER_GROUP
    tile = B_GROUPS[group][0] // dil
    slabs, per_slab = seq // row_tile, row_tile // dil
    assert tile == LANES and hpg % heads == 0 and (seq // dil) % tile == 0
    assert per_slab % tile == 0 or tile % per_slab == 0
    width = heads * LANES
    view = qkv.reshape(batch, slabs, dil, per_slab, qkv.shape[1])

    def in_spec(col):
        first = col + group * hpg
        assert first % heads == 0
        return pl.BlockSpec((None, slabs, None, per_slab, width),
                            lambda b, h, r: (b, 0, r, 0, first // heads + h))

    out_spec = pl.BlockSpec((None, slabs, None, per_slab, width), lambda b, h, r: (b, 0, r, 0, h))
    out_sds = jax.ShapeDtypeStruct((batch, slabs, dil, per_slab, hpg * LANES), F32)
    o, lse = pl.pallas_call(
        functools.partial(_dilated_body, tile=tile, heads=heads),
        grid=(batch, hpg // heads, dil),
        in_specs=[in_spec(q_col), in_spec(k_col), in_spec(v_col)],
        out_specs=[out_spec, out_spec],
        out_shape=[out_sds, out_sds],
        compiler_params=_params("parallel", "parallel", "parallel"),
        name=f"dilated_g{group}",
    )(view, view, view)
    shape = (batch * slabs, dil, per_slab, hpg * LANES)
    return o.reshape(shape), lse.reshape(shape)


def _merge_body(o0_ref, o1_ref, o2_ref, l0_ref, l1_ref, l2_ref, out_ref, nat_ref):
    def natural(ref, slot, c):
        dil, rows = ref.shape[0], ref.shape[1]
        if dil == 1:
            return ref[0, :, _lanes(c)]
        for r in range(dil):
            nat_ref[slot, pl.ds(r, rows, stride=dil), :] = ref[r, :, _lanes(c)]
        return nat_ref[slot]

    for c in range(out_ref.shape[1] // LANES):
        o0, o1, o2 = natural(o0_ref, None, c), natural(o1_ref, 0, c), natural(o2_ref, 1, c)
        l0, l1, l2 = natural(l0_ref, None, c), natural(l1_ref, 2, c), natural(l2_ref, 3, c)
        m = jnp.maximum(jnp.maximum(l0, l1), l2)
        w0, w1, w2 = jnp.exp(l0 - m), jnp.exp(l1 - m), jnp.exp(l2 - m)
        out_ref[:, _lanes(c)] = ((w0 * o0 + w1 * o1 + w2 * o2) / (w0 + w1 + w2)).astype(out_ref.dtype)


def _merge_groups(outs, lses):
    tiles, _, _, width = outs[0].shape
    row_tile = outs[0].shape[1] * outs[0].shape[2]
    specs = [pl.BlockSpec((None,) + a.shape[1:], lambda i: (i, 0, 0, 0)) for a in (*outs, *lses)]
    return pl.pallas_call(
        _merge_body,
        grid=(tiles,),
        in_specs=specs,
        out_specs=pl.BlockSpec((row_tile, width), lambda i: (i, 0)),
        out_shape=jax.ShapeDtypeStruct((tiles * row_tile, width), BF16),
        scratch_shapes=[pltpu.VMEM((4, row_tile, LANES), F32)],
        compiler_params=_params("parallel"),
        name="dilated_merge",
    )(*outs, *lses)


def _mla_body(q_ref, kv_ref, kr_ref, o_ref, vt_ref, s_ref, p_ref, acc_ref, *, tq, heads):
    qi = pl.program_id(2)

    @pl.when(qi == 0)
    def _():
        _fill_transposed(vt_ref, kv_ref, [(2 * c + 1) * LANES for c in range(heads)], tq)

    def make_score_fn(c):
        q_t = q_ref[:, _lanes(c, 2 * LANES)].T

        def score(start):
            k = jnp.concatenate([kv_ref[pl.ds(start, tq), _lanes(2 * c)], kr_ref[pl.ds(start, tq), :]], axis=1)
            return jnp.dot(k, q_t, preferred_element_type=F32)

        return score

    results = _causal_flash_t([make_score_fn(c) for c in range(heads)], vt_ref, s_ref, p_ref, acc_ref, qi)
    for c, (l, acc) in enumerate(results):
        o_ref[:, _lanes(c)] = (acc / l).T.astype(o_ref.dtype)


def _mla_attention(q, kv, k_rope_src, *, batch, seq, tq, heads, k_rope_col):
    nq = seq // tq
    assert C_HEADS % heads == 0
    return pl.pallas_call(
        functools.partial(_mla_body, tq=tq, heads=heads),
        grid=(batch, C_HEADS // heads, nq),
        in_specs=[pl.BlockSpec((tq, heads * 2 * LANES), lambda b, h, i: (b * nq + i, h)),
                  pl.BlockSpec((seq, heads * 2 * LANES), lambda b, h, i: (b, h)),
                  pl.BlockSpec((seq, LANES), lambda b, h, i: (b, k_rope_col))],
        out_specs=pl.BlockSpec((tq, heads * LANES), lambda b, h, i: (b * nq + i, h)),
        out_shape=jax.ShapeDtypeStruct((batch * seq, C_HEADS * LANES), BF16),
        scratch_shapes=[pltpu.VMEM((heads, LANES, seq), BF16)] + _flash_scratch(heads, tq, tq, LANES),
        compiler_params=_params("parallel", "parallel", "arbitrary"),
        name="mla_attn",
    )(q, kv, k_rope_src)


def _moba_body(q_ref, k_ref, v_ref, o_ref, kmean_ref, vt_ref, sel_ref, s_ref, p_ref, acc_ref, *,
               n_blk, heads):
    qi = pl.program_id(2)
    blk = MOBA_BLOCK

    @pl.when(qi == 0)
    def _():
        for c in range(heads):
            k_all = k_ref[:, _lanes(c)].astype(F32).reshape(n_blk, blk, LANES)
            kmean_ref[c] = jnp.mean(k_all, axis=1)
        _fill_transposed(vt_ref, v_ref, [c * LANES for c in range(heads)], blk)

    blk_id = lax.broadcasted_iota(jnp.int32, (n_blk, 1), 0)
    past = blk_id < qi

    def make_score_fn(c):
        q_t = q_ref[:, _lanes(c)].T
        km = kmean_ref[c]
        km_hi = km.astype(BF16)
        rem = km - km_hi.astype(F32)
        km_mid = rem.astype(BF16)
        km_lo = (rem - km_mid.astype(F32)).astype(BF16)
        gate = (jnp.dot(km_hi, q_t, preferred_element_type=F32) + jnp.dot(km_mid, q_t, preferred_element_type=F32)
                + jnp.dot(km_lo, q_t, preferred_element_type=F32))
        gate = jnp.where(past, gate, -jnp.inf)
        selected = jnp.zeros(gate.shape, F32)
        for _ in range(min(MOBA_TOPK, n_blk - 1)):
            best = jnp.max(gate, axis=0, keepdims=True)
            first = jnp.min(jnp.where(gate == best, blk_id, n_blk), axis=0, keepdims=True)
            pick = blk_id == first
            selected = jnp.where(jnp.logical_and(pick, past), 1.0, selected)
            gate = jnp.where(pick, -jnp.inf, gate)
        sel_ref[c] = selected
        return lambda start: jnp.dot(k_ref[pl.ds(start, blk), _lanes(c)], q_t, preferred_element_type=F32)

    score_fns = [make_score_fn(c) for c in range(heads)]
    chosen_fns = [lambda n, c=c: jnp.logical_or(sel_ref[c, pl.ds(n, 1), :] > 0.0, n >= qi)
                  for c in range(heads)]
    results = _causal_flash_t(score_fns, vt_ref, s_ref, p_ref, acc_ref, qi, past_mask_fns=chosen_fns)
    for c, (l, acc) in enumerate(results):
        o_ref[:, _lanes(c)] = (acc / l).T.astype(o_ref.dtype)


def _moba_attention(qkv, *, batch, seq, heads, q_col, k_col, v_col):
    blk = MOBA_BLOCK
    n_blk = seq // blk
    width = heads * LANES
    assert seq % blk == 0 and D_HEADS % heads == 0
    assert q_col % heads == 0 and k_col % heads == 0 and v_col % heads == 0
    return pl.pallas_call(
        functools.partial(_moba_body, n_blk=n_blk, heads=heads),
        grid=(batch, D_HEADS // heads, n_blk),
        in_specs=[pl.BlockSpec((blk, width), lambda b, h, i: (b * n_blk + i, q_col // heads + h)),
                  pl.BlockSpec((seq, width), lambda b, h, i: (b, k_col // heads + h)),
                  pl.BlockSpec((seq, width), lambda b, h, i: (b, v_col // heads + h))],
        out_specs=pl.BlockSpec((blk, width), lambda b, h, i: (b * n_blk + i, h)),
        out_shape=jax.ShapeDtypeStruct((batch * seq, D_HEADS * LANES), BF16),
        scratch_shapes=[pltpu.VMEM((heads, n_blk, LANES), F32),
                        pltpu.VMEM((heads, LANES, seq), BF16),
                        pltpu.VMEM((heads, n_blk, blk), F32)] + _flash_scratch(heads, blk, blk, LANES),
        compiler_params=_params("parallel", "parallel", "arbitrary"),
        name="moba_attn",
    )(qkv, qkv, qkv)


def _rope_tables(seq):
    pos = jnp.arange(seq, dtype=F32)[:, None]

    def angles(dim):
        inv = 1.0 / (ROPE_THETA ** (jnp.arange(0, dim, 2, dtype=F32) / dim))
        return pos * inv[None, :]

    a128 = angles(HEAD_DIM)
    a64 = angles(A_QK_DIM)
    c128 = jnp.concatenate([jnp.cos(a128)] * 2, axis=1)
    s128 = jnp.concatenate([-jnp.sin(a128), jnp.sin(a128)], axis=1)
    c64 = jnp.concatenate([jnp.cos(a64)] * 4, axis=1)
    s64 = jnp.concatenate([-jnp.sin(a64)] * 2 + [jnp.sin(a64)] * 2, axis=1)
    return c128, s128, c64, s64


def _interleave_diff_heads(w):
    d = w.shape[0]
    return w.reshape(d, A_HEADS, 2, 2, 32).transpose(0, 1, 3, 2, 4).reshape(d, A_HEADS * LANES)


def _spread_rope64(w):
    z = jnp.zeros(w.shape[:-1] + (32,), w.dtype)
    return jnp.concatenate([w[..., :32], z, w[..., 32:], z], axis=-1)


def _ab_mixer(x, gain, w_in, lam_params, subln, w_out, layer_idx, rope, *, batch, seq):
    aw = A_HEADS * LANES
    w = jnp.concatenate([_interleave_diff_heads(w_in[:, :aw]), _interleave_diff_heads(w_in[:, aw:2 * aw]),
                         w_in[:, 2 * aw:]], axis=1).astype(BF16)
    a_scale = A_QK_DIM ** -0.5
    b_scale = HEAD_DIM ** -0.5
    dils = [dil for _, dil in B_GROUPS for _ in range(B_HEADS_PER_GROUP)]
    modes = ([(ROPE64, a_scale, 1)] * A_HEADS + [(ROPE64, 1.0, 1)] * A_HEADS + [(PLAIN, 1.0, 1)] * A_HEADS
             + [(ROPE128, b_scale, d) for d in dils] + [(ROPE128, 1.0, d) for d in dils]
             + [(PLAIN, 1.0, d) for d in dils])
    row_tile = 512
    qkv = _norm_proj(x, gain, w, rope, modes, seq=seq, tm=row_tile, tn=512, out_dtype=BF16)
    lam_init = 0.8 - 0.6 * math.exp(-0.3 * layer_idx)
    oa = _diff_attention(qkv, lam_params, subln, batch=batch, seq=seq, tq=256, heads=2,
                         q_col=0, k_col=A_HEADS, v_col=2 * A_HEADS, lam_init=lam_init)
    b0 = 3 * A_HEADS
    outs, lses = [], []
    for g, (_, dil) in enumerate(B_GROUPS):
        o, lse = _dilated_group(qkv, batch=batch, seq=seq, row_tile=row_tile, group=g, dil=dil,
                                heads=2 if dil == 1 else 4,
                                q_col=b0, k_col=b0 + B_HEADS, v_col=b0 + 2 * B_HEADS)
        outs.append(o)
        lses.append(lse)
    ob = _merge_groups(outs, lses)
    o = jnp.concatenate([oa, ob], axis=1)
    return _out_proj(o, w_out.astype(BF16), x, tm=512, tn=1024)


def _cd_mixer(x, gain, w_in, q_norm, w_uq, kv_norm, w_ukv, w_out, rope, *, batch, seq):
    lat = C_Q_LORA + C_KV_LORA
    d_scale = HEAD_DIM ** -0.5
    latent = _norm_proj(x, gain, w_in[:, :lat].astype(BF16), rope, [(PLAIN, 1.0, 1)] * (lat // LANES),
                        seq=seq, tm=512, tn=lat // 2, out_dtype=F32)
    w_main = jnp.concatenate([w_in[:, lat + C_ROPE:], _spread_rope64(w_in[:, lat:lat + C_ROPE])],
                             axis=1).astype(BF16)
    modes = ([(ROPE128, d_scale, 1)] * D_HEADS + [(ROPE128, 1.0, 1)] * D_HEADS + [(PLAIN, 1.0, 1)] * D_HEADS
             + [(ROPE64, 1.0, 1)])
    main = _norm_proj(x, gain, w_main, rope, modes, seq=seq, tm=512, tn=5 * LANES, out_dtype=BF16)

    c_scale = (C_NOPE + C_ROPE) ** -0.5
    w_q = w_uq.reshape(C_Q_LORA, C_HEADS, C_NOPE + C_ROPE)
    w_q = jnp.concatenate([w_q[..., :C_NOPE], _spread_rope64(w_q[..., C_NOPE:])], axis=-1)
    w_q = w_q.reshape(C_Q_LORA, C_HEADS * 2 * LANES).astype(BF16)
    q_modes = [(PLAIN, c_scale, 1), (ROPE64, c_scale, 1)] * C_HEADS
    qc = _norm_proj(latent[:, :C_Q_LORA], q_norm, w_q, rope, q_modes, seq=seq, tm=512, tn=512, out_dtype=BF16)
    kv = _norm_proj(latent[:, C_Q_LORA:], kv_norm, w_ukv.astype(BF16), rope,
                    [(PLAIN, 1.0, 1)] * (2 * C_HEADS), seq=seq, tm=512, tn=512, out_dtype=BF16)
    oc = _mla_attention(qc, kv, main, batch=batch, seq=seq, tq=256, heads=4, k_rope_col=3 * D_HEADS)
    od = _moba_attention(main, batch=batch, seq=seq, heads=4, q_col=0, k_col=D_HEADS, v_col=2 * D_HEADS)
    o = jnp.concatenate([oc, od], axis=1)
    return _out_proj(o, w_out.astype(BF16), x, tm=512, tn=1024)


def kernel(x, p, ffn_norm, ffn_w_gu, ffn_w_down, mix_norm, ab_w_in, ab_lambda, ab_subln, ab_w_out,
           cd_w_in, cd_q_norm, cd_w_uq, cd_kv_norm, cd_w_ukv, cd_w_out, ple_norm, ple_w_gate,
           ple_w_proj, final_norm):
    batch, seq, d = x.shape
    depth = p.shape[0]
    rope = _rope_tables(seq)
    x = x.reshape(batch * seq, d)
    for i in range(depth):
        j = i // 2
        x = _ffn(x, ffn_norm[i, 0], ffn_w_gu[i, 0].astype(BF16), ffn_w_down[i, 0].astype(BF16), tm=512, tf=512)
        if i % 2 == 0:
            x = _ab_mixer(x, mix_norm[i], ab_w_in[j], ab_lambda[j], ab_subln[j], ab_w_out[j], i, rope,
                          batch=batch, seq=seq)
        else:
            x = _cd_mixer(x, mix_norm[i], cd_w_in[j], cd_q_norm[j], cd_w_uq[j], cd_kv_norm[j], cd_w_ukv[j],
                          cd_w_out[j], rope, batch=batch, seq=seq)
        x = _ffn(x, ffn_norm[i, 1], ffn_w_gu[i, 1].astype(BF16), ffn_w_down[i, 1].astype(BF16), tm=512, tf=512)
        x = _ple(x, p[i].reshape(batch * seq, -1), ple_norm[i], ple_w_gate[i].astype(BF16),
                 ple_w_proj[i].astype(BF16), final_norm, tm=256, final_norm=(i == depth - 1))
    return x.reshape(batch, seq, d)
```

```python
import functools
import math

import jax
import jax.numpy as jnp
from jax import lax
from jax.experimental import pallas as pl
from jax.experimental.pallas import tpu as pltpu

F32 = jnp.float32
BF16 = jnp.bfloat16

D_MODEL = 2048
D_FF = 5632
PLE_DIM = 256
HEAD_DIM = 128
ROPE_THETA = 10000.0
NORM_EPS = 1e-6

A_HEADS = 8
A_QK_DIM = 64
B_GROUPS = ((128, 1), (512, 4), (2048, 16))
B_HEADS_PER_GROUP = 4
B_HEADS = B_HEADS_PER_GROUP * len(B_GROUPS)
C_HEADS = 8
C_Q_LORA = 512
C_KV_LORA = 256
C_NOPE = 128
C_ROPE = 64
D_HEADS = 8
MOBA_BLOCK = 256
MOBA_TOPK = 3

LOG2_E = 1.0 / math.log(2.0)
LANES = 128
VMEM_LIMIT = 56 * 1024 * 1024

PLAIN, ROPE128, ROPE64 = 0, 1, 2


def _rms(x, gain):
    return x * lax.rsqrt(jnp.mean(x * x, axis=-1, keepdims=True) + NORM_EPS) * gain


def _params(*semantics):
    return pltpu.CompilerParams(dimension_semantics=semantics, vmem_limit_bytes=VMEM_LIMIT)


def _norm_proj_body(x_ref, g_ref, w_ref, c128_ref, s128_ref, c64_ref, s64_ref, o_ref, xn_ref, stage_ref, *,
                    tile_patterns):
    j = pl.program_id(1)
    tm = o_ref.shape[0]

    @pl.when(j == 0)
    def _():
        xn_ref[...] = _rms(x_ref[...], g_ref[...]).astype(BF16)

    def tile(pattern):
        acc = jnp.dot(xn_ref[...], w_ref[...], preferred_element_type=F32)
        for c, (kind, scale, dil) in enumerate(pattern):
            a = acc[:, _lanes(c)]
            if scale != 1.0:
                a = a * scale
            if kind == ROPE128:
                a = a * c128_ref[...] + pltpu.roll(a, 64, 1) * s128_ref[...]
            elif kind == ROPE64:
                a = a * c64_ref[...] + pltpu.roll(a, 64, 1) * s64_ref[...]
            if dil == 1:
                o_ref[:, _lanes(c)] = a.astype(o_ref.dtype)
            else:
                rows = tm // dil
                stage_ref[c] = a
                for r in range(dil):
                    o_ref[r * rows:(r + 1) * rows, _lanes(c)] = (
                        stage_ref[c, pl.ds(r, rows, stride=dil), :].astype(o_ref.dtype))

    distinct = sorted(set(tile_patterns), key=tile_patterns.index)
    if len(distinct) == 1:
        tile(distinct[0])
    else:
        for pattern in distinct:
            tiles = [t for t, p in enumerate(tile_patterns) if p == pattern]
            cond = functools.reduce(jnp.logical_or, [j == t for t in tiles])
            pl.when(cond)(functools.partial(tile, pattern))


def _norm_proj(x, gain, w, rope, chunk_modes, *, seq, tm, tn, out_dtype):
    t, k = x.shape
    n = w.shape[1]
    per_tile = tn // LANES
    assert t % tm == 0 and n % tn == 0 and seq % tm == 0 and len(chunk_modes) * LANES == n
    tile_patterns = tuple(tuple(chunk_modes[a * per_tile:(a + 1) * per_tile]) for a in range(n // tn))
    seq_tiles = seq // tm
    tab = pl.BlockSpec((tm, LANES), lambda i, j: (i % seq_tiles, 0))
    return pl.pallas_call(
        functools.partial(_norm_proj_body, tile_patterns=tile_patterns),
        grid=(t // tm, n // tn),
        in_specs=[pl.BlockSpec((tm, k), lambda i, j: (i, 0)),
                  pl.BlockSpec((1, k), lambda i, j: (0, 0)),
                  pl.BlockSpec((k, tn), lambda i, j: (0, j)),
                  tab, tab, tab, tab],
        out_specs=pl.BlockSpec((tm, tn), lambda i, j: (i, j)),
        out_shape=jax.ShapeDtypeStruct((t, n), out_dtype),
        scratch_shapes=[pltpu.VMEM((tm, k), BF16), pltpu.VMEM((per_tile, tm, LANES), F32)],
        compiler_params=_params("parallel", "arbitrary"),
        name="norm_proj",
    )(x, gain.reshape(1, k), w, *rope)


def _out_proj_body(a_ref, w_ref, res_ref, o_ref):
    o_ref[...] = res_ref[...] + jnp.dot(a_ref[...], w_ref[...], preferred_element_type=F32)


def _out_proj(a, w, res, *, tm, tn):
    t, k = a.shape
    n = w.shape[1]
    assert t % tm == 0 and n % tn == 0
    return pl.pallas_call(
        _out_proj_body,
        grid=(t // tm, n // tn),
        in_specs=[pl.BlockSpec((tm, k), lambda i, j: (i, 0)),
                  pl.BlockSpec((k, tn), lambda i, j: (0, j)),
                  pl.BlockSpec((tm, tn), lambda i, j: (i, j))],
        out_specs=pl.BlockSpec((tm, tn), lambda i, j: (i, j)),
        out_shape=jax.ShapeDtypeStruct((t, n), F32),
        compiler_params=_params("parallel", "parallel"),
        name="out_proj",
    )(a, w, res)


def _ffn_body(x_ref, g_ref, wg_ref, wu_ref, wd_ref, o_ref, xn_ref, acc_ref):
    j = pl.program_id(1)

    @pl.when(j == 0)
    def _():
        xn_ref[...] = _rms(x_ref[...], g_ref[...]).astype(BF16)
        acc_ref[...] = jnp.zeros_like(acc_ref)

    xn = xn_ref[...]
    g = jnp.dot(xn, wg_ref[...], preferred_element_type=F32)
    u = jnp.dot(xn, wu_ref[...], preferred_element_type=F32)
    h = (g * jax.nn.sigmoid(g) * u).astype(BF16)
    acc_ref[...] += jnp.dot(h, wd_ref[...], preferred_element_type=F32)

    @pl.when(j == pl.num_programs(1) - 1)
    def _():
        o_ref[...] = x_ref[...] + 0.5 * acc_ref[...]


def _ffn(x, gain, w_gu, w_down, *, tm, tf):
    t, d = x.shape
    d_ff = w_down.shape[0]
    assert t % tm == 0 and d_ff % tf == 0
    nf = d_ff // tf
    return pl.pallas_call(
        _ffn_body,
        grid=(t // tm, nf),
        in_specs=[pl.BlockSpec((tm, d), lambda i, j: (i, 0)),
                  pl.BlockSpec((1, d), lambda i, j: (0, 0)),
                  pl.BlockSpec((d, tf), lambda i, j: (0, j)),
                  pl.BlockSpec((d, tf), lambda i, j: (0, nf + j)),
                  pl.BlockSpec((tf, d), lambda i, j: (j, 0))],
        out_specs=pl.BlockSpec((tm, d), lambda i, j: (i, 0)),
        out_shape=jax.ShapeDtypeStruct((t, d), F32),
        scratch_shapes=[pltpu.VMEM((tm, d), BF16), pltpu.VMEM((tm, d), F32)],
        compiler_params=_params("parallel", "arbitrary"),
        name="ffn",
    )(x, gain.reshape(1, d), w_gu, w_gu, w_down)


def _ple_body(x_ref, p_ref, g_ref, wg_ref, wp_ref, fg_ref, o_ref, *, final_norm):
    x = x_ref[...]
    xn = _rms(x, g_ref[...]).astype(BF16)
    gate = jax.nn.sigmoid(jnp.dot(xn, wg_ref[...], preferred_element_type=F32))
    proj = jnp.dot(p_ref[...].astype(BF16), wp_ref[...], preferred_element_type=F32)
    y = x + gate * proj
    if final_norm:
        y = _rms(y, fg_ref[...])
    o_ref[...] = y


def _ple(x, p, gain, w_gate, w_proj, final_gain, *, tm, final_norm):
    t, d = x.shape
    pd = p.shape[1]
    assert t % tm == 0
    const = lambda i: (0, 0)
    return pl.pallas_call(
        functools.partial(_ple_body, final_norm=final_norm),
        grid=(t // tm,),
        in_specs=[pl.BlockSpec((tm, d), lambda i: (i, 0)),
                  pl.BlockSpec((tm, pd), lambda i: (i, 0)),
                  pl.BlockSpec((1, d), const),
                  pl.BlockSpec((d, d), const),
                  pl.BlockSpec((pd, d), const),
                  pl.BlockSpec((1, d), const)],
        out_specs=pl.BlockSpec((tm, d), lambda i: (i, 0)),
        out_shape=jax.ShapeDtypeStruct((t, d), F32),
        compiler_params=_params("parallel"),
        name="ple",
    )(x, p, gain.reshape(1, d), w_gate, w_proj, final_gain.reshape(1, d))


def _dot_nt(a, b):
    return lax.dot_general(a, b, (((1,), (1,)), ((), ())), preferred_element_type=F32)


def _lanes(c, width=LANES):
    return slice(c * width, (c + 1) * width)


def _fill_transposed(vt_ref, v_ref, lane_offsets, chunk):
    dv = vt_ref.shape[1]

    def body(j, _):
        start = pl.multiple_of(j * chunk, chunk)
        for c, off in enumerate(lane_offsets):
            vt_ref[c, :, pl.ds(start, chunk)] = v_ref[pl.ds(start, chunk), off:off + dv].T
        return 0

    lax.fori_loop(0, v_ref.shape[0] // chunk, body, 0)


def _flash_scratch(heads, tk, cols, dv):
    return [pltpu.VMEM((heads, tk, cols), F32), pltpu.VMEM((heads, tk, cols), BF16),
            pltpu.VMEM((heads, dv, cols), F32)]


def _causal_flash_t(score_fns, vt_ref, s_ref, p_ref, acc_ref, qi, past_mask_fns=None):
    heads = range(len(score_fns))
    _, tk, cols = s_ref.shape

    def stage_scores(block):
        start = pl.multiple_of(block * tk, tk)
        for c in heads:
            s = score_fns[c](start)
            if past_mask_fns is not None:
                s = jnp.where(past_mask_fns[c](block), s, -jnp.inf)
            s_ref[c] = s

    def accumulate(block, alphas):
        start = pl.multiple_of(block * tk, tk)
        for c in heads:
            pv = jnp.dot(vt_ref[c, :, pl.ds(start, tk)], p_ref[c], preferred_element_type=F32)
            acc_ref[c] = alphas[c] * acc_ref[c] + pv

    def softmax(stats, visible=None):
        new_stats, alphas = [], []
        for c in heads:
            m, l = stats[c]
            read = (lambda: s_ref[c]) if visible is None else (lambda: jnp.where(visible, s_ref[c], -jnp.inf))
            m_new = jnp.maximum(m, jnp.max(read(), axis=0, keepdims=True))
            alpha = jnp.exp2(m - m_new)
            p = jnp.exp2(read() - m_new)
            p_ref[c] = p.astype(BF16)
            new_stats.append((m_new, alpha * l + jnp.sum(p, axis=0, keepdims=True)))
            alphas.append(alpha)
        return tuple(new_stats), tuple(alphas)

    def past_block(j, carry):
        stats, alphas = carry
        accumulate(jnp.maximum(j - 1, 0), alphas)
        stats, alphas = softmax(stats)
        stage_scores(j + 1)
        return stats, alphas

    for c in heads:
        p_ref[c] = jnp.zeros(p_ref.shape[1:], BF16)
        acc_ref[c] = jnp.zeros(acc_ref.shape[1:], F32)
    stage_scores(0)
    init = (tuple((jnp.full((1, cols), jnp.finfo(F32).min, F32), jnp.zeros((1, cols), F32)) for _ in heads),
            tuple(jnp.ones((1, cols), F32) for _ in heads))
    stats, alphas = lax.fori_loop(0, qi, past_block, init)

    accumulate(jnp.maximum(qi - 1, 0), alphas)
    key_row = lax.broadcasted_iota(jnp.int32, (tk, 1), 0)
    query_col = lax.broadcasted_iota(jnp.int32, (1, cols), 1) % tk
    stats, alphas = softmax(stats, visible=key_row <= query_col)
    accumulate(qi, alphas)
    return tuple((stats[c][1], acc_ref[c]) for c in heads)


def _diff_attn_body(lam_ref, subln_ref, q_ref, k_ref, v_ref, o_ref, vt_ref, s_ref, p_ref, acc_ref, *,
                    tq, heads, lam_init):
    qi = pl.program_id(2)

    @pl.when(qi == 0)
    def _():
        _fill_transposed(vt_ref, v_ref, [c * LANES for c in range(heads)], tq)

    is_map0 = (lax.broadcasted_iota(jnp.int32, (LANES, 1), 0) % 64) < 32

    def make_score_fn(c):
        q_t = q_ref[:, _lanes(c)].T
        zero = jnp.zeros_like(q_t)
        qq_t = jnp.concatenate([jnp.where(is_map0, q_t, zero), jnp.where(is_map0, zero, q_t)], axis=1)
        return lambda start: jnp.dot(k_ref[pl.ds(start, tq), _lanes(c)], qq_t, preferred_element_type=F32)

    results = _causal_flash_t([make_score_fn(c) for c in range(heads)], vt_ref, s_ref, p_ref, acc_ref, qi)

    lp = lam_ref[...]
    lam = (jnp.exp(jnp.sum(lp[0:1] * lp[1:2], axis=-1, keepdims=True))
           - jnp.exp(jnp.sum(lp[2:3] * lp[3:4], axis=-1, keepdims=True)) + lam_init)
    for c, (l, acc) in enumerate(results):
        o_t = acc / l
        o = (o_t[:, :tq] - lam * o_t[:, tq:]).T
        o_ref[:, _lanes(c)] = (_rms(o, subln_ref[...]) * (1.0 - lam_init)).astype(o_ref.dtype)


def _diff_attention(qkv, lam_params, subln, *, batch, seq, tq, heads, q_col, k_col, v_col, lam_init):
    nq = seq // tq
    width = heads * LANES
    assert A_HEADS % heads == 0 and q_col % heads == 0 and k_col % heads == 0 and v_col % heads == 0
    return pl.pallas_call(
        functools.partial(_diff_attn_body, tq=tq, heads=heads, lam_init=lam_init),
        grid=(batch, A_HEADS // heads, nq),
        in_specs=[pl.BlockSpec((4, A_QK_DIM), lambda b, h, i: (0, 0)),
                  pl.BlockSpec((1, LANES), lambda b, h, i: (0, 0)),
                  pl.BlockSpec((tq, width), lambda b, h, i: (b * nq + i, q_col // heads + h)),
                  pl.BlockSpec((seq, width), lambda b, h, i: (b, k_col // heads + h)),
                  pl.BlockSpec((seq, width), lambda b, h, i: (b, v_col // heads + h))],
        out_specs=pl.BlockSpec((tq, width), lambda b, h, i: (b * nq + i, h)),
        out_shape=jax.ShapeDtypeStruct((batch * seq, A_HEADS * LANES), BF16),
        scratch_shapes=[pltpu.VMEM((heads, LANES, seq), BF16)] + _flash_scratch(heads, tq, 2 * tq, LANES),
        compiler_params=_params("parallel", "parallel", "arbitrary"),
        name="diff_attn",
    )(lam_params, subln.reshape(1, LANES), qkv, qkv, qkv)


def _tile_index(ref, i, tile):
    per_slab = ref.shape[1]
    if per_slab >= tile:
        runs = per_slab // tile
        return i // runs, pl.ds(pl.multiple_of((i % runs) * tile, tile), tile)
    slabs = tile // per_slab
    return pl.ds(i * slabs, slabs), slice(None)


def _load_tile(ref, i, c, tile):
    slab, rows = _tile_index(ref, i, tile)
    return ref[slab, rows, _lanes(c)].reshape(tile, LANES)


def _store_tile(ref, i, c, tile, value):
    slab, rows = _tile_index(ref, i, tile)
    per_slab = ref.shape[1]
    ref[slab, rows, _lanes(c)] = value if per_slab >= tile else value.reshape(tile // per_slab, per_slab, LANES)


def _dilated_body(q_ref, k_ref, v_ref, o_ref, lse_ref, *, tile, heads):
    length = q_ref.shape[0] * q_ref.shape[1]
    row = lax.broadcasted_iota(jnp.int32, (tile, 1), 0)

    def attend(i, c, k, v, valid):
        s = jnp.where(valid, _dot_nt(_load_tile(q_ref, i, c, tile), k), -jnp.inf)
        m = jnp.max(s, axis=-1, keepdims=True)
        p = jnp.exp(s - m)
        l = jnp.sum(p, axis=-1, keepdims=True)
        o = jnp.dot(p.astype(BF16), v, preferred_element_type=F32) / l
        _store_tile(o_ref, i, c, tile, o)
        _store_tile(lse_ref, i, c, tile, jnp.broadcast_to(m + jnp.log(l), (tile, LANES)))

    col = lax.broadcasted_iota(jnp.int32, (1, tile), 1)
    for c in range(heads):
        attend(0, c, _load_tile(k_ref, 0, c, tile), _load_tile(v_ref, 0, c, tile), col <= row)

    col2 = lax.broadcasted_iota(jnp.int32, (1, 2 * tile), 1)
    band = jnp.logical_and(col2 >= row, col2 <= row + tile)

    def body(i, _):
        for c in range(heads):
            k = jnp.concatenate([_load_tile(k_ref, i - 1, c, tile), _load_tile(k_ref, i, c, tile)], axis=0)
            v = jnp.concatenate([_load_tile(v_ref, i - 1, c, tile), _load_tile(v_ref, i, c, tile)], axis=0)
            attend(i, c, k, v, band)
        return 0

    lax.fori_loop(1, length // tile, body, 0)


def _dilated_group(qkv, *, batch, seq, row_tile, group, dil, heads, q_col, k_col, v_col):
    hpg = B_HEADS_PER_GROUP
    tile = B_GROUPS[group][0] // dil
    slabs, per_slab = seq // row_tile, row_tile // dil
    assert tile == LANES and hpg % heads == 0 and (seq // dil) % tile == 0
    assert per_slab % tile == 0 or tile % per_slab == 0
    width = heads * LANES
    view = qkv.reshape(batch, slabs, dil, per_slab, qkv.shape[1])

    def in_spec(col):
        first = col + group * hpg
        assert first % heads == 0
        return pl.BlockSpec((None, slabs, None, per_slab, width),
                            lambda b, h, r: (b, 0, r, 0, first // heads + h))

    out_spec = pl.BlockSpec((None, slabs, None, per_slab, width), lambda b, h, r: (b, 0, r, 0, h))
    out_sds = jax.ShapeDtypeStruct((batch, slabs, dil, per_slab, hpg * LANES), F32)
    o, lse = pl.pallas_call(
        functools.partial(_dilated_body, tile=tile, heads=heads),
        grid=(batch, hpg // heads, dil),
        in_specs=[in_spec(q_col), in_spec(k_col), in_spec(v_col)],
        out_specs=[out_spec, out_spec],
        out_shape=[out_sds, out_sds],
        compiler_params=_params("parallel", "parallel", "parallel"),
        name=f"dilated_g{group}",
    )(view, view, view)
    shape = (batch * slabs, dil, per_slab, hpg * LANES)
    return o.reshape(shape), lse.reshape(shape)


def _merge_body(o0_ref, o1_ref, o2_ref, l0_ref, l1_ref, l2_ref, out_ref, nat_ref):
    def natural(ref, slot, c):
        dil, rows = ref.shape[0], ref.shape[1]
        if dil == 1:
            return ref[0, :, _lanes(c)]
        for r in range(dil):
            nat_ref[slot, pl.ds(r, rows, stride=dil), :] = ref[r, :, _lanes(c)]
        return nat_ref[slot]

    for c in range(out_ref.shape[1] // LANES):
        o0, o1, o2 = natural(o0_ref, None, c), natural(o1_ref, 0, c), natural(o2_ref, 1, c)
        l0, l1, l2 = natural(l0_ref, None, c), natural(l1_ref, 2, c), natural(l2_ref, 3, c)
        m = jnp.maximum(jnp.maximum(l0, l1), l2)
        w0, w1, w2 = jnp.exp(l0 - m), jnp.exp(l1 - m), jnp.exp(l2 - m)
        out_ref[:, _lanes(c)] = ((w0 * o0 + w1 * o1 + w2 * o2) / (w0 + w1 + w2)).astype(out_ref.dtype)


def _merge_groups(outs, lses):
    tiles, _, _, width = outs[0].shape
    row_tile = outs[0].shape[1] * outs[0].shape[2]
    specs = [pl.BlockSpec((None,) + a.shape[1:], lambda i: (i, 0, 0, 0)) for a in (*outs, *lses)]
    return pl.pallas_call(
        _merge_body,
        grid=(tiles,),
        in_specs=specs,
        out_specs=pl.BlockSpec((row_tile, width), lambda i: (i, 0)),
        out_shape=jax.ShapeDtypeStruct((tiles * row_tile, width), BF16),
        scratch_shapes=[pltpu.VMEM((4, row_tile, LANES), F32)],
        compiler_params=_params("parallel"),
        name="dilated_merge",
    )(*outs, *lses)


def _mla_body(q_ref, kv_ref, kr_ref, o_ref, vt_ref, qt_ref, s_ref, p_ref, acc_ref, *, tq, heads):
    qi = pl.program_id(2)

    @pl.when(qi == 0)
    def _():
        _fill_transposed(vt_ref, kv_ref, [(2 * c + 1) * LANES for c in range(heads)], tq)

    def make_score_fn(c):
        qt_ref[c] = q_ref[:, _lanes(c, 2 * LANES)].T

        def score(start):
            k = jnp.concatenate([kv_ref[pl.ds(start, tq), _lanes(2 * c)], kr_ref[pl.ds(start, tq), :]], axis=1)
            return jnp.dot(k, qt_ref[c], preferred_element_type=F32)

        return score

    results = _causal_flash_t([make_score_fn(c) for c in range(heads)], vt_ref, s_ref, p_ref, acc_ref, qi)
    for c, (l, acc) in enumerate(results):
        o_ref[:, _lanes(c)] = (acc / l).T.astype(o_ref.dtype)


def _mla_attention(q, kv, k_rope_src, *, batch, seq, tq, heads, k_rope_col):
    nq = seq // tq
    assert C_HEADS % heads == 0
    return pl.pallas_call(
        functools.partial(_mla_body, tq=tq, heads=heads),
        grid=(batch, C_HEADS // heads, nq),
        in_specs=[pl.BlockSpec((tq, heads * 2 * LANES), lambda b, h, i: (b * nq + i, h)),
                  pl.BlockSpec((seq, heads * 2 * LANES), lambda b, h, i: (b, h)),
                  pl.BlockSpec((seq, LANES), lambda b, h, i: (b, k_rope_col))],
        out_specs=pl.BlockSpec((tq, heads * LANES), lambda b, h, i: (b * nq + i, h)),
        out_shape=jax.ShapeDtypeStruct((batch * seq, C_HEADS * LANES), BF16),
        scratch_shapes=[pltpu.VMEM((heads, LANES, seq), BF16), pltpu.VMEM((heads, 2 * LANES, tq), BF16)]
        + _flash_scratch(heads, tq, tq, LANES),
        compiler_params=_params("parallel", "parallel", "arbitrary"),
        name="mla_attn",
    )(q, kv, k_rope_src)


def _moba_body(q_ref, k_ref, v_ref, o_ref, kmean_ref, vt_ref, sel_ref, qt_ref, s_ref, p_ref, acc_ref, *,
               n_blk, heads):
    qi = pl.program_id(2)
    blk = MOBA_BLOCK

    @pl.when(qi == 0)
    def _():
        for c in range(heads):
            k_all = k_ref[:, _lanes(c)].astype(F32).reshape(n_blk, blk, LANES)
            kmean_ref[c] = jnp.mean(k_all, axis=1)
        _fill_transposed(vt_ref, v_ref, [c * LANES for c in range(heads)], blk)

    blk_id = lax.broadcasted_iota(jnp.int32, (n_blk, 1), 0)
    past = blk_id < qi

    def make_score_fn(c):
        q_t = q_ref[:, _lanes(c)].T
        km = kmean_ref[c]
        km_hi = km.astype(BF16)
        rem = km - km_hi.astype(F32)
        km_mid = rem.astype(BF16)
        km_lo = (rem - km_mid.astype(F32)).astype(BF16)
        gate = (jnp.dot(km_hi, q_t, preferred_element_type=F32) + jnp.dot(km_mid, q_t, preferred_element_type=F32)
                + jnp.dot(km_lo, q_t, preferred_element_type=F32))
        gate = jnp.where(past, gate, -jnp.inf)
        selected = jnp.zeros(gate.shape, F32)
        for _ in range(min(MOBA_TOPK, n_blk - 1)):
            best = jnp.max(gate, axis=0, keepdims=True)
            first = jnp.min(jnp.where(gate == best, blk_id, n_blk), axis=0, keepdims=True)
            pick = blk_id == first
            selected = jnp.where(jnp.logical_and(pick, past), 1.0, selected)
            gate = jnp.where(pick, -jnp.inf, gate)
        sel_ref[c] = selected
        qt_ref[c] = q_t
        return lambda start: jnp.dot(k_ref[pl.ds(start, blk), _lanes(c)], qt_ref[c], preferred_element_type=F32)

    score_fns = [make_score_fn(c) for c in range(heads)]
    chosen_fns = [lambda n, c=c: jnp.logical_or(sel_ref[c, pl.ds(n, 1), :] > 0.0, n >= qi)
                  for c in range(heads)]
    results = _causal_flash_t(score_fns, vt_ref, s_ref, p_ref, acc_ref, qi, past_mask_fns=chosen_fns)
    for c, (l, acc) in enumerate(results):
        o_ref[:, _lanes(c)] = (acc / l).T.astype(o_ref.dtype)


def _moba_attention(qkv, *, batch, seq, heads, q_col, k_col, v_col):
    blk = MOBA_BLOCK
    n_blk = seq // blk
    width = heads * LANES
    assert seq % blk == 0 and D_HEADS % heads == 0
    assert q_col % heads == 0 and k_col % heads == 0 and v_col % heads == 0
    return pl.pallas_call(
        functools.partial(_moba_body, n_blk=n_blk, heads=heads),
        grid=(batch, D_HEADS // heads, n_blk),
        in_specs=[pl.BlockSpec((blk, width), lambda b, h, i: (b * n_blk + i, q_col // heads + h)),
                  pl.BlockSpec((seq, width), lambda b, h, i: (b, k_col // heads + h)),
                  pl.BlockSpec((seq, width), lambda b, h, i: (b, v_col // heads + h))],
        out_specs=pl.BlockSpec((blk, width), lambda b, h, i: (b * n_blk + i, h)),
        out_shape=jax.ShapeDtypeStruct((batch * seq, D_HEADS * LANES), BF16),
        scratch_shapes=[pltpu.VMEM((heads, n_blk, LANES), F32),
                        pltpu.VMEM((heads, LANES, seq), BF16),
                        pltpu.VMEM((heads, n_blk, blk), F32),
                        pltpu.VMEM((heads, LANES, blk), BF16)] + _flash_scratch(heads, blk, blk, LANES),
        compiler_params=_params("parallel", "parallel", "arbitrary"),
        name="moba_attn",
    )(qkv, qkv, qkv)


def _rope_tables(seq):
    pos = jnp.arange(seq, dtype=F32)[:, None]

    def angles(dim):
        inv = 1.0 / (ROPE_THETA ** (jnp.arange(0, dim, 2, dtype=F32) / dim))
        return pos * inv[None, :]

    a128 = angles(HEAD_DIM)
    a64 = angles(A_QK_DIM)
    c128 = jnp.concatenate([jnp.cos(a128)] * 2, axis=1)
    s128 = jnp.concatenate([-jnp.sin(a128), jnp.sin(a128)], axis=1)
    c64 = jnp.concatenate([jnp.cos(a64)] * 4, axis=1)
    s64 = jnp.concatenate([-jnp.sin(a64)] * 2 + [jnp.sin(a64)] * 2, axis=1)
    return c128, s128, c64, s64


def _interleave_diff_heads(w):
    d = w.shape[0]
    return w.reshape(d, A_HEADS, 2, 2, 32).transpose(0, 1, 3, 2, 4).reshape(d, A_HEADS * LANES)


def _spread_rope64(w):
    z = jnp.zeros(w.shape[:-1] + (32,), w.dtype)
    return jnp.concatenate([w[..., :32], z, w[..., 32:], z], axis=-1)


def _ab_mixer(x, gain, w_in, lam_params, subln, w_out, layer_idx, rope, *, batch, seq):
    aw = A_HEADS * LANES
    w = jnp.concatenate([_interleave_diff_heads(w_in[:, :aw]), _interleave_diff_heads(w_in[:, aw:2 * aw]),
                         w_in[:, 2 * aw:]], axis=1).astype(BF16)
    a_scale = A_QK_DIM ** -0.5 * LOG2_E
    b_scale = HEAD_DIM ** -0.5
    dils = [dil for _, dil in B_GROUPS for _ in range(B_HEADS_PER_GROUP)]
    modes = ([(ROPE64, a_scale, 1)] * A_HEADS + [(ROPE64, 1.0, 1)] * A_HEADS + [(PLAIN, 1.0, 1)] * A_HEADS
             + [(ROPE128, b_scale, d) for d in dils] + [(ROPE128, 1.0, d) for d in dils]
             + [(PLAIN, 1.0, d) for d in dils])
    row_tile = 512
    qkv = _norm_proj(x, gain, w, rope, modes, seq=seq, tm=row_tile, tn=512, out_dtype=BF16)
    lam_init = 0.8 - 0.6 * math.exp(-0.3 * layer_idx)
    oa = _diff_attention(qkv, lam_params, subln, batch=batch, seq=seq, tq=256, heads=2,
                         q_col=0, k_col=A_HEADS, v_col=2 * A_HEADS, lam_init=lam_init)
    b0 = 3 * A_HEADS
    outs, lses = [], []
    for g, (_, dil) in enumerate(B_GROUPS):
        o, lse = _dilated_group(qkv, batch=batch, seq=seq, row_tile=row_tile, group=g, dil=dil,
                                heads=2 if dil == 1 else 4,
                                q_col=b0, k_col=b0 + B_HEADS, v_col=b0 + 2 * B_HEADS)
        outs.append(o)
        lses.append(lse)
    ob = _merge_groups(outs, lses)
    o = jnp.concatenate([oa, ob], axis=1)
    return _out_proj(o, w_out.astype(BF16), x, tm=512, tn=1024)


def _cd_mixer(x, gain, w_in, q_norm, w_uq, kv_norm, w_ukv, w_out, rope, *, batch, seq):
    lat = C_Q_LORA + C_KV_LORA
    d_scale = HEAD_DIM ** -0.5 * LOG2_E
    latent = _norm_proj(x, gain, w_in[:, :lat].astype(BF16), rope, [(PLAIN, 1.0, 1)] * (lat // LANES),
                        seq=seq, tm=512, tn=lat // 2, out_dtype=F32)
    w_main = jnp.concatenate([w_in[:, lat + C_ROPE:], _spread_rope64(w_in[:, lat:lat + C_ROPE])],
                             axis=1).astype(BF16)
    modes = ([(ROPE128, d_scale, 1)] * D_HEADS + [(ROPE128, 1.0, 1)] * D_HEADS + [(PLAIN, 1.0, 1)] * D_HEADS
             + [(ROPE64, 1.0, 1)])
    main = _norm_proj(x, gain, w_main, rope, modes, seq=seq, tm=512, tn=5 * LANES, out_dtype=BF16)

    c_scale = (C_NOPE + C_ROPE) ** -0.5 * LOG2_E
    w_q = w_uq.reshape(C_Q_LORA, C_HEADS, C_NOPE + C_ROPE)
    w_q = jnp.concatenate([w_q[..., :C_NOPE], _spread_rope64(w_q[..., C_NOPE:])], axis=-1)
    w_q = w_q.reshape(C_Q_LORA, C_HEADS * 2 * LANES).astype(BF16)
    q_modes = [(PLAIN, c_scale, 1), (ROPE64, c_scale, 1)] * C_HEADS
    qc = _norm_proj(latent[:, :C_Q_LORA], q_norm, w_q, rope, q_modes, seq=seq, tm=512, tn=512, out_dtype=BF16)
    kv = _norm_proj(latent[:, C_Q_LORA:], kv_norm, w_ukv.astype(BF16), rope,
                    [(PLAIN, 1.0, 1)] * (2 * C_HEADS), seq=seq, tm=512, tn=512, out_dtype=BF16)
    oc = _mla_attention(qc, kv, main, batch=batch, seq=seq, tq=256, heads=4, k_rope_col=3 * D_HEADS)
    od = _moba_attention(main, batch=batch, seq=seq, heads=4, q_col=0, k_col=D_HEADS, v_col=2 * D_HEADS)
    o = jnp.concatenate([oc, od], axis=1)
    return _out_proj(o, w_out.astype(BF16), x, tm=512, tn=1024)


def kernel(x, p, ffn_norm, ffn_w_gu, ffn_w_down, mix_norm, ab_w_in, ab_lambda, ab_subln, ab_w_out,
           cd_w_in, cd_q_norm, cd_w_uq, cd_kv_norm, cd_w_ukv, cd_w_out, ple_norm, ple_w_gate,
           ple_w_proj, final_norm):
    batch, seq, d = x.shape
    depth = p.shape[0]
    rope = _rope_tables(seq)
    x = x.reshape(batch * seq, d)
    for i in range(depth):
        j = i // 2
        x = _ffn(x, ffn_norm[i, 0], ffn_w_gu[i, 0].astype(BF16), ffn_w_down[i, 0].astype(BF16), tm=512, tf=512)
        if i % 2 == 0:
            x = _ab_mixer(x, mix_norm[i], ab_w_in[j], ab_lambda[j], ab_subln[j], ab_w_out[j], i, rope,
                          batch=batch, seq=seq)
        else:
            x = _cd_mixer(x, mix_norm[i], cd_w_in[j], cd_q_norm[j], cd_w_uq[j], cd_kv_norm[j], cd_w_ukv[j],
                          cd_w_out[j], rope, batch=batch, seq=seq)
        x = _ffn(x, ffn_norm[i, 1], ffn_w_gu[i, 1].astype(BF16), ffn_w_down[i, 1].astype(BF16), tm=512, tf=512)
        x = _ple(x, p[i].reshape(batch * seq, -1), ple_norm[i], ple_w_gate[i].astype(BF16),
                 ple_w_proj[i].astype(BF16), final_norm, tm=256, final_norm=(i == depth - 1))
    return x.reshape(batch, seq, d)
```

```python
import functools
import math

import jax
import jax.numpy as jnp
from jax import lax
from jax.experimental import pallas as pl
from jax.experimental.pallas import tpu as pltpu

F32 = jnp.float32
BF16 = jnp.bfloat16

D_MODEL = 2048
D_FF = 5632
PLE_DIM = 256
HEAD_DIM = 128
ROPE_THETA = 10000.0
NORM_EPS = 1e-6

A_HEADS = 8
A_QK_DIM = 64
B_GROUPS = ((128, 1), (512, 4), (2048, 16))
B_HEADS_PER_GROUP = 4
B_HEADS = B_HEADS_PER_GROUP * len(B_GROUPS)
C_HEADS = 8
C_Q_LORA = 512
C_KV_LORA = 256
C_NOPE = 128
C_ROPE = 64
D_HEADS = 8
MOBA_BLOCK = 256
MOBA_TOPK = 3

LOG2_E = 1.0 / math.log(2.0)
LANES = 128
VMEM_LIMIT = 56 * 1024 * 1024

PLAIN, ROPE128, ROPE64 = 0, 1, 2


def _rms(x, gain):
    return x * lax.rsqrt(jnp.mean(x * x, axis=-1, keepdims=True) + NORM_EPS) * gain


def _params(*semantics):
    return pltpu.CompilerParams(dimension_semantics=semantics, vmem_limit_bytes=VMEM_LIMIT)


def _norm_proj_body(x_ref, g_ref, w_ref, c128_ref, s128_ref, c64_ref, s64_ref, o_ref, xn_ref, stage_ref, *,
                    tile_patterns):
    j = pl.program_id(1)
    tm = o_ref.shape[0]

    @pl.when(j == 0)
    def _():
        xn_ref[...] = _rms(x_ref[...], g_ref[...]).astype(BF16)

    def tile(pattern):
        acc = jnp.dot(xn_ref[...], w_ref[...], preferred_element_type=F32)
        for c, (kind, scale, dil) in enumerate(pattern):
            a = acc[:, _lanes(c)]
            if scale != 1.0:
                a = a * scale
            if kind == ROPE128:
                a = a * c128_ref[...] + pltpu.roll(a, 64, 1) * s128_ref[...]
            elif kind == ROPE64:
                a = a * c64_ref[...] + pltpu.roll(a, 64, 1) * s64_ref[...]
            if dil == 1:
                o_ref[:, _lanes(c)] = a.astype(o_ref.dtype)
            else:
                rows = tm // dil
                stage_ref[c] = a
                for r in range(dil):
                    o_ref[r * rows:(r + 1) * rows, _lanes(c)] = (
                        stage_ref[c, pl.ds(r, rows, stride=dil), :].astype(o_ref.dtype))

    distinct = sorted(set(tile_patterns), key=tile_patterns.index)
    if len(distinct) == 1:
        tile(distinct[0])
    else:
        for pattern in distinct:
            tiles = [t for t, p in enumerate(tile_patterns) if p == pattern]
            cond = functools.reduce(jnp.logical_or, [j == t for t in tiles])
            pl.when(cond)(functools.partial(tile, pattern))


def _norm_proj(x, gain, w, rope, chunk_modes, *, seq, tm, tn, out_dtype):
    t, k = x.shape
    n = w.shape[1]
    per_tile = tn // LANES
    assert t % tm == 0 and n % tn == 0 and seq % tm == 0 and len(chunk_modes) * LANES == n
    tile_patterns = tuple(tuple(chunk_modes[a * per_tile:(a + 1) * per_tile]) for a in range(n // tn))
    seq_tiles = seq // tm
    tab = pl.BlockSpec((tm, LANES), lambda i, j: (i % seq_tiles, 0))
    return pl.pallas_call(
        functools.partial(_norm_proj_body, tile_patterns=tile_patterns),
        grid=(t // tm, n // tn),
        in_specs=[pl.BlockSpec((tm, k), lambda i, j: (i, 0)),
                  pl.BlockSpec((1, k), lambda i, j: (0, 0)),
                  pl.BlockSpec((k, tn), lambda i, j: (0, j)),
                  tab, tab, tab, tab],
        out_specs=pl.BlockSpec((tm, tn), lambda i, j: (i, j)),
        out_shape=jax.ShapeDtypeStruct((t, n), out_dtype),
        scratch_shapes=[pltpu.VMEM((tm, k), BF16), pltpu.VMEM((per_tile, tm, LANES), F32)],
        compiler_params=_params("parallel", "arbitrary"),
        name="norm_proj",
    )(x, gain.reshape(1, k), w, *rope)


def _out_proj_body(a_ref, w_ref, res_ref, o_ref):
    o_ref[...] = res_ref[...] + jnp.dot(a_ref[...], w_ref[...], preferred_element_type=F32)


def _out_proj(a, w, res, *, tm, tn):
    t, k = a.shape
    n = w.shape[1]
    assert t % tm == 0 and n % tn == 0
    return pl.pallas_call(
        _out_proj_body,
        grid=(t // tm, n // tn),
        in_specs=[pl.BlockSpec((tm, k), lambda i, j: (i, 0)),
                  pl.BlockSpec((k, tn), lambda i, j: (0, j)),
                  pl.BlockSpec((tm, tn), lambda i, j: (i, j))],
        out_specs=pl.BlockSpec((tm, tn), lambda i, j: (i, j)),
        out_shape=jax.ShapeDtypeStruct((t, n), F32),
        compiler_params=_params("parallel", "parallel"),
        name="out_proj",
    )(a, w, res)


def _ffn_body(x_ref, g_ref, wg_ref, wu_ref, wd_ref, o_ref, xn_ref):
    j = pl.program_id(1)

    @pl.when(j == 0)
    def _():
        xn_ref[...] = _rms(x_ref[...], g_ref[...]).astype(BF16)
        o_ref[...] = jnp.zeros_like(o_ref)

    xn = xn_ref[...]
    g = jnp.dot(xn, wg_ref[...], preferred_element_type=F32)
    u = jnp.dot(xn, wu_ref[...], preferred_element_type=F32)
    h = (g * jax.nn.sigmoid(g) * u).astype(BF16)
    o_ref[...] += jnp.dot(h, wd_ref[...], preferred_element_type=F32)

    @pl.when(j == pl.num_programs(1) - 1)
    def _():
        o_ref[...] = x_ref[...] + 0.5 * o_ref[...]


def _ffn(x, gain, w_gu, w_down, *, tm, tf):
    t, d = x.shape
    d_ff = w_down.shape[0]
    assert t % tm == 0 and d_ff % tf == 0
    nf = d_ff // tf
    return pl.pallas_call(
        _ffn_body,
        grid=(t // tm, nf),
        in_specs=[pl.BlockSpec((tm, d), lambda i, j: (i, 0)),
                  pl.BlockSpec((1, d), lambda i, j: (0, 0)),
                  pl.BlockSpec((d, tf), lambda i, j: (0, j)),
                  pl.BlockSpec((d, tf), lambda i, j: (0, nf + j)),
                  pl.BlockSpec((tf, d), lambda i, j: (j, 0))],
        out_specs=pl.BlockSpec((tm, d), lambda i, j: (i, 0)),
        out_shape=jax.ShapeDtypeStruct((t, d), F32),
        scratch_shapes=[pltpu.VMEM((tm, d), BF16)],
        compiler_params=_params("parallel", "arbitrary"),
        name="ffn",
    )(x, gain.reshape(1, d), w_gu, w_gu, w_down)


def _ple_body(x_ref, p_ref, g_ref, wg_ref, wp_ref, fg_ref, o_ref, *, final_norm):
    x = x_ref[...]
    xn = _rms(x, g_ref[...]).astype(BF16)
    gate = jax.nn.sigmoid(jnp.dot(xn, wg_ref[...], preferred_element_type=F32))
    proj = jnp.dot(p_ref[...].astype(BF16), wp_ref[...], preferred_element_type=F32)
    y = x + gate * proj
    if final_norm:
        y = _rms(y, fg_ref[...])
    o_ref[...] = y


def _ple(x, p, gain, w_gate, w_proj, final_gain, *, tm, final_norm):
    t, d = x.shape
    pd = p.shape[1]
    assert t % tm == 0
    const = lambda i: (0, 0)
    return pl.pallas_call(
        functools.partial(_ple_body, final_norm=final_norm),
        grid=(t // tm,),
        in_specs=[pl.BlockSpec((tm, d), lambda i: (i, 0)),
                  pl.BlockSpec((tm, pd), lambda i: (i, 0)),
                  pl.BlockSpec((1, d), const),
                  pl.BlockSpec((d, d), const),
                  pl.BlockSpec((pd, d), const),
                  pl.BlockSpec((1, d), const)],
        out_specs=pl.BlockSpec((tm, d), lambda i: (i, 0)),
        out_shape=jax.ShapeDtypeStruct((t, d), F32),
        compiler_params=_params("parallel"),
        name="ple",
    )(x, p, gain.reshape(1, d), w_gate, w_proj, final_gain.reshape(1, d))


def _dot_nt(a, b):
    return lax.dot_general(a, b, (((1,), (1,)), ((), ())), preferred_element_type=F32)


def _lanes(c, width=LANES):
    return slice(c * width, (c + 1) * width)


def _fill_transposed(vt_ref, v_ref, lane_offsets, chunk):
    dv = vt_ref.shape[1]

    def body(j, _):
        start = pl.multiple_of(j * chunk, chunk)
        for c, off in enumerate(lane_offsets):
            vt_ref[c, :, pl.ds(start, chunk)] = v_ref[pl.ds(start, chunk), off:off + dv].T
        return 0

    lax.fori_loop(0, v_ref.shape[0] // chunk, body, 0)


def _flash_scratch(heads, tk, cols, dv):
    return [pltpu.VMEM((heads, tk, cols), F32), pltpu.VMEM((heads, tk, cols), BF16),
            pltpu.VMEM((heads, dv, cols), F32)]


def _causal_flash_t(score_fns, vt_ref, s_ref, p_ref, acc_ref, qi, past_mask_fns=None):
    heads = range(len(score_fns))
    _, tk, cols = s_ref.shape

    def stage_scores(block):
        start = pl.multiple_of(block * tk, tk)
        for c in heads:
            s = score_fns[c](start)
            if past_mask_fns is not None:
                s = jnp.where(past_mask_fns[c](block), s, -jnp.inf)
            s_ref[c] = s

    def accumulate(block, alphas):
        start = pl.multiple_of(block * tk, tk)
        for c in heads:
            pv = jnp.dot(vt_ref[c, :, pl.ds(start, tk)], p_ref[c], preferred_element_type=F32)
            acc_ref[c] = alphas[c] * acc_ref[c] + pv

    def softmax(stats, visible=None):
        new_stats, alphas = [], []
        for c in heads:
            m, l = stats[c]
            read = (lambda: s_ref[c]) if visible is None else (lambda: jnp.where(visible, s_ref[c], -jnp.inf))
            m_new = jnp.maximum(m, jnp.max(read(), axis=0, keepdims=True))
            alpha = jnp.exp2(m - m_new)
            p = jnp.exp2(read() - m_new)
            p_ref[c] = p.astype(BF16)
            new_stats.append((m_new, alpha * l + jnp.sum(p, axis=0, keepdims=True)))
            alphas.append(alpha)
        return tuple(new_stats), tuple(alphas)

    def past_block(j, carry):
        stats, alphas = carry
        accumulate(jnp.maximum(j - 1, 0), alphas)
        stats, alphas = softmax(stats)
        stage_scores(j + 1)
        return stats, alphas

    for c in heads:
        p_ref[c] = jnp.zeros(p_ref.shape[1:], BF16)
        acc_ref[c] = jnp.zeros(acc_ref.shape[1:], F32)
    stage_scores(0)
    init = (tuple((jnp.full((1, cols), jnp.finfo(F32).min, F32), jnp.zeros((1, cols), F32)) for _ in heads),
            tuple(jnp.ones((1, cols), F32) for _ in heads))
    stats, alphas = lax.fori_loop(0, qi, past_block, init)

    accumulate(jnp.maximum(qi - 1, 0), alphas)
    key_row = lax.broadcasted_iota(jnp.int32, (tk, 1), 0)
    query_col = lax.broadcasted_iota(jnp.int32, (1, cols), 1) % tk
    stats, alphas = softmax(stats, visible=key_row <= query_col)
    accumulate(qi, alphas)
    return tuple((stats[c][1], acc_ref[c]) for c in heads)


def _diff_attn_body(lam_ref, subln_ref, q_ref, k_ref, v_ref, o_ref, vt_ref, s_ref, p_ref, acc_ref, *,
                    tq, heads, lam_init):
    qi = pl.program_id(2)

    @pl.when(qi == 0)
    def _():
        _fill_transposed(vt_ref, v_ref, [c * LANES for c in range(heads)], tq)

    is_map0 = (lax.broadcasted_iota(jnp.int32, (LANES, 1), 0) % 64) < 32

    def make_score_fn(c):
        q_t = q_ref[:, _lanes(c)].T
        zero = jnp.zeros_like(q_t)
        qq_t = jnp.concatenate([jnp.where(is_map0, q_t, zero), jnp.where(is_map0, zero, q_t)], axis=1)
        return lambda start: jnp.dot(k_ref[pl.ds(start, tq), _lanes(c)], qq_t, preferred_element_type=F32)

    results = _causal_flash_t([make_score_fn(c) for c in range(heads)], vt_ref, s_ref, p_ref, acc_ref, qi)

    lp = lam_ref[...]
    lam = (jnp.exp(jnp.sum(lp[0:1] * lp[1:2], axis=-1, keepdims=True))
           - jnp.exp(jnp.sum(lp[2:3] * lp[3:4], axis=-1, keepdims=True)) + lam_init)
    for c, (l, acc) in enumerate(results):
        o_t = acc / l
        o = (o_t[:, :tq] - lam * o_t[:, tq:]).T
        o_ref[:, _lanes(c)] = (_rms(o, subln_ref[...]) * (1.0 - lam_init)).astype(o_ref.dtype)


def _diff_attention(qkv, lam_params, subln, *, batch, seq, tq, heads, q_col, k_col, v_col, lam_init):
    nq = seq // tq
    width = heads * LANES
    assert A_HEADS % heads == 0 and q_col % heads == 0 and k_col % heads == 0 and v_col % heads == 0
    return pl.pallas_call(
        functools.partial(_diff_attn_body, tq=tq, heads=heads, lam_init=lam_init),
        grid=(batch, A_HEADS // heads, nq),
        in_specs=[pl.BlockSpec((4, A_QK_DIM), lambda b, h, i: (0, 0)),
                  pl.BlockSpec((1, LANES), lambda b, h, i: (0, 0)),
                  pl.BlockSpec((tq, width), lambda b, h, i: (b * nq + i, q_col // heads + h)),
                  pl.BlockSpec((seq, width), lambda b, h, i: (b, k_col // heads + h)),
                  pl.BlockSpec((seq, width), lambda b, h, i: (b, v_col // heads + h))],
        out_specs=pl.BlockSpec((tq, width), lambda b, h, i: (b * nq + i, h)),
        out_shape=jax.ShapeDtypeStruct((batch * seq, A_HEADS * LANES), BF16),
        scratch_shapes=[pltpu.VMEM((heads, LANES, seq), BF16)] + _flash_scratch(heads, tq, 2 * tq, LANES),
        compiler_params=_params("parallel", "parallel", "arbitrary"),
        name="diff_attn",
    )(lam_params, subln.reshape(1, LANES), qkv, qkv, qkv)


def _tile_index(ref, i, tile):
    per_slab = ref.shape[1]
    if per_slab >= tile:
        runs = per_slab // tile
        return i // runs, pl.ds(pl.multiple_of((i % runs) * tile, tile), tile)
    slabs = tile // per_slab
    return pl.ds(i * slabs, slabs), slice(None)


def _load_tile(ref, i, c, tile):
    slab, rows = _tile_index(ref, i, tile)
    return ref[slab, rows, _lanes(c)].reshape(tile, LANES)


def _store_tile(ref, i, c, tile, value):
    slab, rows = _tile_index(ref, i, tile)
    per_slab = ref.shape[1]
    ref[slab, rows, _lanes(c)] = value if per_slab >= tile else value.reshape(tile // per_slab, per_slab, LANES)


def _dilated_body(q_ref, k_ref, v_ref, o_ref, lse_ref, *, tile, heads):
    length = q_ref.shape[0] * q_ref.shape[1]
    row = lax.broadcasted_iota(jnp.int32, (tile, 1), 0)

    def attend(i, c, k, v, valid):
        s = jnp.where(valid, _dot_nt(_load_tile(q_ref, i, c, tile), k), -jnp.inf)
        m = jnp.max(s, axis=-1, keepdims=True)
        p = jnp.exp(s - m)
        l = jnp.sum(p, axis=-1, keepdims=True)
        o = jnp.dot(p.astype(BF16), v, preferred_element_type=F32) / l
        _store_tile(o_ref, i, c, tile, o)
        _store_tile(lse_ref, i, c, tile, jnp.broadcast_to(m + jnp.log(l), (tile, LANES)))

    col = lax.broadcasted_iota(jnp.int32, (1, tile), 1)
    for c in range(heads):
        attend(0, c, _load_tile(k_ref, 0, c, tile), _load_tile(v_ref, 0, c, tile), col <= row)

    col2 = lax.broadcasted_iota(jnp.int32, (1, 2 * tile), 1)
    band = jnp.logical_and(col2 >= row, col2 <= row + tile)

    def body(i, _):
        for c in range(heads):
            k = jnp.concatenate([_load_tile(k_ref, i - 1, c, tile), _load_tile(k_ref, i, c, tile)], axis=0)
            v = jnp.concatenate([_load_tile(v_ref, i - 1, c, tile), _load_tile(v_ref, i, c, tile)], axis=0)
            attend(i, c, k, v, band)
        return 0

    lax.fori_loop(1, length // tile, body, 0)


def _dilated_group(qkv, *, batch, seq, row_tile, group, dil, heads, q_col, k_col, v_col):
    hpg = B_HEADS_PER_GROUP
    tile = B_GROUPS[group][0] // dil
    slabs, per_slab = seq // row_tile, row_tile // dil
    assert tile == LANES and hpg % heads == 0 and (seq // dil) % tile == 0
    assert per_slab % tile == 0 or tile % per_slab == 0
    width = heads * LANES
    view = qkv.reshape(batch, slabs, dil, per_slab, qkv.shape[1])

    def in_spec(col):
        first = col + group * hpg
        assert first % heads == 0
        return pl.BlockSpec((None, slabs, None, per_slab, width),
                            lambda b, h, r: (b, 0, r, 0, first // heads + h))

    out_spec = pl.BlockSpec((None, slabs, None, per_slab, width), lambda b, h, r: (b, 0, r, 0, h))
    out_sds = jax.ShapeDtypeStruct((batch, slabs, dil, per_slab, hpg * LANES), F32)
    o, lse = pl.pallas_call(
        functools.partial(_dilated_body, tile=tile, heads=heads),
        grid=(batch, hpg // heads, dil),
        in_specs=[in_spec(q_col), in_spec(k_col), in_spec(v_col)],
        out_specs=[out_spec, out_spec],
        out_shape=[out_sds, out_sds],
        compiler_params=_params("parallel", "parallel", "parallel"),
        name=f"dilated_g{group}",
    )(view, view, view)
    shape = (batch * slabs, dil, per_slab, hpg * LANES)
    return o.reshape(shape), lse.reshape(shape)


def _merge_body(o0_ref, o1_ref, o2_ref, l0_ref, l1_ref, l2_ref, out_ref, nat_ref):
    def natural(ref, slot, c):
        dil, rows = ref.shape[0], ref.shape[1]
        if dil == 1:
            return ref[0, :, _lanes(c)]
        for r in range(dil):
            nat_ref[slot, pl.ds(r, rows, stride=dil), :] = ref[r, :, _lanes(c)]
        return nat_ref[slot]

    for c in range(out_ref.shape[1] // LANES):
        o0, o1, o2 = natural(o0_ref, None, c), natural(o1_ref, 0, c), natural(o2_ref, 1, c)
        l0, l1, l2 = natural(l0_ref, None, c), natural(l1_ref, 2, c), natural(l2_ref, 3, c)
        m = jnp.maximum(jnp.maximum(l0, l1), l2)
        w0, w1, w2 = jnp.exp(l0 - m), jnp.exp(l1 - m), jnp.exp(l2 - m)
        out_ref[:, _lanes(c)] = ((w0 * o0 + w1 * o1 + w2 * o2) / (w0 + w1 + w2)).astype(out_ref.dtype)


def _merge_groups(outs, lses):
    tiles, _, _, width = outs[0].shape
    row_tile = outs[0].shape[1] * outs[0].shape[2]
    specs = [pl.BlockSpec((None,) + a.shape[1:], lambda i: (i, 0, 0, 0)) for a in (*outs, *lses)]
    return pl.pallas_call(
        _merge_body,
        grid=(tiles,),
        in_specs=specs,
        out_specs=pl.BlockSpec((row_tile, width), lambda i: (i, 0)),
        out_shape=jax.ShapeDtypeStruct((tiles * row_tile, width), BF16),
        scratch_shapes=[pltpu.VMEM((4, row_tile, LANES), F32)],
        compiler_params=_params("parallel"),
        name="dilated_merge",
    )(*outs, *lses)


def _mla_body(q_ref, kv_ref, kr_ref, o_ref, vt_ref, qt_ref, s_ref, p_ref, acc_ref, *, tq, heads):
    qi = pl.program_id(2)

    @pl.when(qi == 0)
    def _():
        _fill_transposed(vt_ref, kv_ref, [(2 * c + 1) * LANES for c in range(heads)], tq)

    def make_score_fn(c):
        qt_ref[c] = q_ref[:, _lanes(c, 2 * LANES)].T

        def score(start):
            k = jnp.concatenate([kv_ref[pl.ds(start, tq), _lanes(2 * c)], kr_ref[pl.ds(start, tq), :]], axis=1)
            return jnp.dot(k, qt_ref[c], preferred_element_type=F32)

        return score

    results = _causal_flash_t([make_score_fn(c) for c in range(heads)], vt_ref, s_ref, p_ref, acc_ref, qi)
    for c, (l, acc) in enumerate(results):
        o_ref[:, _lanes(c)] = (acc / l).T.astype(o_ref.dtype)


def _mla_attention(q, kv, k_rope_src, *, batch, seq, tq, heads, k_rope_col):
    nq = seq // tq
    assert C_HEADS % heads == 0
    return pl.pallas_call(
        functools.partial(_mla_body, tq=tq, heads=heads),
        grid=(batch, C_HEADS // heads, nq),
        in_specs=[pl.BlockSpec((tq, heads * 2 * LANES), lambda b, h, i: (b * nq + i, h)),
                  pl.BlockSpec((seq, heads * 2 * LANES), lambda b, h, i: (b, h)),
                  pl.BlockSpec((seq, LANES), lambda b, h, i: (b, k_rope_col))],
        out_specs=pl.BlockSpec((tq, heads * LANES), lambda b, h, i: (b * nq + i, h)),
        out_shape=jax.ShapeDtypeStruct((batch * seq, C_HEADS * LANES), BF16),
        scratch_shapes=[pltpu.VMEM((heads, LANES, seq), BF16), pltpu.VMEM((heads, 2 * LANES, tq), BF16)]
        + _flash_scratch(heads, tq, tq, LANES),
        compiler_params=_params("parallel", "parallel", "arbitrary"),
        name="mla_attn",
    )(q, kv, k_rope_src)


def _moba_body(q_ref, k_ref, v_ref, o_ref, kmean_ref, vt_ref, sel_ref, qt_ref, s_ref, p_ref, acc_ref, *,
               n_blk, heads):
    qi = pl.program_id(2)
    blk = MOBA_BLOCK

    @pl.when(qi == 0)
    def _():
        for c in range(heads):
            k_all = k_ref[:, _lanes(c)].astype(F32).reshape(n_blk, blk, LANES)
            kmean_ref[c] = jnp.mean(k_all, axis=1)
        _fill_transposed(vt_ref, v_ref, [c * LANES for c in range(heads)], blk)

    blk_id = lax.broadcasted_iota(jnp.int32, (n_blk, 1), 0)
    past = blk_id < qi

    def make_score_fn(c):
        q_t = q_ref[:, _lanes(c)].T
        km = kmean_ref[c]
        km_hi = km.astype(BF16)
        rem = km - km_hi.astype(F32)
        km_mid = rem.astype(BF16)
        km_lo = (rem - km_mid.astype(F32)).astype(BF16)
        gate = (jnp.dot(km_hi, q_t, preferred_element_type=F32) + jnp.dot(km_mid, q_t, preferred_element_type=F32)
                + jnp.dot(km_lo, q_t, preferred_element_type=F32))
        gate = jnp.where(past, gate, -jnp.inf)
        selected = jnp.zeros(gate.shape, F32)
        for _ in range(min(MOBA_TOPK, n_blk - 1)):
            best = jnp.max(gate, axis=0, keepdims=True)
            first = jnp.min(jnp.where(gate == best, blk_id, n_blk), axis=0, keepdims=True)
            pick = blk_id == first
            selected = jnp.where(jnp.logical_and(pick, past), 1.0, selected)
            gate = jnp.where(pick, -jnp.inf, gate)
        sel_ref[c] = selected
        qt_ref[c] = q_t
        return lambda start: jnp.dot(k_ref[pl.ds(start, blk), _lanes(c)], qt_ref[c], preferred_element_type=F32)

    score_fns = [make_score_fn(c) for c in range(heads)]
    chosen_fns = [lambda n, c=c: jnp.logical_or(sel_ref[c, pl.ds(n, 1), :] > 0.0, n >= qi)
                  for c in range(heads)]
    results = _causal_flash_t(score_fns, vt_ref, s_ref, p_ref, acc_ref, qi, past_mask_fns=chosen_fns)
    for c, (l, acc) in enumerate(results):
        o_ref[:, _lanes(c)] = (acc / l).T.astype(o_ref.dtype)


def _moba_attention(qkv, *, batch, seq, heads, q_col, k_col, v_col):
    blk = MOBA_BLOCK
    n_blk = seq // blk
    width = heads * LANES
    assert seq % blk == 0 and D_HEADS % heads == 0
    assert q_col % heads == 0 and k_col % heads == 0 and v_col % heads == 0
    return pl.pallas_call(
        functools.partial(_moba_body, n_blk=n_blk, heads=heads),
        grid=(batch, D_HEADS // heads, n_blk),
        in_specs=[pl.BlockSpec((blk, width), lambda b, h, i: (b * n_blk + i, q_col // heads + h)),
                  pl.BlockSpec((seq, width), lambda b, h, i: (b, k_col // heads + h)),
                  pl.BlockSpec((seq, width), lambda b, h, i: (b, v_col // heads + h))],
        out_specs=pl.BlockSpec((blk, width), lambda b, h, i: (b * n_blk + i, h)),
        out_shape=jax.ShapeDtypeStruct((batch * seq, D_HEADS * LANES), BF16),
        scratch_shapes=[pltpu.VMEM((heads, n_blk, LANES), F32),
                        pltpu.VMEM((heads, LANES, seq), BF16),
                        pltpu.VMEM((heads, n_blk, blk), F32),
                        pltpu.VMEM((heads, LANES, blk), BF16)] + _flash_scratch(heads, blk, blk, LANES),
        compiler_params=_params("parallel", "parallel", "arbitrary"),
        name="moba_attn",
    )(qkv, qkv, qkv)


def _rope_tables(seq):
    pos = jnp.arange(seq, dtype=F32)[:, None]

    def angles(dim):
        inv = 1.0 / (ROPE_THETA ** (jnp.arange(0, dim, 2, dtype=F32) / dim))
        return pos * inv[None, :]

    a128 = angles(HEAD_DIM)
    a64 = angles(A_QK_DIM)
    c128 = jnp.concatenate([jnp.cos(a128)] * 2, axis=1)
    s128 = jnp.concatenate([-jnp.sin(a128), jnp.sin(a128)], axis=1)
    c64 = jnp.concatenate([jnp.cos(a64)] * 4, axis=1)
    s64 = jnp.concatenate([-jnp.sin(a64)] * 2 + [jnp.sin(a64)] * 2, axis=1)
    return c128, s128, c64, s64


def _interleave_diff_heads(w):
    d = w.shape[0]
    return w.reshape(d, A_HEADS, 2, 2, 32).transpose(0, 1, 3, 2, 4).reshape(d, A_HEADS * LANES)


def _spread_rope64(w):
    z = jnp.zeros(w.shape[:-1] + (32,), w.dtype)
    return jnp.concatenate([w[..., :32], z, w[..., 32:], z], axis=-1)


def _ab_mixer(x, gain, w_in, lam_params, subln, w_out, layer_idx, rope, *, batch, seq):
    aw = A_HEADS * LANES
    w = jnp.concatenate([_interleave_diff_heads(w_in[:, :aw]), _interleave_diff_heads(w_in[:, aw:2 * aw]),
                         w_in[:, 2 * aw:]], axis=1).astype(BF16)
    a_scale = A_QK_DIM ** -0.5 * LOG2_E
    b_scale = HEAD_DIM ** -0.5
    dils = [dil for _, dil in B_GROUPS for _ in range(B_HEADS_PER_GROUP)]
    modes = ([(ROPE64, a_scale, 1)] * A_HEADS + [(ROPE64, 1.0, 1)] * A_HEADS + [(PLAIN, 1.0, 1)] * A_HEADS
             + [(ROPE128, b_scale, d) for d in dils] + [(ROPE128, 1.0, d) for d in dils]
             + [(PLAIN, 1.0, d) for d in dils])
    row_tile = 1024
    qkv = _norm_proj(x, gain, w, rope, modes, seq=seq, tm=row_tile, tn=768, out_dtype=BF16)
    lam_init = 0.8 - 0.6 * math.exp(-0.3 * layer_idx)
    oa = _diff_attention(qkv, lam_params, subln, batch=batch, seq=seq, tq=256, heads=2,
                         q_col=0, k_col=A_HEADS, v_col=2 * A_HEADS, lam_init=lam_init)
    b0 = 3 * A_HEADS
    outs, lses = [], []
    for g, (_, dil) in enumerate(B_GROUPS):
        o, lse = _dilated_group(qkv, batch=batch, seq=seq, row_tile=row_tile, group=g, dil=dil,
                                heads=2 if dil == 1 else 4,
                                q_col=b0, k_col=b0 + B_HEADS, v_col=b0 + 2 * B_HEADS)
        outs.append(o)
        lses.append(lse)
    ob = _merge_groups(outs, lses)
    o = jnp.concatenate([oa, ob], axis=1)
    return _out_proj(o, w_out.astype(BF16), x, tm=512, tn=1024)


def _cd_mixer(x, gain, w_in, q_norm, w_uq, kv_norm, w_ukv, w_out, rope, *, batch, seq):
    lat = C_Q_LORA + C_KV_LORA
    d_scale = HEAD_DIM ** -0.5 * LOG2_E
    latent = _norm_proj(x, gain, w_in[:, :lat].astype(BF16), rope, [(PLAIN, 1.0, 1)] * (lat // LANES),
                        seq=seq, tm=512, tn=lat // 2, out_dtype=F32)
    w_main = jnp.concatenate([w_in[:, lat + C_ROPE:], _spread_rope64(w_in[:, lat:lat + C_ROPE])],
                             axis=1).astype(BF16)
    modes = ([(ROPE128, d_scale, 1)] * D_HEADS + [(ROPE128, 1.0, 1)] * D_HEADS + [(PLAIN, 1.0, 1)] * D_HEADS
             + [(ROPE64, 1.0, 1)])
    main = _norm_proj(x, gain, w_main, rope, modes, seq=seq, tm=1024, tn=5 * LANES, out_dtype=BF16)

    c_scale = (C_NOPE + C_ROPE) ** -0.5 * LOG2_E
    w_q = w_uq.reshape(C_Q_LORA, C_HEADS, C_NOPE + C_ROPE)
    w_q = jnp.concatenate([w_q[..., :C_NOPE], _spread_rope64(w_q[..., C_NOPE:])], axis=-1)
    w_q = w_q.reshape(C_Q_LORA, C_HEADS * 2 * LANES).astype(BF16)
    q_modes = [(PLAIN, c_scale, 1), (ROPE64, c_scale, 1)] * C_HEADS
    qc = _norm_proj(latent[:, :C_Q_LORA], q_norm, w_q, rope, q_modes, seq=seq, tm=512, tn=512, out_dtype=BF16)
    kv = _norm_proj(latent[:, C_Q_LORA:], kv_norm, w_ukv.astype(BF16), rope,
                    [(PLAIN, 1.0, 1)] * (2 * C_HEADS), seq=seq, tm=512, tn=512, out_dtype=BF16)
    oc = _mla_attention(qc, kv, main, batch=batch, seq=seq, tq=256, heads=4, k_rope_col=3 * D_HEADS)
    od = _moba_attention(main, batch=batch, seq=seq, heads=4, q_col=0, k_col=D_HEADS, v_col=2 * D_HEADS)
    o = jnp.concatenate([oc, od], axis=1)
    return _out_proj(o, w_out.astype(BF16), x, tm=512, tn=1024)


def kernel(x, p, ffn_norm, ffn_w_gu, ffn_w_down, mix_norm, ab_w_in, ab_lambda, ab_subln, ab_w_out,
           cd_w_in, cd_q_norm, cd_w_uq, cd_kv_norm, cd_w_ukv, cd_w_out, ple_norm, ple_w_gate,
           ple_w_proj, final_norm):
    batch, seq, d = x.shape
    depth = p.shape[0]
    rope = _rope_tables(seq)
    x = x.reshape(batch * seq, d)
    for i in range(depth):
        j = i // 2
        x = _ffn(x, ffn_norm[i, 0], ffn_w_gu[i, 0].astype(BF16), ffn_w_down[i, 0].astype(BF16), tm=1024, tf=512)
        if i % 2 == 0:
            x = _ab_mixer(x, mix_norm[i], ab_w_in[j], ab_lambda[j], ab_subln[j], ab_w_out[j], i, rope,
                          batch=batch, seq=seq)
        else:
            x = _cd_mixer(x, mix_norm[i], cd_w_in[j], cd_q_norm[j], cd_w_uq[j], cd_kv_norm[j], cd_w_ukv[j],
                          cd_w_out[j], rope, batch=batch, seq=seq)
        x = _ffn(x, ffn_norm[i, 1], ffn_w_gu[i, 1].astype(BF16), ffn_w_down[i, 1].astype(BF16), tm=1024, tf=512)
        x = _ple(x, p[i].reshape(batch * seq, -1), ple_norm[i], ple_w_gate[i].astype(BF16),
                 ple_w_proj[i].astype(BF16), final_norm, tm=256, final_norm=(i == depth - 1))
    return x.reshape(batch, seq, d)
```

```python
import functools
import math

import jax
import jax.numpy as jnp
from jax import lax
from jax.experimental import pallas as pl
from jax.experimental.pallas import tpu as pltpu

F32 = jnp.float32
BF16 = jnp.bfloat16

D_MODEL = 2048
D_FF = 5632
PLE_DIM = 256
HEAD_DIM = 128
ROPE_THETA = 10000.0
NORM_EPS = 1e-6

A_HEADS = 8
A_QK_DIM = 64
B_GROUPS = ((128, 1), (512, 4), (2048, 16))
B_HEADS_PER_GROUP = 4
B_HEADS = B_HEADS_PER_GROUP * len(B_GROUPS)
C_HEADS = 8
C_Q_LORA = 512
C_KV_LORA = 256
C_NOPE = 128
C_ROPE = 64
D_HEADS = 8
MOBA_BLOCK = 256
MOBA_TOPK = 3

LOG2_E = 1.0 / math.log(2.0)
LANES = 128
VMEM_LIMIT = 56 * 1024 * 1024

PLAIN, ROPE128, ROPE64 = 0, 1, 2


def _rms(x, gain):
    return x * lax.rsqrt(jnp.mean(x * x, axis=-1, keepdims=True) + NORM_EPS) * gain


def _params(*semantics):
    return pltpu.CompilerParams(dimension_semantics=semantics, vmem_limit_bytes=VMEM_LIMIT)


def _norm_proj_body(x_ref, g_ref, w_ref, c128_ref, s128_ref, c64_ref, s64_ref, o_ref, xn_ref, stage_ref, *,
                    tile_patterns):
    j = pl.program_id(1)
    tm = o_ref.shape[0]

    @pl.when(j == 0)
    def _():
        xn_ref[...] = _rms(x_ref[...], g_ref[...]).astype(BF16)

    def tile(pattern):
        acc = jnp.dot(xn_ref[...], w_ref[...], preferred_element_type=F32)
        for c, (kind, scale, dil) in enumerate(pattern):
            a = acc[:, _lanes(c)]
            if scale != 1.0:
                a = a * scale
            if kind == ROPE128:
                a = a * c128_ref[...] + pltpu.roll(a, 64, 1) * s128_ref[...]
            elif kind == ROPE64:
                a = a * c64_ref[...] + pltpu.roll(a, 64, 1) * s64_ref[...]
            if dil == 1:
                o_ref[:, _lanes(c)] = a.astype(o_ref.dtype)
            else:
                rows = tm // dil
                stage_ref[c] = a
                for r in range(dil):
                    o_ref[r * rows:(r + 1) * rows, _lanes(c)] = (
                        stage_ref[c, pl.ds(r, rows, stride=dil), :].astype(o_ref.dtype))

    distinct = sorted(set(tile_patterns), key=tile_patterns.index)
    if len(distinct) == 1:
        tile(distinct[0])
    else:
        for pattern in distinct:
            tiles = [t for t, p in enumerate(tile_patterns) if p == pattern]
            cond = functools.reduce(jnp.logical_or, [j == t for t in tiles])
            pl.when(cond)(functools.partial(tile, pattern))


def _norm_proj(x, gain, w, rope, chunk_modes, *, seq, tm, tn, out_dtype):
    t, k = x.shape
    n = w.shape[1]
    per_tile = tn // LANES
    assert t % tm == 0 and n % tn == 0 and seq % tm == 0 and len(chunk_modes) * LANES == n
    tile_patterns = tuple(tuple(chunk_modes[a * per_tile:(a + 1) * per_tile]) for a in range(n // tn))
    seq_tiles = seq // tm
    tab = pl.BlockSpec((tm, LANES), lambda i, j: (i % seq_tiles, 0))
    return pl.pallas_call(
        functools.partial(_norm_proj_body, tile_patterns=tile_patterns),
        grid=(t // tm, n // tn),
        in_specs=[pl.BlockSpec((tm, k), lambda i, j: (i, 0)),
                  pl.BlockSpec((1, k), lambda i, j: (0, 0)),
                  pl.BlockSpec((k, tn), lambda i, j: (0, j)),
                  tab, tab, tab, tab],
        out_specs=pl.BlockSpec((tm, tn), lambda i, j: (i, j)),
        out_shape=jax.ShapeDtypeStruct((t, n), out_dtype),
        scratch_shapes=[pltpu.VMEM((tm, k), BF16), pltpu.VMEM((per_tile, tm, LANES), F32)],
        compiler_params=_params("parallel", "arbitrary"),
        name="norm_proj",
    )(x, gain.reshape(1, k), w, *rope)


def _out_proj_body(a_ref, w_ref, res_ref, o_ref):
    o_ref[...] = res_ref[...] + jnp.dot(a_ref[...], w_ref[...], preferred_element_type=F32)


def _out_proj(a, w, res, *, tm, tn):
    t, k = a.shape
    n = w.shape[1]
    assert t % tm == 0 and n % tn == 0
    return pl.pallas_call(
        _out_proj_body,
        grid=(t // tm, n // tn),
        in_specs=[pl.BlockSpec((tm, k), lambda i, j: (i, 0)),
                  pl.BlockSpec((k, tn), lambda i, j: (0, j)),
                  pl.BlockSpec((tm, tn), lambda i, j: (i, j))],
        out_specs=pl.BlockSpec((tm, tn), lambda i, j: (i, j)),
        out_shape=jax.ShapeDtypeStruct((t, n), F32),
        compiler_params=_params("parallel", "parallel"),
        name="out_proj",
    )(a, w, res)


def _ffn_body(x_ref, g_ref, wg_ref, wu_ref, wd_ref, o_ref, xn_ref):
    j = pl.program_id(1)

    @pl.when(j == 0)
    def _():
        xn_ref[...] = _rms(x_ref[...], g_ref[...]).astype(BF16)
        o_ref[...] = jnp.zeros_like(o_ref)

    xn = xn_ref[...]
    g = jnp.dot(xn, wg_ref[...], preferred_element_type=F32)
    u = jnp.dot(xn, wu_ref[...], preferred_element_type=F32)
    h = (g * jax.nn.sigmoid(g) * u).astype(BF16)
    o_ref[...] += jnp.dot(h, wd_ref[...], preferred_element_type=F32)

    @pl.when(j == pl.num_programs(1) - 1)
    def _():
        o_ref[...] = x_ref[...] + 0.5 * o_ref[...]


def _ffn(x, gain, w_gu, w_down, *, tm, tf):
    t, d = x.shape
    d_ff = w_down.shape[0]
    assert t % tm == 0 and d_ff % tf == 0
    nf = d_ff // tf
    return pl.pallas_call(
        _ffn_body,
        grid=(t // tm, nf),
        in_specs=[pl.BlockSpec((tm, d), lambda i, j: (i, 0)),
                  pl.BlockSpec((1, d), lambda i, j: (0, 0)),
                  pl.BlockSpec((d, tf), lambda i, j: (0, j)),
                  pl.BlockSpec((d, tf), lambda i, j: (0, nf + j)),
                  pl.BlockSpec((tf, d), lambda i, j: (j, 0))],
        out_specs=pl.BlockSpec((tm, d), lambda i, j: (i, 0)),
        out_shape=jax.ShapeDtypeStruct((t, d), F32),
        scratch_shapes=[pltpu.VMEM((tm, d), BF16)],
        compiler_params=_params("parallel", "arbitrary"),
        name="ffn",
    )(x, gain.reshape(1, d), w_gu, w_gu, w_down)


def _ple_body(x_ref, p_ref, g_ref, wg_ref, wp_ref, fg_ref, o_ref, *, final_norm):
    x = x_ref[...]
    xn = _rms(x, g_ref[...]).astype(BF16)
    gate = jax.nn.sigmoid(jnp.dot(xn, wg_ref[...], preferred_element_type=F32))
    proj = jnp.dot(p_ref[...].astype(BF16), wp_ref[...], preferred_element_type=F32)
    y = x + gate * proj
    if final_norm:
        y = _rms(y, fg_ref[...])
    o_ref[...] = y


def _ple(x, p, gain, w_gate, w_proj, final_gain, *, tm, final_norm):
    t, d = x.shape
    pd = p.shape[1]
    assert t % tm == 0
    const = lambda i: (0, 0)
    return pl.pallas_call(
        functools.partial(_ple_body, final_norm=final_norm),
        grid=(t // tm,),
        in_specs=[pl.BlockSpec((tm, d), lambda i: (i, 0)),
                  pl.BlockSpec((tm, pd), lambda i: (i, 0)),
                  pl.BlockSpec((1, d), const),
                  pl.BlockSpec((d, d), const),
                  pl.BlockSpec((pd, d), const),
                  pl.BlockSpec((1, d), const)],
        out_specs=pl.BlockSpec((tm, d), lambda i: (i, 0)),
        out_shape=jax.ShapeDtypeStruct((t, d), F32),
        compiler_params=_params("parallel"),
        name="ple",
    )(x, p, gain.reshape(1, d), w_gate, w_proj, final_gain.reshape(1, d))


def _dot_nt(a, b):
    return lax.dot_general(a, b, (((1,), (1,)), ((), ())), preferred_element_type=F32)


def _lanes(c, width=LANES):
    return slice(c * width, (c + 1) * width)


def _fill_transposed(vt_ref, v_ref, lane_offsets, chunk):
    dv = vt_ref.shape[1]

    def body(j, _):
        start = pl.multiple_of(j * chunk, chunk)
        for c, off in enumerate(lane_offsets):
            vt_ref[c, :, pl.ds(start, chunk)] = v_ref[pl.ds(start, chunk), off:off + dv].T
        return 0

    lax.fori_loop(0, v_ref.shape[0] // chunk, body, 0)


def _flash_scratch(heads, tk, cols, dv):
    return [pltpu.VMEM((heads, tk, cols), F32), pltpu.VMEM((heads, tk, cols), BF16),
            pltpu.VMEM((heads, dv, cols), F32)]


def _causal_diag_masks(tk, tq, cols):
    key_row = lax.broadcasted_iota(jnp.int32, (tk, 1), 0)
    query_col = lax.broadcasted_iota(jnp.int32, (1, cols), 1) % tq
    return [key_row + d * tk <= query_col for d in range(tq // tk)]


def _causal_flash_t(score_fns, vt_ref, s_ref, p_ref, acc_ref, n_past, diag_mask_fn, n_diag, past_mask_fns=None):
    heads = range(len(score_fns))
    _, tk, cols = s_ref.shape

    def stage_scores(block):
        start = pl.multiple_of(block * tk, tk)
        for c in heads:
            s = score_fns[c](start)
            if past_mask_fns is not None:
                s = jnp.where(past_mask_fns[c](block), s, -jnp.inf)
            s_ref[c] = s

    def accumulate(block, alphas):
        start = pl.multiple_of(block * tk, tk)
        for c in heads:
            pv = jnp.dot(vt_ref[c, :, pl.ds(start, tk)], p_ref[c], preferred_element_type=F32)
            acc_ref[c] = alphas[c] * acc_ref[c] + pv

    def softmax(stats, diag=None):
        new_stats, alphas = [], []
        for c in heads:
            m, l = stats[c]
            if diag is None:
                read = lambda: s_ref[c]
            else:
                visible = diag_mask_fn(diag, c)
                read = lambda: jnp.where(visible, s_ref[c], -jnp.inf)
            m_new = jnp.maximum(m, jnp.max(read(), axis=0, keepdims=True))
            alpha = jnp.exp2(m - m_new)
            p = jnp.exp2(read() - m_new)
            p_ref[c] = p.astype(BF16)
            new_stats.append((m_new, alpha * l + jnp.sum(p, axis=0, keepdims=True)))
            alphas.append(alpha)
        return tuple(new_stats), tuple(alphas)

    def past_block(j, carry):
        stats, alphas = carry
        accumulate(jnp.maximum(j - 1, 0), alphas)
        stats, alphas = softmax(stats)
        stage_scores(j + 1)
        return stats, alphas

    for c in heads:
        p_ref[c] = jnp.zeros(p_ref.shape[1:], BF16)
        acc_ref[c] = jnp.zeros(acc_ref.shape[1:], F32)
    stage_scores(0)
    init = (tuple((jnp.full((1, cols), jnp.finfo(F32).min, F32), jnp.zeros((1, cols), F32)) for _ in heads),
            tuple(jnp.ones((1, cols), F32) for _ in heads))
    stats, alphas = lax.fori_loop(0, n_past, past_block, init)

    accumulate(jnp.maximum(n_past - 1, 0), alphas)
    for d in range(n_diag):
        stats, alphas = softmax(stats, diag=d)
        if d + 1 < n_diag:
            stage_scores(n_past + d + 1)
        accumulate(n_past + d, alphas)
    return tuple((stats[c][1], acc_ref[c]) for c in heads)


def _diff_attn_body(lam_ref, subln_ref, q_ref, k_ref, v_ref, o_ref, vt_ref, qt_ref, s_ref, p_ref, acc_ref, *,
                    tq, tk, heads, lam_init):
    qi = pl.program_id(2)

    @pl.when(qi == 0)
    def _():
        _fill_transposed(vt_ref, v_ref, [c * LANES for c in range(heads)], tk)

    is_map0 = (lax.broadcasted_iota(jnp.int32, (LANES, 1), 0) % 64) < 32

    def make_score_fn(c):
        q_t = q_ref[:, _lanes(c)].T
        zero = jnp.zeros_like(q_t)
        qt_ref[c] = jnp.concatenate([jnp.where(is_map0, q_t, zero), jnp.where(is_map0, zero, q_t)], axis=1)
        return lambda start: jnp.dot(k_ref[pl.ds(start, tk), _lanes(c)], qt_ref[c], preferred_element_type=F32)

    diag = _causal_diag_masks(tk, tq, 2 * tq)
    results = _causal_flash_t([make_score_fn(c) for c in range(heads)], vt_ref, s_ref, p_ref, acc_ref,
                              qi * (tq // tk), lambda d, c: diag[d], tq // tk)

    lp = lam_ref[...]
    lam = (jnp.exp(jnp.sum(lp[0:1] * lp[1:2], axis=-1, keepdims=True))
           - jnp.exp(jnp.sum(lp[2:3] * lp[3:4], axis=-1, keepdims=True)) + lam_init)
    for c, (l, acc) in enumerate(results):
        o_t = acc / l
        o = (o_t[:, :tq] - lam * o_t[:, tq:]).T
        o_ref[:, _lanes(c)] = (_rms(o, subln_ref[...]) * (1.0 - lam_init)).astype(o_ref.dtype)


def _diff_attention(qkv, lam_params, subln, *, batch, seq, tq, tk, heads, q_col, k_col, v_col, lam_init):
    nq = seq // tq
    width = heads * LANES
    assert A_HEADS % heads == 0 and q_col % heads == 0 and k_col % heads == 0 and v_col % heads == 0
    assert seq % tq == 0 and tq % tk == 0
    return pl.pallas_call(
        functools.partial(_diff_attn_body, tq=tq, tk=tk, heads=heads, lam_init=lam_init),
        grid=(batch, A_HEADS // heads, nq),
        in_specs=[pl.BlockSpec((4, A_QK_DIM), lambda b, h, i: (0, 0)),
                  pl.BlockSpec((1, LANES), lambda b, h, i: (0, 0)),
                  pl.BlockSpec((tq, width), lambda b, h, i: (b * nq + i, q_col // heads + h)),
                  pl.BlockSpec((seq, width), lambda b, h, i: (b, k_col // heads + h)),
                  pl.BlockSpec((seq, width), lambda b, h, i: (b, v_col // heads + h))],
        out_specs=pl.BlockSpec((tq, width), lambda b, h, i: (b * nq + i, h)),
        out_shape=jax.ShapeDtypeStruct((batch * seq, A_HEADS * LANES), BF16),
        scratch_shapes=[pltpu.VMEM((heads, LANES, seq), BF16), pltpu.VMEM((heads, LANES, 2 * tq), BF16)]
        + _flash_scratch(heads, tk, 2 * tq, LANES),
        compiler_params=_params("parallel", "parallel", "arbitrary"),
        name="diff_attn",
    )(lam_params, subln.reshape(1, LANES), qkv, qkv, qkv)


def _tile_index(ref, i, tile):
    per_slab = ref.shape[1]
    if per_slab >= tile:
        runs = per_slab // tile
        return i // runs, pl.ds(pl.multiple_of((i % runs) * tile, tile), tile)
    slabs = tile // per_slab
    return pl.ds(i * slabs, slabs), slice(None)


def _load_tile(ref, i, c, tile):
    slab, rows = _tile_index(ref, i, tile)
    return ref[slab, rows, _lanes(c)].reshape(tile, LANES)


def _store_tile(ref, i, c, tile, value):
    slab, rows = _tile_index(ref, i, tile)
    per_slab = ref.shape[1]
    ref[slab, rows, _lanes(c)] = value if per_slab >= tile else value.reshape(tile // per_slab, per_slab, LANES)


def _dilated_body(q_ref, k_ref, v_ref, o_ref, lse_ref, *, tile, heads):
    length = q_ref.shape[0] * q_ref.shape[1]
    row = lax.broadcasted_iota(jnp.int32, (tile, 1), 0)

    def attend(i, c, k, v, valid):
        s = jnp.where(valid, _dot_nt(_load_tile(q_ref, i, c, tile), k), -jnp.inf)
        m = jnp.max(s, axis=-1, keepdims=True)
        p = jnp.exp(s - m)
        l = jnp.sum(p, axis=-1, keepdims=True)
        o = jnp.dot(p.astype(BF16), v, preferred_element_type=F32) / l
        _store_tile(o_ref, i, c, tile, o)
        _store_tile(lse_ref, i, c, tile, jnp.broadcast_to(m + jnp.log(l), (tile, LANES)))

    col = lax.broadcasted_iota(jnp.int32, (1, tile), 1)
    for c in range(heads):
        attend(0, c, _load_tile(k_ref, 0, c, tile), _load_tile(v_ref, 0, c, tile), col <= row)

    col2 = lax.broadcasted_iota(jnp.int32, (1, 2 * tile), 1)
    band = jnp.logical_and(col2 >= row, col2 <= row + tile)

    def body(i, _):
        for c in range(heads):
            k = jnp.concatenate([_load_tile(k_ref, i - 1, c, tile), _load_tile(k_ref, i, c, tile)], axis=0)
            v = jnp.concatenate([_load_tile(v_ref, i - 1, c, tile), _load_tile(v_ref, i, c, tile)], axis=0)
            attend(i, c, k, v, band)
        return 0

    lax.fori_loop(1, length // tile, body, 0)


def _dilated_group(qkv, *, batch, seq, row_tile, group, dil, heads, q_col, k_col, v_col):
    hpg = B_HEADS_PER_GROUP
    tile = B_GROUPS[group][0] // dil
    slabs, per_slab = seq // row_tile, row_tile // dil
    assert tile == LANES and hpg % heads == 0 and (seq // dil) % tile == 0
    assert per_slab % tile == 0 or tile % per_slab == 0
    width = heads * LANES
    view = qkv.reshape(batch, slabs, dil, per_slab, qkv.shape[1])

    def in_spec(col):
        first = col + group * hpg
        assert first % heads == 0
        return pl.BlockSpec((None, slabs, None, per_slab, width),
                            lambda b, h, r: (b, 0, r, 0, first // heads + h))

    out_spec = pl.BlockSpec((None, slabs, None, per_slab, width), lambda b, h, r: (b, 0, r, 0, h))
    out_sds = jax.ShapeDtypeStruct((batch, slabs, dil, per_slab, hpg * LANES), F32)
    o, lse = pl.pallas_call(
        functools.partial(_dilated_body, tile=tile, heads=heads),
        grid=(batch, hpg // heads, dil),
        in_specs=[in_spec(q_col), in_spec(k_col), in_spec(v_col)],
        out_specs=[out_spec, out_spec],
        out_shape=[out_sds, out_sds],
        compiler_params=_params("parallel", "parallel", "parallel"),
        name=f"dilated_g{group}",
    )(view, view, view)
    shape = (batch * slabs, dil, per_slab, hpg * LANES)
    return o.reshape(shape), lse.reshape(shape)


def _merge_body(o0_ref, o1_ref, o2_ref, l0_ref, l1_ref, l2_ref, out_ref, nat_ref):
    def natural(ref, slot, c):
        dil, rows = ref.shape[0], ref.shape[1]
        if dil == 1:
            return ref[0, :, _lanes(c)]
        for r in range(dil):
            nat_ref[slot, pl.ds(r, rows, stride=dil), :] = ref[r, :, _lanes(c)]
        return nat_ref[slot]

    for c in range(out_ref.shape[1] // LANES):
        o0, o1, o2 = natural(o0_ref, None, c), natural(o1_ref, 0, c), natural(o2_ref, 1, c)
        l0, l1, l2 = natural(l0_ref, None, c), natural(l1_ref, 2, c), natural(l2_ref, 3, c)
        m = jnp.maximum(jnp.maximum(l0, l1), l2)
        w0, w1, w2 = jnp.exp(l0 - m), jnp.exp(l1 - m), jnp.exp(l2 - m)
        out_ref[:, _lanes(c)] = ((w0 * o0 + w1 * o1 + w2 * o2) / (w0 + w1 + w2)).astype(out_ref.dtype)


def _merge_groups(outs, lses):
    tiles, _, _, width = outs[0].shape
    row_tile = outs[0].shape[1] * outs[0].shape[2]
    specs = [pl.BlockSpec((None,) + a.shape[1:], lambda i: (i, 0, 0, 0)) for a in (*outs, *lses)]
    return pl.pallas_call(
        _merge_body,
        grid=(tiles,),
        in_specs=specs,
        out_specs=pl.BlockSpec((row_tile, width), lambda i: (i, 0)),
        out_shape=jax.ShapeDtypeStruct((tiles * row_tile, width), BF16),
        scratch_shapes=[pltpu.VMEM((4, row_tile, LANES), F32)],
        compiler_params=_params("parallel"),
        name="dilated_merge",
    )(*outs, *lses)


def _mla_body(q_ref, kv_ref, kr_ref, o_ref, vt_ref, qt_ref, s_ref, p_ref, acc_ref, *, tq, tk, heads):
    qi = pl.program_id(2)

    @pl.when(qi == 0)
    def _():
        _fill_transposed(vt_ref, kv_ref, [(2 * c + 1) * LANES for c in range(heads)], tk)

    def make_score_fn(c):
        qt_ref[c] = q_ref[:, _lanes(c, 2 * LANES)].T

        def score(start):
            k = jnp.concatenate([kv_ref[pl.ds(start, tk), _lanes(2 * c)], kr_ref[pl.ds(start, tk), :]], axis=1)
            return jnp.dot(k, qt_ref[c], preferred_element_type=F32)

        return score

    diag = _causal_diag_masks(tk, tq, tq)
    results = _causal_flash_t([make_score_fn(c) for c in range(heads)], vt_ref, s_ref, p_ref, acc_ref,
                              qi * (tq // tk), lambda d, c: diag[d], tq // tk)
    for c, (l, acc) in enumerate(results):
        o_ref[:, _lanes(c)] = (acc / l).T.astype(o_ref.dtype)


def _mla_attention(q, kv, k_rope_src, *, batch, seq, tq, tk, heads, k_rope_col):
    nq = seq // tq
    assert C_HEADS % heads == 0 and seq % tq == 0 and tq % tk == 0
    return pl.pallas_call(
        functools.partial(_mla_body, tq=tq, tk=tk, heads=heads),
        grid=(batch, C_HEADS // heads, nq),
        in_specs=[pl.BlockSpec((tq, heads * 2 * LANES), lambda b, h, i: (b * nq + i, h)),
                  pl.BlockSpec((seq, heads * 2 * LANES), lambda b, h, i: (b, h)),
                  pl.BlockSpec((seq, LANES), lambda b, h, i: (b, k_rope_col))],
        out_specs=pl.BlockSpec((tq, heads * LANES), lambda b, h, i: (b * nq + i, h)),
        out_shape=jax.ShapeDtypeStruct((batch * seq, C_HEADS * LANES), BF16),
        scratch_shapes=[pltpu.VMEM((heads, LANES, seq), BF16), pltpu.VMEM((heads, 2 * LANES, tq), BF16)]
        + _flash_scratch(heads, tk, tq, LANES),
        compiler_params=_params("parallel", "parallel", "arbitrary"),
        name="mla_attn",
    )(q, kv, k_rope_src)


def _moba_body(q_ref, k_ref, v_ref, o_ref, kmean_ref, vt_ref, sel_ref, qt_ref, s_ref, p_ref, acc_ref, *,
               n_blk, tq, heads):
    qi = pl.program_id(2)
    blk = MOBA_BLOCK
    n_own = tq // blk

    @pl.when(qi == 0)
    def _():
        for c in range(heads):
            k_all = k_ref[:, _lanes(c)].astype(F32).reshape(n_blk, blk, LANES)
            kmean_ref[c] = jnp.mean(k_all, axis=1)
        _fill_transposed(vt_ref, v_ref, [c * LANES for c in range(heads)], blk)

    blk_id = lax.broadcasted_iota(jnp.int32, (n_blk, 1), 0)
    query_col = lax.broadcasted_iota(jnp.int32, (1, tq), 1)
    own_in_tile = query_col // blk
    past = blk_id < qi * n_own + own_in_tile
    causal = lax.broadcasted_iota(jnp.int32, (blk, 1), 0) <= query_col % blk

    def make_score_fn(c):
        q_t = q_ref[:, _lanes(c)].T
        km = kmean_ref[c]
        km_hi = km.astype(BF16)
        rem = km - km_hi.astype(F32)
        km_mid = rem.astype(BF16)
        km_lo = (rem - km_mid.astype(F32)).astype(BF16)
        gate = (jnp.dot(km_hi, q_t, preferred_element_type=F32) + jnp.dot(km_mid, q_t, preferred_element_type=F32)
                + jnp.dot(km_lo, q_t, preferred_element_type=F32))
        gate = jnp.where(past, gate, -jnp.inf)
        selected = jnp.zeros(gate.shape, F32)
        for _ in range(min(MOBA_TOPK, n_blk - 1)):
            best = jnp.max(gate, axis=0, keepdims=True)
            first = jnp.min(jnp.where(gate == best, blk_id, n_blk), axis=0, keepdims=True)
            pick = blk_id == first
            selected = jnp.where(jnp.logical_and(pick, past), 1.0, selected)
            gate = jnp.where(pick, -jnp.inf, gate)
        sel_ref[c] = selected
        qt_ref[c] = q_t
        return lambda start: jnp.dot(k_ref[pl.ds(start, blk), _lanes(c)], qt_ref[c], preferred_element_type=F32)

    score_fns = [make_score_fn(c) for c in range(heads)]
    n_past = qi * n_own
    chosen_fns = [lambda n, c=c: jnp.logical_or(sel_ref[c, pl.ds(n, 1), :] > 0.0, n >= n_past)
                  for c in range(heads)]

    def own_block_mask(d, c):
        chosen = sel_ref[c, pl.ds(n_past + d, 1), :] > 0.0
        return jnp.logical_or(jnp.logical_and(own_in_tile == d, causal),
                              jnp.logical_and(own_in_tile > d, chosen))

    results = _causal_flash_t(score_fns, vt_ref, s_ref, p_ref, acc_ref, n_past, own_block_mask, n_own,
                              past_mask_fns=chosen_fns)
    for c, (l, acc) in enumerate(results):
        o_ref[:, _lanes(c)] = (acc / l).T.astype(o_ref.dtype)


def _moba_attention(qkv, *, batch, seq, tq, heads, q_col, k_col, v_col):
    blk = MOBA_BLOCK
    n_blk = seq // blk
    nq = seq // tq
    width = heads * LANES
    assert seq % tq == 0 and tq % blk == 0 and D_HEADS % heads == 0
    assert q_col % heads == 0 and k_col % heads == 0 and v_col % heads == 0
    return pl.pallas_call(
        functools.partial(_moba_body, n_blk=n_blk, tq=tq, heads=heads),
        grid=(batch, D_HEADS // heads, nq),
        in_specs=[pl.BlockSpec((tq, width), lambda b, h, i: (b * nq + i, q_col // heads + h)),
                  pl.BlockSpec((seq, width), lambda b, h, i: (b, k_col // heads + h)),
                  pl.BlockSpec((seq, width), lambda b, h, i: (b, v_col // heads + h))],
        out_specs=pl.BlockSpec((tq, width), lambda b, h, i: (b * nq + i, h)),
        out_shape=jax.ShapeDtypeStruct((batch * seq, D_HEADS * LANES), BF16),
        scratch_shapes=[pltpu.VMEM((heads, n_blk, LANES), F32),
                        pltpu.VMEM((heads, LANES, seq), BF16),
                        pltpu.VMEM((heads, n_blk, tq), F32),
                        pltpu.VMEM((heads, LANES, tq), BF16)] + _flash_scratch(heads, blk, tq, LANES),
        compiler_params=_params("parallel", "parallel", "arbitrary"),
        name="moba_attn",
    )(qkv, qkv, qkv)


def _rope_tables(seq):
    pos = jnp.arange(seq, dtype=F32)[:, None]

    def angles(dim):
        inv = 1.0 / (ROPE_THETA ** (jnp.arange(0, dim, 2, dtype=F32) / dim))
        return pos * inv[None, :]

    a128 = angles(HEAD_DIM)
    a64 = angles(A_QK_DIM)
    c128 = jnp.concatenate([jnp.cos(a128)] * 2, axis=1)
    s128 = jnp.concatenate([-jnp.sin(a128), jnp.sin(a128)], axis=1)
    c64 = jnp.concatenate([jnp.cos(a64)] * 4, axis=1)
    s64 = jnp.concatenate([-jnp.sin(a64)] * 2 + [jnp.sin(a64)] * 2, axis=1)
    return c128, s128, c64, s64


def _interleave_diff_heads(w):
    d = w.shape[0]
    return w.reshape(d, A_HEADS, 2, 2, 32).transpose(0, 1, 3, 2, 4).reshape(d, A_HEADS * LANES)


def _spread_rope64(w):
    z = jnp.zeros(w.shape[:-1] + (32,), w.dtype)
    return jnp.concatenate([w[..., :32], z, w[..., 32:], z], axis=-1)


def _ab_mixer(x, gain, w_in, lam_params, subln, w_out, layer_idx, rope, *, batch, seq):
    aw = A_HEADS * LANES
    w = jnp.concatenate([_interleave_diff_heads(w_in[:, :aw]), _interleave_diff_heads(w_in[:, aw:2 * aw]),
                         w_in[:, 2 * aw:]], axis=1).astype(BF16)
    a_scale = A_QK_DIM ** -0.5 * LOG2_E
    b_scale = HEAD_DIM ** -0.5
    dils = [dil for _, dil in B_GROUPS for _ in range(B_HEADS_PER_GROUP)]
    modes = ([(ROPE64, a_scale, 1)] * A_HEADS + [(ROPE64, 1.0, 1)] * A_HEADS + [(PLAIN, 1.0, 1)] * A_HEADS
             + [(ROPE128, b_scale, d) for d in dils] + [(ROPE128, 1.0, d) for d in dils]
             + [(PLAIN, 1.0, d) for d in dils])
    row_tile = 1024
    qkv = _norm_proj(x, gain, w, rope, modes, seq=seq, tm=row_tile, tn=768, out_dtype=BF16)
    lam_init = 0.8 - 0.6 * math.exp(-0.3 * layer_idx)
    oa = _diff_attention(qkv, lam_params, subln, batch=batch, seq=seq, tq=512, tk=256, heads=2,
                         q_col=0, k_col=A_HEADS, v_col=2 * A_HEADS, lam_init=lam_init)
    b0 = 3 * A_HEADS
    outs, lses = [], []
    for g, (_, dil) in enumerate(B_GROUPS):
        o, lse = _dilated_group(qkv, batch=batch, seq=seq, row_tile=row_tile, group=g, dil=dil,
                                heads=2 if dil == 1 else 4,
                                q_col=b0, k_col=b0 + B_HEADS, v_col=b0 + 2 * B_HEADS)
        outs.append(o)
        lses.append(lse)
    ob = _merge_groups(outs, lses)
    o = jnp.concatenate([oa, ob], axis=1)
    return _out_proj(o, w_out.astype(BF16), x, tm=512, tn=1024)


def _cd_mixer(x, gain, w_in, q_norm, w_uq, kv_norm, w_ukv, w_out, rope, *, batch, seq):
    lat = C_Q_LORA + C_KV_LORA
    d_scale = HEAD_DIM ** -0.5 * LOG2_E
    latent = _norm_proj(x, gain, w_in[:, :lat].astype(BF16), rope, [(PLAIN, 1.0, 1)] * (lat // LANES),
                        seq=seq, tm=512, tn=lat // 2, out_dtype=F32)
    w_main = jnp.concatenate([w_in[:, lat + C_ROPE:], _spread_rope64(w_in[:, lat:lat + C_ROPE])],
                             axis=1).astype(BF16)
    modes = ([(ROPE128, d_scale, 1)] * D_HEADS + [(ROPE128, 1.0, 1)] * D_HEADS + [(PLAIN, 1.0, 1)] * D_HEADS
             + [(ROPE64, 1.0, 1)])
    main = _norm_proj(x, gain, w_main, rope, modes, seq=seq, tm=1024, tn=5 * LANES, out_dtype=BF16)

    c_scale = (C_NOPE + C_ROPE) ** -0.5 * LOG2_E
    w_q = w_uq.reshape(C_Q_LORA, C_HEADS, C_NOPE + C_ROPE)
    w_q = jnp.concatenate([w_q[..., :C_NOPE], _spread_rope64(w_q[..., C_NOPE:])], axis=-1)
    w_q = w_q.reshape(C_Q_LORA, C_HEADS * 2 * LANES).astype(BF16)
    q_modes = [(PLAIN, c_scale, 1), (ROPE64, c_scale, 1)] * C_HEADS
    qc = _norm_proj(latent[:, :C_Q_LORA], q_norm, w_q, rope, q_modes, seq=seq, tm=512, tn=512, out_dtype=BF16)
    kv = _norm_proj(latent[:, C_Q_LORA:], kv_norm, w_ukv.astype(BF16), rope,
                    [(PLAIN, 1.0, 1)] * (2 * C_HEADS), seq=seq, tm=512, tn=512, out_dtype=BF16)
    oc = _mla_attention(qc, kv, main, batch=batch, seq=seq, tq=512, tk=256, heads=4, k_rope_col=3 * D_HEADS)
    od = _moba_attention(main, batch=batch, seq=seq, tq=512, heads=4,
                         q_col=0, k_col=D_HEADS, v_col=2 * D_HEADS)
    o = jnp.concatenate([oc, od], axis=1)
    return _out_proj(o, w_out.astype(BF16), x, tm=512, tn=1024)


def kernel(x, p, ffn_norm, ffn_w_gu, ffn_w_down, mix_norm, ab_w_in, ab_lambda, ab_subln, ab_w_out,
           cd_w_in, cd_q_norm, cd_w_uq, cd_kv_norm, cd_w_ukv, cd_w_out, ple_norm, ple_w_gate,
           ple_w_proj, final_norm):
    batch, seq, d = x.shape
    depth = p.shape[0]
    rope = _rope_tables(seq)
    x = x.reshape(batch * seq, d)
    for i in range(depth):
        j = i // 2
        x = _ffn(x, ffn_norm[i, 0], ffn_w_gu[i, 0].astype(BF16), ffn_w_down[i, 0].astype(BF16), tm=1024, tf=512)
        if i % 2 == 0:
            x = _ab_mixer(x, mix_norm[i], ab_w_in[j], ab_lambda[j], ab_subln[j], ab_w_out[j], i, rope,
                          batch=batch, seq=seq)
        else:
            x = _cd_mixer(x, mix_norm[i], cd_w_in[j], cd_q_norm[j], cd_w_uq[j], cd_kv_norm[j], cd_w_ukv[j],
                          cd_w_out[j], rope, batch=batch, seq=seq)
        x = _ffn(x, ffn_norm[i, 1], ffn_w_gu[i, 1].astype(BF16), ffn_w_down[i, 1].astype(BF16), tm=1024, tf=512)
        x = _ple(x, p[i].reshape(batch * seq, -1), ple_norm[i], ple_w_gate[i].astype(BF16),
                 ple_w_proj[i].astype(BF16), final_norm, tm=256, final_norm=(i == depth - 1))
    return x.reshape(batch, seq, d)
```

```python
import functools
import math

import jax
import jax.numpy as jnp
from jax import lax
from jax.experimental import pallas as pl
from jax.experimental.pallas import tpu as pltpu

F32 = jnp.float32
BF16 = jnp.bfloat16

D_MODEL = 2048
D_FF = 5632
PLE_DIM = 256
HEAD_DIM = 128
ROPE_THETA = 10000.0
NORM_EPS = 1e-6

A_HEADS = 8
A_QK_DIM = 64
B_GROUPS = ((128, 1), (512, 4), (2048, 16))
B_HEADS_PER_GROUP = 4
B_HEADS = B_HEADS_PER_GROUP * len(B_GROUPS)
C_HEADS = 8
C_Q_LORA = 512
C_KV_LORA = 256
C_NOPE = 128
C_ROPE = 64
D_HEADS = 8
MOBA_BLOCK = 256
MOBA_TOPK = 3

LOG2_E = 1.0 / math.log(2.0)
LANES = 128
VMEM_LIMIT = 56 * 1024 * 1024

PLAIN, ROPE128, ROPE64 = 0, 1, 2


def _rms(x, gain):
    return x * lax.rsqrt(jnp.mean(x * x, axis=-1, keepdims=True) + NORM_EPS) * gain


def _params(*semantics):
    return pltpu.CompilerParams(dimension_semantics=semantics, vmem_limit_bytes=VMEM_LIMIT)


def _norm_proj_body(x_ref, g_ref, w_ref, c128_ref, s128_ref, c64_ref, s64_ref, o_ref, xn_ref, stage_ref, *,
                    tile_patterns):
    j = pl.program_id(1)
    tm = o_ref.shape[0]

    @pl.when(j == 0)
    def _():
        xn_ref[...] = _rms(x_ref[...], g_ref[...]).astype(BF16)

    def tile(pattern):
        acc = jnp.dot(xn_ref[...], w_ref[...], preferred_element_type=F32)
        for c, (kind, scale, dil) in enumerate(pattern):
            a = acc[:, _lanes(c)]
            if scale != 1.0:
                a = a * scale
            if kind == ROPE128:
                a = a * c128_ref[...] + pltpu.roll(a, 64, 1) * s128_ref[...]
            elif kind == ROPE64:
                a = a * c64_ref[...] + pltpu.roll(a, 64, 1) * s64_ref[...]
            if dil == 1:
                o_ref[:, _lanes(c)] = a.astype(o_ref.dtype)
            else:
                rows = tm // dil
                stage_ref[c] = a
                for r in range(dil):
                    o_ref[r * rows:(r + 1) * rows, _lanes(c)] = (
                        stage_ref[c, pl.ds(r, rows, stride=dil), :].astype(o_ref.dtype))

    distinct = sorted(set(tile_patterns), key=tile_patterns.index)
    if len(distinct) == 1:
        tile(distinct[0])
    else:
        for pattern in distinct:
            tiles = [t for t, p in enumerate(tile_patterns) if p == pattern]
            cond = functools.reduce(jnp.logical_or, [j == t for t in tiles])
            pl.when(cond)(functools.partial(tile, pattern))


def _norm_proj(x, gain, w, rope, chunk_modes, *, seq, tm, tn, out_dtype):
    t, k = x.shape
    n = w.shape[1]
    per_tile = tn // LANES
    assert t % tm == 0 and n % tn == 0 and seq % tm == 0 and len(chunk_modes) * LANES == n
    tile_patterns = tuple(tuple(chunk_modes[a * per_tile:(a + 1) * per_tile]) for a in range(n // tn))
    seq_tiles = seq // tm
    tab = pl.BlockSpec((tm, LANES), lambda i, j: (i % seq_tiles, 0))
    return pl.pallas_call(
        functools.partial(_norm_proj_body, tile_patterns=tile_patterns),
        grid=(t // tm, n // tn),
        in_specs=[pl.BlockSpec((tm, k), lambda i, j: (i, 0)),
                  pl.BlockSpec((1, k), lambda i, j: (0, 0)),
                  pl.BlockSpec((k, tn), lambda i, j: (0, j)),
                  tab, tab, tab, tab],
        out_specs=pl.BlockSpec((tm, tn), lambda i, j: (i, j)),
        out_shape=jax.ShapeDtypeStruct((t, n), out_dtype),
        scratch_shapes=[pltpu.VMEM((tm, k), BF16), pltpu.VMEM((per_tile, tm, LANES), F32)],
        compiler_params=_params("parallel", "arbitrary"),
        name="norm_proj",
    )(x, gain.reshape(1, k), w, *rope)


def _out_proj_body(a_ref, w_ref, res_ref, o_ref):
    o_ref[...] = res_ref[...] + jnp.dot(a_ref[...], w_ref[...], preferred_element_type=F32)


def _out_proj(a, w, res, *, tm, tn):
    t, k = a.shape
    n = w.shape[1]
    assert t % tm == 0 and n % tn == 0
    return pl.pallas_call(
        _out_proj_body,
        grid=(t // tm, n // tn),
        in_specs=[pl.BlockSpec((tm, k), lambda i, j: (i, 0)),
                  pl.BlockSpec((k, tn), lambda i, j: (0, j)),
                  pl.BlockSpec((tm, tn), lambda i, j: (i, j))],
        out_specs=pl.BlockSpec((tm, tn), lambda i, j: (i, j)),
        out_shape=jax.ShapeDtypeStruct((t, n), F32),
        compiler_params=_params("parallel", "parallel"),
        name="out_proj",
    )(a, w, res)


def _ffn_body(x_ref, g_ref, wg_ref, wu_ref, wd_ref, o_ref, xn_ref):
    j = pl.program_id(1)

    @pl.when(j == 0)
    def _():
        xn_ref[...] = _rms(x_ref[...], g_ref[...]).astype(BF16)
        o_ref[...] = jnp.zeros_like(o_ref)

    xn = xn_ref[...]
    g = jnp.dot(xn, wg_ref[...], preferred_element_type=F32)
    u = jnp.dot(xn, wu_ref[...], preferred_element_type=F32)
    h = (g * jax.nn.sigmoid(g) * u).astype(BF16)
    o_ref[...] += jnp.dot(h, wd_ref[...], preferred_element_type=F32)

    @pl.when(j == pl.num_programs(1) - 1)
    def _():
        o_ref[...] = x_ref[...] + 0.5 * o_ref[...]


def _ffn(x, gain, w_gu, w_down, *, tm, tf):
    t, d = x.shape
    d_ff = w_down.shape[0]
    assert t % tm == 0 and d_ff % tf == 0
    nf = d_ff // tf
    return pl.pallas_call(
        _ffn_body,
        grid=(t // tm, nf),
        in_specs=[pl.BlockSpec((tm, d), lambda i, j: (i, 0)),
                  pl.BlockSpec((1, d), lambda i, j: (0, 0)),
                  pl.BlockSpec((d, tf), lambda i, j: (0, j)),
                  pl.BlockSpec((d, tf), lambda i, j: (0, nf + j)),
                  pl.BlockSpec((tf, d), lambda i, j: (j, 0))],
        out_specs=pl.BlockSpec((tm, d), lambda i, j: (i, 0)),
        out_shape=jax.ShapeDtypeStruct((t, d), F32),
        scratch_shapes=[pltpu.VMEM((tm, d), BF16)],
        compiler_params=_params("parallel", "arbitrary"),
        name="ffn",
    )(x, gain.reshape(1, d), w_gu, w_gu, w_down)


def _ple_body(x_ref, p_ref, g_ref, wg_ref, wp_ref, fg_ref, o_ref, *, final_norm):
    x = x_ref[...]
    xn = _rms(x, g_ref[...]).astype(BF16)
    gate = jax.nn.sigmoid(jnp.dot(xn, wg_ref[...], preferred_element_type=F32))
    proj = jnp.dot(p_ref[...].astype(BF16), wp_ref[...], preferred_element_type=F32)
    y = x + gate * proj
    if final_norm:
        y = _rms(y, fg_ref[...])
    o_ref[...] = y


def _ple(x, p, gain, w_gate, w_proj, final_gain, *, tm, final_norm):
    t, d = x.shape
    pd = p.shape[1]
    assert t % tm == 0
    const = lambda i: (0, 0)
    return pl.pallas_call(
        functools.partial(_ple_body, final_norm=final_norm),
        grid=(t // tm,),
        in_specs=[pl.BlockSpec((tm, d), lambda i: (i, 0)),
                  pl.BlockSpec((tm, pd), lambda i: (i, 0)),
                  pl.BlockSpec((1, d), const),
                  pl.BlockSpec((d, d), const),
                  pl.BlockSpec((pd, d), const),
                  pl.BlockSpec((1, d), const)],
        out_specs=pl.BlockSpec((tm, d), lambda i: (i, 0)),
        out_shape=jax.ShapeDtypeStruct((t, d), F32),
        compiler_params=_params("parallel"),
        name="ple",
    )(x, p, gain.reshape(1, d), w_gate, w_proj, final_gain.reshape(1, d))


def _dot_nt(a, b):
    return lax.dot_general(a, b, (((1,), (1,)), ((), ())), preferred_element_type=F32)


def _lanes(c, width=LANES):
    return slice(c * width, (c + 1) * width)


def _fill_transposed(vt_ref, v_ref, lane_offsets, chunk):
    dv = vt_ref.shape[1]

    def body(j, _):
        start = pl.multiple_of(j * chunk, chunk)
        for c, off in enumerate(lane_offsets):
            vt_ref[c, :, pl.ds(start, chunk)] = v_ref[pl.ds(start, chunk), off:off + dv].T
        return 0

    lax.fori_loop(0, v_ref.shape[0] // chunk, body, 0)


def _flash_scratch(heads, tk, cols, dv):
    return [pltpu.VMEM((heads, tk, cols), F32), pltpu.VMEM((heads, tk, cols), BF16),
            pltpu.VMEM((heads, dv, cols), F32)]


def _causal_diag_masks(tk, tq, cols):
    key_row = lax.broadcasted_iota(jnp.int32, (tk, 1), 0)
    query_col = lax.broadcasted_iota(jnp.int32, (1, cols), 1) % tq
    return [key_row + d * tk <= query_col for d in range(tq // tk)]


def _causal_flash_t(score_fns, vt_ref, s_ref, p_ref, acc_ref, n_past, diag_mask_fn, n_diag, past_mask_fns=None):
    heads = range(len(score_fns))
    _, tk, cols = s_ref.shape

    def stage_scores(block):
        start = pl.multiple_of(block * tk, tk)
        for c in heads:
            s = score_fns[c](start)
            if past_mask_fns is not None:
                s = jnp.where(past_mask_fns[c](block), s, -jnp.inf)
            s_ref[c] = s

    def accumulate(block, alphas):
        start = pl.multiple_of(block * tk, tk)
        for c in heads:
            pv = jnp.dot(vt_ref[c, :, pl.ds(start, tk)], p_ref[c], preferred_element_type=F32)
            acc_ref[c] = alphas[c] * acc_ref[c] + pv

    def softmax(stats, diag=None):
        new_stats, alphas = [], []
        for c in heads:
            m, l = stats[c]
            if diag is None:
                read = lambda: s_ref[c]
            else:
                visible = diag_mask_fn(diag, c)
                read = lambda: jnp.where(visible, s_ref[c], -jnp.inf)
            m_new = jnp.maximum(m, jnp.max(read(), axis=0, keepdims=True))
            alpha = jnp.exp2(m - m_new)
            p = jnp.exp2(read() - m_new)
            p_ref[c] = p.astype(BF16)
            new_stats.append((m_new, alpha * l + jnp.sum(p, axis=0, keepdims=True)))
            alphas.append(alpha)
        return tuple(new_stats), tuple(alphas)

    def past_block(j, carry):
        stats, alphas = carry
        accumulate(jnp.maximum(j - 1, 0), alphas)
        stats, alphas = softmax(stats)
        stage_scores(j + 1)
        return stats, alphas

    for c in heads:
        p_ref[c] = jnp.zeros(p_ref.shape[1:], BF16)
        acc_ref[c] = jnp.zeros(acc_ref.shape[1:], F32)
    stage_scores(0)
    init = (tuple((jnp.full((1, cols), jnp.finfo(F32).min, F32), jnp.zeros((1, cols), F32)) for _ in heads),
            tuple(jnp.ones((1, cols), F32) for _ in heads))
    stats, alphas = lax.fori_loop(0, n_past, past_block, init)

    accumulate(jnp.maximum(n_past - 1, 0), alphas)
    for d in range(n_diag):
        stats, alphas = softmax(stats, diag=d)
        if d + 1 < n_diag:
            stage_scores(n_past + d + 1)
        accumulate(n_past + d, alphas)
    return tuple((stats[c][1], acc_ref[c]) for c in heads)


def _diff_attn_body(lam_ref, subln_ref, q_ref, k_ref, v_ref, o_ref, vt_ref, qt_ref, s_ref, p_ref, acc_ref, *,
                    tq, tk, heads, lam_init):
    qi = pl.program_id(2)

    @pl.when(qi == 0)
    def _():
        _fill_transposed(vt_ref, v_ref, [c * LANES for c in range(heads)], tk)

    is_map0 = (lax.broadcasted_iota(jnp.int32, (LANES, 1), 0) % 64) < 32

    def make_score_fn(c):
        q_t = q_ref[:, _lanes(c)].T
        zero = jnp.zeros_like(q_t)
        qt_ref[c] = jnp.concatenate([jnp.where(is_map0, q_t, zero), jnp.where(is_map0, zero, q_t)], axis=1)
        return lambda start: jnp.dot(k_ref[pl.ds(start, tk), _lanes(c)], qt_ref[c], preferred_element_type=F32)

    diag = _causal_diag_masks(tk, tq, 2 * tq)
    results = _causal_flash_t([make_score_fn(c) for c in range(heads)], vt_ref, s_ref, p_ref, acc_ref,
                              qi * (tq // tk), lambda d, c: diag[d], tq // tk)

    lp = lam_ref[...]
    lam = (jnp.exp(jnp.sum(lp[0:1] * lp[1:2], axis=-1, keepdims=True))
           - jnp.exp(jnp.sum(lp[2:3] * lp[3:4], axis=-1, keepdims=True)) + lam_init)
    for c, (l, acc) in enumerate(results):
        o_t = acc / l
        o = (o_t[:, :tq] - lam * o_t[:, tq:]).T
        o_ref[:, _lanes(c)] = (_rms(o, subln_ref[...]) * (1.0 - lam_init)).astype(o_ref.dtype)


def _diff_attention(qkv, lam_params, subln, *, batch, seq, tq, tk, heads, q_col, k_col, v_col, lam_init):
    nq = seq // tq
    width = heads * LANES
    assert A_HEADS % heads == 0 and q_col % heads == 0 and k_col % heads == 0 and v_col % heads == 0
    assert seq % tq == 0 and tq % tk == 0
    return pl.pallas_call(
        functools.partial(_diff_attn_body, tq=tq, tk=tk, heads=heads, lam_init=lam_init),
        grid=(batch, A_HEADS // heads, nq),
        in_specs=[pl.BlockSpec((4, A_QK_DIM), lambda b, h, i: (0, 0)),
                  pl.BlockSpec((1, LANES), lambda b, h, i: (0, 0)),
                  pl.BlockSpec((tq, width), lambda b, h, i: (b * nq + i, q_col // heads + h)),
                  pl.BlockSpec((seq, width), lambda b, h, i: (b, k_col // heads + h)),
                  pl.BlockSpec((seq, width), lambda b, h, i: (b, v_col // heads + h))],
        out_specs=pl.BlockSpec((tq, width), lambda b, h, i: (b * nq + i, h)),
        out_shape=jax.ShapeDtypeStruct((batch * seq, A_HEADS * LANES), BF16),
        scratch_shapes=[pltpu.VMEM((heads, LANES, seq), BF16), pltpu.VMEM((heads, LANES, 2 * tq), BF16)]
        + _flash_scratch(heads, tk, 2 * tq, LANES),
        compiler_params=_params("parallel", "parallel", "arbitrary"),
        name="diff_attn",
    )(lam_params, subln.reshape(1, LANES), qkv, qkv, qkv)


def _tile_index(ref, i, tile):
    per_slab = ref.shape[1]
    if per_slab >= tile:
        runs = per_slab // tile
        return i // runs, pl.ds(pl.multiple_of((i % runs) * tile, tile), tile)
    slabs = tile // per_slab
    return pl.ds(i * slabs, slabs), slice(None)


def _load_tile(ref, i, c, tile):
    slab, rows = _tile_index(ref, i, tile)
    return ref[slab, rows, _lanes(c)].reshape(tile, LANES)


def _store_tile(ref, i, c, tile, value):
    slab, rows = _tile_index(ref, i, tile)
    per_slab = ref.shape[1]
    ref[slab, rows, _lanes(c)] = value if per_slab >= tile else value.reshape(tile // per_slab, per_slab, LANES)


def _dilated_body(q_ref, k_ref, v_ref, o_ref, lse_ref, *, tile, heads):
    length = q_ref.shape[0] * q_ref.shape[1]
    row = lax.broadcasted_iota(jnp.int32, (tile, 1), 0)

    def attend(i, c, k, v, valid):
        s = jnp.where(valid, _dot_nt(_load_tile(q_ref, i, c, tile), k), -jnp.inf)
        m = jnp.max(s, axis=-1, keepdims=True)
        p = jnp.exp(s - m)
        l = jnp.sum(p, axis=-1, keepdims=True)
        o = jnp.dot(p.astype(BF16), v, preferred_element_type=F32) / l
        _store_tile(o_ref, i, c, tile, o)
        _store_tile(lse_ref, i, c, tile, jnp.broadcast_to(m + jnp.log(l), (tile, LANES)))

    col = lax.broadcasted_iota(jnp.int32, (1, tile), 1)
    for c in range(heads):
        attend(0, c, _load_tile(k_ref, 0, c, tile), _load_tile(v_ref, 0, c, tile), col <= row)

    col2 = lax.broadcasted_iota(jnp.int32, (1, 2 * tile), 1)
    band = jnp.logical_and(col2 >= row, col2 <= row + tile)

    def body(i, _):
        for c in range(heads):
            k = jnp.concatenate([_load_tile(k_ref, i - 1, c, tile), _load_tile(k_ref, i, c, tile)], axis=0)
            v = jnp.concatenate([_load_tile(v_ref, i - 1, c, tile), _load_tile(v_ref, i, c, tile)], axis=0)
            attend(i, c, k, v, band)
        return 0

    lax.fori_loop(1, length // tile, body, 0)


def _dilated_group(qkv, *, batch, seq, row_tile, group, dil, heads, q_col, k_col, v_col):
    hpg = B_HEADS_PER_GROUP
    tile = B_GROUPS[group][0] // dil
    slabs, per_slab = seq // row_tile, row_tile // dil
    assert tile == LANES and hpg % heads == 0 and (seq // dil) % tile == 0
    assert per_slab % tile == 0 or tile % per_slab == 0
    width = heads * LANES
    view = qkv.reshape(batch, slabs, dil, per_slab, qkv.shape[1])

    def in_spec(col):
        first = col + group * hpg
        assert first % heads == 0
        return pl.BlockSpec((None, slabs, None, per_slab, width),
                            lambda b, h, r: (b, 0, r, 0, first // heads + h))

    out_spec = pl.BlockSpec((None, slabs, None, per_slab, width), lambda b, h, r: (b, 0, r, 0, h))
    out_sds = jax.ShapeDtypeStruct((batch, slabs, dil, per_slab, hpg * LANES), F32)
    o, lse = pl.pallas_call(
        functools.partial(_dilated_body, tile=tile, heads=heads),
        grid=(batch, hpg // heads, dil),
        in_specs=[in_spec(q_col), in_spec(k_col), in_spec(v_col)],
        out_specs=[out_spec, out_spec],
        out_shape=[out_sds, out_sds],
        compiler_params=_params("parallel", "parallel", "parallel"),
        name=f"dilated_g{group}",
    )(view, view, view)
    shape = (batch * slabs, dil, per_slab, hpg * LANES)
    return o.reshape(shape), lse.reshape(shape)


def _merge_body(o0_ref, o1_ref, o2_ref, l0_ref, l1_ref, l2_ref, out_ref, nat_ref):
    def natural(ref, slot, c):
        dil, rows = ref.shape[0], ref.shape[1]
        if dil == 1:
            return ref[0, :, _lanes(c)]
        for r in range(dil):
            nat_ref[slot, pl.ds(r, rows, stride=dil), :] = ref[r, :, _lanes(c)]
        return nat_ref[slot]

    for c in range(out_ref.shape[1] // LANES):
        o0, o1, o2 = natural(o0_ref, None, c), natural(o1_ref, 0, c), natural(o2_ref, 1, c)
        l0, l1, l2 = natural(l0_ref, None, c), natural(l1_ref, 2, c), natural(l2_ref, 3, c)
        m = jnp.maximum(jnp.maximum(l0, l1), l2)
        w0, w1, w2 = jnp.exp(l0 - m), jnp.exp(l1 - m), jnp.exp(l2 - m)
        out_ref[:, _lanes(c)] = ((w0 * o0 + w1 * o1 + w2 * o2) / (w0 + w1 + w2)).astype(out_ref.dtype)


def _merge_groups(outs, lses):
    tiles, _, _, width = outs[0].shape
    row_tile = outs[0].shape[1] * outs[0].shape[2]
    specs = [pl.BlockSpec((None,) + a.shape[1:], lambda i: (i, 0, 0, 0)) for a in (*outs, *lses)]
    return pl.pallas_call(
        _merge_body,
        grid=(tiles,),
        in_specs=specs,
        out_specs=pl.BlockSpec((row_tile, width), lambda i: (i, 0)),
        out_shape=jax.ShapeDtypeStruct((tiles * row_tile, width), BF16),
        scratch_shapes=[pltpu.VMEM((4, row_tile, LANES), F32)],
        compiler_params=_params("parallel"),
        name="dilated_merge",
    )(*outs, *lses)


def _mla_body(q_ref, kv_ref, kr_ref, o_ref, vt_ref, qt_ref, s_ref, p_ref, acc_ref, *, tq, tk, heads):
    qi = pl.program_id(2)

    @pl.when(qi == 0)
    def _():
        _fill_transposed(vt_ref, kv_ref, [(2 * c + 1) * LANES for c in range(heads)], tk)

    def make_score_fn(c):
        qt_ref[c] = q_ref[:, _lanes(c, 2 * LANES)].T

        def score(start):
            k = jnp.concatenate([kv_ref[pl.ds(start, tk), _lanes(2 * c)], kr_ref[pl.ds(start, tk), :]], axis=1)
            return jnp.dot(k, qt_ref[c], preferred_element_type=F32)

        return score

    diag = _causal_diag_masks(tk, tq, tq)
    results = _causal_flash_t([make_score_fn(c) for c in range(heads)], vt_ref, s_ref, p_ref, acc_ref,
                              qi * (tq // tk), lambda d, c: diag[d], tq // tk)
    for c, (l, acc) in enumerate(results):
        o_ref[:, _lanes(c)] = (acc / l).T.astype(o_ref.dtype)


def _mla_attention(q, kv, k_rope_src, *, batch, seq, tq, tk, heads, k_rope_col):
    nq = seq // tq
    assert C_HEADS % heads == 0 and seq % tq == 0 and tq % tk == 0
    return pl.pallas_call(
        functools.partial(_mla_body, tq=tq, tk=tk, heads=heads),
        grid=(batch, C_HEADS // heads, nq),
        in_specs=[pl.BlockSpec((tq, heads * 2 * LANES), lambda b, h, i: (b * nq + i, h)),
                  pl.BlockSpec((seq, heads * 2 * LANES), lambda b, h, i: (b, h)),
                  pl.BlockSpec((seq, LANES), lambda b, h, i: (b, k_rope_col))],
        out_specs=pl.BlockSpec((tq, heads * LANES), lambda b, h, i: (b * nq + i, h)),
        out_shape=jax.ShapeDtypeStruct((batch * seq, C_HEADS * LANES), BF16),
        scratch_shapes=[pltpu.VMEM((heads, LANES, seq), BF16), pltpu.VMEM((heads, 2 * LANES, tq), BF16)]
        + _flash_scratch(heads, tk, tq, LANES),
        compiler_params=_params("parallel", "parallel", "arbitrary"),
        name="mla_attn",
    )(q, kv, k_rope_src)


def _moba_body(q_ref, k_ref, v_ref, o_ref, kmean_ref, vt_ref, sel_ref, qt_ref, s_ref, p_ref, acc_ref, *,
               n_blk, tq, heads):
    qi = pl.program_id(2)
    blk = MOBA_BLOCK
    n_own = tq // blk

    @pl.when(qi == 0)
    def _():
        for c in range(heads):
            k_all = k_ref[:, _lanes(c)].astype(F32).reshape(n_blk, blk, LANES)
            kmean_ref[c] = jnp.mean(k_all, axis=1)
        _fill_transposed(vt_ref, v_ref, [c * LANES for c in range(heads)], blk)

    blk_id = lax.broadcasted_iota(jnp.int32, (n_blk, 1), 0)
    query_col = lax.broadcasted_iota(jnp.int32, (1, tq), 1)
    own_in_tile = query_col // blk
    past = blk_id < qi * n_own + own_in_tile
    causal = lax.broadcasted_iota(jnp.int32, (blk, 1), 0) <= query_col % blk

    def make_score_fn(c):
        q_t = q_ref[:, _lanes(c)].T
        km = kmean_ref[c]
        km_hi = km.astype(BF16)
        rem = km - km_hi.astype(F32)
        km_mid = rem.astype(BF16)
        km_lo = (rem - km_mid.astype(F32)).astype(BF16)
        gate = (jnp.dot(km_hi, q_t, preferred_element_type=F32) + jnp.dot(km_mid, q_t, preferred_element_type=F32)
                + jnp.dot(km_lo, q_t, preferred_element_type=F32))
        gate = jnp.where(past, gate, -jnp.inf)
        selected = jnp.zeros(gate.shape, F32)
        for _ in range(min(MOBA_TOPK, n_blk - 1)):
            best = jnp.max(gate, axis=0, keepdims=True)
            first = jnp.min(jnp.where(gate == best, blk_id, n_blk), axis=0, keepdims=True)
            pick = blk_id == first
            selected = jnp.where(jnp.logical_and(pick, past), 1.0, selected)
            gate = jnp.where(pick, -jnp.inf, gate)
        sel_ref[c] = selected
        qt_ref[c] = q_t
        return lambda start: jnp.dot(k_ref[pl.ds(start, blk), _lanes(c)], qt_ref[c], preferred_element_type=F32)

    score_fns = [make_score_fn(c) for c in range(heads)]
    n_past = qi * n_own
    chosen_fns = [lambda n, c=c: jnp.logical_or(sel_ref[c, pl.ds(n, 1), :] > 0.0, n >= n_past)
                  for c in range(heads)]

    def own_block_mask(d, c):
        chosen = sel_ref[c, pl.ds(n_past + d, 1), :] > 0.0
        return jnp.logical_or(jnp.logical_and(own_in_tile == d, causal),
                              jnp.logical_and(own_in_tile > d, chosen))

    results = _causal_flash_t(score_fns, vt_ref, s_ref, p_ref, acc_ref, n_past, own_block_mask, n_own,
                              past_mask_fns=chosen_fns)
    for c, (l, acc) in enumerate(results):
        o_ref[:, _lanes(c)] = (acc / l).T.astype(o_ref.dtype)


def _moba_attention(qkv, *, batch, seq, tq, heads, q_col, k_col, v_col):
    blk = MOBA_BLOCK
    n_blk = seq // blk
    nq = seq // tq
    width = heads * LANES
    assert seq % tq == 0 and tq % blk == 0 and D_HEADS % heads == 0
    assert q_col % heads == 0 and k_col % heads == 0 and v_col % heads == 0
    return pl.pallas_call(
        functools.partial(_moba_body, n_blk=n_blk, tq=tq, heads=heads),
        grid=(batch, D_HEADS // heads, nq),
        in_specs=[pl.BlockSpec((tq, width), lambda b, h, i: (b * nq + i, q_col // heads + h)),
                  pl.BlockSpec((seq, width), lambda b, h, i: (b, k_col // heads + h)),
                  pl.BlockSpec((seq, width), lambda b, h, i: (b, v_col // heads + h))],
        out_specs=pl.BlockSpec((tq, width), lambda b, h, i: (b * nq + i, h)),
        out_shape=jax.ShapeDtypeStruct((batch * seq, D_HEADS * LANES), BF16),
        scratch_shapes=[pltpu.VMEM((heads, n_blk, LANES), F32),
                        pltpu.VMEM((heads, LANES, seq), BF16),
                        pltpu.VMEM((heads, n_blk, tq), F32),
                        pltpu.VMEM((heads, LANES, tq), BF16)] + _flash_scratch(heads, blk, tq, LANES),
        compiler_params=_params("parallel", "parallel", "arbitrary"),
        name="moba_attn",
    )(qkv, qkv, qkv)


def _rope_tables(seq):
    pos = jnp.arange(seq, dtype=F32)[:, None]

    def angles(dim):
        inv = 1.0 / (ROPE_THETA ** (jnp.arange(0, dim, 2, dtype=F32) / dim))
        return pos * inv[None, :]

    a128 = angles(HEAD_DIM)
    a64 = angles(A_QK_DIM)
    c128 = jnp.concatenate([jnp.cos(a128)] * 2, axis=1)
    s128 = jnp.concatenate([-jnp.sin(a128), jnp.sin(a128)], axis=1)
    c64 = jnp.concatenate([jnp.cos(a64)] * 4, axis=1)
    s64 = jnp.concatenate([-jnp.sin(a64)] * 2 + [jnp.sin(a64)] * 2, axis=1)
    return c128, s128, c64, s64


def _interleave_diff_heads(w):
    d = w.shape[0]
    return w.reshape(d, A_HEADS, 2, 2, 32).transpose(0, 1, 3, 2, 4).reshape(d, A_HEADS * LANES)


def _spread_rope64(w):
    z = jnp.zeros(w.shape[:-1] + (32,), w.dtype)
    return jnp.concatenate([w[..., :32], z, w[..., 32:], z], axis=-1)


def _ab_mixer(x, gain, w_in, lam_params, subln, w_out, layer_idx, rope, *, batch, seq):
    aw = A_HEADS * LANES
    w = jnp.concatenate([_interleave_diff_heads(w_in[:, :aw]), _interleave_diff_heads(w_in[:, aw:2 * aw]),
                         w_in[:, 2 * aw:]], axis=1).astype(BF16)
    a_scale = A_QK_DIM ** -0.5 * LOG2_E
    b_scale = HEAD_DIM ** -0.5
    dils = [dil for _, dil in B_GROUPS for _ in range(B_HEADS_PER_GROUP)]
    modes = ([(ROPE64, a_scale, 1)] * A_HEADS + [(ROPE64, 1.0, 1)] * A_HEADS + [(PLAIN, 1.0, 1)] * A_HEADS
             + [(ROPE128, b_scale, d) for d in dils] + [(ROPE128, 1.0, d) for d in dils]
             + [(PLAIN, 1.0, d) for d in dils])
    row_tile = 1024
    qkv = _norm_proj(x, gain, w, rope, modes, seq=seq, tm=row_tile, tn=768, out_dtype=BF16)
    lam_init = 0.8 - 0.6 * math.exp(-0.3 * layer_idx)
    oa = _diff_attention(qkv, lam_params, subln, batch=batch, seq=seq, tq=512, tk=256, heads=2,
                         q_col=0, k_col=A_HEADS, v_col=2 * A_HEADS, lam_init=lam_init)
    b0 = 3 * A_HEADS
    outs, lses = [], []
    for g, (_, dil) in enumerate(B_GROUPS):
        o, lse = _dilated_group(qkv, batch=batch, seq=seq, row_tile=row_tile, group=g, dil=dil,
                                heads=2 if dil == 1 else 4,
                                q_col=b0, k_col=b0 + B_HEADS, v_col=b0 + 2 * B_HEADS)
        outs.append(o)
        lses.append(lse)
    ob = _merge_groups(outs, lses)
    o = jnp.concatenate([oa, ob], axis=1)
    return _out_proj(o, w_out.astype(BF16), x, tm=512, tn=1024)


def _cd_mixer(x, gain, w_in, q_norm, w_uq, kv_norm, w_ukv, w_out, rope, *, batch, seq):
    lat = C_Q_LORA + C_KV_LORA
    d_scale = HEAD_DIM ** -0.5 * LOG2_E
    latent = _norm_proj(x, gain, w_in[:, :lat].astype(BF16), rope, [(PLAIN, 1.0, 1)] * (lat // LANES),
                        seq=seq, tm=512, tn=lat // 2, out_dtype=F32)
    w_main = jnp.concatenate([w_in[:, lat + C_ROPE:], _spread_rope64(w_in[:, lat:lat + C_ROPE])],
                             axis=1).astype(BF16)
    modes = ([(ROPE128, d_scale, 1)] * D_HEADS + [(ROPE128, 1.0, 1)] * D_HEADS + [(PLAIN, 1.0, 1)] * D_HEADS
             + [(ROPE64, 1.0, 1)])
    main = _norm_proj(x, gain, w_main, rope, modes, seq=seq, tm=1024, tn=5 * LANES, out_dtype=BF16)

    c_scale = (C_NOPE + C_ROPE) ** -0.5 * LOG2_E
    w_q = w_uq.reshape(C_Q_LORA, C_HEADS, C_NOPE + C_ROPE)
    w_q = jnp.concatenate([w_q[..., :C_NOPE], _spread_rope64(w_q[..., C_NOPE:])], axis=-1)
    w_q = w_q.reshape(C_Q_LORA, C_HEADS * 2 * LANES).astype(BF16)
    q_modes = [(PLAIN, c_scale, 1), (ROPE64, c_scale, 1)] * C_HEADS
    qc = _norm_proj(latent[:, :C_Q_LORA], q_norm, w_q, rope, q_modes, seq=seq, tm=512, tn=512, out_dtype=BF16)
    kv = _norm_proj(latent[:, C_Q_LORA:], kv_norm, w_ukv.astype(BF16), rope,
                    [(PLAIN, 1.0, 1)] * (2 * C_HEADS), seq=seq, tm=512, tn=512, out_dtype=BF16)
    oc = _mla_attention(qc, kv, main, batch=batch, seq=seq, tq=1024, tk=256, heads=2, k_rope_col=3 * D_HEADS)
    od = _moba_attention(main, batch=batch, seq=seq, tq=512, heads=4,
                         q_col=0, k_col=D_HEADS, v_col=2 * D_HEADS)
    o = jnp.concatenate([oc, od], axis=1)
    return _out_proj(o, w_out.astype(BF16), x, tm=512, tn=1024)


def kernel(x, p, ffn_norm, ffn_w_gu, ffn_w_down, mix_norm, ab_w_in, ab_lambda, ab_subln, ab_w_out,
           cd_w_in, cd_q_norm, cd_w_uq, cd_kv_norm, cd_w_ukv, cd_w_out, ple_norm, ple_w_gate,
           ple_w_proj, final_norm):
    batch, seq, d = x.shape
    depth = p.shape[0]
    rope = _rope_tables(seq)
    x = x.reshape(batch * seq, d)
    for i in range(depth):
        j = i // 2
        x = _ffn(x, ffn_norm[i, 0], ffn_w_gu[i, 0].astype(BF16), ffn_w_down[i, 0].astype(BF16), tm=1024, tf=512)
        if i % 2 == 0:
            x = _ab_mixer(x, mix_norm[i], ab_w_in[j], ab_lambda[j], ab_subln[j], ab_w_out[j], i, rope,
                          batch=batch, seq=seq)
        else:
            x = _cd_mixer(x, mix_norm[i], cd_w_in[j], cd_q_norm[j], cd_w_uq[j], cd_kv_norm[j], cd_w_ukv[j],
                          cd_w_out[j], rope, batch=batch, seq=seq)
        x = _ffn(x, ffn_norm[i, 1], ffn_w_gu[i, 1].astype(BF16), ffn_w_down[i, 1].astype(BF16), tm=1024, tf=512)
        x = _ple(x, p[i].reshape(batch * seq, -1), ple_norm[i], ple_w_gate[i].astype(BF16),
                 ple_w_proj[i].astype(BF16), final_norm, tm=256, final_norm=(i == depth - 1))
    return x.reshape(batch, seq, d)
```

```python
import functools
import math

import jax
import jax.numpy as jnp
from jax import lax
from jax.experimental import pallas as pl
from jax.experimental.pallas import tpu as pltpu

F32 = jnp.float32
BF16 = jnp.bfloat16

D_MODEL = 2048
D_FF = 5632
PLE_DIM = 256
HEAD_DIM = 128
ROPE_THETA = 10000.0
NORM_EPS = 1e-6

A_HEADS = 8
A_QK_DIM = 64
B_GROUPS = ((128, 1), (512, 4), (2048, 16))
B_HEADS_PER_GROUP = 4
B_HEADS = B_HEADS_PER_GROUP * len(B_GROUPS)
C_HEADS = 8
C_Q_LORA = 512
C_KV_LORA = 256
C_NOPE = 128
C_ROPE = 64
D_HEADS = 8
MOBA_BLOCK = 256
MOBA_TOPK = 3

LOG2_E = 1.0 / math.log(2.0)
LANES = 128
VMEM_LIMIT = 56 * 1024 * 1024

PLAIN, ROPE128, ROPE64 = 0, 1, 2


def _rms(x, gain):
    return x * lax.rsqrt(jnp.mean(x * x, axis=-1, keepdims=True) + NORM_EPS) * gain


def _params(*semantics):
    return pltpu.CompilerParams(dimension_semantics=semantics, vmem_limit_bytes=VMEM_LIMIT)


def _norm_proj_body(x_ref, g_ref, w_ref, c128_ref, s128_ref, c64_ref, s64_ref, o_ref, xn_ref, stage_ref, *,
                    tile_patterns):
    j = pl.program_id(1)
    tm = o_ref.shape[0]

    @pl.when(j == 0)
    def _():
        xn_ref[...] = _rms(x_ref[...], g_ref[...]).astype(BF16)

    def tile(pattern):
        acc = jnp.dot(xn_ref[...], w_ref[...], preferred_element_type=F32)
        for c, (kind, scale, dil) in enumerate(pattern):
            a = acc[:, _lanes(c)]
            if scale != 1.0:
                a = a * scale
            if kind == ROPE128:
                a = a * c128_ref[...] + pltpu.roll(a, 64, 1) * s128_ref[...]
            elif kind == ROPE64:
                a = a * c64_ref[...] + pltpu.roll(a, 64, 1) * s64_ref[...]
            if dil == 1:
                o_ref[:, _lanes(c)] = a.astype(o_ref.dtype)
            else:
                rows = tm // dil
                stage_ref[c] = a
                for r in range(dil):
                    o_ref[r * rows:(r + 1) * rows, _lanes(c)] = (
                        stage_ref[c, pl.ds(r, rows, stride=dil), :].astype(o_ref.dtype))

    distinct = sorted(set(tile_patterns), key=tile_patterns.index)
    if len(distinct) == 1:
        tile(distinct[0])
    else:
        for pattern in distinct:
            tiles = [t for t, p in enumerate(tile_patterns) if p == pattern]
            cond = functools.reduce(jnp.logical_or, [j == t for t in tiles])
            pl.when(cond)(functools.partial(tile, pattern))


def _norm_proj(x, gain, w, rope, chunk_modes, *, seq, tm, tn, out_dtype):
    t, k = x.shape
    n = w.shape[1]
    per_tile = tn // LANES
    assert t % tm == 0 and n % tn == 0 and seq % tm == 0 and len(chunk_modes) * LANES == n
    tile_patterns = tuple(tuple(chunk_modes[a * per_tile:(a + 1) * per_tile]) for a in range(n // tn))
    seq_tiles = seq // tm
    tab = pl.BlockSpec((tm, LANES), lambda i, j: (i % seq_tiles, 0))
    return pl.pallas_call(
        functools.partial(_norm_proj_body, tile_patterns=tile_patterns),
        grid=(t // tm, n // tn),
        in_specs=[pl.BlockSpec((tm, k), lambda i, j: (i, 0)),
                  pl.BlockSpec((1, k), lambda i, j: (0, 0)),
                  pl.BlockSpec((k, tn), lambda i, j: (0, j)),
                  tab, tab, tab, tab],
        out_specs=pl.BlockSpec((tm, tn), lambda i, j: (i, j)),
        out_shape=jax.ShapeDtypeStruct((t, n), out_dtype),
        scratch_shapes=[pltpu.VMEM((tm, k), BF16), pltpu.VMEM((per_tile, tm, LANES), F32)],
        compiler_params=_params("parallel", "arbitrary"),
        name="norm_proj",
    )(x, gain.reshape(1, k), w, *rope)


def _out_proj_body(a1_ref, a2_ref, w1_ref, w2_ref, res_ref, o_ref):
    o_ref[...] = (res_ref[...] + jnp.dot(a1_ref[...], w1_ref[...], preferred_element_type=F32)
                  + jnp.dot(a2_ref[...], w2_ref[...], preferred_element_type=F32))


def _out_proj(a1, a2, w, res, *, tm, tn):
    t, k1 = a1.shape
    k2 = a2.shape[1]
    n = w.shape[1]
    assert t % tm == 0 and n % tn == 0 and w.shape[0] == k1 + k2 and k1 % k2 == 0
    return pl.pallas_call(
        _out_proj_body,
        grid=(t // tm, n // tn),
        in_specs=[pl.BlockSpec((tm, k1), lambda i, j: (i, 0)),
                  pl.BlockSpec((tm, k2), lambda i, j: (i, 0)),
                  pl.BlockSpec((k1, tn), lambda i, j: (0, j)),
                  pl.BlockSpec((k2, tn), lambda i, j: (k1 // k2, j)),
                  pl.BlockSpec((tm, tn), lambda i, j: (i, j))],
        out_specs=pl.BlockSpec((tm, tn), lambda i, j: (i, j)),
        out_shape=jax.ShapeDtypeStruct((t, n), F32),
        compiler_params=_params("parallel", "parallel"),
        name="out_proj",
    )(a1, a2, w, w, res)


def _mla_proj_body(x_ref, g_ref, wl_ref, qn_ref, wq_ref, kvn_ref, wkv_ref, c64_ref, s64_ref, q_ref, kv_ref, *,
                   q_lora, q_scale):
    xn = _rms(x_ref[...], g_ref[...]).astype(BF16)
    latent = jnp.dot(xn, wl_ref[...], preferred_element_type=F32)
    c_q = _rms(latent[:, :q_lora], qn_ref[...]).astype(BF16)
    c_kv = _rms(latent[:, q_lora:], kvn_ref[...]).astype(BF16)
    kv_ref[...] = jnp.dot(c_kv, wkv_ref[...], preferred_element_type=F32).astype(kv_ref.dtype)
    q = jnp.dot(c_q, wq_ref[...], preferred_element_type=F32) * q_scale
    for c in range(q_ref.shape[1] // LANES):
        a = q[:, _lanes(c)]
        if c % 2 == 1:
            a = a * c64_ref[...] + pltpu.roll(a, 64, 1) * s64_ref[...]
        q_ref[:, _lanes(c)] = a.astype(q_ref.dtype)


def _mla_proj(x, gain, w_latent, q_norm, w_q, kv_norm, w_kv, rope, *, seq, tm, q_scale):
    t, d = x.shape
    q_lora, kv_lora = w_q.shape[0], w_kv.shape[0]
    assert t % tm == 0 and seq % tm == 0 and w_latent.shape[1] == q_lora + kv_lora
    seq_tiles = seq // tm
    const = lambda i: (0, 0)
    tab = pl.BlockSpec((tm, LANES), lambda i: (i % seq_tiles, 0))
    full = lambda a: pl.BlockSpec(a.shape, const)
    out_spec = lambda a: pl.BlockSpec((tm, a.shape[1]), lambda i: (i, 0))
    return pl.pallas_call(
        functools.partial(_mla_proj_body, q_lora=q_lora, q_scale=q_scale),
        grid=(t // tm,),
        in_specs=[pl.BlockSpec((tm, d), lambda i: (i, 0)), pl.BlockSpec((1, d), const), full(w_latent),
                  pl.BlockSpec((1, q_lora), const), full(w_q), pl.BlockSpec((1, kv_lora), const), full(w_kv),
                  tab, tab],
        out_specs=[out_spec(w_q), out_spec(w_kv)],
        out_shape=[jax.ShapeDtypeStruct((t, w_q.shape[1]), BF16), jax.ShapeDtypeStruct((t, w_kv.shape[1]), BF16)],
        compiler_params=_params("parallel"),
        name="mla_proj",
    )(x, gain.reshape(1, d), w_latent, q_norm.reshape(1, q_lora), w_q, kv_norm.reshape(1, kv_lora), w_kv,
      rope[2], rope[3])


def _ffn_body(x_ref, g_ref, wg_ref, wu_ref, wd_ref, o_ref, xn_ref):
    j = pl.program_id(1)

    @pl.when(j == 0)
    def _():
        xn_ref[...] = _rms(x_ref[...], g_ref[...]).astype(BF16)
        o_ref[...] = jnp.zeros_like(o_ref)

    xn = xn_ref[...]
    g = jnp.dot(xn, wg_ref[...], preferred_element_type=F32)
    u = jnp.dot(xn, wu_ref[...], preferred_element_type=F32)
    h = (g * jax.nn.sigmoid(g) * u).astype(BF16)
    o_ref[...] += jnp.dot(h, wd_ref[...], preferred_element_type=F32)

    @pl.when(j == pl.num_programs(1) - 1)
    def _():
        o_ref[...] = x_ref[...] + 0.5 * o_ref[...]


def _ffn(x, gain, w_gu, w_down, *, tm, tf):
    t, d = x.shape
    d_ff = w_down.shape[0]
    assert t % tm == 0 and d_ff % tf == 0
    nf = d_ff // tf
    return pl.pallas_call(
        _ffn_body,
        grid=(t // tm, nf),
        in_specs=[pl.BlockSpec((tm, d), lambda i, j: (i, 0)),
                  pl.BlockSpec((1, d), lambda i, j: (0, 0)),
                  pl.BlockSpec((d, tf), lambda i, j: (0, j)),
                  pl.BlockSpec((d, tf), lambda i, j: (0, nf + j)),
                  pl.BlockSpec((tf, d), lambda i, j: (j, 0))],
        out_specs=pl.BlockSpec((tm, d), lambda i, j: (i, 0)),
        out_shape=jax.ShapeDtypeStruct((t, d), F32),
        scratch_shapes=[pltpu.VMEM((tm, d), BF16)],
        compiler_params=_params("parallel", "arbitrary"),
        name="ffn",
    )(x, gain.reshape(1, d), w_gu, w_gu, w_down)


def _ple_body(x_ref, p_ref, g_ref, wg_ref, wp_ref, fg_ref, o_ref, *, final_norm):
    x = x_ref[...]
    xn = _rms(x, g_ref[...]).astype(BF16)
    gate = jax.nn.sigmoid(jnp.dot(xn, wg_ref[...], preferred_element_type=F32))
    proj = jnp.dot(p_ref[...].astype(BF16), wp_ref[...], preferred_element_type=F32)
    y = x + gate * proj
    if final_norm:
        y = _rms(y, fg_ref[...])
    o_ref[...] = y


def _ple(x, p, layer, gain, w_gate, w_proj, final_gain, *, tm, final_norm):
    t, d = x.shape
    pd = p.shape[1]
    assert t % tm == 0
    first_tile = layer * (t // tm)
    const = lambda i: (0, 0)
    return pl.pallas_call(
        functools.partial(_ple_body, final_norm=final_norm),
        grid=(t // tm,),
        in_specs=[pl.BlockSpec((tm, d), lambda i: (i, 0)),
                  pl.BlockSpec((tm, pd), lambda i: (first_tile + i, 0)),
                  pl.BlockSpec((1, d), const),
                  pl.BlockSpec((d, d), const),
                  pl.BlockSpec((pd, d), const),
                  pl.BlockSpec((1, d), const)],
        out_specs=pl.BlockSpec((tm, d), lambda i: (i, 0)),
        out_shape=jax.ShapeDtypeStruct((t, d), F32),
        compiler_params=_params("parallel"),
        name="ple",
    )(x, p, gain.reshape(1, d), w_gate, w_proj, final_gain.reshape(1, d))


def _dot_nt(a, b):
    return lax.dot_general(a, b, (((1,), (1,)), ((), ())), preferred_element_type=F32)


def _lanes(c, width=LANES):
    return slice(c * width, (c + 1) * width)


def _fill_transposed(vt_ref, v_ref, lane_offsets, chunk):
    dv = vt_ref.shape[1]

    def body(j, _):
        start = pl.multiple_of(j * chunk, chunk)
        for c, off in enumerate(lane_offsets):
            vt_ref[c, :, pl.ds(start, chunk)] = v_ref[pl.ds(start, chunk), off:off + dv].T
        return 0

    lax.fori_loop(0, v_ref.shape[0] // chunk, body, 0)


def _flash_scratch(heads, tk, cols, dv):
    return [pltpu.VMEM((heads, tk, cols), F32), pltpu.VMEM((heads, tk, cols), BF16),
            pltpu.VMEM((heads, dv, cols), F32)]


def _causal_diag_masks(tk, tq, cols):
    key_row = lax.broadcasted_iota(jnp.int32, (tk, 1), 0)
    query_col = lax.broadcasted_iota(jnp.int32, (1, cols), 1) % tq
    return [key_row + d * tk <= query_col for d in range(tq // tk)]


def _causal_flash_t(score_fns, vt_ref, s_ref, p_ref, acc_ref, n_past, diag_mask_fn, n_diag, past_mask_fns=None):
    heads = range(len(score_fns))
    _, tk, cols = s_ref.shape

    def stage_scores(block):
        start = pl.multiple_of(block * tk, tk)
        for c in heads:
            s = score_fns[c](start)
            if past_mask_fns is not None:
                s = jnp.where(past_mask_fns[c](block), s, -jnp.inf)
            s_ref[c] = s

    def accumulate(block, alphas):
        start = pl.multiple_of(block * tk, tk)
        for c in heads:
            pv = jnp.dot(vt_ref[c, :, pl.ds(start, tk)], p_ref[c], preferred_element_type=F32)
            acc_ref[c] = alphas[c] * acc_ref[c] + pv

    def softmax(stats, diag=None):
        new_stats, alphas = [], []
        for c in heads:
            m, l = stats[c]
            if diag is None:
                read = lambda: s_ref[c]
            else:
                visible = diag_mask_fn(diag, c)
                read = lambda: jnp.where(visible, s_ref[c], -jnp.inf)
            m_new = jnp.maximum(m, jnp.max(read(), axis=0, keepdims=True))
            alpha = jnp.exp2(m - m_new)
            p = jnp.exp2(read() - m_new)
            p_ref[c] = p.astype(BF16)
            new_stats.append((m_new, alpha * l + jnp.sum(p, axis=0, keepdims=True)))
            alphas.append(alpha)
        return tuple(new_stats), tuple(alphas)

    def past_block(j, carry):
        stats, alphas = carry
        accumulate(jnp.maximum(j - 1, 0), alphas)
        stats, alphas = softmax(stats)
        stage_scores(j + 1)
        return stats, alphas

    for c in heads:
        p_ref[c] = jnp.zeros(p_ref.shape[1:], BF16)
        acc_ref[c] = jnp.zeros(acc_ref.shape[1:], F32)
    stage_scores(0)
    init = (tuple((jnp.full((1, cols), jnp.finfo(F32).min, F32), jnp.zeros((1, cols), F32)) for _ in heads),
            tuple(jnp.ones((1, cols), F32) for _ in heads))
    stats, alphas = lax.fori_loop(0, n_past, past_block, init)

    accumulate(jnp.maximum(n_past - 1, 0), alphas)
    for d in range(n_diag):
        stats, alphas = softmax(stats, diag=d)
        if d + 1 < n_diag:
            stage_scores(n_past + d + 1)
        accumulate(n_past + d, alphas)
    return tuple((stats[c][1], acc_ref[c]) for c in heads)


def _diff_attn_body(lam_ref, subln_ref, q_ref, k_ref, v_ref, o_ref, vt_ref, qt_ref, s_ref, p_ref, acc_ref, *,
                    tq, tk, heads, lam_init):
    qi = pl.program_id(2)

    @pl.when(qi == 0)
    def _():
        _fill_transposed(vt_ref, v_ref, [c * LANES for c in range(heads)], tk)

    is_map0 = (lax.broadcasted_iota(jnp.int32, (LANES, 1), 0) % 64) < 32

    def make_score_fn(c):
        q_t = q_ref[:, _lanes(c)].T
        zero = jnp.zeros_like(q_t)
        qt_ref[c] = jnp.concatenate([jnp.where(is_map0, q_t, zero), jnp.where(is_map0, zero, q_t)], axis=1)
        return lambda start: jnp.dot(k_ref[pl.ds(start, tk), _lanes(c)], qt_ref[c], preferred_element_type=F32)

    diag = _causal_diag_masks(tk, tq, 2 * tq)
    results = _causal_flash_t([make_score_fn(c) for c in range(heads)], vt_ref, s_ref, p_ref, acc_ref,
                              qi * (tq // tk), lambda d, c: diag[d], tq // tk)

    lp = lam_ref[...]
    lam = (jnp.exp(jnp.sum(lp[0:1] * lp[1:2], axis=-1, keepdims=True))
           - jnp.exp(jnp.sum(lp[2:3] * lp[3:4], axis=-1, keepdims=True)) + lam_init)
    for c, (l, acc) in enumerate(results):
        o_t = acc / l
        o = (o_t[:, :tq] - lam * o_t[:, tq:]).T
        o_ref[:, _lanes(c)] = (_rms(o, subln_ref[...]) * (1.0 - lam_init)).astype(o_ref.dtype)


def _diff_attention(qkv, lam_params, subln, *, batch, seq, tq, tk, heads, q_col, k_col, v_col, lam_init):
    nq = seq // tq
    width = heads * LANES
    assert A_HEADS % heads == 0 and q_col % heads == 0 and k_col % heads == 0 and v_col % heads == 0
    assert seq % tq == 0 and tq % tk == 0
    return pl.pallas_call(
        functools.partial(_diff_attn_body, tq=tq, tk=tk, heads=heads, lam_init=lam_init),
        grid=(batch, A_HEADS // heads, nq),
        in_specs=[pl.BlockSpec((4, A_QK_DIM), lambda b, h, i: (0, 0)),
                  pl.BlockSpec((1, LANES), lambda b, h, i: (0, 0)),
                  pl.BlockSpec((tq, width), lambda b, h, i: (b * nq + i, q_col // heads + h)),
                  pl.BlockSpec((seq, width), lambda b, h, i: (b, k_col // heads + h)),
                  pl.BlockSpec((seq, width), lambda b, h, i: (b, v_col // heads + h))],
        out_specs=pl.BlockSpec((tq, width), lambda b, h, i: (b * nq + i, h)),
        out_shape=jax.ShapeDtypeStruct((batch * seq, A_HEADS * LANES), BF16),
        scratch_shapes=[pltpu.VMEM((heads, LANES, seq), BF16), pltpu.VMEM((heads, LANES, 2 * tq), BF16)]
        + _flash_scratch(heads, tk, 2 * tq, LANES),
        compiler_params=_params("parallel", "parallel", "arbitrary"),
        name="diff_attn",
    )(lam_params, subln.reshape(1, LANES), qkv, qkv, qkv)


def _tile_index(ref, i, tile):
    per_slab = ref.shape[1]
    if per_slab >= tile:
        runs = per_slab // tile
        return i // runs, pl.ds(pl.multiple_of((i % runs) * tile, tile), tile)
    slabs = tile // per_slab
    return pl.ds(i * slabs, slabs), slice(None)


def _load_tile(ref, i, c, tile):
    slab, rows = _tile_index(ref, i, tile)
    return ref[slab, rows, _lanes(c)].reshape(tile, LANES)


def _store_tile(ref, i, c, tile, value):
    slab, rows = _tile_index(ref, i, tile)
    per_slab = ref.shape[1]
    ref[slab, rows, _lanes(c)] = value if per_slab >= tile else value.reshape(tile // per_slab, per_slab, LANES)


def _dilated_body(q_ref, k_ref, v_ref, o_ref, lse_ref, *, tile, heads):
    length = q_ref.shape[0] * q_ref.shape[1]
    row = lax.broadcasted_iota(jnp.int32, (tile, 1), 0)

    def attend(i, c, k, v, valid):
        s = jnp.where(valid, _dot_nt(_load_tile(q_ref, i, c, tile), k), -jnp.inf)
        m = jnp.max(s, axis=-1, keepdims=True)
        p = jnp.exp(s - m)
        l = jnp.sum(p, axis=-1, keepdims=True)
        o = jnp.dot(p.astype(BF16), v, preferred_element_type=F32) / l
        _store_tile(o_ref, i, c, tile, o)
        _store_tile(lse_ref, i, c, tile, jnp.broadcast_to(m + jnp.log(l), (tile, LANES)))

    col = lax.broadcasted_iota(jnp.int32, (1, tile), 1)
    for c in range(heads):
        attend(0, c, _load_tile(k_ref, 0, c, tile), _load_tile(v_ref, 0, c, tile), col <= row)

    col2 = lax.broadcasted_iota(jnp.int32, (1, 2 * tile), 1)
    band = jnp.logical_and(col2 >= row, col2 <= row + tile)

    def body(i, _):
        for c in range(heads):
            k = jnp.concatenate([_load_tile(k_ref, i - 1, c, tile), _load_tile(k_ref, i, c, tile)], axis=0)
            v = jnp.concatenate([_load_tile(v_ref, i - 1, c, tile), _load_tile(v_ref, i, c, tile)], axis=0)
            attend(i, c, k, v, band)
        return 0

    lax.fori_loop(1, length // tile, body, 0)


def _dilated_group(qkv, *, batch, seq, row_tile, group, dil, heads, q_col, k_col, v_col):
    hpg = B_HEADS_PER_GROUP
    tile = B_GROUPS[group][0] // dil
    slabs, per_slab = seq // row_tile, row_tile // dil
    assert tile == LANES and hpg % heads == 0 and (seq // dil) % tile == 0
    assert per_slab % tile == 0 or tile % per_slab == 0
    width = heads * LANES
    view = qkv.reshape(batch, slabs, dil, per_slab, qkv.shape[1])

    def in_spec(col):
        first = col + group * hpg
        assert first % heads == 0
        return pl.BlockSpec((None, slabs, None, per_slab, width),
                            lambda b, h, r: (b, 0, r, 0, first // heads + h))

    out_spec = pl.BlockSpec((None, slabs, None, per_slab, width), lambda b, h, r: (b, 0, r, 0, h))
    out_sds = jax.ShapeDtypeStruct((batch, slabs, dil, per_slab, hpg * LANES), F32)
    o, lse = pl.pallas_call(
        functools.partial(_dilated_body, tile=tile, heads=heads),
        grid=(batch, hpg // heads, dil),
        in_specs=[in_spec(q_col), in_spec(k_col), in_spec(v_col)],
        out_specs=[out_spec, out_spec],
        out_shape=[out_sds, out_sds],
        compiler_params=_params("parallel", "parallel", "parallel"),
        name=f"dilated_g{group}",
    )(view, view, view)
    shape = (batch * slabs, dil, per_slab, hpg * LANES)
    return o.reshape(shape), lse.reshape(shape)


def _merge_body(o0_ref, o1_ref, o2_ref, l0_ref, l1_ref, l2_ref, out_ref, nat_ref):
    def natural(ref, slot, c):
        dil, rows = ref.shape[0], ref.shape[1]
        if dil == 1:
            return ref[0, :, _lanes(c)]
        for r in range(dil):
            nat_ref[slot, pl.ds(r, rows, stride=dil), :] = ref[r, :, _lanes(c)]
        return nat_ref[slot]

    for c in range(out_ref.shape[1] // LANES):
        o0, o1, o2 = natural(o0_ref, None, c), natural(o1_ref, 0, c), natural(o2_ref, 1, c)
        l0, l1, l2 = natural(l0_ref, None, c), natural(l1_ref, 2, c), natural(l2_ref, 3, c)
        m = jnp.maximum(jnp.maximum(l0, l1), l2)
        w0, w1, w2 = jnp.exp(l0 - m), jnp.exp(l1 - m), jnp.exp(l2 - m)
        out_ref[:, _lanes(c)] = ((w0 * o0 + w1 * o1 + w2 * o2) / (w0 + w1 + w2)).astype(out_ref.dtype)


def _merge_groups(outs, lses):
    tiles, _, _, width = outs[0].shape
    row_tile = outs[0].shape[1] * outs[0].shape[2]
    specs = [pl.BlockSpec((None,) + a.shape[1:], lambda i: (i, 0, 0, 0)) for a in (*outs, *lses)]
    return pl.pallas_call(
        _merge_body,
        grid=(tiles,),
        in_specs=specs,
        out_specs=pl.BlockSpec((row_tile, width), lambda i: (i, 0)),
        out_shape=jax.ShapeDtypeStruct((tiles * row_tile, width), BF16),
        scratch_shapes=[pltpu.VMEM((4, row_tile, LANES), F32)],
        compiler_params=_params("parallel"),
        name="dilated_merge",
    )(*outs, *lses)


def _mla_body(q_ref, kv_ref, kr_ref, o_ref, vt_ref, qt_ref, s_ref, p_ref, acc_ref, *, tq, tk, heads):
    qi = pl.program_id(2)

    @pl.when(qi == 0)
    def _():
        _fill_transposed(vt_ref, kv_ref, [(2 * c + 1) * LANES for c in range(heads)], tk)

    def make_score_fn(c):
        qt_ref[c] = q_ref[:, _lanes(c, 2 * LANES)].T

        def score(start):
            k = jnp.concatenate([kv_ref[pl.ds(start, tk), _lanes(2 * c)], kr_ref[pl.ds(start, tk), :]], axis=1)
            return jnp.dot(k, qt_ref[c], preferred_element_type=F32)

        return score

    diag = _causal_diag_masks(tk, tq, tq)
    results = _causal_flash_t([make_score_fn(c) for c in range(heads)], vt_ref, s_ref, p_ref, acc_ref,
                              qi * (tq // tk), lambda d, c: diag[d], tq // tk)
    for c, (l, acc) in enumerate(results):
        o_ref[:, _lanes(c)] = (acc / l).T.astype(o_ref.dtype)


def _mla_attention(q, kv, k_rope_src, *, batch, seq, tq, tk, heads, k_rope_col):
    nq = seq // tq
    assert C_HEADS % heads == 0 and seq % tq == 0 and tq % tk == 0
    return pl.pallas_call(
        functools.partial(_mla_body, tq=tq, tk=tk, heads=heads),
        grid=(batch, C_HEADS // heads, nq),
        in_specs=[pl.BlockSpec((tq, heads * 2 * LANES), lambda b, h, i: (b * nq + i, h)),
                  pl.BlockSpec((seq, heads * 2 * LANES), lambda b, h, i: (b, h)),
                  pl.BlockSpec((seq, LANES), lambda b, h, i: (b, k_rope_col))],
        out_specs=pl.BlockSpec((tq, heads * LANES), lambda b, h, i: (b * nq + i, h)),
        out_shape=jax.ShapeDtypeStruct((batch * seq, C_HEADS * LANES), BF16),
        scratch_shapes=[pltpu.VMEM((heads, LANES, seq), BF16), pltpu.VMEM((heads, 2 * LANES, tq), BF16)]
        + _flash_scratch(heads, tk, tq, LANES),
        compiler_params=_params("parallel", "parallel", "arbitrary"),
        name="mla_attn",
    )(q, kv, k_rope_src)


def _moba_body(q_ref, k_ref, v_ref, o_ref, kmean_ref, vt_ref, sel_ref, qt_ref, s_ref, p_ref, acc_ref, *,
               n_blk, tq, heads):
    qi = pl.program_id(2)
    blk = MOBA_BLOCK
    n_own = tq // blk

    @pl.when(qi == 0)
    def _():
        for c in range(heads):
            k_all = k_ref[:, _lanes(c)].astype(F32).reshape(n_blk, blk, LANES)
            kmean_ref[c] = jnp.mean(k_all, axis=1)
        _fill_transposed(vt_ref, v_ref, [c * LANES for c in range(heads)], blk)

    blk_id = lax.broadcasted_iota(jnp.int32, (n_blk, 1), 0)
    query_col = lax.broadcasted_iota(jnp.int32, (1, tq), 1)
    own_in_tile = query_col // blk
    past = blk_id < qi * n_own + own_in_tile
    causal = lax.broadcasted_iota(jnp.int32, (blk, 1), 0) <= query_col % blk

    def make_score_fn(c):
        q_t = q_ref[:, _lanes(c)].T
        km = kmean_ref[c]
        km_hi = km.astype(BF16)
        rem = km - km_hi.astype(F32)
        km_mid = rem.astype(BF16)
        km_lo = (rem - km_mid.astype(F32)).astype(BF16)
        gate = (jnp.dot(km_hi, q_t, preferred_element_type=F32) + jnp.dot(km_mid, q_t, preferred_element_type=F32)
                + jnp.dot(km_lo, q_t, preferred_element_type=F32))
        gate = jnp.where(past, gate, -jnp.inf)
        selected = jnp.zeros(gate.shape, F32)
        for _ in range(min(MOBA_TOPK, n_blk - 1)):
            best = jnp.max(gate, axis=0, keepdims=True)
            first = jnp.min(jnp.where(gate == best, blk_id, n_blk), axis=0, keepdims=True)
            pick = blk_id == first
            selected = jnp.where(jnp.logical_and(pick, past), 1.0, selected)
            gate = jnp.where(pick, -jnp.inf, gate)
        sel_ref[c] = selected
        qt_ref[c] = q_t
        return lambda start: jnp.dot(k_ref[pl.ds(start, blk), _lanes(c)], qt_ref[c], preferred_element_type=F32)

    score_fns = [make_score_fn(c) for c in range(heads)]
    n_past = qi * n_own
    chosen_fns = [lambda n, c=c: jnp.logical_or(sel_ref[c, pl.ds(n, 1), :] > 0.0, n >= n_past)
                  for c in range(heads)]

    def own_block_mask(d, c):
        chosen = sel_ref[c, pl.ds(n_past + d, 1), :] > 0.0
        return jnp.logical_or(jnp.logical_and(own_in_tile == d, causal),
                              jnp.logical_and(own_in_tile > d, chosen))

    results = _causal_flash_t(score_fns, vt_ref, s_ref, p_ref, acc_ref, n_past, own_block_mask, n_own,
                              past_mask_fns=chosen_fns)
    for c, (l, acc) in enumerate(results):
        o_ref[:, _lanes(c)] = (acc / l).T.astype(o_ref.dtype)


def _moba_attention(qkv, *, batch, seq, tq, heads, q_col, k_col, v_col):
    blk = MOBA_BLOCK
    n_blk = seq // blk
    nq = seq // tq
    width = heads * LANES
    assert seq % tq == 0 and tq % blk == 0 and D_HEADS % heads == 0
    assert q_col % heads == 0 and k_col % heads == 0 and v_col % heads == 0
    return pl.pallas_call(
        functools.partial(_moba_body, n_blk=n_blk, tq=tq, heads=heads),
        grid=(batch, D_HEADS // heads, nq),
        in_specs=[pl.BlockSpec((tq, width), lambda b, h, i: (b * nq + i, q_col // heads + h)),
                  pl.BlockSpec((seq, width), lambda b, h, i: (b, k_col // heads + h)),
                  pl.BlockSpec((seq, width), lambda b, h, i: (b, v_col // heads + h))],
        out_specs=pl.BlockSpec((tq, width), lambda b, h, i: (b * nq + i, h)),
        out_shape=jax.ShapeDtypeStruct((batch * seq, D_HEADS * LANES), BF16),
        scratch_shapes=[pltpu.VMEM((heads, n_blk, LANES), F32),
                        pltpu.VMEM((heads, LANES, seq), BF16),
                        pltpu.VMEM((heads, n_blk, tq), F32),
                        pltpu.VMEM((heads, LANES, tq), BF16)] + _flash_scratch(heads, blk, tq, LANES),
        compiler_params=_params("parallel", "parallel", "arbitrary"),
        name="moba_attn",
    )(qkv, qkv, qkv)


def _rope_tables(seq):
    pos = jnp.arange(seq, dtype=F32)[:, None]

    def angles(dim):
        inv = 1.0 / (ROPE_THETA ** (jnp.arange(0, dim, 2, dtype=F32) / dim))
        return pos * inv[None, :]

    a128 = angles(HEAD_DIM)
    a64 = angles(A_QK_DIM)
    c128 = jnp.concatenate([jnp.cos(a128)] * 2, axis=1)
    s128 = jnp.concatenate([-jnp.sin(a128), jnp.sin(a128)], axis=1)
    c64 = jnp.concatenate([jnp.cos(a64)] * 4, axis=1)
    s64 = jnp.concatenate([-jnp.sin(a64)] * 2 + [jnp.sin(a64)] * 2, axis=1)
    return c128, s128, c64, s64


def _interleave_diff_heads(w):
    d = w.shape[0]
    return w.reshape(d, A_HEADS, 2, 2, 32).transpose(0, 1, 3, 2, 4).reshape(d, A_HEADS * LANES)


def _spread_rope64(w):
    z = jnp.zeros(w.shape[:-1] + (32,), w.dtype)
    return jnp.concatenate([w[..., :32], z, w[..., 32:], z], axis=-1)


def _ab_mixer(x, gain, w_in, lam_params, subln, w_out, layer_idx, rope, *, batch, seq):
    aw = A_HEADS * LANES
    w = jnp.concatenate([_interleave_diff_heads(w_in[:, :aw]), _interleave_diff_heads(w_in[:, aw:2 * aw]),
                         w_in[:, 2 * aw:]], axis=1).astype(BF16)
    a_scale = A_QK_DIM ** -0.5 * LOG2_E
    b_scale = HEAD_DIM ** -0.5
    dils = [dil for _, dil in B_GROUPS for _ in range(B_HEADS_PER_GROUP)]
    modes = ([(ROPE64, a_scale, 1)] * A_HEADS + [(ROPE64, 1.0, 1)] * A_HEADS + [(PLAIN, 1.0, 1)] * A_HEADS
             + [(ROPE128, b_scale, d) for d in dils] + [(ROPE128, 1.0, d) for d in dils]
             + [(PLAIN, 1.0, d) for d in dils])
    row_tile = 1024
    qkv = _norm_proj(x, gain, w, rope, modes, seq=seq, tm=row_tile, tn=768, out_dtype=BF16)
    lam_init = 0.8 - 0.6 * math.exp(-0.3 * layer_idx)
    oa = _diff_attention(qkv, lam_params, subln, batch=batch, seq=seq, tq=512, tk=256, heads=2,
                         q_col=0, k_col=A_HEADS, v_col=2 * A_HEADS, lam_init=lam_init)
    b0 = 3 * A_HEADS
    outs, lses = [], []
    for g, (_, dil) in enumerate(B_GROUPS):
        o, lse = _dilated_group(qkv, batch=batch, seq=seq, row_tile=row_tile, group=g, dil=dil,
                                heads=2 if dil == 1 else 4,
                                q_col=b0, k_col=b0 + B_HEADS, v_col=b0 + 2 * B_HEADS)
        outs.append(o)
        lses.append(lse)
    ob = _merge_groups(outs, lses)
    return _out_proj(oa, ob, w_out.astype(BF16), x, tm=1024, tn=1024)


def _cd_mixer(x, gain, w_in, q_norm, w_uq, kv_norm, w_ukv, w_out, rope, *, batch, seq):
    lat = C_Q_LORA + C_KV_LORA
    d_scale = HEAD_DIM ** -0.5 * LOG2_E
    w_main = jnp.concatenate([w_in[:, lat + C_ROPE:], _spread_rope64(w_in[:, lat:lat + C_ROPE])],
                             axis=1).astype(BF16)
    modes = ([(ROPE128, d_scale, 1)] * D_HEADS + [(ROPE128, 1.0, 1)] * D_HEADS + [(PLAIN, 1.0, 1)] * D_HEADS
             + [(ROPE64, 1.0, 1)])
    main = _norm_proj(x, gain, w_main, rope, modes, seq=seq, tm=1024, tn=5 * LANES, out_dtype=BF16)

    c_scale = (C_NOPE + C_ROPE) ** -0.5 * LOG2_E
    w_q = w_uq.reshape(C_Q_LORA, C_HEADS, C_NOPE + C_ROPE)
    w_q = jnp.concatenate([w_q[..., :C_NOPE], _spread_rope64(w_q[..., C_NOPE:])], axis=-1)
    w_q = w_q.reshape(C_Q_LORA, C_HEADS * 2 * LANES).astype(BF16)
    qc, kv = _mla_proj(x, gain, w_in[:, :lat].astype(BF16), q_norm, w_q, kv_norm, w_ukv.astype(BF16), rope,
                       seq=seq, tm=512, q_scale=c_scale)
    oc = _mla_attention(qc, kv, main, batch=batch, seq=seq, tq=1024, tk=256, heads=2, k_rope_col=3 * D_HEADS)
    od = _moba_attention(main, batch=batch, seq=seq, tq=512, heads=4,
                         q_col=0, k_col=D_HEADS, v_col=2 * D_HEADS)
    return _out_proj(oc, od, w_out.astype(BF16), x, tm=1024, tn=1024)


def kernel(x, p, ffn_norm, ffn_w_gu, ffn_w_down, mix_norm, ab_w_in, ab_lambda, ab_subln, ab_w_out,
           cd_w_in, cd_q_norm, cd_w_uq, cd_kv_norm, cd_w_ukv, cd_w_out, ple_norm, ple_w_gate,
           ple_w_proj, final_norm):
    batch, seq, d = x.shape
    depth = p.shape[0]
    rope = _rope_tables(seq)
    x = x.reshape(batch * seq, d)
    for i in range(depth):
        j = i // 2
        x = _ffn(x, ffn_norm[i, 0], ffn_w_gu[i, 0].astype(BF16), ffn_w_down[i, 0].astype(BF16), tm=1024, tf=512)
        if i % 2 == 0:
            x = _ab_mixer(x, mix_norm[i], ab_w_in[j], ab_lambda[j], ab_subln[j], ab_w_out[j], i, rope,
                          batch=batch, seq=seq)
        else:
            x = _cd_mixer(x, mix_norm[i], cd_w_in[j], cd_q_norm[j], cd_w_uq[j], cd_kv_norm[j], cd_w_ukv[j],
                          cd_w_out[j], rope, batch=batch, seq=seq)
        x = _ffn(x, ffn_norm[i, 1], ffn_w_gu[i, 1].astype(BF16), ffn_w_down[i, 1].astype(BF16), tm=1024, tf=512)
        x = _ple(x, p.reshape(depth * batch * seq, -1), i, ple_norm[i], ple_w_gate[i].astype(BF16),
                 ple_w_proj[i].astype(BF16), final_norm, tm=256, final_norm=(i == depth - 1))
    return x.reshape(batch, seq, d)
```

```python
import functools
import math

import jax
import jax.numpy as jnp
from jax import lax
from jax.experimental import pallas as pl
from jax.experimental.pallas import tpu as pltpu

F32 = jnp.float32
BF16 = jnp.bfloat16

D_MODEL = 2048
D_FF = 5632
PLE_DIM = 256
HEAD_DIM = 128
ROPE_THETA = 10000.0
NORM_EPS = 1e-6

A_HEADS = 8
A_QK_DIM = 64
B_GROUPS = ((128, 1), (512, 4), (2048, 16))
B_HEADS_PER_GROUP = 4
B_HEADS = B_HEADS_PER_GROUP * len(B_GROUPS)
C_HEADS = 8
C_Q_LORA = 512
C_KV_LORA = 256
C_NOPE = 128
C_ROPE = 64
D_HEADS = 8
MOBA_BLOCK = 256
MOBA_TOPK = 3

LOG2_E = 1.0 / math.log(2.0)
LANES = 128
VMEM_LIMIT = 56 * 1024 * 1024

PLAIN, ROPE128, ROPE64 = 0, 1, 2


def _rms(x, gain):
    return x * lax.rsqrt(jnp.mean(x * x, axis=-1, keepdims=True) + NORM_EPS) * gain


def _params(*semantics):
    return pltpu.CompilerParams(dimension_semantics=semantics, vmem_limit_bytes=VMEM_LIMIT)


def _norm_proj_body(x_ref, g_ref, w_ref, c128_ref, s128_ref, c64_ref, s64_ref, o_ref, xn_ref, stage_ref, *,
                    tile_patterns):
    j = pl.program_id(1)
    tm = o_ref.shape[0]

    @pl.when(j == 0)
    def _():
        xn_ref[...] = _rms(x_ref[...], g_ref[...]).astype(BF16)

    def tile(pattern):
        acc = jnp.dot(xn_ref[...], w_ref[...], preferred_element_type=F32)
        for c, (kind, scale, dil) in enumerate(pattern):
            a = acc[:, _lanes(c)]
            if scale != 1.0:
                a = a * scale
            if kind == ROPE128:
                a = a * c128_ref[...] + pltpu.roll(a, 64, 1) * s128_ref[...]
            elif kind == ROPE64:
                a = a * c64_ref[...] + pltpu.roll(a, 64, 1) * s64_ref[...]
            if dil == 1:
                o_ref[:, _lanes(c)] = a.astype(o_ref.dtype)
            else:
                rows = tm // dil
                stage_ref[c] = a
                for r in range(dil):
                    o_ref[r * rows:(r + 1) * rows, _lanes(c)] = (
                        stage_ref[c, pl.ds(r, rows, stride=dil), :].astype(o_ref.dtype))

    distinct = sorted(set(tile_patterns), key=tile_patterns.index)
    if len(distinct) == 1:
        tile(distinct[0])
    else:
        for pattern in distinct:
            tiles = [t for t, p in enumerate(tile_patterns) if p == pattern]
            cond = functools.reduce(jnp.logical_or, [j == t for t in tiles])
            pl.when(cond)(functools.partial(tile, pattern))


def _norm_proj(x, gain, w, rope, chunk_modes, *, seq, tm, tn, out_dtype):
    t, k = x.shape
    n = w.shape[1]
    per_tile = tn // LANES
    assert t % tm == 0 and n % tn == 0 and seq % tm == 0 and len(chunk_modes) * LANES == n
    tile_patterns = tuple(tuple(chunk_modes[a * per_tile:(a + 1) * per_tile]) for a in range(n // tn))
    seq_tiles = seq // tm
    tab = pl.BlockSpec((tm, LANES), lambda i, j: (i % seq_tiles, 0))
    return pl.pallas_call(
        functools.partial(_norm_proj_body, tile_patterns=tile_patterns),
        grid=(t // tm, n // tn),
        in_specs=[pl.BlockSpec((tm, k), lambda i, j: (i, 0)),
                  pl.BlockSpec((1, k), lambda i, j: (0, 0)),
                  pl.BlockSpec((k, tn), lambda i, j: (0, j)),
                  tab, tab, tab, tab],
        out_specs=pl.BlockSpec((tm, tn), lambda i, j: (i, j)),
        out_shape=jax.ShapeDtypeStruct((t, n), out_dtype),
        scratch_shapes=[pltpu.VMEM((tm, k), BF16), pltpu.VMEM((per_tile, tm, LANES), F32)],
        compiler_params=_params("parallel", "arbitrary"),
        name="norm_proj",
    )(x, gain.reshape(1, k), w, *rope)


def _out_proj_body(a1_ref, a2_ref, w1_ref, w2_ref, res_ref, o_ref):
    o_ref[...] = (res_ref[...] + jnp.dot(a1_ref[...], w1_ref[...], preferred_element_type=F32)
                  + jnp.dot(a2_ref[...], w2_ref[...], preferred_element_type=F32))


def _out_proj(a1, a2, w, res, *, tm, tn):
    t, k1 = a1.shape
    k2 = a2.shape[1]
    n = w.shape[1]
    assert t % tm == 0 and n % tn == 0 and w.shape[0] == k1 + k2 and k1 % k2 == 0
    return pl.pallas_call(
        _out_proj_body,
        grid=(t // tm, n // tn),
        in_specs=[pl.BlockSpec((tm, k1), lambda i, j: (i, 0)),
                  pl.BlockSpec((tm, k2), lambda i, j: (i, 0)),
                  pl.BlockSpec((k1, tn), lambda i, j: (0, j)),
                  pl.BlockSpec((k2, tn), lambda i, j: (k1 // k2, j)),
                  pl.BlockSpec((tm, tn), lambda i, j: (i, j))],
        out_specs=pl.BlockSpec((tm, tn), lambda i, j: (i, j)),
        out_shape=jax.ShapeDtypeStruct((t, n), F32),
        compiler_params=_params("parallel", "parallel"),
        name="out_proj",
    )(a1, a2, w, w, res)


def _mla_proj_body(x_ref, g_ref, wl_ref, qn_ref, wq_ref, kvn_ref, wkv_ref, c64_ref, s64_ref, q_ref, kv_ref, *,
                   q_lora, q_scale):
    xn = _rms(x_ref[...], g_ref[...]).astype(BF16)
    latent = jnp.dot(xn, wl_ref[...], preferred_element_type=F32)
    c_q = _rms(latent[:, :q_lora], qn_ref[...]).astype(BF16)
    c_kv = _rms(latent[:, q_lora:], kvn_ref[...]).astype(BF16)
    kv_ref[...] = jnp.dot(c_kv, wkv_ref[...], preferred_element_type=F32).astype(kv_ref.dtype)
    q = jnp.dot(c_q, wq_ref[...], preferred_element_type=F32) * q_scale
    for c in range(q_ref.shape[1] // LANES):
        a = q[:, _lanes(c)]
        if c % 2 == 1:
            a = a * c64_ref[...] + pltpu.roll(a, 64, 1) * s64_ref[...]
        q_ref[:, _lanes(c)] = a.astype(q_ref.dtype)


def _mla_proj(x, gain, w_latent, q_norm, w_q, kv_norm, w_kv, rope, *, seq, tm, q_scale):
    t, d = x.shape
    q_lora, kv_lora = w_q.shape[0], w_kv.shape[0]
    assert t % tm == 0 and seq % tm == 0 and w_latent.shape[1] == q_lora + kv_lora
    seq_tiles = seq // tm
    const = lambda i: (0, 0)
    tab = pl.BlockSpec((tm, LANES), lambda i: (i % seq_tiles, 0))
    full = lambda a: pl.BlockSpec(a.shape, const)
    out_spec = lambda a: pl.BlockSpec((tm, a.shape[1]), lambda i: (i, 0))
    return pl.pallas_call(
        functools.partial(_mla_proj_body, q_lora=q_lora, q_scale=q_scale),
        grid=(t // tm,),
        in_specs=[pl.BlockSpec((tm, d), lambda i: (i, 0)), pl.BlockSpec((1, d), const), full(w_latent),
                  pl.BlockSpec((1, q_lora), const), full(w_q), pl.BlockSpec((1, kv_lora), const), full(w_kv),
                  tab, tab],
        out_specs=[out_spec(w_q), out_spec(w_kv)],
        out_shape=[jax.ShapeDtypeStruct((t, w_q.shape[1]), BF16), jax.ShapeDtypeStruct((t, w_kv.shape[1]), BF16)],
        compiler_params=_params("parallel"),
        name="mla_proj",
    )(x, gain.reshape(1, d), w_latent, q_norm.reshape(1, q_lora), w_q, kv_norm.reshape(1, kv_lora), w_kv,
      rope[2], rope[3])


def _ffn_body(x_ref, g_ref, wg_ref, wu_ref, wd_ref, o_ref, xn_ref):
    j = pl.program_id(1)

    @pl.when(j == 0)
    def _():
        xn_ref[...] = _rms(x_ref[...], g_ref[...]).astype(BF16)
        o_ref[...] = jnp.zeros_like(o_ref)

    xn = xn_ref[...]
    g = jnp.dot(xn, wg_ref[...], preferred_element_type=F32)
    u = jnp.dot(xn, wu_ref[...], preferred_element_type=F32)
    h = (g * jax.nn.sigmoid(g) * u).astype(BF16)
    o_ref[...] += jnp.dot(h, wd_ref[...], preferred_element_type=F32)

    @pl.when(j == pl.num_programs(1) - 1)
    def _():
        o_ref[...] = x_ref[...] + 0.5 * o_ref[...]


def _ffn(x, gain, w_gu, w_down, layer, half, *, tm, tf):
    t, d = x.shape
    d_ff = w_down.shape[2]
    assert t % tm == 0 and d_ff % tf == 0
    nf = d_ff // tf
    return pl.pallas_call(
        _ffn_body,
        grid=(t // tm, nf),
        in_specs=[pl.BlockSpec((tm, d), lambda i, j: (i, 0)),
                  pl.BlockSpec((1, d), lambda i, j: (0, 0)),
                  pl.BlockSpec((None, None, d, tf), lambda i, j: (layer, half, 0, j)),
                  pl.BlockSpec((None, None, d, tf), lambda i, j: (layer, half, 0, nf + j)),
                  pl.BlockSpec((None, None, tf, d), lambda i, j: (layer, half, j, 0))],
        out_specs=pl.BlockSpec((tm, d), lambda i, j: (i, 0)),
        out_shape=jax.ShapeDtypeStruct((t, d), F32),
        scratch_shapes=[pltpu.VMEM((tm, d), BF16)],
        compiler_params=_params("parallel", "arbitrary"),
        name="ffn",
    )(x, gain.reshape(1, d), w_gu, w_gu, w_down)


def _ple_body(x_ref, p_ref, g_ref, wg_ref, wp_ref, fg_ref, o_ref, *, final_norm):
    x = x_ref[...]
    xn = _rms(x, g_ref[...]).astype(BF16)
    gate = jax.nn.sigmoid(jnp.dot(xn, wg_ref[...], preferred_element_type=F32))
    proj = jnp.dot(p_ref[...].astype(BF16), wp_ref[...], preferred_element_type=F32)
    y = x + gate * proj
    if final_norm:
        y = _rms(y, fg_ref[...])
    o_ref[...] = y


def _ple(x, p, layer, gain, w_gate, w_proj, final_gain, *, tm, final_norm):
    t, d = x.shape
    pd = p.shape[1]
    assert t % tm == 0
    first_tile = layer * (t // tm)
    const = lambda i: (0, 0)
    return pl.pallas_call(
        functools.partial(_ple_body, final_norm=final_norm),
        grid=(t // tm,),
        in_specs=[pl.BlockSpec((tm, d), lambda i: (i, 0)),
                  pl.BlockSpec((tm, pd), lambda i: (first_tile + i, 0)),
                  pl.BlockSpec((1, d), const),
                  pl.BlockSpec((d, d), const),
                  pl.BlockSpec((pd, d), const),
                  pl.BlockSpec((1, d), const)],
        out_specs=pl.BlockSpec((tm, d), lambda i: (i, 0)),
        out_shape=jax.ShapeDtypeStruct((t, d), F32),
        compiler_params=_params("parallel"),
        name="ple",
    )(x, p, gain.reshape(1, d), w_gate, w_proj, final_gain.reshape(1, d))


def _dot_nt(a, b):
    return lax.dot_general(a, b, (((1,), (1,)), ((), ())), preferred_element_type=F32)


def _lanes(c, width=LANES):
    return slice(c * width, (c + 1) * width)


def _fill_transposed(vt_ref, v_ref, lane_offsets, chunk):
    dv = vt_ref.shape[1]

    def body(j, _):
        start = pl.multiple_of(j * chunk, chunk)
        for c, off in enumerate(lane_offsets):
            vt_ref[c, :, pl.ds(start, chunk)] = v_ref[pl.ds(start, chunk), off:off + dv].T
        return 0

    lax.fori_loop(0, v_ref.shape[0] // chunk, body, 0)


def _flash_scratch(heads, tk, cols, dv):
    return [pltpu.VMEM((heads, tk, cols), F32), pltpu.VMEM((heads, tk, cols), BF16),
            pltpu.VMEM((heads, dv, cols), F32)]


def _causal_diag_masks(tk, tq, cols):
    key_row = lax.broadcasted_iota(jnp.int32, (tk, 1), 0)
    query_col = lax.broadcasted_iota(jnp.int32, (1, cols), 1) % tq
    return [key_row + d * tk <= query_col for d in range(tq // tk)]


def _causal_flash_t(score_fns, vt_ref, s_ref, p_ref, acc_ref, n_past, diag_mask_fn, n_diag, past_mask_fns=None):
    heads = range(len(score_fns))
    _, tk, cols = s_ref.shape

    def stage_scores(block):
        start = pl.multiple_of(block * tk, tk)
        for c in heads:
            s = score_fns[c](start)
            if past_mask_fns is not None:
                s = jnp.where(past_mask_fns[c](block), s, -jnp.inf)
            s_ref[c] = s

    def accumulate(block, alphas):
        start = pl.multiple_of(block * tk, tk)
        for c in heads:
            pv = jnp.dot(vt_ref[c, :, pl.ds(start, tk)], p_ref[c], preferred_element_type=F32)
            acc_ref[c] = alphas[c] * acc_ref[c] + pv

    def softmax(stats, diag=None):
        new_stats, alphas = [], []
        for c in heads:
            m, l = stats[c]
            if diag is None:
                read = lambda: s_ref[c]
            else:
                visible = diag_mask_fn(diag, c)
                read = lambda: jnp.where(visible, s_ref[c], -jnp.inf)
            m_new = jnp.maximum(m, jnp.max(read(), axis=0, keepdims=True))
            alpha = jnp.exp2(m - m_new)
            p = jnp.exp2(read() - m_new)
            p_ref[c] = p.astype(BF16)
            new_stats.append((m_new, alpha * l + jnp.sum(p, axis=0, keepdims=True)))
            alphas.append(alpha)
        return tuple(new_stats), tuple(alphas)

    def past_block(j, carry):
        stats, alphas = carry
        accumulate(jnp.maximum(j - 1, 0), alphas)
        stats, alphas = softmax(stats)
        stage_scores(j + 1)
        return stats, alphas

    for c in heads:
        p_ref[c] = jnp.zeros(p_ref.shape[1:], BF16)
        acc_ref[c] = jnp.zeros(acc_ref.shape[1:], F32)
    stage_scores(0)
    init = (tuple((jnp.full((1, cols), jnp.finfo(F32).min, F32), jnp.zeros((1, cols), F32)) for _ in heads),
            tuple(jnp.ones((1, cols), F32) for _ in heads))
    stats, alphas = lax.fori_loop(0, n_past, past_block, init)

    accumulate(jnp.maximum(n_past - 1, 0), alphas)
    for d in range(n_diag):
        stats, alphas = softmax(stats, diag=d)
        if d + 1 < n_diag:
            stage_scores(n_past + d + 1)
        accumulate(n_past + d, alphas)
    return tuple((stats[c][1], acc_ref[c]) for c in heads)


def _diff_attn_body(lam_ref, subln_ref, q_ref, k_ref, v_ref, o_ref, vt_ref, qt_ref, s_ref, p_ref, acc_ref, *,
                    tq, tk, heads, lam_init):
    qi = pl.program_id(2)

    @pl.when(qi == 0)
    def _():
        _fill_transposed(vt_ref, v_ref, [c * LANES for c in range(heads)], tk)

    is_map0 = (lax.broadcasted_iota(jnp.int32, (LANES, 1), 0) % 64) < 32

    def make_score_fn(c):
        q_t = q_ref[:, _lanes(c)].T
        zero = jnp.zeros_like(q_t)
        qt_ref[c] = jnp.concatenate([jnp.where(is_map0, q_t, zero), jnp.where(is_map0, zero, q_t)], axis=1)
        return lambda start: jnp.dot(k_ref[pl.ds(start, tk), _lanes(c)], qt_ref[c], preferred_element_type=F32)

    diag = _causal_diag_masks(tk, tq, 2 * tq)
    results = _causal_flash_t([make_score_fn(c) for c in range(heads)], vt_ref, s_ref, p_ref, acc_ref,
                              qi * (tq // tk), lambda d, c: diag[d], tq // tk)

    lp = lam_ref[...]
    lam = (jnp.exp(jnp.sum(lp[0:1] * lp[1:2], axis=-1, keepdims=True))
           - jnp.exp(jnp.sum(lp[2:3] * lp[3:4], axis=-1, keepdims=True)) + lam_init)
    for c, (l, acc) in enumerate(results):
        o_t = acc / l
        o = (o_t[:, :tq] - lam * o_t[:, tq:]).T
        o_ref[:, _lanes(c)] = (_rms(o, subln_ref[...]) * (1.0 - lam_init)).astype(o_ref.dtype)


def _diff_attention(qkv, lam_params, subln, *, batch, seq, tq, tk, heads, q_col, k_col, v_col, lam_init):
    nq = seq // tq
    width = heads * LANES
    assert A_HEADS % heads == 0 and q_col % heads == 0 and k_col % heads == 0 and v_col % heads == 0
    assert seq % tq == 0 and tq % tk == 0
    return pl.pallas_call(
        functools.partial(_diff_attn_body, tq=tq, tk=tk, heads=heads, lam_init=lam_init),
        grid=(batch, A_HEADS // heads, nq),
        in_specs=[pl.BlockSpec((4, A_QK_DIM), lambda b, h, i: (0, 0)),
                  pl.BlockSpec((1, LANES), lambda b, h, i: (0, 0)),
                  pl.BlockSpec((tq, width), lambda b, h, i: (b * nq + i, q_col // heads + h)),
                  pl.BlockSpec((seq, width), lambda b, h, i: (b, k_col // heads + h)),
                  pl.BlockSpec((seq, width), lambda b, h, i: (b, v_col // heads + h))],
        out_specs=pl.BlockSpec((tq, width), lambda b, h, i: (b * nq + i, h)),
        out_shape=jax.ShapeDtypeStruct((batch * seq, A_HEADS * LANES), BF16),
        scratch_shapes=[pltpu.VMEM((heads, LANES, seq), BF16), pltpu.VMEM((heads, LANES, 2 * tq), BF16)]
        + _flash_scratch(heads, tk, 2 * tq, LANES),
        compiler_params=_params("parallel", "parallel", "arbitrary"),
        name="diff_attn",
    )(lam_params, subln.reshape(1, LANES), qkv, qkv, qkv)


def _tile_index(ref, i, tile):
    per_slab = ref.shape[1]
    if per_slab >= tile:
        runs = per_slab // tile
        return i // runs, pl.ds(pl.multiple_of((i % runs) * tile, tile), tile)
    slabs = tile // per_slab
    return pl.ds(i * slabs, slabs), slice(None)


def _load_tile(ref, i, c, tile):
    slab, rows = _tile_index(ref, i, tile)
    return ref[slab, rows, _lanes(c)].reshape(tile, LANES)


def _store_tile(ref, i, c, tile, value):
    slab, rows = _tile_index(ref, i, tile)
    per_slab = ref.shape[1]
    ref[slab, rows, _lanes(c)] = value if per_slab >= tile else value.reshape(tile // per_slab, per_slab, LANES)


def _dilated_body(q_ref, k_ref, v_ref, o_ref, lse_ref, *, tile, heads):
    length = q_ref.shape[0] * q_ref.shape[1]
    row = lax.broadcasted_iota(jnp.int32, (tile, 1), 0)

    def attend(i, c, k, v, valid):
        s = jnp.where(valid, _dot_nt(_load_tile(q_ref, i, c, tile), k), -jnp.inf)
        m = jnp.max(s, axis=-1, keepdims=True)
        p = jnp.exp(s - m)
        l = jnp.sum(p, axis=-1, keepdims=True)
        o = jnp.dot(p.astype(BF16), v, preferred_element_type=F32) / l
        _store_tile(o_ref, i, c, tile, o)
        _store_tile(lse_ref, i, c, tile, jnp.broadcast_to(m + jnp.log(l), (tile, LANES)))

    col = lax.broadcasted_iota(jnp.int32, (1, tile), 1)
    for c in range(heads):
        attend(0, c, _load_tile(k_ref, 0, c, tile), _load_tile(v_ref, 0, c, tile), col <= row)

    col2 = lax.broadcasted_iota(jnp.int32, (1, 2 * tile), 1)
    band = jnp.logical_and(col2 >= row, col2 <= row + tile)

    def body(i, _):
        for c in range(heads):
            k = jnp.concatenate([_load_tile(k_ref, i - 1, c, tile), _load_tile(k_ref, i, c, tile)], axis=0)
            v = jnp.concatenate([_load_tile(v_ref, i - 1, c, tile), _load_tile(v_ref, i, c, tile)], axis=0)
            attend(i, c, k, v, band)
        return 0

    n_tiles = length // tile
    lax.fori_loop(1, n_tiles, body, 0, unroll=max(1, min(4, n_tiles - 1)))


def _dilated_group(qkv, *, batch, seq, row_tile, group, dil, heads, q_col, k_col, v_col):
    hpg = B_HEADS_PER_GROUP
    tile = B_GROUPS[group][0] // dil
    slabs, per_slab = seq // row_tile, row_tile // dil
    assert tile == LANES and hpg % heads == 0 and (seq // dil) % tile == 0
    assert per_slab % tile == 0 or tile % per_slab == 0
    width = heads * LANES
    view = qkv.reshape(batch, slabs, dil, per_slab, qkv.shape[1])

    def in_spec(col):
        first = col + group * hpg
        assert first % heads == 0
        return pl.BlockSpec((None, slabs, None, per_slab, width),
                            lambda b, h, r: (b, 0, r, 0, first // heads + h))

    out_spec = pl.BlockSpec((None, slabs, None, per_slab, width), lambda b, h, r: (b, 0, r, 0, h))
    out_sds = jax.ShapeDtypeStruct((batch, slabs, dil, per_slab, hpg * LANES), F32)
    o, lse = pl.pallas_call(
        functools.partial(_dilated_body, tile=tile, heads=heads),
        grid=(batch, hpg // heads, dil),
        in_specs=[in_spec(q_col), in_spec(k_col), in_spec(v_col)],
        out_specs=[out_spec, out_spec],
        out_shape=[out_sds, out_sds],
        compiler_params=_params("parallel", "parallel", "parallel"),
        name=f"dilated_g{group}",
    )(view, view, view)
    shape = (batch * slabs, dil, per_slab, hpg * LANES)
    return o.reshape(shape), lse.reshape(shape)


def _merge_body(o0_ref, o1_ref, o2_ref, l0_ref, l1_ref, l2_ref, out_ref, nat_ref):
    def natural(ref, slot, c):
        dil, rows = ref.shape[0], ref.shape[1]
        if dil == 1:
            return ref[0, :, _lanes(c)]
        for r in range(dil):
            nat_ref[slot, pl.ds(r, rows, stride=dil), :] = ref[r, :, _lanes(c)]
        return nat_ref[slot]

    for c in range(out_ref.shape[1] // LANES):
        o0, o1, o2 = natural(o0_ref, None, c), natural(o1_ref, 0, c), natural(o2_ref, 1, c)
        l0, l1, l2 = natural(l0_ref, None, c), natural(l1_ref, 2, c), natural(l2_ref, 3, c)
        m = jnp.maximum(jnp.maximum(l0, l1), l2)
        w0, w1, w2 = jnp.exp(l0 - m), jnp.exp(l1 - m), jnp.exp(l2 - m)
        out_ref[:, _lanes(c)] = ((w0 * o0 + w1 * o1 + w2 * o2) / (w0 + w1 + w2)).astype(out_ref.dtype)


def _merge_groups(outs, lses):
    tiles, _, _, width = outs[0].shape
    row_tile = outs[0].shape[1] * outs[0].shape[2]
    specs = [pl.BlockSpec((None,) + a.shape[1:], lambda i: (i, 0, 0, 0)) for a in (*outs, *lses)]
    return pl.pallas_call(
        _merge_body,
        grid=(tiles,),
        in_specs=specs,
        out_specs=pl.BlockSpec((row_tile, width), lambda i: (i, 0)),
        out_shape=jax.ShapeDtypeStruct((tiles * row_tile, width), BF16),
        scratch_shapes=[pltpu.VMEM((4, row_tile, LANES), F32)],
        compiler_params=_params("parallel"),
        name="dilated_merge",
    )(*outs, *lses)


def _mla_body(q_ref, kv_ref, kr_ref, o_ref, vt_ref, qt_ref, s_ref, p_ref, acc_ref, *, tq, tk, heads):
    qi = pl.program_id(2)

    @pl.when(qi == 0)
    def _():
        _fill_transposed(vt_ref, kv_ref, [(2 * c + 1) * LANES for c in range(heads)], tk)

    def make_score_fn(c):
        qt_ref[c] = q_ref[:, _lanes(c, 2 * LANES)].T

        def score(start):
            k = jnp.concatenate([kv_ref[pl.ds(start, tk), _lanes(2 * c)], kr_ref[pl.ds(start, tk), :]], axis=1)
            return jnp.dot(k, qt_ref[c], preferred_element_type=F32)

        return score

    diag = _causal_diag_masks(tk, tq, tq)
    results = _causal_flash_t([make_score_fn(c) for c in range(heads)], vt_ref, s_ref, p_ref, acc_ref,
                              qi * (tq // tk), lambda d, c: diag[d], tq // tk)
    for c, (l, acc) in enumerate(results):
        o_ref[:, _lanes(c)] = (acc / l).T.astype(o_ref.dtype)


def _mla_attention(q, kv, k_rope_src, *, batch, seq, tq, tk, heads, k_rope_col):
    nq = seq // tq
    assert C_HEADS % heads == 0 and seq % tq == 0 and tq % tk == 0
    return pl.pallas_call(
        functools.partial(_mla_body, tq=tq, tk=tk, heads=heads),
        grid=(batch, C_HEADS // heads, nq),
        in_specs=[pl.BlockSpec((tq, heads * 2 * LANES), lambda b, h, i: (b * nq + i, h)),
                  pl.BlockSpec((seq, heads * 2 * LANES), lambda b, h, i: (b, h)),
                  pl.BlockSpec((seq, LANES), lambda b, h, i: (b, k_rope_col))],
        out_specs=pl.BlockSpec((tq, heads * LANES), lambda b, h, i: (b * nq + i, h)),
        out_shape=jax.ShapeDtypeStruct((batch * seq, C_HEADS * LANES), BF16),
        scratch_shapes=[pltpu.VMEM((heads, LANES, seq), BF16), pltpu.VMEM((heads, 2 * LANES, tq), BF16)]
        + _flash_scratch(heads, tk, tq, LANES),
        compiler_params=_params("parallel", "parallel", "arbitrary"),
        name="mla_attn",
    )(q, kv, k_rope_src)


def _moba_body(q_ref, k_ref, v_ref, o_ref, kmean_ref, vt_ref, sel_ref, qt_ref, s_ref, p_ref, acc_ref, *,
               n_blk, tq, heads):
    qi = pl.program_id(2)
    blk = MOBA_BLOCK
    n_own = tq // blk

    @pl.when(qi == 0)
    def _():
        for c in range(heads):
            k_all = k_ref[:, _lanes(c)].astype(F32).reshape(n_blk, blk, LANES)
            kmean_ref[c] = jnp.mean(k_all, axis=1)
        _fill_transposed(vt_ref, v_ref, [c * LANES for c in range(heads)], blk)

    blk_id = lax.broadcasted_iota(jnp.int32, (n_blk, 1), 0)
    query_col = lax.broadcasted_iota(jnp.int32, (1, tq), 1)
    own_in_tile = query_col // blk
    past = blk_id < qi * n_own + own_in_tile
    causal = lax.broadcasted_iota(jnp.int32, (blk, 1), 0) <= query_col % blk

    def make_score_fn(c):
        q_t = q_ref[:, _lanes(c)].T
        km = kmean_ref[c]
        km_hi = km.astype(BF16)
        rem = km - km_hi.astype(F32)
        km_mid = rem.astype(BF16)
        km_lo = (rem - km_mid.astype(F32)).astype(BF16)
        gate = (jnp.dot(km_hi, q_t, preferred_element_type=F32) + jnp.dot(km_mid, q_t, preferred_element_type=F32)
                + jnp.dot(km_lo, q_t, preferred_element_type=F32))
        gate = jnp.where(past, gate, -jnp.inf)
        selected = jnp.zeros(gate.shape, F32)
        for _ in range(min(MOBA_TOPK, n_blk - 1)):
            best = jnp.max(gate, axis=0, keepdims=True)
            first = jnp.min(jnp.where(gate == best, blk_id, n_blk), axis=0, keepdims=True)
            pick = blk_id == first
            selected = jnp.where(jnp.logical_and(pick, past), 1.0, selected)
            gate = jnp.where(pick, -jnp.inf, gate)
        sel_ref[c] = selected
        qt_ref[c] = q_t
        return lambda start: jnp.dot(k_ref[pl.ds(start, blk), _lanes(c)], qt_ref[c], preferred_element_type=F32)

    score_fns = [make_score_fn(c) for c in range(heads)]
    n_past = qi * n_own
    chosen_fns = [lambda n, c=c: jnp.logical_or(sel_ref[c, pl.ds(n, 1), :] > 0.0, n >= n_past)
                  for c in range(heads)]

    def own_block_mask(d, c):
        chosen = sel_ref[c, pl.ds(n_past + d, 1), :] > 0.0
        return jnp.logical_or(jnp.logical_and(own_in_tile == d, causal),
                              jnp.logical_and(own_in_tile > d, chosen))

    results = _causal_flash_t(score_fns, vt_ref, s_ref, p_ref, acc_ref, n_past, own_block_mask, n_own,
                              past_mask_fns=chosen_fns)
    for c, (l, acc) in enumerate(results):
        o_ref[:, _lanes(c)] = (acc / l).T.astype(o_ref.dtype)


def _moba_attention(qkv, *, batch, seq, tq, heads, q_col, k_col, v_col):
    blk = MOBA_BLOCK
    n_blk = seq // blk
    nq = seq // tq
    width = heads * LANES
    assert seq % tq == 0 and tq % blk == 0 and D_HEADS % heads == 0
    assert q_col % heads == 0 and k_col % heads == 0 and v_col % heads == 0
    return pl.pallas_call(
        functools.partial(_moba_body, n_blk=n_blk, tq=tq, heads=heads),
        grid=(batch, D_HEADS // heads, nq),
        in_specs=[pl.BlockSpec((tq, width), lambda b, h, i: (b * nq + i, q_col // heads + h)),
                  pl.BlockSpec((seq, width), lambda b, h, i: (b, k_col // heads + h)),
                  pl.BlockSpec((seq, width), lambda b, h, i: (b, v_col // heads + h))],
        out_specs=pl.BlockSpec((tq, width), lambda b, h, i: (b * nq + i, h)),
        out_shape=jax.ShapeDtypeStruct((batch * seq, D_HEADS * LANES), BF16),
        scratch_shapes=[pltpu.VMEM((heads, n_blk, LANES), F32),
                        pltpu.VMEM((heads, LANES, seq), BF16),
                        pltpu.VMEM((heads, n_blk, tq), F32),
                        pltpu.VMEM((heads, LANES, tq), BF16)] + _flash_scratch(heads, blk, tq, LANES),
        compiler_params=_params("parallel", "parallel", "arbitrary"),
        name="moba_attn",
    )(qkv, qkv, qkv)


def _rope_tables(seq):
    pos = jnp.arange(seq, dtype=F32)[:, None]

    def angles(dim):
        inv = 1.0 / (ROPE_THETA ** (jnp.arange(0, dim, 2, dtype=F32) / dim))
        return pos * inv[None, :]

    a128 = angles(HEAD_DIM)
    a64 = angles(A_QK_DIM)
    c128 = jnp.concatenate([jnp.cos(a128)] * 2, axis=1)
    s128 = jnp.concatenate([-jnp.sin(a128), jnp.sin(a128)], axis=1)
    c64 = jnp.concatenate([jnp.cos(a64)] * 4, axis=1)
    s64 = jnp.concatenate([-jnp.sin(a64)] * 2 + [jnp.sin(a64)] * 2, axis=1)
    return c128, s128, c64, s64


def _interleave_diff_heads(w):
    d = w.shape[0]
    return w.reshape(d, A_HEADS, 2, 2, 32).transpose(0, 1, 3, 2, 4).reshape(d, A_HEADS * LANES)


def _spread_rope64(w):
    z = jnp.zeros(w.shape[:-1] + (32,), w.dtype)
    return jnp.concatenate([w[..., :32], z, w[..., 32:], z], axis=-1)


def _ab_mixer(x, gain, w_in, lam_params, subln, w_out, layer_idx, rope, *, batch, seq):
    aw = A_HEADS * LANES
    w = jnp.concatenate([_interleave_diff_heads(w_in[:, :aw]), _interleave_diff_heads(w_in[:, aw:2 * aw]),
                         w_in[:, 2 * aw:]], axis=1).astype(BF16)
    a_scale = A_QK_DIM ** -0.5 * LOG2_E
    b_scale = HEAD_DIM ** -0.5
    dils = [dil for _, dil in B_GROUPS for _ in range(B_HEADS_PER_GROUP)]
    modes = ([(ROPE64, a_scale, 1)] * A_HEADS + [(ROPE64, 1.0, 1)] * A_HEADS + [(PLAIN, 1.0, 1)] * A_HEADS
             + [(ROPE128, b_scale, d) for d in dils] + [(ROPE128, 1.0, d) for d in dils]
             + [(PLAIN, 1.0, d) for d in dils])
    row_tile = 1024
    qkv = _norm_proj(x, gain, w, rope, modes, seq=seq, tm=row_tile, tn=768, out_dtype=BF16)
    lam_init = 0.8 - 0.6 * math.exp(-0.3 * layer_idx)
    oa = _diff_attention(qkv, lam_params, subln, batch=batch, seq=seq, tq=512, tk=256, heads=2,
                         q_col=0, k_col=A_HEADS, v_col=2 * A_HEADS, lam_init=lam_init)
    b0 = 3 * A_HEADS
    outs, lses = [], []
    for g, (_, dil) in enumerate(B_GROUPS):
        o, lse = _dilated_group(qkv, batch=batch, seq=seq, row_tile=row_tile, group=g, dil=dil,
                                heads=2 if dil == 1 else 4,
                                q_col=b0, k_col=b0 + B_HEADS, v_col=b0 + 2 * B_HEADS)
        outs.append(o)
        lses.append(lse)
    ob = _merge_groups(outs, lses)
    return _out_proj(oa, ob, w_out.astype(BF16), x, tm=1024, tn=1024)


def _cd_mixer(x, gain, w_in, q_norm, w_uq, kv_norm, w_ukv, w_out, rope, *, batch, seq):
    lat = C_Q_LORA + C_KV_LORA
    d_scale = HEAD_DIM ** -0.5 * LOG2_E
    w_main = jnp.concatenate([w_in[:, lat + C_ROPE:], _spread_rope64(w_in[:, lat:lat + C_ROPE])],
                             axis=1).astype(BF16)
    modes = ([(ROPE128, d_scale, 1)] * D_HEADS + [(ROPE128, 1.0, 1)] * D_HEADS + [(PLAIN, 1.0, 1)] * D_HEADS
             + [(ROPE64, 1.0, 1)])
    main = _norm_proj(x, gain, w_main, rope, modes, seq=seq, tm=1024, tn=5 * LANES, out_dtype=BF16)

    c_scale = (C_NOPE + C_ROPE) ** -0.5 * LOG2_E
    w_q = w_uq.reshape(C_Q_LORA, C_HEADS, C_NOPE + C_ROPE)
    w_q = jnp.concatenate([w_q[..., :C_NOPE], _spread_rope64(w_q[..., C_NOPE:])], axis=-1)
    w_q = w_q.reshape(C_Q_LORA, C_HEADS * 2 * LANES).astype(BF16)
    qc, kv = _mla_proj(x, gain, w_in[:, :lat].astype(BF16), q_norm, w_q, kv_norm, w_ukv.astype(BF16), rope,
                       seq=seq, tm=512, q_scale=c_scale)
    oc = _mla_attention(qc, kv, main, batch=batch, seq=seq, tq=1024, tk=256, heads=2, k_rope_col=3 * D_HEADS)
    od = _moba_attention(main, batch=batch, seq=seq, tq=512, heads=4,
                         q_col=0, k_col=D_HEADS, v_col=2 * D_HEADS)
    return _out_proj(oc, od, w_out.astype(BF16), x, tm=1024, tn=1024)


def kernel(x, p, ffn_norm, ffn_w_gu, ffn_w_down, mix_norm, ab_w_in, ab_lambda, ab_subln, ab_w_out,
           cd_w_in, cd_q_norm, cd_w_uq, cd_kv_norm, cd_w_ukv, cd_w_out, ple_norm, ple_w_gate,
           ple_w_proj, final_norm):
    batch, seq, d = x.shape
    depth = p.shape[0]
    rope = _rope_tables(seq)
    w_gu, w_down = ffn_w_gu.astype(BF16), ffn_w_down.astype(BF16)
    x = x.reshape(batch * seq, d)
    for i in range(depth):
        j = i // 2
        x = _ffn(x, ffn_norm[i, 0], w_gu, w_down, i, 0, tm=1024, tf=512)
        if i % 2 == 0:
            x = _ab_mixer(x, mix_norm[i], ab_w_in[j], ab_lambda[j], ab_subln[j], ab_w_out[j], i, rope,
                          batch=batch, seq=seq)
        else:
            x = _cd_mixer(x, mix_norm[i], cd_w_in[j], cd_q_norm[j], cd_w_uq[j], cd_kv_norm[j], cd_w_ukv[j],
                          cd_w_out[j], rope, batch=batch, seq=seq)
        x = _ffn(x, ffn_norm[i, 1], w_gu, w_down, i, 1, tm=1024, tf=512)
        x = _ple(x, p.reshape(depth * batch * seq, -1), i, ple_norm[i], ple_w_gate[i].astype(BF16),
                 ple_w_proj[i].astype(BF16), final_norm, tm=512, final_norm=(i == depth - 1))
    return x.reshape(batch, seq, d)
```

```python
import functools
import math

import jax
import jax.numpy as jnp
from jax import lax
from jax.experimental import pallas as pl
from jax.experimental.pallas import tpu as pltpu

F32 = jnp.float32
BF16 = jnp.bfloat16

D_MODEL = 2048
D_FF = 5632
PLE_DIM = 256
HEAD_DIM = 128
ROPE_THETA = 10000.0
NORM_EPS = 1e-6

A_HEADS = 8
A_QK_DIM = 64
B_GROUPS = ((128, 1), (512, 4), (2048, 16))
B_HEADS_PER_GROUP = 4
B_HEADS = B_HEADS_PER_GROUP * len(B_GROUPS)
C_HEADS = 8
C_Q_LORA = 512
C_KV_LORA = 256
C_NOPE = 128
C_ROPE = 64
D_HEADS = 8
MOBA_BLOCK = 256
MOBA_TOPK = 3

LOG2_E = 1.0 / math.log(2.0)
LANES = 128
VMEM_LIMIT = 56 * 1024 * 1024

PLAIN, ROPE128, ROPE64 = 0, 1, 2


def _rms(x, gain):
    return x * lax.rsqrt(jnp.mean(x * x, axis=-1, keepdims=True) + NORM_EPS) * gain


def _params(*semantics):
    return pltpu.CompilerParams(dimension_semantics=semantics, vmem_limit_bytes=VMEM_LIMIT)


def _norm_proj_body(x_ref, g_ref, w_ref, c128_ref, s128_ref, c64_ref, s64_ref, o_ref, xn_ref, stage_ref, *,
                    tile_patterns):
    j = pl.program_id(1)
    tm = o_ref.shape[0]

    @pl.when(j == 0)
    def _():
        xn_ref[...] = _rms(x_ref[...], g_ref[...]).astype(BF16)

    def tile(pattern):
        acc = jnp.dot(xn_ref[...], w_ref[...], preferred_element_type=F32)
        for c, (kind, scale, dil) in enumerate(pattern):
            a = acc[:, _lanes(c)]
            if scale != 1.0:
                a = a * scale
            if kind == ROPE128:
                a = a * c128_ref[...] + pltpu.roll(a, 64, 1) * s128_ref[...]
            elif kind == ROPE64:
                a = a * c64_ref[...] + pltpu.roll(a, 64, 1) * s64_ref[...]
            if dil == 1:
                o_ref[:, _lanes(c)] = a.astype(o_ref.dtype)
            else:
                rows = tm // dil
                stage_ref[c] = a
                for r in range(dil):
                    o_ref[r * rows:(r + 1) * rows, _lanes(c)] = (
                        stage_ref[c, pl.ds(r, rows, stride=dil), :].astype(o_ref.dtype))

    distinct = sorted(set(tile_patterns), key=tile_patterns.index)
    if len(distinct) == 1:
        tile(distinct[0])
    else:
        for pattern in distinct:
            tiles = [t for t, p in enumerate(tile_patterns) if p == pattern]
            cond = functools.reduce(jnp.logical_or, [j == t for t in tiles])
            pl.when(cond)(functools.partial(tile, pattern))


def _norm_proj(x, gain, w, rope, chunk_modes, *, seq, tm, tn, out_dtype):
    t, k = x.shape
    n = w.shape[1]
    per_tile = tn // LANES
    assert t % tm == 0 and n % tn == 0 and seq % tm == 0 and len(chunk_modes) * LANES == n
    tile_patterns = tuple(tuple(chunk_modes[a * per_tile:(a + 1) * per_tile]) for a in range(n // tn))
    seq_tiles = seq // tm
    tab = pl.BlockSpec((tm, LANES), lambda i, j: (i % seq_tiles, 0))
    return pl.pallas_call(
        functools.partial(_norm_proj_body, tile_patterns=tile_patterns),
        grid=(t // tm, n // tn),
        in_specs=[pl.BlockSpec((tm, k), lambda i, j: (i, 0)),
                  pl.BlockSpec((1, k), lambda i, j: (0, 0)),
                  pl.BlockSpec((k, tn), lambda i, j: (0, j)),
                  tab, tab, tab, tab],
        out_specs=pl.BlockSpec((tm, tn), lambda i, j: (i, j)),
        out_shape=jax.ShapeDtypeStruct((t, n), out_dtype),
        scratch_shapes=[pltpu.VMEM((tm, k), BF16), pltpu.VMEM((per_tile, tm, LANES), F32)],
        compiler_params=_params("parallel", "arbitrary"),
        name="norm_proj",
    )(x, gain.reshape(1, k), w, *rope)


def _out_proj_body(a1_ref, a2_ref, w1_ref, w2_ref, res_ref, o_ref):
    o_ref[...] = (res_ref[...] + jnp.dot(a1_ref[...], w1_ref[...], preferred_element_type=F32)
                  + jnp.dot(a2_ref[...], w2_ref[...], preferred_element_type=F32))


def _out_proj(a1, a2, w, res, *, tm, tn):
    t, k1 = a1.shape
    k2 = a2.shape[1]
    n = w.shape[1]
    assert t % tm == 0 and n % tn == 0 and w.shape[0] == k1 + k2 and k1 % k2 == 0
    return pl.pallas_call(
        _out_proj_body,
        grid=(t // tm, n // tn),
        in_specs=[pl.BlockSpec((tm, k1), lambda i, j: (i, 0)),
                  pl.BlockSpec((tm, k2), lambda i, j: (i, 0)),
                  pl.BlockSpec((k1, tn), lambda i, j: (0, j)),
                  pl.BlockSpec((k2, tn), lambda i, j: (k1 // k2, j)),
                  pl.BlockSpec((tm, tn), lambda i, j: (i, j))],
        out_specs=pl.BlockSpec((tm, tn), lambda i, j: (i, j)),
        out_shape=jax.ShapeDtypeStruct((t, n), F32),
        compiler_params=_params("parallel", "parallel"),
        name="out_proj",
    )(a1, a2, w, w, res)


def _mla_proj_body(x_ref, g_ref, wl_ref, qn_ref, wq_ref, kvn_ref, wkv_ref, c64_ref, s64_ref, q_ref, kv_ref, *,
                   q_lora, q_scale):
    xn = _rms(x_ref[...], g_ref[...]).astype(BF16)
    latent = jnp.dot(xn, wl_ref[...], preferred_element_type=F32)
    c_q = _rms(latent[:, :q_lora], qn_ref[...]).astype(BF16)
    c_kv = _rms(latent[:, q_lora:], kvn_ref[...]).astype(BF16)
    kv_ref[...] = jnp.dot(c_kv, wkv_ref[...], preferred_element_type=F32).astype(kv_ref.dtype)
    q = jnp.dot(c_q, wq_ref[...], preferred_element_type=F32) * q_scale
    for c in range(q_ref.shape[1] // LANES):
        a = q[:, _lanes(c)]
        if c % 2 == 1:
            a = a * c64_ref[...] + pltpu.roll(a, 64, 1) * s64_ref[...]
        q_ref[:, _lanes(c)] = a.astype(q_ref.dtype)


def _mla_proj(x, gain, w_latent, q_norm, w_q, kv_norm, w_kv, rope, *, seq, tm, q_scale):
    t, d = x.shape
    q_lora, kv_lora = w_q.shape[0], w_kv.shape[0]
    assert t % tm == 0 and seq % tm == 0 and w_latent.shape[1] == q_lora + kv_lora
    seq_tiles = seq // tm
    const = lambda i: (0, 0)
    tab = pl.BlockSpec((tm, LANES), lambda i: (i % seq_tiles, 0))
    full = lambda a: pl.BlockSpec(a.shape, const)
    out_spec = lambda a: pl.BlockSpec((tm, a.shape[1]), lambda i: (i, 0))
    return pl.pallas_call(
        functools.partial(_mla_proj_body, q_lora=q_lora, q_scale=q_scale),
        grid=(t // tm,),
        in_specs=[pl.BlockSpec((tm, d), lambda i: (i, 0)), pl.BlockSpec((1, d), const), full(w_latent),
                  pl.BlockSpec((1, q_lora), const), full(w_q), pl.BlockSpec((1, kv_lora), const), full(w_kv),
                  tab, tab],
        out_specs=[out_spec(w_q), out_spec(w_kv)],
        out_shape=[jax.ShapeDtypeStruct((t, w_q.shape[1]), BF16), jax.ShapeDtypeStruct((t, w_kv.shape[1]), BF16)],
        compiler_params=_params("parallel"),
        name="mla_proj",
    )(x, gain.reshape(1, d), w_latent, q_norm.reshape(1, q_lora), w_q, kv_norm.reshape(1, kv_lora), w_kv,
      rope[2], rope[3])


def _ffn_body(x_ref, g_ref, wg_ref, wu_ref, wd_ref, o_ref, xn_ref):
    j = pl.program_id(1)

    @pl.when(j == 0)
    def _():
        xn_ref[...] = _rms(x_ref[...], g_ref[...]).astype(BF16)
        o_ref[...] = jnp.zeros_like(o_ref)

    xn = xn_ref[...]
    g = jnp.dot(xn, wg_ref[...], preferred_element_type=F32)
    u = jnp.dot(xn, wu_ref[...], preferred_element_type=F32)
    h = (g * jax.nn.sigmoid(g) * u).astype(BF16)
    o_ref[...] += jnp.dot(h, wd_ref[...], preferred_element_type=F32)

    @pl.when(j == pl.num_programs(1) - 1)
    def _():
        o_ref[...] = x_ref[...] + 0.5 * o_ref[...]


def _ffn(x, gain, w_gu, w_down, layer, half, *, tm, tf):
    t, d = x.shape
    d_ff = w_down.shape[2]
    assert t % tm == 0 and d_ff % tf == 0
    nf = d_ff // tf
    return pl.pallas_call(
        _ffn_body,
        grid=(t // tm, nf),
        in_specs=[pl.BlockSpec((tm, d), lambda i, j: (i, 0)),
                  pl.BlockSpec((1, d), lambda i, j: (0, 0)),
                  pl.BlockSpec((None, None, d, tf), lambda i, j: (layer, half, 0, j)),
                  pl.BlockSpec((None, None, d, tf), lambda i, j: (layer, half, 0, nf + j)),
                  pl.BlockSpec((None, None, tf, d), lambda i, j: (layer, half, j, 0))],
        out_specs=pl.BlockSpec((tm, d), lambda i, j: (i, 0)),
        out_shape=jax.ShapeDtypeStruct((t, d), F32),
        scratch_shapes=[pltpu.VMEM((tm, d), BF16)],
        compiler_params=_params("parallel", "arbitrary"),
        name="ffn",
    )(x, gain.reshape(1, d), w_gu, w_gu, w_down)


def _ple_body(x_ref, p_ref, g_ref, wg_ref, wp_ref, fg_ref, o_ref, *, final_norm):
    x = x_ref[...]
    xn = _rms(x, g_ref[...]).astype(BF16)
    gate = jax.nn.sigmoid(jnp.dot(xn, wg_ref[...], preferred_element_type=F32))
    proj = jnp.dot(p_ref[...].astype(BF16), wp_ref[...], preferred_element_type=F32)
    y = x + gate * proj
    if final_norm:
        y = _rms(y, fg_ref[...])
    o_ref[...] = y


def _ple(x, p, layer, gain, w_gate, w_proj, final_gain, *, tm, final_norm):
    t, d = x.shape
    pd = p.shape[1]
    assert t % tm == 0
    first_tile = layer * (t // tm)
    const = lambda i: (0, 0)
    return pl.pallas_call(
        functools.partial(_ple_body, final_norm=final_norm),
        grid=(t // tm,),
        in_specs=[pl.BlockSpec((tm, d), lambda i: (i, 0)),
                  pl.BlockSpec((tm, pd), lambda i: (first_tile + i, 0)),
                  pl.BlockSpec((1, d), const),
                  pl.BlockSpec((d, d), const),
                  pl.BlockSpec((pd, d), const),
                  pl.BlockSpec((1, d), const)],
        out_specs=pl.BlockSpec((tm, d), lambda i: (i, 0)),
        out_shape=jax.ShapeDtypeStruct((t, d), F32),
        compiler_params=_params("parallel"),
        name="ple",
    )(x, p, gain.reshape(1, d), w_gate, w_proj, final_gain.reshape(1, d))


def _dot_nt(a, b):
    return lax.dot_general(a, b, (((1,), (1,)), ((), ())), preferred_element_type=F32)


def _lanes(c, width=LANES):
    return slice(c * width, (c + 1) * width)


def _fill_transposed(vt_ref, v_ref, lane_offsets, chunk):
    dv = vt_ref.shape[1]

    def body(j, _):
        start = pl.multiple_of(j * chunk, chunk)
        for c, off in enumerate(lane_offsets):
            vt_ref[c, :, pl.ds(start, chunk)] = v_ref[pl.ds(start, chunk), off:off + dv].T
        return 0

    lax.fori_loop(0, v_ref.shape[0] // chunk, body, 0)


def _flash_scratch(heads, tk, cols, dv):
    return [pltpu.VMEM((heads, tk, cols), F32), pltpu.VMEM((heads, tk, cols), BF16),
            pltpu.VMEM((heads, dv, cols), F32)]


def _causal_diag_masks(tk, tq, cols):
    key_row = lax.broadcasted_iota(jnp.int32, (tk, 1), 0)
    query_col = lax.broadcasted_iota(jnp.int32, (1, cols), 1) % tq
    return [key_row + d * tk <= query_col for d in range(tq // tk)]


def _causal_flash_t(score_fns, vt_ref, s_ref, p_ref, acc_ref, n_past, diag_mask_fn, n_diag, past_mask_fns=None):
    heads = range(len(score_fns))
    _, tk, cols = s_ref.shape

    def stage_scores(block):
        start = pl.multiple_of(block * tk, tk)
        for c in heads:
            s = score_fns[c](start)
            if past_mask_fns is not None:
                s = jnp.where(past_mask_fns[c](block), s, -jnp.inf)
            s_ref[c] = s

    def accumulate(block, alphas):
        start = pl.multiple_of(block * tk, tk)
        for c in heads:
            pv = jnp.dot(vt_ref[c, :, pl.ds(start, tk)], p_ref[c], preferred_element_type=F32)
            acc_ref[c] = alphas[c] * acc_ref[c] + pv

    def softmax(stats, diag=None):
        new_stats, alphas = [], []
        for c in heads:
            m, l = stats[c]
            if diag is None:
                read = lambda: s_ref[c]
            else:
                visible = diag_mask_fn(diag, c)
                read = lambda: jnp.where(visible, s_ref[c], -jnp.inf)
            m_new = jnp.maximum(m, jnp.max(read(), axis=0, keepdims=True))
            alpha = jnp.exp2(m - m_new)
            p = jnp.exp2(read() - m_new)
            p_ref[c] = p.astype(BF16)
            new_stats.append((m_new, alpha * l + jnp.sum(p, axis=0, keepdims=True)))
            alphas.append(alpha)
        return tuple(new_stats), tuple(alphas)

    def past_block(j, carry):
        stats, alphas = carry
        accumulate(jnp.maximum(j - 1, 0), alphas)
        stats, alphas = softmax(stats)
        stage_scores(j + 1)
        return stats, alphas

    for c in heads:
        p_ref[c] = jnp.zeros(p_ref.shape[1:], BF16)
        acc_ref[c] = jnp.zeros(acc_ref.shape[1:], F32)
    stage_scores(0)
    init = (tuple((jnp.full((1, cols), jnp.finfo(F32).min, F32), jnp.zeros((1, cols), F32)) for _ in heads),
            tuple(jnp.ones((1, cols), F32) for _ in heads))
    group = 2 if n_diag % 2 == 0 else 1

    def past_group(t, carry):
        for u in range(group):
            carry = past_block(t * group + u, carry)
        return carry

    stats, alphas = lax.fori_loop(0, n_past // group, past_group, init)

    accumulate(jnp.maximum(n_past - 1, 0), alphas)
    for d in range(n_diag):
        stats, alphas = softmax(stats, diag=d)
        if d + 1 < n_diag:
            stage_scores(n_past + d + 1)
        accumulate(n_past + d, alphas)
    return tuple((stats[c][1], acc_ref[c]) for c in heads)


def _diff_attn_body(lam_ref, subln_ref, q_ref, k_ref, v_ref, o_ref, vt_ref, qt_ref, s_ref, p_ref, acc_ref, *,
                    tq, tk, heads, lam_init):
    qi = pl.program_id(2)

    @pl.when(qi == 0)
    def _():
        _fill_transposed(vt_ref, v_ref, [c * LANES for c in range(heads)], tk)

    is_map0 = (lax.broadcasted_iota(jnp.int32, (LANES, 1), 0) % 64) < 32

    def make_score_fn(c):
        q_t = q_ref[:, _lanes(c)].T
        zero = jnp.zeros_like(q_t)
        qt_ref[c] = jnp.concatenate([jnp.where(is_map0, q_t, zero), jnp.where(is_map0, zero, q_t)], axis=1)
        return lambda start: jnp.dot(k_ref[pl.ds(start, tk), _lanes(c)], qt_ref[c], preferred_element_type=F32)

    diag = _causal_diag_masks(tk, tq, 2 * tq)
    results = _causal_flash_t([make_score_fn(c) for c in range(heads)], vt_ref, s_ref, p_ref, acc_ref,
                              qi * (tq // tk), lambda d, c: diag[d], tq // tk)

    lp = lam_ref[...]
    lam = (jnp.exp(jnp.sum(lp[0:1] * lp[1:2], axis=-1, keepdims=True))
           - jnp.exp(jnp.sum(lp[2:3] * lp[3:4], axis=-1, keepdims=True)) + lam_init)
    for c, (l, acc) in enumerate(results):
        o_t = acc / l
        o = (o_t[:, :tq] - lam * o_t[:, tq:]).T
        o_ref[:, _lanes(c)] = (_rms(o, subln_ref[...]) * (1.0 - lam_init)).astype(o_ref.dtype)


def _diff_attention(qkv, lam_params, subln, *, batch, seq, tq, tk, heads, q_col, k_col, v_col, lam_init):
    nq = seq // tq
    width = heads * LANES
    assert A_HEADS % heads == 0 and q_col % heads == 0 and k_col % heads == 0 and v_col % heads == 0
    assert seq % tq == 0 and tq % tk == 0
    return pl.pallas_call(
        functools.partial(_diff_attn_body, tq=tq, tk=tk, heads=heads, lam_init=lam_init),
        grid=(batch, A_HEADS // heads, nq),
        in_specs=[pl.BlockSpec((4, A_QK_DIM), lambda b, h, i: (0, 0)),
                  pl.BlockSpec((1, LANES), lambda b, h, i: (0, 0)),
                  pl.BlockSpec((tq, width), lambda b, h, i: (b * nq + i, q_col // heads + h)),
                  pl.BlockSpec((seq, width), lambda b, h, i: (b, k_col // heads + h)),
                  pl.BlockSpec((seq, width), lambda b, h, i: (b, v_col // heads + h))],
        out_specs=pl.BlockSpec((tq, width), lambda b, h, i: (b * nq + i, h)),
        out_shape=jax.ShapeDtypeStruct((batch * seq, A_HEADS * LANES), BF16),
        scratch_shapes=[pltpu.VMEM((heads, LANES, seq), BF16), pltpu.VMEM((heads, LANES, 2 * tq), BF16)]
        + _flash_scratch(heads, tk, 2 * tq, LANES),
        compiler_params=_params("parallel", "parallel", "arbitrary"),
        name="diff_attn",
    )(lam_params, subln.reshape(1, LANES), qkv, qkv, qkv)


def _tile_index(ref, r, i, tile):
    per_slab = ref.shape[2]
    if per_slab >= tile:
        runs = per_slab // tile
        return i // runs, r, pl.ds(pl.multiple_of((i % runs) * tile, tile), tile)
    slabs = tile // per_slab
    return pl.ds(i * slabs, slabs), r, slice(None)


def _load_tile(ref, r, i, c, tile):
    slab, res, rows = _tile_index(ref, r, i, tile)
    return ref[slab, res, rows, _lanes(c)].reshape(tile, LANES)


def _store_tile(ref, r, i, c, tile, value):
    slab, res, rows = _tile_index(ref, r, i, tile)
    per_slab = ref.shape[2]
    ref[slab, res, rows, _lanes(c)] = (value if per_slab >= tile
                                       else value.reshape(tile // per_slab, per_slab, LANES))


def _dilated_body(q_ref, k_ref, v_ref, o_ref, lse_ref, *, tile, heads):
    slabs, residues, per_slab, _ = q_ref.shape
    row = lax.broadcasted_iota(jnp.int32, (tile, 1), 0)
    streams = [(r, c) for r in range(residues) for c in range(heads)]

    def attend(r, i, c, k, v, valid):
        s = jnp.where(valid, _dot_nt(_load_tile(q_ref, r, i, c, tile), k), -jnp.inf)
        m = jnp.max(s, axis=-1, keepdims=True)
        p = jnp.exp(s - m)
        l = jnp.sum(p, axis=-1, keepdims=True)
        o = jnp.dot(p.astype(BF16), v, preferred_element_type=F32) / l
        _store_tile(o_ref, r, i, c, tile, o)
        _store_tile(lse_ref, r, i, c, tile, jnp.broadcast_to(m + jnp.log(l), (tile, LANES)))

    col = lax.broadcasted_iota(jnp.int32, (1, tile), 1)
    for r, c in streams:
        attend(r, 0, c, _load_tile(k_ref, r, 0, c, tile), _load_tile(v_ref, r, 0, c, tile), col <= row)

    col2 = lax.broadcasted_iota(jnp.int32, (1, 2 * tile), 1)
    band = jnp.logical_and(col2 >= row, col2 <= row + tile)

    def body(i, _):
        for r, c in streams:
            k = jnp.concatenate([_load_tile(k_ref, r, i - 1, c, tile), _load_tile(k_ref, r, i, c, tile)], axis=0)
            v = jnp.concatenate([_load_tile(v_ref, r, i - 1, c, tile), _load_tile(v_ref, r, i, c, tile)], axis=0)
            attend(r, i, c, k, v, band)
        return 0

    n_tiles = slabs * per_slab // tile
    lax.fori_loop(1, n_tiles, body, 0, unroll=max(1, min(4, n_tiles - 1)))


def _dilated_group(qkv, *, batch, seq, row_tile, group, dil, heads, residues, q_col, k_col, v_col):
    hpg = B_HEADS_PER_GROUP
    tile = B_GROUPS[group][0] // dil
    slabs, per_slab = seq // row_tile, row_tile // dil
    assert tile == LANES and hpg % heads == 0 and (seq // dil) % tile == 0 and dil % residues == 0
    assert per_slab % tile == 0 or tile % per_slab == 0
    width = heads * LANES
    view = qkv.reshape(batch, slabs, dil, per_slab, qkv.shape[1])

    def in_spec(col):
        first = col + group * hpg
        assert first % heads == 0
        return pl.BlockSpec((None, slabs, residues, per_slab, width),
                            lambda b, h, r: (b, 0, r, 0, first // heads + h))

    out_spec = pl.BlockSpec((None, slabs, residues, per_slab, width), lambda b, h, r: (b, 0, r, 0, h))
    out_sds = jax.ShapeDtypeStruct((batch, slabs, dil, per_slab, hpg * LANES), F32)
    o, lse = pl.pallas_call(
        functools.partial(_dilated_body, tile=tile, heads=heads),
        grid=(batch, hpg // heads, dil // residues),
        in_specs=[in_spec(q_col), in_spec(k_col), in_spec(v_col)],
        out_specs=[out_spec, out_spec],
        out_shape=[out_sds, out_sds],
        compiler_params=_params("parallel", "parallel", "parallel"),
        name=f"dilated_g{group}",
    )(view, view, view)
    shape = (batch * slabs, dil, per_slab, hpg * LANES)
    return o.reshape(shape), lse.reshape(shape)


def _merge_body(o0_ref, o1_ref, o2_ref, l0_ref, l1_ref, l2_ref, out_ref, nat_ref):
    def natural(ref, slot, c):
        dil, rows = ref.shape[0], ref.shape[1]
        if dil == 1:
            return ref[0, :, _lanes(c)]
        for r in range(dil):
            nat_ref[slot, pl.ds(r, rows, stride=dil), :] = ref[r, :, _lanes(c)]
        return nat_ref[slot]

    for c in range(out_ref.shape[1] // LANES):
        o0, o1, o2 = natural(o0_ref, None, c), natural(o1_ref, 0, c), natural(o2_ref, 1, c)
        l0, l1, l2 = natural(l0_ref, None, c), natural(l1_ref, 2, c), natural(l2_ref, 3, c)
        m = jnp.maximum(jnp.maximum(l0, l1), l2)
        w0, w1, w2 = jnp.exp(l0 - m), jnp.exp(l1 - m), jnp.exp(l2 - m)
        out_ref[:, _lanes(c)] = ((w0 * o0 + w1 * o1 + w2 * o2) / (w0 + w1 + w2)).astype(out_ref.dtype)


def _merge_groups(outs, lses):
    tiles, _, _, width = outs[0].shape
    row_tile = outs[0].shape[1] * outs[0].shape[2]
    specs = [pl.BlockSpec((None,) + a.shape[1:], lambda i: (i, 0, 0, 0)) for a in (*outs, *lses)]
    return pl.pallas_call(
        _merge_body,
        grid=(tiles,),
        in_specs=specs,
        out_specs=pl.BlockSpec((row_tile, width), lambda i: (i, 0)),
        out_shape=jax.ShapeDtypeStruct((tiles * row_tile, width), BF16),
        scratch_shapes=[pltpu.VMEM((4, row_tile, LANES), F32)],
        compiler_params=_params("parallel"),
        name="dilated_merge",
    )(*outs, *lses)


def _mla_body(q_ref, kv_ref, kr_ref, o_ref, vt_ref, qt_ref, s_ref, p_ref, acc_ref, *, tq, tk, heads):
    qi = pl.program_id(2)

    @pl.when(qi == 0)
    def _():
        _fill_transposed(vt_ref, kv_ref, [(2 * c + 1) * LANES for c in range(heads)], tk)

    def make_score_fn(c):
        qt_ref[c] = q_ref[:, _lanes(c, 2 * LANES)].T

        def score(start):
            k = jnp.concatenate([kv_ref[pl.ds(start, tk), _lanes(2 * c)], kr_ref[pl.ds(start, tk), :]], axis=1)
            return jnp.dot(k, qt_ref[c], preferred_element_type=F32)

        return score

    diag = _causal_diag_masks(tk, tq, tq)
    results = _causal_flash_t([make_score_fn(c) for c in range(heads)], vt_ref, s_ref, p_ref, acc_ref,
                              qi * (tq // tk), lambda d, c: diag[d], tq // tk)
    for c, (l, acc) in enumerate(results):
        o_ref[:, _lanes(c)] = (acc / l).T.astype(o_ref.dtype)


def _mla_attention(q, kv, k_rope_src, *, batch, seq, tq, tk, heads, k_rope_col):
    nq = seq // tq
    assert C_HEADS % heads == 0 and seq % tq == 0 and tq % tk == 0
    return pl.pallas_call(
        functools.partial(_mla_body, tq=tq, tk=tk, heads=heads),
        grid=(batch, C_HEADS // heads, nq),
        in_specs=[pl.BlockSpec((tq, heads * 2 * LANES), lambda b, h, i: (b * nq + i, h)),
                  pl.BlockSpec((seq, heads * 2 * LANES), lambda b, h, i: (b, h)),
                  pl.BlockSpec((seq, LANES), lambda b, h, i: (b, k_rope_col))],
        out_specs=pl.BlockSpec((tq, heads * LANES), lambda b, h, i: (b * nq + i, h)),
        out_shape=jax.ShapeDtypeStruct((batch * seq, C_HEADS * LANES), BF16),
        scratch_shapes=[pltpu.VMEM((heads, LANES, seq), BF16), pltpu.VMEM((heads, 2 * LANES, tq), BF16)]
        + _flash_scratch(heads, tk, tq, LANES),
        compiler_params=_params("parallel", "parallel", "arbitrary"),
        name="mla_attn",
    )(q, kv, k_rope_src)


def _moba_body(q_ref, k_ref, v_ref, o_ref, kmean_ref, vt_ref, sel_ref, qt_ref, s_ref, p_ref, acc_ref, *,
               n_blk, tq, heads):
    qi = pl.program_id(2)
    blk = MOBA_BLOCK
    n_own = tq // blk

    @pl.when(qi == 0)
    def _():
        for c in range(heads):
            k_all = k_ref[:, _lanes(c)].astype(F32).reshape(n_blk, blk, LANES)
            kmean_ref[c] = jnp.mean(k_all, axis=1)
        _fill_transposed(vt_ref, v_ref, [c * LANES for c in range(heads)], blk)

    blk_id = lax.broadcasted_iota(jnp.int32, (n_blk, 1), 0)
    query_col = lax.broadcasted_iota(jnp.int32, (1, tq), 1)
    own_in_tile = query_col // blk
    past = blk_id < qi * n_own + own_in_tile
    causal = lax.broadcasted_iota(jnp.int32, (blk, 1), 0) <= query_col % blk

    def make_score_fn(c):
        q_t = q_ref[:, _lanes(c)].T
        km = kmean_ref[c]
        km_hi = km.astype(BF16)
        rem = km - km_hi.astype(F32)
        km_mid = rem.astype(BF16)
        km_lo = (rem - km_mid.astype(F32)).astype(BF16)
        gate = (jnp.dot(km_hi, q_t, preferred_element_type=F32) + jnp.dot(km_mid, q_t, preferred_element_type=F32)
                + jnp.dot(km_lo, q_t, preferred_element_type=F32))
        gate = jnp.where(past, gate, -jnp.inf)
        selected = jnp.zeros(gate.shape, F32)
        for _ in range(min(MOBA_TOPK, n_blk - 1)):
            best = jnp.max(gate, axis=0, keepdims=True)
            first = jnp.min(jnp.where(gate == best, blk_id, n_blk), axis=0, keepdims=True)
            pick = blk_id == first
            selected = jnp.where(jnp.logical_and(pick, past), 1.0, selected)
            gate = jnp.where(pick, -jnp.inf, gate)
        sel_ref[c] = selected
        qt_ref[c] = q_t
        return lambda start: jnp.dot(k_ref[pl.ds(start, blk), _lanes(c)], qt_ref[c], preferred_element_type=F32)

    score_fns = [make_score_fn(c) for c in range(heads)]
    n_past = qi * n_own
    chosen_fns = [lambda n, c=c: jnp.logical_or(sel_ref[c, pl.ds(n, 1), :] > 0.0, n >= n_past)
                  for c in range(heads)]

    def own_block_mask(d, c):
        chosen = sel_ref[c, pl.ds(n_past + d, 1), :] > 0.0
        return jnp.logical_or(jnp.logical_and(own_in_tile == d, causal),
                              jnp.logical_and(own_in_tile > d, chosen))

    results = _causal_flash_t(score_fns, vt_ref, s_ref, p_ref, acc_ref, n_past, own_block_mask, n_own,
                              past_mask_fns=chosen_fns)
    for c, (l, acc) in enumerate(results):
        o_ref[:, _lanes(c)] = (acc / l).T.astype(o_ref.dtype)


def _moba_attention(qkv, *, batch, seq, tq, heads, q_col, k_col, v_col):
    blk = MOBA_BLOCK
    n_blk = seq // blk
    nq = seq // tq
    width = heads * LANES
    assert seq % tq == 0 and tq % blk == 0 and D_HEADS % heads == 0
    assert q_col % heads == 0 and k_col % heads == 0 and v_col % heads == 0
    return pl.pallas_call(
        functools.partial(_moba_body, n_blk=n_blk, tq=tq, heads=heads),
        grid=(batch, D_HEADS // heads, nq),
        in_specs=[pl.BlockSpec((tq, width), lambda b, h, i: (b * nq + i, q_col // heads + h)),
                  pl.BlockSpec((seq, width), lambda b, h, i: (b, k_col // heads + h)),
                  pl.BlockSpec((seq, width), lambda b, h, i: (b, v_col // heads + h))],
        out_specs=pl.BlockSpec((tq, width), lambda b, h, i: (b * nq + i, h)),
        out_shape=jax.ShapeDtypeStruct((batch * seq, D_HEADS * LANES), BF16),
        scratch_shapes=[pltpu.VMEM((heads, n_blk, LANES), F32),
                        pltpu.VMEM((heads, LANES, seq), BF16),
                        pltpu.VMEM((heads, n_blk, tq), F32),
                        pltpu.VMEM((heads, LANES, tq), BF16)] + _flash_scratch(heads, blk, tq, LANES),
        compiler_params=_params("parallel", "parallel", "arbitrary"),
        name="moba_attn",
    )(qkv, qkv, qkv)


def _rope_tables(seq):
    pos = jnp.arange(seq, dtype=F32)[:, None]

    def angles(dim):
        inv = 1.0 / (ROPE_THETA ** (jnp.arange(0, dim, 2, dtype=F32) / dim))
        return pos * inv[None, :]

    a128 = angles(HEAD_DIM)
    a64 = angles(A_QK_DIM)
    c128 = jnp.concatenate([jnp.cos(a128)] * 2, axis=1)
    s128 = jnp.concatenate([-jnp.sin(a128), jnp.sin(a128)], axis=1)
    c64 = jnp.concatenate([jnp.cos(a64)] * 4, axis=1)
    s64 = jnp.concatenate([-jnp.sin(a64)] * 2 + [jnp.sin(a64)] * 2, axis=1)
    return c128, s128, c64, s64


def _interleave_diff_heads(w):
    d = w.shape[0]
    return w.reshape(d, A_HEADS, 2, 2, 32).transpose(0, 1, 3, 2, 4).reshape(d, A_HEADS * LANES)


def _spread_rope64(w):
    z = jnp.zeros(w.shape[:-1] + (32,), w.dtype)
    return jnp.concatenate([w[..., :32], z, w[..., 32:], z], axis=-1)


def _ab_mixer(x, gain, w_in, lam_params, subln, w_out, layer_idx, rope, *, batch, seq):
    aw = A_HEADS * LANES
    w = jnp.concatenate([_interleave_diff_heads(w_in[:, :aw]), _interleave_diff_heads(w_in[:, aw:2 * aw]),
                         w_in[:, 2 * aw:]], axis=1).astype(BF16)
    a_scale = A_QK_DIM ** -0.5 * LOG2_E
    b_scale = HEAD_DIM ** -0.5
    dils = [dil for _, dil in B_GROUPS for _ in range(B_HEADS_PER_GROUP)]
    modes = ([(ROPE64, a_scale, 1)] * A_HEADS + [(ROPE64, 1.0, 1)] * A_HEADS + [(PLAIN, 1.0, 1)] * A_HEADS
             + [(ROPE128, b_scale, d) for d in dils] + [(ROPE128, 1.0, d) for d in dils]
             + [(PLAIN, 1.0, d) for d in dils])
    row_tile = 1024
    qkv = _norm_proj(x, gain, w, rope, modes, seq=seq, tm=row_tile, tn=768, out_dtype=BF16)
    lam_init = 0.8 - 0.6 * math.exp(-0.3 * layer_idx)
    oa = _diff_attention(qkv, lam_params, subln, batch=batch, seq=seq, tq=512, tk=256, heads=2,
                         q_col=0, k_col=A_HEADS, v_col=2 * A_HEADS, lam_init=lam_init)
    b0 = 3 * A_HEADS
    outs, lses = [], []
    for g, (_, dil) in enumerate(B_GROUPS):
        o, lse = _dilated_group(qkv, batch=batch, seq=seq, row_tile=row_tile, group=g, dil=dil,
                                heads=2 if dil == 1 else 4, residues=4 if dil == 16 else 1,
                                q_col=b0, k_col=b0 + B_HEADS, v_col=b0 + 2 * B_HEADS)
        outs.append(o)
        lses.append(lse)
    ob = _merge_groups(outs, lses)
    return _out_proj(oa, ob, w_out.astype(BF16), x, tm=1024, tn=1024)


def _cd_mixer(x, gain, w_in, q_norm, w_uq, kv_norm, w_ukv, w_out, rope, *, batch, seq):
    lat = C_Q_LORA + C_KV_LORA
    d_scale = HEAD_DIM ** -0.5 * LOG2_E
    w_main = jnp.concatenate([w_in[:, lat + C_ROPE:], _spread_rope64(w_in[:, lat:lat + C_ROPE])],
                             axis=1).astype(BF16)
    modes = ([(ROPE128, d_scale, 1)] * D_HEADS + [(ROPE128, 1.0, 1)] * D_HEADS + [(PLAIN, 1.0, 1)] * D_HEADS
             + [(ROPE64, 1.0, 1)])
    main = _norm_proj(x, gain, w_main, rope, modes, seq=seq, tm=1024, tn=5 * LANES, out_dtype=BF16)

    c_scale = (C_NOPE + C_ROPE) ** -0.5 * LOG2_E
    w_q = w_uq.reshape(C_Q_LORA, C_HEADS, C_NOPE + C_ROPE)
    w_q = jnp.concatenate([w_q[..., :C_NOPE], _spread_rope64(w_q[..., C_NOPE:])], axis=-1)
    w_q = w_q.reshape(C_Q_LORA, C_HEADS * 2 * LANES).astype(BF16)
    qc, kv = _mla_proj(x, gain, w_in[:, :lat].astype(BF16), q_norm, w_q, kv_norm, w_ukv.astype(BF16), rope,
                       seq=seq, tm=512, q_scale=c_scale)
    oc = _mla_attention(qc, kv, main, batch=batch, seq=seq, tq=1024, tk=256, heads=2, k_rope_col=3 * D_HEADS)
    od = _moba_attention(main, batch=batch, seq=seq, tq=512, heads=4,
                         q_col=0, k_col=D_HEADS, v_col=2 * D_HEADS)
    return _out_proj(oc, od, w_out.astype(BF16), x, tm=1024, tn=1024)


def kernel(x, p, ffn_norm, ffn_w_gu, ffn_w_down, mix_norm, ab_w_in, ab_lambda, ab_subln, ab_w_out,
           cd_w_in, cd_q_norm, cd_w_uq, cd_kv_norm, cd_w_ukv, cd_w_out, ple_norm, ple_w_gate,
           ple_w_proj, final_norm):
    batch, seq, d = x.shape
    depth = p.shape[0]
    rope = _rope_tables(seq)
    w_gu, w_down = ffn_w_gu.astype(BF16), ffn_w_down.astype(BF16)
    x = x.reshape(batch * seq, d)
    for i in range(depth):
        j = i // 2
        x = _ffn(x, ffn_norm[i, 0], w_gu, w_down, i, 0, tm=1024, tf=512)
        if i % 2 == 0:
            x = _ab_mixer(x, mix_norm[i], ab_w_in[j], ab_lambda[j], ab_subln[j], ab_w_out[j], i, rope,
                          batch=batch, seq=seq)
        else:
            x = _cd_mixer(x, mix_norm[i], cd_w_in[j], cd_q_norm[j], cd_w_uq[j], cd_kv_norm[j], cd_w_ukv[j],
                          cd_w_out[j], rope, batch=batch, seq=seq)
        x = _ffn(x, ffn_norm[i, 1], w_gu, w_down, i, 1, tm=1024, tf=512)
        x = _ple(x, p.reshape(depth * batch * seq, -1), i, ple_norm[i], ple_w_gate[i].astype(BF16),
                 ple_w_proj[i].astype(BF16), final_norm, tm=512, final_norm=(i == depth - 1))
    return x.reshape(batch, seq, d)
```

```python
import functools
import math

import jax
import jax.numpy as jnp
from jax import lax
from jax.experimental import pallas as pl
from jax.experimental.pallas import tpu as pltpu

F32 = jnp.float32
BF16 = jnp.bfloat16

D_MODEL = 2048
D_FF = 5632
PLE_DIM = 256
HEAD_DIM = 128
ROPE_THETA = 10000.0
NORM_EPS = 1e-6

A_HEADS = 8
A_QK_DIM = 64
B_GROUPS = ((128, 1), (512, 4), (2048, 16))
B_HEADS_PER_GROUP = 4
B_HEADS = B_HEADS_PER_GROUP * len(B_GROUPS)
C_HEADS = 8
C_Q_LORA = 512
C_KV_LORA = 256
C_NOPE = 128
C_ROPE = 64
D_HEADS = 8
MOBA_BLOCK = 256
MOBA_TOPK = 3

LOG2_E = 1.0 / math.log(2.0)
PAST_BLOCKS_PER_ITERATION = 4
LANES = 128
VMEM_LIMIT = 56 * 1024 * 1024

PLAIN, ROPE128, ROPE64 = 0, 1, 2


def _rms(x, gain):
    return x * lax.rsqrt(jnp.mean(x * x, axis=-1, keepdims=True) + NORM_EPS) * gain


def _params(*semantics):
    return pltpu.CompilerParams(dimension_semantics=semantics, vmem_limit_bytes=VMEM_LIMIT)


def _norm_proj_body(x_ref, g_ref, w_ref, c128_ref, s128_ref, c64_ref, s64_ref, o_ref, xn_ref, stage_ref, *,
                    tile_patterns):
    j = pl.program_id(1)
    tm = o_ref.shape[0]

    @pl.when(j == 0)
    def _():
        xn_ref[...] = _rms(x_ref[...], g_ref[...]).astype(BF16)

    def tile(pattern):
        acc = jnp.dot(xn_ref[...], w_ref[...], preferred_element_type=F32)
        for c, (kind, scale, dil) in enumerate(pattern):
            a = acc[:, _lanes(c)]
            if scale != 1.0:
                a = a * scale
            if kind == ROPE128:
                a = a * c128_ref[...] + pltpu.roll(a, 64, 1) * s128_ref[...]
            elif kind == ROPE64:
                a = a * c64_ref[...] + pltpu.roll(a, 64, 1) * s64_ref[...]
            if dil == 1:
                o_ref[:, _lanes(c)] = a.astype(o_ref.dtype)
            else:
                rows = tm // dil
                stage_ref[c] = a
                for r in range(dil):
                    o_ref[r * rows:(r + 1) * rows, _lanes(c)] = (
                        stage_ref[c, pl.ds(r, rows, stride=dil), :].astype(o_ref.dtype))

    distinct = sorted(set(tile_patterns), key=tile_patterns.index)
    if len(distinct) == 1:
        tile(distinct[0])
    else:
        for pattern in distinct:
            tiles = [t for t, p in enumerate(tile_patterns) if p == pattern]
            cond = functools.reduce(jnp.logical_or, [j == t for t in tiles])
            pl.when(cond)(functools.partial(tile, pattern))


def _norm_proj(x, gain, w, rope, chunk_modes, *, seq, tm, tn, out_dtype):
    t, k = x.shape
    n = w.shape[1]
    per_tile = tn // LANES
    assert t % tm == 0 and n % tn == 0 and seq % tm == 0 and len(chunk_modes) * LANES == n
    tile_patterns = tuple(tuple(chunk_modes[a * per_tile:(a + 1) * per_tile]) for a in range(n // tn))
    seq_tiles = seq // tm
    tab = pl.BlockSpec((tm, LANES), lambda i, j: (i % seq_tiles, 0))
    return pl.pallas_call(
        functools.partial(_norm_proj_body, tile_patterns=tile_patterns),
        grid=(t // tm, n // tn),
        in_specs=[pl.BlockSpec((tm, k), lambda i, j: (i, 0)),
                  pl.BlockSpec((1, k), lambda i, j: (0, 0)),
                  pl.BlockSpec((k, tn), lambda i, j: (0, j)),
                  tab, tab, tab, tab],
        out_specs=pl.BlockSpec((tm, tn), lambda i, j: (i, j)),
        out_shape=jax.ShapeDtypeStruct((t, n), out_dtype),
        scratch_shapes=[pltpu.VMEM((tm, k), BF16), pltpu.VMEM((per_tile, tm, LANES), F32)],
        compiler_params=_params("parallel", "arbitrary"),
        name="norm_proj",
    )(x, gain.reshape(1, k), w, *rope)


def _out_proj_body(a1_ref, a2_ref, w1_ref, w2_ref, res_ref, o_ref):
    o_ref[...] = (res_ref[...] + jnp.dot(a1_ref[...], w1_ref[...], preferred_element_type=F32)
                  + jnp.dot(a2_ref[...], w2_ref[...], preferred_element_type=F32))


def _out_proj(a1, a2, w, res, *, tm, tn):
    t, k1 = a1.shape
    k2 = a2.shape[1]
    n = w.shape[1]
    assert t % tm == 0 and n % tn == 0 and w.shape[0] == k1 + k2 and k1 % k2 == 0
    return pl.pallas_call(
        _out_proj_body,
        grid=(t // tm, n // tn),
        in_specs=[pl.BlockSpec((tm, k1), lambda i, j: (i, 0)),
                  pl.BlockSpec((tm, k2), lambda i, j: (i, 0)),
                  pl.BlockSpec((k1, tn), lambda i, j: (0, j)),
                  pl.BlockSpec((k2, tn), lambda i, j: (k1 // k2, j)),
                  pl.BlockSpec((tm, tn), lambda i, j: (i, j))],
        out_specs=pl.BlockSpec((tm, tn), lambda i, j: (i, j)),
        out_shape=jax.ShapeDtypeStruct((t, n), F32),
        compiler_params=_params("parallel", "parallel"),
        name="out_proj",
    )(a1, a2, w, w, res)


def _mla_proj_body(x_ref, g_ref, wl_ref, qn_ref, wq_ref, kvn_ref, wkv_ref, c64_ref, s64_ref, q_ref, kv_ref, *,
                   q_lora, q_scale):
    xn = _rms(x_ref[...], g_ref[...]).astype(BF16)
    latent = jnp.dot(xn, wl_ref[...], preferred_element_type=F32)
    c_q = _rms(latent[:, :q_lora], qn_ref[...]).astype(BF16)
    c_kv = _rms(latent[:, q_lora:], kvn_ref[...]).astype(BF16)
    kv_ref[...] = jnp.dot(c_kv, wkv_ref[...], preferred_element_type=F32).astype(kv_ref.dtype)
    q = jnp.dot(c_q, wq_ref[...], preferred_element_type=F32) * q_scale
    for c in range(q_ref.shape[1] // LANES):
        a = q[:, _lanes(c)]
        if c % 2 == 1:
            a = a * c64_ref[...] + pltpu.roll(a, 64, 1) * s64_ref[...]
        q_ref[:, _lanes(c)] = a.astype(q_ref.dtype)


def _mla_proj(x, gain, w_latent, q_norm, w_q, kv_norm, w_kv, rope, *, seq, tm, q_scale):
    t, d = x.shape
    q_lora, kv_lora = w_q.shape[0], w_kv.shape[0]
    assert t % tm == 0 and seq % tm == 0 and w_latent.shape[1] == q_lora + kv_lora
    seq_tiles = seq // tm
    const = lambda i: (0, 0)
    tab = pl.BlockSpec((tm, LANES), lambda i: (i % seq_tiles, 0))
    full = lambda a: pl.BlockSpec(a.shape, const)
    out_spec = lambda a: pl.BlockSpec((tm, a.shape[1]), lambda i: (i, 0))
    return pl.pallas_call(
        functools.partial(_mla_proj_body, q_lora=q_lora, q_scale=q_scale),
        grid=(t // tm,),
        in_specs=[pl.BlockSpec((tm, d), lambda i: (i, 0)), pl.BlockSpec((1, d), const), full(w_latent),
                  pl.BlockSpec((1, q_lora), const), full(w_q), pl.BlockSpec((1, kv_lora), const), full(w_kv),
                  tab, tab],
        out_specs=[out_spec(w_q), out_spec(w_kv)],
        out_shape=[jax.ShapeDtypeStruct((t, w_q.shape[1]), BF16), jax.ShapeDtypeStruct((t, w_kv.shape[1]), BF16)],
        compiler_params=_params("parallel"),
        name="mla_proj",
    )(x, gain.reshape(1, d), w_latent, q_norm.reshape(1, q_lora), w_q, kv_norm.reshape(1, kv_lora), w_kv,
      rope[2], rope[3])


def _ffn_body(x_ref, g_ref, wg_ref, wu_ref, wd_ref, o_ref, xn_ref):
    j = pl.program_id(1)

    @pl.when(j == 0)
    def _():
        xn_ref[...] = _rms(x_ref[...], g_ref[...]).astype(BF16)
        o_ref[...] = jnp.zeros_like(o_ref)

    xn = xn_ref[...]
    g = jnp.dot(xn, wg_ref[...], preferred_element_type=F32)
    u = jnp.dot(xn, wu_ref[...], preferred_element_type=F32)
    h = (g * jax.nn.sigmoid(g) * u).astype(BF16)
    o_ref[...] += jnp.dot(h, wd_ref[...], preferred_element_type=F32)

    @pl.when(j == pl.num_programs(1) - 1)
    def _():
        o_ref[...] = x_ref[...] + 0.5 * o_ref[...]


def _ffn(x, gain, w_gu, w_down, layer, half, *, tm, tf):
    t, d = x.shape
    d_ff = w_down.shape[2]
    assert t % tm == 0 and d_ff % tf == 0
    nf = d_ff // tf
    return pl.pallas_call(
        _ffn_body,
        grid=(t // tm, nf),
        in_specs=[pl.BlockSpec((tm, d), lambda i, j: (i, 0)),
                  pl.BlockSpec((1, d), lambda i, j: (0, 0)),
                  pl.BlockSpec((None, None, d, tf), lambda i, j: (layer, half, 0, j)),
                  pl.BlockSpec((None, None, d, tf), lambda i, j: (layer, half, 0, nf + j)),
                  pl.BlockSpec((None, None, tf, d), lambda i, j: (layer, half, j, 0))],
        out_specs=pl.BlockSpec((tm, d), lambda i, j: (i, 0)),
        out_shape=jax.ShapeDtypeStruct((t, d), F32),
        scratch_shapes=[pltpu.VMEM((tm, d), BF16)],
        compiler_params=_params("parallel", "arbitrary"),
        name="ffn",
    )(x, gain.reshape(1, d), w_gu, w_gu, w_down)


def _ple_body(x_ref, p_ref, g_ref, wg_ref, wp_ref, fg_ref, o_ref, *, final_norm):
    x = x_ref[...]
    xn = _rms(x, g_ref[...]).astype(BF16)
    gate = jax.nn.sigmoid(jnp.dot(xn, wg_ref[...], preferred_element_type=F32))
    proj = jnp.dot(p_ref[...].astype(BF16), wp_ref[...], preferred_element_type=F32)
    y = x + gate * proj
    if final_norm:
        y = _rms(y, fg_ref[...])
    o_ref[...] = y


def _ple(x, p, layer, gain, w_gate, w_proj, final_gain, *, tm, final_norm):
    t, d = x.shape
    pd = p.shape[1]
    assert t % tm == 0
    first_tile = layer * (t // tm)
    const = lambda i: (0, 0)
    return pl.pallas_call(
        functools.partial(_ple_body, final_norm=final_norm),
        grid=(t // tm,),
        in_specs=[pl.BlockSpec((tm, d), lambda i: (i, 0)),
                  pl.BlockSpec((tm, pd), lambda i: (first_tile + i, 0)),
                  pl.BlockSpec((1, d), const),
                  pl.BlockSpec((d, d), const),
                  pl.BlockSpec((pd, d), const),
                  pl.BlockSpec((1, d), const)],
        out_specs=pl.BlockSpec((tm, d), lambda i: (i, 0)),
        out_shape=jax.ShapeDtypeStruct((t, d), F32),
        compiler_params=_params("parallel"),
        name="ple",
    )(x, p, gain.reshape(1, d), w_gate, w_proj, final_gain.reshape(1, d))


def _dot_nt(a, b):
    return lax.dot_general(a, b, (((1,), (1,)), ((), ())), preferred_element_type=F32)


def _lanes(c, width=LANES):
    return slice(c * width, (c + 1) * width)


def _fill_transposed(vt_ref, v_ref, lane_offsets, chunk):
    dv = vt_ref.shape[1]

    def body(j, _):
        start = pl.multiple_of(j * chunk, chunk)
        for c, off in enumerate(lane_offsets):
            vt_ref[c, :, pl.ds(start, chunk)] = v_ref[pl.ds(start, chunk), off:off + dv].T
        return 0

    lax.fori_loop(0, v_ref.shape[0] // chunk, body, 0)


def _flash_scratch(heads, tk, cols, dv):
    return [pltpu.VMEM((heads, tk, cols), F32), pltpu.VMEM((heads, tk, cols), BF16),
            pltpu.VMEM((heads, dv, cols), F32)]


def _causal_diag_masks(tk, tq, cols):
    key_row = lax.broadcasted_iota(jnp.int32, (tk, 1), 0)
    query_col = lax.broadcasted_iota(jnp.int32, (1, cols), 1) % tq
    return [key_row + d * tk <= query_col for d in range(tq // tk)]


def _causal_flash_t(score_fns, vt_ref, s_ref, p_ref, acc_ref, n_past, diag_mask_fn, n_diag, past_mask_fns=None):
    heads = range(len(score_fns))
    _, tk, cols = s_ref.shape

    def stage_scores(block):
        start = pl.multiple_of(block * tk, tk)
        for c in heads:
            s = score_fns[c](start)
            if past_mask_fns is not None:
                s = jnp.where(past_mask_fns[c](block), s, -jnp.inf)
            s_ref[c] = s

    def accumulate(block, alphas):
        start = pl.multiple_of(block * tk, tk)
        for c in heads:
            pv = jnp.dot(vt_ref[c, :, pl.ds(start, tk)], p_ref[c], preferred_element_type=F32)
            acc_ref[c] = alphas[c] * acc_ref[c] + pv

    def softmax(stats, diag=None):
        new_stats, alphas = [], []
        for c in heads:
            m, l = stats[c]
            if diag is None:
                read = lambda: s_ref[c]
            else:
                visible = diag_mask_fn(diag, c)
                read = lambda: jnp.where(visible, s_ref[c], -jnp.inf)
            m_new = jnp.maximum(m, jnp.max(read(), axis=0, keepdims=True))
            alpha = jnp.exp2(m - m_new)
            p = jnp.exp2(read() - m_new)
            p_ref[c] = p.astype(BF16)
            new_stats.append((m_new, alpha * l + jnp.sum(p, axis=0, keepdims=True)))
            alphas.append(alpha)
        return tuple(new_stats), tuple(alphas)

    def past_block(j, carry):
        stats, alphas = carry
        accumulate(jnp.maximum(j - 1, 0), alphas)
        stats, alphas = softmax(stats)
        stage_scores(j + 1)
        return stats, alphas

    for c in heads:
        p_ref[c] = jnp.zeros(p_ref.shape[1:], BF16)
        acc_ref[c] = jnp.zeros(acc_ref.shape[1:], F32)
    stage_scores(0)
    init = (tuple((jnp.full((1, cols), jnp.finfo(F32).min, F32), jnp.zeros((1, cols), F32)) for _ in heads),
            tuple(jnp.ones((1, cols), F32) for _ in heads))
    def past_group(t, carry):
        for u in range(PAST_BLOCKS_PER_ITERATION):
            carry = past_block(t * PAST_BLOCKS_PER_ITERATION + u, carry)
        return carry

    n_grouped = n_past // PAST_BLOCKS_PER_ITERATION
    carry = lax.fori_loop(0, n_grouped, past_group, init)
    stats, alphas = lax.fori_loop(n_grouped * PAST_BLOCKS_PER_ITERATION, n_past, past_block, carry)

    accumulate(jnp.maximum(n_past - 1, 0), alphas)
    for d in range(n_diag):
        stats, alphas = softmax(stats, diag=d)
        if d + 1 < n_diag:
            stage_scores(n_past + d + 1)
        accumulate(n_past + d, alphas)
    return tuple((stats[c][1], acc_ref[c]) for c in heads)


def _diff_attn_body(lam_ref, subln_ref, q_ref, k_ref, v_ref, o_ref, vt_ref, qt_ref, s_ref, p_ref, acc_ref, *,
                    tq, tk, heads, lam_init):
    qi = pl.program_id(2)

    @pl.when(qi == 0)
    def _():
        _fill_transposed(vt_ref, v_ref, [c * LANES for c in range(heads)], tk)

    is_map0 = (lax.broadcasted_iota(jnp.int32, (LANES, 1), 0) % 64) < 32

    def make_score_fn(c):
        q_t = q_ref[:, _lanes(c)].T
        zero = jnp.zeros_like(q_t)
        qt_ref[c] = jnp.concatenate([jnp.where(is_map0, q_t, zero), jnp.where(is_map0, zero, q_t)], axis=1)
        return lambda start: jnp.dot(k_ref[pl.ds(start, tk), _lanes(c)], qt_ref[c], preferred_element_type=F32)

    diag = _causal_diag_masks(tk, tq, 2 * tq)
    results = _causal_flash_t([make_score_fn(c) for c in range(heads)], vt_ref, s_ref, p_ref, acc_ref,
                              qi * (tq // tk), lambda d, c: diag[d], tq // tk)

    lp = lam_ref[...]
    lam = (jnp.exp(jnp.sum(lp[0:1] * lp[1:2], axis=-1, keepdims=True))
           - jnp.exp(jnp.sum(lp[2:3] * lp[3:4], axis=-1, keepdims=True)) + lam_init)
    for c, (l, acc) in enumerate(results):
        o_t = acc / l
        o = (o_t[:, :tq] - lam * o_t[:, tq:]).T
        o_ref[:, _lanes(c)] = (_rms(o, subln_ref[...]) * (1.0 - lam_init)).astype(o_ref.dtype)


def _diff_attention(qkv, lam_params, subln, *, batch, seq, tq, tk, heads, q_col, k_col, v_col, lam_init):
    nq = seq // tq
    width = heads * LANES
    assert A_HEADS % heads == 0 and q_col % heads == 0 and k_col % heads == 0 and v_col % heads == 0
    assert seq % tq == 0 and tq % tk == 0
    return pl.pallas_call(
        functools.partial(_diff_attn_body, tq=tq, tk=tk, heads=heads, lam_init=lam_init),
        grid=(batch, A_HEADS // heads, nq),
        in_specs=[pl.BlockSpec((4, A_QK_DIM), lambda b, h, i: (0, 0)),
                  pl.BlockSpec((1, LANES), lambda b, h, i: (0, 0)),
                  pl.BlockSpec((tq, width), lambda b, h, i: (b * nq + i, q_col // heads + h)),
                  pl.BlockSpec((seq, width), lambda b, h, i: (b, k_col // heads + h)),
                  pl.BlockSpec((seq, width), lambda b, h, i: (b, v_col // heads + h))],
        out_specs=pl.BlockSpec((tq, width), lambda b, h, i: (b * nq + i, h)),
        out_shape=jax.ShapeDtypeStruct((batch * seq, A_HEADS * LANES), BF16),
        scratch_shapes=[pltpu.VMEM((heads, LANES, seq), BF16), pltpu.VMEM((heads, LANES, 2 * tq), BF16)]
        + _flash_scratch(heads, tk, 2 * tq, LANES),
        compiler_params=_params("parallel", "parallel", "arbitrary"),
        name="diff_attn",
    )(lam_params, subln.reshape(1, LANES), qkv, qkv, qkv)


def _tile_index(ref, r, i, tile):
    per_slab = ref.shape[2]
    if per_slab >= tile:
        runs = per_slab // tile
        return i // runs, r, pl.ds(pl.multiple_of((i % runs) * tile, tile), tile)
    slabs = tile // per_slab
    return pl.ds(i * slabs, slabs), r, slice(None)


def _load_tile(ref, r, i, c, tile):
    slab, res, rows = _tile_index(ref, r, i, tile)
    return ref[slab, res, rows, _lanes(c)].reshape(tile, LANES)


def _store_tile(ref, r, i, c, tile, value):
    slab, res, rows = _tile_index(ref, r, i, tile)
    per_slab = ref.shape[2]
    ref[slab, res, rows, _lanes(c)] = (value if per_slab >= tile
                                       else value.reshape(tile // per_slab, per_slab, LANES))


def _dilated_body(q_ref, k_ref, v_ref, o_ref, lse_ref, *, tile, heads):
    slabs, residues, per_slab, _ = q_ref.shape
    row = lax.broadcasted_iota(jnp.int32, (tile, 1), 0)
    streams = [(r, c) for r in range(residues) for c in range(heads)]

    def attend(r, i, c, k, v, valid):
        s = jnp.where(valid, _dot_nt(_load_tile(q_ref, r, i, c, tile), k), -jnp.inf)
        m = jnp.max(s, axis=-1, keepdims=True)
        p = jnp.exp(s - m)
        l = jnp.sum(p, axis=-1, keepdims=True)
        o = jnp.dot(p.astype(BF16), v, preferred_element_type=F32) / l
        _store_tile(o_ref, r, i, c, tile, o)
        _store_tile(lse_ref, r, i, c, tile, jnp.broadcast_to(m + jnp.log(l), (tile, LANES)))

    col = lax.broadcasted_iota(jnp.int32, (1, tile), 1)
    for r, c in streams:
        attend(r, 0, c, _load_tile(k_ref, r, 0, c, tile), _load_tile(v_ref, r, 0, c, tile), col <= row)

    col2 = lax.broadcasted_iota(jnp.int32, (1, 2 * tile), 1)
    band = jnp.logical_and(col2 >= row, col2 <= row + tile)

    def body(i, _):
        for r, c in streams:
            k = jnp.concatenate([_load_tile(k_ref, r, i - 1, c, tile), _load_tile(k_ref, r, i, c, tile)], axis=0)
            v = jnp.concatenate([_load_tile(v_ref, r, i - 1, c, tile), _load_tile(v_ref, r, i, c, tile)], axis=0)
            attend(r, i, c, k, v, band)
        return 0

    n_tiles = slabs * per_slab // tile
    lax.fori_loop(1, n_tiles, body, 0, unroll=max(1, min(4, n_tiles - 1)))


def _dilated_group(qkv, *, batch, seq, row_tile, group, dil, heads, residues, q_col, k_col, v_col):
    hpg = B_HEADS_PER_GROUP
    tile = B_GROUPS[group][0] // dil
    slabs, per_slab = seq // row_tile, row_tile // dil
    assert tile == LANES and hpg % heads == 0 and (seq // dil) % tile == 0 and dil % residues == 0
    assert per_slab % tile == 0 or tile % per_slab == 0
    width = heads * LANES
    view = qkv.reshape(batch, slabs, dil, per_slab, qkv.shape[1])

    def in_spec(col):
        first = col + group * hpg
        assert first % heads == 0
        return pl.BlockSpec((None, slabs, residues, per_slab, width),
                            lambda b, h, r: (b, 0, r, 0, first // heads + h))

    out_spec = pl.BlockSpec((None, slabs, residues, per_slab, width), lambda b, h, r: (b, 0, r, 0, h))
    out_sds = jax.ShapeDtypeStruct((batch, slabs, dil, per_slab, hpg * LANES), F32)
    o, lse = pl.pallas_call(
        functools.partial(_dilated_body, tile=tile, heads=heads),
        grid=(batch, hpg // heads, dil // residues),
        in_specs=[in_spec(q_col), in_spec(k_col), in_spec(v_col)],
        out_specs=[out_spec, out_spec],
        out_shape=[out_sds, out_sds],
        compiler_params=_params("parallel", "parallel", "parallel"),
        name=f"dilated_g{group}",
    )(view, view, view)
    shape = (batch * slabs, dil, per_slab, hpg * LANES)
    return o.reshape(shape), lse.reshape(shape)


def _merge_body(o0_ref, o1_ref, o2_ref, l0_ref, l1_ref, l2_ref, out_ref, nat_ref):
    def natural(ref, slot, c):
        dil, rows = ref.shape[0], ref.shape[1]
        if dil == 1:
            return ref[0, :, _lanes(c)]
        for r in range(dil):
            nat_ref[slot, pl.ds(r, rows, stride=dil), :] = ref[r, :, _lanes(c)]
        return nat_ref[slot]

    for c in range(out_ref.shape[1] // LANES):
        o0, o1, o2 = natural(o0_ref, None, c), natural(o1_ref, 0, c), natural(o2_ref, 1, c)
        l0, l1, l2 = natural(l0_ref, None, c), natural(l1_ref, 2, c), natural(l2_ref, 3, c)
        m = jnp.maximum(jnp.maximum(l0, l1), l2)
        w0, w1, w2 = jnp.exp(l0 - m), jnp.exp(l1 - m), jnp.exp(l2 - m)
        out_ref[:, _lanes(c)] = ((w0 * o0 + w1 * o1 + w2 * o2) / (w0 + w1 + w2)).astype(out_ref.dtype)


def _merge_groups(outs, lses):
    tiles, _, _, width = outs[0].shape
    row_tile = outs[0].shape[1] * outs[0].shape[2]
    specs = [pl.BlockSpec((None,) + a.shape[1:], lambda i: (i, 0, 0, 0)) for a in (*outs, *lses)]
    return pl.pallas_call(
        _merge_body,
        grid=(tiles,),
        in_specs=specs,
        out_specs=pl.BlockSpec((row_tile, width), lambda i: (i, 0)),
        out_shape=jax.ShapeDtypeStruct((tiles * row_tile, width), BF16),
        scratch_shapes=[pltpu.VMEM((4, row_tile, LANES), F32)],
        compiler_params=_params("parallel"),
        name="dilated_merge",
    )(*outs, *lses)


def _mla_body(q_ref, kv_ref, kr_ref, o_ref, vt_ref, qt_ref, s_ref, p_ref, acc_ref, *, tq, tk, heads):
    qi = pl.program_id(2)

    @pl.when(qi == 0)
    def _():
        _fill_transposed(vt_ref, kv_ref, [(2 * c + 1) * LANES for c in range(heads)], tk)

    def make_score_fn(c):
        qt_ref[c] = q_ref[:, _lanes(c, 2 * LANES)].T

        def score(start):
            k = jnp.concatenate([kv_ref[pl.ds(start, tk), _lanes(2 * c)], kr_ref[pl.ds(start, tk), :]], axis=1)
            return jnp.dot(k, qt_ref[c], preferred_element_type=F32)

        return score

    diag = _causal_diag_masks(tk, tq, tq)
    results = _causal_flash_t([make_score_fn(c) for c in range(heads)], vt_ref, s_ref, p_ref, acc_ref,
                              qi * (tq // tk), lambda d, c: diag[d], tq // tk)
    for c, (l, acc) in enumerate(results):
        o_ref[:, _lanes(c)] = (acc / l).T.astype(o_ref.dtype)


def _mla_attention(q, kv, k_rope_src, *, batch, seq, tq, tk, heads, k_rope_col):
    nq = seq // tq
    assert C_HEADS % heads == 0 and seq % tq == 0 and tq % tk == 0
    return pl.pallas_call(
        functools.partial(_mla_body, tq=tq, tk=tk, heads=heads),
        grid=(batch, C_HEADS // heads, nq),
        in_specs=[pl.BlockSpec((tq, heads * 2 * LANES), lambda b, h, i: (b * nq + i, h)),
                  pl.BlockSpec((seq, heads * 2 * LANES), lambda b, h, i: (b, h)),
                  pl.BlockSpec((seq, LANES), lambda b, h, i: (b, k_rope_col))],
        out_specs=pl.BlockSpec((tq, heads * LANES), lambda b, h, i: (b * nq + i, h)),
        out_shape=jax.ShapeDtypeStruct((batch * seq, C_HEADS * LANES), BF16),
        scratch_shapes=[pltpu.VMEM((heads, LANES, seq), BF16), pltpu.VMEM((heads, 2 * LANES, tq), BF16)]
        + _flash_scratch(heads, tk, tq, LANES),
        compiler_params=_params("parallel", "parallel", "arbitrary"),
        name="mla_attn",
    )(q, kv, k_rope_src)


def _moba_body(q_ref, k_ref, v_ref, o_ref, kmean_ref, vt_ref, sel_ref, qt_ref, s_ref, p_ref, acc_ref, *,
               n_blk, tq, heads):
    qi = pl.program_id(2)
    blk = MOBA_BLOCK
    n_own = tq // blk

    @pl.when(qi == 0)
    def _():
        for c in range(heads):
            k_all = k_ref[:, _lanes(c)].astype(F32).reshape(n_blk, blk, LANES)
            kmean_ref[c] = jnp.mean(k_all, axis=1)
        _fill_transposed(vt_ref, v_ref, [c * LANES for c in range(heads)], blk)

    blk_id = lax.broadcasted_iota(jnp.int32, (n_blk, 1), 0)
    query_col = lax.broadcasted_iota(jnp.int32, (1, tq), 1)
    own_in_tile = query_col // blk
    past = blk_id < qi * n_own + own_in_tile
    causal = lax.broadcasted_iota(jnp.int32, (blk, 1), 0) <= query_col % blk

    def make_score_fn(c):
        q_t = q_ref[:, _lanes(c)].T
        km = kmean_ref[c]
        km_hi = km.astype(BF16)
        rem = km - km_hi.astype(F32)
        km_mid = rem.astype(BF16)
        km_lo = (rem - km_mid.astype(F32)).astype(BF16)
        gate = (jnp.dot(km_hi, q_t, preferred_element_type=F32) + jnp.dot(km_mid, q_t, preferred_element_type=F32)
                + jnp.dot(km_lo, q_t, preferred_element_type=F32))
        gate = jnp.where(past, gate, -jnp.inf)
        selected = jnp.zeros(gate.shape, F32)
        for _ in range(min(MOBA_TOPK, n_blk - 1)):
            best = jnp.max(gate, axis=0, keepdims=True)
            first = jnp.min(jnp.where(gate == best, blk_id, n_blk), axis=0, keepdims=True)
            pick = blk_id == first
            selected = jnp.where(jnp.logical_and(pick, past), 1.0, selected)
            gate = jnp.where(pick, -jnp.inf, gate)
        sel_ref[c] = selected
        qt_ref[c] = q_t
        return lambda start: jnp.dot(k_ref[pl.ds(start, blk), _lanes(c)], qt_ref[c], preferred_element_type=F32)

    score_fns = [make_score_fn(c) for c in range(heads)]
    n_past = qi * n_own
    chosen_fns = [lambda n, c=c: jnp.logical_or(sel_ref[c, pl.ds(n, 1), :] > 0.0, n >= n_past)
                  for c in range(heads)]

    def own_block_mask(d, c):
        chosen = sel_ref[c, pl.ds(n_past + d, 1), :] > 0.0
        return jnp.logical_or(jnp.logical_and(own_in_tile == d, causal),
                              jnp.logical_and(own_in_tile > d, chosen))

    results = _causal_flash_t(score_fns, vt_ref, s_ref, p_ref, acc_ref, n_past, own_block_mask, n_own,
                              past_mask_fns=chosen_fns)
    for c, (l, acc) in enumerate(results):
        o_ref[:, _lanes(c)] = (acc / l).T.astype(o_ref.dtype)


def _moba_attention(qkv, *, batch, seq, tq, heads, q_col, k_col, v_col):
    blk = MOBA_BLOCK
    n_blk = seq // blk
    nq = seq // tq
    width = heads * LANES
    assert seq % tq == 0 and tq % blk == 0 and D_HEADS % heads == 0
    assert q_col % heads == 0 and k_col % heads == 0 and v_col % heads == 0
    return pl.pallas_call(
        functools.partial(_moba_body, n_blk=n_blk, tq=tq, heads=heads),
        grid=(batch, D_HEADS // heads, nq),
        in_specs=[pl.BlockSpec((tq, width), lambda b, h, i: (b * nq + i, q_col // heads + h)),
                  pl.BlockSpec((seq, width), lambda b, h, i: (b, k_col // heads + h)),
                  pl.BlockSpec((seq, width), lambda b, h, i: (b, v_col // heads + h))],
        out_specs=pl.BlockSpec((tq, width), lambda b, h, i: (b * nq + i, h)),
        out_shape=jax.ShapeDtypeStruct((batch * seq, D_HEADS * LANES), BF16),
        scratch_shapes=[pltpu.VMEM((heads, n_blk, LANES), F32),
                        pltpu.VMEM((heads, LANES, seq), BF16),
                        pltpu.VMEM((heads, n_blk, tq), F32),
                        pltpu.VMEM((heads, LANES, tq), BF16)] + _flash_scratch(heads, blk, tq, LANES),
        compiler_params=_params("parallel", "parallel", "arbitrary"),
        name="moba_attn",
    )(qkv, qkv, qkv)


def _rope_tables(seq):
    pos = jnp.arange(seq, dtype=F32)[:, None]

    def angles(dim):
        inv = 1.0 / (ROPE_THETA ** (jnp.arange(0, dim, 2, dtype=F32) / dim))
        return pos * inv[None, :]

    a128 = angles(HEAD_DIM)
    a64 = angles(A_QK_DIM)
    c128 = jnp.concatenate([jnp.cos(a128)] * 2, axis=1)
    s128 = jnp.concatenate([-jnp.sin(a128), jnp.sin(a128)], axis=1)
    c64 = jnp.concatenate([jnp.cos(a64)] * 4, axis=1)
    s64 = jnp.concatenate([-jnp.sin(a64)] * 2 + [jnp.sin(a64)] * 2, axis=1)
    return c128, s128, c64, s64


def _interleave_diff_heads(w):
    d = w.shape[0]
    return w.reshape(d, A_HEADS, 2, 2, 32).transpose(0, 1, 3, 2, 4).reshape(d, A_HEADS * LANES)


def _spread_rope64(w):
    z = jnp.zeros(w.shape[:-1] + (32,), w.dtype)
    return jnp.concatenate([w[..., :32], z, w[..., 32:], z], axis=-1)


def _ab_mixer(x, gain, w_in, lam_params, subln, w_out, layer_idx, rope, *, batch, seq):
    aw = A_HEADS * LANES
    w = jnp.concatenate([_interleave_diff_heads(w_in[:, :aw]), _interleave_diff_heads(w_in[:, aw:2 * aw]),
                         w_in[:, 2 * aw:]], axis=1).astype(BF16)
    a_scale = A_QK_DIM ** -0.5 * LOG2_E
    b_scale = HEAD_DIM ** -0.5
    dils = [dil for _, dil in B_GROUPS for _ in range(B_HEADS_PER_GROUP)]
    modes = ([(ROPE64, a_scale, 1)] * A_HEADS + [(ROPE64, 1.0, 1)] * A_HEADS + [(PLAIN, 1.0, 1)] * A_HEADS
             + [(ROPE128, b_scale, d) for d in dils] + [(ROPE128, 1.0, d) for d in dils]
             + [(PLAIN, 1.0, d) for d in dils])
    row_tile = 1024
    qkv = _norm_proj(x, gain, w, rope, modes, seq=seq, tm=row_tile, tn=768, out_dtype=BF16)
    lam_init = 0.8 - 0.6 * math.exp(-0.3 * layer_idx)
    oa = _diff_attention(qkv, lam_params, subln, batch=batch, seq=seq, tq=512, tk=256, heads=2,
                         q_col=0, k_col=A_HEADS, v_col=2 * A_HEADS, lam_init=lam_init)
    b0 = 3 * A_HEADS
    outs, lses = [], []
    for g, (_, dil) in enumerate(B_GROUPS):
        o, lse = _dilated_group(qkv, batch=batch, seq=seq, row_tile=row_tile, group=g, dil=dil,
                                heads=2 if dil == 1 else 4, residues=4 if dil == 16 else 1,
                                q_col=b0, k_col=b0 + B_HEADS, v_col=b0 + 2 * B_HEADS)
        outs.append(o)
        lses.append(lse)
    ob = _merge_groups(outs, lses)
    return _out_proj(oa, ob, w_out.astype(BF16), x, tm=1024, tn=1024)


def _cd_mixer(x, gain, w_in, q_norm, w_uq, kv_norm, w_ukv, w_out, rope, *, batch, seq):
    lat = C_Q_LORA + C_KV_LORA
    d_scale = HEAD_DIM ** -0.5 * LOG2_E
    w_main = jnp.concatenate([w_in[:, lat + C_ROPE:], _spread_rope64(w_in[:, lat:lat + C_ROPE])],
                             axis=1).astype(BF16)
    modes = ([(ROPE128, d_scale, 1)] * D_HEADS + [(ROPE128, 1.0, 1)] * D_HEADS + [(PLAIN, 1.0, 1)] * D_HEADS
             + [(ROPE64, 1.0, 1)])
    main = _norm_proj(x, gain, w_main, rope, modes, seq=seq, tm=1024, tn=5 * LANES, out_dtype=BF16)

    c_scale = (C_NOPE + C_ROPE) ** -0.5 * LOG2_E
    w_q = w_uq.reshape(C_Q_LORA, C_HEADS, C_NOPE + C_ROPE)
    w_q = jnp.concatenate([w_q[..., :C_NOPE], _spread_rope64(w_q[..., C_NOPE:])], axis=-1)
    w_q = w_q.reshape(C_Q_LORA, C_HEADS * 2 * LANES).astype(BF16)
    qc, kv = _mla_proj(x, gain, w_in[:, :lat].astype(BF16), q_norm, w_q, kv_norm, w_ukv.astype(BF16), rope,
                       seq=seq, tm=512, q_scale=c_scale)
    oc = _mla_attention(qc, kv, main, batch=batch, seq=seq, tq=1024, tk=256, heads=2, k_rope_col=3 * D_HEADS)
    od = _moba_attention(main, batch=batch, seq=seq, tq=512, heads=4,
                         q_col=0, k_col=D_HEADS, v_col=2 * D_HEADS)
    return _out_proj(oc, od, w_out.astype(BF16), x, tm=1024, tn=1024)


def kernel(x, p, ffn_norm, ffn_w_gu, ffn_w_down, mix_norm, ab_w_in, ab_lambda, ab_subln, ab_w_out,
           cd_w_in, cd_q_norm, cd_w_uq, cd_kv_norm, cd_w_ukv, cd_w_out, ple_norm, ple_w_gate,
           ple_w_proj, final_norm):
    batch, seq, d = x.shape
    depth = p.shape[0]
    rope = _rope_tables(seq)
    w_gu, w_down = ffn_w_gu.astype(BF16), ffn_w_down.astype(BF16)
    x = x.reshape(batch * seq, d)
    for i in range(depth):
        j = i // 2
        x = _ffn(x, ffn_norm[i, 0], w_gu, w_down, i, 0, tm=1024, tf=512)
        if i % 2 == 0:
            x = _ab_mixer(x, mix_norm[i], ab_w_in[j], ab_lambda[j], ab_subln[j], ab_w_out[j], i, rope,
                          batch=batch, seq=seq)
        else:
            x = _cd_mixer(x, mix_norm[i], cd_w_in[j], cd_q_norm[j], cd_w_uq[j], cd_kv_norm[j], cd_w_ukv[j],
                          cd_w_out[j], rope, batch=batch, seq=seq)
        x = _ffn(x, ffn_norm[i, 1], w_gu, w_down, i, 1, tm=1024, tf=512)
        x = _ple(x, p.reshape(depth * batch * seq, -1), i, ple_norm[i], ple_w_gate[i].astype(BF16),
                 ple_w_proj[i].astype(BF16), final_norm, tm=512, final_norm=(i == depth - 1))
    return x.reshape(batch, seq, d)
```

```python
import functools
import math

import jax
import jax.numpy as jnp
from jax import lax
from jax.experimental import pallas as pl
from jax.experimental.pallas import tpu as pltpu

F32 = jnp.float32
BF16 = jnp.bfloat16

D_MODEL = 2048
D_FF = 5632
PLE_DIM = 256
HEAD_DIM = 128
ROPE_THETA = 10000.0
NORM_EPS = 1e-6

A_HEADS = 8
A_QK_DIM = 64
B_GROUPS = ((128, 1), (512, 4), (2048, 16))
B_HEADS_PER_GROUP = 4
B_HEADS = B_HEADS_PER_GROUP * len(B_GROUPS)
C_HEADS = 8
C_Q_LORA = 512
C_KV_LORA = 256
C_NOPE = 128
C_ROPE = 64
D_HEADS = 8
MOBA_BLOCK = 256
MOBA_TOPK = 3

LOG2_E = 1.0 / math.log(2.0)
PAST_BLOCKS_PER_ITERATION = 4
LANES = 128
VMEM_LIMIT = 56 * 1024 * 1024

PLAIN, ROPE128, ROPE64 = 0, 1, 2


def _rms(x, gain):
    return x * lax.rsqrt(jnp.mean(x * x, axis=-1, keepdims=True) + NORM_EPS) * gain


def _params(*semantics):
    return pltpu.CompilerParams(dimension_semantics=semantics, vmem_limit_bytes=VMEM_LIMIT)


def _norm_proj_body(x_ref, g_ref, w_ref, c128_ref, s128_ref, c64_ref, s64_ref, o_ref, xn_ref, stage_ref, *,
                    tile_patterns):
    j = pl.program_id(1)
    tm = o_ref.shape[0]

    @pl.when(j == 0)
    def _():
        xn_ref[...] = _rms(x_ref[...], g_ref[...]).astype(BF16)

    def tile(pattern):
        acc = jnp.dot(xn_ref[...], w_ref[...], preferred_element_type=F32)
        for c, (kind, scale, dil) in enumerate(pattern):
            a = acc[:, _lanes(c)]
            if scale != 1.0:
                a = a * scale
            if kind == ROPE128:
                a = a * c128_ref[...] + pltpu.roll(a, 64, 1) * s128_ref[...]
            elif kind == ROPE64:
                a = a * c64_ref[...] + pltpu.roll(a, 64, 1) * s64_ref[...]
            if dil == 1:
                o_ref[:, _lanes(c)] = a.astype(o_ref.dtype)
            else:
                rows = tm // dil
                stage_ref[c] = a
                for r in range(dil):
                    o_ref[r * rows:(r + 1) * rows, _lanes(c)] = (
                        stage_ref[c, pl.ds(r, rows, stride=dil), :].astype(o_ref.dtype))

    distinct = sorted(set(tile_patterns), key=tile_patterns.index)
    if len(distinct) == 1:
        tile(distinct[0])
    else:
        for pattern in distinct:
            tiles = [t for t, p in enumerate(tile_patterns) if p == pattern]
            cond = functools.reduce(jnp.logical_or, [j == t for t in tiles])
            pl.when(cond)(functools.partial(tile, pattern))


def _norm_proj(x, gain, w, rope, chunk_modes, *, seq, tm, tn, out_dtype):
    t, k = x.shape
    n = w.shape[1]
    per_tile = tn // LANES
    assert t % tm == 0 and n % tn == 0 and seq % tm == 0 and len(chunk_modes) * LANES == n
    tile_patterns = tuple(tuple(chunk_modes[a * per_tile:(a + 1) * per_tile]) for a in range(n // tn))
    seq_tiles = seq // tm
    tab = pl.BlockSpec((tm, LANES), lambda i, j: (i % seq_tiles, 0))
    return pl.pallas_call(
        functools.partial(_norm_proj_body, tile_patterns=tile_patterns),
        grid=(t // tm, n // tn),
        in_specs=[pl.BlockSpec((tm, k), lambda i, j: (i, 0)),
                  pl.BlockSpec((1, k), lambda i, j: (0, 0)),
                  pl.BlockSpec((k, tn), lambda i, j: (0, j)),
                  tab, tab, tab, tab],
        out_specs=pl.BlockSpec((tm, tn), lambda i, j: (i, j)),
        out_shape=jax.ShapeDtypeStruct((t, n), out_dtype),
        scratch_shapes=[pltpu.VMEM((tm, k), BF16), pltpu.VMEM((per_tile, tm, LANES), F32)],
        compiler_params=_params("parallel", "arbitrary"),
        name="norm_proj",
    )(x, gain.reshape(1, k), w, *rope)


def _out_proj_body(a1_ref, a2_ref, w1_ref, w2_ref, res_ref, o_ref):
    o_ref[...] = (res_ref[...] + jnp.dot(a1_ref[...], w1_ref[...], preferred_element_type=F32)
                  + jnp.dot(a2_ref[...], w2_ref[...], preferred_element_type=F32))


def _out_proj(a1, a2, w, res, *, tm, tn):
    t, k1 = a1.shape
    k2 = a2.shape[1]
    n = w.shape[1]
    assert t % tm == 0 and n % tn == 0 and w.shape[0] == k1 + k2 and k1 % k2 == 0
    return pl.pallas_call(
        _out_proj_body,
        grid=(t // tm, n // tn),
        in_specs=[pl.BlockSpec((tm, k1), lambda i, j: (i, 0)),
                  pl.BlockSpec((tm, k2), lambda i, j: (i, 0)),
                  pl.BlockSpec((k1, tn), lambda i, j: (0, j)),
                  pl.BlockSpec((k2, tn), lambda i, j: (k1 // k2, j)),
                  pl.BlockSpec((tm, tn), lambda i, j: (i, j))],
        out_specs=pl.BlockSpec((tm, tn), lambda i, j: (i, j)),
        out_shape=jax.ShapeDtypeStruct((t, n), F32),
        compiler_params=_params("parallel", "parallel"),
        name="out_proj",
    )(a1, a2, w, w, res)


def _mla_proj_body(x_ref, g_ref, wl_ref, qn_ref, wq_ref, kvn_ref, wkv_ref, c64_ref, s64_ref, q_ref, kv_ref, *,
                   q_lora, q_scale):
    xn = _rms(x_ref[...], g_ref[...]).astype(BF16)
    latent = jnp.dot(xn, wl_ref[...], preferred_element_type=F32)
    c_q = _rms(latent[:, :q_lora], qn_ref[...]).astype(BF16)
    c_kv = _rms(latent[:, q_lora:], kvn_ref[...]).astype(BF16)
    kv_ref[...] = jnp.dot(c_kv, wkv_ref[...], preferred_element_type=F32).astype(kv_ref.dtype)
    q = jnp.dot(c_q, wq_ref[...], preferred_element_type=F32) * q_scale
    for c in range(q_ref.shape[1] // LANES):
        a = q[:, _lanes(c)]
        if c % 2 == 1:
            a = a * c64_ref[...] + pltpu.roll(a, 64, 1) * s64_ref[...]
        q_ref[:, _lanes(c)] = a.astype(q_ref.dtype)


def _mla_proj(x, gain, w_latent, q_norm, w_q, kv_norm, w_kv, rope, *, seq, tm, q_scale):
    t, d = x.shape
    q_lora, kv_lora = w_q.shape[0], w_kv.shape[0]
    assert t % tm == 0 and seq % tm == 0 and w_latent.shape[1] == q_lora + kv_lora
    seq_tiles = seq // tm
    const = lambda i: (0, 0)
    tab = pl.BlockSpec((tm, LANES), lambda i: (i % seq_tiles, 0))
    full = lambda a: pl.BlockSpec(a.shape, const)
    out_spec = lambda a: pl.BlockSpec((tm, a.shape[1]), lambda i: (i, 0))
    return pl.pallas_call(
        functools.partial(_mla_proj_body, q_lora=q_lora, q_scale=q_scale),
        grid=(t // tm,),
        in_specs=[pl.BlockSpec((tm, d), lambda i: (i, 0)), pl.BlockSpec((1, d), const), full(w_latent),
                  pl.BlockSpec((1, q_lora), const), full(w_q), pl.BlockSpec((1, kv_lora), const), full(w_kv),
                  tab, tab],
        out_specs=[out_spec(w_q), out_spec(w_kv)],
        out_shape=[jax.ShapeDtypeStruct((t, w_q.shape[1]), BF16), jax.ShapeDtypeStruct((t, w_kv.shape[1]), BF16)],
        compiler_params=_params("parallel"),
        name="mla_proj",
    )(x, gain.reshape(1, d), w_latent, q_norm.reshape(1, q_lora), w_q, kv_norm.reshape(1, kv_lora), w_kv,
      rope[2], rope[3])


def _ffn_body(x_ref, g_ref, wg_ref, wu_ref, wd_ref, o_ref, xn_ref):
    j = pl.program_id(1)

    @pl.when(j == 0)
    def _():
        xn_ref[...] = _rms(x_ref[...], g_ref[...]).astype(BF16)
        o_ref[...] = jnp.zeros_like(o_ref)

    xn = xn_ref[...]
    g = jnp.dot(xn, wg_ref[...], preferred_element_type=F32)
    u = jnp.dot(xn, wu_ref[...], preferred_element_type=F32)
    h = (g * jax.nn.sigmoid(g) * u).astype(BF16)
    o_ref[...] += jnp.dot(h, wd_ref[...], preferred_element_type=F32)

    @pl.when(j == pl.num_programs(1) - 1)
    def _():
        o_ref[...] = x_ref[...] + 0.5 * o_ref[...]


def _ffn(x, gain, w_gu, w_down, layer, half, *, tm, tf):
    t, d = x.shape
    d_ff = w_down.shape[2]
    assert t % tm == 0 and d_ff % tf == 0
    nf = d_ff // tf
    return pl.pallas_call(
        _ffn_body,
        grid=(t // tm, nf),
        in_specs=[pl.BlockSpec((tm, d), lambda i, j: (i, 0)),
                  pl.BlockSpec((1, d), lambda i, j: (0, 0)),
                  pl.BlockSpec((None, None, d, tf), lambda i, j: (layer, half, 0, j)),
                  pl.BlockSpec((None, None, d, tf), lambda i, j: (layer, half, 0, nf + j)),
                  pl.BlockSpec((None, None, tf, d), lambda i, j: (layer, half, j, 0))],
        out_specs=pl.BlockSpec((tm, d), lambda i, j: (i, 0)),
        out_shape=jax.ShapeDtypeStruct((t, d), F32),
        scratch_shapes=[pltpu.VMEM((tm, d), BF16)],
        compiler_params=_params("parallel", "arbitrary"),
        name="ffn",
    )(x, gain.reshape(1, d), w_gu, w_gu, w_down)


def _ple_body(x_ref, p_ref, g_ref, wg_ref, wp_ref, fg_ref, o_ref, *, final_norm):
    x = x_ref[...]
    xn = _rms(x, g_ref[...]).astype(BF16)
    gate = jax.nn.sigmoid(jnp.dot(xn, wg_ref[...], preferred_element_type=F32))
    proj = jnp.dot(p_ref[...].astype(BF16), wp_ref[...], preferred_element_type=F32)
    y = x + gate * proj
    if final_norm:
        y = _rms(y, fg_ref[...])
    o_ref[...] = y


def _ple(x, p, layer, gain, w_gate, w_proj, final_gain, *, tm, final_norm):
    t, d = x.shape
    pd = p.shape[1]
    assert t % tm == 0
    first_tile = layer * (t // tm)
    const = lambda i: (0, 0)
    return pl.pallas_call(
        functools.partial(_ple_body, final_norm=final_norm),
        grid=(t // tm,),
        in_specs=[pl.BlockSpec((tm, d), lambda i: (i, 0)),
                  pl.BlockSpec((tm, pd), lambda i: (first_tile + i, 0)),
                  pl.BlockSpec((1, d), const),
                  pl.BlockSpec((d, d), const),
                  pl.BlockSpec((pd, d), const),
                  pl.BlockSpec((1, d), const)],
        out_specs=pl.BlockSpec((tm, d), lambda i: (i, 0)),
        out_shape=jax.ShapeDtypeStruct((t, d), F32),
        compiler_params=_params("parallel"),
        name="ple",
    )(x, p, gain.reshape(1, d), w_gate, w_proj, final_gain.reshape(1, d))


def _dot_nt(a, b):
    return lax.dot_general(a, b, (((1,), (1,)), ((), ())), preferred_element_type=F32)


def _lanes(c, width=LANES):
    return slice(c * width, (c + 1) * width)


def _fill_transposed(vt_ref, v_ref, lane_offsets, chunk):
    dv = vt_ref.shape[1]

    def body(j, _):
        start = pl.multiple_of(j * chunk, chunk)
        for c, off in enumerate(lane_offsets):
            vt_ref[c, :, pl.ds(start, chunk)] = v_ref[pl.ds(start, chunk), off:off + dv].T
        return 0

    lax.fori_loop(0, v_ref.shape[0] // chunk, body, 0)


def _flash_scratch(heads, tk, cols, dv):
    return [pltpu.VMEM((heads, tk, cols), F32), pltpu.VMEM((heads, tk, cols), BF16),
            pltpu.VMEM((heads, dv, cols), F32)]


def _causal_diag_masks(tk, tq, cols):
    key_row = lax.broadcasted_iota(jnp.int32, (tk, 1), 0)
    query_col = lax.broadcasted_iota(jnp.int32, (1, cols), 1) % tq
    return [key_row + d * tk <= query_col for d in range(tq // tk)]


def _causal_flash_t(score_fns, vt_ref, s_ref, p_ref, acc_ref, n_past, diag_mask_fn, n_diag, past_mask_fns=None):
    heads = range(len(score_fns))
    _, tk, cols = s_ref.shape

    def stage_scores(block):
        start = pl.multiple_of(block * tk, tk)
        for c in heads:
            s = score_fns[c](start)
            if past_mask_fns is not None:
                s = jnp.where(past_mask_fns[c](block), s, -jnp.inf)
            s_ref[c] = s

    def accumulate(block, alphas):
        start = pl.multiple_of(block * tk, tk)
        for c in heads:
            pv = jnp.dot(vt_ref[c, :, pl.ds(start, tk)], p_ref[c], preferred_element_type=F32)
            acc_ref[c] = alphas[c] * acc_ref[c] + pv

    def softmax(stats, diag=None):
        new_stats, alphas = [], []
        for c in heads:
            m, l = stats[c]
            if diag is None:
                read = lambda: s_ref[c]
            else:
                visible = diag_mask_fn(diag, c)
                read = lambda: jnp.where(visible, s_ref[c], -jnp.inf)
            m_new = jnp.maximum(m, jnp.max(read(), axis=0, keepdims=True))
            alpha = jnp.exp2(m - m_new)
            p = jnp.exp2(read() - m_new)
            p_ref[c] = p.astype(BF16)
            new_stats.append((m_new, alpha * l + jnp.sum(p, axis=0, keepdims=True)))
            alphas.append(alpha)
        return tuple(new_stats), tuple(alphas)

    def past_block(j, carry):
        stats, alphas = carry
        accumulate(jnp.maximum(j - 1, 0), alphas)
        stats, alphas = softmax(stats)
        stage_scores(j + 1)
        return stats, alphas

    for c in heads:
        p_ref[c] = jnp.zeros(p_ref.shape[1:], BF16)
        acc_ref[c] = jnp.zeros(acc_ref.shape[1:], F32)
    stage_scores(0)
    init = (tuple((jnp.full((1, cols), jnp.finfo(F32).min, F32), jnp.zeros((1, cols), F32)) for _ in heads),
            tuple(jnp.ones((1, cols), F32) for _ in heads))
    def past_group(t, carry):
        for u in range(PAST_BLOCKS_PER_ITERATION):
            carry = past_block(t * PAST_BLOCKS_PER_ITERATION + u, carry)
        return carry

    n_grouped = n_past // PAST_BLOCKS_PER_ITERATION
    carry = lax.fori_loop(0, n_grouped, past_group, init)
    stats, alphas = lax.fori_loop(n_grouped * PAST_BLOCKS_PER_ITERATION, n_past, past_block, carry)

    accumulate(jnp.maximum(n_past - 1, 0), alphas)
    for d in range(n_diag):
        stats, alphas = softmax(stats, diag=d)
        if d + 1 < n_diag:
            stage_scores(n_past + d + 1)
        accumulate(n_past + d, alphas)
    return tuple((stats[c][1], acc_ref[c]) for c in heads)


def _diff_attn_body(lam_ref, subln_ref, q_ref, k_ref, v_ref, o_ref, vt_ref, qt_ref, s_ref, p_ref, acc_ref, *,
                    tq, tk, heads, lam_init):
    qi = pl.program_id(2)

    @pl.when(qi == 0)
    def _():
        _fill_transposed(vt_ref, v_ref, [c * LANES for c in range(heads)], tk)

    is_map0 = (lax.broadcasted_iota(jnp.int32, (LANES, 1), 0) % 64) < 32

    def make_score_fn(c):
        q_t = q_ref[:, _lanes(c)].T
        zero = jnp.zeros_like(q_t)
        qt_ref[c] = jnp.concatenate([jnp.where(is_map0, q_t, zero), jnp.where(is_map0, zero, q_t)], axis=1)
        return lambda start: jnp.dot(k_ref[pl.ds(start, tk), _lanes(c)], qt_ref[c], preferred_element_type=F32)

    diag = _causal_diag_masks(tk, tq, 2 * tq)
    results = _causal_flash_t([make_score_fn(c) for c in range(heads)], vt_ref, s_ref, p_ref, acc_ref,
                              qi * (tq // tk), lambda d, c: diag[d], tq // tk)

    lp = lam_ref[...]
    lam = (jnp.exp(jnp.sum(lp[0:1] * lp[1:2], axis=-1, keepdims=True))
           - jnp.exp(jnp.sum(lp[2:3] * lp[3:4], axis=-1, keepdims=True)) + lam_init)
    for c, (l, acc) in enumerate(results):
        o_t = acc / l
        o = (o_t[:, :tq] - lam * o_t[:, tq:]).T
        o_ref[:, _lanes(c)] = (_rms(o, subln_ref[...]) * (1.0 - lam_init)).astype(o_ref.dtype)


def _diff_attention(qkv, lam_params, subln, *, batch, seq, tq, tk, heads, q_col, k_col, v_col, lam_init):
    nq = seq // tq
    width = heads * LANES
    assert A_HEADS % heads == 0 and q_col % heads == 0 and k_col % heads == 0 and v_col % heads == 0
    assert seq % tq == 0 and tq % tk == 0
    return pl.pallas_call(
        functools.partial(_diff_attn_body, tq=tq, tk=tk, heads=heads, lam_init=lam_init),
        grid=(batch, A_HEADS // heads, nq),
        in_specs=[pl.BlockSpec((4, A_QK_DIM), lambda b, h, i: (0, 0)),
                  pl.BlockSpec((1, LANES), lambda b, h, i: (0, 0)),
                  pl.BlockSpec((tq, width), lambda b, h, i: (b * nq + i, q_col // heads + h)),
                  pl.BlockSpec((seq, width), lambda b, h, i: (b, k_col // heads + h)),
                  pl.BlockSpec((seq, width), lambda b, h, i: (b, v_col // heads + h))],
        out_specs=pl.BlockSpec((tq, width), lambda b, h, i: (b * nq + i, h)),
        out_shape=jax.ShapeDtypeStruct((batch * seq, A_HEADS * LANES), BF16),
        scratch_shapes=[pltpu.VMEM((heads, LANES, seq), BF16), pltpu.VMEM((heads, LANES, 2 * tq), BF16)]
        + _flash_scratch(heads, tk, 2 * tq, LANES),
        compiler_params=_params("parallel", "parallel", "arbitrary"),
        name="diff_attn",
    )(lam_params, subln.reshape(1, LANES), qkv, qkv, qkv)


def _tile_index(ref, r, i, tile):
    per_slab = ref.shape[2]
    if per_slab >= tile:
        runs = per_slab // tile
        return i // runs, r, pl.ds(pl.multiple_of((i % runs) * tile, tile), tile)
    slabs = tile // per_slab
    return pl.ds(i * slabs, slabs), r, slice(None)


def _load_tile(ref, r, i, c, tile):
    slab, res, rows = _tile_index(ref, r, i, tile)
    return ref[slab, res, rows, _lanes(c)].reshape(tile, LANES)


def _store_tile(ref, r, i, c, tile, value):
    slab, res, rows = _tile_index(ref, r, i, tile)
    per_slab = ref.shape[2]
    ref[slab, res, rows, _lanes(c)] = (value if per_slab >= tile
                                       else value.reshape(tile // per_slab, per_slab, LANES))


def _dilated_body(q_ref, k_ref, v_ref, o_ref, lse_ref, *, tile, heads):
    slabs, residues, per_slab, _ = q_ref.shape
    row = lax.broadcasted_iota(jnp.int32, (tile, 1), 0)
    streams = [(r, c) for r in range(residues) for c in range(heads)]

    def attend(r, i, c, k, v, valid):
        s = jnp.where(valid, _dot_nt(_load_tile(q_ref, r, i, c, tile), k), -jnp.inf)
        m = jnp.max(s, axis=-1, keepdims=True)
        p = jnp.exp(s - m)
        l = jnp.sum(p, axis=-1, keepdims=True)
        o = jnp.dot(p.astype(BF16), v, preferred_element_type=F32) / l
        _store_tile(o_ref, r, i, c, tile, o)
        _store_tile(lse_ref, r, i, c, tile, jnp.broadcast_to(m + jnp.log(l), (tile, LANES)))

    col = lax.broadcasted_iota(jnp.int32, (1, tile), 1)
    for r, c in streams:
        attend(r, 0, c, _load_tile(k_ref, r, 0, c, tile), _load_tile(v_ref, r, 0, c, tile), col <= row)

    col2 = lax.broadcasted_iota(jnp.int32, (1, 2 * tile), 1)
    band = jnp.logical_and(col2 >= row, col2 <= row + tile)

    def body(i, _):
        for r, c in streams:
            k = jnp.concatenate([_load_tile(k_ref, r, i - 1, c, tile), _load_tile(k_ref, r, i, c, tile)], axis=0)
            v = jnp.concatenate([_load_tile(v_ref, r, i - 1, c, tile), _load_tile(v_ref, r, i, c, tile)], axis=0)
            attend(r, i, c, k, v, band)
        return 0

    n_tiles = slabs * per_slab // tile
    lax.fori_loop(1, n_tiles, body, 0, unroll=max(1, min(4, n_tiles - 1)))


def _dilated_group(qkv, *, batch, seq, row_tile, group, dil, heads, residues, q_col, k_col, v_col):
    hpg = B_HEADS_PER_GROUP
    tile = B_GROUPS[group][0] // dil
    slabs, per_slab = seq // row_tile, row_tile // dil
    assert tile == LANES and hpg % heads == 0 and (seq // dil) % tile == 0 and dil % residues == 0
    assert per_slab % tile == 0 or tile % per_slab == 0
    width = heads * LANES
    view = qkv.reshape(batch, slabs, dil, per_slab, qkv.shape[1])

    def in_spec(col):
        first = col + group * hpg
        assert first % heads == 0
        return pl.BlockSpec((None, slabs, residues, per_slab, width),
                            lambda b, h, r: (b, 0, r, 0, first // heads + h))

    out_spec = pl.BlockSpec((None, slabs, residues, per_slab, width), lambda b, h, r: (b, 0, r, 0, h))
    out_sds = jax.ShapeDtypeStruct((batch, slabs, dil, per_slab, hpg * LANES), F32)
    o, lse = pl.pallas_call(
        functools.partial(_dilated_body, tile=tile, heads=heads),
        grid=(batch, hpg // heads, dil // residues),
        in_specs=[in_spec(q_col), in_spec(k_col), in_spec(v_col)],
        out_specs=[out_spec, out_spec],
        out_shape=[out_sds, out_sds],
        compiler_params=_params("parallel", "parallel", "parallel"),
        name=f"dilated_g{group}",
    )(view, view, view)
    shape = (batch * slabs, dil, per_slab, hpg * LANES)
    return o.reshape(shape), lse.reshape(shape)


def _merge_body(o0_ref, o1_ref, o2_ref, l0_ref, l1_ref, l2_ref, out_ref, nat_ref):
    def natural(ref, slot, c):
        dil, rows = ref.shape[0], ref.shape[1]
        if dil == 1:
            return ref[0, :, _lanes(c)]
        for r in range(dil):
            nat_ref[slot, pl.ds(r, rows, stride=dil), :] = ref[r, :, _lanes(c)]
        return nat_ref[slot]

    for c in range(out_ref.shape[1] // LANES):
        o0, o1, o2 = natural(o0_ref, None, c), natural(o1_ref, 0, c), natural(o2_ref, 1, c)
        l0, l1, l2 = natural(l0_ref, None, c), natural(l1_ref, 2, c), natural(l2_ref, 3, c)
        m = jnp.maximum(jnp.maximum(l0, l1), l2)
        w0, w1, w2 = jnp.exp(l0 - m), jnp.exp(l1 - m), jnp.exp(l2 - m)
        out_ref[:, _lanes(c)] = ((w0 * o0 + w1 * o1 + w2 * o2) / (w0 + w1 + w2)).astype(out_ref.dtype)


def _merge_groups(outs, lses):
    tiles, _, _, width = outs[0].shape
    row_tile = outs[0].shape[1] * outs[0].shape[2]
    specs = [pl.BlockSpec((None,) + a.shape[1:], lambda i: (i, 0, 0, 0)) for a in (*outs, *lses)]
    return pl.pallas_call(
        _merge_body,
        grid=(tiles,),
        in_specs=specs,
        out_specs=pl.BlockSpec((row_tile, width), lambda i: (i, 0)),
        out_shape=jax.ShapeDtypeStruct((tiles * row_tile, width), BF16),
        scratch_shapes=[pltpu.VMEM((4, row_tile, LANES), F32)],
        compiler_params=_params("parallel"),
        name="dilated_merge",
    )(*outs, *lses)


def _mla_body(q_ref, kv_ref, kr_ref, o_ref, vt_ref, qt_ref, s_ref, p_ref, acc_ref, *, tq, tk, heads):
    qi = pl.program_id(2)

    @pl.when(qi == 0)
    def _():
        _fill_transposed(vt_ref, kv_ref, [(2 * c + 1) * LANES for c in range(heads)], tk)

    def make_score_fn(c):
        qt_ref[c] = q_ref[:, _lanes(c, 2 * LANES)].T

        def score(start):
            k = jnp.concatenate([kv_ref[pl.ds(start, tk), _lanes(2 * c)], kr_ref[pl.ds(start, tk), :]], axis=1)
            return jnp.dot(k, qt_ref[c], preferred_element_type=F32)

        return score

    diag = _causal_diag_masks(tk, tq, tq)
    results = _causal_flash_t([make_score_fn(c) for c in range(heads)], vt_ref, s_ref, p_ref, acc_ref,
                              qi * (tq // tk), lambda d, c: diag[d], tq // tk)
    for c, (l, acc) in enumerate(results):
        o_ref[:, _lanes(c)] = (acc / l).T.astype(o_ref.dtype)


def _mla_attention(q, kv, k_rope_src, *, batch, seq, tq, tk, heads, k_rope_col):
    nq = seq // tq
    assert C_HEADS % heads == 0 and seq % tq == 0 and tq % tk == 0
    return pl.pallas_call(
        functools.partial(_mla_body, tq=tq, tk=tk, heads=heads),
        grid=(batch, C_HEADS // heads, nq),
        in_specs=[pl.BlockSpec((tq, heads * 2 * LANES), lambda b, h, i: (b * nq + i, h)),
                  pl.BlockSpec((seq, heads * 2 * LANES), lambda b, h, i: (b, h)),
                  pl.BlockSpec((seq, LANES), lambda b, h, i: (b, k_rope_col))],
        out_specs=pl.BlockSpec((tq, heads * LANES), lambda b, h, i: (b * nq + i, h)),
        out_shape=jax.ShapeDtypeStruct((batch * seq, C_HEADS * LANES), BF16),
        scratch_shapes=[pltpu.VMEM((heads, LANES, seq), BF16), pltpu.VMEM((heads, 2 * LANES, tq), BF16)]
        + _flash_scratch(heads, tk, tq, LANES),
        compiler_params=_params("parallel", "parallel", "arbitrary"),
        name="mla_attn",
    )(q, kv, k_rope_src)


def _moba_body(q_ref, k_ref, v_ref, o_ref, kmean_ref, vt_ref, sel_ref, qt_ref, s_ref, p_ref, acc_ref, *,
               n_blk, tq, heads):
    qi = pl.program_id(2)
    blk = MOBA_BLOCK
    n_own = tq // blk

    @pl.when(qi == 0)
    def _():
        for c in range(heads):
            k_all = k_ref[:, _lanes(c)].astype(F32).reshape(n_blk, blk, LANES)
            kmean_ref[c] = jnp.mean(k_all, axis=1)
        _fill_transposed(vt_ref, v_ref, [c * LANES for c in range(heads)], blk)

    blk_id = lax.broadcasted_iota(jnp.int32, (n_blk, 1), 0)
    query_col = lax.broadcasted_iota(jnp.int32, (1, tq), 1)
    own_in_tile = query_col // blk
    past = blk_id < qi * n_own + own_in_tile
    causal = lax.broadcasted_iota(jnp.int32, (blk, 1), 0) <= query_col % blk

    def make_score_fn(c):
        q_t = q_ref[:, _lanes(c)].T
        km = kmean_ref[c]
        km_hi = km.astype(BF16)
        rem = km - km_hi.astype(F32)
        km_mid = rem.astype(BF16)
        km_lo = (rem - km_mid.astype(F32)).astype(BF16)
        gate = (jnp.dot(km_hi, q_t, preferred_element_type=F32) + jnp.dot(km_mid, q_t, preferred_element_type=F32)
                + jnp.dot(km_lo, q_t, preferred_element_type=F32))
        gate = jnp.where(past, gate, -jnp.inf)
        selected = jnp.zeros(gate.shape, F32)
        for _ in range(min(MOBA_TOPK, n_blk - 1)):
            best = jnp.max(gate, axis=0, keepdims=True)
            first = jnp.min(jnp.where(gate == best, blk_id, n_blk), axis=0, keepdims=True)
            pick = blk_id == first
            selected = jnp.where(jnp.logical_and(pick, past), 1.0, selected)
            gate = jnp.where(pick, -jnp.inf, gate)
        sel_ref[c] = selected
        qt_ref[c] = q_t
        return lambda start: jnp.dot(k_ref[pl.ds(start, blk), _lanes(c)], qt_ref[c], preferred_element_type=F32)

    score_fns = [make_score_fn(c) for c in range(heads)]
    n_past = qi * n_own
    chosen_fns = [lambda n, c=c: jnp.logical_or(sel_ref[c, pl.ds(n, 1), :] > 0.0, n >= n_past)
                  for c in range(heads)]

    def own_block_mask(d, c):
        chosen = sel_ref[c, pl.ds(n_past + d, 1), :] > 0.0
        return jnp.logical_or(jnp.logical_and(own_in_tile == d, causal),
                              jnp.logical_and(own_in_tile > d, chosen))

    results = _causal_flash_t(score_fns, vt_ref, s_ref, p_ref, acc_ref, n_past, own_block_mask, n_own,
                              past_mask_fns=chosen_fns)
    for c, (l, acc) in enumerate(results):
        o_ref[:, _lanes(c)] = (acc / l).T.astype(o_ref.dtype)


def _moba_attention(qkv, *, batch, seq, tq, heads, q_col, k_col, v_col):
    blk = MOBA_BLOCK
    n_blk = seq // blk
    nq = seq // tq
    width = heads * LANES
    assert seq % tq == 0 and tq % blk == 0 and D_HEADS % heads == 0
    assert q_col % heads == 0 and k_col % heads == 0 and v_col % heads == 0
    return pl.pallas_call(
        functools.partial(_moba_body, n_blk=n_blk, tq=tq, heads=heads),
        grid=(batch, D_HEADS // heads, nq),
        in_specs=[pl.BlockSpec((tq, width), lambda b, h, i: (b * nq + i, q_col // heads + h)),
                  pl.BlockSpec((seq, width), lambda b, h, i: (b, k_col // heads + h)),
                  pl.BlockSpec((seq, width), lambda b, h, i: (b, v_col // heads + h))],
        out_specs=pl.BlockSpec((tq, width), lambda b, h, i: (b * nq + i, h)),
        out_shape=jax.ShapeDtypeStruct((batch * seq, D_HEADS * LANES), BF16),
        scratch_shapes=[pltpu.VMEM((heads, n_blk, LANES), F32),
                        pltpu.VMEM((heads, LANES, seq), BF16),
                        pltpu.VMEM((heads, n_blk, tq), F32),
                        pltpu.VMEM((heads, LANES, tq), BF16)] + _flash_scratch(heads, blk, tq, LANES),
        compiler_params=_params("parallel", "parallel", "arbitrary"),
        name="moba_attn",
    )(qkv, qkv, qkv)


def _rope_tables(seq):
    pos = jnp.arange(seq, dtype=F32)[:, None]

    def angles(dim):
        inv = 1.0 / (ROPE_THETA ** (jnp.arange(0, dim, 2, dtype=F32) / dim))
        return pos * inv[None, :]

    a128 = angles(HEAD_DIM)
    a64 = angles(A_QK_DIM)
    c128 = jnp.concatenate([jnp.cos(a128)] * 2, axis=1)
    s128 = jnp.concatenate([-jnp.sin(a128), jnp.sin(a128)], axis=1)
    c64 = jnp.concatenate([jnp.cos(a64)] * 4, axis=1)
    s64 = jnp.concatenate([-jnp.sin(a64)] * 2 + [jnp.sin(a64)] * 2, axis=1)
    return c128, s128, c64, s64


def _interleave_diff_heads(w):
    d = w.shape[0]
    return w.reshape(d, A_HEADS, 2, 2, 32).transpose(0, 1, 3, 2, 4).reshape(d, A_HEADS * LANES)


def _spread_rope64(w):
    z = jnp.zeros(w.shape[:-1] + (32,), w.dtype)
    return jnp.concatenate([w[..., :32], z, w[..., 32:], z], axis=-1)


def _ab_mixer(x, gain, w_in, lam_params, subln, w_out, layer_idx, rope, *, batch, seq):
    aw = A_HEADS * LANES
    w = jnp.concatenate([_interleave_diff_heads(w_in[:, :aw]), _interleave_diff_heads(w_in[:, aw:2 * aw]),
                         w_in[:, 2 * aw:]], axis=1).astype(BF16)
    a_scale = A_QK_DIM ** -0.5 * LOG2_E
    b_scale = HEAD_DIM ** -0.5
    dils = [dil for _, dil in B_GROUPS for _ in range(B_HEADS_PER_GROUP)]
    modes = ([(ROPE64, a_scale, 1)] * A_HEADS + [(ROPE64, 1.0, 1)] * A_HEADS + [(PLAIN, 1.0, 1)] * A_HEADS
             + [(ROPE128, b_scale, d) for d in dils] + [(ROPE128, 1.0, d) for d in dils]
             + [(PLAIN, 1.0, d) for d in dils])
    row_tile = 1024
    qkv = _norm_proj(x, gain, w, rope, modes, seq=seq, tm=row_tile, tn=768, out_dtype=BF16)
    lam_init = 0.8 - 0.6 * math.exp(-0.3 * layer_idx)
    oa = _diff_attention(qkv, lam_params, subln, batch=batch, seq=seq, tq=512, tk=256, heads=4,
                         q_col=0, k_col=A_HEADS, v_col=2 * A_HEADS, lam_init=lam_init)
    b0 = 3 * A_HEADS
    outs, lses = [], []
    for g, (_, dil) in enumerate(B_GROUPS):
        o, lse = _dilated_group(qkv, batch=batch, seq=seq, row_tile=row_tile, group=g, dil=dil,
                                heads=2 if dil == 1 else 4, residues=4 if dil == 16 else 1,
                                q_col=b0, k_col=b0 + B_HEADS, v_col=b0 + 2 * B_HEADS)
        outs.append(o)
        lses.append(lse)
    ob = _merge_groups(outs, lses)
    return _out_proj(oa, ob, w_out.astype(BF16), x, tm=1024, tn=1024)


def _cd_mixer(x, gain, w_in, q_norm, w_uq, kv_norm, w_ukv, w_out, rope, *, batch, seq):
    lat = C_Q_LORA + C_KV_LORA
    d_scale = HEAD_DIM ** -0.5 * LOG2_E
    w_main = jnp.concatenate([w_in[:, lat + C_ROPE:], _spread_rope64(w_in[:, lat:lat + C_ROPE])],
                             axis=1).astype(BF16)
    modes = ([(ROPE128, d_scale, 1)] * D_HEADS + [(ROPE128, 1.0, 1)] * D_HEADS + [(PLAIN, 1.0, 1)] * D_HEADS
             + [(ROPE64, 1.0, 1)])
    main = _norm_proj(x, gain, w_main, rope, modes, seq=seq, tm=1024, tn=5 * LANES, out_dtype=BF16)

    c_scale = (C_NOPE + C_ROPE) ** -0.5 * LOG2_E
    w_q = w_uq.reshape(C_Q_LORA, C_HEADS, C_NOPE + C_ROPE)
    w_q = jnp.concatenate([w_q[..., :C_NOPE], _spread_rope64(w_q[..., C_NOPE:])], axis=-1)
    w_q = w_q.reshape(C_Q_LORA, C_HEADS * 2 * LANES).astype(BF16)
    qc, kv = _mla_proj(x, gain, w_in[:, :lat].astype(BF16), q_norm, w_q, kv_norm, w_ukv.astype(BF16), rope,
                       seq=seq, tm=512, q_scale=c_scale)
    oc = _mla_attention(qc, kv, main, batch=batch, seq=seq, tq=1024, tk=256, heads=4, k_rope_col=3 * D_HEADS)
    od = _moba_attention(main, batch=batch, seq=seq, tq=512, heads=4,
                         q_col=0, k_col=D_HEADS, v_col=2 * D_HEADS)
    return _out_proj(oc, od, w_out.astype(BF16), x, tm=1024, tn=1024)


def kernel(x, p, ffn_norm, ffn_w_gu, ffn_w_down, mix_norm, ab_w_in, ab_lambda, ab_subln, ab_w_out,
           cd_w_in, cd_q_norm, cd_w_uq, cd_kv_norm, cd_w_ukv, cd_w_out, ple_norm, ple_w_gate,
           ple_w_proj, final_norm):
    batch, seq, d = x.shape
    depth = p.shape[0]
    rope = _rope_tables(seq)
    w_gu, w_down = ffn_w_gu.astype(BF16), ffn_w_down.astype(BF16)
    x = x.reshape(batch * seq, d)
    for i in range(depth):
        j = i // 2
        x = _ffn(x, ffn_norm[i, 0], w_gu, w_down, i, 0, tm=1024, tf=512)
        if i % 2 == 0:
            x = _ab_mixer(x, mix_norm[i], ab_w_in[j], ab_lambda[j], ab_subln[j], ab_w_out[j], i, rope,
                          batch=batch, seq=seq)
        else:
            x = _cd_mixer(x, mix_norm[i], cd_w_in[j], cd_q_norm[j], cd_w_uq[j], cd_kv_norm[j], cd_w_ukv[j],
                          cd_w_out[j], rope, batch=batch, seq=seq)
        x = _ffn(x, ffn_norm[i, 1], w_gu, w_down, i, 1, tm=1024, tf=512)
        x = _ple(x, p.reshape(depth * batch * seq, -1), i, ple_norm[i], ple_w_gate[i].astype(BF16),
                 ple_w_proj[i].astype(BF16), final_norm, tm=512, final_norm=(i == depth - 1))
    return x.reshape(batch, seq, d)
```

```python
import functools
import math

import jax
import jax.numpy as jnp
from jax import lax
from jax.experimental import pallas as pl
from jax.experimental.pallas import tpu as pltpu

F32 = jnp.float32
BF16 = jnp.bfloat16

D_MODEL = 2048
D_FF = 5632
PLE_DIM = 256
HEAD_DIM = 128
ROPE_THETA = 10000.0
NORM_EPS = 1e-6

A_HEADS = 8
A_QK_DIM = 64
B_GROUPS = ((128, 1), (512, 4), (2048, 16))
B_HEADS_PER_GROUP = 4
B_HEADS = B_HEADS_PER_GROUP * len(B_GROUPS)
C_HEADS = 8
C_Q_LORA = 512
C_KV_LORA = 256
C_NOPE = 128
C_ROPE = 64
D_HEADS = 8
MOBA_BLOCK = 256
MOBA_TOPK = 3

LOG2_E = 1.0 / math.log(2.0)
PAST_BLOCKS_PER_ITERATION = 4
SUM_ROWS = 16
LANES = 128
VMEM_LIMIT = 56 * 1024 * 1024

PLAIN, ROPE128, ROPE64 = 0, 1, 2


def _rms(x, gain):
    return x * lax.rsqrt(jnp.mean(x * x, axis=-1, keepdims=True) + NORM_EPS) * gain


def _params(*semantics):
    return pltpu.CompilerParams(dimension_semantics=semantics, vmem_limit_bytes=VMEM_LIMIT)


def _norm_proj_body(x_ref, g_ref, w_ref, c128_ref, s128_ref, c64_ref, s64_ref, o_ref, xn_ref, stage_ref, *,
                    tile_patterns):
    j = pl.program_id(1)
    tm = o_ref.shape[0]

    @pl.when(j == 0)
    def _():
        xn_ref[...] = _rms(x_ref[...], g_ref[...]).astype(BF16)

    def tile(pattern):
        acc = jnp.dot(xn_ref[...], w_ref[...], preferred_element_type=F32)
        for c, (kind, scale, dil) in enumerate(pattern):
            a = acc[:, _lanes(c)]
            if scale != 1.0:
                a = a * scale
            if kind == ROPE128:
                a = a * c128_ref[...] + pltpu.roll(a, 64, 1) * s128_ref[...]
            elif kind == ROPE64:
                a = a * c64_ref[...] + pltpu.roll(a, 64, 1) * s64_ref[...]
            if dil == 1:
                o_ref[:, _lanes(c)] = a.astype(o_ref.dtype)
            else:
                rows = tm // dil
                stage_ref[c] = a
                for r in range(dil):
                    o_ref[r * rows:(r + 1) * rows, _lanes(c)] = (
                        stage_ref[c, pl.ds(r, rows, stride=dil), :].astype(o_ref.dtype))

    distinct = sorted(set(tile_patterns), key=tile_patterns.index)
    if len(distinct) == 1:
        tile(distinct[0])
    else:
        for pattern in distinct:
            tiles = [t for t, p in enumerate(tile_patterns) if p == pattern]
            cond = functools.reduce(jnp.logical_or, [j == t for t in tiles])
            pl.when(cond)(functools.partial(tile, pattern))


def _norm_proj(x, gain, w, rope, chunk_modes, *, seq, tm, tn, out_dtype):
    t, k = x.shape
    n = w.shape[1]
    per_tile = tn // LANES
    assert t % tm == 0 and n % tn == 0 and seq % tm == 0 and len(chunk_modes) * LANES == n
    tile_patterns = tuple(tuple(chunk_modes[a * per_tile:(a + 1) * per_tile]) for a in range(n // tn))
    seq_tiles = seq // tm
    tab = pl.BlockSpec((tm, LANES), lambda i, j: (i % seq_tiles, 0))
    return pl.pallas_call(
        functools.partial(_norm_proj_body, tile_patterns=tile_patterns),
        grid=(t // tm, n // tn),
        in_specs=[pl.BlockSpec((tm, k), lambda i, j: (i, 0)),
                  pl.BlockSpec((1, k), lambda i, j: (0, 0)),
                  pl.BlockSpec((k, tn), lambda i, j: (0, j)),
                  tab, tab, tab, tab],
        out_specs=pl.BlockSpec((tm, tn), lambda i, j: (i, j)),
        out_shape=jax.ShapeDtypeStruct((t, n), out_dtype),
        scratch_shapes=[pltpu.VMEM((tm, k), BF16), pltpu.VMEM((per_tile, tm, LANES), F32)],
        compiler_params=_params("parallel", "arbitrary"),
        name="norm_proj",
    )(x, gain.reshape(1, k), w, *rope)


def _out_proj_body(a1_ref, a2_ref, w1_ref, w2_ref, res_ref, o_ref):
    o_ref[...] = (res_ref[...] + jnp.dot(a1_ref[...], w1_ref[...], preferred_element_type=F32)
                  + jnp.dot(a2_ref[...], w2_ref[...], preferred_element_type=F32))


def _out_proj(a1, a2, w, res, *, tm, tn):
    t, k1 = a1.shape
    k2 = a2.shape[1]
    n = w.shape[1]
    assert t % tm == 0 and n % tn == 0 and w.shape[0] == k1 + k2 and k1 % k2 == 0
    return pl.pallas_call(
        _out_proj_body,
        grid=(t // tm, n // tn),
        in_specs=[pl.BlockSpec((tm, k1), lambda i, j: (i, 0)),
                  pl.BlockSpec((tm, k2), lambda i, j: (i, 0)),
                  pl.BlockSpec((k1, tn), lambda i, j: (0, j)),
                  pl.BlockSpec((k2, tn), lambda i, j: (k1 // k2, j)),
                  pl.BlockSpec((tm, tn), lambda i, j: (i, j))],
        out_specs=pl.BlockSpec((tm, tn), lambda i, j: (i, j)),
        out_shape=jax.ShapeDtypeStruct((t, n), F32),
        compiler_params=_params("parallel", "parallel"),
        name="out_proj",
    )(a1, a2, w, w, res)


def _mla_proj_body(x_ref, g_ref, wl_ref, qn_ref, wq_ref, kvn_ref, wkv_ref, c64_ref, s64_ref, q_ref, kv_ref, *,
                   q_lora, q_scale):
    half = x_ref.shape[0] // 2
    for rows in (slice(0, half), slice(half, 2 * half)):
        xn = _rms(x_ref[rows, :], g_ref[...]).astype(BF16)
        latent = jnp.dot(xn, wl_ref[...], preferred_element_type=F32)
        c_q = _rms(latent[:, :q_lora], qn_ref[...]).astype(BF16)
        c_kv = _rms(latent[:, q_lora:], kvn_ref[...]).astype(BF16)
        kv_ref[rows, :] = jnp.dot(c_kv, wkv_ref[...], preferred_element_type=F32).astype(kv_ref.dtype)
        q = jnp.dot(c_q, wq_ref[...], preferred_element_type=F32) * q_scale
        for c in range(q_ref.shape[1] // LANES):
            a = q[:, _lanes(c)]
            if c % 2 == 1:
                a = a * c64_ref[rows, :] + pltpu.roll(a, 64, 1) * s64_ref[rows, :]
            q_ref[rows, _lanes(c)] = a.astype(q_ref.dtype)


def _mla_proj(x, gain, w_latent, q_norm, w_q, kv_norm, w_kv, rope, *, seq, tm, q_scale):
    t, d = x.shape
    q_lora, kv_lora = w_q.shape[0], w_kv.shape[0]
    assert t % tm == 0 and seq % tm == 0 and w_latent.shape[1] == q_lora + kv_lora
    seq_tiles = seq // tm
    const = lambda i: (0, 0)
    tab = pl.BlockSpec((tm, LANES), lambda i: (i % seq_tiles, 0))
    full = lambda a: pl.BlockSpec(a.shape, const)
    out_spec = lambda a: pl.BlockSpec((tm, a.shape[1]), lambda i: (i, 0))
    return pl.pallas_call(
        functools.partial(_mla_proj_body, q_lora=q_lora, q_scale=q_scale),
        grid=(t // tm,),
        in_specs=[pl.BlockSpec((tm, d), lambda i: (i, 0)), pl.BlockSpec((1, d), const), full(w_latent),
                  pl.BlockSpec((1, q_lora), const), full(w_q), pl.BlockSpec((1, kv_lora), const), full(w_kv),
                  tab, tab],
        out_specs=[out_spec(w_q), out_spec(w_kv)],
        out_shape=[jax.ShapeDtypeStruct((t, w_q.shape[1]), BF16), jax.ShapeDtypeStruct((t, w_kv.shape[1]), BF16)],
        compiler_params=_params("parallel"),
        name="mla_proj",
    )(x, gain.reshape(1, d), w_latent, q_norm.reshape(1, q_lora), w_q, kv_norm.reshape(1, kv_lora), w_kv,
      rope[2], rope[3])


def _ffn_body(x_ref, g_ref, wg_ref, wu_ref, wd_ref, o_ref, xn_ref):
    j = pl.program_id(1)

    @pl.when(j == 0)
    def _():
        xn_ref[...] = _rms(x_ref[...], g_ref[...]).astype(BF16)
        o_ref[...] = jnp.zeros_like(o_ref)

    xn = xn_ref[...]
    g = jnp.dot(xn, wg_ref[...], preferred_element_type=F32)
    u = jnp.dot(xn, wu_ref[...], preferred_element_type=F32)
    h = (g * jax.nn.sigmoid(g) * u).astype(BF16)
    o_ref[...] += jnp.dot(h, wd_ref[...], preferred_element_type=F32)

    @pl.when(j == pl.num_programs(1) - 1)
    def _():
        o_ref[...] = x_ref[...] + 0.5 * o_ref[...]


def _ffn(x, gain, w_gu, w_down, layer, half, *, tm, tf):
    t, d = x.shape
    d_ff = w_down.shape[2]
    assert t % tm == 0 and d_ff % tf == 0
    nf = d_ff // tf
    return pl.pallas_call(
        _ffn_body,
        grid=(t // tm, nf),
        in_specs=[pl.BlockSpec((tm, d), lambda i, j: (i, 0)),
                  pl.BlockSpec((1, d), lambda i, j: (0, 0)),
                  pl.BlockSpec((None, None, d, tf), lambda i, j: (layer, half, 0, j)),
                  pl.BlockSpec((None, None, d, tf), lambda i, j: (layer, half, 0, nf + j)),
                  pl.BlockSpec((None, None, tf, d), lambda i, j: (layer, half, j, 0))],
        out_specs=pl.BlockSpec((tm, d), lambda i, j: (i, 0)),
        out_shape=jax.ShapeDtypeStruct((t, d), F32),
        scratch_shapes=[pltpu.VMEM((tm, d), BF16)],
        compiler_params=_params("parallel", "arbitrary"),
        name="ffn",
    )(x, gain.reshape(1, d), w_gu, w_gu, w_down)


def _ple_body(x_ref, p_ref, g_ref, wg_ref, wp_ref, fg_ref, o_ref, *, final_norm):
    half = x_ref.shape[0] // 2
    for rows in (slice(0, half), slice(half, 2 * half)):
        x = x_ref[rows, :]
        xn = _rms(x, g_ref[...]).astype(BF16)
        gate = jax.nn.sigmoid(jnp.dot(xn, wg_ref[...], preferred_element_type=F32))
        proj = jnp.dot(p_ref[rows, :].astype(BF16), wp_ref[...], preferred_element_type=F32)
        y = x + gate * proj
        if final_norm:
            y = _rms(y, fg_ref[...])
        o_ref[rows, :] = y


def _ple(x, p, layer, gain, w_gate, w_proj, final_gain, *, tm, final_norm):
    t, d = x.shape
    pd = p.shape[1]
    assert t % tm == 0
    first_tile = layer * (t // tm)
    const = lambda i: (0, 0)
    return pl.pallas_call(
        functools.partial(_ple_body, final_norm=final_norm),
        grid=(t // tm,),
        in_specs=[pl.BlockSpec((tm, d), lambda i: (i, 0)),
                  pl.BlockSpec((tm, pd), lambda i: (first_tile + i, 0)),
                  pl.BlockSpec((1, d), const),
                  pl.BlockSpec((d, d), const),
                  pl.BlockSpec((pd, d), const),
                  pl.BlockSpec((1, d), const)],
        out_specs=pl.BlockSpec((tm, d), lambda i: (i, 0)),
        out_shape=jax.ShapeDtypeStruct((t, d), F32),
        compiler_params=_params("parallel"),
        name="ple",
    )(x, p, gain.reshape(1, d), w_gate, w_proj, final_gain.reshape(1, d))


def _dot_nt(a, b):
    return lax.dot_general(a, b, (((1,), (1,)), ((), ())), preferred_element_type=F32)


def _lanes(c, width=LANES):
    return slice(c * width, (c + 1) * width)


def _fill_transposed(vt_ref, v_ref, lane_offsets, chunk):
    dv = vt_ref.shape[1] - SUM_ROWS
    ones_row = jnp.where(lax.broadcasted_iota(jnp.int32, (SUM_ROWS, chunk), 0) == 0, 1.0, 0.0).astype(BF16)

    def body(j, _):
        start = pl.multiple_of(j * chunk, chunk)
        for c, off in enumerate(lane_offsets):
            vt_ref[c, :dv, pl.ds(start, chunk)] = v_ref[pl.ds(start, chunk), off:off + dv].T
            vt_ref[c, dv:, pl.ds(start, chunk)] = ones_row
        return 0

    lax.fori_loop(0, v_ref.shape[0] // chunk, body, 0)


def _flash_scratch(heads, tk, cols, dv, seq):
    return [pltpu.VMEM((heads, dv + SUM_ROWS, seq), BF16),
            pltpu.VMEM((heads, tk, cols), F32), pltpu.VMEM((heads, tk, cols), BF16),
            pltpu.VMEM((heads, dv + SUM_ROWS, cols), F32)]


def _causal_diag_masks(tk, tq, cols):
    key_row = lax.broadcasted_iota(jnp.int32, (tk, 1), 0)
    query_col = lax.broadcasted_iota(jnp.int32, (1, cols), 1) % tq
    return [key_row + d * tk <= query_col for d in range(tq // tk)]


def _causal_flash_t(score_fns, vt_ref, s_ref, p_ref, acc_ref, n_past, diag_mask_fn, n_diag, past_mask_fns=None):
    heads = range(len(score_fns))
    _, tk, cols = s_ref.shape

    def stage_scores(block):
        start = pl.multiple_of(block * tk, tk)
        for c in heads:
            s = score_fns[c](start)
            if past_mask_fns is not None:
                s = jnp.where(past_mask_fns[c](block), s, -jnp.inf)
            s_ref[c] = s

    def accumulate(block, alphas):
        start = pl.multiple_of(block * tk, tk)
        for c in heads:
            pv = jnp.dot(vt_ref[c, :, pl.ds(start, tk)], p_ref[c], preferred_element_type=F32)
            acc_ref[c] = alphas[c] * acc_ref[c] + pv

    def softmax(stats, diag=None):
        new_stats, alphas = [], []
        for c in heads:
            m = stats[c]
            if diag is None:
                read = lambda: s_ref[c]
            else:
                visible = diag_mask_fn(diag, c)
                read = lambda: jnp.where(visible, s_ref[c], -jnp.inf)
            m_new = jnp.maximum(m, jnp.max(read(), axis=0, keepdims=True))
            alpha = jnp.exp2(m - m_new)
            p_ref[c] = jnp.exp2(read() - m_new).astype(BF16)
            new_stats.append(m_new)
            alphas.append(alpha)
        return tuple(new_stats), tuple(alphas)

    def past_block(j, carry):
        stats, alphas = carry
        accumulate(jnp.maximum(j - 1, 0), alphas)
        stats, alphas = softmax(stats)
        stage_scores(j + 1)
        return stats, alphas

    for c in heads:
        p_ref[c] = jnp.zeros(p_ref.shape[1:], BF16)
        acc_ref[c] = jnp.zeros(acc_ref.shape[1:], F32)
    stage_scores(0)
    init = (tuple(jnp.full((1, cols), jnp.finfo(F32).min, F32) for _ in heads),
            tuple(jnp.ones((1, cols), F32) for _ in heads))
    def past_group(t, carry):
        for u in range(PAST_BLOCKS_PER_ITERATION):
            carry = past_block(t * PAST_BLOCKS_PER_ITERATION + u, carry)
        return carry

    n_grouped = n_past // PAST_BLOCKS_PER_ITERATION
    carry = lax.fori_loop(0, n_grouped, past_group, init)
    stats, alphas = lax.fori_loop(n_grouped * PAST_BLOCKS_PER_ITERATION, n_past, past_block, carry)

    accumulate(jnp.maximum(n_past - 1, 0), alphas)
    for d in range(n_diag):
        stats, alphas = softmax(stats, diag=d)
        if d + 1 < n_diag:
            stage_scores(n_past + d + 1)
        accumulate(n_past + d, alphas)
    dv = acc_ref.shape[1] - SUM_ROWS
    return tuple((acc_ref[c, dv:dv + 1, :], acc_ref[c, :dv, :]) for c in heads)


def _diff_attn_body(lam_ref, subln_ref, q_ref, k_ref, v_ref, o_ref, qt_ref, vt_ref, s_ref, p_ref, acc_ref, *,
                    tq, tk, heads, lam_init):
    qi = pl.program_id(2)

    @pl.when(qi == 0)
    def _():
        _fill_transposed(vt_ref, v_ref, [c * LANES for c in range(heads)], tk)

    is_map0 = (lax.broadcasted_iota(jnp.int32, (LANES, 1), 0) % 64) < 32

    def make_score_fn(c):
        q_t = q_ref[:, _lanes(c)].T
        zero = jnp.zeros_like(q_t)
        qt_ref[c] = jnp.concatenate([jnp.where(is_map0, q_t, zero), jnp.where(is_map0, zero, q_t)], axis=1)
        return lambda start: jnp.dot(k_ref[pl.ds(start, tk), _lanes(c)], qt_ref[c], preferred_element_type=F32)

    diag = _causal_diag_masks(tk, tq, 2 * tq)
    results = _causal_flash_t([make_score_fn(c) for c in range(heads)], vt_ref, s_ref, p_ref, acc_ref,
                              qi * (tq // tk), lambda d, c: diag[d], tq // tk)

    lp = lam_ref[...]
    lam = (jnp.exp(jnp.sum(lp[0:1] * lp[1:2], axis=-1, keepdims=True))
           - jnp.exp(jnp.sum(lp[2:3] * lp[3:4], axis=-1, keepdims=True)) + lam_init)
    for c, (l, acc) in enumerate(results):
        o_t = acc / l
        o = (o_t[:, :tq] - lam * o_t[:, tq:]).T
        o_ref[:, _lanes(c)] = (_rms(o, subln_ref[...]) * (1.0 - lam_init)).astype(o_ref.dtype)


def _diff_attention(qkv, lam_params, subln, *, batch, seq, tq, tk, heads, q_col, k_col, v_col, lam_init):
    nq = seq // tq
    width = heads * LANES
    assert A_HEADS % heads == 0 and q_col % heads == 0 and k_col % heads == 0 and v_col % heads == 0
    assert seq % tq == 0 and tq % tk == 0
    return pl.pallas_call(
        functools.partial(_diff_attn_body, tq=tq, tk=tk, heads=heads, lam_init=lam_init),
        grid=(batch, A_HEADS // heads, nq),
        in_specs=[pl.BlockSpec((4, A_QK_DIM), lambda b, h, i: (0, 0)),
                  pl.BlockSpec((1, LANES), lambda b, h, i: (0, 0)),
                  pl.BlockSpec((tq, width), lambda b, h, i: (b * nq + i, q_col // heads + h)),
                  pl.BlockSpec((seq, width), lambda b, h, i: (b, k_col // heads + h)),
                  pl.BlockSpec((seq, width), lambda b, h, i: (b, v_col // heads + h))],
        out_specs=pl.BlockSpec((tq, width), lambda b, h, i: (b * nq + i, h)),
        out_shape=jax.ShapeDtypeStruct((batch * seq, A_HEADS * LANES), BF16),
        scratch_shapes=[pltpu.VMEM((heads, LANES, 2 * tq), BF16)] + _flash_scratch(heads, tk, 2 * tq, LANES, seq),
        compiler_params=_params("parallel", "parallel", "arbitrary"),
        name="diff_attn",
    )(lam_params, subln.reshape(1, LANES), qkv, qkv, qkv)


def _tile_index(ref, r, i, tile):
    per_slab = ref.shape[2]
    if per_slab >= tile:
        runs = per_slab // tile
        return i // runs, r, pl.ds(pl.multiple_of((i % runs) * tile, tile), tile)
    slabs = tile // per_slab
    return pl.ds(i * slabs, slabs), r, slice(None)


def _load_tile(ref, r, i, c, tile):
    slab, res, rows = _tile_index(ref, r, i, tile)
    return ref[slab, res, rows, _lanes(c)].reshape(tile, LANES)


def _store_tile(ref, r, i, c, tile, value):
    slab, res, rows = _tile_index(ref, r, i, tile)
    per_slab = ref.shape[2]
    ref[slab, res, rows, _lanes(c)] = (value if per_slab >= tile
                                       else value.reshape(tile // per_slab, per_slab, LANES))


def _dilated_body(q_ref, k_ref, v_ref, o_ref, lse_ref, *, tile, heads):
    slabs, residues, per_slab, _ = q_ref.shape
    row = lax.broadcasted_iota(jnp.int32, (tile, 1), 0)
    streams = [(r, c) for r in range(residues) for c in range(heads)]

    def attend(r, i, c, k, v, valid):
        s = jnp.where(valid, _dot_nt(_load_tile(q_ref, r, i, c, tile), k), -jnp.inf)
        m = jnp.max(s, axis=-1, keepdims=True)
        p = jnp.exp(s - m)
        l = jnp.sum(p, axis=-1, keepdims=True)
        o = jnp.dot(p.astype(BF16), v, preferred_element_type=F32) / l
        _store_tile(o_ref, r, i, c, tile, o)
        _store_tile(lse_ref, r, i, c, tile, jnp.broadcast_to(m + jnp.log(l), (tile, LANES)))

    col = lax.broadcasted_iota(jnp.int32, (1, tile), 1)
    for r, c in streams:
        attend(r, 0, c, _load_tile(k_ref, r, 0, c, tile), _load_tile(v_ref, r, 0, c, tile), col <= row)

    col2 = lax.broadcasted_iota(jnp.int32, (1, 2 * tile), 1)
    band = jnp.logical_and(col2 >= row, col2 <= row + tile)

    def body(i, _):
        for r, c in streams:
            k = jnp.concatenate([_load_tile(k_ref, r, i - 1, c, tile), _load_tile(k_ref, r, i, c, tile)], axis=0)
            v = jnp.concatenate([_load_tile(v_ref, r, i - 1, c, tile), _load_tile(v_ref, r, i, c, tile)], axis=0)
            attend(r, i, c, k, v, band)
        return 0

    n_tiles = slabs * per_slab // tile
    lax.fori_loop(1, n_tiles, body, 0, unroll=max(1, min(4, n_tiles - 1)))


def _dilated_group(qkv, *, batch, seq, row_tile, group, dil, heads, residues, q_col, k_col, v_col):
    hpg = B_HEADS_PER_GROUP
    tile = B_GROUPS[group][0] // dil
    slabs, per_slab = seq // row_tile, row_tile // dil
    assert tile == LANES and hpg % heads == 0 and (seq // dil) % tile == 0 and dil % residues == 0
    assert per_slab % tile == 0 or tile % per_slab == 0
    width = heads * LANES
    view = qkv.reshape(batch, slabs, dil, per_slab, qkv.shape[1])

    def in_spec(col):
        first = col + group * hpg
        assert first % heads == 0
        return pl.BlockSpec((None, slabs, residues, per_slab, width),
                            lambda b, h, r: (b, 0, r, 0, first // heads + h))

    out_spec = pl.BlockSpec((None, slabs, residues, per_slab, width), lambda b, h, r: (b, 0, r, 0, h))
    out_sds = jax.ShapeDtypeStruct((batch, slabs, dil, per_slab, hpg * LANES), F32)
    o, lse = pl.pallas_call(
        functools.partial(_dilated_body, tile=tile, heads=heads),
        grid=(batch, hpg // heads, dil // residues),
        in_specs=[in_spec(q_col), in_spec(k_col), in_spec(v_col)],
        out_specs=[out_spec, out_spec],
        out_shape=[out_sds, out_sds],
        compiler_params=_params("parallel", "parallel", "parallel"),
        name=f"dilated_g{group}",
    )(view, view, view)
    shape = (batch * slabs, dil, per_slab, hpg * LANES)
    return o.reshape(shape), lse.reshape(shape)


def _merge_body(o0_ref, o1_ref, o2_ref, l0_ref, l1_ref, l2_ref, out_ref, nat_ref):
    def natural(ref, slot, c):
        dil, rows = ref.shape[0], ref.shape[1]
        if dil == 1:
            return ref[0, :, _lanes(c)]
        for r in range(dil):
            nat_ref[slot, pl.ds(r, rows, stride=dil), :] = ref[r, :, _lanes(c)]
        return nat_ref[slot]

    for c in range(out_ref.shape[1] // LANES):
        o0, o1, o2 = natural(o0_ref, None, c), natural(o1_ref, 0, c), natural(o2_ref, 1, c)
        l0, l1, l2 = natural(l0_ref, None, c), natural(l1_ref, 2, c), natural(l2_ref, 3, c)
        m = jnp.maximum(jnp.maximum(l0, l1), l2)
        w0, w1, w2 = jnp.exp(l0 - m), jnp.exp(l1 - m), jnp.exp(l2 - m)
        out_ref[:, _lanes(c)] = ((w0 * o0 + w1 * o1 + w2 * o2) / (w0 + w1 + w2)).astype(out_ref.dtype)


def _merge_groups(outs, lses):
    tiles, _, _, width = outs[0].shape
    row_tile = outs[0].shape[1] * outs[0].shape[2]
    specs = [pl.BlockSpec((None,) + a.shape[1:], lambda i: (i, 0, 0, 0)) for a in (*outs, *lses)]
    return pl.pallas_call(
        _merge_body,
        grid=(tiles,),
        in_specs=specs,
        out_specs=pl.BlockSpec((row_tile, width), lambda i: (i, 0)),
        out_shape=jax.ShapeDtypeStruct((tiles * row_tile, width), BF16),
        scratch_shapes=[pltpu.VMEM((4, row_tile, LANES), F32)],
        compiler_params=_params("parallel"),
        name="dilated_merge",
    )(*outs, *lses)


def _mla_body(q_ref, kv_ref, kr_ref, o_ref, qt_ref, vt_ref, s_ref, p_ref, acc_ref, *, tq, tk, heads):
    qi = pl.program_id(2)

    @pl.when(qi == 0)
    def _():
        _fill_transposed(vt_ref, kv_ref, [(2 * c + 1) * LANES for c in range(heads)], tk)

    def make_score_fn(c):
        qt_ref[c] = q_ref[:, _lanes(c, 2 * LANES)].T

        def score(start):
            k = jnp.concatenate([kv_ref[pl.ds(start, tk), _lanes(2 * c)], kr_ref[pl.ds(start, tk), :]], axis=1)
            return jnp.dot(k, qt_ref[c], preferred_element_type=F32)

        return score

    diag = _causal_diag_masks(tk, tq, tq)
    results = _causal_flash_t([make_score_fn(c) for c in range(heads)], vt_ref, s_ref, p_ref, acc_ref,
                              qi * (tq // tk), lambda d, c: diag[d], tq // tk)
    for c, (l, acc) in enumerate(results):
        o_ref[:, _lanes(c)] = (acc / l).T.astype(o_ref.dtype)


def _mla_attention(q, kv, k_rope_src, *, batch, seq, tq, tk, heads, k_rope_col):
    nq = seq // tq
    assert C_HEADS % heads == 0 and seq % tq == 0 and tq % tk == 0
    return pl.pallas_call(
        functools.partial(_mla_body, tq=tq, tk=tk, heads=heads),
        grid=(batch, C_HEADS // heads, nq),
        in_specs=[pl.BlockSpec((tq, heads * 2 * LANES), lambda b, h, i: (b * nq + i, h)),
                  pl.BlockSpec((seq, heads * 2 * LANES), lambda b, h, i: (b, h)),
                  pl.BlockSpec((seq, LANES), lambda b, h, i: (b, k_rope_col))],
        out_specs=pl.BlockSpec((tq, heads * LANES), lambda b, h, i: (b * nq + i, h)),
        out_shape=jax.ShapeDtypeStruct((batch * seq, C_HEADS * LANES), BF16),
        scratch_shapes=[pltpu.VMEM((heads, 2 * LANES, tq), BF16)] + _flash_scratch(heads, tk, tq, LANES, seq),
        compiler_params=_params("parallel", "parallel", "arbitrary"),
        name="mla_attn",
    )(q, kv, k_rope_src)


def _moba_body(q_ref, k_ref, v_ref, o_ref, kmean_ref, sel_ref, qt_ref, vt_ref, s_ref, p_ref, acc_ref, *,
               n_blk, tq, heads):
    qi = pl.program_id(2)
    blk = MOBA_BLOCK
    n_own = tq // blk

    @pl.when(qi == 0)
    def _():
        for c in range(heads):
            k_all = k_ref[:, _lanes(c)].astype(F32).reshape(n_blk, blk, LANES)
            kmean_ref[c] = jnp.mean(k_all, axis=1)
        _fill_transposed(vt_ref, v_ref, [c * LANES for c in range(heads)], blk)

    blk_id = lax.broadcasted_iota(jnp.int32, (n_blk, 1), 0)
    query_col = lax.broadcasted_iota(jnp.int32, (1, tq), 1)
    own_in_tile = query_col // blk
    past = blk_id < qi * n_own + own_in_tile
    causal = lax.broadcasted_iota(jnp.int32, (blk, 1), 0) <= query_col % blk

    def make_score_fn(c):
        q_t = q_ref[:, _lanes(c)].T
        km = kmean_ref[c]
        km_hi = km.astype(BF16)
        rem = km - km_hi.astype(F32)
        km_mid = rem.astype(BF16)
        km_lo = (rem - km_mid.astype(F32)).astype(BF16)
        gate = (jnp.dot(km_hi, q_t, preferred_element_type=F32) + jnp.dot(km_mid, q_t, preferred_element_type=F32)
                + jnp.dot(km_lo, q_t, preferred_element_type=F32))
        gate = jnp.where(past, gate, -jnp.inf)
        selected = jnp.zeros(gate.shape, F32)
        for _ in range(min(MOBA_TOPK, n_blk - 1)):
            best = jnp.max(gate, axis=0, keepdims=True)
            first = jnp.min(jnp.where(gate == best, blk_id, n_blk), axis=0, keepdims=True)
            pick = blk_id == first
            selected = jnp.where(jnp.logical_and(pick, past), 1.0, selected)
            gate = jnp.where(pick, -jnp.inf, gate)
        sel_ref[c] = selected
        qt_ref[c] = q_t
        return lambda start: jnp.dot(k_ref[pl.ds(start, blk), _lanes(c)], qt_ref[c], preferred_element_type=F32)

    score_fns = [make_score_fn(c) for c in range(heads)]
    n_past = qi * n_own
    chosen_fns = [lambda n, c=c: jnp.logical_or(sel_ref[c, pl.ds(n, 1), :] > 0.0, n >= n_past)
                  for c in range(heads)]

    def own_block_mask(d, c):
        chosen = sel_ref[c, pl.ds(n_past + d, 1), :] > 0.0
        return jnp.logical_or(jnp.logical_and(own_in_tile == d, causal),
                              jnp.logical_and(own_in_tile > d, chosen))

    results = _causal_flash_t(score_fns, vt_ref, s_ref, p_ref, acc_ref, n_past, own_block_mask, n_own,
                              past_mask_fns=chosen_fns)
    for c, (l, acc) in enumerate(results):
        o_ref[:, _lanes(c)] = (acc / l).T.astype(o_ref.dtype)


def _moba_attention(qkv, *, batch, seq, tq, heads, q_col, k_col, v_col):
    blk = MOBA_BLOCK
    n_blk = seq // blk
    nq = seq // tq
    width = heads * LANES
    assert seq % tq == 0 and tq % blk == 0 and D_HEADS % heads == 0
    assert q_col % heads == 0 and k_col % heads == 0 and v_col % heads == 0
    return pl.pallas_call(
        functools.partial(_moba_body, n_blk=n_blk, tq=tq, heads=heads),
        grid=(batch, D_HEADS // heads, nq),
        in_specs=[pl.BlockSpec((tq, width), lambda b, h, i: (b * nq + i, q_col // heads + h)),
                  pl.BlockSpec((seq, width), lambda b, h, i: (b, k_col // heads + h)),
                  pl.BlockSpec((seq, width), lambda b, h, i: (b, v_col // heads + h))],
        out_specs=pl.BlockSpec((tq, width), lambda b, h, i: (b * nq + i, h)),
        out_shape=jax.ShapeDtypeStruct((batch * seq, D_HEADS * LANES), BF16),
        scratch_shapes=[pltpu.VMEM((heads, n_blk, LANES), F32),
                        pltpu.VMEM((heads, n_blk, tq), F32),
                        pltpu.VMEM((heads, LANES, tq), BF16)] + _flash_scratch(heads, blk, tq, LANES, seq),
        compiler_params=_params("parallel", "parallel", "arbitrary"),
        name="moba_attn",
    )(qkv, qkv, qkv)


def _rope_tables(seq):
    pos = jnp.arange(seq, dtype=F32)[:, None]

    def angles(dim):
        inv = 1.0 / (ROPE_THETA ** (jnp.arange(0, dim, 2, dtype=F32) / dim))
        return pos * inv[None, :]

    a128 = angles(HEAD_DIM)
    a64 = angles(A_QK_DIM)
    c128 = jnp.concatenate([jnp.cos(a128)] * 2, axis=1)
    s128 = jnp.concatenate([-jnp.sin(a128), jnp.sin(a128)], axis=1)
    c64 = jnp.concatenate([jnp.cos(a64)] * 4, axis=1)
    s64 = jnp.concatenate([-jnp.sin(a64)] * 2 + [jnp.sin(a64)] * 2, axis=1)
    return c128, s128, c64, s64


def _interleave_diff_heads(w):
    d = w.shape[0]
    return w.reshape(d, A_HEADS, 2, 2, 32).transpose(0, 1, 3, 2, 4).reshape(d, A_HEADS * LANES)


def _spread_rope64(w):
    z = jnp.zeros(w.shape[:-1] + (32,), w.dtype)
    return jnp.concatenate([w[..., :32], z, w[..., 32:], z], axis=-1)


def _ab_mixer(x, gain, w_in, lam_params, subln, w_out, layer_idx, rope, *, batch, seq):
    aw = A_HEADS * LANES
    w = jnp.concatenate([_interleave_diff_heads(w_in[:, :aw]), _interleave_diff_heads(w_in[:, aw:2 * aw]),
                         w_in[:, 2 * aw:]], axis=1).astype(BF16)
    a_scale = A_QK_DIM ** -0.5 * LOG2_E
    b_scale = HEAD_DIM ** -0.5
    dils = [dil for _, dil in B_GROUPS for _ in range(B_HEADS_PER_GROUP)]
    modes = ([(ROPE64, a_scale, 1)] * A_HEADS + [(ROPE64, 1.0, 1)] * A_HEADS + [(PLAIN, 1.0, 1)] * A_HEADS
             + [(ROPE128, b_scale, d) for d in dils] + [(ROPE128, 1.0, d) for d in dils]
             + [(PLAIN, 1.0, d) for d in dils])
    row_tile = 1024
    qkv = _norm_proj(x, gain, w, rope, modes, seq=seq, tm=row_tile, tn=768, out_dtype=BF16)
    lam_init = 0.8 - 0.6 * math.exp(-0.3 * layer_idx)
    oa = _diff_attention(qkv, lam_params, subln, batch=batch, seq=seq, tq=512, tk=256, heads=2,
                         q_col=0, k_col=A_HEADS, v_col=2 * A_HEADS, lam_init=lam_init)
    b0 = 3 * A_HEADS
    outs, lses = [], []
    for g, (_, dil) in enumerate(B_GROUPS):
        o, lse = _dilated_group(qkv, batch=batch, seq=seq, row_tile=row_tile, group=g, dil=dil,
                                heads=2 if dil == 1 else 4, residues=4 if dil == 16 else 1,
                                q_col=b0, k_col=b0 + B_HEADS, v_col=b0 + 2 * B_HEADS)
        outs.append(o)
        lses.append(lse)
    ob = _merge_groups(outs, lses)
    return _out_proj(oa, ob, w_out.astype(BF16), x, tm=1024, tn=1024)


def _cd_mixer(x, gain, w_in, q_norm, w_uq, kv_norm, w_ukv, w_out, rope, *, batch, seq):
    lat = C_Q_LORA + C_KV_LORA
    d_scale = HEAD_DIM ** -0.5 * LOG2_E
    w_main = jnp.concatenate([w_in[:, lat + C_ROPE:], _spread_rope64(w_in[:, lat:lat + C_ROPE])],
                             axis=1).astype(BF16)
    modes = ([(ROPE128, d_scale, 1)] * D_HEADS + [(ROPE128, 1.0, 1)] * D_HEADS + [(PLAIN, 1.0, 1)] * D_HEADS
             + [(ROPE64, 1.0, 1)])
    main = _norm_proj(x, gain, w_main, rope, modes, seq=seq, tm=1024, tn=5 * LANES, out_dtype=BF16)

    c_scale = (C_NOPE + C_ROPE) ** -0.5 * LOG2_E
    w_q = w_uq.reshape(C_Q_LORA, C_HEADS, C_NOPE + C_ROPE)
    w_q = jnp.concatenate([w_q[..., :C_NOPE], _spread_rope64(w_q[..., C_NOPE:])], axis=-1)
    w_q = w_q.reshape(C_Q_LORA, C_HEADS * 2 * LANES).astype(BF16)
    qc, kv = _mla_proj(x, gain, w_in[:, :lat].astype(BF16), q_norm, w_q, kv_norm, w_ukv.astype(BF16), rope,
                       seq=seq, tm=512, q_scale=c_scale)
    oc = _mla_attention(qc, kv, main, batch=batch, seq=seq, tq=1024, tk=256, heads=2, k_rope_col=3 * D_HEADS)
    od = _moba_attention(main, batch=batch, seq=seq, tq=512, heads=4,
                         q_col=0, k_col=D_HEADS, v_col=2 * D_HEADS)
    return _out_proj(oc, od, w_out.astype(BF16), x, tm=1024, tn=1024)


def kernel(x, p, ffn_norm, ffn_w_gu, ffn_w_down, mix_norm, ab_w_in, ab_lambda, ab_subln, ab_w_out,
           cd_w_in, cd_q_norm, cd_w_uq, cd_kv_norm, cd_w_ukv, cd_w_out, ple_norm, ple_w_gate,
           ple_w_proj, final_norm):
    batch, seq, d = x.shape
    depth = p.shape[0]
    rope = _rope_tables(seq)
    w_gu, w_down = ffn_w_gu.astype(BF16), ffn_w_down.astype(BF16)
    x = x.reshape(batch * seq, d)
    for i in range(depth):
        j = i // 2
        x = _ffn(x, ffn_norm[i, 0], w_gu, w_down, i, 0, tm=1024, tf=512)
        if i % 2 == 0:
            x = _ab_mixer(x, mix_norm[i], ab_w_in[j], ab_lambda[j], ab_subln[j], ab_w_out[j], i, rope,
                          batch=batch, seq=seq)
        else:
            x = _cd_mixer(x, mix_norm[i], cd_w_in[j], cd_q_norm[j], cd_w_uq[j], cd_kv_norm[j], cd_w_ukv[j],
                          cd_w_out[j], rope, batch=batch, seq=seq)
        x = _ffn(x, ffn_norm[i, 1], w_gu, w_down, i, 1, tm=1024, tf=512)
        x = _ple(x, p.reshape(depth * batch * seq, -1), i, ple_norm[i], ple_w_gate[i].astype(BF16),
                 ple_w_proj[i].astype(BF16), final_norm, tm=512, final_norm=(i == depth - 1))
    return x.reshape(batch, seq, d)
```

```python
import functools
import math

import jax
import jax.numpy as jnp
from jax import lax
from jax.experimental import pallas as pl
from jax.experimental.pallas import tpu as pltpu

F32 = jnp.float32
BF16 = jnp.bfloat16

D_MODEL = 2048
D_FF = 5632
PLE_DIM = 256
HEAD_DIM = 128
ROPE_THETA = 10000.0
NORM_EPS = 1e-6

A_HEADS = 8
A_QK_DIM = 64
B_GROUPS = ((128, 1), (512, 4), (2048, 16))
B_HEADS_PER_GROUP = 4
B_HEADS = B_HEADS_PER_GROUP * len(B_GROUPS)
C_HEADS = 8
C_Q_LORA = 512
C_KV_LORA = 256
C_NOPE = 128
C_ROPE = 64
D_HEADS = 8
MOBA_BLOCK = 256
MOBA_TOPK = 3

LOG2_E = 1.0 / math.log(2.0)
PAST_BLOCKS_PER_ITERATION = 2
SUM_ROWS = 16
LANES = 128
VMEM_LIMIT = 56 * 1024 * 1024

PLAIN, ROPE128, ROPE64 = 0, 1, 2


def _rms(x, gain):
    return x * lax.rsqrt(jnp.mean(x * x, axis=-1, keepdims=True) + NORM_EPS) * gain


def _params(*semantics):
    return pltpu.CompilerParams(dimension_semantics=semantics, vmem_limit_bytes=VMEM_LIMIT)


def _norm_proj_body(x_ref, g_ref, w_ref, c128_ref, s128_ref, c64_ref, s64_ref, o_ref, xn_ref, stage_ref, *,
                    tile_patterns):
    j = pl.program_id(1)
    tm = o_ref.shape[0]

    @pl.when(j == 0)
    def _():
        xn_ref[...] = _rms(x_ref[...], g_ref[...]).astype(BF16)

    def tile(pattern):
        acc = jnp.dot(xn_ref[...], w_ref[...], preferred_element_type=F32)
        for c, (kind, scale, dil) in enumerate(pattern):
            a = acc[:, _lanes(c)]
            if scale != 1.0:
                a = a * scale
            if kind == ROPE128:
                a = a * c128_ref[...] + pltpu.roll(a, 64, 1) * s128_ref[...]
            elif kind == ROPE64:
                a = a * c64_ref[...] + pltpu.roll(a, 64, 1) * s64_ref[...]
            if dil == 1:
                o_ref[:, _lanes(c)] = a.astype(o_ref.dtype)
            else:
                rows = tm // dil
                stage_ref[c] = a
                for r in range(dil):
                    o_ref[r * rows:(r + 1) * rows, _lanes(c)] = (
                        stage_ref[c, pl.ds(r, rows, stride=dil), :].astype(o_ref.dtype))

    distinct = sorted(set(tile_patterns), key=tile_patterns.index)
    if len(distinct) == 1:
        tile(distinct[0])
    else:
        for pattern in distinct:
            tiles = [t for t, p in enumerate(tile_patterns) if p == pattern]
            cond = functools.reduce(jnp.logical_or, [j == t for t in tiles])
            pl.when(cond)(functools.partial(tile, pattern))


def _norm_proj(x, gain, w, rope, chunk_modes, *, seq, tm, tn, out_dtype):
    t, k = x.shape
    n = w.shape[1]
    per_tile = tn // LANES
    assert t % tm == 0 and n % tn == 0 and seq % tm == 0 and len(chunk_modes) * LANES == n
    tile_patterns = tuple(tuple(chunk_modes[a * per_tile:(a + 1) * per_tile]) for a in range(n // tn))
    seq_tiles = seq // tm
    tab = pl.BlockSpec((tm, LANES), lambda i, j: (i % seq_tiles, 0))
    return pl.pallas_call(
        functools.partial(_norm_proj_body, tile_patterns=tile_patterns),
        grid=(t // tm, n // tn),
        in_specs=[pl.BlockSpec((tm, k), lambda i, j: (i, 0)),
                  pl.BlockSpec((1, k), lambda i, j: (0, 0)),
                  pl.BlockSpec((k, tn), lambda i, j: (0, j)),
                  tab, tab, tab, tab],
        out_specs=pl.BlockSpec((tm, tn), lambda i, j: (i, j)),
        out_shape=jax.ShapeDtypeStruct((t, n), out_dtype),
        scratch_shapes=[pltpu.VMEM((tm, k), BF16), pltpu.VMEM((per_tile, tm, LANES), F32)],
        compiler_params=_params("parallel", "arbitrary"),
        name="norm_proj",
    )(x, gain.reshape(1, k), w, *rope)


def _out_proj_body(a1_ref, a2_ref, w1_ref, w2_ref, res_ref, o_ref):
    o_ref[...] = (res_ref[...] + jnp.dot(a1_ref[...], w1_ref[...], preferred_element_type=F32)
                  + jnp.dot(a2_ref[...], w2_ref[...], preferred_element_type=F32))


def _out_proj(a1, a2, w, res, *, tm, tn):
    t, k1 = a1.shape
    k2 = a2.shape[1]
    n = w.shape[1]
    assert t % tm == 0 and n % tn == 0 and w.shape[0] == k1 + k2 and k1 % k2 == 0
    return pl.pallas_call(
        _out_proj_body,
        grid=(t // tm, n // tn),
        in_specs=[pl.BlockSpec((tm, k1), lambda i, j: (i, 0)),
                  pl.BlockSpec((tm, k2), lambda i, j: (i, 0)),
                  pl.BlockSpec((k1, tn), lambda i, j: (0, j)),
                  pl.BlockSpec((k2, tn), lambda i, j: (k1 // k2, j)),
                  pl.BlockSpec((tm, tn), lambda i, j: (i, j))],
        out_specs=pl.BlockSpec((tm, tn), lambda i, j: (i, j)),
        out_shape=jax.ShapeDtypeStruct((t, n), F32),
        compiler_params=_params("parallel", "parallel"),
        name="out_proj",
    )(a1, a2, w, w, res)


def _mla_proj_body(x_ref, g_ref, wl_ref, qn_ref, wq_ref, kvn_ref, wkv_ref, c64_ref, s64_ref, q_ref, kv_ref, *,
                   q_lora, q_scale):
    half = x_ref.shape[0] // 2
    for rows in (slice(0, half), slice(half, 2 * half)):
        xn = _rms(x_ref[rows, :], g_ref[...]).astype(BF16)
        latent = jnp.dot(xn, wl_ref[...], preferred_element_type=F32)
        c_q = _rms(latent[:, :q_lora], qn_ref[...]).astype(BF16)
        c_kv = _rms(latent[:, q_lora:], kvn_ref[...]).astype(BF16)
        kv_ref[rows, :] = jnp.dot(c_kv, wkv_ref[...], preferred_element_type=F32).astype(kv_ref.dtype)
        q = jnp.dot(c_q, wq_ref[...], preferred_element_type=F32) * q_scale
        for c in range(q_ref.shape[1] // LANES):
            a = q[:, _lanes(c)]
            if c % 2 == 1:
                a = a * c64_ref[rows, :] + pltpu.roll(a, 64, 1) * s64_ref[rows, :]
            q_ref[rows, _lanes(c)] = a.astype(q_ref.dtype)


def _mla_proj(x, gain, w_latent, q_norm, w_q, kv_norm, w_kv, rope, *, seq, tm, q_scale):
    t, d = x.shape
    q_lora, kv_lora = w_q.shape[0], w_kv.shape[0]
    assert t % tm == 0 and seq % tm == 0 and w_latent.shape[1] == q_lora + kv_lora
    seq_tiles = seq // tm
    const = lambda i: (0, 0)
    tab = pl.BlockSpec((tm, LANES), lambda i: (i % seq_tiles, 0))
    full = lambda a: pl.BlockSpec(a.shape, const)
    out_spec = lambda a: pl.BlockSpec((tm, a.shape[1]), lambda i: (i, 0))
    return pl.pallas_call(
        functools.partial(_mla_proj_body, q_lora=q_lora, q_scale=q_scale),
        grid=(t // tm,),
        in_specs=[pl.BlockSpec((tm, d), lambda i: (i, 0)), pl.BlockSpec((1, d), const), full(w_latent),
                  pl.BlockSpec((1, q_lora), const), full(w_q), pl.BlockSpec((1, kv_lora), const), full(w_kv),
                  tab, tab],
        out_specs=[out_spec(w_q), out_spec(w_kv)],
        out_shape=[jax.ShapeDtypeStruct((t, w_q.shape[1]), BF16), jax.ShapeDtypeStruct((t, w_kv.shape[1]), BF16)],
        compiler_params=_params("parallel"),
        name="mla_proj",
    )(x, gain.reshape(1, d), w_latent, q_norm.reshape(1, q_lora), w_q, kv_norm.reshape(1, kv_lora), w_kv,
      rope[2], rope[3])


def _ffn_body(x_ref, g_ref, wg_ref, wu_ref, wd_ref, o_ref, xn_ref):
    j = pl.program_id(1)

    @pl.when(j == 0)
    def _():
        xn_ref[...] = _rms(x_ref[...], g_ref[...]).astype(BF16)
        o_ref[...] = jnp.zeros_like(o_ref)

    xn = xn_ref[...]
    g = jnp.dot(xn, wg_ref[...], preferred_element_type=F32)
    u = jnp.dot(xn, wu_ref[...], preferred_element_type=F32)
    h = (g * jax.nn.sigmoid(g) * u).astype(BF16)
    o_ref[...] += jnp.dot(h, wd_ref[...], preferred_element_type=F32)

    @pl.when(j == pl.num_programs(1) - 1)
    def _():
        o_ref[...] = x_ref[...] + 0.5 * o_ref[...]


def _ffn(x, gain, w_gu, w_down, layer, half, *, tm, tf):
    t, d = x.shape
    d_ff = w_down.shape[2]
    assert t % tm == 0 and d_ff % tf == 0
    nf = d_ff // tf
    return pl.pallas_call(
        _ffn_body,
        grid=(t // tm, nf),
        in_specs=[pl.BlockSpec((tm, d), lambda i, j: (i, 0)),
                  pl.BlockSpec((1, d), lambda i, j: (0, 0)),
                  pl.BlockSpec((None, None, d, tf), lambda i, j: (layer, half, 0, j)),
                  pl.BlockSpec((None, None, d, tf), lambda i, j: (layer, half, 0, nf + j)),
                  pl.BlockSpec((None, None, tf, d), lambda i, j: (layer, half, j, 0))],
        out_specs=pl.BlockSpec((tm, d), lambda i, j: (i, 0)),
        out_shape=jax.ShapeDtypeStruct((t, d), F32),
        scratch_shapes=[pltpu.VMEM((tm, d), BF16)],
        compiler_params=_params("parallel", "arbitrary"),
        name="ffn",
    )(x, gain.reshape(1, d), w_gu, w_gu, w_down)


def _ple_body(x_ref, p_ref, g_ref, wg_ref, wp_ref, fg_ref, o_ref, *, final_norm):
    half = x_ref.shape[0] // 2
    for rows in (slice(0, half), slice(half, 2 * half)):
        x = x_ref[rows, :]
        xn = _rms(x, g_ref[...]).astype(BF16)
        gate = jax.nn.sigmoid(jnp.dot(xn, wg_ref[...], preferred_element_type=F32))
        proj = jnp.dot(p_ref[rows, :].astype(BF16), wp_ref[...], preferred_element_type=F32)
        y = x + gate * proj
        if final_norm:
            y = _rms(y, fg_ref[...])
        o_ref[rows, :] = y


def _ple(x, p, layer, gain, w_gate, w_proj, final_gain, *, tm, final_norm):
    t, d = x.shape
    pd = p.shape[1]
    assert t % tm == 0
    first_tile = layer * (t // tm)
    const = lambda i: (0, 0)
    return pl.pallas_call(
        functools.partial(_ple_body, final_norm=final_norm),
        grid=(t // tm,),
        in_specs=[pl.BlockSpec((tm, d), lambda i: (i, 0)),
                  pl.BlockSpec((tm, pd), lambda i: (first_tile + i, 0)),
                  pl.BlockSpec((1, d), const),
                  pl.BlockSpec((d, d), const),
                  pl.BlockSpec((pd, d), const),
                  pl.BlockSpec((1, d), const)],
        out_specs=pl.BlockSpec((tm, d), lambda i: (i, 0)),
        out_shape=jax.ShapeDtypeStruct((t, d), F32),
        compiler_params=_params("parallel"),
        name="ple",
    )(x, p, gain.reshape(1, d), w_gate, w_proj, final_gain.reshape(1, d))


def _dot_nt(a, b):
    return lax.dot_general(a, b, (((1,), (1,)), ((), ())), preferred_element_type=F32)


def _lanes(c, width=LANES):
    return slice(c * width, (c + 1) * width)


def _fill_transposed(vt_ref, v_ref, lane_offsets, chunk):
    dv = vt_ref.shape[1] - SUM_ROWS
    ones_row = jnp.where(lax.broadcasted_iota(jnp.int32, (SUM_ROWS, chunk), 0) == 0, 1.0, 0.0).astype(BF16)

    def body(j, _):
        start = pl.multiple_of(j * chunk, chunk)
        for c, off in enumerate(lane_offsets):
            vt_ref[c, :dv, pl.ds(start, chunk)] = v_ref[pl.ds(start, chunk), off:off + dv].T
            vt_ref[c, dv:, pl.ds(start, chunk)] = ones_row
        return 0

    lax.fori_loop(0, v_ref.shape[0] // chunk, body, 0)


def _flash_scratch(heads, tk, cols, dv, seq):
    return [pltpu.VMEM((heads, dv + SUM_ROWS, seq), BF16),
            pltpu.VMEM((heads, tk, cols), F32), pltpu.VMEM((heads, tk, cols), BF16),
            pltpu.VMEM((heads, dv + SUM_ROWS, cols), F32)]


def _causal_diag_masks(tk, tq, cols):
    key_row = lax.broadcasted_iota(jnp.int32, (tk, 1), 0)
    query_col = lax.broadcasted_iota(jnp.int32, (1, cols), 1) % tq
    return [key_row + d * tk <= query_col for d in range(tq // tk)]


def _causal_flash_t(score_fns, vt_ref, s_ref, p_ref, acc_ref, n_past, diag_mask_fn, n_diag, past_mask_fns=None):
    heads = range(len(score_fns))
    _, tk, cols = s_ref.shape

    def stage_scores(block):
        start = pl.multiple_of(block * tk, tk)
        for c in heads:
            s = score_fns[c](start)
            if past_mask_fns is not None:
                s = jnp.where(past_mask_fns[c](block), s, -jnp.inf)
            s_ref[c] = s

    def accumulate(block, alphas):
        start = pl.multiple_of(block * tk, tk)
        for c in heads:
            pv = jnp.dot(vt_ref[c, :, pl.ds(start, tk)], p_ref[c], preferred_element_type=F32)
            acc_ref[c] = alphas[c] * acc_ref[c] + pv

    def softmax(stats, diag=None):
        new_stats, alphas = [], []
        for c in heads:
            m = stats[c]
            if diag is None:
                read = lambda: s_ref[c]
            else:
                visible = diag_mask_fn(diag, c)
                read = lambda: jnp.where(visible, s_ref[c], -jnp.inf)
            m_new = jnp.maximum(m, jnp.max(read(), axis=0, keepdims=True))
            alpha = jnp.exp2(m - m_new)
            p_ref[c] = jnp.exp2(read() - m_new).astype(BF16)
            new_stats.append(m_new)
            alphas.append(alpha)
        return tuple(new_stats), tuple(alphas)

    def past_block(j, carry):
        stats, alphas = carry
        accumulate(jnp.maximum(j - 1, 0), alphas)
        stats, alphas = softmax(stats)
        stage_scores(j + 1)
        return stats, alphas

    for c in heads:
        p_ref[c] = jnp.zeros(p_ref.shape[1:], BF16)
        acc_ref[c] = jnp.zeros(acc_ref.shape[1:], F32)
    stage_scores(0)
    init = (tuple(jnp.full((1, cols), jnp.finfo(F32).min, F32) for _ in heads),
            tuple(jnp.ones((1, cols), F32) for _ in heads))
    def past_group(t, carry):
        for u in range(PAST_BLOCKS_PER_ITERATION):
            carry = past_block(t * PAST_BLOCKS_PER_ITERATION + u, carry)
        return carry

    n_grouped = n_past // PAST_BLOCKS_PER_ITERATION
    carry = lax.fori_loop(0, n_grouped, past_group, init)
    stats, alphas = lax.fori_loop(n_grouped * PAST_BLOCKS_PER_ITERATION, n_past, past_block, carry)

    accumulate(jnp.maximum(n_past - 1, 0), alphas)
    for d in range(n_diag):
        stats, alphas = softmax(stats, diag=d)
        if d + 1 < n_diag:
            stage_scores(n_past + d + 1)
        accumulate(n_past + d, alphas)
    dv = acc_ref.shape[1] - SUM_ROWS
    return tuple((acc_ref[c, dv:dv + 1, :], acc_ref[c, :dv, :]) for c in heads)


def _diff_attn_body(lam_ref, subln_ref, q_ref, k_ref, v_ref, o_ref, qt_ref, vt_ref, s_ref, p_ref, acc_ref, *,
                    tq, tk, heads, lam_init):
    qi = pl.program_id(2)

    @pl.when(qi == 0)
    def _():
        _fill_transposed(vt_ref, v_ref, [c * LANES for c in range(heads)], tk)

    is_map0 = (lax.broadcasted_iota(jnp.int32, (LANES, 1), 0) % 64) < 32

    def make_score_fn(c):
        q_t = q_ref[:, _lanes(c)].T
        zero = jnp.zeros_like(q_t)
        qt_ref[c] = jnp.concatenate([jnp.where(is_map0, q_t, zero), jnp.where(is_map0, zero, q_t)], axis=1)
        return lambda start: jnp.dot(k_ref[pl.ds(start, tk), _lanes(c)], qt_ref[c], preferred_element_type=F32)

    diag = _causal_diag_masks(tk, tq, 2 * tq)
    results = _causal_flash_t([make_score_fn(c) for c in range(heads)], vt_ref, s_ref, p_ref, acc_ref,
                              qi * (tq // tk), lambda d, c: diag[d], tq // tk)

    lp = lam_ref[...]
    lam = (jnp.exp(jnp.sum(lp[0:1] * lp[1:2], axis=-1, keepdims=True))
           - jnp.exp(jnp.sum(lp[2:3] * lp[3:4], axis=-1, keepdims=True)) + lam_init)
    for c, (l, acc) in enumerate(results):
        o_t = acc / l
        o = (o_t[:, :tq] - lam * o_t[:, tq:]).T
        o_ref[:, _lanes(c)] = (_rms(o, subln_ref[...]) * (1.0 - lam_init)).astype(o_ref.dtype)


def _diff_attention(qkv, lam_params, subln, *, batch, seq, tq, tk, heads, q_col, k_col, v_col, lam_init):
    nq = seq // tq
    width = heads * LANES
    assert A_HEADS % heads == 0 and q_col % heads == 0 and k_col % heads == 0 and v_col % heads == 0
    assert seq % tq == 0 and tq % tk == 0
    return pl.pallas_call(
        functools.partial(_diff_attn_body, tq=tq, tk=tk, heads=heads, lam_init=lam_init),
        grid=(batch, A_HEADS // heads, nq),
        in_specs=[pl.BlockSpec((4, A_QK_DIM), lambda b, h, i: (0, 0)),
                  pl.BlockSpec((1, LANES), lambda b, h, i: (0, 0)),
                  pl.BlockSpec((tq, width), lambda b, h, i: (b * nq + i, q_col // heads + h)),
                  pl.BlockSpec((seq, width), lambda b, h, i: (b, k_col // heads + h)),
                  pl.BlockSpec((seq, width), lambda b, h, i: (b, v_col // heads + h))],
        out_specs=pl.BlockSpec((tq, width), lambda b, h, i: (b * nq + i, h)),
        out_shape=jax.ShapeDtypeStruct((batch * seq, A_HEADS * LANES), BF16),
        scratch_shapes=[pltpu.VMEM((heads, LANES, 2 * tq), BF16)] + _flash_scratch(heads, tk, 2 * tq, LANES, seq),
        compiler_params=_params("parallel", "parallel", "arbitrary"),
        name="diff_attn",
    )(lam_params, subln.reshape(1, LANES), qkv, qkv, qkv)


def _tile_index(ref, r, i, tile):
    per_slab = ref.shape[2]
    if per_slab >= tile:
        runs = per_slab // tile
        return i // runs, r, pl.ds(pl.multiple_of((i % runs) * tile, tile), tile)
    slabs = tile // per_slab
    return pl.ds(i * slabs, slabs), r, slice(None)


def _load_tile(ref, r, i, c, tile):
    slab, res, rows = _tile_index(ref, r, i, tile)
    return ref[slab, res, rows, _lanes(c)].reshape(tile, LANES)


def _store_tile(ref, r, i, c, tile, value):
    slab, res, rows = _tile_index(ref, r, i, tile)
    per_slab = ref.shape[2]
    ref[slab, res, rows, _lanes(c)] = (value if per_slab >= tile
                                       else value.reshape(tile // per_slab, per_slab, LANES))


def _dilated_body(q_ref, k_ref, v_ref, o_ref, lse_ref, *, tile, heads):
    slabs, residues, per_slab, _ = q_ref.shape
    row = lax.broadcasted_iota(jnp.int32, (tile, 1), 0)
    streams = [(r, c) for r in range(residues) for c in range(heads)]

    def attend(r, i, c, k, v, valid):
        s = jnp.where(valid, _dot_nt(_load_tile(q_ref, r, i, c, tile), k), -jnp.inf)
        m = jnp.max(s, axis=-1, keepdims=True)
        p = jnp.exp(s - m)
        l = jnp.sum(p, axis=-1, keepdims=True)
        o = jnp.dot(p.astype(BF16), v, preferred_element_type=F32) / l
        _store_tile(o_ref, r, i, c, tile, o)
        _store_tile(lse_ref, r, i, c, tile, jnp.broadcast_to(m + jnp.log(l), (tile, LANES)))

    col = lax.broadcasted_iota(jnp.int32, (1, tile), 1)
    for r, c in streams:
        attend(r, 0, c, _load_tile(k_ref, r, 0, c, tile), _load_tile(v_ref, r, 0, c, tile), col <= row)

    col2 = lax.broadcasted_iota(jnp.int32, (1, 2 * tile), 1)
    band = jnp.logical_and(col2 >= row, col2 <= row + tile)

    def body(i, _):
        for r, c in streams:
            k = jnp.concatenate([_load_tile(k_ref, r, i - 1, c, tile), _load_tile(k_ref, r, i, c, tile)], axis=0)
            v = jnp.concatenate([_load_tile(v_ref, r, i - 1, c, tile), _load_tile(v_ref, r, i, c, tile)], axis=0)
            attend(r, i, c, k, v, band)
        return 0

    n_tiles = slabs * per_slab // tile
    lax.fori_loop(1, n_tiles, body, 0, unroll=max(1, min(4, n_tiles - 1)))


def _dilated_group(qkv, *, batch, seq, row_tile, group, dil, heads, residues, q_col, k_col, v_col):
    hpg = B_HEADS_PER_GROUP
    tile = B_GROUPS[group][0] // dil
    slabs, per_slab = seq // row_tile, row_tile // dil
    assert tile == LANES and hpg % heads == 0 and (seq // dil) % tile == 0 and dil % residues == 0
    assert per_slab % tile == 0 or tile % per_slab == 0
    width = heads * LANES
    view = qkv.reshape(batch, slabs, dil, per_slab, qkv.shape[1])

    def in_spec(col):
        first = col + group * hpg
        assert first % heads == 0
        return pl.BlockSpec((None, slabs, residues, per_slab, width),
                            lambda b, h, r: (b, 0, r, 0, first // heads + h))

    out_spec = pl.BlockSpec((None, slabs, residues, per_slab, width), lambda b, h, r: (b, 0, r, 0, h))
    out_sds = jax.ShapeDtypeStruct((batch, slabs, dil, per_slab, hpg * LANES), F32)
    o, lse = pl.pallas_call(
        functools.partial(_dilated_body, tile=tile, heads=heads),
        grid=(batch, hpg // heads, dil // residues),
        in_specs=[in_spec(q_col), in_spec(k_col), in_spec(v_col)],
        out_specs=[out_spec, out_spec],
        out_shape=[out_sds, out_sds],
        compiler_params=_params("parallel", "parallel", "parallel"),
        name=f"dilated_g{group}",
    )(view, view, view)
    shape = (batch * slabs, dil, per_slab, hpg * LANES)
    return o.reshape(shape), lse.reshape(shape)


def _merge_body(o0_ref, o1_ref, o2_ref, l0_ref, l1_ref, l2_ref, out_ref, nat_ref):
    def natural(ref, slot, c):
        dil, rows = ref.shape[0], ref.shape[1]
        if dil == 1:
            return ref[0, :, _lanes(c)]
        for r in range(dil):
            nat_ref[slot, pl.ds(r, rows, stride=dil), :] = ref[r, :, _lanes(c)]
        return nat_ref[slot]

    for c in range(out_ref.shape[1] // LANES):
        o0, o1, o2 = natural(o0_ref, None, c), natural(o1_ref, 0, c), natural(o2_ref, 1, c)
        l0, l1, l2 = natural(l0_ref, None, c), natural(l1_ref, 2, c), natural(l2_ref, 3, c)
        m = jnp.maximum(jnp.maximum(l0, l1), l2)
        w0, w1, w2 = jnp.exp(l0 - m), jnp.exp(l1 - m), jnp.exp(l2 - m)
        out_ref[:, _lanes(c)] = ((w0 * o0 + w1 * o1 + w2 * o2) / (w0 + w1 + w2)).astype(out_ref.dtype)


def _merge_groups(outs, lses):
    tiles, _, _, width = outs[0].shape
    row_tile = outs[0].shape[1] * outs[0].shape[2]
    specs = [pl.BlockSpec((None,) + a.shape[1:], lambda i: (i, 0, 0, 0)) for a in (*outs, *lses)]
    return pl.pallas_call(
        _merge_body,
        grid=(tiles,),
        in_specs=specs,
        out_specs=pl.BlockSpec((row_tile, width), lambda i: (i, 0)),
        out_shape=jax.ShapeDtypeStruct((tiles * row_tile, width), BF16),
        scratch_shapes=[pltpu.VMEM((4, row_tile, LANES), F32)],
        compiler_params=_params("parallel"),
        name="dilated_merge",
    )(*outs, *lses)


def _mla_body(q_ref, kv_ref, kr_ref, o_ref, qt_ref, vt_ref, s_ref, p_ref, acc_ref, *, tq, tk, heads):
    qi = pl.program_id(2)

    @pl.when(qi == 0)
    def _():
        _fill_transposed(vt_ref, kv_ref, [(2 * c + 1) * LANES for c in range(heads)], tk)

    def make_score_fn(c):
        qt_ref[c] = q_ref[:, _lanes(c, 2 * LANES)].T

        def score(start):
            k = jnp.concatenate([kv_ref[pl.ds(start, tk), _lanes(2 * c)], kr_ref[pl.ds(start, tk), :]], axis=1)
            return jnp.dot(k, qt_ref[c], preferred_element_type=F32)

        return score

    diag = _causal_diag_masks(tk, tq, tq)
    results = _causal_flash_t([make_score_fn(c) for c in range(heads)], vt_ref, s_ref, p_ref, acc_ref,
                              qi * (tq // tk), lambda d, c: diag[d], tq // tk)
    for c, (l, acc) in enumerate(results):
        o_ref[:, _lanes(c)] = (acc / l).T.astype(o_ref.dtype)


def _mla_attention(q, kv, k_rope_src, *, batch, seq, tq, tk, heads, k_rope_col):
    nq = seq // tq
    assert C_HEADS % heads == 0 and seq % tq == 0 and tq % tk == 0
    return pl.pallas_call(
        functools.partial(_mla_body, tq=tq, tk=tk, heads=heads),
        grid=(batch, C_HEADS // heads, nq),
        in_specs=[pl.BlockSpec((tq, heads * 2 * LANES), lambda b, h, i: (b * nq + i, h)),
                  pl.BlockSpec((seq, heads * 2 * LANES), lambda b, h, i: (b, h)),
                  pl.BlockSpec((seq, LANES), lambda b, h, i: (b, k_rope_col))],
        out_specs=pl.BlockSpec((tq, heads * LANES), lambda b, h, i: (b * nq + i, h)),
        out_shape=jax.ShapeDtypeStruct((batch * seq, C_HEADS * LANES), BF16),
        scratch_shapes=[pltpu.VMEM((heads, 2 * LANES, tq), BF16)] + _flash_scratch(heads, tk, tq, LANES, seq),
        compiler_params=_params("parallel", "parallel", "arbitrary"),
        name="mla_attn",
    )(q, kv, k_rope_src)


def _moba_body(q_ref, k_ref, v_ref, o_ref, kmean_ref, sel_ref, qt_ref, vt_ref, s_ref, p_ref, acc_ref, *,
               n_blk, tq, heads):
    qi = pl.program_id(2)
    blk = MOBA_BLOCK
    n_own = tq // blk

    @pl.when(qi == 0)
    def _():
        for c in range(heads):
            k_all = k_ref[:, _lanes(c)].astype(F32).reshape(n_blk, blk, LANES)
            kmean_ref[c] = jnp.mean(k_all, axis=1)
        _fill_transposed(vt_ref, v_ref, [c * LANES for c in range(heads)], blk)

    blk_id = lax.broadcasted_iota(jnp.int32, (n_blk, 1), 0)
    query_col = lax.broadcasted_iota(jnp.int32, (1, tq), 1)
    own_in_tile = query_col // blk
    past = blk_id < qi * n_own + own_in_tile
    causal = lax.broadcasted_iota(jnp.int32, (blk, 1), 0) <= query_col % blk

    def make_score_fn(c):
        q_t = q_ref[:, _lanes(c)].T
        km = kmean_ref[c]
        km_hi = km.astype(BF16)
        rem = km - km_hi.astype(F32)
        km_mid = rem.astype(BF16)
        km_lo = (rem - km_mid.astype(F32)).astype(BF16)
        gate = (jnp.dot(km_hi, q_t, preferred_element_type=F32) + jnp.dot(km_mid, q_t, preferred_element_type=F32)
                + jnp.dot(km_lo, q_t, preferred_element_type=F32))
        gate = jnp.where(past, gate, -jnp.inf)
        selected = jnp.zeros(gate.shape, F32)
        for _ in range(min(MOBA_TOPK, n_blk - 1)):
            best = jnp.max(gate, axis=0, keepdims=True)
            first = jnp.min(jnp.where(gate == best, blk_id, n_blk), axis=0, keepdims=True)
            pick = blk_id == first
            selected = jnp.where(jnp.logical_and(pick, past), 1.0, selected)
            gate = jnp.where(pick, -jnp.inf, gate)
        sel_ref[c] = selected
        qt_ref[c] = q_t
        return lambda start: jnp.dot(k_ref[pl.ds(start, blk), _lanes(c)], qt_ref[c], preferred_element_type=F32)

    score_fns = [make_score_fn(c) for c in range(heads)]
    n_past = qi * n_own
    chosen_fns = [lambda n, c=c: jnp.logical_or(sel_ref[c, pl.ds(n, 1), :] > 0.0, n >= n_past)
                  for c in range(heads)]

    def own_block_mask(d, c):
        chosen = sel_ref[c, pl.ds(n_past + d, 1), :] > 0.0
        return jnp.logical_or(jnp.logical_and(own_in_tile == d, causal),
                              jnp.logical_and(own_in_tile > d, chosen))

    results = _causal_flash_t(score_fns, vt_ref, s_ref, p_ref, acc_ref, n_past, own_block_mask, n_own,
                              past_mask_fns=chosen_fns)
    for c, (l, acc) in enumerate(results):
        o_ref[:, _lanes(c)] = (acc / l).T.astype(o_ref.dtype)


def _moba_attention(qkv, *, batch, seq, tq, heads, q_col, k_col, v_col):
    blk = MOBA_BLOCK
    n_blk = seq // blk
    nq = seq // tq
    width = heads * LANES
    assert seq % tq == 0 and tq % blk == 0 and D_HEADS % heads == 0
    assert q_col % heads == 0 and k_col % heads == 0 and v_col % heads == 0
    return pl.pallas_call(
        functools.partial(_moba_body, n_blk=n_blk, tq=tq, heads=heads),
        grid=(batch, D_HEADS // heads, nq),
        in_specs=[pl.BlockSpec((tq, width), lambda b, h, i: (b * nq + i, q_col // heads + h)),
                  pl.BlockSpec((seq, width), lambda b, h, i: (b, k_col // heads + h)),
                  pl.BlockSpec((seq, width), lambda b, h, i: (b, v_col // heads + h))],
        out_specs=pl.BlockSpec((tq, width), lambda b, h, i: (b * nq + i, h)),
        out_shape=jax.ShapeDtypeStruct((batch * seq, D_HEADS * LANES), BF16),
        scratch_shapes=[pltpu.VMEM((heads, n_blk, LANES), F32),
                        pltpu.VMEM((heads, n_blk, tq), F32),
                        pltpu.VMEM((heads, LANES, tq), BF16)] + _flash_scratch(heads, blk, tq, LANES, seq),
        compiler_params=_params("parallel", "parallel", "arbitrary"),
        name="moba_attn",
    )(qkv, qkv, qkv)


def _rope_tables(seq):
    pos = jnp.arange(seq, dtype=F32)[:, None]

    def angles(dim):
        inv = 1.0 / (ROPE_THETA ** (jnp.arange(0, dim, 2, dtype=F32) / dim))
        return pos * inv[None, :]

    a128 = angles(HEAD_DIM)
    a64 = angles(A_QK_DIM)
    c128 = jnp.concatenate([jnp.cos(a128)] * 2, axis=1)
    s128 = jnp.concatenate([-jnp.sin(a128), jnp.sin(a128)], axis=1)
    c64 = jnp.concatenate([jnp.cos(a64)] * 4, axis=1)
    s64 = jnp.concatenate([-jnp.sin(a64)] * 2 + [jnp.sin(a64)] * 2, axis=1)
    return c128, s128, c64, s64


def _interleave_diff_heads(w):
    d = w.shape[0]
    return w.reshape(d, A_HEADS, 2, 2, 32).transpose(0, 1, 3, 2, 4).reshape(d, A_HEADS * LANES)


def _spread_rope64(w):
    z = jnp.zeros(w.shape[:-1] + (32,), w.dtype)
    return jnp.concatenate([w[..., :32], z, w[..., 32:], z], axis=-1)


def _ab_mixer(x, gain, w_in, lam_params, subln, w_out, layer_idx, rope, *, batch, seq):
    aw = A_HEADS * LANES
    w = jnp.concatenate([_interleave_diff_heads(w_in[:, :aw]), _interleave_diff_heads(w_in[:, aw:2 * aw]),
                         w_in[:, 2 * aw:]], axis=1).astype(BF16)
    a_scale = A_QK_DIM ** -0.5 * LOG2_E
    b_scale = HEAD_DIM ** -0.5
    dils = [dil for _, dil in B_GROUPS for _ in range(B_HEADS_PER_GROUP)]
    modes = ([(ROPE64, a_scale, 1)] * A_HEADS + [(ROPE64, 1.0, 1)] * A_HEADS + [(PLAIN, 1.0, 1)] * A_HEADS
             + [(ROPE128, b_scale, d) for d in dils] + [(ROPE128, 1.0, d) for d in dils]
             + [(PLAIN, 1.0, d) for d in dils])
    row_tile = 1024
    qkv = _norm_proj(x, gain, w, rope, modes, seq=seq, tm=row_tile, tn=768, out_dtype=BF16)
    lam_init = 0.8 - 0.6 * math.exp(-0.3 * layer_idx)
    oa = _diff_attention(qkv, lam_params, subln, batch=batch, seq=seq, tq=512, tk=512, heads=2,
                         q_col=0, k_col=A_HEADS, v_col=2 * A_HEADS, lam_init=lam_init)
    b0 = 3 * A_HEADS
    outs, lses = [], []
    for g, (_, dil) in enumerate(B_GROUPS):
        o, lse = _dilated_group(qkv, batch=batch, seq=seq, row_tile=row_tile, group=g, dil=dil,
                                heads=2 if dil == 1 else 4, residues=4 if dil == 16 else 1,
                                q_col=b0, k_col=b0 + B_HEADS, v_col=b0 + 2 * B_HEADS)
        outs.append(o)
        lses.append(lse)
    ob = _merge_groups(outs, lses)
    return _out_proj(oa, ob, w_out.astype(BF16), x, tm=1024, tn=1024)


def _cd_mixer(x, gain, w_in, q_norm, w_uq, kv_norm, w_ukv, w_out, rope, *, batch, seq):
    lat = C_Q_LORA + C_KV_LORA
    d_scale = HEAD_DIM ** -0.5 * LOG2_E
    w_main = jnp.concatenate([w_in[:, lat + C_ROPE:], _spread_rope64(w_in[:, lat:lat + C_ROPE])],
                             axis=1).astype(BF16)
    modes = ([(ROPE128, d_scale, 1)] * D_HEADS + [(ROPE128, 1.0, 1)] * D_HEADS + [(PLAIN, 1.0, 1)] * D_HEADS
             + [(ROPE64, 1.0, 1)])
    main = _norm_proj(x, gain, w_main, rope, modes, seq=seq, tm=1024, tn=5 * LANES, out_dtype=BF16)

    c_scale = (C_NOPE + C_ROPE) ** -0.5 * LOG2_E
    w_q = w_uq.reshape(C_Q_LORA, C_HEADS, C_NOPE + C_ROPE)
    w_q = jnp.concatenate([w_q[..., :C_NOPE], _spread_rope64(w_q[..., C_NOPE:])], axis=-1)
    w_q = w_q.reshape(C_Q_LORA, C_HEADS * 2 * LANES).astype(BF16)
    qc, kv = _mla_proj(x, gain, w_in[:, :lat].astype(BF16), q_norm, w_q, kv_norm, w_ukv.astype(BF16), rope,
                       seq=seq, tm=512, q_scale=c_scale)
    oc = _mla_attention(qc, kv, main, batch=batch, seq=seq, tq=1024, tk=512, heads=2, k_rope_col=3 * D_HEADS)
    od = _moba_attention(main, batch=batch, seq=seq, tq=512, heads=4,
                         q_col=0, k_col=D_HEADS, v_col=2 * D_HEADS)
    return _out_proj(oc, od, w_out.astype(BF16), x, tm=1024, tn=1024)


def kernel(x, p, ffn_norm, ffn_w_gu, ffn_w_down, mix_norm, ab_w_in, ab_lambda, ab_subln, ab_w_out,
           cd_w_in, cd_q_norm, cd_w_uq, cd_kv_norm, cd_w_ukv, cd_w_out, ple_norm, ple_w_gate,
           ple_w_proj, final_norm):
    batch, seq, d = x.shape
    depth = p.shape[0]
    rope = _rope_tables(seq)
    w_gu, w_down = ffn_w_gu.astype(BF16), ffn_w_down.astype(BF16)
    x = x.reshape(batch * seq, d)
    for i in range(depth):
        j = i // 2
        x = _ffn(x, ffn_norm[i, 0], w_gu, w_down, i, 0, tm=1024, tf=512)
        if i % 2 == 0:
            x = _ab_mixer(x, mix_norm[i], ab_w_in[j], ab_lambda[j], ab_subln[j], ab_w_out[j], i, rope,
                          batch=batch, seq=seq)
        else:
            x = _cd_mixer(x, mix_norm[i], cd_w_in[j], cd_q_norm[j], cd_w_uq[j], cd_kv_norm[j], cd_w_ukv[j],
                          cd_w_out[j], rope, batch=batch, seq=seq)
        x = _ffn(x, ffn_norm[i, 1], w_gu, w_down, i, 1, tm=1024, tf=512)
        x = _ple(x, p.reshape(depth * batch * seq, -1), i, ple_norm[i], ple_w_gate[i].astype(BF16),
                 ple_w_proj[i].astype(BF16), final_norm, tm=512, final_norm=(i == depth - 1))
    return x.reshape(batch, seq, d)
```

```python
import functools
import math

import jax
import jax.numpy as jnp
from jax import lax
from jax.experimental import pallas as pl
from jax.experimental.pallas import tpu as pltpu

F32 = jnp.float32
BF16 = jnp.bfloat16

D_MODEL = 2048
D_FF = 5632
PLE_DIM = 256
HEAD_DIM = 128
ROPE_THETA = 10000.0
NORM_EPS = 1e-6

A_HEADS = 8
A_QK_DIM = 64
B_GROUPS = ((128, 1), (512, 4), (2048, 16))
B_HEADS_PER_GROUP = 4
B_HEADS = B_HEADS_PER_GROUP * len(B_GROUPS)
C_HEADS = 8
C_Q_LORA = 512
C_KV_LORA = 256
C_NOPE = 128
C_ROPE = 64
D_HEADS = 8
MOBA_BLOCK = 256
MOBA_TOPK = 3

LOG2_E = 1.0 / math.log(2.0)
PAST_BLOCKS_PER_ITERATION = 4
SUM_ROWS = 16
LANES = 128
VMEM_LIMIT = 56 * 1024 * 1024

PLAIN, ROPE128, ROPE64 = 0, 1, 2


def _rms(x, gain):
    return x * lax.rsqrt(jnp.mean(x * x, axis=-1, keepdims=True) + NORM_EPS) * gain


def _params(*semantics):
    return pltpu.CompilerParams(dimension_semantics=semantics, vmem_limit_bytes=VMEM_LIMIT)


def _norm_proj_body(x_ref, g_ref, w_ref, c128_ref, s128_ref, c64_ref, s64_ref, o_ref, xn_ref, stage_ref, *,
                    tile_patterns):
    j = pl.program_id(1)
    tm = o_ref.shape[0]

    @pl.when(j == 0)
    def _():
        xn_ref[...] = _rms(x_ref[...], g_ref[...]).astype(BF16)

    def tile(pattern):
        acc = jnp.dot(xn_ref[...], w_ref[...], preferred_element_type=F32)
        for c, (kind, scale, dil) in enumerate(pattern):
            a = acc[:, _lanes(c)]
            if scale != 1.0:
                a = a * scale
            if kind == ROPE128:
                a = a * c128_ref[...] + pltpu.roll(a, 64, 1) * s128_ref[...]
            elif kind == ROPE64:
                a = a * c64_ref[...] + pltpu.roll(a, 64, 1) * s64_ref[...]
            if dil == 1:
                o_ref[:, _lanes(c)] = a.astype(o_ref.dtype)
            else:
                rows = tm // dil
                stage_ref[c] = a
                for r in range(dil):
                    o_ref[r * rows:(r + 1) * rows, _lanes(c)] = (
                        stage_ref[c, pl.ds(r, rows, stride=dil), :].astype(o_ref.dtype))

    distinct = sorted(set(tile_patterns), key=tile_patterns.index)
    if len(distinct) == 1:
        tile(distinct[0])
    else:
        for pattern in distinct:
            tiles = [t for t, p in enumerate(tile_patterns) if p == pattern]
            cond = functools.reduce(jnp.logical_or, [j == t for t in tiles])
            pl.when(cond)(functools.partial(tile, pattern))


def _norm_proj(x, gain, w, rope, chunk_modes, *, seq, tm, tn, out_dtype):
    t, k = x.shape
    n = w.shape[1]
    per_tile = tn // LANES
    assert t % tm == 0 and n % tn == 0 and seq % tm == 0 and len(chunk_modes) * LANES == n
    tile_patterns = tuple(tuple(chunk_modes[a * per_tile:(a + 1) * per_tile]) for a in range(n // tn))
    seq_tiles = seq // tm
    tab = pl.BlockSpec((tm, LANES), lambda i, j: (i % seq_tiles, 0))
    return pl.pallas_call(
        functools.partial(_norm_proj_body, tile_patterns=tile_patterns),
        grid=(t // tm, n // tn),
        in_specs=[pl.BlockSpec((tm, k), lambda i, j: (i, 0)),
                  pl.BlockSpec((1, k), lambda i, j: (0, 0)),
                  pl.BlockSpec((k, tn), lambda i, j: (0, j)),
                  tab, tab, tab, tab],
        out_specs=pl.BlockSpec((tm, tn), lambda i, j: (i, j)),
        out_shape=jax.ShapeDtypeStruct((t, n), out_dtype),
        scratch_shapes=[pltpu.VMEM((tm, k), BF16), pltpu.VMEM((per_tile, tm, LANES), F32)],
        compiler_params=_params("parallel", "arbitrary"),
        name="norm_proj",
    )(x, gain.reshape(1, k), w, *rope)


def _out_proj_body(a1_ref, a2_ref, w1_ref, w2_ref, res_ref, o_ref):
    o_ref[...] = (res_ref[...] + jnp.dot(a1_ref[...], w1_ref[...], preferred_element_type=F32)
                  + jnp.dot(a2_ref[...], w2_ref[...], preferred_element_type=F32))


def _out_proj(a1, a2, w, res, *, tm, tn):
    t, k1 = a1.shape
    k2 = a2.shape[1]
    n = w.shape[1]
    assert t % tm == 0 and n % tn == 0 and w.shape[0] == k1 + k2 and k1 % k2 == 0
    return pl.pallas_call(
        _out_proj_body,
        grid=(t // tm, n // tn),
        in_specs=[pl.BlockSpec((tm, k1), lambda i, j: (i, 0)),
                  pl.BlockSpec((tm, k2), lambda i, j: (i, 0)),
                  pl.BlockSpec((k1, tn), lambda i, j: (0, j)),
                  pl.BlockSpec((k2, tn), lambda i, j: (k1 // k2, j)),
                  pl.BlockSpec((tm, tn), lambda i, j: (i, j))],
        out_specs=pl.BlockSpec((tm, tn), lambda i, j: (i, j)),
        out_shape=jax.ShapeDtypeStruct((t, n), F32),
        compiler_params=_params("parallel", "parallel"),
        name="out_proj",
    )(a1, a2, w, w, res)


def _mla_proj_body(x_ref, g_ref, wl_ref, qn_ref, wq_ref, kvn_ref, wkv_ref, c64_ref, s64_ref, q_ref, kv_ref, *,
                   q_lora, q_scale):
    half = x_ref.shape[0] // 2
    for rows in (slice(0, half), slice(half, 2 * half)):
        xn = _rms(x_ref[rows, :], g_ref[...]).astype(BF16)
        latent = jnp.dot(xn, wl_ref[...], preferred_element_type=F32)
        c_q = _rms(latent[:, :q_lora], qn_ref[...]).astype(BF16)
        c_kv = _rms(latent[:, q_lora:], kvn_ref[...]).astype(BF16)
        kv_ref[rows, :] = jnp.dot(c_kv, wkv_ref[...], preferred_element_type=F32).astype(kv_ref.dtype)
        q = jnp.dot(c_q, wq_ref[...], preferred_element_type=F32) * q_scale
        for c in range(q_ref.shape[1] // LANES):
            a = q[:, _lanes(c)]
            if c % 2 == 1:
                a = a * c64_ref[rows, :] + pltpu.roll(a, 64, 1) * s64_ref[rows, :]
            q_ref[rows, _lanes(c)] = a.astype(q_ref.dtype)


def _mla_proj(x, gain, w_latent, q_norm, w_q, kv_norm, w_kv, rope, *, seq, tm, q_scale):
    t, d = x.shape
    q_lora, kv_lora = w_q.shape[0], w_kv.shape[0]
    assert t % tm == 0 and seq % tm == 0 and w_latent.shape[1] == q_lora + kv_lora
    seq_tiles = seq // tm
    const = lambda i: (0, 0)
    tab = pl.BlockSpec((tm, LANES), lambda i: (i % seq_tiles, 0))
    full = lambda a: pl.BlockSpec(a.shape, const)
    out_spec = lambda a: pl.BlockSpec((tm, a.shape[1]), lambda i: (i, 0))
    return pl.pallas_call(
        functools.partial(_mla_proj_body, q_lora=q_lora, q_scale=q_scale),
        grid=(t // tm,),
        in_specs=[pl.BlockSpec((tm, d), lambda i: (i, 0)), pl.BlockSpec((1, d), const), full(w_latent),
                  pl.BlockSpec((1, q_lora), const), full(w_q), pl.BlockSpec((1, kv_lora), const), full(w_kv),
                  tab, tab],
        out_specs=[out_spec(w_q), out_spec(w_kv)],
        out_shape=[jax.ShapeDtypeStruct((t, w_q.shape[1]), BF16), jax.ShapeDtypeStruct((t, w_kv.shape[1]), BF16)],
        compiler_params=_params("parallel"),
        name="mla_proj",
    )(x, gain.reshape(1, d), w_latent, q_norm.reshape(1, q_lora), w_q, kv_norm.reshape(1, kv_lora), w_kv,
      rope[2], rope[3])


def _ffn_body(x_ref, g_ref, wg_ref, wu_ref, wd_ref, o_ref, xn_ref):
    j = pl.program_id(1)

    @pl.when(j == 0)
    def _():
        xn_ref[...] = _rms(x_ref[...], g_ref[...]).astype(BF16)
        o_ref[...] = jnp.zeros_like(o_ref)

    xn = xn_ref[...]
    g = jnp.dot(xn, wg_ref[...], preferred_element_type=F32)
    u = jnp.dot(xn, wu_ref[...], preferred_element_type=F32)
    h = (g * jax.nn.sigmoid(g) * u).astype(BF16)
    o_ref[...] += jnp.dot(h, wd_ref[...], preferred_element_type=F32)

    @pl.when(j == pl.num_programs(1) - 1)
    def _():
        o_ref[...] = x_ref[...] + 0.5 * o_ref[...]


def _ffn(x, gain, w_gu, w_down, layer, half, *, tm, tf):
    t, d = x.shape
    d_ff = w_down.shape[2]
    assert t % tm == 0 and d_ff % tf == 0
    nf = d_ff // tf
    return pl.pallas_call(
        _ffn_body,
        grid=(t // tm, nf),
        in_specs=[pl.BlockSpec((tm, d), lambda i, j: (i, 0)),
                  pl.BlockSpec((1, d), lambda i, j: (0, 0)),
                  pl.BlockSpec((None, None, d, tf), lambda i, j: (layer, half, 0, j)),
                  pl.BlockSpec((None, None, d, tf), lambda i, j: (layer, half, 0, nf + j)),
                  pl.BlockSpec((None, None, tf, d), lambda i, j: (layer, half, j, 0))],
        out_specs=pl.BlockSpec((tm, d), lambda i, j: (i, 0)),
        out_shape=jax.ShapeDtypeStruct((t, d), F32),
        scratch_shapes=[pltpu.VMEM((tm, d), BF16)],
        compiler_params=_params("parallel", "arbitrary"),
        name="ffn",
    )(x, gain.reshape(1, d), w_gu, w_gu, w_down)


def _ple_body(x_ref, p_ref, g_ref, wg_ref, wp_ref, fg_ref, o_ref, *, final_norm):
    half = x_ref.shape[0] // 2
    for rows in (slice(0, half), slice(half, 2 * half)):
        x = x_ref[rows, :]
        xn = _rms(x, g_ref[...]).astype(BF16)
        gate = jax.nn.sigmoid(jnp.dot(xn, wg_ref[...], preferred_element_type=F32))
        proj = jnp.dot(p_ref[rows, :].astype(BF16), wp_ref[...], preferred_element_type=F32)
        y = x + gate * proj
        if final_norm:
            y = _rms(y, fg_ref[...])
        o_ref[rows, :] = y


def _ple(x, p, layer, gain, w_gate, w_proj, final_gain, *, tm, final_norm):
    t, d = x.shape
    pd = p.shape[1]
    assert t % tm == 0
    first_tile = layer * (t // tm)
    const = lambda i: (0, 0)
    return pl.pallas_call(
        functools.partial(_ple_body, final_norm=final_norm),
        grid=(t // tm,),
        in_specs=[pl.BlockSpec((tm, d), lambda i: (i, 0)),
                  pl.BlockSpec((tm, pd), lambda i: (first_tile + i, 0)),
                  pl.BlockSpec((1, d), const),
                  pl.BlockSpec((d, d), const),
                  pl.BlockSpec((pd, d), const),
                  pl.BlockSpec((1, d), const)],
        out_specs=pl.BlockSpec((tm, d), lambda i: (i, 0)),
        out_shape=jax.ShapeDtypeStruct((t, d), F32),
        compiler_params=_params("parallel"),
        name="ple",
    )(x, p, gain.reshape(1, d), w_gate, w_proj, final_gain.reshape(1, d))


def _dot_nt(a, b):
    return lax.dot_general(a, b, (((1,), (1,)), ((), ())), preferred_element_type=F32)


def _lanes(c, width=LANES):
    return slice(c * width, (c + 1) * width)


def _fill_transposed(vt_ref, v_ref, lane_offsets, chunk):
    dv = vt_ref.shape[1] - SUM_ROWS
    ones_row = jnp.where(lax.broadcasted_iota(jnp.int32, (SUM_ROWS, chunk), 0) == 0, 1.0, 0.0).astype(BF16)

    def body(j, _):
        start = pl.multiple_of(j * chunk, chunk)
        for c, off in enumerate(lane_offsets):
            vt_ref[c, :dv, pl.ds(start, chunk)] = v_ref[pl.ds(start, chunk), off:off + dv].T
            vt_ref[c, dv:, pl.ds(start, chunk)] = ones_row
        return 0

    lax.fori_loop(0, v_ref.shape[0] // chunk, body, 0)


def _flash_scratch(heads, tk, cols, dv, seq):
    return [pltpu.VMEM((heads, dv + SUM_ROWS, seq), BF16),
            pltpu.VMEM((heads, tk, cols), F32), pltpu.VMEM((heads, tk, cols), BF16),
            pltpu.VMEM((heads, dv + SUM_ROWS, cols), F32)]


def _causal_diag_masks(tk, tq, cols):
    key_row = lax.broadcasted_iota(jnp.int32, (tk, 1), 0)
    query_col = lax.broadcasted_iota(jnp.int32, (1, cols), 1) % tq
    return [key_row + d * tk <= query_col for d in range(tq // tk)]


def _causal_flash_t(score_fns, vt_ref, s_ref, p_ref, acc_ref, n_past, diag_mask_fn, n_diag, past_mask_fns=None):
    heads = range(len(score_fns))
    _, tk, cols = s_ref.shape

    def stage_scores(block):
        start = pl.multiple_of(block * tk, tk)
        for c in heads:
            s = score_fns[c](start)
            if past_mask_fns is not None:
                s = jnp.where(past_mask_fns[c](block), s, -jnp.inf)
            s_ref[c] = s

    def accumulate(block, alphas):
        start = pl.multiple_of(block * tk, tk)
        for c in heads:
            pv = jnp.dot(vt_ref[c, :, pl.ds(start, tk)], p_ref[c], preferred_element_type=F32)
            acc_ref[c] = alphas[c] * acc_ref[c] + pv

    def softmax(stats, diag=None):
        new_stats, alphas = [], []
        for c in heads:
            m = stats[c]
            if diag is None:
                read = lambda: s_ref[c]
            else:
                visible = diag_mask_fn(diag, c)
                read = lambda: jnp.where(visible, s_ref[c], -jnp.inf)
            m_new = jnp.maximum(m, jnp.max(read(), axis=0, keepdims=True))
            alpha = jnp.exp2(m - m_new)
            p_ref[c] = jnp.exp2(read() - m_new).astype(BF16)
            new_stats.append(m_new)
            alphas.append(alpha)
        return tuple(new_stats), tuple(alphas)

    def past_block(j, carry):
        stats, alphas = carry
        accumulate(jnp.maximum(j - 1, 0), alphas)
        stats, alphas = softmax(stats)
        stage_scores(j + 1)
        return stats, alphas

    for c in heads:
        p_ref[c] = jnp.zeros(p_ref.shape[1:], BF16)
        acc_ref[c] = jnp.zeros(acc_ref.shape[1:], F32)
    stage_scores(0)
    init = (tuple(jnp.full((1, cols), jnp.finfo(F32).min, F32) for _ in heads),
            tuple(jnp.ones((1, cols), F32) for _ in heads))
    def past_group(t, carry):
        for u in range(PAST_BLOCKS_PER_ITERATION):
            carry = past_block(t * PAST_BLOCKS_PER_ITERATION + u, carry)
        return carry

    n_grouped = n_past // PAST_BLOCKS_PER_ITERATION
    carry = lax.fori_loop(0, n_grouped, past_group, init)
    stats, alphas = lax.fori_loop(n_grouped * PAST_BLOCKS_PER_ITERATION, n_past, past_block, carry)

    accumulate(jnp.maximum(n_past - 1, 0), alphas)
    for d in range(n_diag):
        stats, alphas = softmax(stats, diag=d)
        if d + 1 < n_diag:
            stage_scores(n_past + d + 1)
        accumulate(n_past + d, alphas)
    dv = acc_ref.shape[1] - SUM_ROWS
    return tuple((acc_ref[c, dv:dv + 1, :], acc_ref[c, :dv, :]) for c in heads)


def _diff_attn_body(lam_ref, subln_ref, q_ref, k_ref, v_ref, o_ref, qt_ref, vt_ref, s_ref, p_ref, acc_ref, *,
                    tq, tk, heads, lam_init):
    qi = pl.program_id(2)

    @pl.when(qi == 0)
    def _():
        _fill_transposed(vt_ref, v_ref, [c * LANES for c in range(heads)], tk)

    is_map0 = (lax.broadcasted_iota(jnp.int32, (LANES, 1), 0) % 64) < 32

    def make_score_fn(c):
        q_t = q_ref[:, _lanes(c)].T
        zero = jnp.zeros_like(q_t)
        qt_ref[c] = jnp.concatenate([jnp.where(is_map0, q_t, zero), jnp.where(is_map0, zero, q_t)], axis=1)
        return lambda start: jnp.dot(k_ref[pl.ds(start, tk), _lanes(c)], qt_ref[c], preferred_element_type=F32)

    diag = _causal_diag_masks(tk, tq, 2 * tq)
    results = _causal_flash_t([make_score_fn(c) for c in range(heads)], vt_ref, s_ref, p_ref, acc_ref,
                              qi * (tq // tk), lambda d, c: diag[d], tq // tk)

    lp = lam_ref[...]
    lam = (jnp.exp(jnp.sum(lp[0:1] * lp[1:2], axis=-1, keepdims=True))
           - jnp.exp(jnp.sum(lp[2:3] * lp[3:4], axis=-1, keepdims=True)) + lam_init)
    for c, (l, acc) in enumerate(results):
        o_t = acc / l
        o = (o_t[:, :tq] - lam * o_t[:, tq:]).T
        o_ref[:, _lanes(c)] = (_rms(o, subln_ref[...]) * (1.0 - lam_init)).astype(o_ref.dtype)


def _diff_attention(qkv, lam_params, subln, *, batch, seq, tq, tk, heads, q_col, k_col, v_col, lam_init):
    nq = seq // tq
    width = heads * LANES
    assert A_HEADS % heads == 0 and q_col % heads == 0 and k_col % heads == 0 and v_col % heads == 0
    assert seq % tq == 0 and tq % tk == 0
    return pl.pallas_call(
        functools.partial(_diff_attn_body, tq=tq, tk=tk, heads=heads, lam_init=lam_init),
        grid=(batch, A_HEADS // heads, nq),
        in_specs=[pl.BlockSpec((4, A_QK_DIM), lambda b, h, i: (0, 0)),
                  pl.BlockSpec((1, LANES), lambda b, h, i: (0, 0)),
                  pl.BlockSpec((tq, width), lambda b, h, i: (b * nq + i, q_col // heads + h)),
                  pl.BlockSpec((seq, width), lambda b, h, i: (b, k_col // heads + h)),
                  pl.BlockSpec((seq, width), lambda b, h, i: (b, v_col // heads + h))],
        out_specs=pl.BlockSpec((tq, width), lambda b, h, i: (b * nq + i, h)),
        out_shape=jax.ShapeDtypeStruct((batch * seq, A_HEADS * LANES), BF16),
        scratch_shapes=[pltpu.VMEM((heads, LANES, 2 * tq), BF16)] + _flash_scratch(heads, tk, 2 * tq, LANES, seq),
        compiler_params=_params("parallel", "parallel", "arbitrary"),
        name="diff_attn",
    )(lam_params, subln.reshape(1, LANES), qkv, qkv, qkv)


def _tile_index(ref, r, i, tile):
    per_slab = ref.shape[2]
    if per_slab >= tile:
        runs = per_slab // tile
        return i // runs, r, pl.ds(pl.multiple_of((i % runs) * tile, tile), tile)
    slabs = tile // per_slab
    return pl.ds(i * slabs, slabs), r, slice(None)


def _load_tile(ref, r, i, c, tile):
    slab, res, rows = _tile_index(ref, r, i, tile)
    return ref[slab, res, rows, _lanes(c)].reshape(tile, LANES)


def _store_tile(ref, r, i, c, tile, value):
    slab, res, rows = _tile_index(ref, r, i, tile)
    per_slab = ref.shape[2]
    ref[slab, res, rows, _lanes(c)] = (value if per_slab >= tile
                                       else value.reshape(tile // per_slab, per_slab, LANES))


def _dilated_body(q_ref, k_ref, v_ref, o_ref, lse_ref, *, tile, heads):
    slabs, residues, per_slab, _ = q_ref.shape
    row = lax.broadcasted_iota(jnp.int32, (tile, 1), 0)
    streams = [(r, c) for r in range(residues) for c in range(heads)]

    def attend(r, i, c, k, v, valid):
        s = jnp.where(valid, _dot_nt(_load_tile(q_ref, r, i, c, tile), k), -jnp.inf)
        m = jnp.max(s, axis=-1, keepdims=True)
        p = jnp.exp(s - m)
        l = jnp.sum(p, axis=-1, keepdims=True)
        o = jnp.dot(p.astype(BF16), v, preferred_element_type=F32) / l
        _store_tile(o_ref, r, i, c, tile, o)
        _store_tile(lse_ref, r, i, c, tile, jnp.broadcast_to(m + jnp.log(l), (tile, LANES)))

    col = lax.broadcasted_iota(jnp.int32, (1, tile), 1)
    for r, c in streams:
        attend(r, 0, c, _load_tile(k_ref, r, 0, c, tile), _load_tile(v_ref, r, 0, c, tile), col <= row)

    col2 = lax.broadcasted_iota(jnp.int32, (1, 2 * tile), 1)
    band = jnp.logical_and(col2 >= row, col2 <= row + tile)

    def body(i, _):
        for r, c in streams:
            k = jnp.concatenate([_load_tile(k_ref, r, i - 1, c, tile), _load_tile(k_ref, r, i, c, tile)], axis=0)
            v = jnp.concatenate([_load_tile(v_ref, r, i - 1, c, tile), _load_tile(v_ref, r, i, c, tile)], axis=0)
            attend(r, i, c, k, v, band)
        return 0

    n_tiles = slabs * per_slab // tile
    lax.fori_loop(1, n_tiles, body, 0, unroll=max(1, min(4, n_tiles - 1)))


def _dilated_group(qkv, *, batch, seq, row_tile, group, dil, heads, residues, q_col, k_col, v_col):
    hpg = B_HEADS_PER_GROUP
    tile = B_GROUPS[group][0] // dil
    slabs, per_slab = seq // row_tile, row_tile // dil
    assert tile == LANES and hpg % heads == 0 and (seq // dil) % tile == 0 and dil % residues == 0
    assert per_slab % tile == 0 or tile % per_slab == 0
    width = heads * LANES
    view = qkv.reshape(batch, slabs, dil, per_slab, qkv.shape[1])

    def in_spec(col):
        first = col + group * hpg
        assert first % heads == 0
        return pl.BlockSpec((None, slabs, residues, per_slab, width),
                            lambda b, h, r: (b, 0, r, 0, first // heads + h))

    out_spec = pl.BlockSpec((None, slabs, residues, per_slab, width), lambda b, h, r: (b, 0, r, 0, h))
    out_sds = jax.ShapeDtypeStruct((batch, slabs, dil, per_slab, hpg * LANES), F32)
    o, lse = pl.pallas_call(
        functools.partial(_dilated_body, tile=tile, heads=heads),
        grid=(batch, hpg // heads, dil // residues),
        in_specs=[in_spec(q_col), in_spec(k_col), in_spec(v_col)],
        out_specs=[out_spec, out_spec],
        out_shape=[out_sds, out_sds],
        compiler_params=_params("parallel", "parallel", "parallel"),
        name=f"dilated_g{group}",
    )(view, view, view)
    shape = (batch * slabs, dil, per_slab, hpg * LANES)
    return o.reshape(shape), lse.reshape(shape)


def _merge_body(o0_ref, o1_ref, o2_ref, l0_ref, l1_ref, l2_ref, out_ref, nat_ref):
    def natural(ref, slot, c):
        dil, rows = ref.shape[0], ref.shape[1]
        if dil == 1:
            return ref[0, :, _lanes(c)]
        for r in range(dil):
            nat_ref[slot, pl.ds(r, rows, stride=dil), :] = ref[r, :, _lanes(c)]
        return nat_ref[slot]

    for c in range(out_ref.shape[1] // LANES):
        o0, o1, o2 = natural(o0_ref, None, c), natural(o1_ref, 0, c), natural(o2_ref, 1, c)
        l0, l1, l2 = natural(l0_ref, None, c), natural(l1_ref, 2, c), natural(l2_ref, 3, c)
        m = jnp.maximum(jnp.maximum(l0, l1), l2)
        w0, w1, w2 = jnp.exp(l0 - m), jnp.exp(l1 - m), jnp.exp(l2 - m)
        out_ref[:, _lanes(c)] = ((w0 * o0 + w1 * o1 + w2 * o2) / (w0 + w1 + w2)).astype(out_ref.dtype)


def _merge_groups(outs, lses):
    tiles, _, _, width = outs[0].shape
    row_tile = outs[0].shape[1] * outs[0].shape[2]
    specs = [pl.BlockSpec((None,) + a.shape[1:], lambda i: (i, 0, 0, 0)) for a in (*outs, *lses)]
    return pl.pallas_call(
        _merge_body,
        grid=(tiles,),
        in_specs=specs,
        out_specs=pl.BlockSpec((row_tile, width), lambda i: (i, 0)),
        out_shape=jax.ShapeDtypeStruct((tiles * row_tile, width), BF16),
        scratch_shapes=[pltpu.VMEM((4, row_tile, LANES), F32)],
        compiler_params=_params("parallel"),
        name="dilated_merge",
    )(*outs, *lses)


def _mla_body(q_ref, kv_ref, kr_ref, o_ref, qt_ref, vt_ref, s_ref, p_ref, acc_ref, *, tq, tk, heads):
    qi = pl.program_id(2)

    @pl.when(qi == 0)
    def _():
        _fill_transposed(vt_ref, kv_ref, [(2 * c + 1) * LANES for c in range(heads)], tk)

    def make_score_fn(c):
        qt_ref[c] = q_ref[:, _lanes(c, 2 * LANES)].T

        def score(start):
            k = jnp.concatenate([kv_ref[pl.ds(start, tk), _lanes(2 * c)], kr_ref[pl.ds(start, tk), :]], axis=1)
            return jnp.dot(k, qt_ref[c], preferred_element_type=F32)

        return score

    diag = _causal_diag_masks(tk, tq, tq)
    results = _causal_flash_t([make_score_fn(c) for c in range(heads)], vt_ref, s_ref, p_ref, acc_ref,
                              qi * (tq // tk), lambda d, c: diag[d], tq // tk)
    for c, (l, acc) in enumerate(results):
        o_ref[:, _lanes(c)] = (acc / l).T.astype(o_ref.dtype)


def _mla_attention(q, kv, k_rope_src, *, batch, seq, tq, tk, heads, k_rope_col):
    nq = seq // tq
    assert C_HEADS % heads == 0 and seq % tq == 0 and tq % tk == 0
    return pl.pallas_call(
        functools.partial(_mla_body, tq=tq, tk=tk, heads=heads),
        grid=(batch, C_HEADS // heads, nq),
        in_specs=[pl.BlockSpec((tq, heads * 2 * LANES), lambda b, h, i: (b * nq + i, h)),
                  pl.BlockSpec((seq, heads * 2 * LANES), lambda b, h, i: (b, h)),
                  pl.BlockSpec((seq, LANES), lambda b, h, i: (b, k_rope_col))],
        out_specs=pl.BlockSpec((tq, heads * LANES), lambda b, h, i: (b * nq + i, h)),
        out_shape=jax.ShapeDtypeStruct((batch * seq, C_HEADS * LANES), BF16),
        scratch_shapes=[pltpu.VMEM((heads, 2 * LANES, tq), BF16)] + _flash_scratch(heads, tk, tq, LANES, seq),
        compiler_params=_params("parallel", "parallel", "arbitrary"),
        name="mla_attn",
    )(q, kv, k_rope_src)


def _moba_body(q_ref, k_ref, v_ref, o_ref, kmean_ref, sel_ref, qt_ref, vt_ref, s_ref, p_ref, acc_ref, *,
               n_blk, tq, heads):
    qi = pl.program_id(2)
    blk = MOBA_BLOCK
    n_own = tq // blk

    @pl.when(qi == 0)
    def _():
        for c in range(heads):
            k_all = k_ref[:, _lanes(c)].astype(F32).reshape(n_blk, blk, LANES)
            kmean_ref[c] = jnp.mean(k_all, axis=1)
        _fill_transposed(vt_ref, v_ref, [c * LANES for c in range(heads)], blk)

    blk_id = lax.broadcasted_iota(jnp.int32, (n_blk, 1), 0)
    query_col = lax.broadcasted_iota(jnp.int32, (1, tq), 1)
    own_in_tile = query_col // blk
    past = blk_id < qi * n_own + own_in_tile
    causal = lax.broadcasted_iota(jnp.int32, (blk, 1), 0) <= query_col % blk

    def make_score_fn(c):
        q_t = q_ref[:, _lanes(c)].T
        km = kmean_ref[c]
        km_hi = km.astype(BF16)
        rem = km - km_hi.astype(F32)
        km_mid = rem.astype(BF16)
        km_lo = (rem - km_mid.astype(F32)).astype(BF16)
        gate = (jnp.dot(km_hi, q_t, preferred_element_type=F32) + jnp.dot(km_mid, q_t, preferred_element_type=F32)
                + jnp.dot(km_lo, q_t, preferred_element_type=F32))
        gate = jnp.where(past, gate, -jnp.inf)
        selected = jnp.zeros(gate.shape, F32)
        for _ in range(min(MOBA_TOPK, n_blk - 1)):
            best = jnp.max(gate, axis=0, keepdims=True)
            first = jnp.min(jnp.where(gate == best, blk_id, n_blk), axis=0, keepdims=True)
            pick = blk_id == first
            selected = jnp.where(jnp.logical_and(pick, past), 1.0, selected)
            gate = jnp.where(pick, -jnp.inf, gate)
        sel_ref[c] = selected
        qt_ref[c] = q_t
        return lambda start: jnp.dot(k_ref[pl.ds(start, blk), _lanes(c)], qt_ref[c], preferred_element_type=F32)

    score_fns = [make_score_fn(c) for c in range(heads)]
    n_past = qi * n_own
    chosen_fns = [lambda n, c=c: jnp.logical_or(sel_ref[c, pl.ds(n, 1), :] > 0.0, n >= n_past)
                  for c in range(heads)]

    def own_block_mask(d, c):
        chosen = sel_ref[c, pl.ds(n_past + d, 1), :] > 0.0
        return jnp.logical_or(jnp.logical_and(own_in_tile == d, causal),
                              jnp.logical_and(own_in_tile > d, chosen))

    results = _causal_flash_t(score_fns, vt_ref, s_ref, p_ref, acc_ref, n_past, own_block_mask, n_own,
                              past_mask_fns=chosen_fns)
    for c, (l, acc) in enumerate(results):
        o_ref[:, _lanes(c)] = (acc / l).T.astype(o_ref.dtype)


def _moba_attention(qkv, *, batch, seq, tq, heads, q_col, k_col, v_col):
    blk = MOBA_BLOCK
    n_blk = seq // blk
    nq = seq // tq
    width = heads * LANES
    assert seq % tq == 0 and tq % blk == 0 and D_HEADS % heads == 0
    assert q_col % heads == 0 and k_col % heads == 0 and v_col % heads == 0
    return pl.pallas_call(
        functools.partial(_moba_body, n_blk=n_blk, tq=tq, heads=heads),
        grid=(batch, D_HEADS // heads, nq),
        in_specs=[pl.BlockSpec((tq, width), lambda b, h, i: (b * nq + i, q_col // heads + h)),
                  pl.BlockSpec((seq, width), lambda b, h, i: (b, k_col // heads + h)),
                  pl.BlockSpec((seq, width), lambda b, h, i: (b, v_col // heads + h))],
        out_specs=pl.BlockSpec((tq, width), lambda b, h, i: (b * nq + i, h)),
        out_shape=jax.ShapeDtypeStruct((batch * seq, D_HEADS * LANES), BF16),
        scratch_shapes=[pltpu.VMEM((heads, n_blk, LANES), F32),
                        pltpu.VMEM((heads, n_blk, tq), F32),
                        pltpu.VMEM((heads, LANES, tq), BF16)] + _flash_scratch(heads, blk, tq, LANES, seq),
        compiler_params=_params("parallel", "parallel", "arbitrary"),
        name="moba_attn",
    )(qkv, qkv, qkv)


def _rope_tables(seq):
    pos = jnp.arange(seq, dtype=F32)[:, None]

    def angles(dim):
        inv = 1.0 / (ROPE_THETA ** (jnp.arange(0, dim, 2, dtype=F32) / dim))
        return pos * inv[None, :]

    a128 = angles(HEAD_DIM)
    a64 = angles(A_QK_DIM)
    c128 = jnp.concatenate([jnp.cos(a128)] * 2, axis=1)
    s128 = jnp.concatenate([-jnp.sin(a128), jnp.sin(a128)], axis=1)
    c64 = jnp.concatenate([jnp.cos(a64)] * 4, axis=1)
    s64 = jnp.concatenate([-jnp.sin(a64)] * 2 + [jnp.sin(a64)] * 2, axis=1)
    return c128, s128, c64, s64


def _interleave_diff_heads(w):
    d = w.shape[0]
    return w.reshape(d, A_HEADS, 2, 2, 32).transpose(0, 1, 3, 2, 4).reshape(d, A_HEADS * LANES)


def _spread_rope64(w):
    z = jnp.zeros(w.shape[:-1] + (32,), w.dtype)
    return jnp.concatenate([w[..., :32], z, w[..., 32:], z], axis=-1)


def _ab_mixer(x, gain, w_in, lam_params, subln, w_out, layer_idx, rope, *, batch, seq):
    aw = A_HEADS * LANES
    w = jnp.concatenate([_interleave_diff_heads(w_in[:, :aw]), _interleave_diff_heads(w_in[:, aw:2 * aw]),
                         w_in[:, 2 * aw:]], axis=1).astype(BF16)
    a_scale = A_QK_DIM ** -0.5 * LOG2_E
    b_scale = HEAD_DIM ** -0.5
    dils = [dil for _, dil in B_GROUPS for _ in range(B_HEADS_PER_GROUP)]
    modes = ([(ROPE64, a_scale, 1)] * A_HEADS + [(ROPE64, 1.0, 1)] * A_HEADS + [(PLAIN, 1.0, 1)] * A_HEADS
             + [(ROPE128, b_scale, d) for d in dils] + [(ROPE128, 1.0, d) for d in dils]
             + [(PLAIN, 1.0, d) for d in dils])
    row_tile = 1024
    qkv = _norm_proj(x, gain, w, rope, modes, seq=seq, tm=row_tile, tn=1280, out_dtype=BF16)
    lam_init = 0.8 - 0.6 * math.exp(-0.3 * layer_idx)
    oa = _diff_attention(qkv, lam_params, subln, batch=batch, seq=seq, tq=512, tk=256, heads=2,
                         q_col=0, k_col=A_HEADS, v_col=2 * A_HEADS, lam_init=lam_init)
    b0 = 3 * A_HEADS
    outs, lses = [], []
    for g, (_, dil) in enumerate(B_GROUPS):
        o, lse = _dilated_group(qkv, batch=batch, seq=seq, row_tile=row_tile, group=g, dil=dil,
                                heads=2 if dil == 1 else 4, residues=4 if dil == 16 else 1,
                                q_col=b0, k_col=b0 + B_HEADS, v_col=b0 + 2 * B_HEADS)
        outs.append(o)
        lses.append(lse)
    ob = _merge_groups(outs, lses)
    return _out_proj(oa, ob, w_out.astype(BF16), x, tm=1024, tn=1024)


def _cd_mixer(x, gain, w_in, q_norm, w_uq, kv_norm, w_ukv, w_out, rope, *, batch, seq):
    lat = C_Q_LORA + C_KV_LORA
    d_scale = HEAD_DIM ** -0.5 * LOG2_E
    w_main = jnp.concatenate([w_in[:, lat + C_ROPE:], _spread_rope64(w_in[:, lat:lat + C_ROPE])],
                             axis=1).astype(BF16)
    modes = ([(ROPE128, d_scale, 1)] * D_HEADS + [(ROPE128, 1.0, 1)] * D_HEADS + [(PLAIN, 1.0, 1)] * D_HEADS
             + [(ROPE64, 1.0, 1)])
    main = _norm_proj(x, gain, w_main, rope, modes, seq=seq, tm=1024, tn=5 * LANES, out_dtype=BF16)

    c_scale = (C_NOPE + C_ROPE) ** -0.5 * LOG2_E
    w_q = w_uq.reshape(C_Q_LORA, C_HEADS, C_NOPE + C_ROPE)
    w_q = jnp.concatenate([w_q[..., :C_NOPE], _spread_rope64(w_q[..., C_NOPE:])], axis=-1)
    w_q = w_q.reshape(C_Q_LORA, C_HEADS * 2 * LANES).astype(BF16)
    qc, kv = _mla_proj(x, gain, w_in[:, :lat].astype(BF16), q_norm, w_q, kv_norm, w_ukv.astype(BF16), rope,
                       seq=seq, tm=512, q_scale=c_scale)
    oc = _mla_attention(qc, kv, main, batch=batch, seq=seq, tq=1024, tk=256, heads=2, k_rope_col=3 * D_HEADS)
    od = _moba_attention(main, batch=batch, seq=seq, tq=512, heads=4,
                         q_col=0, k_col=D_HEADS, v_col=2 * D_HEADS)
    return _out_proj(oc, od, w_out.astype(BF16), x, tm=1024, tn=1024)


def kernel(x, p, ffn_norm, ffn_w_gu, ffn_w_down, mix_norm, ab_w_in, ab_lambda, ab_subln, ab_w_out,
           cd_w_in, cd_q_norm, cd_w_uq, cd_kv_norm, cd_w_ukv, cd_w_out, ple_norm, ple_w_gate,
           ple_w_proj, final_norm):
    batch, seq, d = x.shape
    depth = p.shape[0]
    rope = _rope_tables(seq)
    w_gu, w_down = ffn_w_gu.astype(BF16), ffn_w_down.astype(BF16)
    x = x.reshape(batch * seq, d)
    for i in range(depth):
        j = i // 2
        x = _ffn(x, ffn_norm[i, 0], w_gu, w_down, i, 0, tm=1024, tf=512)
        if i % 2 == 0:
            x = _ab_mixer(x, mix_norm[i], ab_w_in[j], ab_lambda[j], ab_subln[j], ab_w_out[j], i, rope,
                          batch=batch, seq=seq)
        else:
            x = _cd_mixer(x, mix_norm[i], cd_w_in[j], cd_q_norm[j], cd_w_uq[j], cd_kv_norm[j], cd_w_ukv[j],
                          cd_w_out[j], rope, batch=batch, seq=seq)
        x = _ffn(x, ffn_norm[i, 1], w_gu, w_down, i, 1, tm=1024, tf=512)
        x = _ple(x, p.reshape(depth * batch * seq, -1), i, ple_norm[i], ple_w_gate[i].astype(BF16),
                 ple_w_proj[i].astype(BF16), final_norm, tm=512, final_norm=(i == depth - 1))
    return x.reshape(batch, seq, d)
```

```python
import functools
import math

import jax
import jax.numpy as jnp
from jax import lax
from jax.experimental import pallas as pl
from jax.experimental.pallas import tpu as pltpu

F32 = jnp.float32
BF16 = jnp.bfloat16

D_MODEL = 2048
D_FF = 5632
PLE_DIM = 256
HEAD_DIM = 128
ROPE_THETA = 10000.0
NORM_EPS = 1e-6

A_HEADS = 8
A_QK_DIM = 64
B_GROUPS = ((128, 1), (512, 4), (2048, 16))
B_HEADS_PER_GROUP = 4
B_HEADS = B_HEADS_PER_GROUP * len(B_GROUPS)
C_HEADS = 8
C_Q_LORA = 512
C_KV_LORA = 256
C_NOPE = 128
C_ROPE = 64
D_HEADS = 8
MOBA_BLOCK = 256
MOBA_TOPK = 3

LOG2_E = 1.0 / math.log(2.0)
PAST_BLOCKS_PER_ITERATION = 4
SUM_ROWS = 16
LANES = 128
VMEM_LIMIT = 56 * 1024 * 1024

PLAIN, ROPE128, ROPE64 = 0, 1, 2


def _rms(x, gain):
    return x * lax.rsqrt(jnp.mean(x * x, axis=-1, keepdims=True) + NORM_EPS) * gain


def _params(*semantics):
    return pltpu.CompilerParams(dimension_semantics=semantics, vmem_limit_bytes=VMEM_LIMIT)


def _norm_proj_body(x_ref, g_ref, w_ref, c128_ref, s128_ref, c64_ref, s64_ref, o_ref, xn_ref, stage_ref, *,
                    tile_patterns):
    j = pl.program_id(1)
    tm = o_ref.shape[0]

    @pl.when(j == 0)
    def _():
        xn_ref[...] = _rms(x_ref[...], g_ref[...]).astype(BF16)

    def tile(pattern):
        acc = jnp.dot(xn_ref[...], w_ref[...], preferred_element_type=F32)
        for c, (kind, scale, dil) in enumerate(pattern):
            a = acc[:, _lanes(c)]
            if scale != 1.0:
                a = a * scale
            if kind == ROPE128:
                a = a * c128_ref[...] + pltpu.roll(a, 64, 1) * s128_ref[...]
            elif kind == ROPE64:
                a = a * c64_ref[...] + pltpu.roll(a, 64, 1) * s64_ref[...]
            if dil == 1:
                o_ref[:, _lanes(c)] = a.astype(o_ref.dtype)
            else:
                rows = tm // dil
                stage_ref[c] = a
                for r in range(dil):
                    o_ref[r * rows:(r + 1) * rows, _lanes(c)] = (
                        stage_ref[c, pl.ds(r, rows, stride=dil), :].astype(o_ref.dtype))

    distinct = sorted(set(tile_patterns), key=tile_patterns.index)
    if len(distinct) == 1:
        tile(distinct[0])
    else:
        for pattern in distinct:
            tiles = [t for t, p in enumerate(tile_patterns) if p == pattern]
            cond = functools.reduce(jnp.logical_or, [j == t for t in tiles])
            pl.when(cond)(functools.partial(tile, pattern))


def _norm_proj(x, gain, w, rope, chunk_modes, *, seq, tm, tn, out_dtype):
    t, k = x.shape
    n = w.shape[1]
    per_tile = tn // LANES
    assert t % tm == 0 and n % tn == 0 and seq % tm == 0 and len(chunk_modes) * LANES == n
    tile_patterns = tuple(tuple(chunk_modes[a * per_tile:(a + 1) * per_tile]) for a in range(n // tn))
    seq_tiles = seq // tm
    tab = pl.BlockSpec((tm, LANES), lambda i, j: (i % seq_tiles, 0))
    return pl.pallas_call(
        functools.partial(_norm_proj_body, tile_patterns=tile_patterns),
        grid=(t // tm, n // tn),
        in_specs=[pl.BlockSpec((tm, k), lambda i, j: (i, 0)),
                  pl.BlockSpec((1, k), lambda i, j: (0, 0)),
                  pl.BlockSpec((k, tn), lambda i, j: (0, j)),
                  tab, tab, tab, tab],
        out_specs=pl.BlockSpec((tm, tn), lambda i, j: (i, j)),
        out_shape=jax.ShapeDtypeStruct((t, n), out_dtype),
        scratch_shapes=[pltpu.VMEM((tm, k), BF16), pltpu.VMEM((per_tile, tm, LANES), F32)],
        compiler_params=_params("parallel", "arbitrary"),
        name="norm_proj",
    )(x, gain.reshape(1, k), w, *rope)


def _out_proj_body(a1_ref, a2_ref, w1_ref, w2_ref, res_ref, o_ref):
    o_ref[...] = (res_ref[...] + jnp.dot(a1_ref[...], w1_ref[...], preferred_element_type=F32)
                  + jnp.dot(a2_ref[...], w2_ref[...], preferred_element_type=F32))


def _out_proj(a1, a2, w, res, *, tm, tn):
    t, k1 = a1.shape
    k2 = a2.shape[1]
    n = w.shape[1]
    assert t % tm == 0 and n % tn == 0 and w.shape[0] == k1 + k2 and k1 % k2 == 0
    return pl.pallas_call(
        _out_proj_body,
        grid=(t // tm, n // tn),
        in_specs=[pl.BlockSpec((tm, k1), lambda i, j: (i, 0)),
                  pl.BlockSpec((tm, k2), lambda i, j: (i, 0)),
                  pl.BlockSpec((k1, tn), lambda i, j: (0, j)),
                  pl.BlockSpec((k2, tn), lambda i, j: (k1 // k2, j)),
                  pl.BlockSpec((tm, tn), lambda i, j: (i, j))],
        out_specs=pl.BlockSpec((tm, tn), lambda i, j: (i, j)),
        out_shape=jax.ShapeDtypeStruct((t, n), F32),
        compiler_params=_params("parallel", "parallel"),
        name="out_proj",
    )(a1, a2, w, w, res)


def _mla_proj_body(x_ref, g_ref, wl_ref, qn_ref, wq_ref, kvn_ref, wkv_ref, c64_ref, s64_ref, q_ref, kv_ref, kr_ref,
                   *, q_lora, kv_lora, q_scale):
    half = x_ref.shape[0] // 2
    lat = q_lora + kv_lora
    for rows in (slice(0, half), slice(half, 2 * half)):
        xn = _rms(x_ref[rows, :], g_ref[...]).astype(BF16)
        latent = jnp.dot(xn, wl_ref[...], preferred_element_type=F32)
        k_rope = latent[:, lat:]
        kr_ref[rows, :] = (k_rope * c64_ref[rows, :] + pltpu.roll(k_rope, 64, 1) * s64_ref[rows, :]).astype(kr_ref.dtype)
        c_q = _rms(latent[:, :q_lora], qn_ref[...]).astype(BF16)
        c_kv = _rms(latent[:, q_lora:lat], kvn_ref[...]).astype(BF16)
        kv_ref[rows, :] = jnp.dot(c_kv, wkv_ref[...], preferred_element_type=F32).astype(kv_ref.dtype)
        q = jnp.dot(c_q, wq_ref[...], preferred_element_type=F32) * q_scale
        for c in range(q_ref.shape[1] // LANES):
            a = q[:, _lanes(c)]
            if c % 2 == 1:
                a = a * c64_ref[rows, :] + pltpu.roll(a, 64, 1) * s64_ref[rows, :]
            q_ref[rows, _lanes(c)] = a.astype(q_ref.dtype)


def _mla_proj(x, gain, w_latent, q_norm, w_q, kv_norm, w_kv, rope, *, seq, tm, q_scale):
    t, d = x.shape
    q_lora, kv_lora = w_q.shape[0], w_kv.shape[0]
    assert t % tm == 0 and seq % tm == 0 and w_latent.shape[1] == q_lora + kv_lora + LANES
    seq_tiles = seq // tm
    const = lambda i: (0, 0)
    tab = pl.BlockSpec((tm, LANES), lambda i: (i % seq_tiles, 0))
    full = lambda a: pl.BlockSpec(a.shape, const)
    out_spec = lambda width: pl.BlockSpec((tm, width), lambda i: (i, 0))
    widths = (w_q.shape[1], w_kv.shape[1], LANES)
    return pl.pallas_call(
        functools.partial(_mla_proj_body, q_lora=q_lora, kv_lora=kv_lora, q_scale=q_scale),
        grid=(t // tm,),
        in_specs=[pl.BlockSpec((tm, d), lambda i: (i, 0)), pl.BlockSpec((1, d), const), full(w_latent),
                  pl.BlockSpec((1, q_lora), const), full(w_q), pl.BlockSpec((1, kv_lora), const), full(w_kv),
                  tab, tab],
        out_specs=[out_spec(w) for w in widths],
        out_shape=[jax.ShapeDtypeStruct((t, w), BF16) for w in widths],
        compiler_params=_params("parallel"),
        name="mla_proj",
    )(x, gain.reshape(1, d), w_latent, q_norm.reshape(1, q_lora), w_q, kv_norm.reshape(1, kv_lora), w_kv,
      rope[2], rope[3])


def _ffn_body(x_ref, g_ref, wg_ref, wu_ref, wd_ref, o_ref, xn_ref):
    j = pl.program_id(1)

    @pl.when(j == 0)
    def _():
        xn_ref[...] = _rms(x_ref[...], g_ref[...]).astype(BF16)
        o_ref[...] = jnp.zeros_like(o_ref)

    xn = xn_ref[...]
    g = jnp.dot(xn, wg_ref[...], preferred_element_type=F32)
    u = jnp.dot(xn, wu_ref[...], preferred_element_type=F32)
    h = (g * jax.nn.sigmoid(g) * u).astype(BF16)
    o_ref[...] += jnp.dot(h, wd_ref[...], preferred_element_type=F32)

    @pl.when(j == pl.num_programs(1) - 1)
    def _():
        o_ref[...] = x_ref[...] + 0.5 * o_ref[...]


def _ffn(x, gain, w_gu, w_down, layer, half, *, tm, tf):
    t, d = x.shape
    d_ff = w_down.shape[2]
    assert t % tm == 0 and d_ff % tf == 0
    nf = d_ff // tf
    return pl.pallas_call(
        _ffn_body,
        grid=(t // tm, nf),
        in_specs=[pl.BlockSpec((tm, d), lambda i, j: (i, 0)),
                  pl.BlockSpec((1, d), lambda i, j: (0, 0)),
                  pl.BlockSpec((None, None, d, tf), lambda i, j: (layer, half, 0, j)),
                  pl.BlockSpec((None, None, d, tf), lambda i, j: (layer, half, 0, nf + j)),
                  pl.BlockSpec((None, None, tf, d), lambda i, j: (layer, half, j, 0))],
        out_specs=pl.BlockSpec((tm, d), lambda i, j: (i, 0)),
        out_shape=jax.ShapeDtypeStruct((t, d), F32),
        scratch_shapes=[pltpu.VMEM((tm, d), BF16)],
        compiler_params=_params("parallel", "arbitrary"),
        name="ffn",
    )(x, gain.reshape(1, d), w_gu, w_gu, w_down)


def _ple_body(x_ref, p_ref, g_ref, wg_ref, wp_ref, fg_ref, o_ref, *, final_norm):
    half = x_ref.shape[0] // 2
    for rows in (slice(0, half), slice(half, 2 * half)):
        x = x_ref[rows, :]
        xn = _rms(x, g_ref[...]).astype(BF16)
        gate = jax.nn.sigmoid(jnp.dot(xn, wg_ref[...], preferred_element_type=F32))
        proj = jnp.dot(p_ref[rows, :].astype(BF16), wp_ref[...], preferred_element_type=F32)
        y = x + gate * proj
        if final_norm:
            y = _rms(y, fg_ref[...])
        o_ref[rows, :] = y


def _ple(x, p, layer, gain, w_gate, w_proj, final_gain, *, tm, final_norm):
    t, d = x.shape
    pd = p.shape[1]
    assert t % tm == 0
    first_tile = layer * (t // tm)
    const = lambda i: (0, 0)
    return pl.pallas_call(
        functools.partial(_ple_body, final_norm=final_norm),
        grid=(t // tm,),
        in_specs=[pl.BlockSpec((tm, d), lambda i: (i, 0)),
                  pl.BlockSpec((tm, pd), lambda i: (first_tile + i, 0)),
                  pl.BlockSpec((1, d), const),
                  pl.BlockSpec((d, d), const),
                  pl.BlockSpec((pd, d), const),
                  pl.BlockSpec((1, d), const)],
        out_specs=pl.BlockSpec((tm, d), lambda i: (i, 0)),
        out_shape=jax.ShapeDtypeStruct((t, d), F32),
        compiler_params=_params("parallel"),
        name="ple",
    )(x, p, gain.reshape(1, d), w_gate, w_proj, final_gain.reshape(1, d))


def _dot_nt(a, b):
    return lax.dot_general(a, b, (((1,), (1,)), ((), ())), preferred_element_type=F32)


def _lanes(c, width=LANES):
    return slice(c * width, (c + 1) * width)


def _fill_transposed(vt_ref, v_ref, lane_offsets, chunk):
    dv = vt_ref.shape[1] - SUM_ROWS
    ones_row = jnp.where(lax.broadcasted_iota(jnp.int32, (SUM_ROWS, chunk), 0) == 0, 1.0, 0.0).astype(BF16)

    def body(j, _):
        start = pl.multiple_of(j * chunk, chunk)
        for c, off in enumerate(lane_offsets):
            vt_ref[c, :dv, pl.ds(start, chunk)] = v_ref[pl.ds(start, chunk), off:off + dv].T
            vt_ref[c, dv:, pl.ds(start, chunk)] = ones_row
        return 0

    lax.fori_loop(0, v_ref.shape[0] // chunk, body, 0)


def _flash_scratch(heads, tk, cols, dv, seq):
    return [pltpu.VMEM((heads, dv + SUM_ROWS, seq), BF16),
            pltpu.VMEM((heads, tk, cols), F32), pltpu.VMEM((heads, tk, cols), BF16),
            pltpu.VMEM((heads, dv + SUM_ROWS, cols), F32)]


def _causal_diag_masks(tk, tq, cols):
    key_row = lax.broadcasted_iota(jnp.int32, (tk, 1), 0)
    query_col = lax.broadcasted_iota(jnp.int32, (1, cols), 1) % tq
    return [key_row + d * tk <= query_col for d in range(tq // tk)]


def _causal_flash_t(score_fns, vt_ref, s_ref, p_ref, acc_ref, n_past, diag_mask_fn, n_diag, past_mask_fns=None):
    heads = range(len(score_fns))
    _, tk, cols = s_ref.shape

    def stage_scores(block):
        start = pl.multiple_of(block * tk, tk)
        for c in heads:
            s = score_fns[c](start)
            if past_mask_fns is not None:
                s = jnp.where(past_mask_fns[c](block), s, -jnp.inf)
            s_ref[c] = s

    def accumulate(block, alphas):
        start = pl.multiple_of(block * tk, tk)
        for c in heads:
            pv = jnp.dot(vt_ref[c, :, pl.ds(start, tk)], p_ref[c], preferred_element_type=F32)
            acc_ref[c] = alphas[c] * acc_ref[c] + pv

    def softmax(stats, diag=None):
        new_stats, alphas = [], []
        for c in heads:
            m = stats[c]
            if diag is None:
                read = lambda: s_ref[c]
            else:
                visible = diag_mask_fn(diag, c)
                read = lambda: jnp.where(visible, s_ref[c], -jnp.inf)
            m_new = jnp.maximum(m, jnp.max(read(), axis=0, keepdims=True))
            alpha = jnp.exp2(m - m_new)
            p_ref[c] = jnp.exp2(read() - m_new).astype(BF16)
            new_stats.append(m_new)
            alphas.append(alpha)
        return tuple(new_stats), tuple(alphas)

    def past_block(j, carry):
        stats, alphas = carry
        accumulate(jnp.maximum(j - 1, 0), alphas)
        stats, alphas = softmax(stats)
        stage_scores(j + 1)
        return stats, alphas

    for c in heads:
        p_ref[c] = jnp.zeros(p_ref.shape[1:], BF16)
        acc_ref[c] = jnp.zeros(acc_ref.shape[1:], F32)
    stage_scores(0)
    init = (tuple(jnp.full((1, cols), jnp.finfo(F32).min, F32) for _ in heads),
            tuple(jnp.ones((1, cols), F32) for _ in heads))
    def past_group(t, carry):
        for u in range(PAST_BLOCKS_PER_ITERATION):
            carry = past_block(t * PAST_BLOCKS_PER_ITERATION + u, carry)
        return carry

    n_grouped = n_past // PAST_BLOCKS_PER_ITERATION
    carry = lax.fori_loop(0, n_grouped, past_group, init)
    stats, alphas = lax.fori_loop(n_grouped * PAST_BLOCKS_PER_ITERATION, n_past, past_block, carry)

    accumulate(jnp.maximum(n_past - 1, 0), alphas)
    for d in range(n_diag):
        stats, alphas = softmax(stats, diag=d)
        if d + 1 < n_diag:
            stage_scores(n_past + d + 1)
        accumulate(n_past + d, alphas)
    dv = acc_ref.shape[1] - SUM_ROWS
    return tuple((acc_ref[c, dv:dv + 1, :], acc_ref[c, :dv, :]) for c in heads)


def _diff_attn_body(lam_ref, subln_ref, q_ref, k_ref, v_ref, o_ref, qt_ref, vt_ref, s_ref, p_ref, acc_ref, *,
                    tq, tk, heads, lam_init):
    qi = pl.program_id(2)

    @pl.when(qi == 0)
    def _():
        _fill_transposed(vt_ref, v_ref, [c * LANES for c in range(heads)], tk)

    is_map0 = (lax.broadcasted_iota(jnp.int32, (LANES, 1), 0) % 64) < 32

    def make_score_fn(c):
        q_t = q_ref[:, _lanes(c)].T
        zero = jnp.zeros_like(q_t)
        qt_ref[c] = jnp.concatenate([jnp.where(is_map0, q_t, zero), jnp.where(is_map0, zero, q_t)], axis=1)
        return lambda start: jnp.dot(k_ref[pl.ds(start, tk), _lanes(c)], qt_ref[c], preferred_element_type=F32)

    diag = _causal_diag_masks(tk, tq, 2 * tq)
    results = _causal_flash_t([make_score_fn(c) for c in range(heads)], vt_ref, s_ref, p_ref, acc_ref,
                              qi * (tq // tk), lambda d, c: diag[d], tq // tk)

    lp = lam_ref[...]
    lam = (jnp.exp(jnp.sum(lp[0:1] * lp[1:2], axis=-1, keepdims=True))
           - jnp.exp(jnp.sum(lp[2:3] * lp[3:4], axis=-1, keepdims=True)) + lam_init)
    for c, (l, acc) in enumerate(results):
        o_t = acc / l
        o = (o_t[:, :tq] - lam * o_t[:, tq:]).T
        o_ref[:, _lanes(c)] = (_rms(o, subln_ref[...]) * (1.0 - lam_init)).astype(o_ref.dtype)


def _diff_attention(qkv, lam_params, subln, *, batch, seq, tq, tk, heads, q_col, k_col, v_col, lam_init):
    nq = seq // tq
    width = heads * LANES
    assert A_HEADS % heads == 0 and q_col % heads == 0 and k_col % heads == 0 and v_col % heads == 0
    assert seq % tq == 0 and tq % tk == 0
    return pl.pallas_call(
        functools.partial(_diff_attn_body, tq=tq, tk=tk, heads=heads, lam_init=lam_init),
        grid=(batch, A_HEADS // heads, nq),
        in_specs=[pl.BlockSpec((4, A_QK_DIM), lambda b, h, i: (0, 0)),
                  pl.BlockSpec((1, LANES), lambda b, h, i: (0, 0)),
                  pl.BlockSpec((tq, width), lambda b, h, i: (b * nq + i, q_col // heads + h)),
                  pl.BlockSpec((seq, width), lambda b, h, i: (b, k_col // heads + h)),
                  pl.BlockSpec((seq, width), lambda b, h, i: (b, v_col // heads + h))],
        out_specs=pl.BlockSpec((tq, width), lambda b, h, i: (b * nq + i, h)),
        out_shape=jax.ShapeDtypeStruct((batch * seq, A_HEADS * LANES), BF16),
        scratch_shapes=[pltpu.VMEM((heads, LANES, 2 * tq), BF16)] + _flash_scratch(heads, tk, 2 * tq, LANES, seq),
        compiler_params=_params("parallel", "parallel", "arbitrary"),
        name="diff_attn",
    )(lam_params, subln.reshape(1, LANES), qkv, qkv, qkv)


def _tile_index(ref, r, i, tile):
    per_slab = ref.shape[2]
    if per_slab >= tile:
        runs = per_slab // tile
        return i // runs, r, pl.ds(pl.multiple_of((i % runs) * tile, tile), tile)
    slabs = tile // per_slab
    return pl.ds(i * slabs, slabs), r, slice(None)


def _load_tile(ref, r, i, c, tile):
    slab, res, rows = _tile_index(ref, r, i, tile)
    return ref[slab, res, rows, _lanes(c)].reshape(tile, LANES)


def _store_tile(ref, r, i, c, tile, value):
    slab, res, rows = _tile_index(ref, r, i, tile)
    per_slab = ref.shape[2]
    ref[slab, res, rows, _lanes(c)] = (value if per_slab >= tile
                                       else value.reshape(tile // per_slab, per_slab, LANES))


def _dilated_body(q_ref, k_ref, v_ref, o_ref, lse_ref, *, tile, heads):
    slabs, residues, per_slab, _ = q_ref.shape
    row = lax.broadcasted_iota(jnp.int32, (tile, 1), 0)
    streams = [(r, c) for r in range(residues) for c in range(heads)]

    def attend(r, i, c, k, v, valid):
        s = jnp.where(valid, _dot_nt(_load_tile(q_ref, r, i, c, tile), k), -jnp.inf)
        m = jnp.max(s, axis=-1, keepdims=True)
        p = jnp.exp(s - m)
        l = jnp.sum(p, axis=-1, keepdims=True)
        o = jnp.dot(p.astype(BF16), v, preferred_element_type=F32) / l
        _store_tile(o_ref, r, i, c, tile, o)
        _store_tile(lse_ref, r, i, c, tile, jnp.broadcast_to(m + jnp.log(l), (tile, LANES)))

    col = lax.broadcasted_iota(jnp.int32, (1, tile), 1)
    for r, c in streams:
        attend(r, 0, c, _load_tile(k_ref, r, 0, c, tile), _load_tile(v_ref, r, 0, c, tile), col <= row)

    col2 = lax.broadcasted_iota(jnp.int32, (1, 2 * tile), 1)
    band = jnp.logical_and(col2 >= row, col2 <= row + tile)

    def body(i, _):
        for r, c in streams:
            k = jnp.concatenate([_load_tile(k_ref, r, i - 1, c, tile), _load_tile(k_ref, r, i, c, tile)], axis=0)
            v = jnp.concatenate([_load_tile(v_ref, r, i - 1, c, tile), _load_tile(v_ref, r, i, c, tile)], axis=0)
            attend(r, i, c, k, v, band)
        return 0

    n_tiles = slabs * per_slab // tile
    lax.fori_loop(1, n_tiles, body, 0, unroll=max(1, min(4, n_tiles - 1)))


def _dilated_group(qkv, *, batch, seq, row_tile, group, dil, heads, residues, q_col, k_col, v_col):
    hpg = B_HEADS_PER_GROUP
    tile = B_GROUPS[group][0] // dil
    slabs, per_slab = seq // row_tile, row_tile // dil
    assert tile == LANES and hpg % heads == 0 and (seq // dil) % tile == 0 and dil % residues == 0
    assert per_slab % tile == 0 or tile % per_slab == 0
    width = heads * LANES
    view = qkv.reshape(batch, slabs, dil, per_slab, qkv.shape[1])

    def in_spec(col):
        first = col + group * hpg
        assert first % heads == 0
        return pl.BlockSpec((None, slabs, residues, per_slab, width),
                            lambda b, h, r: (b, 0, r, 0, first // heads + h))

    out_spec = pl.BlockSpec((None, slabs, residues, per_slab, width), lambda b, h, r: (b, 0, r, 0, h))
    out_sds = jax.ShapeDtypeStruct((batch, slabs, dil, per_slab, hpg * LANES), F32)
    o, lse = pl.pallas_call(
        functools.partial(_dilated_body, tile=tile, heads=heads),
        grid=(batch, hpg // heads, dil // residues),
        in_specs=[in_spec(q_col), in_spec(k_col), in_spec(v_col)],
        out_specs=[out_spec, out_spec],
        out_shape=[out_sds, out_sds],
        compiler_params=_params("parallel", "parallel", "parallel"),
        name=f"dilated_g{group}",
    )(view, view, view)
    shape = (batch * slabs, dil, per_slab, hpg * LANES)
    return o.reshape(shape), lse.reshape(shape)


def _merge_body(o0_ref, o1_ref, o2_ref, l0_ref, l1_ref, l2_ref, out_ref, nat_ref):
    def natural(ref, slot, c):
        dil, rows = ref.shape[0], ref.shape[1]
        if dil == 1:
            return ref[0, :, _lanes(c)]
        for r in range(dil):
            nat_ref[slot, pl.ds(r, rows, stride=dil), :] = ref[r, :, _lanes(c)]
        return nat_ref[slot]

    for c in range(out_ref.shape[1] // LANES):
        o0, o1, o2 = natural(o0_ref, None, c), natural(o1_ref, 0, c), natural(o2_ref, 1, c)
        l0, l1, l2 = natural(l0_ref, None, c), natural(l1_ref, 2, c), natural(l2_ref, 3, c)
        m = jnp.maximum(jnp.maximum(l0, l1), l2)
        w0, w1, w2 = jnp.exp(l0 - m), jnp.exp(l1 - m), jnp.exp(l2 - m)
        out_ref[:, _lanes(c)] = ((w0 * o0 + w1 * o1 + w2 * o2) / (w0 + w1 + w2)).astype(out_ref.dtype)


def _merge_groups(outs, lses):
    tiles, _, _, width = outs[0].shape
    row_tile = outs[0].shape[1] * outs[0].shape[2]
    specs = [pl.BlockSpec((None,) + a.shape[1:], lambda i: (i, 0, 0, 0)) for a in (*outs, *lses)]
    return pl.pallas_call(
        _merge_body,
        grid=(tiles,),
        in_specs=specs,
        out_specs=pl.BlockSpec((row_tile, width), lambda i: (i, 0)),
        out_shape=jax.ShapeDtypeStruct((tiles * row_tile, width), BF16),
        scratch_shapes=[pltpu.VMEM((4, row_tile, LANES), F32)],
        compiler_params=_params("parallel"),
        name="dilated_merge",
    )(*outs, *lses)


def _mla_body(q_ref, kv_ref, kr_ref, o_ref, qt_ref, vt_ref, s_ref, p_ref, acc_ref, *, tq, tk, heads):
    qi = pl.program_id(2)

    @pl.when(qi == 0)
    def _():
        _fill_transposed(vt_ref, kv_ref, [(2 * c + 1) * LANES for c in range(heads)], tk)

    def make_score_fn(c):
        qt_ref[c] = q_ref[:, _lanes(c, 2 * LANES)].T

        def score(start):
            k = jnp.concatenate([kv_ref[pl.ds(start, tk), _lanes(2 * c)], kr_ref[pl.ds(start, tk), :]], axis=1)
            return jnp.dot(k, qt_ref[c], preferred_element_type=F32)

        return score

    diag = _causal_diag_masks(tk, tq, tq)
    results = _causal_flash_t([make_score_fn(c) for c in range(heads)], vt_ref, s_ref, p_ref, acc_ref,
                              qi * (tq // tk), lambda d, c: diag[d], tq // tk)
    for c, (l, acc) in enumerate(results):
        o_ref[:, _lanes(c)] = (acc / l).T.astype(o_ref.dtype)


def _mla_attention(q, kv, k_rope_src, *, batch, seq, tq, tk, heads, k_rope_col):
    nq = seq // tq
    assert C_HEADS % heads == 0 and seq % tq == 0 and tq % tk == 0
    return pl.pallas_call(
        functools.partial(_mla_body, tq=tq, tk=tk, heads=heads),
        grid=(batch, C_HEADS // heads, nq),
        in_specs=[pl.BlockSpec((tq, heads * 2 * LANES), lambda b, h, i: (b * nq + i, h)),
                  pl.BlockSpec((seq, heads * 2 * LANES), lambda b, h, i: (b, h)),
                  pl.BlockSpec((seq, LANES), lambda b, h, i: (b, k_rope_col))],
        out_specs=pl.BlockSpec((tq, heads * LANES), lambda b, h, i: (b * nq + i, h)),
        out_shape=jax.ShapeDtypeStruct((batch * seq, C_HEADS * LANES), BF16),
        scratch_shapes=[pltpu.VMEM((heads, 2 * LANES, tq), BF16)] + _flash_scratch(heads, tk, tq, LANES, seq),
        compiler_params=_params("parallel", "parallel", "arbitrary"),
        name="mla_attn",
    )(q, kv, k_rope_src)


def _moba_body(q_ref, k_ref, v_ref, o_ref, kmean_ref, sel_ref, qt_ref, vt_ref, s_ref, p_ref, acc_ref, *,
               n_blk, tq, heads):
    qi = pl.program_id(2)
    blk = MOBA_BLOCK
    n_own = tq // blk

    @pl.when(qi == 0)
    def _():
        for c in range(heads):
            k_all = k_ref[:, _lanes(c)].astype(F32).reshape(n_blk, blk, LANES)
            kmean_ref[c] = jnp.mean(k_all, axis=1)
        _fill_transposed(vt_ref, v_ref, [c * LANES for c in range(heads)], blk)

    blk_id = lax.broadcasted_iota(jnp.int32, (n_blk, 1), 0)
    query_col = lax.broadcasted_iota(jnp.int32, (1, tq), 1)
    own_in_tile = query_col // blk
    past = blk_id < qi * n_own + own_in_tile
    causal = lax.broadcasted_iota(jnp.int32, (blk, 1), 0) <= query_col % blk

    def make_score_fn(c):
        q_t = q_ref[:, _lanes(c)].T
        km = kmean_ref[c]
        km_hi = km.astype(BF16)
        rem = km - km_hi.astype(F32)
        km_mid = rem.astype(BF16)
        km_lo = (rem - km_mid.astype(F32)).astype(BF16)
        gate = (jnp.dot(km_hi, q_t, preferred_element_type=F32) + jnp.dot(km_mid, q_t, preferred_element_type=F32)
                + jnp.dot(km_lo, q_t, preferred_element_type=F32))
        gate = jnp.where(past, gate, -jnp.inf)
        selected = jnp.zeros(gate.shape, F32)
        for _ in range(min(MOBA_TOPK, n_blk - 1)):
            best = jnp.max(gate, axis=0, keepdims=True)
            first = jnp.min(jnp.where(gate == best, blk_id, n_blk), axis=0, keepdims=True)
            pick = blk_id == first
            selected = jnp.where(jnp.logical_and(pick, past), 1.0, selected)
            gate = jnp.where(pick, -jnp.inf, gate)
        sel_ref[c] = selected
        qt_ref[c] = q_t
        return lambda start: jnp.dot(k_ref[pl.ds(start, blk), _lanes(c)], qt_ref[c], preferred_element_type=F32)

    score_fns = [make_score_fn(c) for c in range(heads)]
    n_past = qi * n_own
    chosen_fns = [lambda n, c=c: jnp.logical_or(sel_ref[c, pl.ds(n, 1), :] > 0.0, n >= n_past)
                  for c in range(heads)]

    def own_block_mask(d, c):
        chosen = sel_ref[c, pl.ds(n_past + d, 1), :] > 0.0
        return jnp.logical_or(jnp.logical_and(own_in_tile == d, causal),
                              jnp.logical_and(own_in_tile > d, chosen))

    results = _causal_flash_t(score_fns, vt_ref, s_ref, p_ref, acc_ref, n_past, own_block_mask, n_own,
                              past_mask_fns=chosen_fns)
    for c, (l, acc) in enumerate(results):
        o_ref[:, _lanes(c)] = (acc / l).T.astype(o_ref.dtype)


def _moba_attention(qkv, *, batch, seq, tq, heads, q_col, k_col, v_col):
    blk = MOBA_BLOCK
    n_blk = seq // blk
    nq = seq // tq
    width = heads * LANES
    assert seq % tq == 0 and tq % blk == 0 and D_HEADS % heads == 0
    assert q_col % heads == 0 and k_col % heads == 0 and v_col % heads == 0
    return pl.pallas_call(
        functools.partial(_moba_body, n_blk=n_blk, tq=tq, heads=heads),
        grid=(batch, D_HEADS // heads, nq),
        in_specs=[pl.BlockSpec((tq, width), lambda b, h, i: (b * nq + i, q_col // heads + h)),
                  pl.BlockSpec((seq, width), lambda b, h, i: (b, k_col // heads + h)),
                  pl.BlockSpec((seq, width), lambda b, h, i: (b, v_col // heads + h))],
        out_specs=pl.BlockSpec((tq, width), lambda b, h, i: (b * nq + i, h)),
        out_shape=jax.ShapeDtypeStruct((batch * seq, D_HEADS * LANES), BF16),
        scratch_shapes=[pltpu.VMEM((heads, n_blk, LANES), F32),
                        pltpu.VMEM((heads, n_blk, tq), F32),
                        pltpu.VMEM((heads, LANES, tq), BF16)] + _flash_scratch(heads, blk, tq, LANES, seq),
        compiler_params=_params("parallel", "parallel", "arbitrary"),
        name="moba_attn",
    )(qkv, qkv, qkv)


def _rope_tables(seq):
    pos = jnp.arange(seq, dtype=F32)[:, None]

    def angles(dim):
        inv = 1.0 / (ROPE_THETA ** (jnp.arange(0, dim, 2, dtype=F32) / dim))
        return pos * inv[None, :]

    a128 = angles(HEAD_DIM)
    a64 = angles(A_QK_DIM)
    c128 = jnp.concatenate([jnp.cos(a128)] * 2, axis=1)
    s128 = jnp.concatenate([-jnp.sin(a128), jnp.sin(a128)], axis=1)
    c64 = jnp.concatenate([jnp.cos(a64)] * 4, axis=1)
    s64 = jnp.concatenate([-jnp.sin(a64)] * 2 + [jnp.sin(a64)] * 2, axis=1)
    return c128, s128, c64, s64


def _interleave_diff_heads(w):
    d = w.shape[0]
    return w.reshape(d, A_HEADS, 2, 2, 32).transpose(0, 1, 3, 2, 4).reshape(d, A_HEADS * LANES)


def _spread_rope64(w):
    z = jnp.zeros(w.shape[:-1] + (32,), w.dtype)
    return jnp.concatenate([w[..., :32], z, w[..., 32:], z], axis=-1)


def _ab_mixer(x, gain, w_in, lam_params, subln, w_out, layer_idx, rope, *, batch, seq):
    aw = A_HEADS * LANES
    w = jnp.concatenate([_interleave_diff_heads(w_in[:, :aw]), _interleave_diff_heads(w_in[:, aw:2 * aw]),
                         w_in[:, 2 * aw:]], axis=1).astype(BF16)
    a_scale = A_QK_DIM ** -0.5 * LOG2_E
    b_scale = HEAD_DIM ** -0.5
    dils = [dil for _, dil in B_GROUPS for _ in range(B_HEADS_PER_GROUP)]
    modes = ([(ROPE64, a_scale, 1)] * A_HEADS + [(ROPE64, 1.0, 1)] * A_HEADS + [(PLAIN, 1.0, 1)] * A_HEADS
             + [(ROPE128, b_scale, d) for d in dils] + [(ROPE128, 1.0, d) for d in dils]
             + [(PLAIN, 1.0, d) for d in dils])
    row_tile = 1024
    qkv = _norm_proj(x, gain, w, rope, modes, seq=seq, tm=row_tile, tn=1280, out_dtype=BF16)
    lam_init = 0.8 - 0.6 * math.exp(-0.3 * layer_idx)
    oa = _diff_attention(qkv, lam_params, subln, batch=batch, seq=seq, tq=512, tk=256, heads=2,
                         q_col=0, k_col=A_HEADS, v_col=2 * A_HEADS, lam_init=lam_init)
    b0 = 3 * A_HEADS
    outs, lses = [], []
    for g, (_, dil) in enumerate(B_GROUPS):
        o, lse = _dilated_group(qkv, batch=batch, seq=seq, row_tile=row_tile, group=g, dil=dil,
                                heads=2 if dil == 1 else 4, residues=4 if dil == 16 else 1,
                                q_col=b0, k_col=b0 + B_HEADS, v_col=b0 + 2 * B_HEADS)
        outs.append(o)
        lses.append(lse)
    ob = _merge_groups(outs, lses)
    return _out_proj(oa, ob, w_out.astype(BF16), x, tm=1024, tn=1024)


def _cd_mixer(x, gain, w_in, q_norm, w_uq, kv_norm, w_ukv, w_out, rope, *, batch, seq):
    lat = C_Q_LORA + C_KV_LORA
    d_scale = HEAD_DIM ** -0.5 * LOG2_E
    modes = [(ROPE128, d_scale, 1)] * D_HEADS + [(ROPE128, 1.0, 1)] * D_HEADS + [(PLAIN, 1.0, 1)] * D_HEADS
    main = _norm_proj(x, gain, w_in[:, lat + C_ROPE:].astype(BF16), rope, modes, seq=seq, tm=1024, tn=1024,
                      out_dtype=BF16)

    c_scale = (C_NOPE + C_ROPE) ** -0.5 * LOG2_E
    w_q = w_uq.reshape(C_Q_LORA, C_HEADS, C_NOPE + C_ROPE)
    w_q = jnp.concatenate([w_q[..., :C_NOPE], _spread_rope64(w_q[..., C_NOPE:])], axis=-1)
    w_q = w_q.reshape(C_Q_LORA, C_HEADS * 2 * LANES).astype(BF16)
    w_latent = jnp.concatenate([w_in[:, :lat], _spread_rope64(w_in[:, lat:lat + C_ROPE])], axis=1).astype(BF16)
    qc, kv, k_rope = _mla_proj(x, gain, w_latent, q_norm, w_q, kv_norm, w_ukv.astype(BF16), rope,
                               seq=seq, tm=512, q_scale=c_scale)
    oc = _mla_attention(qc, kv, k_rope, batch=batch, seq=seq, tq=1024, tk=256, heads=2, k_rope_col=0)
    od = _moba_attention(main, batch=batch, seq=seq, tq=512, heads=4,
                         q_col=0, k_col=D_HEADS, v_col=2 * D_HEADS)
    return _out_proj(oc, od, w_out.astype(BF16), x, tm=1024, tn=1024)


def kernel(x, p, ffn_norm, ffn_w_gu, ffn_w_down, mix_norm, ab_w_in, ab_lambda, ab_subln, ab_w_out,
           cd_w_in, cd_q_norm, cd_w_uq, cd_kv_norm, cd_w_ukv, cd_w_out, ple_norm, ple_w_gate,
           ple_w_proj, final_norm):
    batch, seq, d = x.shape
    depth = p.shape[0]
    rope = _rope_tables(seq)
    w_gu, w_down = ffn_w_gu.astype(BF16), ffn_w_down.astype(BF16)
    x = x.reshape(batch * seq, d)
    for i in range(depth):
        j = i // 2
        x = _ffn(x, ffn_norm[i, 0], w_gu, w_down, i, 0, tm=1024, tf=512)
        if i % 2 == 0:
            x = _ab_mixer(x, mix_norm[i], ab_w_in[j], ab_lambda[j], ab_subln[j], ab_w_out[j], i, rope,
                          batch=batch, seq=seq)
        else:
            x = _cd_mixer(x, mix_norm[i], cd_w_in[j], cd_q_norm[j], cd_w_uq[j], cd_kv_norm[j], cd_w_ukv[j],
                          cd_w_out[j], rope, batch=batch, seq=seq)
        x = _ffn(x, ffn_norm[i, 1], w_gu, w_down, i, 1, tm=1024, tf=512)
        x = _ple(x, p.reshape(depth * batch * seq, -1), i, ple_norm[i], ple_w_gate[i].astype(BF16),
                 ple_w_proj[i].astype(BF16), final_norm, tm=512, final_norm=(i == depth - 1))
    return x.reshape(batch, seq, d)
```

```python
import functools
import math

import jax
import jax.numpy as jnp
from jax import lax
from jax.experimental import pallas as pl
from jax.experimental.pallas import tpu as pltpu

F32 = jnp.float32
BF16 = jnp.bfloat16

D_MODEL = 2048
D_FF = 5632
PLE_DIM = 256
HEAD_DIM = 128
ROPE_THETA = 10000.0
NORM_EPS = 1e-6

A_HEADS = 8
A_QK_DIM = 64
B_GROUPS = ((128, 1), (512, 4), (2048, 16))
B_HEADS_PER_GROUP = 4
B_HEADS = B_HEADS_PER_GROUP * len(B_GROUPS)
C_HEADS = 8
C_Q_LORA = 512
C_KV_LORA = 256
C_NOPE = 128
C_ROPE = 64
D_HEADS = 8
MOBA_BLOCK = 256
MOBA_TOPK = 3

LOG2_E = 1.0 / math.log(2.0)
PAST_BLOCKS_PER_ITERATION = 4
SUM_ROWS = 16
LANES = 128
VMEM_LIMIT = 56 * 1024 * 1024

PLAIN, ROPE128, ROPE64 = 0, 1, 2


def _rms(x, gain):
    return x * lax.rsqrt(jnp.mean(x * x, axis=-1, keepdims=True) + NORM_EPS) * gain


def _params(*semantics):
    return pltpu.CompilerParams(dimension_semantics=semantics, vmem_limit_bytes=VMEM_LIMIT)


def _norm_proj_body(x_ref, g_ref, w_ref, c128_ref, s128_ref, c64_ref, s64_ref, o_ref, xn_ref, stage_ref, *,
                    tile_patterns):
    j = pl.program_id(1)
    tm = o_ref.shape[0]

    @pl.when(j == 0)
    def _():
        xn_ref[...] = _rms(x_ref[...], g_ref[...]).astype(BF16)

    def tile(pattern):
        acc = jnp.dot(xn_ref[...], w_ref[...], preferred_element_type=F32)
        for c, (kind, scale, dil) in enumerate(pattern):
            a = acc[:, _lanes(c)]
            if scale != 1.0:
                a = a * scale
            if kind == ROPE128:
                a = a * c128_ref[...] + pltpu.roll(a, 64, 1) * s128_ref[...]
            elif kind == ROPE64:
                a = a * c64_ref[...] + pltpu.roll(a, 64, 1) * s64_ref[...]
            if dil == 1:
                o_ref[:, _lanes(c)] = a.astype(o_ref.dtype)
            else:
                rows = tm // dil
                stage_ref[c] = a
                for r in range(dil):
                    o_ref[r * rows:(r + 1) * rows, _lanes(c)] = (
                        stage_ref[c, pl.ds(r, rows, stride=dil), :].astype(o_ref.dtype))

    distinct = sorted(set(tile_patterns), key=tile_patterns.index)
    if len(distinct) == 1:
        tile(distinct[0])
    else:
        for pattern in distinct:
            tiles = [t for t, p in enumerate(tile_patterns) if p == pattern]
            cond = functools.reduce(jnp.logical_or, [j == t for t in tiles])
            pl.when(cond)(functools.partial(tile, pattern))


def _norm_proj(x, gain, w, rope, chunk_modes, *, seq, tm, tn, out_dtype):
    t, k = x.shape
    n = w.shape[1]
    per_tile = tn // LANES
    assert t % tm == 0 and n % tn == 0 and seq % tm == 0 and len(chunk_modes) * LANES == n
    tile_patterns = tuple(tuple(chunk_modes[a * per_tile:(a + 1) * per_tile]) for a in range(n // tn))
    seq_tiles = seq // tm
    tab = pl.BlockSpec((tm, LANES), lambda i, j: (i % seq_tiles, 0))
    return pl.pallas_call(
        functools.partial(_norm_proj_body, tile_patterns=tile_patterns),
        grid=(t // tm, n // tn),
        in_specs=[pl.BlockSpec((tm, k), lambda i, j: (i, 0)),
                  pl.BlockSpec((1, k), lambda i, j: (0, 0)),
                  pl.BlockSpec((k, tn), lambda i, j: (0, j)),
                  tab, tab, tab, tab],
        out_specs=pl.BlockSpec((tm, tn), lambda i, j: (i, j)),
        out_shape=jax.ShapeDtypeStruct((t, n), out_dtype),
        scratch_shapes=[pltpu.VMEM((tm, k), BF16), pltpu.VMEM((per_tile, tm, LANES), F32)],
        compiler_params=_params("parallel", "arbitrary"),
        name="norm_proj",
    )(x, gain.reshape(1, k), w, *rope)


def _out_proj_body(a1_ref, a2_ref, w1_ref, w2_ref, res_ref, o_ref):
    o_ref[...] = (res_ref[...] + jnp.dot(a1_ref[...], w1_ref[...], preferred_element_type=F32)
                  + jnp.dot(a2_ref[...], w2_ref[...], preferred_element_type=F32))


def _out_proj(a1, a2, w, res, *, tm, tn):
    t, k1 = a1.shape
    k2 = a2.shape[1]
    n = w.shape[1]
    assert t % tm == 0 and n % tn == 0 and w.shape[0] == k1 + k2 and k1 % k2 == 0
    return pl.pallas_call(
        _out_proj_body,
        grid=(t // tm, n // tn),
        in_specs=[pl.BlockSpec((tm, k1), lambda i, j: (i, 0)),
                  pl.BlockSpec((tm, k2), lambda i, j: (i, 0)),
                  pl.BlockSpec((k1, tn), lambda i, j: (0, j)),
                  pl.BlockSpec((k2, tn), lambda i, j: (k1 // k2, j)),
                  pl.BlockSpec((tm, tn), lambda i, j: (i, j))],
        out_specs=pl.BlockSpec((tm, tn), lambda i, j: (i, j)),
        out_shape=jax.ShapeDtypeStruct((t, n), F32),
        compiler_params=_params("parallel", "parallel"),
        name="out_proj",
    )(a1, a2, w, w, res)


def _mla_proj_body(x_ref, g_ref, wl_ref, qn_ref, wq_ref, kvn_ref, wkv_ref, c64_ref, s64_ref, q_ref, kv_ref, kr_ref,
                   *, q_lora, kv_lora, q_scale):
    half = x_ref.shape[0] // 2
    lat = q_lora + kv_lora
    for rows in (slice(0, half), slice(half, 2 * half)):
        xn = _rms(x_ref[rows, :], g_ref[...]).astype(BF16)
        latent = jnp.dot(xn, wl_ref[...], preferred_element_type=F32)
        k_rope = latent[:, lat:]
        kr_ref[rows, :] = (k_rope * c64_ref[rows, :] + pltpu.roll(k_rope, 64, 1) * s64_ref[rows, :]).astype(kr_ref.dtype)
        c_q = _rms(latent[:, :q_lora], qn_ref[...]).astype(BF16)
        c_kv = _rms(latent[:, q_lora:lat], kvn_ref[...]).astype(BF16)
        kv_ref[rows, :] = jnp.dot(c_kv, wkv_ref[...], preferred_element_type=F32).astype(kv_ref.dtype)
        q = jnp.dot(c_q, wq_ref[...], preferred_element_type=F32) * q_scale
        for c in range(q_ref.shape[1] // LANES):
            a = q[:, _lanes(c)]
            if c % 2 == 1:
                a = a * c64_ref[rows, :] + pltpu.roll(a, 64, 1) * s64_ref[rows, :]
            q_ref[rows, _lanes(c)] = a.astype(q_ref.dtype)


def _mla_proj(x, gain, w_latent, q_norm, w_q, kv_norm, w_kv, rope, *, seq, tm, q_scale):
    t, d = x.shape
    q_lora, kv_lora = w_q.shape[0], w_kv.shape[0]
    assert t % tm == 0 and seq % tm == 0 and w_latent.shape[1] == q_lora + kv_lora + LANES
    seq_tiles = seq // tm
    const = lambda i: (0, 0)
    tab = pl.BlockSpec((tm, LANES), lambda i: (i % seq_tiles, 0))
    full = lambda a: pl.BlockSpec(a.shape, const)
    out_spec = lambda width: pl.BlockSpec((tm, width), lambda i: (i, 0))
    widths = (w_q.shape[1], w_kv.shape[1], LANES)
    return pl.pallas_call(
        functools.partial(_mla_proj_body, q_lora=q_lora, kv_lora=kv_lora, q_scale=q_scale),
        grid=(t // tm,),
        in_specs=[pl.BlockSpec((tm, d), lambda i: (i, 0)), pl.BlockSpec((1, d), const), full(w_latent),
                  pl.BlockSpec((1, q_lora), const), full(w_q), pl.BlockSpec((1, kv_lora), const), full(w_kv),
                  tab, tab],
        out_specs=[out_spec(w) for w in widths],
        out_shape=[jax.ShapeDtypeStruct((t, w), BF16) for w in widths],
        compiler_params=_params("parallel"),
        name="mla_proj",
    )(x, gain.reshape(1, d), w_latent, q_norm.reshape(1, q_lora), w_q, kv_norm.reshape(1, kv_lora), w_kv,
      rope[2], rope[3])


def _ffn_body(x_ref, g_ref, wg_ref, wu_ref, wd_ref, o_ref, xn_ref):
    j = pl.program_id(1)

    @pl.when(j == 0)
    def _():
        xn_ref[...] = _rms(x_ref[...], g_ref[...]).astype(BF16)
        o_ref[...] = jnp.zeros_like(o_ref)

    xn = xn_ref[...]
    g = jnp.dot(xn, wg_ref[...], preferred_element_type=F32)
    u = jnp.dot(xn, wu_ref[...], preferred_element_type=F32)
    h = (g * jax.nn.sigmoid(g) * u).astype(BF16)
    o_ref[...] += jnp.dot(h, wd_ref[...], preferred_element_type=F32)

    @pl.when(j == pl.num_programs(1) - 1)
    def _():
        o_ref[...] = x_ref[...] + 0.5 * o_ref[...]


def _ffn(x, gain, w_gu, w_down, layer, half, *, tm, tf):
    t, d = x.shape
    d_ff = w_down.shape[2]
    assert t % tm == 0 and d_ff % tf == 0
    nf = d_ff // tf
    return pl.pallas_call(
        _ffn_body,
        grid=(t // tm, nf),
        in_specs=[pl.BlockSpec((tm, d), lambda i, j: (i, 0)),
                  pl.BlockSpec((1, d), lambda i, j: (0, 0)),
                  pl.BlockSpec((None, None, d, tf), lambda i, j: (layer, half, 0, j)),
                  pl.BlockSpec((None, None, d, tf), lambda i, j: (layer, half, 0, nf + j)),
                  pl.BlockSpec((None, None, tf, d), lambda i, j: (layer, half, j, 0))],
        out_specs=pl.BlockSpec((tm, d), lambda i, j: (i, 0)),
        out_shape=jax.ShapeDtypeStruct((t, d), F32),
        scratch_shapes=[pltpu.VMEM((tm, d), BF16)],
        compiler_params=_params("parallel", "arbitrary"),
        name="ffn",
    )(x, gain.reshape(1, d), w_gu, w_gu, w_down)


def _ple_body(x_ref, p_ref, g_ref, wg_ref, wp_ref, fg_ref, o_ref, *, final_norm):
    half = x_ref.shape[0] // 2
    for rows in (slice(0, half), slice(half, 2 * half)):
        x = x_ref[rows, :]
        xn = _rms(x, g_ref[...]).astype(BF16)
        gate = jax.nn.sigmoid(jnp.dot(xn, wg_ref[...], preferred_element_type=F32))
        proj = jnp.dot(p_ref[rows, :].astype(BF16), wp_ref[...], preferred_element_type=F32)
        y = x + gate * proj
        if final_norm:
            y = _rms(y, fg_ref[...])
        o_ref[rows, :] = y


def _ple(x, p, layer, gain, w_gate, w_proj, final_gain, *, tm, final_norm):
    t, d = x.shape
    pd = p.shape[1]
    assert t % tm == 0
    first_tile = layer * (t // tm)
    const = lambda i: (0, 0)
    return pl.pallas_call(
        functools.partial(_ple_body, final_norm=final_norm),
        grid=(t // tm,),
        in_specs=[pl.BlockSpec((tm, d), lambda i: (i, 0)),
                  pl.BlockSpec((tm, pd), lambda i: (first_tile + i, 0)),
                  pl.BlockSpec((1, d), const),
                  pl.BlockSpec((d, d), const),
                  pl.BlockSpec((pd, d), const),
                  pl.BlockSpec((1, d), const)],
        out_specs=pl.BlockSpec((tm, d), lambda i: (i, 0)),
        out_shape=jax.ShapeDtypeStruct((t, d), F32),
        compiler_params=_params("parallel"),
        name="ple",
    )(x, p, gain.reshape(1, d), w_gate, w_proj, final_gain.reshape(1, d))


def _dot_nt(a, b):
    return lax.dot_general(a, b, (((1,), (1,)), ((), ())), preferred_element_type=F32)


def _lanes(c, width=LANES):
    return slice(c * width, (c + 1) * width)


def _fill_transposed(vt_ref, v_ref, lane_offsets, chunk):
    dv = vt_ref.shape[1] - SUM_ROWS
    ones_row = jnp.where(lax.broadcasted_iota(jnp.int32, (SUM_ROWS, chunk), 0) == 0, 1.0, 0.0).astype(BF16)

    def body(j, _):
        start = pl.multiple_of(j * chunk, chunk)
        for c, off in enumerate(lane_offsets):
            vt_ref[c, :dv, pl.ds(start, chunk)] = v_ref[pl.ds(start, chunk), off:off + dv].T
            vt_ref[c, dv:, pl.ds(start, chunk)] = ones_row
        return 0

    lax.fori_loop(0, v_ref.shape[0] // chunk, body, 0)


def _flash_scratch(heads, tk, cols, dv, seq):
    return [pltpu.VMEM((heads, dv + SUM_ROWS, seq), BF16),
            pltpu.VMEM((heads, tk, cols), F32), pltpu.VMEM((heads, tk, cols), BF16),
            pltpu.VMEM((heads, dv + SUM_ROWS, cols), F32)]


def _causal_diag_masks(tk, tq, cols):
    key_row = lax.broadcasted_iota(jnp.int32, (tk, 1), 0)
    query_col = lax.broadcasted_iota(jnp.int32, (1, cols), 1) % tq
    return [key_row + d * tk <= query_col for d in range(tq // tk)]


def _causal_flash_t(score_fns, vt_ref, s_ref, p_ref, acc_ref, n_past, diag_mask_fn, n_diag, past_mask_fns=None):
    heads = range(len(score_fns))
    _, tk, cols = s_ref.shape

    def stage_scores(block):
        start = pl.multiple_of(block * tk, tk)
        for c in heads:
            s = score_fns[c](start)
            if past_mask_fns is not None:
                s = jnp.where(past_mask_fns[c](block), s, -jnp.inf)
            s_ref[c] = s

    def accumulate(block, alphas):
        start = pl.multiple_of(block * tk, tk)
        for c in heads:
            pv = jnp.dot(vt_ref[c, :, pl.ds(start, tk)], p_ref[c], preferred_element_type=F32)
            acc_ref[c] = alphas[c] * acc_ref[c] + pv

    def softmax(stats, diag=None):
        new_stats, alphas = [], []
        for c in heads:
            m = stats[c]
            if diag is None:
                read = lambda: s_ref[c]
            else:
                visible = diag_mask_fn(diag, c)
                read = lambda: jnp.where(visible, s_ref[c], -jnp.inf)
            m_new = jnp.maximum(m, jnp.max(read(), axis=0, keepdims=True))
            alpha = jnp.exp2(m - m_new)
            p_ref[c] = jnp.exp2(read() - m_new).astype(BF16)
            new_stats.append(m_new)
            alphas.append(alpha)
        return tuple(new_stats), tuple(alphas)

    def past_block(j, carry):
        stats, alphas = carry
        accumulate(jnp.maximum(j - 1, 0), alphas)
        stats, alphas = softmax(stats)
        stage_scores(j + 1)
        return stats, alphas

    for c in heads:
        p_ref[c] = jnp.zeros(p_ref.shape[1:], BF16)
        acc_ref[c] = jnp.zeros(acc_ref.shape[1:], F32)
    stage_scores(0)
    init = (tuple(jnp.full((1, cols), jnp.finfo(F32).min, F32) for _ in heads),
            tuple(jnp.ones((1, cols), F32) for _ in heads))
    def past_group(t, carry):
        for u in range(PAST_BLOCKS_PER_ITERATION):
            carry = past_block(t * PAST_BLOCKS_PER_ITERATION + u, carry)
        return carry

    n_grouped = n_past // PAST_BLOCKS_PER_ITERATION
    carry = lax.fori_loop(0, n_grouped, past_group, init)
    stats, alphas = lax.fori_loop(n_grouped * PAST_BLOCKS_PER_ITERATION, n_past, past_block, carry)

    accumulate(jnp.maximum(n_past - 1, 0), alphas)
    for d in range(n_diag):
        stats, alphas = softmax(stats, diag=d)
        if d + 1 < n_diag:
            stage_scores(n_past + d + 1)
        accumulate(n_past + d, alphas)
    dv = acc_ref.shape[1] - SUM_ROWS
    return tuple((acc_ref[c, dv:dv + 1, :], acc_ref[c, :dv, :]) for c in heads)


def _diff_attn_body(lam_ref, subln_ref, q_ref, k_ref, v_ref, o_ref, qt_ref, vt_ref, s_ref, p_ref, acc_ref, *,
                    tq, tk, heads, lam_init):
    qi = pl.program_id(2)

    @pl.when(qi == 0)
    def _():
        _fill_transposed(vt_ref, v_ref, [c * LANES for c in range(heads)], tk)

    is_map0 = (lax.broadcasted_iota(jnp.int32, (LANES, 1), 0) % 64) < 32

    def make_score_fn(c):
        q_t = q_ref[:, _lanes(c)].T
        zero = jnp.zeros_like(q_t)
        qt_ref[c] = jnp.concatenate([jnp.where(is_map0, q_t, zero), jnp.where(is_map0, zero, q_t)], axis=1)
        return lambda start: jnp.dot(k_ref[pl.ds(start, tk), _lanes(c)], qt_ref[c], preferred_element_type=F32)

    diag = _causal_diag_masks(tk, tq, 2 * tq)
    results = _causal_flash_t([make_score_fn(c) for c in range(heads)], vt_ref, s_ref, p_ref, acc_ref,
                              qi * (tq // tk), lambda d, c: diag[d], tq // tk)

    lp = lam_ref[...]
    lam = (jnp.exp(jnp.sum(lp[0:1] * lp[1:2], axis=-1, keepdims=True))
           - jnp.exp(jnp.sum(lp[2:3] * lp[3:4], axis=-1, keepdims=True)) + lam_init)
    for c, (l, acc) in enumerate(results):
        o_t = acc / l
        o = (o_t[:, :tq] - lam * o_t[:, tq:]).T
        o_ref[:, _lanes(c)] = (_rms(o, subln_ref[...]) * (1.0 - lam_init)).astype(o_ref.dtype)


def _diff_attention(qkv, lam_params, subln, *, batch, seq, tq, tk, heads, q_col, k_col, v_col, lam_init):
    nq = seq // tq
    width = heads * LANES
    assert A_HEADS % heads == 0 and q_col % heads == 0 and k_col % heads == 0 and v_col % heads == 0
    assert seq % tq == 0 and tq % tk == 0
    return pl.pallas_call(
        functools.partial(_diff_attn_body, tq=tq, tk=tk, heads=heads, lam_init=lam_init),
        grid=(batch, A_HEADS // heads, nq),
        in_specs=[pl.BlockSpec((4, A_QK_DIM), lambda b, h, i: (0, 0)),
                  pl.BlockSpec((1, LANES), lambda b, h, i: (0, 0)),
                  pl.BlockSpec((tq, width), lambda b, h, i: (b * nq + i, q_col // heads + h)),
                  pl.BlockSpec((seq, width), lambda b, h, i: (b, k_col // heads + h)),
                  pl.BlockSpec((seq, width), lambda b, h, i: (b, v_col // heads + h))],
        out_specs=pl.BlockSpec((tq, width), lambda b, h, i: (b * nq + i, h)),
        out_shape=jax.ShapeDtypeStruct((batch * seq, A_HEADS * LANES), BF16),
        scratch_shapes=[pltpu.VMEM((heads, LANES, 2 * tq), BF16)] + _flash_scratch(heads, tk, 2 * tq, LANES, seq),
        compiler_params=_params("parallel", "parallel", "arbitrary"),
        name="diff_attn",
    )(lam_params, subln.reshape(1, LANES), qkv, qkv, qkv)


def _tile_index(ref, r, i, tile):
    per_slab = ref.shape[2]
    if per_slab >= tile:
        runs = per_slab // tile
        return i // runs, r, pl.ds(pl.multiple_of((i % runs) * tile, tile), tile)
    slabs = tile // per_slab
    return pl.ds(i * slabs, slabs), r, slice(None)


def _load_tile(ref, r, i, c, tile):
    slab, res, rows = _tile_index(ref, r, i, tile)
    return ref[slab, res, rows, _lanes(c)].reshape(tile, LANES)


def _store_tile(ref, r, i, c, tile, value):
    slab, res, rows = _tile_index(ref, r, i, tile)
    per_slab = ref.shape[2]
    ref[slab, res, rows, _lanes(c)] = (value if per_slab >= tile
                                       else value.reshape(tile // per_slab, per_slab, LANES))


def _dilated_body(q_ref, k_ref, v_ref, o_ref, lse_ref, *, tile, heads):
    slabs, residues, per_slab, _ = q_ref.shape
    row = lax.broadcasted_iota(jnp.int32, (tile, 1), 0)
    streams = [(r, c) for r in range(residues) for c in range(heads)]

    def attend(r, i, c, k, v, valid):
        s = jnp.where(valid, _dot_nt(_load_tile(q_ref, r, i, c, tile), k), -jnp.inf)
        m = jnp.max(s, axis=-1, keepdims=True)
        p = jnp.exp(s - m)
        l = jnp.sum(p, axis=-1, keepdims=True)
        o = jnp.dot(p.astype(BF16), v, preferred_element_type=F32) / l
        _store_tile(o_ref, r, i, c, tile, o)
        _store_tile(lse_ref, r, i, c, tile, jnp.broadcast_to(m + jnp.log(l), (tile, LANES)))

    col = lax.broadcasted_iota(jnp.int32, (1, tile), 1)
    for r, c in streams:
        attend(r, 0, c, _load_tile(k_ref, r, 0, c, tile), _load_tile(v_ref, r, 0, c, tile), col <= row)

    col2 = lax.broadcasted_iota(jnp.int32, (1, 2 * tile), 1)
    band = jnp.logical_and(col2 >= row, col2 <= row + tile)

    def body(i, _):
        for r, c in streams:
            k = jnp.concatenate([_load_tile(k_ref, r, i - 1, c, tile), _load_tile(k_ref, r, i, c, tile)], axis=0)
            v = jnp.concatenate([_load_tile(v_ref, r, i - 1, c, tile), _load_tile(v_ref, r, i, c, tile)], axis=0)
            attend(r, i, c, k, v, band)
        return 0

    n_tiles = slabs * per_slab // tile
    lax.fori_loop(1, n_tiles, body, 0, unroll=max(1, min(4, n_tiles - 1)))


def _dilated_group(qkv, *, batch, seq, row_tile, group, dil, heads, residues, q_col, k_col, v_col):
    hpg = B_HEADS_PER_GROUP
    tile = B_GROUPS[group][0] // dil
    slabs, per_slab = seq // row_tile, row_tile // dil
    assert tile == LANES and hpg % heads == 0 and (seq // dil) % tile == 0 and dil % residues == 0
    assert per_slab % tile == 0 or tile % per_slab == 0
    width = heads * LANES
    view = qkv.reshape(batch, slabs, dil, per_slab, qkv.shape[1])

    def in_spec(col):
        first = col + group * hpg
        assert first % heads == 0
        return pl.BlockSpec((None, slabs, residues, per_slab, width),
                            lambda b, h, r: (b, 0, r, 0, first // heads + h))

    out_spec = pl.BlockSpec((None, slabs, residues, per_slab, width), lambda b, h, r: (b, 0, r, 0, h))
    out_sds = jax.ShapeDtypeStruct((batch, slabs, dil, per_slab, hpg * LANES), F32)
    o, lse = pl.pallas_call(
        functools.partial(_dilated_body, tile=tile, heads=heads),
        grid=(batch, hpg // heads, dil // residues),
        in_specs=[in_spec(q_col), in_spec(k_col), in_spec(v_col)],
        out_specs=[out_spec, out_spec],
        out_shape=[out_sds, out_sds],
        compiler_params=_params("parallel", "parallel", "parallel"),
        name=f"dilated_g{group}",
    )(view, view, view)
    shape = (batch * slabs, dil, per_slab, hpg * LANES)
    return o.reshape(shape), lse.reshape(shape)


def _merge_body(o0_ref, o1_ref, o2_ref, l0_ref, l1_ref, l2_ref, out_ref, nat_ref):
    def natural(ref, slot, c):
        dil, rows = ref.shape[0], ref.shape[1]
        if dil == 1:
            return ref[0, :, _lanes(c)]
        for r in range(dil):
            nat_ref[slot, pl.ds(r, rows, stride=dil), :] = ref[r, :, _lanes(c)]
        return nat_ref[slot]

    for c in range(out_ref.shape[1] // LANES):
        o0, o1, o2 = natural(o0_ref, None, c), natural(o1_ref, 0, c), natural(o2_ref, 1, c)
        l0, l1, l2 = natural(l0_ref, None, c), natural(l1_ref, 2, c), natural(l2_ref, 3, c)
        m = jnp.maximum(jnp.maximum(l0, l1), l2)
        w0, w1, w2 = jnp.exp(l0 - m), jnp.exp(l1 - m), jnp.exp(l2 - m)
        out_ref[:, _lanes(c)] = ((w0 * o0 + w1 * o1 + w2 * o2) / (w0 + w1 + w2)).astype(out_ref.dtype)


def _merge_groups(outs, lses):
    tiles, _, _, width = outs[0].shape
    row_tile = outs[0].shape[1] * outs[0].shape[2]
    specs = [pl.BlockSpec((None,) + a.shape[1:], lambda i: (i, 0, 0, 0)) for a in (*outs, *lses)]
    return pl.pallas_call(
        _merge_body,
        grid=(tiles,),
        in_specs=specs,
        out_specs=pl.BlockSpec((row_tile, width), lambda i: (i, 0)),
        out_shape=jax.ShapeDtypeStruct((tiles * row_tile, width), BF16),
        scratch_shapes=[pltpu.VMEM((4, row_tile, LANES), F32)],
        compiler_params=_params("parallel"),
        name="dilated_merge",
    )(*outs, *lses)


def _mla_body(q_ref, kv_ref, kr_ref, o_ref, qt_ref, vt_ref, s_ref, p_ref, acc_ref, *, tq, tk, heads):
    qi = pl.program_id(2)

    @pl.when(qi == 0)
    def _():
        _fill_transposed(vt_ref, kv_ref, [(2 * c + 1) * LANES for c in range(heads)], tk)

    def make_score_fn(c):
        qt_ref[c] = q_ref[:, _lanes(c, 2 * LANES)].T

        def score(start):
            k = jnp.concatenate([kv_ref[pl.ds(start, tk), _lanes(2 * c)], kr_ref[pl.ds(start, tk), :]], axis=1)
            return jnp.dot(k, qt_ref[c], preferred_element_type=F32)

        return score

    diag = _causal_diag_masks(tk, tq, tq)
    results = _causal_flash_t([make_score_fn(c) for c in range(heads)], vt_ref, s_ref, p_ref, acc_ref,
                              qi * (tq // tk), lambda d, c: diag[d], tq // tk)
    for c, (l, acc) in enumerate(results):
        o_ref[:, _lanes(c)] = (acc / l).T.astype(o_ref.dtype)


def _mla_attention(q, kv, k_rope_src, *, batch, seq, tq, tk, heads, k_rope_col):
    nq = seq // tq
    assert C_HEADS % heads == 0 and seq % tq == 0 and tq % tk == 0
    return pl.pallas_call(
        functools.partial(_mla_body, tq=tq, tk=tk, heads=heads),
        grid=(batch, C_HEADS // heads, nq),
        in_specs=[pl.BlockSpec((tq, heads * 2 * LANES), lambda b, h, i: (b * nq + i, h)),
                  pl.BlockSpec((seq, heads * 2 * LANES), lambda b, h, i: (b, h)),
                  pl.BlockSpec((seq, LANES), lambda b, h, i: (b, k_rope_col))],
        out_specs=pl.BlockSpec((tq, heads * LANES), lambda b, h, i: (b * nq + i, h)),
        out_shape=jax.ShapeDtypeStruct((batch * seq, C_HEADS * LANES), BF16),
        scratch_shapes=[pltpu.VMEM((heads, 2 * LANES, tq), BF16)] + _flash_scratch(heads, tk, tq, LANES, seq),
        compiler_params=_params("parallel", "parallel", "arbitrary"),
        name="mla_attn",
    )(q, kv, k_rope_src)


def _moba_body(q_ref, k_ref, v_ref, o_ref, kmean_ref, sel_ref, qt_ref, vt_ref, s_ref, p_ref, acc_ref, *,
               n_blk, tq, heads):
    qi = pl.program_id(2)
    blk = MOBA_BLOCK
    n_own = tq // blk

    @pl.when(qi == 0)
    def _():
        for c in range(heads):
            k_all = k_ref[:, _lanes(c)].astype(F32).reshape(n_blk, blk, LANES)
            kmean_ref[c] = jnp.mean(k_all, axis=1)
        _fill_transposed(vt_ref, v_ref, [c * LANES for c in range(heads)], blk)

    blk_id = lax.broadcasted_iota(jnp.int32, (n_blk, 1), 0)
    query_col = lax.broadcasted_iota(jnp.int32, (1, tq), 1)
    own_in_tile = query_col // blk
    past = blk_id < qi * n_own + own_in_tile
    causal = lax.broadcasted_iota(jnp.int32, (blk, 1), 0) <= query_col % blk

    def make_score_fn(c):
        q_t = q_ref[:, _lanes(c)].T
        km = kmean_ref[c]
        km_hi = km.astype(BF16)
        rem = km - km_hi.astype(F32)
        km_mid = rem.astype(BF16)
        km_lo = (rem - km_mid.astype(F32)).astype(BF16)
        gate = (jnp.dot(km_hi, q_t, preferred_element_type=F32) + jnp.dot(km_mid, q_t, preferred_element_type=F32)
                + jnp.dot(km_lo, q_t, preferred_element_type=F32))
        gate = jnp.where(past, gate, -jnp.inf)
        selected = jnp.zeros(gate.shape, F32)
        for _ in range(min(MOBA_TOPK, n_blk - 1)):
            best = jnp.max(gate, axis=0, keepdims=True)
            first = jnp.min(jnp.where(gate == best, blk_id, n_blk), axis=0, keepdims=True)
            pick = blk_id == first
            selected = jnp.where(jnp.logical_and(pick, past), 1.0, selected)
            gate = jnp.where(pick, -jnp.inf, gate)
        sel_ref[c] = selected
        qt_ref[c] = q_t
        return lambda start: jnp.dot(k_ref[pl.ds(start, blk), _lanes(c)], qt_ref[c], preferred_element_type=F32)

    score_fns = [make_score_fn(c) for c in range(heads)]
    n_past = qi * n_own
    chosen_fns = [lambda n, c=c: jnp.logical_or(sel_ref[c, pl.ds(n, 1), :] > 0.0, n >= n_past)
                  for c in range(heads)]

    def own_block_mask(d, c):
        chosen = sel_ref[c, pl.ds(n_past + d, 1), :] > 0.0
        return jnp.logical_or(jnp.logical_and(own_in_tile == d, causal),
                              jnp.logical_and(own_in_tile > d, chosen))

    results = _causal_flash_t(score_fns, vt_ref, s_ref, p_ref, acc_ref, n_past, own_block_mask, n_own,
                              past_mask_fns=chosen_fns)
    for c, (l, acc) in enumerate(results):
        o_ref[:, _lanes(c)] = (acc / l).T.astype(o_ref.dtype)


def _moba_attention(qkv, *, batch, seq, tq, heads, q_col, k_col, v_col):
    blk = MOBA_BLOCK
    n_blk = seq // blk
    nq = seq // tq
    width = heads * LANES
    assert seq % tq == 0 and tq % blk == 0 and D_HEADS % heads == 0
    assert q_col % heads == 0 and k_col % heads == 0 and v_col % heads == 0
    return pl.pallas_call(
        functools.partial(_moba_body, n_blk=n_blk, tq=tq, heads=heads),
        grid=(batch, D_HEADS // heads, nq),
        in_specs=[pl.BlockSpec((tq, width), lambda b, h, i: (b * nq + i, q_col // heads + h)),
                  pl.BlockSpec((seq, width), lambda b, h, i: (b, k_col // heads + h)),
                  pl.BlockSpec((seq, width), lambda b, h, i: (b, v_col // heads + h))],
        out_specs=pl.BlockSpec((tq, width), lambda b, h, i: (b * nq + i, h)),
        out_shape=jax.ShapeDtypeStruct((batch * seq, D_HEADS * LANES), BF16),
        scratch_shapes=[pltpu.VMEM((heads, n_blk, LANES), F32),
                        pltpu.VMEM((heads, n_blk, tq), F32),
                        pltpu.VMEM((heads, LANES, tq), BF16)] + _flash_scratch(heads, blk, tq, LANES, seq),
        compiler_params=_params("parallel", "parallel", "arbitrary"),
        name="moba_attn",
    )(qkv, qkv, qkv)


def _rope_tables(seq):
    pos = jnp.arange(seq, dtype=F32)[:, None]

    def angles(dim):
        inv = 1.0 / (ROPE_THETA ** (jnp.arange(0, dim, 2, dtype=F32) / dim))
        return pos * inv[None, :]

    a128 = angles(HEAD_DIM)
    a64 = angles(A_QK_DIM)
    c128 = jnp.concatenate([jnp.cos(a128)] * 2, axis=1)
    s128 = jnp.concatenate([-jnp.sin(a128), jnp.sin(a128)], axis=1)
    c64 = jnp.concatenate([jnp.cos(a64)] * 4, axis=1)
    s64 = jnp.concatenate([-jnp.sin(a64)] * 2 + [jnp.sin(a64)] * 2, axis=1)
    return c128, s128, c64, s64


def _interleave_diff_heads(w):
    d = w.shape[0]
    return w.reshape(d, A_HEADS, 2, 2, 32).transpose(0, 1, 3, 2, 4).reshape(d, A_HEADS * LANES)


def _spread_rope64(w):
    z = jnp.zeros(w.shape[:-1] + (32,), w.dtype)
    return jnp.concatenate([w[..., :32], z, w[..., 32:], z], axis=-1)


def _ab_mixer(x, gain, w_in, lam_params, subln, w_out, layer_idx, rope, *, batch, seq):
    aw = A_HEADS * LANES
    w = jnp.concatenate([_interleave_diff_heads(w_in[:, :aw]), _interleave_diff_heads(w_in[:, aw:2 * aw]),
                         w_in[:, 2 * aw:]], axis=1).astype(BF16)
    a_scale = A_QK_DIM ** -0.5 * LOG2_E
    b_scale = HEAD_DIM ** -0.5
    dils = [dil for _, dil in B_GROUPS for _ in range(B_HEADS_PER_GROUP)]
    modes = ([(ROPE64, a_scale, 1)] * A_HEADS + [(ROPE64, 1.0, 1)] * A_HEADS + [(PLAIN, 1.0, 1)] * A_HEADS
             + [(ROPE128, b_scale, d) for d in dils] + [(ROPE128, 1.0, d) for d in dils]
             + [(PLAIN, 1.0, d) for d in dils])
    row_tile = 1024
    qkv = _norm_proj(x, gain, w, rope, modes, seq=seq, tm=row_tile, tn=1280, out_dtype=BF16)
    lam_init = 0.8 - 0.6 * math.exp(-0.3 * layer_idx)
    oa = _diff_attention(qkv, lam_params, subln, batch=batch, seq=seq, tq=1024, tk=256, heads=2,
                         q_col=0, k_col=A_HEADS, v_col=2 * A_HEADS, lam_init=lam_init)
    b0 = 3 * A_HEADS
    outs, lses = [], []
    for g, (_, dil) in enumerate(B_GROUPS):
        o, lse = _dilated_group(qkv, batch=batch, seq=seq, row_tile=row_tile, group=g, dil=dil,
                                heads=2 if dil == 1 else 4, residues=4 if dil == 16 else 1,
                                q_col=b0, k_col=b0 + B_HEADS, v_col=b0 + 2 * B_HEADS)
        outs.append(o)
        lses.append(lse)
    ob = _merge_groups(outs, lses)
    return _out_proj(oa, ob, w_out.astype(BF16), x, tm=1024, tn=1024)


def _cd_mixer(x, gain, w_in, q_norm, w_uq, kv_norm, w_ukv, w_out, rope, *, batch, seq):
    lat = C_Q_LORA + C_KV_LORA
    d_scale = HEAD_DIM ** -0.5 * LOG2_E
    modes = [(ROPE128, d_scale, 1)] * D_HEADS + [(ROPE128, 1.0, 1)] * D_HEADS + [(PLAIN, 1.0, 1)] * D_HEADS
    main = _norm_proj(x, gain, w_in[:, lat + C_ROPE:].astype(BF16), rope, modes, seq=seq, tm=1024, tn=1024,
                      out_dtype=BF16)

    c_scale = (C_NOPE + C_ROPE) ** -0.5 * LOG2_E
    w_q = w_uq.reshape(C_Q_LORA, C_HEADS, C_NOPE + C_ROPE)
    w_q = jnp.concatenate([w_q[..., :C_NOPE], _spread_rope64(w_q[..., C_NOPE:])], axis=-1)
    w_q = w_q.reshape(C_Q_LORA, C_HEADS * 2 * LANES).astype(BF16)
    w_latent = jnp.concatenate([w_in[:, :lat], _spread_rope64(w_in[:, lat:lat + C_ROPE])], axis=1).astype(BF16)
    qc, kv, k_rope = _mla_proj(x, gain, w_latent, q_norm, w_q, kv_norm, w_ukv.astype(BF16), rope,
                               seq=seq, tm=512, q_scale=c_scale)
    oc = _mla_attention(qc, kv, k_rope, batch=batch, seq=seq, tq=1024, tk=256, heads=2, k_rope_col=0)
    od = _moba_attention(main, batch=batch, seq=seq, tq=1024, heads=4,
                         q_col=0, k_col=D_HEADS, v_col=2 * D_HEADS)
    return _out_proj(oc, od, w_out.astype(BF16), x, tm=1024, tn=1024)


def kernel(x, p, ffn_norm, ffn_w_gu, ffn_w_down, mix_norm, ab_w_in, ab_lambda, ab_subln, ab_w_out,
           cd_w_in, cd_q_norm, cd_w_uq, cd_kv_norm, cd_w_ukv, cd_w_out, ple_norm, ple_w_gate,
           ple_w_proj, final_norm):
    batch, seq, d = x.shape
    depth = p.shape[0]
    rope = _rope_tables(seq)
    w_gu, w_down = ffn_w_gu.astype(BF16), ffn_w_down.astype(BF16)
    x = x.reshape(batch * seq, d)
    for i in range(depth):
        j = i // 2
        x = _ffn(x, ffn_norm[i, 0], w_gu, w_down, i, 0, tm=1024, tf=512)
        if i % 2 == 0:
            x = _ab_mixer(x, mix_norm[i], ab_w_in[j], ab_lambda[j], ab_subln[j], ab_w_out[j], i, rope,
                          batch=batch, seq=seq)
        else:
            x = _cd_mixer(x, mix_norm[i], cd_w_in[j], cd_q_norm[j], cd_w_uq[j], cd_kv_norm[j], cd_w_ukv[j],
                          cd_w_out[j], rope, batch=batch, seq=seq)
        x = _ffn(x, ffn_norm[i, 1], w_gu, w_down, i, 1, tm=1024, tf=512)
        x = _ple(x, p.reshape(depth * batch * seq, -1), i, ple_norm[i], ple_w_gate[i].astype(BF16),
                 ple_w_proj[i].astype(BF16), final_norm, tm=512, final_norm=(i == depth - 1))
    return x.reshape(batch, seq, d)
```

```python
import functools
import math

import jax
import jax.numpy as jnp
from jax import lax
from jax.experimental import pallas as pl
from jax.experimental.pallas import tpu as pltpu

F32 = jnp.float32
BF16 = jnp.bfloat16

D_MODEL = 2048
D_FF = 5632
PLE_DIM = 256
HEAD_DIM = 128
ROPE_THETA = 10000.0
NORM_EPS = 1e-6

A_HEADS = 8
A_QK_DIM = 64
B_GROUPS = ((128, 1), (512, 4), (2048, 16))
B_HEADS_PER_GROUP = 4
B_HEADS = B_HEADS_PER_GROUP * len(B_GROUPS)
C_HEADS = 8
C_Q_LORA = 512
C_KV_LORA = 256
C_NOPE = 128
C_ROPE = 64
D_HEADS = 8
MOBA_BLOCK = 256
MOBA_TOPK = 3

LOG2_E = 1.0 / math.log(2.0)
PAST_BLOCKS_PER_ITERATION = 4
SUM_ROWS = 16
LANES = 128
VMEM_LIMIT = 56 * 1024 * 1024

PLAIN, ROPE128, ROPE64 = 0, 1, 2


def _rms(x, gain):
    return x * lax.rsqrt(jnp.mean(x * x, axis=-1, keepdims=True) + NORM_EPS) * gain


def _params(*semantics):
    return pltpu.CompilerParams(dimension_semantics=semantics, vmem_limit_bytes=VMEM_LIMIT)


def _norm_proj_body(x_ref, g_ref, w_ref, c128_ref, s128_ref, c64_ref, s64_ref, o_ref, xn_ref, stage_ref, *,
                    tile_patterns):
    j = pl.program_id(1)
    tm = o_ref.shape[0]

    @pl.when(j == 0)
    def _():
        xn_ref[...] = _rms(x_ref[...], g_ref[...]).astype(BF16)

    def tile(pattern):
        acc = jnp.dot(xn_ref[...], w_ref[...], preferred_element_type=F32)
        for c, (kind, scale, dil) in enumerate(pattern):
            a = acc[:, _lanes(c)]
            if scale != 1.0:
                a = a * scale
            if kind == ROPE128:
                a = a * c128_ref[...] + pltpu.roll(a, 64, 1) * s128_ref[...]
            elif kind == ROPE64:
                a = a * c64_ref[...] + pltpu.roll(a, 64, 1) * s64_ref[...]
            if dil == 1:
                o_ref[:, _lanes(c)] = a.astype(o_ref.dtype)
            else:
                rows = tm // dil
                stage_ref[c] = a
                for r in range(dil):
                    o_ref[r * rows:(r + 1) * rows, _lanes(c)] = (
                        stage_ref[c, pl.ds(r, rows, stride=dil), :].astype(o_ref.dtype))

    distinct = sorted(set(tile_patterns), key=tile_patterns.index)
    if len(distinct) == 1:
        tile(distinct[0])
    else:
        for pattern in distinct:
            tiles = [t for t, p in enumerate(tile_patterns) if p == pattern]
            cond = functools.reduce(jnp.logical_or, [j == t for t in tiles])
            pl.when(cond)(functools.partial(tile, pattern))


def _norm_proj(x, gain, w, rope, chunk_modes, *, seq, tm, tn, out_dtype):
    t, k = x.shape
    n = w.shape[1]
    per_tile = tn // LANES
    assert t % tm == 0 and n % tn == 0 and seq % tm == 0 and len(chunk_modes) * LANES == n
    tile_patterns = tuple(tuple(chunk_modes[a * per_tile:(a + 1) * per_tile]) for a in range(n // tn))
    seq_tiles = seq // tm
    tab = pl.BlockSpec((tm, LANES), lambda i, j: (i % seq_tiles, 0))
    return pl.pallas_call(
        functools.partial(_norm_proj_body, tile_patterns=tile_patterns),
        grid=(t // tm, n // tn),
        in_specs=[pl.BlockSpec((tm, k), lambda i, j: (i, 0)),
                  pl.BlockSpec((1, k), lambda i, j: (0, 0)),
                  pl.BlockSpec((k, tn), lambda i, j: (0, j)),
                  tab, tab, tab, tab],
        out_specs=pl.BlockSpec((tm, tn), lambda i, j: (i, j)),
        out_shape=jax.ShapeDtypeStruct((t, n), out_dtype),
        scratch_shapes=[pltpu.VMEM((tm, k), BF16), pltpu.VMEM((per_tile, tm, LANES), F32)],
        compiler_params=_params("parallel", "arbitrary"),
        name="norm_proj",
    )(x, gain.reshape(1, k), w, *rope)


def _out_proj_body(a1_ref, a2_ref, w1_ref, w2_ref, res_ref, o_ref):
    o_ref[...] = (res_ref[...] + jnp.dot(a1_ref[...], w1_ref[...], preferred_element_type=F32)
                  + jnp.dot(a2_ref[...], w2_ref[...], preferred_element_type=F32))


def _out_proj(a1, a2, w, res, *, tm, tn):
    t, k1 = a1.shape
    k2 = a2.shape[1]
    n = w.shape[1]
    assert t % tm == 0 and n % tn == 0 and w.shape[0] == k1 + k2 and k1 % k2 == 0
    return pl.pallas_call(
        _out_proj_body,
        grid=(t // tm, n // tn),
        in_specs=[pl.BlockSpec((tm, k1), lambda i, j: (i, 0)),
                  pl.BlockSpec((tm, k2), lambda i, j: (i, 0)),
                  pl.BlockSpec((k1, tn), lambda i, j: (0, j)),
                  pl.BlockSpec((k2, tn), lambda i, j: (k1 // k2, j)),
                  pl.BlockSpec((tm, tn), lambda i, j: (i, j))],
        out_specs=pl.BlockSpec((tm, tn), lambda i, j: (i, j)),
        out_shape=jax.ShapeDtypeStruct((t, n), F32),
        compiler_params=_params("parallel", "parallel"),
        name="out_proj",
    )(a1, a2, w, w, res)


def _mla_proj_body(x_ref, g_ref, wl_ref, qn_ref, wq_ref, kvn_ref, wkv_ref, c64_ref, s64_ref, q_ref, kv_ref, kr_ref,
                   *, q_lora, kv_lora, q_scale):
    half = x_ref.shape[0] // 2
    lat = q_lora + kv_lora
    for rows in (slice(0, half), slice(half, 2 * half)):
        xn = _rms(x_ref[rows, :], g_ref[...]).astype(BF16)
        latent = jnp.dot(xn, wl_ref[...], preferred_element_type=F32)
        k_rope = latent[:, lat:]
        kr_ref[rows, :] = (k_rope * c64_ref[rows, :] + pltpu.roll(k_rope, 64, 1) * s64_ref[rows, :]).astype(kr_ref.dtype)
        c_q = _rms(latent[:, :q_lora], qn_ref[...]).astype(BF16)
        c_kv = _rms(latent[:, q_lora:lat], kvn_ref[...]).astype(BF16)
        kv_ref[rows, :] = jnp.dot(c_kv, wkv_ref[...], preferred_element_type=F32).astype(kv_ref.dtype)
        q = jnp.dot(c_q, wq_ref[...], preferred_element_type=F32) * q_scale
        for c in range(q_ref.shape[1] // LANES):
            a = q[:, _lanes(c)]
            if c % 2 == 1:
                a = a * c64_ref[rows, :] + pltpu.roll(a, 64, 1) * s64_ref[rows, :]
            q_ref[rows, _lanes(c)] = a.astype(q_ref.dtype)


def _mla_proj(x, gain, w_latent, q_norm, w_q, kv_norm, w_kv, rope, *, seq, tm, q_scale):
    t, d = x.shape
    q_lora, kv_lora = w_q.shape[0], w_kv.shape[0]
    assert t % tm == 0 and seq % tm == 0 and w_latent.shape[1] == q_lora + kv_lora + LANES
    seq_tiles = seq // tm
    const = lambda i: (0, 0)
    tab = pl.BlockSpec((tm, LANES), lambda i: (i % seq_tiles, 0))
    full = lambda a: pl.BlockSpec(a.shape, const)
    out_spec = lambda width: pl.BlockSpec((tm, width), lambda i: (i, 0))
    widths = (w_q.shape[1], w_kv.shape[1], LANES)
    return pl.pallas_call(
        functools.partial(_mla_proj_body, q_lora=q_lora, kv_lora=kv_lora, q_scale=q_scale),
        grid=(t // tm,),
        in_specs=[pl.BlockSpec((tm, d), lambda i: (i, 0)), pl.BlockSpec((1, d), const), full(w_latent),
                  pl.BlockSpec((1, q_lora), const), full(w_q), pl.BlockSpec((1, kv_lora), const), full(w_kv),
                  tab, tab],
        out_specs=[out_spec(w) for w in widths],
        out_shape=[jax.ShapeDtypeStruct((t, w), BF16) for w in widths],
        compiler_params=_params("parallel"),
        name="mla_proj",
    )(x, gain.reshape(1, d), w_latent, q_norm.reshape(1, q_lora), w_q, kv_norm.reshape(1, kv_lora), w_kv,
      rope[2], rope[3])


def _ffn_body(x_ref, g_ref, wg_ref, wu_ref, wd_ref, o_ref, xn_ref):
    j = pl.program_id(1)

    @pl.when(j == 0)
    def _():
        xn_ref[...] = _rms(x_ref[...], g_ref[...]).astype(BF16)
        o_ref[...] = jnp.zeros_like(o_ref)

    xn = xn_ref[...]
    g = jnp.dot(xn, wg_ref[...], preferred_element_type=F32)
    u = jnp.dot(xn, wu_ref[...], preferred_element_type=F32)
    h = (g * jax.nn.sigmoid(g) * u).astype(BF16)
    o_ref[...] += jnp.dot(h, wd_ref[...], preferred_element_type=F32)

    @pl.when(j == pl.num_programs(1) - 1)
    def _():
        o_ref[...] = x_ref[...] + 0.5 * o_ref[...]


def _ffn(x, gain, w_gu, w_down, layer, half, *, tm, tf):
    t, d = x.shape
    d_ff = w_down.shape[2]
    assert t % tm == 0 and d_ff % tf == 0
    nf = d_ff // tf
    return pl.pallas_call(
        _ffn_body,
        grid=(t // tm, nf),
        in_specs=[pl.BlockSpec((tm, d), lambda i, j: (i, 0)),
                  pl.BlockSpec((1, d), lambda i, j: (0, 0)),
                  pl.BlockSpec((None, None, d, tf), lambda i, j: (layer, half, 0, j)),
                  pl.BlockSpec((None, None, d, tf), lambda i, j: (layer, half, 0, nf + j)),
                  pl.BlockSpec((None, None, tf, d), lambda i, j: (layer, half, j, 0))],
        out_specs=pl.BlockSpec((tm, d), lambda i, j: (i, 0)),
        out_shape=jax.ShapeDtypeStruct((t, d), F32),
        scratch_shapes=[pltpu.VMEM((tm, d), BF16)],
        compiler_params=_params("parallel", "arbitrary"),
        name="ffn",
    )(x, gain.reshape(1, d), w_gu, w_gu, w_down)


def _ple_body(x_ref, p_ref, g_ref, wg_ref, wp_ref, fg_ref, o_ref, *, final_norm):
    half = x_ref.shape[0] // 2
    for rows in (slice(0, half), slice(half, 2 * half)):
        x = x_ref[rows, :]
        xn = _rms(x, g_ref[...]).astype(BF16)
        gate = jax.nn.sigmoid(jnp.dot(xn, wg_ref[...], preferred_element_type=F32))
        proj = jnp.dot(p_ref[rows, :].astype(BF16), wp_ref[...], preferred_element_type=F32)
        y = x + gate * proj
        if final_norm:
            y = _rms(y, fg_ref[...])
        o_ref[rows, :] = y


def _ple(x, p, layer, gain, w_gate, w_proj, final_gain, *, tm, final_norm):
    t, d = x.shape
    pd = p.shape[1]
    assert t % tm == 0
    first_tile = layer * (t // tm)
    const = lambda i: (0, 0)
    return pl.pallas_call(
        functools.partial(_ple_body, final_norm=final_norm),
        grid=(t // tm,),
        in_specs=[pl.BlockSpec((tm, d), lambda i: (i, 0)),
                  pl.BlockSpec((tm, pd), lambda i: (first_tile + i, 0)),
                  pl.BlockSpec((1, d), const),
                  pl.BlockSpec((d, d), const),
                  pl.BlockSpec((pd, d), const),
                  pl.BlockSpec((1, d), const)],
        out_specs=pl.BlockSpec((tm, d), lambda i: (i, 0)),
        out_shape=jax.ShapeDtypeStruct((t, d), F32),
        compiler_params=_params("parallel"),
        name="ple",
    )(x, p, gain.reshape(1, d), w_gate, w_proj, final_gain.reshape(1, d))


def _dot_nt(a, b):
    return lax.dot_general(a, b, (((1,), (1,)), ((), ())), preferred_element_type=F32)


def _lanes(c, width=LANES):
    return slice(c * width, (c + 1) * width)


def _fill_transposed(vt_ref, v_ref, lane_offsets, chunk):
    dv = vt_ref.shape[1] - SUM_ROWS
    ones_row = jnp.where(lax.broadcasted_iota(jnp.int32, (SUM_ROWS, chunk), 0) == 0, 1.0, 0.0).astype(BF16)

    def body(j, _):
        start = pl.multiple_of(j * chunk, chunk)
        for c, off in enumerate(lane_offsets):
            vt_ref[c, :dv, pl.ds(start, chunk)] = v_ref[pl.ds(start, chunk), off:off + dv].T
            vt_ref[c, dv:, pl.ds(start, chunk)] = ones_row
        return 0

    lax.fori_loop(0, v_ref.shape[0] // chunk, body, 0)


def _flash_scratch(heads, tk, cols, dv, seq):
    return [pltpu.VMEM((heads, dv + SUM_ROWS, seq), BF16),
            pltpu.VMEM((heads, tk, cols), F32), pltpu.VMEM((heads, tk, cols), BF16),
            pltpu.VMEM((heads, dv + SUM_ROWS, cols), F32)]


def _causal_diag_masks(tk, tq, cols):
    key_row = lax.broadcasted_iota(jnp.int32, (tk, 1), 0)
    query_col = lax.broadcasted_iota(jnp.int32, (1, cols), 1) % tq
    return [key_row + d * tk <= query_col for d in range(tq // tk)]


def _causal_flash_t(score_fns, vt_ref, s_ref, p_ref, acc_ref, n_past, diag_mask_fn, n_diag, past_mask_fns=None):
    heads = range(len(score_fns))
    _, tk, cols = s_ref.shape

    def stage_scores(block):
        start = pl.multiple_of(block * tk, tk)
        for c in heads:
            s = score_fns[c](start)
            if past_mask_fns is not None:
                s = jnp.where(past_mask_fns[c](block), s, -jnp.inf)
            s_ref[c] = s

    def accumulate(block, alphas):
        start = pl.multiple_of(block * tk, tk)
        for c in heads:
            pv = jnp.dot(vt_ref[c, :, pl.ds(start, tk)], p_ref[c], preferred_element_type=F32)
            acc_ref[c] = alphas[c] * acc_ref[c] + pv

    def softmax(stats, diag=None):
        new_stats, alphas = [], []
        for c in heads:
            m = stats[c]
            if diag is None:
                read = lambda: s_ref[c]
            else:
                visible = diag_mask_fn(diag, c)
                read = lambda: jnp.where(visible, s_ref[c], -jnp.inf)
            m_new = jnp.maximum(m, jnp.max(read(), axis=0, keepdims=True))
            alpha = jnp.exp2(m - m_new)
            p_ref[c] = jnp.exp2(read() - m_new).astype(BF16)
            new_stats.append(m_new)
            alphas.append(alpha)
        return tuple(new_stats), tuple(alphas)

    def past_block(j, carry):
        stats, alphas = carry
        accumulate(jnp.maximum(j - 1, 0), alphas)
        stats, alphas = softmax(stats)
        stage_scores(j + 1)
        return stats, alphas

    for c in heads:
        p_ref[c] = jnp.zeros(p_ref.shape[1:], BF16)
        acc_ref[c] = jnp.zeros(acc_ref.shape[1:], F32)
    stage_scores(0)
    init = (tuple(jnp.full((1, cols), jnp.finfo(F32).min, F32) for _ in heads),
            tuple(jnp.ones((1, cols), F32) for _ in heads))
    def past_group(t, carry):
        for u in range(PAST_BLOCKS_PER_ITERATION):
            carry = past_block(t * PAST_BLOCKS_PER_ITERATION + u, carry)
        return carry

    n_grouped = n_past // PAST_BLOCKS_PER_ITERATION
    carry = lax.fori_loop(0, n_grouped, past_group, init)
    stats, alphas = lax.fori_loop(n_grouped * PAST_BLOCKS_PER_ITERATION, n_past, past_block, carry)

    accumulate(jnp.maximum(n_past - 1, 0), alphas)
    for d in range(n_diag):
        stats, alphas = softmax(stats, diag=d)
        if d + 1 < n_diag:
            stage_scores(n_past + d + 1)
        accumulate(n_past + d, alphas)
    dv = acc_ref.shape[1] - SUM_ROWS
    return tuple((acc_ref[c, dv:dv + 1, :], acc_ref[c, :dv, :]) for c in heads)


def _diff_attn_body(lam_ref, subln_ref, q_ref, k_ref, v_ref, o_ref, qt_ref, vt_ref, s_ref, p_ref, acc_ref, *,
                    tq, tk, heads, lam_init):
    qi = pl.program_id(2)

    @pl.when(qi == 0)
    def _():
        _fill_transposed(vt_ref, v_ref, [c * LANES for c in range(heads)], tk)

    is_map0 = (lax.broadcasted_iota(jnp.int32, (LANES, 1), 0) % 64) < 32

    def make_score_fn(c):
        q_t = q_ref[:, _lanes(c)].T
        zero = jnp.zeros_like(q_t)
        qt_ref[c] = jnp.concatenate([jnp.where(is_map0, q_t, zero), jnp.where(is_map0, zero, q_t)], axis=1)
        return lambda start: jnp.dot(k_ref[pl.ds(start, tk), _lanes(c)], qt_ref[c], preferred_element_type=F32)

    diag = _causal_diag_masks(tk, tq, 2 * tq)
    results = _causal_flash_t([make_score_fn(c) for c in range(heads)], vt_ref, s_ref, p_ref, acc_ref,
                              qi * (tq // tk), lambda d, c: diag[d], tq // tk)

    lp = lam_ref[...]
    lam = (jnp.exp(jnp.sum(lp[0:1] * lp[1:2], axis=-1, keepdims=True))
           - jnp.exp(jnp.sum(lp[2:3] * lp[3:4], axis=-1, keepdims=True)) + lam_init)
    for c, (l, acc) in enumerate(results):
        o_t = acc / l
        o = (o_t[:, :tq] - lam * o_t[:, tq:]).T
        o_ref[:, _lanes(c)] = (_rms(o, subln_ref[...]) * (1.0 - lam_init)).astype(o_ref.dtype)


def _diff_attention(qkv, lam_params, subln, *, batch, seq, tq, tk, heads, q_col, k_col, v_col, lam_init):
    nq = seq // tq
    width = heads * LANES
    assert A_HEADS % heads == 0 and q_col % heads == 0 and k_col % heads == 0 and v_col % heads == 0
    assert seq % tq == 0 and tq % tk == 0
    return pl.pallas_call(
        functools.partial(_diff_attn_body, tq=tq, tk=tk, heads=heads, lam_init=lam_init),
        grid=(batch, A_HEADS // heads, nq),
        in_specs=[pl.BlockSpec((4, A_QK_DIM), lambda b, h, i: (0, 0)),
                  pl.BlockSpec((1, LANES), lambda b, h, i: (0, 0)),
                  pl.BlockSpec((tq, width), lambda b, h, i: (b * nq + i, q_col // heads + h)),
                  pl.BlockSpec((seq, width), lambda b, h, i: (b, k_col // heads + h)),
                  pl.BlockSpec((seq, width), lambda b, h, i: (b, v_col // heads + h))],
        out_specs=pl.BlockSpec((tq, width), lambda b, h, i: (b * nq + i, h)),
        out_shape=jax.ShapeDtypeStruct((batch * seq, A_HEADS * LANES), BF16),
        scratch_shapes=[pltpu.VMEM((heads, LANES, 2 * tq), BF16)] + _flash_scratch(heads, tk, 2 * tq, LANES, seq),
        compiler_params=_params("parallel", "parallel", "arbitrary"),
        name="diff_attn",
    )(lam_params, subln.reshape(1, LANES), qkv, qkv, qkv)


def _tile_index(ref, r, i, tile):
    per_slab = ref.shape[2]
    if per_slab >= tile:
        runs = per_slab // tile
        return i // runs, r, pl.ds(pl.multiple_of((i % runs) * tile, tile), tile)
    slabs = tile // per_slab
    return pl.ds(i * slabs, slabs), r, slice(None)


def _load_tile(ref, r, i, c, tile):
    slab, res, rows = _tile_index(ref, r, i, tile)
    return ref[slab, res, rows, _lanes(c)].reshape(tile, LANES)


def _store_tile(ref, r, i, c, tile, value):
    slab, res, rows = _tile_index(ref, r, i, tile)
    per_slab = ref.shape[2]
    ref[slab, res, rows, _lanes(c)] = (value if per_slab >= tile
                                       else value.reshape(tile // per_slab, per_slab, LANES))


def _dilated_body(q_ref, k_ref, v_ref, o_ref, lse_ref, *, tile, heads):
    slabs, residues, per_slab, _ = q_ref.shape
    row = lax.broadcasted_iota(jnp.int32, (tile, 1), 0)
    streams = [(r, c) for r in range(residues) for c in range(heads)]

    def attend(r, i, c, k, v, valid):
        s = jnp.where(valid, _dot_nt(_load_tile(q_ref, r, i, c, tile), k), -jnp.inf)
        m = jnp.max(s, axis=-1, keepdims=True)
        p = jnp.exp(s - m)
        l = jnp.sum(p, axis=-1, keepdims=True)
        o = jnp.dot(p.astype(BF16), v, preferred_element_type=F32) / l
        _store_tile(o_ref, r, i, c, tile, o)
        _store_tile(lse_ref, r, i, c, tile, jnp.broadcast_to(m + jnp.log(l), (tile, LANES)))

    col = lax.broadcasted_iota(jnp.int32, (1, tile), 1)
    for r, c in streams:
        attend(r, 0, c, _load_tile(k_ref, r, 0, c, tile), _load_tile(v_ref, r, 0, c, tile), col <= row)

    col2 = lax.broadcasted_iota(jnp.int32, (1, 2 * tile), 1)
    band = jnp.logical_and(col2 >= row, col2 <= row + tile)

    def body(i, _):
        for r, c in streams:
            k = jnp.concatenate([_load_tile(k_ref, r, i - 1, c, tile), _load_tile(k_ref, r, i, c, tile)], axis=0)
            v = jnp.concatenate([_load_tile(v_ref, r, i - 1, c, tile), _load_tile(v_ref, r, i, c, tile)], axis=0)
            attend(r, i, c, k, v, band)
        return 0

    n_tiles = slabs * per_slab // tile
    lax.fori_loop(1, n_tiles, body, 0, unroll=max(1, min(4, n_tiles - 1)))


def _dilated_group(qkv, *, batch, seq, row_tile, group, dil, heads, residues, q_col, k_col, v_col):
    hpg = B_HEADS_PER_GROUP
    tile = B_GROUPS[group][0] // dil
    slabs, per_slab = seq // row_tile, row_tile // dil
    assert tile == LANES and hpg % heads == 0 and (seq // dil) % tile == 0 and dil % residues == 0
    assert per_slab % tile == 0 or tile % per_slab == 0
    width = heads * LANES
    view = qkv.reshape(batch, slabs, dil, per_slab, qkv.shape[1])

    def in_spec(col):
        first = col + group * hpg
        assert first % heads == 0
        return pl.BlockSpec((None, slabs, residues, per_slab, width),
                            lambda b, h, r: (b, 0, r, 0, first // heads + h))

    out_spec = pl.BlockSpec((None, slabs, residues, per_slab, width), lambda b, h, r: (b, 0, r, 0, h))
    out_sds = jax.ShapeDtypeStruct((batch, slabs, dil, per_slab, hpg * LANES), F32)
    o, lse = pl.pallas_call(
        functools.partial(_dilated_body, tile=tile, heads=heads),
        grid=(batch, hpg // heads, dil // residues),
        in_specs=[in_spec(q_col), in_spec(k_col), in_spec(v_col)],
        out_specs=[out_spec, out_spec],
        out_shape=[out_sds, out_sds],
        compiler_params=_params("parallel", "parallel", "parallel"),
        name=f"dilated_g{group}",
    )(view, view, view)
    shape = (batch * slabs, dil, per_slab, hpg * LANES)
    return o.reshape(shape), lse.reshape(shape)


def _merge_body(o0_ref, o1_ref, o2_ref, l0_ref, l1_ref, l2_ref, out_ref, nat_ref):
    def natural(ref, slot, c):
        dil, rows = ref.shape[0], ref.shape[1]
        if dil == 1:
            return ref[0, :, _lanes(c)]
        for r in range(dil):
            nat_ref[slot, pl.ds(r, rows, stride=dil), :] = ref[r, :, _lanes(c)]
        return nat_ref[slot]

    for c in range(out_ref.shape[1] // LANES):
        o0, o1, o2 = natural(o0_ref, None, c), natural(o1_ref, 0, c), natural(o2_ref, 1, c)
        l0, l1, l2 = natural(l0_ref, None, c), natural(l1_ref, 2, c), natural(l2_ref, 3, c)
        m = jnp.maximum(jnp.maximum(l0, l1), l2)
        w0, w1, w2 = jnp.exp(l0 - m), jnp.exp(l1 - m), jnp.exp(l2 - m)
        out_ref[:, _lanes(c)] = ((w0 * o0 + w1 * o1 + w2 * o2) / (w0 + w1 + w2)).astype(out_ref.dtype)


def _merge_groups(outs, lses):
    tiles, _, _, width = outs[0].shape
    row_tile = outs[0].shape[1] * outs[0].shape[2]
    specs = [pl.BlockSpec((None,) + a.shape[1:], lambda i: (i, 0, 0, 0)) for a in (*outs, *lses)]
    return pl.pallas_call(
        _merge_body,
        grid=(tiles,),
        in_specs=specs,
        out_specs=pl.BlockSpec((row_tile, width), lambda i: (i, 0)),
        out_shape=jax.ShapeDtypeStruct((tiles * row_tile, width), BF16),
        scratch_shapes=[pltpu.VMEM((4, row_tile, LANES), F32)],
        compiler_params=_params("parallel"),
        name="dilated_merge",
    )(*outs, *lses)


def _mla_body(q_ref, kv_ref, kr_ref, o_ref, qt_ref, vt_ref, s_ref, p_ref, acc_ref, *, tq, tk, heads):
    qi = pl.program_id(2)

    @pl.when(qi == 0)
    def _():
        _fill_transposed(vt_ref, kv_ref, [(2 * c + 1) * LANES for c in range(heads)], tk)

    def make_score_fn(c):
        qt_ref[c] = q_ref[:, _lanes(c, 2 * LANES)].T

        def score(start):
            k = jnp.concatenate([kv_ref[pl.ds(start, tk), _lanes(2 * c)], kr_ref[pl.ds(start, tk), :]], axis=1)
            return jnp.dot(k, qt_ref[c], preferred_element_type=F32)

        return score

    diag = _causal_diag_masks(tk, tq, tq)
    results = _causal_flash_t([make_score_fn(c) for c in range(heads)], vt_ref, s_ref, p_ref, acc_ref,
                              qi * (tq // tk), lambda d, c: diag[d], tq // tk)
    for c, (l, acc) in enumerate(results):
        o_ref[:, _lanes(c)] = (acc / l).T.astype(o_ref.dtype)


def _mla_attention(q, kv, k_rope_src, *, batch, seq, tq, tk, heads, k_rope_col):
    nq = seq // tq
    assert C_HEADS % heads == 0 and seq % tq == 0 and tq % tk == 0
    return pl.pallas_call(
        functools.partial(_mla_body, tq=tq, tk=tk, heads=heads),
        grid=(batch, C_HEADS // heads, nq),
        in_specs=[pl.BlockSpec((tq, heads * 2 * LANES), lambda b, h, i: (b * nq + i, h)),
                  pl.BlockSpec((seq, heads * 2 * LANES), lambda b, h, i: (b, h)),
                  pl.BlockSpec((seq, LANES), lambda b, h, i: (b, k_rope_col))],
        out_specs=pl.BlockSpec((tq, heads * LANES), lambda b, h, i: (b * nq + i, h)),
        out_shape=jax.ShapeDtypeStruct((batch * seq, C_HEADS * LANES), BF16),
        scratch_shapes=[pltpu.VMEM((heads, 2 * LANES, tq), BF16)] + _flash_scratch(heads, tk, tq, LANES, seq),
        compiler_params=_params("parallel", "parallel", "arbitrary"),
        name="mla_attn",
    )(q, kv, k_rope_src)


def _moba_body(q_ref, k_ref, v_ref, o_ref, kmean_ref, sel_ref, qt_ref, vt_ref, s_ref, p_ref, acc_ref, *,
               n_blk, tq, heads):
    qi = pl.program_id(2)
    blk = MOBA_BLOCK
    n_own = tq // blk

    @pl.when(qi == 0)
    def _():
        for c in range(heads):
            k_all = k_ref[:, _lanes(c)].astype(F32).reshape(n_blk, blk, LANES)
            kmean_ref[c] = jnp.mean(k_all, axis=1)
        _fill_transposed(vt_ref, v_ref, [c * LANES for c in range(heads)], blk)

    blk_id = lax.broadcasted_iota(jnp.int32, (n_blk, 1), 0)
    query_col = lax.broadcasted_iota(jnp.int32, (1, tq), 1)
    own_in_tile = query_col // blk
    past = blk_id < qi * n_own + own_in_tile
    causal = lax.broadcasted_iota(jnp.int32, (blk, 1), 0) <= query_col % blk

    def make_score_fn(c):
        q_t = q_ref[:, _lanes(c)].T
        km = kmean_ref[c]
        km_hi = km.astype(BF16)
        rem = km - km_hi.astype(F32)
        km_mid = rem.astype(BF16)
        km_lo = (rem - km_mid.astype(F32)).astype(BF16)
        gate = (jnp.dot(km_hi, q_t, preferred_element_type=F32) + jnp.dot(km_mid, q_t, preferred_element_type=F32)
                + jnp.dot(km_lo, q_t, preferred_element_type=F32))
        gate = jnp.where(past, gate, -jnp.inf)
        selected = jnp.zeros(gate.shape, F32)
        for _ in range(min(MOBA_TOPK, n_blk - 1)):
            best = jnp.max(gate, axis=0, keepdims=True)
            first = jnp.min(jnp.where(gate == best, blk_id, n_blk), axis=0, keepdims=True)
            pick = blk_id == first
            selected = jnp.where(jnp.logical_and(pick, past), 1.0, selected)
            gate = jnp.where(pick, -jnp.inf, gate)
        sel_ref[c] = selected
        qt_ref[c] = q_t
        return lambda start: jnp.dot(k_ref[pl.ds(start, blk), _lanes(c)], qt_ref[c], preferred_element_type=F32)

    score_fns = [make_score_fn(c) for c in range(heads)]
    n_past = qi * n_own
    chosen_fns = [lambda n, c=c: jnp.logical_or(sel_ref[c, pl.ds(n, 1), :] > 0.0, n >= n_past)
                  for c in range(heads)]

    def own_block_mask(d, c):
        chosen = sel_ref[c, pl.ds(n_past + d, 1), :] > 0.0
        return jnp.logical_or(jnp.logical_and(own_in_tile == d, causal),
                              jnp.logical_and(own_in_tile > d, chosen))

    results = _causal_flash_t(score_fns, vt_ref, s_ref, p_ref, acc_ref, n_past, own_block_mask, n_own,
                              past_mask_fns=chosen_fns)
    for c, (l, acc) in enumerate(results):
        o_ref[:, _lanes(c)] = (acc / l).T.astype(o_ref.dtype)


def _moba_attention(qkv, *, batch, seq, tq, heads, q_col, k_col, v_col):
    blk = MOBA_BLOCK
    n_blk = seq // blk
    nq = seq // tq
    width = heads * LANES
    assert seq % tq == 0 and tq % blk == 0 and D_HEADS % heads == 0
    assert q_col % heads == 0 and k_col % heads == 0 and v_col % heads == 0
    return pl.pallas_call(
        functools.partial(_moba_body, n_blk=n_blk, tq=tq, heads=heads),
        grid=(batch, D_HEADS // heads, nq),
        in_specs=[pl.BlockSpec((tq, width), lambda b, h, i: (b * nq + i, q_col // heads + h)),
                  pl.BlockSpec((seq, width), lambda b, h, i: (b, k_col // heads + h)),
                  pl.BlockSpec((seq, width), lambda b, h, i: (b, v_col // heads + h))],
        out_specs=pl.BlockSpec((tq, width), lambda b, h, i: (b * nq + i, h)),
        out_shape=jax.ShapeDtypeStruct((batch * seq, D_HEADS * LANES), BF16),
        scratch_shapes=[pltpu.VMEM((heads, n_blk, LANES), F32),
                        pltpu.VMEM((heads, n_blk, tq), F32),
                        pltpu.VMEM((heads, LANES, tq), BF16)] + _flash_scratch(heads, blk, tq, LANES, seq),
        compiler_params=_params("parallel", "parallel", "arbitrary"),
        name="moba_attn",
    )(qkv, qkv, qkv)


def _rope_tables(seq):
    pos = jnp.arange(seq, dtype=F32)[:, None]

    def angles(dim):
        inv = 1.0 / (ROPE_THETA ** (jnp.arange(0, dim, 2, dtype=F32) / dim))
        return pos * inv[None, :]

    a128 = angles(HEAD_DIM)
    a64 = angles(A_QK_DIM)
    c128 = jnp.concatenate([jnp.cos(a128)] * 2, axis=1)
    s128 = jnp.concatenate([-jnp.sin(a128), jnp.sin(a128)], axis=1)
    c64 = jnp.concatenate([jnp.cos(a64)] * 4, axis=1)
    s64 = jnp.concatenate([-jnp.sin(a64)] * 2 + [jnp.sin(a64)] * 2, axis=1)
    return c128, s128, c64, s64


def _interleave_diff_heads(w):
    d = w.shape[0]
    return w.reshape(d, A_HEADS, 2, 2, 32).transpose(0, 1, 3, 2, 4).reshape(d, A_HEADS * LANES)


def _spread_rope64(w):
    z = jnp.zeros(w.shape[:-1] + (32,), w.dtype)
    return jnp.concatenate([w[..., :32], z, w[..., 32:], z], axis=-1)


def _ab_mixer(x, gain, w_in, lam_params, subln, w_out, layer_idx, rope, *, batch, seq):
    aw = A_HEADS * LANES
    w = jnp.concatenate([_interleave_diff_heads(w_in[:, :aw]), _interleave_diff_heads(w_in[:, aw:2 * aw]),
                         w_in[:, 2 * aw:]], axis=1).astype(BF16)
    a_scale = A_QK_DIM ** -0.5 * LOG2_E
    b_scale = HEAD_DIM ** -0.5
    dils = [dil for _, dil in B_GROUPS for _ in range(B_HEADS_PER_GROUP)]
    modes = ([(ROPE64, a_scale, 1)] * A_HEADS + [(ROPE64, 1.0, 1)] * A_HEADS + [(PLAIN, 1.0, 1)] * A_HEADS
             + [(ROPE128, b_scale, d) for d in dils] + [(ROPE128, 1.0, d) for d in dils]
             + [(PLAIN, 1.0, d) for d in dils])
    row_tile = 1024
    qkv = _norm_proj(x, gain, w, rope, modes, seq=seq, tm=row_tile, tn=1280, out_dtype=BF16)
    lam_init = 0.8 - 0.6 * math.exp(-0.3 * layer_idx)
    oa = _diff_attention(qkv, lam_params, subln, batch=batch, seq=seq, tq=1024, tk=256, heads=2,
                         q_col=0, k_col=A_HEADS, v_col=2 * A_HEADS, lam_init=lam_init)
    b0 = 3 * A_HEADS
    outs, lses = [], []
    for g, (_, dil) in enumerate(B_GROUPS):
        o, lse = _dilated_group(qkv, batch=batch, seq=seq, row_tile=row_tile, group=g, dil=dil,
                                heads=2 if dil == 1 else 4, residues=4 if dil == 16 else 1,
                                q_col=b0, k_col=b0 + B_HEADS, v_col=b0 + 2 * B_HEADS)
        outs.append(o)
        lses.append(lse)
    ob = _merge_groups(outs, lses)
    return _out_proj(oa, ob, w_out.astype(BF16), x, tm=1024, tn=1024)


def _cd_mixer(x, gain, w_in, q_norm, w_uq, kv_norm, w_ukv, w_out, rope, *, batch, seq):
    lat = C_Q_LORA + C_KV_LORA
    d_scale = HEAD_DIM ** -0.5 * LOG2_E
    modes = [(ROPE128, d_scale, 1)] * D_HEADS + [(ROPE128, 1.0, 1)] * D_HEADS + [(PLAIN, 1.0, 1)] * D_HEADS
    main = _norm_proj(x, gain, w_in[:, lat + C_ROPE:].astype(BF16), rope, modes, seq=seq, tm=1024, tn=1024,
                      out_dtype=BF16)

    c_scale = (C_NOPE + C_ROPE) ** -0.5 * LOG2_E
    w_q = w_uq.reshape(C_Q_LORA, C_HEADS, C_NOPE + C_ROPE)
    w_q = jnp.concatenate([w_q[..., :C_NOPE], _spread_rope64(w_q[..., C_NOPE:])], axis=-1)
    w_q = w_q.reshape(C_Q_LORA, C_HEADS * 2 * LANES).astype(BF16)
    w_latent = jnp.concatenate([w_in[:, :lat], _spread_rope64(w_in[:, lat:lat + C_ROPE])], axis=1).astype(BF16)
    qc, kv, k_rope = _mla_proj(x, gain, w_latent, q_norm, w_q, kv_norm, w_ukv.astype(BF16), rope,
                               seq=seq, tm=512, q_scale=c_scale)
    oc = _mla_attention(qc, kv, k_rope, batch=batch, seq=seq, tq=1024, tk=256, heads=2, k_rope_col=0)
    od = _moba_attention(main, batch=batch, seq=seq, tq=512, heads=4,
                         q_col=0, k_col=D_HEADS, v_col=2 * D_HEADS)
    return _out_proj(oc, od, w_out.astype(BF16), x, tm=1024, tn=1024)


def kernel(x, p, ffn_norm, ffn_w_gu, ffn_w_down, mix_norm, ab_w_in, ab_lambda, ab_subln, ab_w_out,
           cd_w_in, cd_q_norm, cd_w_uq, cd_kv_norm, cd_w_ukv, cd_w_out, ple_norm, ple_w_gate,
           ple_w_proj, final_norm):
    batch, seq, d = x.shape
    depth = p.shape[0]
    rope = _rope_tables(seq)
    w_gu, w_down = ffn_w_gu.astype(BF16), ffn_w_down.astype(BF16)
    x = x.reshape(batch * seq, d)
    for i in range(depth):
        j = i // 2
        x = _ffn(x, ffn_norm[i, 0], w_gu, w_down, i, 0, tm=1024, tf=512)
        if i % 2 == 0:
            x = _ab_mixer(x, mix_norm[i], ab_w_in[j], ab_lambda[j], ab_subln[j], ab_w_out[j], i, rope,
                          batch=batch, seq=seq)
        else:
            x = _cd_mixer(x, mix_norm[i], cd_w_in[j], cd_q_norm[j], cd_w_uq[j], cd_kv_norm[j], cd_w_ukv[j],
                          cd_w_out[j], rope, batch=batch, seq=seq)
        x = _ffn(x, ffn_norm[i, 1], w_gu, w_down, i, 1, tm=1024, tf=512)
        x = _ple(x, p.reshape(depth * batch * seq, -1), i, ple_norm[i], ple_w_gate[i].astype(BF16),
                 ple_w_proj[i].astype(BF16), final_norm, tm=512, final_norm=(i == depth - 1))
    return x.reshape(batch, seq, d)
```

```python
import functools
import math

import jax
import jax.numpy as jnp
from jax import lax
from jax.experimental import pallas as pl
from jax.experimental.pallas import tpu as pltpu

F32 = jnp.float32
BF16 = jnp.bfloat16

D_MODEL = 2048
D_FF = 5632
PLE_DIM = 256
HEAD_DIM = 128
ROPE_THETA = 10000.0
NORM_EPS = 1e-6

A_HEADS = 8
A_QK_DIM = 64
B_GROUPS = ((128, 1), (512, 4), (2048, 16))
B_HEADS_PER_GROUP = 4
B_HEADS = B_HEADS_PER_GROUP * len(B_GROUPS)
C_HEADS = 8
C_Q_LORA = 512
C_KV_LORA = 256
C_NOPE = 128
C_ROPE = 64
D_HEADS = 8
MOBA_BLOCK = 256
MOBA_TOPK = 3

LOG2_E = 1.0 / math.log(2.0)
PAST_BLOCKS_PER_ITERATION = 4
SUM_ROWS = 16
LANES = 128
VMEM_LIMIT = 56 * 1024 * 1024

PLAIN, ROPE128, ROPE64 = 0, 1, 2


def _rms(x, gain):
    return x * lax.rsqrt(jnp.mean(x * x, axis=-1, keepdims=True) + NORM_EPS) * gain


def _params(*semantics):
    return pltpu.CompilerParams(dimension_semantics=semantics, vmem_limit_bytes=VMEM_LIMIT)


def _norm_proj_body(x_ref, g_ref, w_ref, c128_ref, s128_ref, c64_ref, s64_ref, o_ref, xn_ref, stage_ref, *,
                    tile_patterns):
    j = pl.program_id(1)
    tm = o_ref.shape[0]

    @pl.when(j == 0)
    def _():
        xn_ref[...] = _rms(x_ref[...], g_ref[...]).astype(BF16)

    def tile(pattern):
        acc = jnp.dot(xn_ref[...], w_ref[...], preferred_element_type=F32)
        for c, (kind, scale, dil) in enumerate(pattern):
            a = acc[:, _lanes(c)]
            if scale != 1.0:
                a = a * scale
            if kind == ROPE128:
                a = a * c128_ref[...] + pltpu.roll(a, 64, 1) * s128_ref[...]
            elif kind == ROPE64:
                a = a * c64_ref[...] + pltpu.roll(a, 64, 1) * s64_ref[...]
            if dil == 1:
                o_ref[:, _lanes(c)] = a.astype(o_ref.dtype)
            else:
                rows = tm // dil
                stage_ref[c] = a
                for r in range(dil):
                    o_ref[r * rows:(r + 1) * rows, _lanes(c)] = (
                        stage_ref[c, pl.ds(r, rows, stride=dil), :].astype(o_ref.dtype))

    distinct = sorted(set(tile_patterns), key=tile_patterns.index)
    if len(distinct) == 1:
        tile(distinct[0])
    else:
        for pattern in distinct:
            tiles = [t for t, p in enumerate(tile_patterns) if p == pattern]
            cond = functools.reduce(jnp.logical_or, [j == t for t in tiles])
            pl.when(cond)(functools.partial(tile, pattern))


def _norm_proj(x, gain, w, rope, chunk_modes, *, seq, tm, tn, out_dtype):
    t, k = x.shape
    n = w.shape[1]
    per_tile = tn // LANES
    assert t % tm == 0 and n % tn == 0 and seq % tm == 0 and len(chunk_modes) * LANES == n
    tile_patterns = tuple(tuple(chunk_modes[a * per_tile:(a + 1) * per_tile]) for a in range(n // tn))
    seq_tiles = seq // tm
    tab = pl.BlockSpec((tm, LANES), lambda i, j: (i % seq_tiles, 0))
    return pl.pallas_call(
        functools.partial(_norm_proj_body, tile_patterns=tile_patterns),
        grid=(t // tm, n // tn),
        in_specs=[pl.BlockSpec((tm, k), lambda i, j: (i, 0)),
                  pl.BlockSpec((1, k), lambda i, j: (0, 0)),
                  pl.BlockSpec((k, tn), lambda i, j: (0, j)),
                  tab, tab, tab, tab],
        out_specs=pl.BlockSpec((tm, tn), lambda i, j: (i, j)),
        out_shape=jax.ShapeDtypeStruct((t, n), out_dtype),
        scratch_shapes=[pltpu.VMEM((tm, k), BF16), pltpu.VMEM((per_tile, tm, LANES), F32)],
        compiler_params=_params("parallel", "arbitrary"),
        name="norm_proj",
    )(x, gain.reshape(1, k), w, *rope)


def _out_proj_body(a1_ref, a2_ref, w1_ref, w2_ref, res_ref, o_ref):
    o_ref[...] = (res_ref[...] + jnp.dot(a1_ref[...], w1_ref[...], preferred_element_type=F32)
                  + jnp.dot(a2_ref[...], w2_ref[...], preferred_element_type=F32))


def _out_proj(a1, a2, w, res, *, tm, tn):
    t, k1 = a1.shape
    k2 = a2.shape[1]
    n = w.shape[1]
    assert t % tm == 0 and n % tn == 0 and w.shape[0] == k1 + k2 and k1 % k2 == 0
    return pl.pallas_call(
        _out_proj_body,
        grid=(t // tm, n // tn),
        in_specs=[pl.BlockSpec((tm, k1), lambda i, j: (i, 0)),
                  pl.BlockSpec((tm, k2), lambda i, j: (i, 0)),
                  pl.BlockSpec((k1, tn), lambda i, j: (0, j)),
                  pl.BlockSpec((k2, tn), lambda i, j: (k1 // k2, j)),
                  pl.BlockSpec((tm, tn), lambda i, j: (i, j))],
        out_specs=pl.BlockSpec((tm, tn), lambda i, j: (i, j)),
        out_shape=jax.ShapeDtypeStruct((t, n), F32),
        compiler_params=_params("parallel", "parallel"),
        name="out_proj",
    )(a1, a2, w, w, res)


def _mla_proj_body(x_ref, g_ref, wl_ref, qn_ref, wq_ref, kvn_ref, wkv_ref, c64_ref, s64_ref, q_ref, kv_ref, kr_ref,
                   *, q_lora, kv_lora, q_scale):
    half = x_ref.shape[0] // 2
    lat = q_lora + kv_lora
    for rows in (slice(0, half), slice(half, 2 * half)):
        xn = _rms(x_ref[rows, :], g_ref[...]).astype(BF16)
        latent = jnp.dot(xn, wl_ref[...], preferred_element_type=F32)
        k_rope = latent[:, lat:]
        kr_ref[rows, :] = (k_rope * c64_ref[rows, :] + pltpu.roll(k_rope, 64, 1) * s64_ref[rows, :]).astype(kr_ref.dtype)
        c_q = _rms(latent[:, :q_lora], qn_ref[...]).astype(BF16)
        c_kv = _rms(latent[:, q_lora:lat], kvn_ref[...]).astype(BF16)
        kv_ref[rows, :] = jnp.dot(c_kv, wkv_ref[...], preferred_element_type=F32).astype(kv_ref.dtype)
        q = jnp.dot(c_q, wq_ref[...], preferred_element_type=F32) * q_scale
        for c in range(q_ref.shape[1] // LANES):
            a = q[:, _lanes(c)]
            if c % 2 == 1:
                a = a * c64_ref[rows, :] + pltpu.roll(a, 64, 1) * s64_ref[rows, :]
            q_ref[rows, _lanes(c)] = a.astype(q_ref.dtype)


def _mla_proj(x, gain, w_latent, q_norm, w_q, kv_norm, w_kv, rope, *, seq, tm, q_scale):
    t, d = x.shape
    q_lora, kv_lora = w_q.shape[0], w_kv.shape[0]
    assert t % tm == 0 and seq % tm == 0 and w_latent.shape[1] == q_lora + kv_lora + LANES
    seq_tiles = seq // tm
    const = lambda i: (0, 0)
    tab = pl.BlockSpec((tm, LANES), lambda i: (i % seq_tiles, 0))
    full = lambda a: pl.BlockSpec(a.shape, const)
    out_spec = lambda width: pl.BlockSpec((tm, width), lambda i: (i, 0))
    widths = (w_q.shape[1], w_kv.shape[1], LANES)
    return pl.pallas_call(
        functools.partial(_mla_proj_body, q_lora=q_lora, kv_lora=kv_lora, q_scale=q_scale),
        grid=(t // tm,),
        in_specs=[pl.BlockSpec((tm, d), lambda i: (i, 0)), pl.BlockSpec((1, d), const), full(w_latent),
                  pl.BlockSpec((1, q_lora), const), full(w_q), pl.BlockSpec((1, kv_lora), const), full(w_kv),
                  tab, tab],
        out_specs=[out_spec(w) for w in widths],
        out_shape=[jax.ShapeDtypeStruct((t, w), BF16) for w in widths],
        compiler_params=_params("parallel"),
        name="mla_proj",
    )(x, gain.reshape(1, d), w_latent, q_norm.reshape(1, q_lora), w_q, kv_norm.reshape(1, kv_lora), w_kv,
      rope[2], rope[3])


def _ffn_body(x_ref, g_ref, wg_ref, wu_ref, wd_ref, o_ref, xn_ref):
    j = pl.program_id(1)

    @pl.when(j == 0)
    def _():
        xn_ref[...] = _rms(x_ref[...], g_ref[...]).astype(BF16)
        o_ref[...] = jnp.zeros_like(o_ref)

    xn = xn_ref[...]
    g = jnp.dot(xn, wg_ref[...], preferred_element_type=F32)
    u = jnp.dot(xn, wu_ref[...], preferred_element_type=F32)
    h = (g * jax.nn.sigmoid(g) * u).astype(BF16)
    o_ref[...] += jnp.dot(h, wd_ref[...], preferred_element_type=F32)

    @pl.when(j == pl.num_programs(1) - 1)
    def _():
        o_ref[...] = x_ref[...] + 0.5 * o_ref[...]


def _ffn(x, gain, w_gu, w_down, layer, half, *, tm, tf):
    t, d = x.shape
    d_ff = w_down.shape[2]
    assert t % tm == 0 and d_ff % tf == 0
    nf = d_ff // tf
    return pl.pallas_call(
        _ffn_body,
        grid=(t // tm, nf),
        in_specs=[pl.BlockSpec((tm, d), lambda i, j: (i, 0)),
                  pl.BlockSpec((1, d), lambda i, j: (0, 0)),
                  pl.BlockSpec((None, None, d, tf), lambda i, j: (layer, half, 0, j)),
                  pl.BlockSpec((None, None, d, tf), lambda i, j: (layer, half, 0, nf + j)),
                  pl.BlockSpec((None, None, tf, d), lambda i, j: (layer, half, j, 0))],
        out_specs=pl.BlockSpec((tm, d), lambda i, j: (i, 0)),
        out_shape=jax.ShapeDtypeStruct((t, d), F32),
        scratch_shapes=[pltpu.VMEM((tm, d), BF16)],
        compiler_params=_params("parallel", "arbitrary"),
        name="ffn",
    )(x, gain.reshape(1, d), w_gu, w_gu, w_down)


def _ple_body(x_ref, p_ref, g_ref, wg_ref, wp_ref, fg_ref, o_ref, *, final_norm):
    half = x_ref.shape[0] // 2
    for rows in (slice(0, half), slice(half, 2 * half)):
        x = x_ref[rows, :]
        xn = _rms(x, g_ref[...]).astype(BF16)
        gate = jax.nn.sigmoid(jnp.dot(xn, wg_ref[...], preferred_element_type=F32))
        proj = jnp.dot(p_ref[rows, :].astype(BF16), wp_ref[...], preferred_element_type=F32)
        y = x + gate * proj
        if final_norm:
            y = _rms(y, fg_ref[...])
        o_ref[rows, :] = y


def _ple(x, p, layer, gain, w_gate, w_proj, final_gain, *, tm, final_norm):
    t, d = x.shape
    pd = p.shape[1]
    assert t % tm == 0
    first_tile = layer * (t // tm)
    const = lambda i: (0, 0)
    return pl.pallas_call(
        functools.partial(_ple_body, final_norm=final_norm),
        grid=(t // tm,),
        in_specs=[pl.BlockSpec((tm, d), lambda i: (i, 0)),
                  pl.BlockSpec((tm, pd), lambda i: (first_tile + i, 0)),
                  pl.BlockSpec((1, d), const),
                  pl.BlockSpec((d, d), const),
                  pl.BlockSpec((pd, d), const),
                  pl.BlockSpec((1, d), const)],
        out_specs=pl.BlockSpec((tm, d), lambda i: (i, 0)),
        out_shape=jax.ShapeDtypeStruct((t, d), F32),
        compiler_params=_params("parallel"),
        name="ple",
    )(x, p, gain.reshape(1, d), w_gate, w_proj, final_gain.reshape(1, d))


def _dot_nt(a, b):
    return lax.dot_general(a, b, (((1,), (1,)), ((), ())), preferred_element_type=F32)


def _lanes(c, width=LANES):
    return slice(c * width, (c + 1) * width)


def _fill_transposed(vt_ref, v_ref, lane_offsets, chunk):
    dv = vt_ref.shape[1] - SUM_ROWS
    ones_row = jnp.where(lax.broadcasted_iota(jnp.int32, (SUM_ROWS, chunk), 0) == 0, 1.0, 0.0).astype(BF16)

    def body(j, _):
        start = pl.multiple_of(j * chunk, chunk)
        for c, off in enumerate(lane_offsets):
            vt_ref[c, :dv, pl.ds(start, chunk)] = v_ref[pl.ds(start, chunk), off:off + dv].T
            vt_ref[c, dv:, pl.ds(start, chunk)] = ones_row
        return 0

    lax.fori_loop(0, v_ref.shape[0] // chunk, body, 0)


def _flash_scratch(heads, tk, cols, dv, seq):
    return [pltpu.VMEM((heads, dv + SUM_ROWS, seq), BF16),
            pltpu.VMEM((heads, tk, cols), F32), pltpu.VMEM((heads, tk, cols), BF16),
            pltpu.VMEM((heads, dv + SUM_ROWS, cols), F32)]


def _causal_diag_masks(tk, tq, cols):
    key_row = lax.broadcasted_iota(jnp.int32, (tk, 1), 0)
    query_col = lax.broadcasted_iota(jnp.int32, (1, cols), 1) % tq
    return [key_row + d * tk <= query_col for d in range(tq // tk)]


def _causal_col_ranges(tk, tq, maps):
    return lambda d: [(g * tq + d * tk, (g + 1) * tq) for g in range(maps)]


def _causal_flash_t(score_fns, vt_ref, s_ref, p_ref, acc_ref, n_past, diag_mask_fn, diag_cols_fn, n_diag,
                    past_mask_fns=None):
    heads = range(len(score_fns))
    _, tk, cols = s_ref.shape

    everything = [(0, cols)]

    def stage_scores(block, ranges=everything):
        start = pl.multiple_of(block * tk, tk)
        for c in heads:
            for lo, hi in ranges:
                s = score_fns[c](start, slice(lo, hi))
                if past_mask_fns is not None:
                    s = jnp.where(past_mask_fns[c](block)[:, lo:hi], s, -jnp.inf)
                s_ref[c, :, lo:hi] = s

    def accumulate(block, alphas, ranges=everything):
        start = pl.multiple_of(block * tk, tk)
        for c in heads:
            for lo, hi in ranges:
                pv = jnp.dot(vt_ref[c, :, pl.ds(start, tk)], p_ref[c, :, lo:hi], preferred_element_type=F32)
                acc_ref[c, :, lo:hi] = alphas[c][:, lo:hi] * acc_ref[c, :, lo:hi] + pv

    def softmax(stats, diag=None, ranges=everything):
        new_stats, alphas = [], []
        for c in heads:
            m = stats[c]
            if diag is None:
                read = lambda lo, hi: s_ref[c, :, lo:hi]
            else:
                visible = diag_mask_fn(diag, c)
                read = lambda lo, hi: jnp.where(visible[:, lo:hi], s_ref[c, :, lo:hi], -jnp.inf)
            pieces, at = [], 0
            for lo, hi in ranges:
                if lo > at:
                    pieces.append(m[:, at:lo])
                pieces.append(jnp.maximum(m[:, lo:hi], jnp.max(read(lo, hi), axis=0, keepdims=True)))
                at = hi
            if at < cols:
                pieces.append(m[:, at:])
            m_new = pieces[0] if len(pieces) == 1 else jnp.concatenate(pieces, axis=1)
            alpha = jnp.exp2(m - m_new)
            for lo, hi in ranges:
                p_ref[c, :, lo:hi] = jnp.exp2(read(lo, hi) - m_new[:, lo:hi]).astype(BF16)
            new_stats.append(m_new)
            alphas.append(alpha)
        return tuple(new_stats), tuple(alphas)

    def past_block(j, carry):
        stats, alphas = carry
        accumulate(jnp.maximum(j - 1, 0), alphas)
        stats, alphas = softmax(stats)
        stage_scores(j + 1)
        return stats, alphas

    for c in heads:
        p_ref[c] = jnp.zeros(p_ref.shape[1:], BF16)
        acc_ref[c] = jnp.zeros(acc_ref.shape[1:], F32)
    stage_scores(0)
    init = (tuple(jnp.full((1, cols), jnp.finfo(F32).min, F32) for _ in heads),
            tuple(jnp.ones((1, cols), F32) for _ in heads))
    def past_group(t, carry):
        for u in range(PAST_BLOCKS_PER_ITERATION):
            carry = past_block(t * PAST_BLOCKS_PER_ITERATION + u, carry)
        return carry

    n_grouped = n_past // PAST_BLOCKS_PER_ITERATION
    carry = lax.fori_loop(0, n_grouped, past_group, init)
    stats, alphas = lax.fori_loop(n_grouped * PAST_BLOCKS_PER_ITERATION, n_past, past_block, carry)

    accumulate(jnp.maximum(n_past - 1, 0), alphas)
    for d in range(n_diag):
        stats, alphas = softmax(stats, diag=d, ranges=diag_cols_fn(d))
        if d + 1 < n_diag:
            stage_scores(n_past + d + 1, ranges=diag_cols_fn(d + 1))
        accumulate(n_past + d, alphas, ranges=diag_cols_fn(d))
    dv = acc_ref.shape[1] - SUM_ROWS
    return tuple((acc_ref[c, dv:dv + 1, :], acc_ref[c, :dv, :]) for c in heads)


def _diff_attn_body(lam_ref, subln_ref, q_ref, k_ref, v_ref, o_ref, qt_ref, vt_ref, s_ref, p_ref, acc_ref, *,
                    tq, tk, heads, lam_init):
    qi = pl.program_id(2)

    @pl.when(qi == 0)
    def _():
        _fill_transposed(vt_ref, v_ref, [c * LANES for c in range(heads)], tk)

    is_map0 = (lax.broadcasted_iota(jnp.int32, (LANES, 1), 0) % 64) < 32

    def make_score_fn(c):
        q_t = q_ref[:, _lanes(c)].T
        zero = jnp.zeros_like(q_t)
        qt_ref[c] = jnp.concatenate([jnp.where(is_map0, q_t, zero), jnp.where(is_map0, zero, q_t)], axis=1)
        return lambda start, cols: jnp.dot(k_ref[pl.ds(start, tk), _lanes(c)], qt_ref[c, :, cols],
                                           preferred_element_type=F32)

    diag = _causal_diag_masks(tk, tq, 2 * tq)
    results = _causal_flash_t([make_score_fn(c) for c in range(heads)], vt_ref, s_ref, p_ref, acc_ref,
                              qi * (tq // tk), lambda d, c: diag[d], _causal_col_ranges(tk, tq, 2), tq // tk)

    lp = lam_ref[...]
    lam = (jnp.exp(jnp.sum(lp[0:1] * lp[1:2], axis=-1, keepdims=True))
           - jnp.exp(jnp.sum(lp[2:3] * lp[3:4], axis=-1, keepdims=True)) + lam_init)
    for c, (l, acc) in enumerate(results):
        o_t = acc / l
        o = (o_t[:, :tq] - lam * o_t[:, tq:]).T
        o_ref[:, _lanes(c)] = (_rms(o, subln_ref[...]) * (1.0 - lam_init)).astype(o_ref.dtype)


def _diff_attention(qkv, lam_params, subln, *, batch, seq, tq, tk, heads, q_col, k_col, v_col, lam_init):
    nq = seq // tq
    width = heads * LANES
    assert A_HEADS % heads == 0 and q_col % heads == 0 and k_col % heads == 0 and v_col % heads == 0
    assert seq % tq == 0 and tq % tk == 0
    return pl.pallas_call(
        functools.partial(_diff_attn_body, tq=tq, tk=tk, heads=heads, lam_init=lam_init),
        grid=(batch, A_HEADS // heads, nq),
        in_specs=[pl.BlockSpec((4, A_QK_DIM), lambda b, h, i: (0, 0)),
                  pl.BlockSpec((1, LANES), lambda b, h, i: (0, 0)),
                  pl.BlockSpec((tq, width), lambda b, h, i: (b * nq + i, q_col // heads + h)),
                  pl.BlockSpec((seq, width), lambda b, h, i: (b, k_col // heads + h)),
                  pl.BlockSpec((seq, width), lambda b, h, i: (b, v_col // heads + h))],
        out_specs=pl.BlockSpec((tq, width), lambda b, h, i: (b * nq + i, h)),
        out_shape=jax.ShapeDtypeStruct((batch * seq, A_HEADS * LANES), BF16),
        scratch_shapes=[pltpu.VMEM((heads, LANES, 2 * tq), BF16)] + _flash_scratch(heads, tk, 2 * tq, LANES, seq),
        compiler_params=_params("parallel", "parallel", "arbitrary"),
        name="diff_attn",
    )(lam_params, subln.reshape(1, LANES), qkv, qkv, qkv)


def _tile_index(ref, r, i, tile):
    per_slab = ref.shape[2]
    if per_slab >= tile:
        runs = per_slab // tile
        return i // runs, r, pl.ds(pl.multiple_of((i % runs) * tile, tile), tile)
    slabs = tile // per_slab
    return pl.ds(i * slabs, slabs), r, slice(None)


def _load_tile(ref, r, i, c, tile):
    slab, res, rows = _tile_index(ref, r, i, tile)
    return ref[slab, res, rows, _lanes(c)].reshape(tile, LANES)


def _store_tile(ref, r, i, c, tile, value):
    slab, res, rows = _tile_index(ref, r, i, tile)
    per_slab = ref.shape[2]
    ref[slab, res, rows, _lanes(c)] = (value if per_slab >= tile
                                       else value.reshape(tile // per_slab, per_slab, LANES))


def _dilated_body(q_ref, k_ref, v_ref, o_ref, lse_ref, *, tile, heads):
    slabs, residues, per_slab, _ = q_ref.shape
    row = lax.broadcasted_iota(jnp.int32, (tile, 1), 0)
    streams = [(r, c) for r in range(residues) for c in range(heads)]

    def attend(r, i, c, k, v, valid):
        s = jnp.where(valid, _dot_nt(_load_tile(q_ref, r, i, c, tile), k), -jnp.inf)
        m = jnp.max(s, axis=-1, keepdims=True)
        p = jnp.exp(s - m)
        l = jnp.sum(p, axis=-1, keepdims=True)
        o = jnp.dot(p.astype(BF16), v, preferred_element_type=F32) / l
        _store_tile(o_ref, r, i, c, tile, o)
        _store_tile(lse_ref, r, i, c, tile, jnp.broadcast_to(m + jnp.log(l), (tile, LANES)))

    col = lax.broadcasted_iota(jnp.int32, (1, tile), 1)
    for r, c in streams:
        attend(r, 0, c, _load_tile(k_ref, r, 0, c, tile), _load_tile(v_ref, r, 0, c, tile), col <= row)

    col2 = lax.broadcasted_iota(jnp.int32, (1, 2 * tile), 1)
    band = jnp.logical_and(col2 >= row, col2 <= row + tile)

    def body(i, _):
        for r, c in streams:
            k = jnp.concatenate([_load_tile(k_ref, r, i - 1, c, tile), _load_tile(k_ref, r, i, c, tile)], axis=0)
            v = jnp.concatenate([_load_tile(v_ref, r, i - 1, c, tile), _load_tile(v_ref, r, i, c, tile)], axis=0)
            attend(r, i, c, k, v, band)
        return 0

    n_tiles = slabs * per_slab // tile
    lax.fori_loop(1, n_tiles, body, 0, unroll=max(1, min(4, n_tiles - 1)))


def _dilated_group(qkv, *, batch, seq, row_tile, group, dil, heads, residues, q_col, k_col, v_col):
    hpg = B_HEADS_PER_GROUP
    tile = B_GROUPS[group][0] // dil
    slabs, per_slab = seq // row_tile, row_tile // dil
    assert tile == LANES and hpg % heads == 0 and (seq // dil) % tile == 0 and dil % residues == 0
    assert per_slab % tile == 0 or tile % per_slab == 0
    width = heads * LANES
    view = qkv.reshape(batch, slabs, dil, per_slab, qkv.shape[1])

    def in_spec(col):
        first = col + group * hpg
        assert first % heads == 0
        return pl.BlockSpec((None, slabs, residues, per_slab, width),
                            lambda b, h, r: (b, 0, r, 0, first // heads + h))

    out_spec = pl.BlockSpec((None, slabs, residues, per_slab, width), lambda b, h, r: (b, 0, r, 0, h))
    out_sds = jax.ShapeDtypeStruct((batch, slabs, dil, per_slab, hpg * LANES), F32)
    o, lse = pl.pallas_call(
        functools.partial(_dilated_body, tile=tile, heads=heads),
        grid=(batch, hpg // heads, dil // residues),
        in_specs=[in_spec(q_col), in_spec(k_col), in_spec(v_col)],
        out_specs=[out_spec, out_spec],
        out_shape=[out_sds, out_sds],
        compiler_params=_params("parallel", "parallel", "parallel"),
        name=f"dilated_g{group}",
    )(view, view, view)
    shape = (batch * slabs, dil, per_slab, hpg * LANES)
    return o.reshape(shape), lse.reshape(shape)


def _merge_body(o0_ref, o1_ref, o2_ref, l0_ref, l1_ref, l2_ref, out_ref, nat_ref):
    def natural(ref, slot, c):
        dil, rows = ref.shape[0], ref.shape[1]
        if dil == 1:
            return ref[0, :, _lanes(c)]
        for r in range(dil):
            nat_ref[slot, pl.ds(r, rows, stride=dil), :] = ref[r, :, _lanes(c)]
        return nat_ref[slot]

    for c in range(out_ref.shape[1] // LANES):
        o0, o1, o2 = natural(o0_ref, None, c), natural(o1_ref, 0, c), natural(o2_ref, 1, c)
        l0, l1, l2 = natural(l0_ref, None, c), natural(l1_ref, 2, c), natural(l2_ref, 3, c)
        m = jnp.maximum(jnp.maximum(l0, l1), l2)
        w0, w1, w2 = jnp.exp(l0 - m), jnp.exp(l1 - m), jnp.exp(l2 - m)
        out_ref[:, _lanes(c)] = ((w0 * o0 + w1 * o1 + w2 * o2) / (w0 + w1 + w2)).astype(out_ref.dtype)


def _merge_groups(outs, lses):
    tiles, _, _, width = outs[0].shape
    row_tile = outs[0].shape[1] * outs[0].shape[2]
    specs = [pl.BlockSpec((None,) + a.shape[1:], lambda i: (i, 0, 0, 0)) for a in (*outs, *lses)]
    return pl.pallas_call(
        _merge_body,
        grid=(tiles,),
        in_specs=specs,
        out_specs=pl.BlockSpec((row_tile, width), lambda i: (i, 0)),
        out_shape=jax.ShapeDtypeStruct((tiles * row_tile, width), BF16),
        scratch_shapes=[pltpu.VMEM((4, row_tile, LANES), F32)],
        compiler_params=_params("parallel"),
        name="dilated_merge",
    )(*outs, *lses)


def _mla_body(q_ref, kv_ref, kr_ref, o_ref, qt_ref, vt_ref, s_ref, p_ref, acc_ref, *, tq, tk, heads):
    qi = pl.program_id(2)

    @pl.when(qi == 0)
    def _():
        _fill_transposed(vt_ref, kv_ref, [(2 * c + 1) * LANES for c in range(heads)], tk)

    def make_score_fn(c):
        qt_ref[c] = q_ref[:, _lanes(c, 2 * LANES)].T

        def score(start, cols):
            k = jnp.concatenate([kv_ref[pl.ds(start, tk), _lanes(2 * c)], kr_ref[pl.ds(start, tk), :]], axis=1)
            return jnp.dot(k, qt_ref[c, :, cols], preferred_element_type=F32)

        return score

    diag = _causal_diag_masks(tk, tq, tq)
    results = _causal_flash_t([make_score_fn(c) for c in range(heads)], vt_ref, s_ref, p_ref, acc_ref,
                              qi * (tq // tk), lambda d, c: diag[d], _causal_col_ranges(tk, tq, 1), tq // tk)
    for c, (l, acc) in enumerate(results):
        o_ref[:, _lanes(c)] = (acc / l).T.astype(o_ref.dtype)


def _mla_attention(q, kv, k_rope_src, *, batch, seq, tq, tk, heads, k_rope_col):
    nq = seq // tq
    assert C_HEADS % heads == 0 and seq % tq == 0 and tq % tk == 0
    return pl.pallas_call(
        functools.partial(_mla_body, tq=tq, tk=tk, heads=heads),
        grid=(batch, C_HEADS // heads, nq),
        in_specs=[pl.BlockSpec((tq, heads * 2 * LANES), lambda b, h, i: (b * nq + i, h)),
                  pl.BlockSpec((seq, heads * 2 * LANES), lambda b, h, i: (b, h)),
                  pl.BlockSpec((seq, LANES), lambda b, h, i: (b, k_rope_col))],
        out_specs=pl.BlockSpec((tq, heads * LANES), lambda b, h, i: (b * nq + i, h)),
        out_shape=jax.ShapeDtypeStruct((batch * seq, C_HEADS * LANES), BF16),
        scratch_shapes=[pltpu.VMEM((heads, 2 * LANES, tq), BF16)] + _flash_scratch(heads, tk, tq, LANES, seq),
        compiler_params=_params("parallel", "parallel", "arbitrary"),
        name="mla_attn",
    )(q, kv, k_rope_src)


def _moba_body(q_ref, k_ref, v_ref, o_ref, kmean_ref, sel_ref, qt_ref, vt_ref, s_ref, p_ref, acc_ref, *,
               n_blk, tq, heads):
    qi = pl.program_id(2)
    blk = MOBA_BLOCK
    n_own = tq // blk

    @pl.when(qi == 0)
    def _():
        for c in range(heads):
            k_all = k_ref[:, _lanes(c)].astype(F32).reshape(n_blk, blk, LANES)
            kmean_ref[c] = jnp.mean(k_all, axis=1)
        _fill_transposed(vt_ref, v_ref, [c * LANES for c in range(heads)], blk)

    blk_id = lax.broadcasted_iota(jnp.int32, (n_blk, 1), 0)
    query_col = lax.broadcasted_iota(jnp.int32, (1, tq), 1)
    own_in_tile = query_col // blk
    past = blk_id < qi * n_own + own_in_tile
    causal = lax.broadcasted_iota(jnp.int32, (blk, 1), 0) <= query_col % blk

    def make_score_fn(c):
        q_t = q_ref[:, _lanes(c)].T
        km = kmean_ref[c]
        km_hi = km.astype(BF16)
        rem = km - km_hi.astype(F32)
        km_mid = rem.astype(BF16)
        km_lo = (rem - km_mid.astype(F32)).astype(BF16)
        gate = (jnp.dot(km_hi, q_t, preferred_element_type=F32) + jnp.dot(km_mid, q_t, preferred_element_type=F32)
                + jnp.dot(km_lo, q_t, preferred_element_type=F32))
        gate = jnp.where(past, gate, -jnp.inf)
        selected = jnp.zeros(gate.shape, F32)
        for _ in range(min(MOBA_TOPK, n_blk - 1)):
            best = jnp.max(gate, axis=0, keepdims=True)
            first = jnp.min(jnp.where(gate == best, blk_id, n_blk), axis=0, keepdims=True)
            pick = blk_id == first
            selected = jnp.where(jnp.logical_and(pick, past), 1.0, selected)
            gate = jnp.where(pick, -jnp.inf, gate)
        sel_ref[c] = selected
        qt_ref[c] = q_t
        return lambda start, cols: jnp.dot(k_ref[pl.ds(start, blk), _lanes(c)], qt_ref[c, :, cols],
                                           preferred_element_type=F32)

    score_fns = [make_score_fn(c) for c in range(heads)]
    n_past = qi * n_own
    chosen_fns = [lambda n, c=c: jnp.logical_or(sel_ref[c, pl.ds(n, 1), :] > 0.0, n >= n_past)
                  for c in range(heads)]

    def own_block_mask(d, c):
        chosen = sel_ref[c, pl.ds(n_past + d, 1), :] > 0.0
        return jnp.logical_or(jnp.logical_and(own_in_tile == d, causal),
                              jnp.logical_and(own_in_tile > d, chosen))

    results = _causal_flash_t(score_fns, vt_ref, s_ref, p_ref, acc_ref, n_past, own_block_mask,
                              _causal_col_ranges(blk, tq, 1), n_own,
                              past_mask_fns=chosen_fns)
    for c, (l, acc) in enumerate(results):
        o_ref[:, _lanes(c)] = (acc / l).T.astype(o_ref.dtype)


def _moba_attention(qkv, *, batch, seq, tq, heads, q_col, k_col, v_col):
    blk = MOBA_BLOCK
    n_blk = seq // blk
    nq = seq // tq
    width = heads * LANES
    assert seq % tq == 0 and tq % blk == 0 and D_HEADS % heads == 0
    assert q_col % heads == 0 and k_col % heads == 0 and v_col % heads == 0
    return pl.pallas_call(
        functools.partial(_moba_body, n_blk=n_blk, tq=tq, heads=heads),
        grid=(batch, D_HEADS // heads, nq),
        in_specs=[pl.BlockSpec((tq, width), lambda b, h, i: (b * nq + i, q_col // heads + h)),
                  pl.BlockSpec((seq, width), lambda b, h, i: (b, k_col // heads + h)),
                  pl.BlockSpec((seq, width), lambda b, h, i: (b, v_col // heads + h))],
        out_specs=pl.BlockSpec((tq, width), lambda b, h, i: (b * nq + i, h)),
        out_shape=jax.ShapeDtypeStruct((batch * seq, D_HEADS * LANES), BF16),
        scratch_shapes=[pltpu.VMEM((heads, n_blk, LANES), F32),
                        pltpu.VMEM((heads, n_blk, tq), F32),
                        pltpu.VMEM((heads, LANES, tq), BF16)] + _flash_scratch(heads, blk, tq, LANES, seq),
        compiler_params=_params("parallel", "parallel", "arbitrary"),
        name="moba_attn",
    )(qkv, qkv, qkv)


def _rope_tables(seq):
    pos = jnp.arange(seq, dtype=F32)[:, None]

    def angles(dim):
        inv = 1.0 / (ROPE_THETA ** (jnp.arange(0, dim, 2, dtype=F32) / dim))
        return pos * inv[None, :]

    a128 = angles(HEAD_DIM)
    a64 = angles(A_QK_DIM)
    c128 = jnp.concatenate([jnp.cos(a128)] * 2, axis=1)
    s128 = jnp.concatenate([-jnp.sin(a128), jnp.sin(a128)], axis=1)
    c64 = jnp.concatenate([jnp.cos(a64)] * 4, axis=1)
    s64 = jnp.concatenate([-jnp.sin(a64)] * 2 + [jnp.sin(a64)] * 2, axis=1)
    return c128, s128, c64, s64


def _interleave_diff_heads(w):
    d = w.shape[0]
    return w.reshape(d, A_HEADS, 2, 2, 32).transpose(0, 1, 3, 2, 4).reshape(d, A_HEADS * LANES)


def _spread_rope64(w):
    z = jnp.zeros(w.shape[:-1] + (32,), w.dtype)
    return jnp.concatenate([w[..., :32], z, w[..., 32:], z], axis=-1)


def _ab_mixer(x, gain, w_in, lam_params, subln, w_out, layer_idx, rope, *, batch, seq):
    aw = A_HEADS * LANES
    w = jnp.concatenate([_interleave_diff_heads(w_in[:, :aw]), _interleave_diff_heads(w_in[:, aw:2 * aw]),
                         w_in[:, 2 * aw:]], axis=1).astype(BF16)
    a_scale = A_QK_DIM ** -0.5 * LOG2_E
    b_scale = HEAD_DIM ** -0.5
    dils = [dil for _, dil in B_GROUPS for _ in range(B_HEADS_PER_GROUP)]
    modes = ([(ROPE64, a_scale, 1)] * A_HEADS + [(ROPE64, 1.0, 1)] * A_HEADS + [(PLAIN, 1.0, 1)] * A_HEADS
             + [(ROPE128, b_scale, d) for d in dils] + [(ROPE128, 1.0, d) for d in dils]
             + [(PLAIN, 1.0, d) for d in dils])
    row_tile = 1024
    qkv = _norm_proj(x, gain, w, rope, modes, seq=seq, tm=row_tile, tn=1280, out_dtype=BF16)
    lam_init = 0.8 - 0.6 * math.exp(-0.3 * layer_idx)
    oa = _diff_attention(qkv, lam_params, subln, batch=batch, seq=seq, tq=1024, tk=256, heads=2,
                         q_col=0, k_col=A_HEADS, v_col=2 * A_HEADS, lam_init=lam_init)
    b0 = 3 * A_HEADS
    outs, lses = [], []
    for g, (_, dil) in enumerate(B_GROUPS):
        o, lse = _dilated_group(qkv, batch=batch, seq=seq, row_tile=row_tile, group=g, dil=dil,
                                heads=2 if dil == 1 else 4, residues=4 if dil == 16 else 1,
                                q_col=b0, k_col=b0 + B_HEADS, v_col=b0 + 2 * B_HEADS)
        outs.append(o)
        lses.append(lse)
    ob = _merge_groups(outs, lses)
    return _out_proj(oa, ob, w_out.astype(BF16), x, tm=1024, tn=1024)


def _cd_mixer(x, gain, w_in, q_norm, w_uq, kv_norm, w_ukv, w_out, rope, *, batch, seq):
    lat = C_Q_LORA + C_KV_LORA
    d_scale = HEAD_DIM ** -0.5 * LOG2_E
    modes = [(ROPE128, d_scale, 1)] * D_HEADS + [(ROPE128, 1.0, 1)] * D_HEADS + [(PLAIN, 1.0, 1)] * D_HEADS
    main = _norm_proj(x, gain, w_in[:, lat + C_ROPE:].astype(BF16), rope, modes, seq=seq, tm=1024, tn=1024,
                      out_dtype=BF16)

    c_scale = (C_NOPE + C_ROPE) ** -0.5 * LOG2_E
    w_q = w_uq.reshape(C_Q_LORA, C_HEADS, C_NOPE + C_ROPE)
    w_q = jnp.concatenate([w_q[..., :C_NOPE], _spread_rope64(w_q[..., C_NOPE:])], axis=-1)
    w_q = w_q.reshape(C_Q_LORA, C_HEADS * 2 * LANES).astype(BF16)
    w_latent = jnp.concatenate([w_in[:, :lat], _spread_rope64(w_in[:, lat:lat + C_ROPE])], axis=1).astype(BF16)
    qc, kv, k_rope = _mla_proj(x, gain, w_latent, q_norm, w_q, kv_norm, w_ukv.astype(BF16), rope,
                               seq=seq, tm=512, q_scale=c_scale)
    oc = _mla_attention(qc, kv, k_rope, batch=batch, seq=seq, tq=1024, tk=256, heads=2, k_rope_col=0)
    od = _moba_attention(main, batch=batch, seq=seq, tq=512, heads=4,
                         q_col=0, k_col=D_HEADS, v_col=2 * D_HEADS)
    return _out_proj(oc, od, w_out.astype(BF16), x, tm=1024, tn=1024)


def kernel(x, p, ffn_norm, ffn_w_gu, ffn_w_down, mix_norm, ab_w_in, ab_lambda, ab_subln, ab_w_out,
           cd_w_in, cd_q_norm, cd_w_uq, cd_kv_norm, cd_w_ukv, cd_w_out, ple_norm, ple_w_gate,
           ple_w_proj, final_norm):
    batch, seq, d = x.shape
    depth = p.shape[0]
    rope = _rope_tables(seq)
    w_gu, w_down = ffn_w_gu.astype(BF16), ffn_w_down.astype(BF16)
    x = x.reshape(batch * seq, d)
    for i in range(depth):
        j = i // 2
        x = _ffn(x, ffn_norm[i, 0], w_gu, w_down, i, 0, tm=1024, tf=512)
        if i % 2 == 0:
            x = _ab_mixer(x, mix_norm[i], ab_w_in[j], ab_lambda[j], ab_subln[j], ab_w_out[j], i, rope,
                          batch=batch, seq=seq)
        else:
            x = _cd_mixer(x, mix_norm[i], cd_w_in[j], cd_q_norm[j], cd_w_uq[j], cd_kv_norm[j], cd_w_ukv[j],
                          cd_w_out[j], rope, batch=batch, seq=seq)
        x = _ffn(x, ffn_norm[i, 1], w_gu, w_down, i, 1, tm=1024, tf=512)
        x = _ple(x, p.reshape(depth * batch * seq, -1), i, ple_norm[i], ple_w_gate[i].astype(BF16),
                 ple_w_proj[i].astype(BF16), final_norm, tm=512, final_norm=(i == depth - 1))
    return x.reshape(batch, seq, d)
```

```python
import functools
import math

import jax
import jax.numpy as jnp
from jax import lax
from jax.experimental import pallas as pl
from jax.experimental.pallas import tpu as pltpu

F32 = jnp.float32
BF16 = jnp.bfloat16

D_MODEL = 2048
D_FF = 5632
PLE_DIM = 256
HEAD_DIM = 128
ROPE_THETA = 10000.0
NORM_EPS = 1e-6

A_HEADS = 8
A_QK_DIM = 64
B_GROUPS = ((128, 1), (512, 4), (2048, 16))
B_HEADS_PER_GROUP = 4
B_HEADS = B_HEADS_PER_GROUP * len(B_GROUPS)
C_HEADS = 8
C_Q_LORA = 512
C_KV_LORA = 256
C_NOPE = 128
C_ROPE = 64
D_HEADS = 8
MOBA_BLOCK = 256
MOBA_TOPK = 3

LOG2_E = 1.0 / math.log(2.0)
PAST_BLOCKS_PER_ITERATION = 4
SUM_ROWS = 16
LANES = 128
VMEM_LIMIT = 56 * 1024 * 1024

PLAIN, ROPE128, ROPE64 = 0, 1, 2


def _rms(x, gain):
    return x * lax.rsqrt(jnp.mean(x * x, axis=-1, keepdims=True) + NORM_EPS) * gain


def _params(*semantics):
    return pltpu.CompilerParams(dimension_semantics=semantics, vmem_limit_bytes=VMEM_LIMIT)


def _norm_proj_body(x_ref, g_ref, w_ref, c128_ref, s128_ref, c64_ref, s64_ref, o_ref, xn_ref, stage_ref, *,
                    tile_patterns):
    j = pl.program_id(1)
    tm = o_ref.shape[0]

    @pl.when(j == 0)
    def _():
        xn_ref[...] = _rms(x_ref[...], g_ref[...]).astype(BF16)

    def tile(pattern):
        acc = jnp.dot(xn_ref[...], w_ref[...], preferred_element_type=F32)
        for c, (kind, scale, dil) in enumerate(pattern):
            a = acc[:, _lanes(c)]
            if scale != 1.0:
                a = a * scale
            if kind == ROPE128:
                a = a * c128_ref[...] + pltpu.roll(a, 64, 1) * s128_ref[...]
            elif kind == ROPE64:
                a = a * c64_ref[...] + pltpu.roll(a, 64, 1) * s64_ref[...]
            if dil == 1:
                o_ref[:, _lanes(c)] = a.astype(o_ref.dtype)
            else:
                rows = tm // dil
                stage_ref[c] = a
                for r in range(dil):
                    o_ref[r * rows:(r + 1) * rows, _lanes(c)] = (
                        stage_ref[c, pl.ds(r, rows, stride=dil), :].astype(o_ref.dtype))

    distinct = sorted(set(tile_patterns), key=tile_patterns.index)
    if len(distinct) == 1:
        tile(distinct[0])
    else:
        for pattern in distinct:
            tiles = [t for t, p in enumerate(tile_patterns) if p == pattern]
            cond = functools.reduce(jnp.logical_or, [j == t for t in tiles])
            pl.when(cond)(functools.partial(tile, pattern))


def _norm_proj(x, gain, w, rope, chunk_modes, *, seq, tm, tn, out_dtype):
    t, k = x.shape
    n = w.shape[1]
    per_tile = tn // LANES
    assert t % tm == 0 and n % tn == 0 and seq % tm == 0 and len(chunk_modes) * LANES == n
    tile_patterns = tuple(tuple(chunk_modes[a * per_tile:(a + 1) * per_tile]) for a in range(n // tn))
    seq_tiles = seq // tm
    tab = pl.BlockSpec((tm, LANES), lambda i, j: (i % seq_tiles, 0))
    return pl.pallas_call(
        functools.partial(_norm_proj_body, tile_patterns=tile_patterns),
        grid=(t // tm, n // tn),
        in_specs=[pl.BlockSpec((tm, k), lambda i, j: (i, 0)),
                  pl.BlockSpec((1, k), lambda i, j: (0, 0)),
                  pl.BlockSpec((k, tn), lambda i, j: (0, j)),
                  tab, tab, tab, tab],
        out_specs=pl.BlockSpec((tm, tn), lambda i, j: (i, j)),
        out_shape=jax.ShapeDtypeStruct((t, n), out_dtype),
        scratch_shapes=[pltpu.VMEM((tm, k), BF16), pltpu.VMEM((per_tile, tm, LANES), F32)],
        compiler_params=_params("parallel", "arbitrary"),
        name="norm_proj",
    )(x, gain.reshape(1, k), w, *rope)


def _out_proj_body(a1_ref, a2_ref, w1_ref, w2_ref, res_ref, o_ref):
    o_ref[...] = (res_ref[...] + jnp.dot(a1_ref[...], w1_ref[...], preferred_element_type=F32)
                  + jnp.dot(a2_ref[...], w2_ref[...], preferred_element_type=F32))


def _out_proj(a1, a2, w, res, *, tm, tn):
    t, k1 = a1.shape
    k2 = a2.shape[1]
    n = w.shape[1]
    assert t % tm == 0 and n % tn == 0 and w.shape[0] == k1 + k2 and k1 % k2 == 0
    return pl.pallas_call(
        _out_proj_body,
        grid=(t // tm, n // tn),
        in_specs=[pl.BlockSpec((tm, k1), lambda i, j: (i, 0)),
                  pl.BlockSpec((tm, k2), lambda i, j: (i, 0)),
                  pl.BlockSpec((k1, tn), lambda i, j: (0, j)),
                  pl.BlockSpec((k2, tn), lambda i, j: (k1 // k2, j)),
                  pl.BlockSpec((tm, tn), lambda i, j: (i, j))],
        out_specs=pl.BlockSpec((tm, tn), lambda i, j: (i, j)),
        out_shape=jax.ShapeDtypeStruct((t, n), F32),
        compiler_params=_params("parallel", "parallel"),
        name="out_proj",
    )(a1, a2, w, w, res)


def _mla_proj_body(x_ref, g_ref, wl_ref, qn_ref, wq_ref, kvn_ref, wkv_ref, c64_ref, s64_ref, q_ref, kv_ref, kr_ref,
                   *, q_lora, kv_lora, q_scale):
    half = x_ref.shape[0] // 2
    lat = q_lora + kv_lora
    for rows in (slice(0, half), slice(half, 2 * half)):
        xn = _rms(x_ref[rows, :], g_ref[...]).astype(BF16)
        latent = jnp.dot(xn, wl_ref[...], preferred_element_type=F32)
        k_rope = latent[:, lat:]
        kr_ref[rows, :] = (k_rope * c64_ref[rows, :] + pltpu.roll(k_rope, 64, 1) * s64_ref[rows, :]).astype(kr_ref.dtype)
        c_q = _rms(latent[:, :q_lora], qn_ref[...]).astype(BF16)
        c_kv = _rms(latent[:, q_lora:lat], kvn_ref[...]).astype(BF16)
        kv_ref[rows, :] = jnp.dot(c_kv, wkv_ref[...], preferred_element_type=F32).astype(kv_ref.dtype)
        q = jnp.dot(c_q, wq_ref[...], preferred_element_type=F32) * q_scale
        for c in range(q_ref.shape[1] // LANES):
            a = q[:, _lanes(c)]
            if c % 2 == 1:
                a = a * c64_ref[rows, :] + pltpu.roll(a, 64, 1) * s64_ref[rows, :]
            q_ref[rows, _lanes(c)] = a.astype(q_ref.dtype)


def _mla_proj(x, gain, w_latent, q_norm, w_q, kv_norm, w_kv, rope, *, seq, tm, q_scale):
    t, d = x.shape
    q_lora, kv_lora = w_q.shape[0], w_kv.shape[0]
    assert t % tm == 0 and seq % tm == 0 and w_latent.shape[1] == q_lora + kv_lora + LANES
    seq_tiles = seq // tm
    const = lambda i: (0, 0)
    tab = pl.BlockSpec((tm, LANES), lambda i: (i % seq_tiles, 0))
    full = lambda a: pl.BlockSpec(a.shape, const)
    out_spec = lambda width: pl.BlockSpec((tm, width), lambda i: (i, 0))
    widths = (w_q.shape[1], w_kv.shape[1], LANES)
    return pl.pallas_call(
        functools.partial(_mla_proj_body, q_lora=q_lora, kv_lora=kv_lora, q_scale=q_scale),
        grid=(t // tm,),
        in_specs=[pl.BlockSpec((tm, d), lambda i: (i, 0)), pl.BlockSpec((1, d), const), full(w_latent),
                  pl.BlockSpec((1, q_lora), const), full(w_q), pl.BlockSpec((1, kv_lora), const), full(w_kv),
                  tab, tab],
        out_specs=[out_spec(w) for w in widths],
        out_shape=[jax.ShapeDtypeStruct((t, w), BF16) for w in widths],
        compiler_params=_params("parallel"),
        name="mla_proj",
    )(x, gain.reshape(1, d), w_latent, q_norm.reshape(1, q_lora), w_q, kv_norm.reshape(1, kv_lora), w_kv,
      rope[2], rope[3])


def _ffn_body(x_ref, g_ref, wg_ref, wu_ref, wd_ref, o_ref, xn_ref):
    j = pl.program_id(1)

    @pl.when(j == 0)
    def _():
        xn_ref[...] = _rms(x_ref[...], g_ref[...]).astype(BF16)
        o_ref[...] = jnp.zeros_like(o_ref)

    xn = xn_ref[...]
    g = jnp.dot(xn, wg_ref[...], preferred_element_type=F32)
    u = jnp.dot(xn, wu_ref[...], preferred_element_type=F32)
    h = (g * jax.nn.sigmoid(g) * u).astype(BF16)
    o_ref[...] += jnp.dot(h, wd_ref[...], preferred_element_type=F32)

    @pl.when(j == pl.num_programs(1) - 1)
    def _():
        o_ref[...] = x_ref[...] + 0.5 * o_ref[...]


def _ffn(x, gain, w_gu, w_down, layer, half, *, tm, tf):
    t, d = x.shape
    d_ff = w_down.shape[2]
    assert t % tm == 0 and d_ff % tf == 0
    nf = d_ff // tf
    return pl.pallas_call(
        _ffn_body,
        grid=(t // tm, nf),
        in_specs=[pl.BlockSpec((tm, d), lambda i, j: (i, 0)),
                  pl.BlockSpec((1, d), lambda i, j: (0, 0)),
                  pl.BlockSpec((None, None, d, tf), lambda i, j: (layer, half, 0, j)),
                  pl.BlockSpec((None, None, d, tf), lambda i, j: (layer, half, 0, nf + j)),
                  pl.BlockSpec((None, None, tf, d), lambda i, j: (layer, half, j, 0))],
        out_specs=pl.BlockSpec((tm, d), lambda i, j: (i, 0)),
        out_shape=jax.ShapeDtypeStruct((t, d), F32),
        scratch_shapes=[pltpu.VMEM((tm, d), BF16)],
        compiler_params=_params("parallel", "arbitrary"),
        name="ffn",
    )(x, gain.reshape(1, d), w_gu, w_gu, w_down)


def _ple_body(x_ref, p_ref, g_ref, wg_ref, wp_ref, fg_ref, o_ref, *, final_norm):
    half = x_ref.shape[0] // 2
    for rows in (slice(0, half), slice(half, 2 * half)):
        x = x_ref[rows, :]
        xn = _rms(x, g_ref[...]).astype(BF16)
        gate = jax.nn.sigmoid(jnp.dot(xn, wg_ref[...], preferred_element_type=F32))
        proj = jnp.dot(p_ref[rows, :].astype(BF16), wp_ref[...], preferred_element_type=F32)
        y = x + gate * proj
        if final_norm:
            y = _rms(y, fg_ref[...])
        o_ref[rows, :] = y


def _ple(x, p, layer, gain, w_gate, w_proj, final_gain, *, tm, final_norm):
    t, d = x.shape
    pd = p.shape[1]
    assert t % tm == 0
    first_tile = layer * (t // tm)
    const = lambda i: (0, 0)
    return pl.pallas_call(
        functools.partial(_ple_body, final_norm=final_norm),
        grid=(t // tm,),
        in_specs=[pl.BlockSpec((tm, d), lambda i: (i, 0)),
                  pl.BlockSpec((tm, pd), lambda i: (first_tile + i, 0)),
                  pl.BlockSpec((1, d), const),
                  pl.BlockSpec((d, d), const),
                  pl.BlockSpec((pd, d), const),
                  pl.BlockSpec((1, d), const)],
        out_specs=pl.BlockSpec((tm, d), lambda i: (i, 0)),
        out_shape=jax.ShapeDtypeStruct((t, d), F32),
        compiler_params=_params("parallel"),
        name="ple",
    )(x, p, gain.reshape(1, d), w_gate, w_proj, final_gain.reshape(1, d))


def _dot_nt(a, b):
    return lax.dot_general(a, b, (((1,), (1,)), ((), ())), preferred_element_type=F32)


def _lanes(c, width=LANES):
    return slice(c * width, (c + 1) * width)


def _fill_transposed(vt_ref, v_ref, lane_offsets, chunk):
    dv = vt_ref.shape[1] - SUM_ROWS
    ones_row = jnp.where(lax.broadcasted_iota(jnp.int32, (SUM_ROWS, chunk), 0) == 0, 1.0, 0.0).astype(BF16)

    def body(j, _):
        start = pl.multiple_of(j * chunk, chunk)
        for c, off in enumerate(lane_offsets):
            vt_ref[c, :dv, pl.ds(start, chunk)] = v_ref[pl.ds(start, chunk), off:off + dv].T
            vt_ref[c, dv:, pl.ds(start, chunk)] = ones_row
        return 0

    lax.fori_loop(0, v_ref.shape[0] // chunk, body, 0)


def _flash_scratch(heads, tk, cols, dv, seq):
    return [pltpu.VMEM((heads, dv + SUM_ROWS, seq), BF16),
            pltpu.VMEM((heads, tk, cols), F32), pltpu.VMEM((heads, tk, cols), BF16),
            pltpu.VMEM((heads, dv + SUM_ROWS, cols), F32)]


def _causal_diag_masks(tk, tq, cols):
    key_row = lax.broadcasted_iota(jnp.int32, (tk, 1), 0)
    query_col = lax.broadcasted_iota(jnp.int32, (1, cols), 1) % tq
    return [key_row + d * tk <= query_col for d in range(tq // tk)]


def _causal_col_ranges(tk, tq, maps):
    return lambda d: [(g * tq + d * tk, (g + 1) * tq) for g in range(maps)]


def _causal_flash_t(score_fns, vt_ref, s_ref, p_ref, acc_ref, n_past, diag_mask_fn, diag_cols_fn, n_diag,
                    past_mask_fns=None):
    heads = range(len(score_fns))
    _, tk, cols = s_ref.shape

    everything = [(0, cols)]

    def stage_scores(block, ranges=everything):
        start = pl.multiple_of(block * tk, tk)
        for c in heads:
            for lo, hi in ranges:
                s = score_fns[c](start, slice(lo, hi))
                if past_mask_fns is not None:
                    s = jnp.where(past_mask_fns[c](block)[:, lo:hi], s, -jnp.inf)
                s_ref[c, :, lo:hi] = s

    def accumulate(block, alphas, ranges=everything):
        start = pl.multiple_of(block * tk, tk)
        for c in heads:
            for lo, hi in ranges:
                pv = jnp.dot(vt_ref[c, :, pl.ds(start, tk)], p_ref[c, :, lo:hi], preferred_element_type=F32)
                acc_ref[c, :, lo:hi] = alphas[c][:, lo:hi] * acc_ref[c, :, lo:hi] + pv

    def softmax(stats, diag=None, ranges=everything):
        new_stats, alphas = [], []
        for c in heads:
            m = stats[c]
            if diag is None:
                read = lambda lo, hi: s_ref[c, :, lo:hi]
            else:
                visible = diag_mask_fn(diag, c)
                read = lambda lo, hi: jnp.where(visible[:, lo:hi], s_ref[c, :, lo:hi], -jnp.inf)
            pieces, at = [], 0
            for lo, hi in ranges:
                if lo > at:
                    pieces.append(m[:, at:lo])
                pieces.append(jnp.maximum(m[:, lo:hi], jnp.max(read(lo, hi), axis=0, keepdims=True)))
                at = hi
            if at < cols:
                pieces.append(m[:, at:])
            m_new = pieces[0] if len(pieces) == 1 else jnp.concatenate(pieces, axis=1)
            alpha = jnp.exp2(m - m_new)
            for lo, hi in ranges:
                p_ref[c, :, lo:hi] = jnp.exp2(read(lo, hi) - m_new[:, lo:hi]).astype(BF16)
            new_stats.append(m_new)
            alphas.append(alpha)
        return tuple(new_stats), tuple(alphas)

    def past_block(j, carry):
        stats, alphas = carry
        accumulate(jnp.maximum(j - 1, 0), alphas)
        stats, alphas = softmax(stats)
        stage_scores(j + 1)
        return stats, alphas

    for c in heads:
        p_ref[c] = jnp.zeros(p_ref.shape[1:], BF16)
        acc_ref[c] = jnp.zeros(acc_ref.shape[1:], F32)
    stage_scores(0)
    init = (tuple(jnp.full((1, cols), jnp.finfo(F32).min, F32) for _ in heads),
            tuple(jnp.ones((1, cols), F32) for _ in heads))
    def past_group(t, carry):
        for u in range(PAST_BLOCKS_PER_ITERATION):
            carry = past_block(t * PAST_BLOCKS_PER_ITERATION + u, carry)
        return carry

    n_grouped = n_past // PAST_BLOCKS_PER_ITERATION
    carry = lax.fori_loop(0, n_grouped, past_group, init)
    stats, alphas = lax.fori_loop(n_grouped * PAST_BLOCKS_PER_ITERATION, n_past, past_block, carry)

    accumulate(jnp.maximum(n_past - 1, 0), alphas)
    for d in range(n_diag):
        stats, alphas = softmax(stats, diag=d, ranges=diag_cols_fn(d))
        if d + 1 < n_diag:
            stage_scores(n_past + d + 1, ranges=diag_cols_fn(d + 1))
        accumulate(n_past + d, alphas, ranges=diag_cols_fn(d))
    dv = acc_ref.shape[1] - SUM_ROWS
    return tuple((acc_ref[c, dv:dv + 1, :], acc_ref[c, :dv, :]) for c in heads)


def _diff_attn_body(lam_ref, subln_ref, q_ref, k_ref, v_ref, o_ref, qt_ref, vt_ref, s_ref, p_ref, acc_ref, *,
                    tq, tk, heads, lam_init):
    qi = pl.program_id(2)

    @pl.when(qi == 0)
    def _():
        _fill_transposed(vt_ref, v_ref, [c * LANES for c in range(heads)], tk)

    is_map0 = (lax.broadcasted_iota(jnp.int32, (LANES, 1), 0) % 64) < 32

    def make_score_fn(c):
        q_t = q_ref[:, _lanes(c)].T
        zero = jnp.zeros_like(q_t)
        qt_ref[c] = jnp.concatenate([jnp.where(is_map0, q_t, zero), jnp.where(is_map0, zero, q_t)], axis=1)
        return lambda start, cols: jnp.dot(k_ref[pl.ds(start, tk), _lanes(c)], qt_ref[c, :, cols],
                                           preferred_element_type=F32)

    diag = _causal_diag_masks(tk, tq, 2 * tq)
    results = _causal_flash_t([make_score_fn(c) for c in range(heads)], vt_ref, s_ref, p_ref, acc_ref,
                              qi * (tq // tk), lambda d, c: diag[d], _causal_col_ranges(tk, tq, 2), tq // tk)

    lp = lam_ref[...]
    lam = (jnp.exp(jnp.sum(lp[0:1] * lp[1:2], axis=-1, keepdims=True))
           - jnp.exp(jnp.sum(lp[2:3] * lp[3:4], axis=-1, keepdims=True)) + lam_init)
    for c, (l, acc) in enumerate(results):
        o_t = acc / l
        o = (o_t[:, :tq] - lam * o_t[:, tq:]).T
        o_ref[:, _lanes(c)] = (_rms(o, subln_ref[...]) * (1.0 - lam_init)).astype(o_ref.dtype)


def _diff_attention(qkv, lam_params, subln, *, batch, seq, tq, tk, heads, q_col, k_col, v_col, lam_init):
    nq = seq // tq
    width = heads * LANES
    assert A_HEADS % heads == 0 and q_col % heads == 0 and k_col % heads == 0 and v_col % heads == 0
    assert seq % tq == 0 and tq % tk == 0
    return pl.pallas_call(
        functools.partial(_diff_attn_body, tq=tq, tk=tk, heads=heads, lam_init=lam_init),
        grid=(batch, A_HEADS // heads, nq),
        in_specs=[pl.BlockSpec((4, A_QK_DIM), lambda b, h, i: (0, 0)),
                  pl.BlockSpec((1, LANES), lambda b, h, i: (0, 0)),
                  pl.BlockSpec((tq, width), lambda b, h, i: (b * nq + i, q_col // heads + h)),
                  pl.BlockSpec((seq, width), lambda b, h, i: (b, k_col // heads + h)),
                  pl.BlockSpec((seq, width), lambda b, h, i: (b, v_col // heads + h))],
        out_specs=pl.BlockSpec((tq, width), lambda b, h, i: (b * nq + i, h)),
        out_shape=jax.ShapeDtypeStruct((batch * seq, A_HEADS * LANES), BF16),
        scratch_shapes=[pltpu.VMEM((heads, LANES, 2 * tq), BF16)] + _flash_scratch(heads, tk, 2 * tq, LANES, seq),
        compiler_params=_params("parallel", "parallel", "arbitrary"),
        name="diff_attn",
    )(lam_params, subln.reshape(1, LANES), qkv, qkv, qkv)


def _tile_index(ref, r, i, tile):
    per_slab = ref.shape[2]
    if per_slab >= tile:
        runs = per_slab // tile
        return i // runs, r, pl.ds(pl.multiple_of((i % runs) * tile, tile), tile)
    slabs = tile // per_slab
    return pl.ds(i * slabs, slabs), r, slice(None)


def _load_tile(ref, r, i, c, tile):
    slab, res, rows = _tile_index(ref, r, i, tile)
    return ref[slab, res, rows, _lanes(c)].reshape(tile, LANES)


def _store_tile(ref, r, i, c, tile, value):
    slab, res, rows = _tile_index(ref, r, i, tile)
    per_slab = ref.shape[2]
    ref[slab, res, rows, _lanes(c)] = (value if per_slab >= tile
                                       else value.reshape(tile // per_slab, per_slab, LANES))


def _dilated_body(q_ref, k_ref, v_ref, o_ref, lse_ref, *, tile, heads):
    slabs, residues, per_slab, _ = q_ref.shape
    row = lax.broadcasted_iota(jnp.int32, (tile, 1), 0)
    streams = [(r, c) for r in range(residues) for c in range(heads)]

    def attend(r, i, c, k, v, valid):
        s = jnp.where(valid, _dot_nt(_load_tile(q_ref, r, i, c, tile), k), -jnp.inf)
        m = jnp.max(s, axis=-1, keepdims=True)
        p = jnp.exp(s - m)
        l = jnp.sum(p, axis=-1, keepdims=True)
        o = jnp.dot(p.astype(BF16), v, preferred_element_type=F32) / l
        _store_tile(o_ref, r, i, c, tile, o)
        _store_tile(lse_ref, r, i, c, tile, jnp.broadcast_to(m + jnp.log(l), (tile, LANES)))

    col = lax.broadcasted_iota(jnp.int32, (1, tile), 1)
    for r, c in streams:
        attend(r, 0, c, _load_tile(k_ref, r, 0, c, tile), _load_tile(v_ref, r, 0, c, tile), col <= row)

    col2 = lax.broadcasted_iota(jnp.int32, (1, 2 * tile), 1)
    band = jnp.logical_and(col2 >= row, col2 <= row + tile)

    def body(i, _):
        for r, c in streams:
            k = jnp.concatenate([_load_tile(k_ref, r, i - 1, c, tile), _load_tile(k_ref, r, i, c, tile)], axis=0)
            v = jnp.concatenate([_load_tile(v_ref, r, i - 1, c, tile), _load_tile(v_ref, r, i, c, tile)], axis=0)
            attend(r, i, c, k, v, band)
        return 0

    n_tiles = slabs * per_slab // tile
    lax.fori_loop(1, n_tiles, body, 0, unroll=max(1, min(4, n_tiles - 1)))


def _dilated_group(qkv, *, batch, seq, row_tile, group, dil, heads, residues, q_col, k_col, v_col):
    hpg = B_HEADS_PER_GROUP
    tile = B_GROUPS[group][0] // dil
    slabs, per_slab = seq // row_tile, row_tile // dil
    assert tile == LANES and hpg % heads == 0 and (seq // dil) % tile == 0 and dil % residues == 0
    assert per_slab % tile == 0 or tile % per_slab == 0
    width = heads * LANES
    view = qkv.reshape(batch, slabs, dil, per_slab, qkv.shape[1])

    def in_spec(col):
        first = col + group * hpg
        assert first % heads == 0
        return pl.BlockSpec((None, slabs, residues, per_slab, width),
                            lambda b, h, r: (b, 0, r, 0, first // heads + h))

    out_spec = pl.BlockSpec((None, slabs, residues, per_slab, width), lambda b, h, r: (b, 0, r, 0, h))
    out_sds = jax.ShapeDtypeStruct((batch, slabs, dil, per_slab, hpg * LANES), F32)
    o, lse = pl.pallas_call(
        functools.partial(_dilated_body, tile=tile, heads=heads),
        grid=(batch, hpg // heads, dil // residues),
        in_specs=[in_spec(q_col), in_spec(k_col), in_spec(v_col)],
        out_specs=[out_spec, out_spec],
        out_shape=[out_sds, out_sds],
        compiler_params=_params("parallel", "parallel", "parallel"),
        name=f"dilated_g{group}",
    )(view, view, view)
    shape = (batch * slabs, dil, per_slab, hpg * LANES)
    return o.reshape(shape), lse.reshape(shape)


def _merge_body(o0_ref, o1_ref, o2_ref, l0_ref, l1_ref, l2_ref, out_ref, nat_ref):
    def natural(ref, slot, c):
        dil, rows = ref.shape[0], ref.shape[1]
        if dil == 1:
            return ref[0, :, _lanes(c)]
        for r in range(dil):
            nat_ref[slot, pl.ds(r, rows, stride=dil), :] = ref[r, :, _lanes(c)]
        return nat_ref[slot]

    for c in range(out_ref.shape[1] // LANES):
        o0, o1, o2 = natural(o0_ref, None, c), natural(o1_ref, 0, c), natural(o2_ref, 1, c)
        l0, l1, l2 = natural(l0_ref, None, c), natural(l1_ref, 2, c), natural(l2_ref, 3, c)
        m = jnp.maximum(jnp.maximum(l0, l1), l2)
        w0, w1, w2 = jnp.exp(l0 - m), jnp.exp(l1 - m), jnp.exp(l2 - m)
        out_ref[:, _lanes(c)] = ((w0 * o0 + w1 * o1 + w2 * o2) / (w0 + w1 + w2)).astype(out_ref.dtype)


def _merge_groups(outs, lses):
    tiles, _, _, width = outs[0].shape
    row_tile = outs[0].shape[1] * outs[0].shape[2]
    specs = [pl.BlockSpec((None,) + a.shape[1:], lambda i: (i, 0, 0, 0)) for a in (*outs, *lses)]
    return pl.pallas_call(
        _merge_body,
        grid=(tiles,),
        in_specs=specs,
        out_specs=pl.BlockSpec((row_tile, width), lambda i: (i, 0)),
        out_shape=jax.ShapeDtypeStruct((tiles * row_tile, width), BF16),
        scratch_shapes=[pltpu.VMEM((4, row_tile, LANES), F32)],
        compiler_params=_params("parallel"),
        name="dilated_merge",
    )(*outs, *lses)


def _mla_body(q_ref, kv_ref, kr_ref, o_ref, qt_ref, vt_ref, s_ref, p_ref, acc_ref, *, tq, tk, heads):
    qi = pl.program_id(2)

    @pl.when(qi == 0)
    def _():
        _fill_transposed(vt_ref, kv_ref, [(2 * c + 1) * LANES for c in range(heads)], tk)

    def make_score_fn(c):
        qt_ref[c] = q_ref[:, _lanes(c, 2 * LANES)].T

        def score(start, cols):
            k = jnp.concatenate([kv_ref[pl.ds(start, tk), _lanes(2 * c)], kr_ref[pl.ds(start, tk), :]], axis=1)
            return jnp.dot(k, qt_ref[c, :, cols], preferred_element_type=F32)

        return score

    diag = _causal_diag_masks(tk, tq, tq)
    results = _causal_flash_t([make_score_fn(c) for c in range(heads)], vt_ref, s_ref, p_ref, acc_ref,
                              qi * (tq // tk), lambda d, c: diag[d], _causal_col_ranges(tk, tq, 1), tq // tk)
    for c, (l, acc) in enumerate(results):
        o_ref[:, _lanes(c)] = (acc / l).T.astype(o_ref.dtype)


def _mla_attention(q, kv, k_rope_src, *, batch, seq, tq, tk, heads, k_rope_col):
    nq = seq // tq
    assert C_HEADS % heads == 0 and seq % tq == 0 and tq % tk == 0
    return pl.pallas_call(
        functools.partial(_mla_body, tq=tq, tk=tk, heads=heads),
        grid=(batch, C_HEADS // heads, nq),
        in_specs=[pl.BlockSpec((tq, heads * 2 * LANES), lambda b, h, i: (b * nq + i, h)),
                  pl.BlockSpec((seq, heads * 2 * LANES), lambda b, h, i: (b, h)),
                  pl.BlockSpec((seq, LANES), lambda b, h, i: (b, k_rope_col))],
        out_specs=pl.BlockSpec((tq, heads * LANES), lambda b, h, i: (b * nq + i, h)),
        out_shape=jax.ShapeDtypeStruct((batch * seq, C_HEADS * LANES), BF16),
        scratch_shapes=[pltpu.VMEM((heads, 2 * LANES, tq), BF16)] + _flash_scratch(heads, tk, tq, LANES, seq),
        compiler_params=_params("parallel", "parallel", "arbitrary"),
        name="mla_attn",
    )(q, kv, k_rope_src)


def _moba_body(q_ref, k_ref, v_ref, o_ref, kmean_ref, sel_ref, qt_ref, vt_ref, s_ref, p_ref, acc_ref, *,
               n_blk, tq, heads):
    qi = pl.program_id(2)
    blk = MOBA_BLOCK
    n_own = tq // blk

    @pl.when(qi == 0)
    def _():
        for c in range(heads):
            k_all = k_ref[:, _lanes(c)].astype(F32).reshape(n_blk, blk, LANES)
            kmean_ref[c] = jnp.mean(k_all, axis=1)
        _fill_transposed(vt_ref, v_ref, [c * LANES for c in range(heads)], blk)

    blk_id = lax.broadcasted_iota(jnp.int32, (n_blk, 1), 0)
    query_col = lax.broadcasted_iota(jnp.int32, (1, tq), 1)
    own_in_tile = query_col // blk
    past = blk_id < qi * n_own + own_in_tile
    causal = lax.broadcasted_iota(jnp.int32, (blk, 1), 0) <= query_col % blk

    def make_score_fn(c):
        q_t = q_ref[:, _lanes(c)].T
        km = kmean_ref[c]
        km_hi = km.astype(BF16)
        rem = km - km_hi.astype(F32)
        km_mid = rem.astype(BF16)
        km_lo = (rem - km_mid.astype(F32)).astype(BF16)
        gate = (jnp.dot(km_hi, q_t, preferred_element_type=F32) + jnp.dot(km_mid, q_t, preferred_element_type=F32)
                + jnp.dot(km_lo, q_t, preferred_element_type=F32))
        gate = jnp.where(past, gate, -jnp.inf)
        selected = jnp.zeros(gate.shape, F32)
        for _ in range(min(MOBA_TOPK, n_blk - 1)):
            best = jnp.max(gate, axis=0, keepdims=True)
            first = jnp.min(jnp.where(gate == best, blk_id, n_blk), axis=0, keepdims=True)
            pick = blk_id == first
            selected = jnp.where(jnp.logical_and(pick, past), 1.0, selected)
            gate = jnp.where(pick, -jnp.inf, gate)
        sel_ref[c] = selected
        qt_ref[c] = q_t
        return lambda start, cols: jnp.dot(k_ref[pl.ds(start, blk), _lanes(c)], qt_ref[c, :, cols],
                                           preferred_element_type=F32)

    score_fns = [make_score_fn(c) for c in range(heads)]
    n_past = qi * n_own
    chosen_fns = [lambda n, c=c: jnp.logical_or(sel_ref[c, pl.ds(n, 1), :] > 0.0, n >= n_past)
                  for c in range(heads)]

    def own_block_mask(d, c):
        chosen = sel_ref[c, pl.ds(n_past + d, 1), :] > 0.0
        return jnp.logical_or(jnp.logical_and(own_in_tile == d, causal),
                              jnp.logical_and(own_in_tile > d, chosen))

    results = _causal_flash_t(score_fns, vt_ref, s_ref, p_ref, acc_ref, n_past, own_block_mask,
                              _causal_col_ranges(blk, tq, 1), n_own,
                              past_mask_fns=chosen_fns)
    for c, (l, acc) in enumerate(results):
        o_ref[:, _lanes(c)] = (acc / l).T.astype(o_ref.dtype)


def _moba_attention(qkv, *, batch, seq, tq, heads, q_col, k_col, v_col):
    blk = MOBA_BLOCK
    n_blk = seq // blk
    nq = seq // tq
    width = heads * LANES
    assert seq % tq == 0 and tq % blk == 0 and D_HEADS % heads == 0
    assert q_col % heads == 0 and k_col % heads == 0 and v_col % heads == 0
    return pl.pallas_call(
        functools.partial(_moba_body, n_blk=n_blk, tq=tq, heads=heads),
        grid=(batch, D_HEADS // heads, nq),
        in_specs=[pl.BlockSpec((tq, width), lambda b, h, i: (b * nq + i, q_col // heads + h)),
                  pl.BlockSpec((seq, width), lambda b, h, i: (b, k_col // heads + h)),
                  pl.BlockSpec((seq, width), lambda b, h, i: (b, v_col // heads + h))],
        out_specs=pl.BlockSpec((tq, width), lambda b, h, i: (b * nq + i, h)),
        out_shape=jax.ShapeDtypeStruct((batch * seq, D_HEADS * LANES), BF16),
        scratch_shapes=[pltpu.VMEM((heads, n_blk, LANES), F32),
                        pltpu.VMEM((heads, n_blk, tq), F32),
                        pltpu.VMEM((heads, LANES, tq), BF16)] + _flash_scratch(heads, blk, tq, LANES, seq),
        compiler_params=_params("parallel", "parallel", "arbitrary"),
        name="moba_attn",
    )(qkv, qkv, qkv)


def _rope_tables(seq):
    pos = jnp.arange(seq, dtype=F32)[:, None]

    def angles(dim):
        inv = 1.0 / (ROPE_THETA ** (jnp.arange(0, dim, 2, dtype=F32) / dim))
        return pos * inv[None, :]

    a128 = angles(HEAD_DIM)
    a64 = angles(A_QK_DIM)
    c128 = jnp.concatenate([jnp.cos(a128)] * 2, axis=1)
    s128 = jnp.concatenate([-jnp.sin(a128), jnp.sin(a128)], axis=1)
    c64 = jnp.concatenate([jnp.cos(a64)] * 4, axis=1)
    s64 = jnp.concatenate([-jnp.sin(a64)] * 2 + [jnp.sin(a64)] * 2, axis=1)
    return c128, s128, c64, s64


def _interleave_diff_heads(w):
    d = w.shape[0]
    return w.reshape(d, A_HEADS, 2, 2, 32).transpose(0, 1, 3, 2, 4).reshape(d, A_HEADS * LANES)


def _spread_rope64(w):
    z = jnp.zeros(w.shape[:-1] + (32,), w.dtype)
    return jnp.concatenate([w[..., :32], z, w[..., 32:], z], axis=-1)


def _ab_mixer(x, gain, w_in, lam_params, subln, w_out, layer_idx, rope, *, batch, seq):
    aw = A_HEADS * LANES
    w = jnp.concatenate([_interleave_diff_heads(w_in[:, :aw]), _interleave_diff_heads(w_in[:, aw:2 * aw]),
                         w_in[:, 2 * aw:]], axis=1).astype(BF16)
    a_scale = A_QK_DIM ** -0.5 * LOG2_E
    b_scale = HEAD_DIM ** -0.5
    dils = [dil for _, dil in B_GROUPS for _ in range(B_HEADS_PER_GROUP)]
    modes = ([(ROPE64, a_scale, 1)] * A_HEADS + [(ROPE64, 1.0, 1)] * A_HEADS + [(PLAIN, 1.0, 1)] * A_HEADS
             + [(ROPE128, b_scale, d) for d in dils] + [(ROPE128, 1.0, d) for d in dils]
             + [(PLAIN, 1.0, d) for d in dils])
    row_tile = 1024
    qkv = _norm_proj(x, gain, w, rope, modes, seq=seq, tm=row_tile, tn=1280, out_dtype=BF16)
    lam_init = 0.8 - 0.6 * math.exp(-0.3 * layer_idx)
    oa = _diff_attention(qkv, lam_params, subln, batch=batch, seq=seq, tq=2048, tk=256, heads=2,
                         q_col=0, k_col=A_HEADS, v_col=2 * A_HEADS, lam_init=lam_init)
    b0 = 3 * A_HEADS
    outs, lses = [], []
    for g, (_, dil) in enumerate(B_GROUPS):
        o, lse = _dilated_group(qkv, batch=batch, seq=seq, row_tile=row_tile, group=g, dil=dil,
                                heads=2 if dil == 1 else 4, residues=4 if dil == 16 else 1,
                                q_col=b0, k_col=b0 + B_HEADS, v_col=b0 + 2 * B_HEADS)
        outs.append(o)
        lses.append(lse)
    ob = _merge_groups(outs, lses)
    return _out_proj(oa, ob, w_out.astype(BF16), x, tm=1024, tn=1024)


def _cd_mixer(x, gain, w_in, q_norm, w_uq, kv_norm, w_ukv, w_out, rope, *, batch, seq):
    lat = C_Q_LORA + C_KV_LORA
    d_scale = HEAD_DIM ** -0.5 * LOG2_E
    modes = [(ROPE128, d_scale, 1)] * D_HEADS + [(ROPE128, 1.0, 1)] * D_HEADS + [(PLAIN, 1.0, 1)] * D_HEADS
    main = _norm_proj(x, gain, w_in[:, lat + C_ROPE:].astype(BF16), rope, modes, seq=seq, tm=1024, tn=1024,
                      out_dtype=BF16)

    c_scale = (C_NOPE + C_ROPE) ** -0.5 * LOG2_E
    w_q = w_uq.reshape(C_Q_LORA, C_HEADS, C_NOPE + C_ROPE)
    w_q = jnp.concatenate([w_q[..., :C_NOPE], _spread_rope64(w_q[..., C_NOPE:])], axis=-1)
    w_q = w_q.reshape(C_Q_LORA, C_HEADS * 2 * LANES).astype(BF16)
    w_latent = jnp.concatenate([w_in[:, :lat], _spread_rope64(w_in[:, lat:lat + C_ROPE])], axis=1).astype(BF16)
    qc, kv, k_rope = _mla_proj(x, gain, w_latent, q_norm, w_q, kv_norm, w_ukv.astype(BF16), rope,
                               seq=seq, tm=512, q_scale=c_scale)
    oc = _mla_attention(qc, kv, k_rope, batch=batch, seq=seq, tq=2048, tk=256, heads=2, k_rope_col=0)
    od = _moba_attention(main, batch=batch, seq=seq, tq=1024, heads=4,
                         q_col=0, k_col=D_HEADS, v_col=2 * D_HEADS)
    return _out_proj(oc, od, w_out.astype(BF16), x, tm=1024, tn=1024)


def kernel(x, p, ffn_norm, ffn_w_gu, ffn_w_down, mix_norm, ab_w_in, ab_lambda, ab_subln, ab_w_out,
           cd_w_in, cd_q_norm, cd_w_uq, cd_kv_norm, cd_w_ukv, cd_w_out, ple_norm, ple_w_gate,
           ple_w_proj, final_norm):
    batch, seq, d = x.shape
    depth = p.shape[0]
    rope = _rope_tables(seq)
    w_gu, w_down = ffn_w_gu.astype(BF16), ffn_w_down.astype(BF16)
    x = x.reshape(batch * seq, d)
    for i in range(depth):
        j = i // 2
        x = _ffn(x, ffn_norm[i, 0], w_gu, w_down, i, 0, tm=1024, tf=512)
        if i % 2 == 0:
            x = _ab_mixer(x, mix_norm[i], ab_w_in[j], ab_lambda[j], ab_subln[j], ab_w_out[j], i, rope,
                          batch=batch, seq=seq)
        else:
            x = _cd_mixer(x, mix_norm[i], cd_w_in[j], cd_q_norm[j], cd_w_uq[j], cd_kv_norm[j], cd_w_ukv[j],
                          cd_w_out[j], rope, batch=batch, seq=seq)
        x = _ffn(x, ffn_norm[i, 1], w_gu, w_down, i, 1, tm=1024, tf=512)
        x = _ple(x, p.reshape(depth * batch * seq, -1), i, ple_norm[i], ple_w_gate[i].astype(BF16),
                 ple_w_proj[i].astype(BF16), final_norm, tm=512, final_norm=(i == depth - 1))
    return x.reshape(batch, seq, d)
```

```python
import functools
import math

import jax
import jax.numpy as jnp
from jax import lax
from jax.experimental import pallas as pl
from jax.experimental.pallas import tpu as pltpu

F32 = jnp.float32
BF16 = jnp.bfloat16

D_MODEL = 2048
D_FF = 5632
PLE_DIM = 256
HEAD_DIM = 128
ROPE_THETA = 10000.0
NORM_EPS = 1e-6

A_HEADS = 8
A_QK_DIM = 64
B_GROUPS = ((128, 1), (512, 4), (2048, 16))
B_HEADS_PER_GROUP = 4
B_HEADS = B_HEADS_PER_GROUP * len(B_GROUPS)
C_HEADS = 8
C_Q_LORA = 512
C_KV_LORA = 256
C_NOPE = 128
C_ROPE = 64
D_HEADS = 8
MOBA_BLOCK = 256
MOBA_TOPK = 3

LOG2_E = 1.0 / math.log(2.0)
PAST_BLOCKS_PER_ITERATION = 4
SUM_ROWS = 16
LANES = 128
VMEM_LIMIT = 56 * 1024 * 1024

PLAIN, ROPE128, ROPE64 = 0, 1, 2


def _rms(x, gain):
    return x * lax.rsqrt(jnp.mean(x * x, axis=-1, keepdims=True) + NORM_EPS) * gain


def _params(*semantics):
    return pltpu.CompilerParams(dimension_semantics=semantics, vmem_limit_bytes=VMEM_LIMIT)


def _norm_proj_body(x_ref, g_ref, w_ref, c128_ref, s128_ref, c64_ref, s64_ref, o_ref, xn_ref, stage_ref, *,
                    tile_patterns):
    j = pl.program_id(1)
    tm = o_ref.shape[0]

    @pl.when(j == 0)
    def _():
        xn_ref[...] = _rms(x_ref[...], g_ref[...]).astype(BF16)

    def tile(pattern):
        acc = jnp.dot(xn_ref[...], w_ref[...], preferred_element_type=F32)
        for c, (kind, scale, dil) in enumerate(pattern):
            a = acc[:, _lanes(c)]
            if scale != 1.0:
                a = a * scale
            if kind == ROPE128:
                a = a * c128_ref[...] + pltpu.roll(a, 64, 1) * s128_ref[...]
            elif kind == ROPE64:
                a = a * c64_ref[...] + pltpu.roll(a, 64, 1) * s64_ref[...]
            if dil == 1:
                o_ref[:, _lanes(c)] = a.astype(o_ref.dtype)
            else:
                rows = tm // dil
                stage_ref[c] = a
                for r in range(dil):
                    o_ref[r * rows:(r + 1) * rows, _lanes(c)] = (
                        stage_ref[c, pl.ds(r, rows, stride=dil), :].astype(o_ref.dtype))

    distinct = sorted(set(tile_patterns), key=tile_patterns.index)
    if len(distinct) == 1:
        tile(distinct[0])
    else:
        for pattern in distinct:
            tiles = [t for t, p in enumerate(tile_patterns) if p == pattern]
            cond = functools.reduce(jnp.logical_or, [j == t for t in tiles])
            pl.when(cond)(functools.partial(tile, pattern))


def _norm_proj(x, gain, w, rope, chunk_modes, *, seq, tm, tn, out_dtype):
    t, k = x.shape
    n = w.shape[1]
    per_tile = tn // LANES
    assert t % tm == 0 and n % tn == 0 and seq % tm == 0 and len(chunk_modes) * LANES == n
    tile_patterns = tuple(tuple(chunk_modes[a * per_tile:(a + 1) * per_tile]) for a in range(n // tn))
    seq_tiles = seq // tm
    tab = pl.BlockSpec((tm, LANES), lambda i, j: (i % seq_tiles, 0))
    return pl.pallas_call(
        functools.partial(_norm_proj_body, tile_patterns=tile_patterns),
        grid=(t // tm, n // tn),
        in_specs=[pl.BlockSpec((tm, k), lambda i, j: (i, 0)),
                  pl.BlockSpec((1, k), lambda i, j: (0, 0)),
                  pl.BlockSpec((k, tn), lambda i, j: (0, j)),
                  tab, tab, tab, tab],
        out_specs=pl.BlockSpec((tm, tn), lambda i, j: (i, j)),
        out_shape=jax.ShapeDtypeStruct((t, n), out_dtype),
        scratch_shapes=[pltpu.VMEM((tm, k), BF16), pltpu.VMEM((per_tile, tm, LANES), F32)],
        compiler_params=_params("parallel", "arbitrary"),
        name="norm_proj",
    )(x, gain.reshape(1, k), w, *rope)


def _out_proj_body(a1_ref, a2_ref, w1_ref, w2_ref, res_ref, o_ref):
    o_ref[...] = (res_ref[...] + jnp.dot(a1_ref[...], w1_ref[...], preferred_element_type=F32)
                  + jnp.dot(a2_ref[...], w2_ref[...], preferred_element_type=F32))


def _out_proj(a1, a2, w, res, *, tm, tn):
    t, k1 = a1.shape
    k2 = a2.shape[1]
    n = w.shape[1]
    assert t % tm == 0 and n % tn == 0 and w.shape[0] == k1 + k2 and k1 % k2 == 0
    return pl.pallas_call(
        _out_proj_body,
        grid=(t // tm, n // tn),
        in_specs=[pl.BlockSpec((tm, k1), lambda i, j: (i, 0)),
                  pl.BlockSpec((tm, k2), lambda i, j: (i, 0)),
                  pl.BlockSpec((k1, tn), lambda i, j: (0, j)),
                  pl.BlockSpec((k2, tn), lambda i, j: (k1 // k2, j)),
                  pl.BlockSpec((tm, tn), lambda i, j: (i, j))],
        out_specs=pl.BlockSpec((tm, tn), lambda i, j: (i, j)),
        out_shape=jax.ShapeDtypeStruct((t, n), F32),
        compiler_params=_params("parallel", "parallel"),
        name="out_proj",
    )(a1, a2, w, w, res)


def _mla_proj_body(x_ref, g_ref, wl_ref, qn_ref, wq_ref, kvn_ref, wkv_ref, c64_ref, s64_ref, q_ref, kv_ref, kr_ref,
                   *, q_lora, kv_lora, q_scale):
    half = x_ref.shape[0] // 2
    lat = q_lora + kv_lora
    for rows in (slice(0, half), slice(half, 2 * half)):
        xn = _rms(x_ref[rows, :], g_ref[...]).astype(BF16)
        latent = jnp.dot(xn, wl_ref[...], preferred_element_type=F32)
        k_rope = latent[:, lat:]
        kr_ref[rows, :] = (k_rope * c64_ref[rows, :] + pltpu.roll(k_rope, 64, 1) * s64_ref[rows, :]).astype(kr_ref.dtype)
        c_q = _rms(latent[:, :q_lora], qn_ref[...]).astype(BF16)
        c_kv = _rms(latent[:, q_lora:lat], kvn_ref[...]).astype(BF16)
        kv_ref[rows, :] = jnp.dot(c_kv, wkv_ref[...], preferred_element_type=F32).astype(kv_ref.dtype)
        q = jnp.dot(c_q, wq_ref[...], preferred_element_type=F32) * q_scale
        for c in range(q_ref.shape[1] // LANES):
            a = q[:, _lanes(c)]
            if c % 2 == 1:
                a = a * c64_ref[rows, :] + pltpu.roll(a, 64, 1) * s64_ref[rows, :]
            q_ref[rows, _lanes(c)] = a.astype(q_ref.dtype)


def _mla_proj(x, gain, w_latent, q_norm, w_q, kv_norm, w_kv, rope, *, seq, tm, q_scale):
    t, d = x.shape
    q_lora, kv_lora = w_q.shape[0], w_kv.shape[0]
    assert t % tm == 0 and seq % tm == 0 and w_latent.shape[1] == q_lora + kv_lora + LANES
    seq_tiles = seq // tm
    const = lambda i: (0, 0)
    tab = pl.BlockSpec((tm, LANES), lambda i: (i % seq_tiles, 0))
    full = lambda a: pl.BlockSpec(a.shape, const)
    out_spec = lambda width: pl.BlockSpec((tm, width), lambda i: (i, 0))
    widths = (w_q.shape[1], w_kv.shape[1], LANES)
    return pl.pallas_call(
        functools.partial(_mla_proj_body, q_lora=q_lora, kv_lora=kv_lora, q_scale=q_scale),
        grid=(t // tm,),
        in_specs=[pl.BlockSpec((tm, d), lambda i: (i, 0)), pl.BlockSpec((1, d), const), full(w_latent),
                  pl.BlockSpec((1, q_lora), const), full(w_q), pl.BlockSpec((1, kv_lora), const), full(w_kv),
                  tab, tab],
        out_specs=[out_spec(w) for w in widths],
        out_shape=[jax.ShapeDtypeStruct((t, w), BF16) for w in widths],
        compiler_params=_params("parallel"),
        name="mla_proj",
    )(x, gain.reshape(1, d), w_latent, q_norm.reshape(1, q_lora), w_q, kv_norm.reshape(1, kv_lora), w_kv,
      rope[2], rope[3])


def _ffn_body(x_ref, g_ref, wg_ref, wu_ref, wd_ref, o_ref, xn_ref):
    j = pl.program_id(1)

    def partial_out(xn):
        g = jnp.dot(xn, wg_ref[...], preferred_element_type=F32)
        u = jnp.dot(xn, wu_ref[...], preferred_element_type=F32)
        h = (g * jax.nn.sigmoid(g) * u).astype(BF16)
        return jnp.dot(h, wd_ref[...], preferred_element_type=F32)

    @pl.when(j == 0)
    def _():
        half = x_ref.shape[0] // 2
        for rows in (slice(0, half), slice(half, 2 * half)):
            xn = _rms(x_ref[rows, :], g_ref[...]).astype(BF16)
            xn_ref[rows, :] = xn
            o_ref[rows, :] = partial_out(xn)

    @pl.when(j > 0)
    def _():
        o_ref[...] += partial_out(xn_ref[...])

    @pl.when(j == pl.num_programs(1) - 1)
    def _():
        o_ref[...] = x_ref[...] + 0.5 * o_ref[...]


def _ffn(x, gain, w_gu, w_down, layer, half, *, tm, tf):
    t, d = x.shape
    d_ff = w_down.shape[2]
    assert t % tm == 0 and d_ff % tf == 0
    nf = d_ff // tf
    return pl.pallas_call(
        _ffn_body,
        grid=(t // tm, nf),
        in_specs=[pl.BlockSpec((tm, d), lambda i, j: (i, 0)),
                  pl.BlockSpec((1, d), lambda i, j: (0, 0)),
                  pl.BlockSpec((None, None, d, tf), lambda i, j: (layer, half, 0, j)),
                  pl.BlockSpec((None, None, d, tf), lambda i, j: (layer, half, 0, nf + j)),
                  pl.BlockSpec((None, None, tf, d), lambda i, j: (layer, half, j, 0))],
        out_specs=pl.BlockSpec((tm, d), lambda i, j: (i, 0)),
        out_shape=jax.ShapeDtypeStruct((t, d), F32),
        scratch_shapes=[pltpu.VMEM((tm, d), BF16)],
        compiler_params=_params("parallel", "arbitrary"),
        name="ffn",
    )(x, gain.reshape(1, d), w_gu, w_gu, w_down)


def _ple_body(x_ref, p_ref, g_ref, wg_ref, wp_ref, fg_ref, o_ref, *, final_norm):
    half = x_ref.shape[0] // 2
    for rows in (slice(0, half), slice(half, 2 * half)):
        x = x_ref[rows, :]
        xn = _rms(x, g_ref[...]).astype(BF16)
        gate = jax.nn.sigmoid(jnp.dot(xn, wg_ref[...], preferred_element_type=F32))
        proj = jnp.dot(p_ref[rows, :].astype(BF16), wp_ref[...], preferred_element_type=F32)
        y = x + gate * proj
        if final_norm:
            y = _rms(y, fg_ref[...])
        o_ref[rows, :] = y


def _ple(x, p, layer, gain, w_gate, w_proj, final_gain, *, tm, final_norm):
    t, d = x.shape
    pd = p.shape[1]
    assert t % tm == 0
    first_tile = layer * (t // tm)
    const = lambda i: (0, 0)
    return pl.pallas_call(
        functools.partial(_ple_body, final_norm=final_norm),
        grid=(t // tm,),
        in_specs=[pl.BlockSpec((tm, d), lambda i: (i, 0)),
                  pl.BlockSpec((tm, pd), lambda i: (first_tile + i, 0)),
                  pl.BlockSpec((1, d), const),
                  pl.BlockSpec((d, d), const),
                  pl.BlockSpec((pd, d), const),
                  pl.BlockSpec((1, d), const)],
        out_specs=pl.BlockSpec((tm, d), lambda i: (i, 0)),
        out_shape=jax.ShapeDtypeStruct((t, d), F32),
        compiler_params=_params("parallel"),
        name="ple",
    )(x, p, gain.reshape(1, d), w_gate, w_proj, final_gain.reshape(1, d))


def _dot_nt(a, b):
    return lax.dot_general(a, b, (((1,), (1,)), ((), ())), preferred_element_type=F32)


def _lanes(c, width=LANES):
    return slice(c * width, (c + 1) * width)


def _fill_transposed(vt_ref, v_ref, lane_offsets, chunk):
    dv = vt_ref.shape[1] - SUM_ROWS
    ones_row = jnp.where(lax.broadcasted_iota(jnp.int32, (SUM_ROWS, chunk), 0) == 0, 1.0, 0.0).astype(BF16)

    def body(j, _):
        start = pl.multiple_of(j * chunk, chunk)
        for c, off in enumerate(lane_offsets):
            vt_ref[c, :dv, pl.ds(start, chunk)] = v_ref[pl.ds(start, chunk), off:off + dv].T
            vt_ref[c, dv:, pl.ds(start, chunk)] = ones_row
        return 0

    lax.fori_loop(0, v_ref.shape[0] // chunk, body, 0)


def _flash_scratch(heads, tk, cols, dv, seq):
    return [pltpu.VMEM((heads, dv + SUM_ROWS, seq), BF16),
            pltpu.VMEM((heads, tk, cols), F32), pltpu.VMEM((heads, tk, cols), BF16),
            pltpu.VMEM((heads, dv + SUM_ROWS, cols), F32)]


def _causal_diag_masks(tk, tq, cols):
    key_row = lax.broadcasted_iota(jnp.int32, (tk, 1), 0)
    query_col = lax.broadcasted_iota(jnp.int32, (1, cols), 1) % tq
    return [key_row + d * tk <= query_col for d in range(tq // tk)]


def _causal_col_ranges(tk, tq, maps):
    return lambda d: [(g * tq + d * tk, (g + 1) * tq) for g in range(maps)]


def _causal_flash_t(score_fns, vt_ref, s_ref, p_ref, acc_ref, n_past, diag_mask_fn, diag_cols_fn, n_diag,
                    past_mask_fns=None):
    heads = range(len(score_fns))
    _, tk, cols = s_ref.shape

    everything = [(0, cols)]

    def stage_scores(block, ranges=everything):
        start = pl.multiple_of(block * tk, tk)
        for c in heads:
            for lo, hi in ranges:
                s = score_fns[c](start, slice(lo, hi))
                if past_mask_fns is not None:
                    s = jnp.where(past_mask_fns[c](block)[:, lo:hi], s, -jnp.inf)
                s_ref[c, :, lo:hi] = s

    def accumulate(block, alphas, ranges=everything):
        start = pl.multiple_of(block * tk, tk)
        for c in heads:
            for lo, hi in ranges:
                pv = jnp.dot(vt_ref[c, :, pl.ds(start, tk)], p_ref[c, :, lo:hi], preferred_element_type=F32)
                acc_ref[c, :, lo:hi] = alphas[c][:, lo:hi] * acc_ref[c, :, lo:hi] + pv

    def softmax(stats, diag=None, ranges=everything):
        new_stats, alphas = [], []
        for c in heads:
            m = stats[c]
            if diag is None:
                read = lambda lo, hi: s_ref[c, :, lo:hi]
            else:
                visible = diag_mask_fn(diag, c)
                read = lambda lo, hi: jnp.where(visible[:, lo:hi], s_ref[c, :, lo:hi], -jnp.inf)
            pieces, at = [], 0
            for lo, hi in ranges:
                if lo > at:
                    pieces.append(m[:, at:lo])
                pieces.append(jnp.maximum(m[:, lo:hi], jnp.max(read(lo, hi), axis=0, keepdims=True)))
                at = hi
            if at < cols:
                pieces.append(m[:, at:])
            m_new = pieces[0] if len(pieces) == 1 else jnp.concatenate(pieces, axis=1)
            alpha = jnp.exp2(m - m_new)
            for lo, hi in ranges:
                p_ref[c, :, lo:hi] = jnp.exp2(read(lo, hi) - m_new[:, lo:hi]).astype(BF16)
            new_stats.append(m_new)
            alphas.append(alpha)
        return tuple(new_stats), tuple(alphas)

    def past_block(j, carry):
        stats, alphas = carry
        accumulate(jnp.maximum(j - 1, 0), alphas)
        stats, alphas = softmax(stats)
        stage_scores(j + 1)
        return stats, alphas

    for c in heads:
        p_ref[c] = jnp.zeros(p_ref.shape[1:], BF16)
        acc_ref[c] = jnp.zeros(acc_ref.shape[1:], F32)
    stage_scores(0)
    init = (tuple(jnp.full((1, cols), jnp.finfo(F32).min, F32) for _ in heads),
            tuple(jnp.ones((1, cols), F32) for _ in heads))
    def past_group(t, carry):
        for u in range(PAST_BLOCKS_PER_ITERATION):
            carry = past_block(t * PAST_BLOCKS_PER_ITERATION + u, carry)
        return carry

    n_grouped = n_past // PAST_BLOCKS_PER_ITERATION
    carry = lax.fori_loop(0, n_grouped, past_group, init)
    stats, alphas = lax.fori_loop(n_grouped * PAST_BLOCKS_PER_ITERATION, n_past, past_block, carry)

    accumulate(jnp.maximum(n_past - 1, 0), alphas)
    for d in range(n_diag):
        stats, alphas = softmax(stats, diag=d, ranges=diag_cols_fn(d))
        if d + 1 < n_diag:
            stage_scores(n_past + d + 1, ranges=diag_cols_fn(d + 1))
        accumulate(n_past + d, alphas, ranges=diag_cols_fn(d))
    dv = acc_ref.shape[1] - SUM_ROWS
    return tuple((acc_ref[c, dv:dv + 1, :], acc_ref[c, :dv, :]) for c in heads)


def _diff_attn_body(lam_ref, subln_ref, q_ref, k_ref, v_ref, o_ref, qt_ref, vt_ref, s_ref, p_ref, acc_ref, *,
                    tq, tk, heads, lam_init):
    qi = pl.program_id(2)

    @pl.when(qi == 0)
    def _():
        _fill_transposed(vt_ref, v_ref, [c * LANES for c in range(heads)], tk)

    is_map0 = (lax.broadcasted_iota(jnp.int32, (LANES, 1), 0) % 64) < 32

    def make_score_fn(c):
        q_t = q_ref[:, _lanes(c)].T
        zero = jnp.zeros_like(q_t)
        qt_ref[c] = jnp.concatenate([jnp.where(is_map0, q_t, zero), jnp.where(is_map0, zero, q_t)], axis=1)
        return lambda start, cols: jnp.dot(k_ref[pl.ds(start, tk), _lanes(c)], qt_ref[c, :, cols],
                                           preferred_element_type=F32)

    diag = _causal_diag_masks(tk, tq, 2 * tq)
    results = _causal_flash_t([make_score_fn(c) for c in range(heads)], vt_ref, s_ref, p_ref, acc_ref,
                              qi * (tq // tk), lambda d, c: diag[d], _causal_col_ranges(tk, tq, 2), tq // tk)

    lp = lam_ref[...]
    lam = (jnp.exp(jnp.sum(lp[0:1] * lp[1:2], axis=-1, keepdims=True))
           - jnp.exp(jnp.sum(lp[2:3] * lp[3:4], axis=-1, keepdims=True)) + lam_init)
    for c, (l, acc) in enumerate(results):
        o_t = acc / l
        o = (o_t[:, :tq] - lam * o_t[:, tq:]).T
        o_ref[:, _lanes(c)] = (_rms(o, subln_ref[...]) * (1.0 - lam_init)).astype(o_ref.dtype)


def _diff_attention(qkv, lam_params, subln, *, batch, seq, tq, tk, heads, q_col, k_col, v_col, lam_init):
    nq = seq // tq
    width = heads * LANES
    assert A_HEADS % heads == 0 and q_col % heads == 0 and k_col % heads == 0 and v_col % heads == 0
    assert seq % tq == 0 and tq % tk == 0
    return pl.pallas_call(
        functools.partial(_diff_attn_body, tq=tq, tk=tk, heads=heads, lam_init=lam_init),
        grid=(batch, A_HEADS // heads, nq),
        in_specs=[pl.BlockSpec((4, A_QK_DIM), lambda b, h, i: (0, 0)),
                  pl.BlockSpec((1, LANES), lambda b, h, i: (0, 0)),
                  pl.BlockSpec((tq, width), lambda b, h, i: (b * nq + i, q_col // heads + h)),
                  pl.BlockSpec((seq, width), lambda b, h, i: (b, k_col // heads + h)),
                  pl.BlockSpec((seq, width), lambda b, h, i: (b, v_col // heads + h))],
        out_specs=pl.BlockSpec((tq, width), lambda b, h, i: (b * nq + i, h)),
        out_shape=jax.ShapeDtypeStruct((batch * seq, A_HEADS * LANES), BF16),
        scratch_shapes=[pltpu.VMEM((heads, LANES, 2 * tq), BF16)] + _flash_scratch(heads, tk, 2 * tq, LANES, seq),
        compiler_params=_params("parallel", "parallel", "arbitrary"),
        name="diff_attn",
    )(lam_params, subln.reshape(1, LANES), qkv, qkv, qkv)


def _tile_index(ref, r, i, tile):
    per_slab = ref.shape[2]
    if per_slab >= tile:
        runs = per_slab // tile
        return i // runs, r, pl.ds(pl.multiple_of((i % runs) * tile, tile), tile)
    slabs = tile // per_slab
    return pl.ds(i * slabs, slabs), r, slice(None)


def _load_tile(ref, r, i, c, tile):
    slab, res, rows = _tile_index(ref, r, i, tile)
    return ref[slab, res, rows, _lanes(c)].reshape(tile, LANES)


def _store_tile(ref, r, i, c, tile, value):
    slab, res, rows = _tile_index(ref, r, i, tile)
    per_slab = ref.shape[2]
    ref[slab, res, rows, _lanes(c)] = (value if per_slab >= tile
                                       else value.reshape(tile // per_slab, per_slab, LANES))


def _dilated_body(q_ref, k_ref, v_ref, o_ref, lse_ref, *, tile, heads):
    slabs, residues, per_slab, _ = q_ref.shape
    row = lax.broadcasted_iota(jnp.int32, (tile, 1), 0)
    streams = [(r, c) for r in range(residues) for c in range(heads)]

    def attend(r, i, c, k, v, valid):
        s = jnp.where(valid, _dot_nt(_load_tile(q_ref, r, i, c, tile), k), -jnp.inf)
        m = jnp.max(s, axis=-1, keepdims=True)
        p = jnp.exp(s - m)
        l = jnp.sum(p, axis=-1, keepdims=True)
        o = jnp.dot(p.astype(BF16), v, preferred_element_type=F32) / l
        _store_tile(o_ref, r, i, c, tile, o)
        _store_tile(lse_ref, r, i, c, tile, jnp.broadcast_to(m + jnp.log(l), (tile, LANES)))

    col = lax.broadcasted_iota(jnp.int32, (1, tile), 1)
    for r, c in streams:
        attend(r, 0, c, _load_tile(k_ref, r, 0, c, tile), _load_tile(v_ref, r, 0, c, tile), col <= row)

    col2 = lax.broadcasted_iota(jnp.int32, (1, 2 * tile), 1)
    band = jnp.logical_and(col2 >= row, col2 <= row + tile)

    def body(i, _):
        for r, c in streams:
            k = jnp.concatenate([_load_tile(k_ref, r, i - 1, c, tile), _load_tile(k_ref, r, i, c, tile)], axis=0)
            v = jnp.concatenate([_load_tile(v_ref, r, i - 1, c, tile), _load_tile(v_ref, r, i, c, tile)], axis=0)
            attend(r, i, c, k, v, band)
        return 0

    n_tiles = slabs * per_slab // tile
    lax.fori_loop(1, n_tiles, body, 0, unroll=max(1, min(4, n_tiles - 1)))


def _dilated_group(qkv, *, batch, seq, row_tile, group, dil, heads, residues, q_col, k_col, v_col):
    hpg = B_HEADS_PER_GROUP
    tile = B_GROUPS[group][0] // dil
    slabs, per_slab = seq // row_tile, row_tile // dil
    assert tile == LANES and hpg % heads == 0 and (seq // dil) % tile == 0 and dil % residues == 0
    assert per_slab % tile == 0 or tile % per_slab == 0
    width = heads * LANES
    view = qkv.reshape(batch, slabs, dil, per_slab, qkv.shape[1])

    def in_spec(col):
        first = col + group * hpg
        assert first % heads == 0
        return pl.BlockSpec((None, slabs, residues, per_slab, width),
                            lambda b, h, r: (b, 0, r, 0, first // heads + h))

    out_spec = pl.BlockSpec((None, slabs, residues, per_slab, width), lambda b, h, r: (b, 0, r, 0, h))
    out_sds = jax.ShapeDtypeStruct((batch, slabs, dil, per_slab, hpg * LANES), F32)
    o, lse = pl.pallas_call(
        functools.partial(_dilated_body, tile=tile, heads=heads),
        grid=(batch, hpg // heads, dil // residues),
        in_specs=[in_spec(q_col), in_spec(k_col), in_spec(v_col)],
        out_specs=[out_spec, out_spec],
        out_shape=[out_sds, out_sds],
        compiler_params=_params("parallel", "parallel", "parallel"),
        name=f"dilated_g{group}",
    )(view, view, view)
    shape = (batch * slabs, dil, per_slab, hpg * LANES)
    return o.reshape(shape), lse.reshape(shape)


def _merge_body(o0_ref, o1_ref, o2_ref, l0_ref, l1_ref, l2_ref, out_ref, nat_ref):
    def natural(ref, slot, c):
        dil, rows = ref.shape[0], ref.shape[1]
        if dil == 1:
            return ref[0, :, _lanes(c)]
        for r in range(dil):
            nat_ref[slot, pl.ds(r, rows, stride=dil), :] = ref[r, :, _lanes(c)]
        return nat_ref[slot]

    for c in range(out_ref.shape[1] // LANES):
        o0, o1, o2 = natural(o0_ref, None, c), natural(o1_ref, 0, c), natural(o2_ref, 1, c)
        l0, l1, l2 = natural(l0_ref, None, c), natural(l1_ref, 2, c), natural(l2_ref, 3, c)
        m = jnp.maximum(jnp.maximum(l0, l1), l2)
        w0, w1, w2 = jnp.exp(l0 - m), jnp.exp(l1 - m), jnp.exp(l2 - m)
        out_ref[:, _lanes(c)] = ((w0 * o0 + w1 * o1 + w2 * o2) / (w0 + w1 + w2)).astype(out_ref.dtype)


def _merge_groups(outs, lses):
    tiles, _, _, width = outs[0].shape
    row_tile = outs[0].shape[1] * outs[0].shape[2]
    specs = [pl.BlockSpec((None,) + a.shape[1:], lambda i: (i, 0, 0, 0)) for a in (*outs, *lses)]
    return pl.pallas_call(
        _merge_body,
        grid=(tiles,),
        in_specs=specs,
        out_specs=pl.BlockSpec((row_tile, width), lambda i: (i, 0)),
        out_shape=jax.ShapeDtypeStruct((tiles * row_tile, width), BF16),
        scratch_shapes=[pltpu.VMEM((4, row_tile, LANES), F32)],
        compiler_params=_params("parallel"),
        name="dilated_merge",
    )(*outs, *lses)


def _mla_body(q_ref, kv_ref, kr_ref, o_ref, qt_ref, vt_ref, s_ref, p_ref, acc_ref, *, tq, tk, heads):
    qi = pl.program_id(2)

    @pl.when(qi == 0)
    def _():
        _fill_transposed(vt_ref, kv_ref, [(2 * c + 1) * LANES for c in range(heads)], tk)

    def make_score_fn(c):
        qt_ref[c] = q_ref[:, _lanes(c, 2 * LANES)].T

        def score(start, cols):
            k = jnp.concatenate([kv_ref[pl.ds(start, tk), _lanes(2 * c)], kr_ref[pl.ds(start, tk), :]], axis=1)
            return jnp.dot(k, qt_ref[c, :, cols], preferred_element_type=F32)

        return score

    diag = _causal_diag_masks(tk, tq, tq)
    results = _causal_flash_t([make_score_fn(c) for c in range(heads)], vt_ref, s_ref, p_ref, acc_ref,
                              qi * (tq // tk), lambda d, c: diag[d], _causal_col_ranges(tk, tq, 1), tq // tk)
    for c, (l, acc) in enumerate(results):
        o_ref[:, _lanes(c)] = (acc / l).T.astype(o_ref.dtype)


def _mla_attention(q, kv, k_rope_src, *, batch, seq, tq, tk, heads, k_rope_col):
    nq = seq // tq
    assert C_HEADS % heads == 0 and seq % tq == 0 and tq % tk == 0
    return pl.pallas_call(
        functools.partial(_mla_body, tq=tq, tk=tk, heads=heads),
        grid=(batch, C_HEADS // heads, nq),
        in_specs=[pl.BlockSpec((tq, heads * 2 * LANES), lambda b, h, i: (b * nq + i, h)),
                  pl.BlockSpec((seq, heads * 2 * LANES), lambda b, h, i: (b, h)),
                  pl.BlockSpec((seq, LANES), lambda b, h, i: (b, k_rope_col))],
        out_specs=pl.BlockSpec((tq, heads * LANES), lambda b, h, i: (b * nq + i, h)),
        out_shape=jax.ShapeDtypeStruct((batch * seq, C_HEADS * LANES), BF16),
        scratch_shapes=[pltpu.VMEM((heads, 2 * LANES, tq), BF16)] + _flash_scratch(heads, tk, tq, LANES, seq),
        compiler_params=_params("parallel", "parallel", "arbitrary"),
        name="mla_attn",
    )(q, kv, k_rope_src)


def _moba_body(q_ref, k_ref, v_ref, o_ref, kmean_ref, sel_ref, qt_ref, vt_ref, s_ref, p_ref, acc_ref, *,
               n_blk, tq, heads):
    qi = pl.program_id(2)
    blk = MOBA_BLOCK
    n_own = tq // blk

    @pl.when(qi == 0)
    def _():
        for c in range(heads):
            k_all = k_ref[:, _lanes(c)].astype(F32).reshape(n_blk, blk, LANES)
            kmean_ref[c] = jnp.mean(k_all, axis=1)
        _fill_transposed(vt_ref, v_ref, [c * LANES for c in range(heads)], blk)

    blk_id = lax.broadcasted_iota(jnp.int32, (n_blk, 1), 0)
    query_col = lax.broadcasted_iota(jnp.int32, (1, tq), 1)
    own_in_tile = query_col // blk
    past = blk_id < qi * n_own + own_in_tile
    causal = lax.broadcasted_iota(jnp.int32, (blk, 1), 0) <= query_col % blk

    def make_score_fn(c):
        q_t = q_ref[:, _lanes(c)].T
        km = kmean_ref[c]
        km_hi = km.astype(BF16)
        rem = km - km_hi.astype(F32)
        km_mid = rem.astype(BF16)
        km_lo = (rem - km_mid.astype(F32)).astype(BF16)
        gate = (jnp.dot(km_hi, q_t, preferred_element_type=F32) + jnp.dot(km_mid, q_t, preferred_element_type=F32)
                + jnp.dot(km_lo, q_t, preferred_element_type=F32))
        gate = jnp.where(past, gate, -jnp.inf)
        selected = jnp.zeros(gate.shape, F32)
        for _ in range(min(MOBA_TOPK, n_blk - 1)):
            best = jnp.max(gate, axis=0, keepdims=True)
            first = jnp.min(jnp.where(gate == best, blk_id, n_blk), axis=0, keepdims=True)
            pick = blk_id == first
            selected = jnp.where(jnp.logical_and(pick, past), 1.0, selected)
            gate = jnp.where(pick, -jnp.inf, gate)
        sel_ref[c] = selected
        qt_ref[c] = q_t
        return lambda start, cols: jnp.dot(k_ref[pl.ds(start, blk), _lanes(c)], qt_ref[c, :, cols],
                                           preferred_element_type=F32)

    score_fns = [make_score_fn(c) for c in range(heads)]
    n_past = qi * n_own
    chosen_fns = [lambda n, c=c: jnp.logical_or(sel_ref[c, pl.ds(n, 1), :] > 0.0, n >= n_past)
                  for c in range(heads)]

    def own_block_mask(d, c):
        chosen = sel_ref[c, pl.ds(n_past + d, 1), :] > 0.0
        return jnp.logical_or(jnp.logical_and(own_in_tile == d, causal),
                              jnp.logical_and(own_in_tile > d, chosen))

    results = _causal_flash_t(score_fns, vt_ref, s_ref, p_ref, acc_ref, n_past, own_block_mask,
                              _causal_col_ranges(blk, tq, 1), n_own,
                              past_mask_fns=chosen_fns)
    for c, (l, acc) in enumerate(results):
        o_ref[:, _lanes(c)] = (acc / l).T.astype(o_ref.dtype)


def _moba_attention(qkv, *, batch, seq, tq, heads, q_col, k_col, v_col):
    blk = MOBA_BLOCK
    n_blk = seq // blk
    nq = seq // tq
    width = heads * LANES
    assert seq % tq == 0 and tq % blk == 0 and D_HEADS % heads == 0
    assert q_col % heads == 0 and k_col % heads == 0 and v_col % heads == 0
    return pl.pallas_call(
        functools.partial(_moba_body, n_blk=n_blk, tq=tq, heads=heads),
        grid=(batch, D_HEADS // heads, nq),
        in_specs=[pl.BlockSpec((tq, width), lambda b, h, i: (b * nq + i, q_col // heads + h)),
                  pl.BlockSpec((seq, width), lambda b, h, i: (b, k_col // heads + h)),
                  pl.BlockSpec((seq, width), lambda b, h, i: (b, v_col // heads + h))],
        out_specs=pl.BlockSpec((tq, width), lambda b, h, i: (b * nq + i, h)),
        out_shape=jax.ShapeDtypeStruct((batch * seq, D_HEADS * LANES), BF16),
        scratch_shapes=[pltpu.VMEM((heads, n_blk, LANES), F32),
                        pltpu.VMEM((heads, n_blk, tq), F32),
                        pltpu.VMEM((heads, LANES, tq), BF16)] + _flash_scratch(heads, blk, tq, LANES, seq),
        compiler_params=_params("parallel", "parallel", "arbitrary"),
        name="moba_attn",
    )(qkv, qkv, qkv)


def _rope_tables(seq):
    pos = jnp.arange(seq, dtype=F32)[:, None]

    def angles(dim):
        inv = 1.0 / (ROPE_THETA ** (jnp.arange(0, dim, 2, dtype=F32) / dim))
        return pos * inv[None, :]

    a128 = angles(HEAD_DIM)
    a64 = angles(A_QK_DIM)
    c128 = jnp.concatenate([jnp.cos(a128)] * 2, axis=1)
    s128 = jnp.concatenate([-jnp.sin(a128), jnp.sin(a128)], axis=1)
    c64 = jnp.concatenate([jnp.cos(a64)] * 4, axis=1)
    s64 = jnp.concatenate([-jnp.sin(a64)] * 2 + [jnp.sin(a64)] * 2, axis=1)
    return c128, s128, c64, s64


def _interleave_diff_heads(w):
    d = w.shape[0]
    return w.reshape(d, A_HEADS, 2, 2, 32).transpose(0, 1, 3, 2, 4).reshape(d, A_HEADS * LANES)


def _spread_rope64(w):
    z = jnp.zeros(w.shape[:-1] + (32,), w.dtype)
    return jnp.concatenate([w[..., :32], z, w[..., 32:], z], axis=-1)


def _ab_mixer(x, gain, w_in, lam_params, subln, w_out, layer_idx, rope, *, batch, seq):
    aw = A_HEADS * LANES
    w = jnp.concatenate([_interleave_diff_heads(w_in[:, :aw]), _interleave_diff_heads(w_in[:, aw:2 * aw]),
                         w_in[:, 2 * aw:]], axis=1).astype(BF16)
    a_scale = A_QK_DIM ** -0.5 * LOG2_E
    b_scale = HEAD_DIM ** -0.5
    dils = [dil for _, dil in B_GROUPS for _ in range(B_HEADS_PER_GROUP)]
    modes = ([(ROPE64, a_scale, 1)] * A_HEADS + [(ROPE64, 1.0, 1)] * A_HEADS + [(PLAIN, 1.0, 1)] * A_HEADS
             + [(ROPE128, b_scale, d) for d in dils] + [(ROPE128, 1.0, d) for d in dils]
             + [(PLAIN, 1.0, d) for d in dils])
    row_tile = 1024
    qkv = _norm_proj(x, gain, w, rope, modes, seq=seq, tm=row_tile, tn=1280, out_dtype=BF16)
    lam_init = 0.8 - 0.6 * math.exp(-0.3 * layer_idx)
    oa = _diff_attention(qkv, lam_params, subln, batch=batch, seq=seq, tq=2048, tk=256, heads=2,
                         q_col=0, k_col=A_HEADS, v_col=2 * A_HEADS, lam_init=lam_init)
    b0 = 3 * A_HEADS
    outs, lses = [], []
    for g, (_, dil) in enumerate(B_GROUPS):
        o, lse = _dilated_group(qkv, batch=batch, seq=seq, row_tile=row_tile, group=g, dil=dil,
                                heads=2 if dil == 1 else 4, residues=4 if dil == 16 else 1,
                                q_col=b0, k_col=b0 + B_HEADS, v_col=b0 + 2 * B_HEADS)
        outs.append(o)
        lses.append(lse)
    ob = _merge_groups(outs, lses)
    return _out_proj(oa, ob, w_out.astype(BF16), x, tm=1024, tn=1024)


def _cd_mixer(x, gain, w_in, q_norm, w_uq, kv_norm, w_ukv, w_out, rope, *, batch, seq):
    lat = C_Q_LORA + C_KV_LORA
    d_scale = HEAD_DIM ** -0.5 * LOG2_E
    modes = [(ROPE128, d_scale, 1)] * D_HEADS + [(ROPE128, 1.0, 1)] * D_HEADS + [(PLAIN, 1.0, 1)] * D_HEADS
    main = _norm_proj(x, gain, w_in[:, lat + C_ROPE:].astype(BF16), rope, modes, seq=seq, tm=1024, tn=1024,
                      out_dtype=BF16)

    c_scale = (C_NOPE + C_ROPE) ** -0.5 * LOG2_E
    w_q = w_uq.reshape(C_Q_LORA, C_HEADS, C_NOPE + C_ROPE)
    w_q = jnp.concatenate([w_q[..., :C_NOPE], _spread_rope64(w_q[..., C_NOPE:])], axis=-1)
    w_q = w_q.reshape(C_Q_LORA, C_HEADS * 2 * LANES).astype(BF16)
    w_latent = jnp.concatenate([w_in[:, :lat], _spread_rope64(w_in[:, lat:lat + C_ROPE])], axis=1).astype(BF16)
    qc, kv, k_rope = _mla_proj(x, gain, w_latent, q_norm, w_q, kv_norm, w_ukv.astype(BF16), rope,
                               seq=seq, tm=512, q_scale=c_scale)
    oc = _mla_attention(qc, kv, k_rope, batch=batch, seq=seq, tq=2048, tk=256, heads=2, k_rope_col=0)
    od = _moba_attention(main, batch=batch, seq=seq, tq=1024, heads=4,
                         q_col=0, k_col=D_HEADS, v_col=2 * D_HEADS)
    return _out_proj(oc, od, w_out.astype(BF16), x, tm=1024, tn=1024)


def kernel(x, p, ffn_norm, ffn_w_gu, ffn_w_down, mix_norm, ab_w_in, ab_lambda, ab_subln, ab_w_out,
           cd_w_in, cd_q_norm, cd_w_uq, cd_kv_norm, cd_w_ukv, cd_w_out, ple_norm, ple_w_gate,
           ple_w_proj, final_norm):
    batch, seq, d = x.shape
    depth = p.shape[0]
    rope = _rope_tables(seq)
    w_gu, w_down = ffn_w_gu.astype(BF16), ffn_w_down.astype(BF16)
    x = x.reshape(batch * seq, d)
    for i in range(depth):
        j = i // 2
        x = _ffn(x, ffn_norm[i, 0], w_gu, w_down, i, 0, tm=1024, tf=512)
        if i % 2 == 0:
            x = _ab_mixer(x, mix_norm[i], ab_w_in[j], ab_lambda[j], ab_subln[j], ab_w_out[j], i, rope,
                          batch=batch, seq=seq)
        else:
            x = _cd_mixer(x, mix_norm[i], cd_w_in[j], cd_q_norm[j], cd_w_uq[j], cd_kv_norm[j], cd_w_ukv[j],
                          cd_w_out[j], rope, batch=batch, seq=seq)
        x = _ffn(x, ffn_norm[i, 1], w_gu, w_down, i, 1, tm=1024, tf=512)
        x = _ple(x, p.reshape(depth * batch * seq, -1), i, ple_norm[i], ple_w_gate[i].astype(BF16),
                 ple_w_proj[i].astype(BF16), final_norm, tm=512, final_norm=(i == depth - 1))
    return x.reshape(batch, seq, d)
```

```python
import functools
import math

import jax
import jax.numpy as jnp
from jax import lax
from jax.experimental import pallas as pl
from jax.experimental.pallas import tpu as pltpu

F32 = jnp.float32
BF16 = jnp.bfloat16

D_MODEL = 2048
D_FF = 5632
PLE_DIM = 256
HEAD_DIM = 128
ROPE_THETA = 10000.0
NORM_EPS = 1e-6

A_HEADS = 8
A_QK_DIM = 64
B_GROUPS = ((128, 1), (512, 4), (2048, 16))
B_HEADS_PER_GROUP = 4
B_HEADS = B_HEADS_PER_GROUP * len(B_GROUPS)
C_HEADS = 8
C_Q_LORA = 512
C_KV_LORA = 256
C_NOPE = 128
C_ROPE = 64
D_HEADS = 8
MOBA_BLOCK = 256
MOBA_TOPK = 3

LOG2_E = 1.0 / math.log(2.0)
PAST_BLOCKS_PER_ITERATION = 8
SUM_ROWS = 16
LANES = 128
VMEM_LIMIT = 56 * 1024 * 1024

PLAIN, ROPE128, ROPE64 = 0, 1, 2


def _rms(x, gain):
    return x * lax.rsqrt(jnp.mean(x * x, axis=-1, keepdims=True) + NORM_EPS) * gain


def _params(*semantics):
    return pltpu.CompilerParams(dimension_semantics=semantics, vmem_limit_bytes=VMEM_LIMIT)


def _norm_proj_body(x_ref, g_ref, w_ref, c128_ref, s128_ref, c64_ref, s64_ref, o_ref, xn_ref, stage_ref, *,
                    tile_patterns):
    j = pl.program_id(1)
    tm = o_ref.shape[0]

    @pl.when(j == 0)
    def _():
        xn_ref[...] = _rms(x_ref[...], g_ref[...]).astype(BF16)

    def tile(pattern):
        acc = jnp.dot(xn_ref[...], w_ref[...], preferred_element_type=F32)
        for c, (kind, scale, dil) in enumerate(pattern):
            a = acc[:, _lanes(c)]
            if scale != 1.0:
                a = a * scale
            if kind == ROPE128:
                a = a * c128_ref[...] + pltpu.roll(a, 64, 1) * s128_ref[...]
            elif kind == ROPE64:
                a = a * c64_ref[...] + pltpu.roll(a, 64, 1) * s64_ref[...]
            if dil == 1:
                o_ref[:, _lanes(c)] = a.astype(o_ref.dtype)
            else:
                rows = tm // dil
                stage_ref[c] = a
                for r in range(dil):
                    o_ref[r * rows:(r + 1) * rows, _lanes(c)] = (
                        stage_ref[c, pl.ds(r, rows, stride=dil), :].astype(o_ref.dtype))

    distinct = sorted(set(tile_patterns), key=tile_patterns.index)
    if len(distinct) == 1:
        tile(distinct[0])
    else:
        for pattern in distinct:
            tiles = [t for t, p in enumerate(tile_patterns) if p == pattern]
            cond = functools.reduce(jnp.logical_or, [j == t for t in tiles])
            pl.when(cond)(functools.partial(tile, pattern))


def _norm_proj(x, gain, w, rope, chunk_modes, *, seq, tm, tn, out_dtype):
    t, k = x.shape
    n = w.shape[1]
    per_tile = tn // LANES
    assert t % tm == 0 and n % tn == 0 and seq % tm == 0 and len(chunk_modes) * LANES == n
    tile_patterns = tuple(tuple(chunk_modes[a * per_tile:(a + 1) * per_tile]) for a in range(n // tn))
    seq_tiles = seq // tm
    tab = pl.BlockSpec((tm, LANES), lambda i, j: (i % seq_tiles, 0))
    return pl.pallas_call(
        functools.partial(_norm_proj_body, tile_patterns=tile_patterns),
        grid=(t // tm, n // tn),
        in_specs=[pl.BlockSpec((tm, k), lambda i, j: (i, 0)),
                  pl.BlockSpec((1, k), lambda i, j: (0, 0)),
                  pl.BlockSpec((k, tn), lambda i, j: (0, j)),
                  tab, tab, tab, tab],
        out_specs=pl.BlockSpec((tm, tn), lambda i, j: (i, j)),
        out_shape=jax.ShapeDtypeStruct((t, n), out_dtype),
        scratch_shapes=[pltpu.VMEM((tm, k), BF16), pltpu.VMEM((per_tile, tm, LANES), F32)],
        compiler_params=_params("parallel", "arbitrary"),
        name="norm_proj",
    )(x, gain.reshape(1, k), w, *rope)


def _out_proj_body(a1_ref, a2_ref, w1_ref, w2_ref, res_ref, o_ref):
    o_ref[...] = (res_ref[...] + jnp.dot(a1_ref[...], w1_ref[...], preferred_element_type=F32)
                  + jnp.dot(a2_ref[...], w2_ref[...], preferred_element_type=F32))


def _out_proj(a1, a2, w, res, *, tm, tn):
    t, k1 = a1.shape
    k2 = a2.shape[1]
    n = w.shape[1]
    assert t % tm == 0 and n % tn == 0 and w.shape[0] == k1 + k2 and k1 % k2 == 0
    return pl.pallas_call(
        _out_proj_body,
        grid=(t // tm, n // tn),
        in_specs=[pl.BlockSpec((tm, k1), lambda i, j: (i, 0)),
                  pl.BlockSpec((tm, k2), lambda i, j: (i, 0)),
                  pl.BlockSpec((k1, tn), lambda i, j: (0, j)),
                  pl.BlockSpec((k2, tn), lambda i, j: (k1 // k2, j)),
                  pl.BlockSpec((tm, tn), lambda i, j: (i, j))],
        out_specs=pl.BlockSpec((tm, tn), lambda i, j: (i, j)),
        out_shape=jax.ShapeDtypeStruct((t, n), F32),
        compiler_params=_params("parallel", "parallel"),
        name="out_proj",
    )(a1, a2, w, w, res)


def _mla_proj_body(x_ref, g_ref, wl_ref, qn_ref, wq_ref, kvn_ref, wkv_ref, c64_ref, s64_ref, q_ref, kv_ref, kr_ref,
                   *, q_lora, kv_lora, q_scale):
    half = x_ref.shape[0] // 2
    lat = q_lora + kv_lora
    for rows in (slice(0, half), slice(half, 2 * half)):
        xn = _rms(x_ref[rows, :], g_ref[...]).astype(BF16)
        latent = jnp.dot(xn, wl_ref[...], preferred_element_type=F32)
        k_rope = latent[:, lat:]
        kr_ref[rows, :] = (k_rope * c64_ref[rows, :] + pltpu.roll(k_rope, 64, 1) * s64_ref[rows, :]).astype(kr_ref.dtype)
        c_q = _rms(latent[:, :q_lora], qn_ref[...]).astype(BF16)
        c_kv = _rms(latent[:, q_lora:lat], kvn_ref[...]).astype(BF16)
        kv_ref[rows, :] = jnp.dot(c_kv, wkv_ref[...], preferred_element_type=F32).astype(kv_ref.dtype)
        q = jnp.dot(c_q, wq_ref[...], preferred_element_type=F32) * q_scale
        for c in range(q_ref.shape[1] // LANES):
            a = q[:, _lanes(c)]
            if c % 2 == 1:
                a = a * c64_ref[rows, :] + pltpu.roll(a, 64, 1) * s64_ref[rows, :]
            q_ref[rows, _lanes(c)] = a.astype(q_ref.dtype)


def _mla_proj(x, gain, w_latent, q_norm, w_q, kv_norm, w_kv, rope, *, seq, tm, q_scale):
    t, d = x.shape
    q_lora, kv_lora = w_q.shape[0], w_kv.shape[0]
    assert t % tm == 0 and seq % tm == 0 and w_latent.shape[1] == q_lora + kv_lora + LANES
    seq_tiles = seq // tm
    const = lambda i: (0, 0)
    tab = pl.BlockSpec((tm, LANES), lambda i: (i % seq_tiles, 0))
    full = lambda a: pl.BlockSpec(a.shape, const)
    out_spec = lambda width: pl.BlockSpec((tm, width), lambda i: (i, 0))
    widths = (w_q.shape[1], w_kv.shape[1], LANES)
    return pl.pallas_call(
        functools.partial(_mla_proj_body, q_lora=q_lora, kv_lora=kv_lora, q_scale=q_scale),
        grid=(t // tm,),
        in_specs=[pl.BlockSpec((tm, d), lambda i: (i, 0)), pl.BlockSpec((1, d), const), full(w_latent),
                  pl.BlockSpec((1, q_lora), const), full(w_q), pl.BlockSpec((1, kv_lora), const), full(w_kv),
                  tab, tab],
        out_specs=[out_spec(w) for w in widths],
        out_shape=[jax.ShapeDtypeStruct((t, w), BF16) for w in widths],
        compiler_params=_params("parallel"),
        name="mla_proj",
    )(x, gain.reshape(1, d), w_latent, q_norm.reshape(1, q_lora), w_q, kv_norm.reshape(1, kv_lora), w_kv,
      rope[2], rope[3])


def _ffn_body(x_ref, g_ref, wg_ref, wu_ref, wd_ref, o_ref, xn_ref):
    j = pl.program_id(1)

    def partial_out(xn):
        g = jnp.dot(xn, wg_ref[...], preferred_element_type=F32)
        u = jnp.dot(xn, wu_ref[...], preferred_element_type=F32)
        h = (g * jax.nn.sigmoid(g) * u).astype(BF16)
        return jnp.dot(h, wd_ref[...], preferred_element_type=F32)

    @pl.when(j == 0)
    def _():
        half = x_ref.shape[0] // 2
        for rows in (slice(0, half), slice(half, 2 * half)):
            xn = _rms(x_ref[rows, :], g_ref[...]).astype(BF16)
            xn_ref[rows, :] = xn
            o_ref[rows, :] = partial_out(xn)

    @pl.when(j > 0)
    def _():
        o_ref[...] += partial_out(xn_ref[...])

    @pl.when(j == pl.num_programs(1) - 1)
    def _():
        o_ref[...] = x_ref[...] + 0.5 * o_ref[...]


def _ffn(x, gain, w_gu, w_down, layer, half, *, tm, tf):
    t, d = x.shape
    d_ff = w_down.shape[2]
    assert t % tm == 0 and d_ff % tf == 0
    nf = d_ff // tf
    return pl.pallas_call(
        _ffn_body,
        grid=(t // tm, nf),
        in_specs=[pl.BlockSpec((tm, d), lambda i, j: (i, 0)),
                  pl.BlockSpec((1, d), lambda i, j: (0, 0)),
                  pl.BlockSpec((None, None, d, tf), lambda i, j: (layer, half, 0, j)),
                  pl.BlockSpec((None, None, d, tf), lambda i, j: (layer, half, 0, nf + j)),
                  pl.BlockSpec((None, None, tf, d), lambda i, j: (layer, half, j, 0))],
        out_specs=pl.BlockSpec((tm, d), lambda i, j: (i, 0)),
        out_shape=jax.ShapeDtypeStruct((t, d), F32),
        scratch_shapes=[pltpu.VMEM((tm, d), BF16)],
        compiler_params=_params("parallel", "arbitrary"),
        name="ffn",
    )(x, gain.reshape(1, d), w_gu, w_gu, w_down)


def _ple_body(x_ref, p_ref, g_ref, wg_ref, wp_ref, fg_ref, o_ref, *, final_norm):
    half = x_ref.shape[0] // 2
    for rows in (slice(0, half), slice(half, 2 * half)):
        x = x_ref[rows, :]
        xn = _rms(x, g_ref[...]).astype(BF16)
        gate = jax.nn.sigmoid(jnp.dot(xn, wg_ref[...], preferred_element_type=F32))
        proj = jnp.dot(p_ref[rows, :].astype(BF16), wp_ref[...], preferred_element_type=F32)
        y = x + gate * proj
        if final_norm:
            y = _rms(y, fg_ref[...])
        o_ref[rows, :] = y


def _ple(x, p, layer, gain, w_gate, w_proj, final_gain, *, tm, final_norm):
    t, d = x.shape
    pd = p.shape[1]
    assert t % tm == 0
    first_tile = layer * (t // tm)
    const = lambda i: (0, 0)
    return pl.pallas_call(
        functools.partial(_ple_body, final_norm=final_norm),
        grid=(t // tm,),
        in_specs=[pl.BlockSpec((tm, d), lambda i: (i, 0)),
                  pl.BlockSpec((tm, pd), lambda i: (first_tile + i, 0)),
                  pl.BlockSpec((1, d), const),
                  pl.BlockSpec((d, d), const),
                  pl.BlockSpec((pd, d), const),
                  pl.BlockSpec((1, d), const)],
        out_specs=pl.BlockSpec((tm, d), lambda i: (i, 0)),
        out_shape=jax.ShapeDtypeStruct((t, d), F32),
        compiler_params=_params("parallel"),
        name="ple",
    )(x, p, gain.reshape(1, d), w_gate, w_proj, final_gain.reshape(1, d))


def _dot_nt(a, b):
    return lax.dot_general(a, b, (((1,), (1,)), ((), ())), preferred_element_type=F32)


def _lanes(c, width=LANES):
    return slice(c * width, (c + 1) * width)


def _fill_transposed(vt_ref, v_ref, lane_offsets, chunk):
    dv = vt_ref.shape[1] - SUM_ROWS
    ones_row = jnp.where(lax.broadcasted_iota(jnp.int32, (SUM_ROWS, chunk), 0) == 0, 1.0, 0.0).astype(BF16)

    def body(j, _):
        start = pl.multiple_of(j * chunk, chunk)
        for c, off in enumerate(lane_offsets):
            vt_ref[c, :dv, pl.ds(start, chunk)] = v_ref[pl.ds(start, chunk), off:off + dv].T
            vt_ref[c, dv:, pl.ds(start, chunk)] = ones_row
        return 0

    lax.fori_loop(0, v_ref.shape[0] // chunk, body, 0)


def _flash_scratch(heads, tk, cols, dv, seq):
    return [pltpu.VMEM((heads, dv + SUM_ROWS, seq), BF16),
            pltpu.VMEM((heads, tk, cols), F32), pltpu.VMEM((heads, tk, cols), BF16),
            pltpu.VMEM((heads, dv + SUM_ROWS, cols), F32)]


def _causal_diag_masks(tk, tq, cols):
    key_row = lax.broadcasted_iota(jnp.int32, (tk, 1), 0)
    query_col = lax.broadcasted_iota(jnp.int32, (1, cols), 1) % tq
    return [key_row + d * tk <= query_col for d in range(tq // tk)]


def _causal_col_ranges(tk, tq, maps):
    return lambda d: [(g * tq + d * tk, (g + 1) * tq) for g in range(maps)]


def _causal_flash_t(score_fns, vt_ref, s_ref, p_ref, acc_ref, n_past, diag_mask_fn, diag_cols_fn, n_diag,
                    past_mask_fns=None):
    heads = range(len(score_fns))
    _, tk, cols = s_ref.shape

    everything = [(0, cols)]

    def stage_scores(block, ranges=everything):
        start = pl.multiple_of(block * tk, tk)
        for c in heads:
            for lo, hi in ranges:
                s = score_fns[c](start, slice(lo, hi))
                if past_mask_fns is not None:
                    s = jnp.where(past_mask_fns[c](block)[:, lo:hi], s, -jnp.inf)
                s_ref[c, :, lo:hi] = s

    def accumulate(block, alphas, ranges=everything):
        start = pl.multiple_of(block * tk, tk)
        for c in heads:
            for lo, hi in ranges:
                pv = jnp.dot(vt_ref[c, :, pl.ds(start, tk)], p_ref[c, :, lo:hi], preferred_element_type=F32)
                acc_ref[c, :, lo:hi] = alphas[c][:, lo:hi] * acc_ref[c, :, lo:hi] + pv

    def softmax(stats, diag=None, ranges=everything):
        new_stats, alphas = [], []
        for c in heads:
            m = stats[c]
            if diag is None:
                read = lambda lo, hi: s_ref[c, :, lo:hi]
            else:
                visible = diag_mask_fn(diag, c)
                read = lambda lo, hi: jnp.where(visible[:, lo:hi], s_ref[c, :, lo:hi], -jnp.inf)
            pieces, at = [], 0
            for lo, hi in ranges:
                if lo > at:
                    pieces.append(m[:, at:lo])
                pieces.append(jnp.maximum(m[:, lo:hi], jnp.max(read(lo, hi), axis=0, keepdims=True)))
                at = hi
            if at < cols:
                pieces.append(m[:, at:])
            m_new = pieces[0] if len(pieces) == 1 else jnp.concatenate(pieces, axis=1)
            alpha = jnp.exp2(m - m_new)
            for lo, hi in ranges:
                p_ref[c, :, lo:hi] = jnp.exp2(read(lo, hi) - m_new[:, lo:hi]).astype(BF16)
            new_stats.append(m_new)
            alphas.append(alpha)
        return tuple(new_stats), tuple(alphas)

    def past_block(j, carry):
        stats, alphas = carry
        accumulate(jnp.maximum(j - 1, 0), alphas)
        stats, alphas = softmax(stats)
        stage_scores(j + 1)
        return stats, alphas

    for c in heads:
        p_ref[c] = jnp.zeros(p_ref.shape[1:], BF16)
        acc_ref[c] = jnp.zeros(acc_ref.shape[1:], F32)
    stage_scores(0)
    init = (tuple(jnp.full((1, cols), jnp.finfo(F32).min, F32) for _ in heads),
            tuple(jnp.ones((1, cols), F32) for _ in heads))
    def past_group(t, carry):
        for u in range(PAST_BLOCKS_PER_ITERATION):
            carry = past_block(t * PAST_BLOCKS_PER_ITERATION + u, carry)
        return carry

    n_grouped = n_past // PAST_BLOCKS_PER_ITERATION
    carry = lax.fori_loop(0, n_grouped, past_group, init)
    stats, alphas = lax.fori_loop(n_grouped * PAST_BLOCKS_PER_ITERATION, n_past, past_block, carry)

    accumulate(jnp.maximum(n_past - 1, 0), alphas)
    for d in range(n_diag):
        stats, alphas = softmax(stats, diag=d, ranges=diag_cols_fn(d))
        if d + 1 < n_diag:
            stage_scores(n_past + d + 1, ranges=diag_cols_fn(d + 1))
        accumulate(n_past + d, alphas, ranges=diag_cols_fn(d))
    dv = acc_ref.shape[1] - SUM_ROWS
    return tuple((acc_ref[c, dv:dv + 1, :], acc_ref[c, :dv, :]) for c in heads)


def _diff_attn_body(lam_ref, subln_ref, q_ref, k_ref, v_ref, o_ref, qt_ref, vt_ref, s_ref, p_ref, acc_ref, *,
                    tq, tk, heads, lam_init):
    qi = pl.program_id(2)

    @pl.when(qi == 0)
    def _():
        _fill_transposed(vt_ref, v_ref, [c * LANES for c in range(heads)], tk)

    is_map0 = (lax.broadcasted_iota(jnp.int32, (LANES, 1), 0) % 64) < 32

    def make_score_fn(c):
        q_t = q_ref[:, _lanes(c)].T
        zero = jnp.zeros_like(q_t)
        qt_ref[c] = jnp.concatenate([jnp.where(is_map0, q_t, zero), jnp.where(is_map0, zero, q_t)], axis=1)
        return lambda start, cols: jnp.dot(k_ref[pl.ds(start, tk), _lanes(c)], qt_ref[c, :, cols],
                                           preferred_element_type=F32)

    diag = _causal_diag_masks(tk, tq, 2 * tq)
    results = _causal_flash_t([make_score_fn(c) for c in range(heads)], vt_ref, s_ref, p_ref, acc_ref,
                              qi * (tq // tk), lambda d, c: diag[d], _causal_col_ranges(tk, tq, 2), tq // tk)

    lp = lam_ref[...]
    lam = (jnp.exp(jnp.sum(lp[0:1] * lp[1:2], axis=-1, keepdims=True))
           - jnp.exp(jnp.sum(lp[2:3] * lp[3:4], axis=-1, keepdims=True)) + lam_init)
    for c, (l, acc) in enumerate(results):
        o_t = acc / l
        o = (o_t[:, :tq] - lam * o_t[:, tq:]).T
        o_ref[:, _lanes(c)] = (_rms(o, subln_ref[...]) * (1.0 - lam_init)).astype(o_ref.dtype)


def _diff_attention(qkv, lam_params, subln, *, batch, seq, tq, tk, heads, q_col, k_col, v_col, lam_init):
    nq = seq // tq
    width = heads * LANES
    assert A_HEADS % heads == 0 and q_col % heads == 0 and k_col % heads == 0 and v_col % heads == 0
    assert seq % tq == 0 and tq % tk == 0
    return pl.pallas_call(
        functools.partial(_diff_attn_body, tq=tq, tk=tk, heads=heads, lam_init=lam_init),
        grid=(batch, A_HEADS // heads, nq),
        in_specs=[pl.BlockSpec((4, A_QK_DIM), lambda b, h, i: (0, 0)),
                  pl.BlockSpec((1, LANES), lambda b, h, i: (0, 0)),
                  pl.BlockSpec((tq, width), lambda b, h, i: (b * nq + i, q_col // heads + h)),
                  pl.BlockSpec((seq, width), lambda b, h, i: (b, k_col // heads + h)),
                  pl.BlockSpec((seq, width), lambda b, h, i: (b, v_col // heads + h))],
        out_specs=pl.BlockSpec((tq, width), lambda b, h, i: (b * nq + i, h)),
        out_shape=jax.ShapeDtypeStruct((batch * seq, A_HEADS * LANES), BF16),
        scratch_shapes=[pltpu.VMEM((heads, LANES, 2 * tq), BF16)] + _flash_scratch(heads, tk, 2 * tq, LANES, seq),
        compiler_params=_params("parallel", "parallel", "arbitrary"),
        name="diff_attn",
    )(lam_params, subln.reshape(1, LANES), qkv, qkv, qkv)


def _tile_index(ref, r, i, tile):
    per_slab = ref.shape[2]
    if per_slab >= tile:
        runs = per_slab // tile
        return i // runs, r, pl.ds(pl.multiple_of((i % runs) * tile, tile), tile)
    slabs = tile // per_slab
    return pl.ds(i * slabs, slabs), r, slice(None)


def _load_tile(ref, r, i, c, tile):
    slab, res, rows = _tile_index(ref, r, i, tile)
    return ref[slab, res, rows, _lanes(c)].reshape(tile, LANES)


def _store_tile(ref, r, i, c, tile, value):
    slab, res, rows = _tile_index(ref, r, i, tile)
    per_slab = ref.shape[2]
    ref[slab, res, rows, _lanes(c)] = (value if per_slab >= tile
                                       else value.reshape(tile // per_slab, per_slab, LANES))


def _dilated_body(q_ref, k_ref, v_ref, o_ref, lse_ref, *, tile, heads):
    slabs, residues, per_slab, _ = q_ref.shape
    row = lax.broadcasted_iota(jnp.int32, (tile, 1), 0)
    streams = [(r, c) for r in range(residues) for c in range(heads)]

    def attend(r, i, c, k, v, valid):
        s = jnp.where(valid, _dot_nt(_load_tile(q_ref, r, i, c, tile), k), -jnp.inf)
        m = jnp.max(s, axis=-1, keepdims=True)
        p = jnp.exp(s - m)
        l = jnp.sum(p, axis=-1, keepdims=True)
        o = jnp.dot(p.astype(BF16), v, preferred_element_type=F32) / l
        _store_tile(o_ref, r, i, c, tile, o)
        _store_tile(lse_ref, r, i, c, tile, jnp.broadcast_to(m + jnp.log(l), (tile, LANES)))

    col = lax.broadcasted_iota(jnp.int32, (1, tile), 1)
    for r, c in streams:
        attend(r, 0, c, _load_tile(k_ref, r, 0, c, tile), _load_tile(v_ref, r, 0, c, tile), col <= row)

    col2 = lax.broadcasted_iota(jnp.int32, (1, 2 * tile), 1)
    band = jnp.logical_and(col2 >= row, col2 <= row + tile)

    def body(i, _):
        for r, c in streams:
            k = jnp.concatenate([_load_tile(k_ref, r, i - 1, c, tile), _load_tile(k_ref, r, i, c, tile)], axis=0)
            v = jnp.concatenate([_load_tile(v_ref, r, i - 1, c, tile), _load_tile(v_ref, r, i, c, tile)], axis=0)
            attend(r, i, c, k, v, band)
        return 0

    n_tiles = slabs * per_slab // tile
    lax.fori_loop(1, n_tiles, body, 0, unroll=max(1, min(4, n_tiles - 1)))


def _dilated_group(qkv, *, batch, seq, row_tile, group, dil, heads, residues, q_col, k_col, v_col):
    hpg = B_HEADS_PER_GROUP
    tile = B_GROUPS[group][0] // dil
    slabs, per_slab = seq // row_tile, row_tile // dil
    assert tile == LANES and hpg % heads == 0 and (seq // dil) % tile == 0 and dil % residues == 0
    assert per_slab % tile == 0 or tile % per_slab == 0
    width = heads * LANES
    view = qkv.reshape(batch, slabs, dil, per_slab, qkv.shape[1])

    def in_spec(col):
        first = col + group * hpg
        assert first % heads == 0
        return pl.BlockSpec((None, slabs, residues, per_slab, width),
                            lambda b, h, r: (b, 0, r, 0, first // heads + h))

    out_spec = pl.BlockSpec((None, slabs, residues, per_slab, width), lambda b, h, r: (b, 0, r, 0, h))
    out_sds = jax.ShapeDtypeStruct((batch, slabs, dil, per_slab, hpg * LANES), F32)
    o, lse = pl.pallas_call(
        functools.partial(_dilated_body, tile=tile, heads=heads),
        grid=(batch, hpg // heads, dil // residues),
        in_specs=[in_spec(q_col), in_spec(k_col), in_spec(v_col)],
        out_specs=[out_spec, out_spec],
        out_shape=[out_sds, out_sds],
        compiler_params=_params("parallel", "parallel", "parallel"),
        name=f"dilated_g{group}",
    )(view, view, view)
    shape = (batch * slabs, dil, per_slab, hpg * LANES)
    return o.reshape(shape), lse.reshape(shape)


def _merge_body(o0_ref, o1_ref, o2_ref, l0_ref, l1_ref, l2_ref, out_ref, nat_ref):
    def natural(ref, slot, c):
        dil, rows = ref.shape[0], ref.shape[1]
        if dil == 1:
            return ref[0, :, _lanes(c)]
        for r in range(dil):
            nat_ref[slot, pl.ds(r, rows, stride=dil), :] = ref[r, :, _lanes(c)]
        return nat_ref[slot]

    for c in range(out_ref.shape[1] // LANES):
        o0, o1, o2 = natural(o0_ref, None, c), natural(o1_ref, 0, c), natural(o2_ref, 1, c)
        l0, l1, l2 = natural(l0_ref, None, c), natural(l1_ref, 2, c), natural(l2_ref, 3, c)
        m = jnp.maximum(jnp.maximum(l0, l1), l2)
        w0, w1, w2 = jnp.exp(l0 - m), jnp.exp(l1 - m), jnp.exp(l2 - m)
        out_ref[:, _lanes(c)] = ((w0 * o0 + w1 * o1 + w2 * o2) / (w0 + w1 + w2)).astype(out_ref.dtype)


def _merge_groups(outs, lses):
    tiles, _, _, width = outs[0].shape
    row_tile = outs[0].shape[1] * outs[0].shape[2]
    specs = [pl.BlockSpec((None,) + a.shape[1:], lambda i: (i, 0, 0, 0)) for a in (*outs, *lses)]
    return pl.pallas_call(
        _merge_body,
        grid=(tiles,),
        in_specs=specs,
        out_specs=pl.BlockSpec((row_tile, width), lambda i: (i, 0)),
        out_shape=jax.ShapeDtypeStruct((tiles * row_tile, width), BF16),
        scratch_shapes=[pltpu.VMEM((4, row_tile, LANES), F32)],
        compiler_params=_params("parallel"),
        name="dilated_merge",
    )(*outs, *lses)


def _mla_body(q_ref, kv_ref, kr_ref, o_ref, qt_ref, vt_ref, s_ref, p_ref, acc_ref, *, tq, tk, heads):
    qi = pl.program_id(2)

    @pl.when(qi == 0)
    def _():
        _fill_transposed(vt_ref, kv_ref, [(2 * c + 1) * LANES for c in range(heads)], tk)

    def make_score_fn(c):
        qt_ref[c] = q_ref[:, _lanes(c, 2 * LANES)].T

        def score(start, cols):
            k = jnp.concatenate([kv_ref[pl.ds(start, tk), _lanes(2 * c)], kr_ref[pl.ds(start, tk), :]], axis=1)
            return jnp.dot(k, qt_ref[c, :, cols], preferred_element_type=F32)

        return score

    diag = _causal_diag_masks(tk, tq, tq)
    results = _causal_flash_t([make_score_fn(c) for c in range(heads)], vt_ref, s_ref, p_ref, acc_ref,
                              qi * (tq // tk), lambda d, c: diag[d], _causal_col_ranges(tk, tq, 1), tq // tk)
    for c, (l, acc) in enumerate(results):
        o_ref[:, _lanes(c)] = (acc / l).T.astype(o_ref.dtype)


def _mla_attention(q, kv, k_rope_src, *, batch, seq, tq, tk, heads, k_rope_col):
    nq = seq // tq
    assert C_HEADS % heads == 0 and seq % tq == 0 and tq % tk == 0
    return pl.pallas_call(
        functools.partial(_mla_body, tq=tq, tk=tk, heads=heads),
        grid=(batch, C_HEADS // heads, nq),
        in_specs=[pl.BlockSpec((tq, heads * 2 * LANES), lambda b, h, i: (b * nq + i, h)),
                  pl.BlockSpec((seq, heads * 2 * LANES), lambda b, h, i: (b, h)),
                  pl.BlockSpec((seq, LANES), lambda b, h, i: (b, k_rope_col))],
        out_specs=pl.BlockSpec((tq, heads * LANES), lambda b, h, i: (b * nq + i, h)),
        out_shape=jax.ShapeDtypeStruct((batch * seq, C_HEADS * LANES), BF16),
        scratch_shapes=[pltpu.VMEM((heads, 2 * LANES, tq), BF16)] + _flash_scratch(heads, tk, tq, LANES, seq),
        compiler_params=_params("parallel", "parallel", "arbitrary"),
        name="mla_attn",
    )(q, kv, k_rope_src)


def _moba_body(q_ref, k_ref, v_ref, o_ref, kmean_ref, sel_ref, qt_ref, vt_ref, s_ref, p_ref, acc_ref, *,
               n_blk, tq, heads):
    qi = pl.program_id(2)
    blk = MOBA_BLOCK
    n_own = tq // blk

    @pl.when(qi == 0)
    def _():
        for c in range(heads):
            k_all = k_ref[:, _lanes(c)].astype(F32).reshape(n_blk, blk, LANES)
            kmean_ref[c] = jnp.mean(k_all, axis=1)
        _fill_transposed(vt_ref, v_ref, [c * LANES for c in range(heads)], blk)

    blk_id = lax.broadcasted_iota(jnp.int32, (n_blk, 1), 0)
    query_col = lax.broadcasted_iota(jnp.int32, (1, tq), 1)
    own_in_tile = query_col // blk
    past = blk_id < qi * n_own + own_in_tile
    causal = lax.broadcasted_iota(jnp.int32, (blk, 1), 0) <= query_col % blk

    def make_score_fn(c):
        q_t = q_ref[:, _lanes(c)].T
        km = kmean_ref[c]
        km_hi = km.astype(BF16)
        rem = km - km_hi.astype(F32)
        km_mid = rem.astype(BF16)
        km_lo = (rem - km_mid.astype(F32)).astype(BF16)
        gate = (jnp.dot(km_hi, q_t, preferred_element_type=F32) + jnp.dot(km_mid, q_t, preferred_element_type=F32)
                + jnp.dot(km_lo, q_t, preferred_element_type=F32))
        gate = jnp.where(past, gate, -jnp.inf)
        selected = jnp.zeros(gate.shape, F32)
        for _ in range(min(MOBA_TOPK, n_blk - 1)):
            best = jnp.max(gate, axis=0, keepdims=True)
            first = jnp.min(jnp.where(gate == best, blk_id, n_blk), axis=0, keepdims=True)
            pick = blk_id == first
            selected = jnp.where(jnp.logical_and(pick, past), 1.0, selected)
            gate = jnp.where(pick, -jnp.inf, gate)
        sel_ref[c] = selected
        qt_ref[c] = q_t
        return lambda start, cols: jnp.dot(k_ref[pl.ds(start, blk), _lanes(c)], qt_ref[c, :, cols],
                                           preferred_element_type=F32)

    score_fns = [make_score_fn(c) for c in range(heads)]
    n_past = qi * n_own
    chosen_fns = [lambda n, c=c: jnp.logical_or(sel_ref[c, pl.ds(n, 1), :] > 0.0, n >= n_past)
                  for c in range(heads)]

    def own_block_mask(d, c):
        chosen = sel_ref[c, pl.ds(n_past + d, 1), :] > 0.0
        return jnp.logical_or(jnp.logical_and(own_in_tile == d, causal),
                              jnp.logical_and(own_in_tile > d, chosen))

    results = _causal_flash_t(score_fns, vt_ref, s_ref, p_ref, acc_ref, n_past, own_block_mask,
                              _causal_col_ranges(blk, tq, 1), n_own,
                              past_mask_fns=chosen_fns)
    for c, (l, acc) in enumerate(results):
        o_ref[:, _lanes(c)] = (acc / l).T.astype(o_ref.dtype)


def _moba_attention(qkv, *, batch, seq, tq, heads, q_col, k_col, v_col):
    blk = MOBA_BLOCK
    n_blk = seq // blk
    nq = seq // tq
    width = heads * LANES
    assert seq % tq == 0 and tq % blk == 0 and D_HEADS % heads == 0
    assert q_col % heads == 0 and k_col % heads == 0 and v_col % heads == 0
    return pl.pallas_call(
        functools.partial(_moba_body, n_blk=n_blk, tq=tq, heads=heads),
        grid=(batch, D_HEADS // heads, nq),
        in_specs=[pl.BlockSpec((tq, width), lambda b, h, i: (b * nq + i, q_col // heads + h)),
                  pl.BlockSpec((seq, width), lambda b, h, i: (b, k_col // heads + h)),
                  pl.BlockSpec((seq, width), lambda b, h, i: (b, v_col // heads + h))],
        out_specs=pl.BlockSpec((tq, width), lambda b, h, i: (b * nq + i, h)),
        out_shape=jax.ShapeDtypeStruct((batch * seq, D_HEADS * LANES), BF16),
        scratch_shapes=[pltpu.VMEM((heads, n_blk, LANES), F32),
                        pltpu.VMEM((heads, n_blk, tq), F32),
                        pltpu.VMEM((heads, LANES, tq), BF16)] + _flash_scratch(heads, blk, tq, LANES, seq),
        compiler_params=_params("parallel", "parallel", "arbitrary"),
        name="moba_attn",
    )(qkv, qkv, qkv)


def _rope_tables(seq):
    pos = jnp.arange(seq, dtype=F32)[:, None]

    def angles(dim):
        inv = 1.0 / (ROPE_THETA ** (jnp.arange(0, dim, 2, dtype=F32) / dim))
        return pos * inv[None, :]

    a128 = angles(HEAD_DIM)
    a64 = angles(A_QK_DIM)
    c128 = jnp.concatenate([jnp.cos(a128)] * 2, axis=1)
    s128 = jnp.concatenate([-jnp.sin(a128), jnp.sin(a128)], axis=1)
    c64 = jnp.concatenate([jnp.cos(a64)] * 4, axis=1)
    s64 = jnp.concatenate([-jnp.sin(a64)] * 2 + [jnp.sin(a64)] * 2, axis=1)
    return c128, s128, c64, s64


def _interleave_diff_heads(w):
    d = w.shape[0]
    return w.reshape(d, A_HEADS, 2, 2, 32).transpose(0, 1, 3, 2, 4).reshape(d, A_HEADS * LANES)


def _spread_rope64(w):
    z = jnp.zeros(w.shape[:-1] + (32,), w.dtype)
    return jnp.concatenate([w[..., :32], z, w[..., 32:], z], axis=-1)


def _ab_mixer(x, gain, w_in, lam_params, subln, w_out, layer_idx, rope, *, batch, seq):
    aw = A_HEADS * LANES
    w = jnp.concatenate([_interleave_diff_heads(w_in[:, :aw]), _interleave_diff_heads(w_in[:, aw:2 * aw]),
                         w_in[:, 2 * aw:]], axis=1).astype(BF16)
    a_scale = A_QK_DIM ** -0.5 * LOG2_E
    b_scale = HEAD_DIM ** -0.5
    dils = [dil for _, dil in B_GROUPS for _ in range(B_HEADS_PER_GROUP)]
    modes = ([(ROPE64, a_scale, 1)] * A_HEADS + [(ROPE64, 1.0, 1)] * A_HEADS + [(PLAIN, 1.0, 1)] * A_HEADS
             + [(ROPE128, b_scale, d) for d in dils] + [(ROPE128, 1.0, d) for d in dils]
             + [(PLAIN, 1.0, d) for d in dils])
    row_tile = 1024
    qkv = _norm_proj(x, gain, w, rope, modes, seq=seq, tm=row_tile, tn=1280, out_dtype=BF16)
    lam_init = 0.8 - 0.6 * math.exp(-0.3 * layer_idx)
    oa = _diff_attention(qkv, lam_params, subln, batch=batch, seq=seq, tq=2048, tk=256, heads=2,
                         q_col=0, k_col=A_HEADS, v_col=2 * A_HEADS, lam_init=lam_init)
    b0 = 3 * A_HEADS
    outs, lses = [], []
    for g, (_, dil) in enumerate(B_GROUPS):
        o, lse = _dilated_group(qkv, batch=batch, seq=seq, row_tile=row_tile, group=g, dil=dil,
                                heads=2 if dil == 1 else 4, residues=4 if dil == 16 else 1,
                                q_col=b0, k_col=b0 + B_HEADS, v_col=b0 + 2 * B_HEADS)
        outs.append(o)
        lses.append(lse)
    ob = _merge_groups(outs, lses)
    return _out_proj(oa, ob, w_out.astype(BF16), x, tm=1024, tn=1024)


def _cd_mixer(x, gain, w_in, q_norm, w_uq, kv_norm, w_ukv, w_out, rope, *, batch, seq):
    lat = C_Q_LORA + C_KV_LORA
    d_scale = HEAD_DIM ** -0.5 * LOG2_E
    modes = [(ROPE128, d_scale, 1)] * D_HEADS + [(ROPE128, 1.0, 1)] * D_HEADS + [(PLAIN, 1.0, 1)] * D_HEADS
    main = _norm_proj(x, gain, w_in[:, lat + C_ROPE:].astype(BF16), rope, modes, seq=seq, tm=1024, tn=1024,
                      out_dtype=BF16)

    c_scale = (C_NOPE + C_ROPE) ** -0.5 * LOG2_E
    w_q = w_uq.reshape(C_Q_LORA, C_HEADS, C_NOPE + C_ROPE)
    w_q = jnp.concatenate([w_q[..., :C_NOPE], _spread_rope64(w_q[..., C_NOPE:])], axis=-1)
    w_q = w_q.reshape(C_Q_LORA, C_HEADS * 2 * LANES).astype(BF16)
    w_latent = jnp.concatenate([w_in[:, :lat], _spread_rope64(w_in[:, lat:lat + C_ROPE])], axis=1).astype(BF16)
    qc, kv, k_rope = _mla_proj(x, gain, w_latent, q_norm, w_q, kv_norm, w_ukv.astype(BF16), rope,
                               seq=seq, tm=512, q_scale=c_scale)
    oc = _mla_attention(qc, kv, k_rope, batch=batch, seq=seq, tq=2048, tk=256, heads=2, k_rope_col=0)
    od = _moba_attention(main, batch=batch, seq=seq, tq=1024, heads=4,
                         q_col=0, k_col=D_HEADS, v_col=2 * D_HEADS)
    return _out_proj(oc, od, w_out.astype(BF16), x, tm=1024, tn=1024)


def kernel(x, p, ffn_norm, ffn_w_gu, ffn_w_down, mix_norm, ab_w_in, ab_lambda, ab_subln, ab_w_out,
           cd_w_in, cd_q_norm, cd_w_uq, cd_kv_norm, cd_w_ukv, cd_w_out, ple_norm, ple_w_gate,
           ple_w_proj, final_norm):
    batch, seq, d = x.shape
    depth = p.shape[0]
    rope = _rope_tables(seq)
    w_gu, w_down = ffn_w_gu.astype(BF16), ffn_w_down.astype(BF16)
    x = x.reshape(batch * seq, d)
    for i in range(depth):
        j = i // 2
        x = _ffn(x, ffn_norm[i, 0], w_gu, w_down, i, 0, tm=1024, tf=512)
        if i % 2 == 0:
            x = _ab_mixer(x, mix_norm[i], ab_w_in[j], ab_lambda[j], ab_subln[j], ab_w_out[j], i, rope,
                          batch=batch, seq=seq)
        else:
            x = _cd_mixer(x, mix_norm[i], cd_w_in[j], cd_q_norm[j], cd_w_uq[j], cd_kv_norm[j], cd_w_ukv[j],
                          cd_w_out[j], rope, batch=batch, seq=seq)
        x = _ffn(x, ffn_norm[i, 1], w_gu, w_down, i, 1, tm=1024, tf=512)
        x = _ple(x, p.reshape(depth * batch * seq, -1), i, ple_norm[i], ple_w_gate[i].astype(BF16),
                 ple_w_proj[i].astype(BF16), final_norm, tm=512, final_norm=(i == depth - 1))
    return x.reshape(batch, seq, d)
```

```python
import functools
import math

import jax
import jax.numpy as jnp
from jax import lax
from jax.experimental import pallas as pl
from jax.experimental.pallas import tpu as pltpu

F32 = jnp.float32
BF16 = jnp.bfloat16

D_MODEL = 2048
D_FF = 5632
PLE_DIM = 256
HEAD_DIM = 128
ROPE_THETA = 10000.0
NORM_EPS = 1e-6

A_HEADS = 8
A_QK_DIM = 64
B_GROUPS = ((128, 1), (512, 4), (2048, 16))
B_HEADS_PER_GROUP = 4
B_HEADS = B_HEADS_PER_GROUP * len(B_GROUPS)
C_HEADS = 8
C_Q_LORA = 512
C_KV_LORA = 256
C_NOPE = 128
C_ROPE = 64
D_HEADS = 8
MOBA_BLOCK = 256
MOBA_TOPK = 3

LOG2_E = 1.0 / math.log(2.0)
PAST_BLOCKS_PER_ITERATION = 4
SUM_ROWS = 16
LANES = 128
VMEM_LIMIT = 56 * 1024 * 1024

PLAIN, ROPE128, ROPE64 = 0, 1, 2


def _rms(x, gain):
    return x * lax.rsqrt(jnp.mean(x * x, axis=-1, keepdims=True) + NORM_EPS) * gain


def _params(*semantics):
    return pltpu.CompilerParams(dimension_semantics=semantics, vmem_limit_bytes=VMEM_LIMIT)


def _norm_proj_body(x_ref, g_ref, w_ref, c128_ref, s128_ref, c64_ref, s64_ref, o_ref, xn_ref, stage_ref, *,
                    tile_patterns):
    j = pl.program_id(1)
    tm = o_ref.shape[0]

    @pl.when(j == 0)
    def _():
        xn_ref[...] = _rms(x_ref[...], g_ref[...]).astype(BF16)

    def tile(pattern):
        acc = jnp.dot(xn_ref[...], w_ref[...], preferred_element_type=F32)
        for c, (kind, scale, dil) in enumerate(pattern):
            a = acc[:, _lanes(c)]
            if scale != 1.0:
                a = a * scale
            if kind == ROPE128:
                a = a * c128_ref[...] + pltpu.roll(a, 64, 1) * s128_ref[...]
            elif kind == ROPE64:
                a = a * c64_ref[...] + pltpu.roll(a, 64, 1) * s64_ref[...]
            if dil == 1:
                o_ref[:, _lanes(c)] = a.astype(o_ref.dtype)
            else:
                rows = tm // dil
                stage_ref[c] = a
                for r in range(dil):
                    o_ref[r * rows:(r + 1) * rows, _lanes(c)] = (
                        stage_ref[c, pl.ds(r, rows, stride=dil), :].astype(o_ref.dtype))

    distinct = sorted(set(tile_patterns), key=tile_patterns.index)
    if len(distinct) == 1:
        tile(distinct[0])
    else:
        for pattern in distinct:
            tiles = [t for t, p in enumerate(tile_patterns) if p == pattern]
            cond = functools.reduce(jnp.logical_or, [j == t for t in tiles])
            pl.when(cond)(functools.partial(tile, pattern))


def _norm_proj(x, gain, w, rope, chunk_modes, *, seq, tm, tn, out_dtype):
    t, k = x.shape
    n = w.shape[1]
    per_tile = tn // LANES
    assert t % tm == 0 and n % tn == 0 and seq % tm == 0 and len(chunk_modes) * LANES == n
    tile_patterns = tuple(tuple(chunk_modes[a * per_tile:(a + 1) * per_tile]) for a in range(n // tn))
    seq_tiles = seq // tm
    tab = pl.BlockSpec((tm, LANES), lambda i, j: (i % seq_tiles, 0))
    return pl.pallas_call(
        functools.partial(_norm_proj_body, tile_patterns=tile_patterns),
        grid=(t // tm, n // tn),
        in_specs=[pl.BlockSpec((tm, k), lambda i, j: (i, 0)),
                  pl.BlockSpec((1, k), lambda i, j: (0, 0)),
                  pl.BlockSpec((k, tn), lambda i, j: (0, j)),
                  tab, tab, tab, tab],
        out_specs=pl.BlockSpec((tm, tn), lambda i, j: (i, j)),
        out_shape=jax.ShapeDtypeStruct((t, n), out_dtype),
        scratch_shapes=[pltpu.VMEM((tm, k), BF16), pltpu.VMEM((per_tile, tm, LANES), F32)],
        compiler_params=_params("parallel", "arbitrary"),
        name="norm_proj",
    )(x, gain.reshape(1, k), w, *rope)


def _out_proj_body(a1_ref, a2_ref, w1_ref, w2_ref, res_ref, o_ref):
    o_ref[...] = (res_ref[...] + jnp.dot(a1_ref[...], w1_ref[...], preferred_element_type=F32)
                  + jnp.dot(a2_ref[...], w2_ref[...], preferred_element_type=F32))


def _out_proj(a1, a2, w, res, *, tm, tn):
    t, k1 = a1.shape
    k2 = a2.shape[1]
    n = w.shape[1]
    assert t % tm == 0 and n % tn == 0 and w.shape[0] == k1 + k2 and k1 % k2 == 0
    return pl.pallas_call(
        _out_proj_body,
        grid=(t // tm, n // tn),
        in_specs=[pl.BlockSpec((tm, k1), lambda i, j: (i, 0)),
                  pl.BlockSpec((tm, k2), lambda i, j: (i, 0)),
                  pl.BlockSpec((k1, tn), lambda i, j: (0, j)),
                  pl.BlockSpec((k2, tn), lambda i, j: (k1 // k2, j)),
                  pl.BlockSpec((tm, tn), lambda i, j: (i, j))],
        out_specs=pl.BlockSpec((tm, tn), lambda i, j: (i, j)),
        out_shape=jax.ShapeDtypeStruct((t, n), F32),
        compiler_params=_params("parallel", "parallel"),
        name="out_proj",
    )(a1, a2, w, w, res)


def _mla_proj_body(x_ref, g_ref, wl_ref, qn_ref, wq_ref, kvn_ref, wkv_ref, c64_ref, s64_ref, q_ref, kv_ref, kr_ref,
                   *, q_lora, kv_lora, q_scale):
    half = x_ref.shape[0] // 2
    lat = q_lora + kv_lora
    for rows in (slice(0, half), slice(half, 2 * half)):
        xn = _rms(x_ref[rows, :], g_ref[...]).astype(BF16)
        latent = jnp.dot(xn, wl_ref[...], preferred_element_type=F32)
        k_rope = latent[:, lat:]
        kr_ref[rows, :] = (k_rope * c64_ref[rows, :] + pltpu.roll(k_rope, 64, 1) * s64_ref[rows, :]).astype(kr_ref.dtype)
        c_q = _rms(latent[:, :q_lora], qn_ref[...]).astype(BF16)
        c_kv = _rms(latent[:, q_lora:lat], kvn_ref[...]).astype(BF16)
        kv_ref[rows, :] = jnp.dot(c_kv, wkv_ref[...], preferred_element_type=F32).astype(kv_ref.dtype)
        q = jnp.dot(c_q, wq_ref[...], preferred_element_type=F32) * q_scale
        for c in range(q_ref.shape[1] // LANES):
            a = q[:, _lanes(c)]
            if c % 2 == 1:
                a = a * c64_ref[rows, :] + pltpu.roll(a, 64, 1) * s64_ref[rows, :]
            q_ref[rows, _lanes(c)] = a.astype(q_ref.dtype)


def _mla_proj(x, gain, w_latent, q_norm, w_q, kv_norm, w_kv, rope, *, seq, tm, q_scale):
    t, d = x.shape
    q_lora, kv_lora = w_q.shape[0], w_kv.shape[0]
    assert t % tm == 0 and seq % tm == 0 and w_latent.shape[1] == q_lora + kv_lora + LANES
    seq_tiles = seq // tm
    const = lambda i: (0, 0)
    tab = pl.BlockSpec((tm, LANES), lambda i: (i % seq_tiles, 0))
    full = lambda a: pl.BlockSpec(a.shape, const)
    out_spec = lambda width: pl.BlockSpec((tm, width), lambda i: (i, 0))
    widths = (w_q.shape[1], w_kv.shape[1], LANES)
    return pl.pallas_call(
        functools.partial(_mla_proj_body, q_lora=q_lora, kv_lora=kv_lora, q_scale=q_scale),
        grid=(t // tm,),
        in_specs=[pl.BlockSpec((tm, d), lambda i: (i, 0)), pl.BlockSpec((1, d), const), full(w_latent),
                  pl.BlockSpec((1, q_lora), const), full(w_q), pl.BlockSpec((1, kv_lora), const), full(w_kv),
                  tab, tab],
        out_specs=[out_spec(w) for w in widths],
        out_shape=[jax.ShapeDtypeStruct((t, w), BF16) for w in widths],
        compiler_params=_params("parallel"),
        name="mla_proj",
    )(x, gain.reshape(1, d), w_latent, q_norm.reshape(1, q_lora), w_q, kv_norm.reshape(1, kv_lora), w_kv,
      rope[2], rope[3])


def _ffn_body(x_ref, g_ref, wg_ref, wu_ref, wd_ref, o_ref, xn_ref):
    j = pl.program_id(1)

    def partial_out(xn):
        seg = wg_ref.shape[1] // 2
        out = None
        for cols in (slice(0, seg), slice(seg, 2 * seg)):
            g = jnp.dot(xn, wg_ref[:, cols], preferred_element_type=F32)
            u = jnp.dot(xn, wu_ref[:, cols], preferred_element_type=F32)
            h = (g * jax.nn.sigmoid(g) * u).astype(BF16)
            part = jnp.dot(h, wd_ref[cols, :], preferred_element_type=F32)
            out = part if out is None else out + part
        return out

    @pl.when(j == 0)
    def _():
        half = x_ref.shape[0] // 2
        for rows in (slice(0, half), slice(half, 2 * half)):
            xn = _rms(x_ref[rows, :], g_ref[...]).astype(BF16)
            xn_ref[rows, :] = xn
            o_ref[rows, :] = partial_out(xn)

    @pl.when(j > 0)
    def _():
        o_ref[...] += partial_out(xn_ref[...])

    @pl.when(j == pl.num_programs(1) - 1)
    def _():
        o_ref[...] = x_ref[...] + 0.5 * o_ref[...]


def _ffn(x, gain, w_gu, w_down, layer, half, *, tm, tf):
    t, d = x.shape
    d_ff = w_down.shape[2]
    assert t % tm == 0 and d_ff % tf == 0
    nf = d_ff // tf
    return pl.pallas_call(
        _ffn_body,
        grid=(t // tm, nf),
        in_specs=[pl.BlockSpec((tm, d), lambda i, j: (i, 0)),
                  pl.BlockSpec((1, d), lambda i, j: (0, 0)),
                  pl.BlockSpec((None, None, d, tf), lambda i, j: (layer, half, 0, j)),
                  pl.BlockSpec((None, None, d, tf), lambda i, j: (layer, half, 0, nf + j)),
                  pl.BlockSpec((None, None, tf, d), lambda i, j: (layer, half, j, 0))],
        out_specs=pl.BlockSpec((tm, d), lambda i, j: (i, 0)),
        out_shape=jax.ShapeDtypeStruct((t, d), F32),
        scratch_shapes=[pltpu.VMEM((tm, d), BF16)],
        compiler_params=_params("parallel", "arbitrary"),
        name="ffn",
    )(x, gain.reshape(1, d), w_gu, w_gu, w_down)


def _ple_body(x_ref, p_ref, g_ref, wg_ref, wp_ref, fg_ref, o_ref, *, final_norm):
    half = x_ref.shape[0] // 2
    for rows in (slice(0, half), slice(half, 2 * half)):
        x = x_ref[rows, :]
        xn = _rms(x, g_ref[...]).astype(BF16)
        gate = jax.nn.sigmoid(jnp.dot(xn, wg_ref[...], preferred_element_type=F32))
        proj = jnp.dot(p_ref[rows, :].astype(BF16), wp_ref[...], preferred_element_type=F32)
        y = x + gate * proj
        if final_norm:
            y = _rms(y, fg_ref[...])
        o_ref[rows, :] = y


def _ple(x, p, layer, gain, w_gate, w_proj, final_gain, *, tm, final_norm):
    t, d = x.shape
    pd = p.shape[1]
    assert t % tm == 0
    first_tile = layer * (t // tm)
    const = lambda i: (0, 0)
    return pl.pallas_call(
        functools.partial(_ple_body, final_norm=final_norm),
        grid=(t // tm,),
        in_specs=[pl.BlockSpec((tm, d), lambda i: (i, 0)),
                  pl.BlockSpec((tm, pd), lambda i: (first_tile + i, 0)),
                  pl.BlockSpec((1, d), const),
                  pl.BlockSpec((d, d), const),
                  pl.BlockSpec((pd, d), const),
                  pl.BlockSpec((1, d), const)],
        out_specs=pl.BlockSpec((tm, d), lambda i: (i, 0)),
        out_shape=jax.ShapeDtypeStruct((t, d), F32),
        compiler_params=_params("parallel"),
        name="ple",
    )(x, p, gain.reshape(1, d), w_gate, w_proj, final_gain.reshape(1, d))


def _dot_nt(a, b):
    return lax.dot_general(a, b, (((1,), (1,)), ((), ())), preferred_element_type=F32)


def _lanes(c, width=LANES):
    return slice(c * width, (c + 1) * width)


def _fill_transposed(vt_ref, v_ref, lane_offsets, chunk):
    dv = vt_ref.shape[1] - SUM_ROWS
    ones_row = jnp.where(lax.broadcasted_iota(jnp.int32, (SUM_ROWS, chunk), 0) == 0, 1.0, 0.0).astype(BF16)

    def body(j, _):
        start = pl.multiple_of(j * chunk, chunk)
        for c, off in enumerate(lane_offsets):
            vt_ref[c, :dv, pl.ds(start, chunk)] = v_ref[pl.ds(start, chunk), off:off + dv].T
            vt_ref[c, dv:, pl.ds(start, chunk)] = ones_row
        return 0

    lax.fori_loop(0, v_ref.shape[0] // chunk, body, 0)


def _flash_scratch(heads, tk, cols, dv, seq):
    return [pltpu.VMEM((heads, dv + SUM_ROWS, seq), BF16),
            pltpu.VMEM((heads, tk, cols), F32), pltpu.VMEM((heads, tk, cols), BF16),
            pltpu.VMEM((heads, dv + SUM_ROWS, cols), F32)]


def _causal_diag_masks(tk, tq, cols):
    key_row = lax.broadcasted_iota(jnp.int32, (tk, 1), 0)
    query_col = lax.broadcasted_iota(jnp.int32, (1, cols), 1) % tq
    return [key_row + d * tk <= query_col for d in range(tq // tk)]


def _causal_col_ranges(tk, tq, maps):
    return lambda d: [(g * tq + d * tk, (g + 1) * tq) for g in range(maps)]


def _causal_flash_t(score_fns, vt_ref, s_ref, p_ref, acc_ref, n_past, diag_mask_fn, diag_cols_fn, n_diag,
                    past_mask_fns=None):
    heads = range(len(score_fns))
    _, tk, cols = s_ref.shape

    everything = [(0, cols)]

    def stage_scores(block, ranges=everything):
        start = pl.multiple_of(block * tk, tk)
        for c in heads:
            for lo, hi in ranges:
                s = score_fns[c](start, slice(lo, hi))
                if past_mask_fns is not None:
                    s = jnp.where(past_mask_fns[c](block)[:, lo:hi], s, -jnp.inf)
                s_ref[c, :, lo:hi] = s

    def accumulate(block, alphas, ranges=everything):
        start = pl.multiple_of(block * tk, tk)
        for c in heads:
            for lo, hi in ranges:
                pv = jnp.dot(vt_ref[c, :, pl.ds(start, tk)], p_ref[c, :, lo:hi], preferred_element_type=F32)
                acc_ref[c, :, lo:hi] = alphas[c][:, lo:hi] * acc_ref[c, :, lo:hi] + pv

    def softmax(stats, diag=None, ranges=everything):
        new_stats, alphas = [], []
        for c in heads:
            m = stats[c]
            if diag is None:
                read = lambda lo, hi: s_ref[c, :, lo:hi]
            else:
                visible = diag_mask_fn(diag, c)
                read = lambda lo, hi: jnp.where(visible[:, lo:hi], s_ref[c, :, lo:hi], -jnp.inf)
            pieces, at = [], 0
            for lo, hi in ranges:
                if lo > at:
                    pieces.append(m[:, at:lo])
                pieces.append(jnp.maximum(m[:, lo:hi], jnp.max(read(lo, hi), axis=0, keepdims=True)))
                at = hi
            if at < cols:
                pieces.append(m[:, at:])
            m_new = pieces[0] if len(pieces) == 1 else jnp.concatenate(pieces, axis=1)
            alpha = jnp.exp2(m - m_new)
            for lo, hi in ranges:
                p_ref[c, :, lo:hi] = jnp.exp2(read(lo, hi) - m_new[:, lo:hi]).astype(BF16)
            new_stats.append(m_new)
            alphas.append(alpha)
        return tuple(new_stats), tuple(alphas)

    def past_block(j, carry):
        stats, alphas = carry
        accumulate(jnp.maximum(j - 1, 0), alphas)
        stats, alphas = softmax(stats)
        stage_scores(j + 1)
        return stats, alphas

    for c in heads:
        p_ref[c] = jnp.zeros(p_ref.shape[1:], BF16)
        acc_ref[c] = jnp.zeros(acc_ref.shape[1:], F32)
    stage_scores(0)
    init = (tuple(jnp.full((1, cols), jnp.finfo(F32).min, F32) for _ in heads),
            tuple(jnp.ones((1, cols), F32) for _ in heads))
    def past_group(t, carry):
        for u in range(PAST_BLOCKS_PER_ITERATION):
            carry = past_block(t * PAST_BLOCKS_PER_ITERATION + u, carry)
        return carry

    n_grouped = n_past // PAST_BLOCKS_PER_ITERATION
    carry = lax.fori_loop(0, n_grouped, past_group, init)
    stats, alphas = lax.fori_loop(n_grouped * PAST_BLOCKS_PER_ITERATION, n_past, past_block, carry)

    accumulate(jnp.maximum(n_past - 1, 0), alphas)
    for d in range(n_diag):
        stats, alphas = softmax(stats, diag=d, ranges=diag_cols_fn(d))
        if d + 1 < n_diag:
            stage_scores(n_past + d + 1, ranges=diag_cols_fn(d + 1))
        accumulate(n_past + d, alphas, ranges=diag_cols_fn(d))
    dv = acc_ref.shape[1] - SUM_ROWS
    return tuple((acc_ref[c, dv:dv + 1, :], acc_ref[c, :dv, :]) for c in heads)


def _diff_attn_body(lam_ref, subln_ref, q_ref, k_ref, v_ref, o_ref, qt_ref, vt_ref, s_ref, p_ref, acc_ref, *,
                    tq, tk, heads, lam_init):
    qi = pl.program_id(2)

    @pl.when(qi == 0)
    def _():
        _fill_transposed(vt_ref, v_ref, [c * LANES for c in range(heads)], tk)

    is_map0 = (lax.broadcasted_iota(jnp.int32, (LANES, 1), 0) % 64) < 32

    def make_score_fn(c):
        q_t = q_ref[:, _lanes(c)].T
        zero = jnp.zeros_like(q_t)
        qt_ref[c] = jnp.concatenate([jnp.where(is_map0, q_t, zero), jnp.where(is_map0, zero, q_t)], axis=1)
        return lambda start, cols: jnp.dot(k_ref[pl.ds(start, tk), _lanes(c)], qt_ref[c, :, cols],
                                           preferred_element_type=F32)

    diag = _causal_diag_masks(tk, tq, 2 * tq)
    results = _causal_flash_t([make_score_fn(c) for c in range(heads)], vt_ref, s_ref, p_ref, acc_ref,
                              qi * (tq // tk), lambda d, c: diag[d], _causal_col_ranges(tk, tq, 2), tq // tk)

    lp = lam_ref[...]
    lam = (jnp.exp(jnp.sum(lp[0:1] * lp[1:2], axis=-1, keepdims=True))
           - jnp.exp(jnp.sum(lp[2:3] * lp[3:4], axis=-1, keepdims=True)) + lam_init)
    for c, (l, acc) in enumerate(results):
        o_t = acc / l
        o = (o_t[:, :tq] - lam * o_t[:, tq:]).T
        o_ref[:, _lanes(c)] = (_rms(o, subln_ref[...]) * (1.0 - lam_init)).astype(o_ref.dtype)


def _diff_attention(qkv, lam_params, subln, *, batch, seq, tq, tk, heads, q_col, k_col, v_col, lam_init):
    nq = seq // tq
    width = heads * LANES
    assert A_HEADS % heads == 0 and q_col % heads == 0 and k_col % heads == 0 and v_col % heads == 0
    assert seq % tq == 0 and tq % tk == 0
    return pl.pallas_call(
        functools.partial(_diff_attn_body, tq=tq, tk=tk, heads=heads, lam_init=lam_init),
        grid=(batch, A_HEADS // heads, nq),
        in_specs=[pl.BlockSpec((4, A_QK_DIM), lambda b, h, i: (0, 0)),
                  pl.BlockSpec((1, LANES), lambda b, h, i: (0, 0)),
                  pl.BlockSpec((tq, width), lambda b, h, i: (b * nq + i, q_col // heads + h)),
                  pl.BlockSpec((seq, width), lambda b, h, i: (b, k_col // heads + h)),
                  pl.BlockSpec((seq, width), lambda b, h, i: (b, v_col // heads + h))],
        out_specs=pl.BlockSpec((tq, width), lambda b, h, i: (b * nq + i, h)),
        out_shape=jax.ShapeDtypeStruct((batch * seq, A_HEADS * LANES), BF16),
        scratch_shapes=[pltpu.VMEM((heads, LANES, 2 * tq), BF16)] + _flash_scratch(heads, tk, 2 * tq, LANES, seq),
        compiler_params=_params("parallel", "parallel", "arbitrary"),
        name="diff_attn",
    )(lam_params, subln.reshape(1, LANES), qkv, qkv, qkv)


def _tile_index(ref, r, i, tile):
    per_slab = ref.shape[2]
    if per_slab >= tile:
        runs = per_slab // tile
        return i // runs, r, pl.ds(pl.multiple_of((i % runs) * tile, tile), tile)
    slabs = tile // per_slab
    return pl.ds(i * slabs, slabs), r, slice(None)


def _load_tile(ref, r, i, c, tile):
    slab, res, rows = _tile_index(ref, r, i, tile)
    return ref[slab, res, rows, _lanes(c)].reshape(tile, LANES)


def _store_tile(ref, r, i, c, tile, value):
    slab, res, rows = _tile_index(ref, r, i, tile)
    per_slab = ref.shape[2]
    ref[slab, res, rows, _lanes(c)] = (value if per_slab >= tile
                                       else value.reshape(tile // per_slab, per_slab, LANES))


def _dilated_body(q_ref, k_ref, v_ref, o_ref, lse_ref, *, tile, heads):
    slabs, residues, per_slab, _ = q_ref.shape
    row = lax.broadcasted_iota(jnp.int32, (tile, 1), 0)
    streams = [(r, c) for r in range(residues) for c in range(heads)]

    def attend(r, i, c, k, v, valid):
        s = jnp.where(valid, _dot_nt(_load_tile(q_ref, r, i, c, tile), k), -jnp.inf)
        m = jnp.max(s, axis=-1, keepdims=True)
        p = jnp.exp(s - m)
        l = jnp.sum(p, axis=-1, keepdims=True)
        o = jnp.dot(p.astype(BF16), v, preferred_element_type=F32) / l
        _store_tile(o_ref, r, i, c, tile, o)
        _store_tile(lse_ref, r, i, c, tile, jnp.broadcast_to(m + jnp.log(l), (tile, LANES)))

    col = lax.broadcasted_iota(jnp.int32, (1, tile), 1)
    for r, c in streams:
        attend(r, 0, c, _load_tile(k_ref, r, 0, c, tile), _load_tile(v_ref, r, 0, c, tile), col <= row)

    col2 = lax.broadcasted_iota(jnp.int32, (1, 2 * tile), 1)
    band = jnp.logical_and(col2 >= row, col2 <= row + tile)

    def body(i, _):
        for r, c in streams:
            k = jnp.concatenate([_load_tile(k_ref, r, i - 1, c, tile), _load_tile(k_ref, r, i, c, tile)], axis=0)
            v = jnp.concatenate([_load_tile(v_ref, r, i - 1, c, tile), _load_tile(v_ref, r, i, c, tile)], axis=0)
            attend(r, i, c, k, v, band)
        return 0

    n_tiles = slabs * per_slab // tile
    lax.fori_loop(1, n_tiles, body, 0, unroll=max(1, min(4, n_tiles - 1)))


def _dilated_group(qkv, *, batch, seq, row_tile, group, dil, heads, residues, q_col, k_col, v_col):
    hpg = B_HEADS_PER_GROUP
    tile = B_GROUPS[group][0] // dil
    slabs, per_slab = seq // row_tile, row_tile // dil
    assert tile == LANES and hpg % heads == 0 and (seq // dil) % tile == 0 and dil % residues == 0
    assert per_slab % tile == 0 or tile % per_slab == 0
    width = heads * LANES
    view = qkv.reshape(batch, slabs, dil, per_slab, qkv.shape[1])

    def in_spec(col):
        first = col + group * hpg
        assert first % heads == 0
        return pl.BlockSpec((None, slabs, residues, per_slab, width),
                            lambda b, h, r: (b, 0, r, 0, first // heads + h))

    out_spec = pl.BlockSpec((None, slabs, residues, per_slab, width), lambda b, h, r: (b, 0, r, 0, h))
    out_sds = jax.ShapeDtypeStruct((batch, slabs, dil, per_slab, hpg * LANES), F32)
    o, lse = pl.pallas_call(
        functools.partial(_dilated_body, tile=tile, heads=heads),
        grid=(batch, hpg // heads, dil // residues),
        in_specs=[in_spec(q_col), in_spec(k_col), in_spec(v_col)],
        out_specs=[out_spec, out_spec],
        out_shape=[out_sds, out_sds],
        compiler_params=_params("parallel", "parallel", "parallel"),
        name=f"dilated_g{group}",
    )(view, view, view)
    shape = (batch * slabs, dil, per_slab, hpg * LANES)
    return o.reshape(shape), lse.reshape(shape)


def _merge_body(o0_ref, o1_ref, o2_ref, l0_ref, l1_ref, l2_ref, out_ref, nat_ref):
    def natural(ref, slot, c):
        dil, rows = ref.shape[0], ref.shape[1]
        if dil == 1:
            return ref[0, :, _lanes(c)]
        for r in range(dil):
            nat_ref[slot, pl.ds(r, rows, stride=dil), :] = ref[r, :, _lanes(c)]
        return nat_ref[slot]

    for c in range(out_ref.shape[1] // LANES):
        o0, o1, o2 = natural(o0_ref, None, c), natural(o1_ref, 0, c), natural(o2_ref, 1, c)
        l0, l1, l2 = natural(l0_ref, None, c), natural(l1_ref, 2, c), natural(l2_ref, 3, c)
        m = jnp.maximum(jnp.maximum(l0, l1), l2)
        w0, w1, w2 = jnp.exp(l0 - m), jnp.exp(l1 - m), jnp.exp(l2 - m)
        out_ref[:, _lanes(c)] = ((w0 * o0 + w1 * o1 + w2 * o2) / (w0 + w1 + w2)).astype(out_ref.dtype)


def _merge_groups(outs, lses):
    tiles, _, _, width = outs[0].shape
    row_tile = outs[0].shape[1] * outs[0].shape[2]
    specs = [pl.BlockSpec((None,) + a.shape[1:], lambda i: (i, 0, 0, 0)) for a in (*outs, *lses)]
    return pl.pallas_call(
        _merge_body,
        grid=(tiles,),
        in_specs=specs,
        out_specs=pl.BlockSpec((row_tile, width), lambda i: (i, 0)),
        out_shape=jax.ShapeDtypeStruct((tiles * row_tile, width), BF16),
        scratch_shapes=[pltpu.VMEM((4, row_tile, LANES), F32)],
        compiler_params=_params("parallel"),
        name="dilated_merge",
    )(*outs, *lses)


def _mla_body(q_ref, kv_ref, kr_ref, o_ref, qt_ref, vt_ref, s_ref, p_ref, acc_ref, *, tq, tk, heads):
    qi = pl.program_id(2)

    @pl.when(qi == 0)
    def _():
        _fill_transposed(vt_ref, kv_ref, [(2 * c + 1) * LANES for c in range(heads)], tk)

    def make_score_fn(c):
        qt_ref[c] = q_ref[:, _lanes(c, 2 * LANES)].T

        def score(start, cols):
            k = jnp.concatenate([kv_ref[pl.ds(start, tk), _lanes(2 * c)], kr_ref[pl.ds(start, tk), :]], axis=1)
            return jnp.dot(k, qt_ref[c, :, cols], preferred_element_type=F32)

        return score

    diag = _causal_diag_masks(tk, tq, tq)
    results = _causal_flash_t([make_score_fn(c) for c in range(heads)], vt_ref, s_ref, p_ref, acc_ref,
                              qi * (tq // tk), lambda d, c: diag[d], _causal_col_ranges(tk, tq, 1), tq // tk)
    for c, (l, acc) in enumerate(results):
        o_ref[:, _lanes(c)] = (acc / l).T.astype(o_ref.dtype)


def _mla_attention(q, kv, k_rope_src, *, batch, seq, tq, tk, heads, k_rope_col):
    nq = seq // tq
    assert C_HEADS % heads == 0 and seq % tq == 0 and tq % tk == 0
    return pl.pallas_call(
        functools.partial(_mla_body, tq=tq, tk=tk, heads=heads),
        grid=(batch, C_HEADS // heads, nq),
        in_specs=[pl.BlockSpec((tq, heads * 2 * LANES), lambda b, h, i: (b * nq + i, h)),
                  pl.BlockSpec((seq, heads * 2 * LANES), lambda b, h, i: (b, h)),
                  pl.BlockSpec((seq, LANES), lambda b, h, i: (b, k_rope_col))],
        out_specs=pl.BlockSpec((tq, heads * LANES), lambda b, h, i: (b * nq + i, h)),
        out_shape=jax.ShapeDtypeStruct((batch * seq, C_HEADS * LANES), BF16),
        scratch_shapes=[pltpu.VMEM((heads, 2 * LANES, tq), BF16)] + _flash_scratch(heads, tk, tq, LANES, seq),
        compiler_params=_params("parallel", "parallel", "arbitrary"),
        name="mla_attn",
    )(q, kv, k_rope_src)


def _moba_body(q_ref, k_ref, v_ref, o_ref, kmean_ref, sel_ref, qt_ref, vt_ref, s_ref, p_ref, acc_ref, *,
               n_blk, tq, heads):
    qi = pl.program_id(2)
    blk = MOBA_BLOCK
    n_own = tq // blk

    @pl.when(qi == 0)
    def _():
        for c in range(heads):
            k_all = k_ref[:, _lanes(c)].astype(F32).reshape(n_blk, blk, LANES)
            kmean_ref[c] = jnp.mean(k_all, axis=1)
        _fill_transposed(vt_ref, v_ref, [c * LANES for c in range(heads)], blk)

    blk_id = lax.broadcasted_iota(jnp.int32, (n_blk, 1), 0)
    query_col = lax.broadcasted_iota(jnp.int32, (1, tq), 1)
    own_in_tile = query_col // blk
    past = blk_id < qi * n_own + own_in_tile
    causal = lax.broadcasted_iota(jnp.int32, (blk, 1), 0) <= query_col % blk

    def make_score_fn(c):
        q_t = q_ref[:, _lanes(c)].T
        km = kmean_ref[c]
        km_hi = km.astype(BF16)
        rem = km - km_hi.astype(F32)
        km_mid = rem.astype(BF16)
        km_lo = (rem - km_mid.astype(F32)).astype(BF16)
        gate = (jnp.dot(km_hi, q_t, preferred_element_type=F32) + jnp.dot(km_mid, q_t, preferred_element_type=F32)
                + jnp.dot(km_lo, q_t, preferred_element_type=F32))
        gate = jnp.where(past, gate, -jnp.inf)
        selected = jnp.zeros(gate.shape, F32)
        for _ in range(min(MOBA_TOPK, n_blk - 1)):
            best = jnp.max(gate, axis=0, keepdims=True)
            first = jnp.min(jnp.where(gate == best, blk_id, n_blk), axis=0, keepdims=True)
            pick = blk_id == first
            selected = jnp.where(jnp.logical_and(pick, past), 1.0, selected)
            gate = jnp.where(pick, -jnp.inf, gate)
        sel_ref[c] = selected
        qt_ref[c] = q_t
        return lambda start, cols: jnp.dot(k_ref[pl.ds(start, blk), _lanes(c)], qt_ref[c, :, cols],
                                           preferred_element_type=F32)

    score_fns = [make_score_fn(c) for c in range(heads)]
    n_past = qi * n_own
    chosen_fns = [lambda n, c=c: jnp.logical_or(sel_ref[c, pl.ds(n, 1), :] > 0.0, n >= n_past)
                  for c in range(heads)]

    def own_block_mask(d, c):
        chosen = sel_ref[c, pl.ds(n_past + d, 1), :] > 0.0
        return jnp.logical_or(jnp.logical_and(own_in_tile == d, causal),
                              jnp.logical_and(own_in_tile > d, chosen))

    results = _causal_flash_t(score_fns, vt_ref, s_ref, p_ref, acc_ref, n_past, own_block_mask,
                              _causal_col_ranges(blk, tq, 1), n_own,
                              past_mask_fns=chosen_fns)
    for c, (l, acc) in enumerate(results):
        o_ref[:, _lanes(c)] = (acc / l).T.astype(o_ref.dtype)


def _moba_attention(qkv, *, batch, seq, tq, heads, q_col, k_col, v_col):
    blk = MOBA_BLOCK
    n_blk = seq // blk
    nq = seq // tq
    width = heads * LANES
    assert seq % tq == 0 and tq % blk == 0 and D_HEADS % heads == 0
    assert q_col % heads == 0 and k_col % heads == 0 and v_col % heads == 0
    return pl.pallas_call(
        functools.partial(_moba_body, n_blk=n_blk, tq=tq, heads=heads),
        grid=(batch, D_HEADS // heads, nq),
        in_specs=[pl.BlockSpec((tq, width), lambda b, h, i: (b * nq + i, q_col // heads + h)),
                  pl.BlockSpec((seq, width), lambda b, h, i: (b, k_col // heads + h)),
                  pl.BlockSpec((seq, width), lambda b, h, i: (b, v_col // heads + h))],
        out_specs=pl.BlockSpec((tq, width), lambda b, h, i: (b * nq + i, h)),
        out_shape=jax.ShapeDtypeStruct((batch * seq, D_HEADS * LANES), BF16),
        scratch_shapes=[pltpu.VMEM((heads, n_blk, LANES), F32),
                        pltpu.VMEM((heads, n_blk, tq), F32),
                        pltpu.VMEM((heads, LANES, tq), BF16)] + _flash_scratch(heads, blk, tq, LANES, seq),
        compiler_params=_params("parallel", "parallel", "arbitrary"),
        name="moba_attn",
    )(qkv, qkv, qkv)


def _rope_tables(seq):
    pos = jnp.arange(seq, dtype=F32)[:, None]

    def angles(dim):
        inv = 1.0 / (ROPE_THETA ** (jnp.arange(0, dim, 2, dtype=F32) / dim))
        return pos * inv[None, :]

    a128 = angles(HEAD_DIM)
    a64 = angles(A_QK_DIM)
    c128 = jnp.concatenate([jnp.cos(a128)] * 2, axis=1)
    s128 = jnp.concatenate([-jnp.sin(a128), jnp.sin(a128)], axis=1)
    c64 = jnp.concatenate([jnp.cos(a64)] * 4, axis=1)
    s64 = jnp.concatenate([-jnp.sin(a64)] * 2 + [jnp.sin(a64)] * 2, axis=1)
    return c128, s128, c64, s64


def _interleave_diff_heads(w):
    d = w.shape[0]
    return w.reshape(d, A_HEADS, 2, 2, 32).transpose(0, 1, 3, 2, 4).reshape(d, A_HEADS * LANES)


def _spread_rope64(w):
    z = jnp.zeros(w.shape[:-1] + (32,), w.dtype)
    return jnp.concatenate([w[..., :32], z, w[..., 32:], z], axis=-1)


def _ab_mixer(x, gain, w_in, lam_params, subln, w_out, layer_idx, rope, *, batch, seq):
    aw = A_HEADS * LANES
    w = jnp.concatenate([_interleave_diff_heads(w_in[:, :aw]), _interleave_diff_heads(w_in[:, aw:2 * aw]),
                         w_in[:, 2 * aw:]], axis=1).astype(BF16)
    a_scale = A_QK_DIM ** -0.5 * LOG2_E
    b_scale = HEAD_DIM ** -0.5
    dils = [dil for _, dil in B_GROUPS for _ in range(B_HEADS_PER_GROUP)]
    modes = ([(ROPE64, a_scale, 1)] * A_HEADS + [(ROPE64, 1.0, 1)] * A_HEADS + [(PLAIN, 1.0, 1)] * A_HEADS
             + [(ROPE128, b_scale, d) for d in dils] + [(ROPE128, 1.0, d) for d in dils]
             + [(PLAIN, 1.0, d) for d in dils])
    row_tile = 1024
    qkv = _norm_proj(x, gain, w, rope, modes, seq=seq, tm=row_tile, tn=1280, out_dtype=BF16)
    lam_init = 0.8 - 0.6 * math.exp(-0.3 * layer_idx)
    oa = _diff_attention(qkv, lam_params, subln, batch=batch, seq=seq, tq=2048, tk=256, heads=2,
                         q_col=0, k_col=A_HEADS, v_col=2 * A_HEADS, lam_init=lam_init)
    b0 = 3 * A_HEADS
    outs, lses = [], []
    for g, (_, dil) in enumerate(B_GROUPS):
        o, lse = _dilated_group(qkv, batch=batch, seq=seq, row_tile=row_tile, group=g, dil=dil,
                                heads=2 if dil == 1 else 4, residues=4 if dil == 16 else 1,
                                q_col=b0, k_col=b0 + B_HEADS, v_col=b0 + 2 * B_HEADS)
        outs.append(o)
        lses.append(lse)
    ob = _merge_groups(outs, lses)
    return _out_proj(oa, ob, w_out.astype(BF16), x, tm=1024, tn=1024)


def _cd_mixer(x, gain, w_in, q_norm, w_uq, kv_norm, w_ukv, w_out, rope, *, batch, seq):
    lat = C_Q_LORA + C_KV_LORA
    d_scale = HEAD_DIM ** -0.5 * LOG2_E
    modes = [(ROPE128, d_scale, 1)] * D_HEADS + [(ROPE128, 1.0, 1)] * D_HEADS + [(PLAIN, 1.0, 1)] * D_HEADS
    main = _norm_proj(x, gain, w_in[:, lat + C_ROPE:].astype(BF16), rope, modes, seq=seq, tm=1024, tn=1024,
                      out_dtype=BF16)

    c_scale = (C_NOPE + C_ROPE) ** -0.5 * LOG2_E
    w_q = w_uq.reshape(C_Q_LORA, C_HEADS, C_NOPE + C_ROPE)
    w_q = jnp.concatenate([w_q[..., :C_NOPE], _spread_rope64(w_q[..., C_NOPE:])], axis=-1)
    w_q = w_q.reshape(C_Q_LORA, C_HEADS * 2 * LANES).astype(BF16)
    w_latent = jnp.concatenate([w_in[:, :lat], _spread_rope64(w_in[:, lat:lat + C_ROPE])], axis=1).astype(BF16)
    qc, kv, k_rope = _mla_proj(x, gain, w_latent, q_norm, w_q, kv_norm, w_ukv.astype(BF16), rope,
                               seq=seq, tm=512, q_scale=c_scale)
    oc = _mla_attention(qc, kv, k_rope, batch=batch, seq=seq, tq=2048, tk=256, heads=2, k_rope_col=0)
    od = _moba_attention(main, batch=batch, seq=seq, tq=1024, heads=4,
                         q_col=0, k_col=D_HEADS, v_col=2 * D_HEADS)
    return _out_proj(oc, od, w_out.astype(BF16), x, tm=1024, tn=1024)


def kernel(x, p, ffn_norm, ffn_w_gu, ffn_w_down, mix_norm, ab_w_in, ab_lambda, ab_subln, ab_w_out,
           cd_w_in, cd_q_norm, cd_w_uq, cd_kv_norm, cd_w_ukv, cd_w_out, ple_norm, ple_w_gate,
           ple_w_proj, final_norm):
    batch, seq, d = x.shape
    depth = p.shape[0]
    rope = _rope_tables(seq)
    w_gu, w_down = ffn_w_gu.astype(BF16), ffn_w_down.astype(BF16)
    x = x.reshape(batch * seq, d)
    for i in range(depth):
        j = i // 2
        x = _ffn(x, ffn_norm[i, 0], w_gu, w_down, i, 0, tm=1024, tf=512)
        if i % 2 == 0:
            x = _ab_mixer(x, mix_norm[i], ab_w_in[j], ab_lambda[j], ab_subln[j], ab_w_out[j], i, rope,
                          batch=batch, seq=seq)
        else:
            x = _cd_mixer(x, mix_norm[i], cd_w_in[j], cd_q_norm[j], cd_w_uq[j], cd_kv_norm[j], cd_w_ukv[j],
                          cd_w_out[j], rope, batch=batch, seq=seq)
        x = _ffn(x, ffn_norm[i, 1], w_gu, w_down, i, 1, tm=1024, tf=512)
        x = _ple(x, p.reshape(depth * batch * seq, -1), i, ple_norm[i], ple_w_gate[i].astype(BF16),
                 ple_w_proj[i].astype(BF16), final_norm, tm=512, final_norm=(i == depth - 1))
    return x.reshape(batch, seq, d)
```
